```python
import math
import jax, jax.numpy as jnp
from jax import lax
import numpy as np

D_MODEL = 2048
BATCH = 8
SEQ = 4096
DEPTH = 4

HEAD_DIM = 128
ROPE_THETA = 10000.0
NORM_EPS = 1e-6

A_Q_HEADS = 8
A_KV_HEADS = 2
A_GROUP = A_Q_HEADS // A_KV_HEADS
A_RADIUS = 128
A_BLOCK = 128

B_PATTERNS = ((128, 1), (512, 4), (2048, 16))
B_GROUPS = len(B_PATTERNS)
B_HEADS_PER_GROUP = 4
B_HEADS = B_GROUPS * B_HEADS_PER_GROUP
B_BLOCK = 64

C_HEADS = 8
GRID_W = 64
C_WIN_ROWS = 8
C_WIN_COLS = 16

N_BRANCH = 3
A_Q_W = A_Q_HEADS * HEAD_DIM
A_KV_W = A_KV_HEADS * HEAD_DIM
B_W = B_HEADS * HEAD_DIM
B_OUT_W = B_HEADS_PER_GROUP * HEAD_DIM
C_W = C_HEADS * HEAD_DIM
IN_WIDTHS = (A_Q_W, A_KV_W, A_KV_W, B_W, B_W, B_W, C_W, C_W, C_W, N_BRANCH * D_MODEL)
N_IN = sum(IN_WIDTHS)

D_FF = ((8 * D_MODEL + 3 * 256 - 1) // (3 * 256)) * 256

kernel_name = 'hybrid_gated_local_dilated_grid_attention_encoder'


def rmsnorm(x, g):
    x32 = x.astype(jnp.float32)
    y = x32 * lax.rsqrt(jnp.mean(x32 * x32, axis=-1, keepdims=True) + NORM_EPS)
    return (y * g.astype(jnp.float32)).astype(x.dtype)


def rope_tables(n):
    half = HEAD_DIM // 2
    inv_freq = ROPE_THETA ** (-jnp.arange(half, dtype=jnp.float32) * 2.0 / HEAD_DIM)
    ang = jnp.arange(n, dtype=jnp.float32)[:, None] * inv_freq[None, :]
    return jnp.cos(ang), jnp.sin(ang)


def apply_rope(x, cos, sin):
    half = HEAD_DIM // 2
    x32 = x.astype(jnp.float32)
    x1, x2 = x32[..., :half], x32[..., half:]
    return jnp.concatenate([x1 * cos - x2 * sin, x2 * cos + x1 * sin], axis=-1).astype(x.dtype)


def banded_attention(q, k, v, radius, block, sink=None):
    bsz, hk, grp, n, dh = q.shape
    nb = n // block
    width = block + 2 * radius
    pad = ((0, 0), (0, 0), (radius, radius), (0, 0))
    idx = jnp.arange(nb)[:, None] * block + jnp.arange(width)[None, :]
    kb = jnp.pad(k, pad)[:, :, idx]
    vb = jnp.pad(v, pad)[:, :, idx]
    qb = q.reshape(bsz, hk, grp, nb, block, dh)
    s = jnp.einsum('bhgnqd,bhnkd->bhgnqk', qb, kb, preferred_element_type=jnp.float32) * (dh ** -0.5)
    qpos = (jnp.arange(nb)[:, None] * block + jnp.arange(block)[None, :])[:, :, None]
    kpos = (idx - radius)[:, None, :]
    valid = (jnp.abs(kpos - qpos) <= radius) & (kpos >= 0) & (kpos < n)
    s = jnp.where(valid, s, -jnp.inf)
    m = jnp.max(s, axis=-1, keepdims=True)
    if sink is not None:
        sk = sink.astype(jnp.float32).reshape(1, hk, grp, 1, 1, 1)
        m = jnp.maximum(m, sk)
    p = jnp.exp(s - m)
    denom = jnp.sum(p, axis=-1, keepdims=True)
    if sink is not None:
        denom = denom + jnp.exp(sk - m)
    o = jnp.einsum('bhgnqk,bhnkd->bhgnqd', (p / denom).astype(v.dtype), vb)
    lse = (m + jnp.log(denom))[..., 0]
    return o.reshape(bsz, hk, grp, n, dh), lse.reshape(bsz, hk, grp, n)


def mixer_a(qa, ka, va, cos, sin, gq, gk, sink):
    bsz, n, _ = qa.shape
    q = qa.reshape(bsz, n, A_KV_HEADS, A_GROUP, HEAD_DIM).transpose(0, 2, 3, 1, 4)
    k = ka.reshape(bsz, n, A_KV_HEADS, HEAD_DIM).transpose(0, 2, 1, 3)
    v = va.reshape(bsz, n, A_KV_HEADS, HEAD_DIM).transpose(0, 2, 1, 3)
    q = apply_rope(rmsnorm(q, gq), cos, sin)
    k = apply_rope(rmsnorm(k, gk), cos, sin)
    o, _ = banded_attention(q, k, v, A_RADIUS, math.gcd(n, A_BLOCK),
                            sink.reshape(A_KV_HEADS, A_GROUP))
    return o.transpose(0, 3, 1, 2, 4).reshape(bsz, n, A_Q_W)


def to_residue_classes(t, dil):
    bsz, h, n, dh = t.shape
    return t.reshape(bsz, h, n // dil, dil, dh).transpose(0, 1, 3, 2, 4).reshape(bsz, h * dil, n // dil, dh)


def mixer_b(qb, kb, vb, cos, sin, gq, gk):
    bsz, n, _ = qb.shape
    hg = B_HEADS_PER_GROUP

    def heads(t):
        return t.reshape(bsz, n, B_GROUPS, hg, HEAD_DIM).transpose(0, 2, 3, 1, 4)

    q = apply_rope(rmsnorm(heads(qb), gq), cos, sin)
    k = apply_rope(rmsnorm(heads(kb), gk), cos, sin)
    v = heads(vb)
    outs, lses = [], []
    for g, (window, dil) in enumerate(B_PATTERNS):
        m = n // dil
        o, lse = banded_attention(to_residue_classes(q[:, g], dil)[:, :, None],
                                  to_residue_classes(k[:, g], dil),
                                  to_residue_classes(v[:, g], dil),
                                  window // (2 * dil), math.gcd(m, B_BLOCK))
        o = o[:, :, 0].reshape(bsz, hg, dil, m, HEAD_DIM).transpose(0, 1, 3, 2, 4).reshape(bsz, hg, n, HEAD_DIM)
        lse = lse[:, :, 0].reshape(bsz, hg, dil, m).transpose(0, 1, 3, 2).reshape(bsz, hg, n)
        outs.append(o)
        lses.append(lse)
    o = jnp.stack(outs, axis=1)
    lse = jnp.stack(lses, axis=1)
    w = jax.nn.softmax(lse, axis=1)
    out = jnp.einsum('bghl,bghld->bhld', w.astype(o.dtype), o)
    return out.transpose(0, 2, 1, 3).reshape(bsz, n, B_OUT_W)


def mixer_c(qc, kc, vc, gq, gk, rpb):
    bsz, n, _ = qc.shape
    rows = n // GRID_W
    wr = min(C_WIN_ROWS, rows)

    def grid(t):
        return t.reshape(bsz, rows, GRID_W, C_HEADS, HEAD_DIM).transpose(0, 3, 1, 2, 4)

    q = rmsnorm(grid(qc), gq)
    k = rmsnorm(grid(kc), gk)
    v = grid(vc)
    r = jnp.arange(rows)
    row_start = jnp.clip(r - wr // 2, 0, rows - wr)
    krow = row_start[:, None] + jnp.arange(wr)[None, :]
    kg = k[:, :, krow]
    vg = v[:, :, krow]
    s = jnp.einsum('bhrcd,bhrwkd->bhrcwk', q, kg, preferred_element_type=jnp.float32) * (HEAD_DIM ** -0.5)
    cq = jnp.arange(GRID_W)
    col_start = jnp.clip(cq - C_WIN_COLS // 2, 0, GRID_W - C_WIN_COLS)
    col_ok = (cq[None, :] >= col_start[:, None]) & (cq[None, :] < col_start[:, None] + C_WIN_COLS)
    drow = krow - r[:, None]
    dcol = jnp.clip(cq[None, :] - cq[:, None], -(C_WIN_COLS - 1), C_WIN_COLS - 1)
    bias = rpb[:, drow[:, None, :, None] + (C_WIN_ROWS - 1), dcol[None, :, None, :] + (C_WIN_COLS - 1)]
    s = s + bias[None].astype(jnp.float32)
    s = jnp.where(col_ok[:, None, :], s, -jnp.inf)
    p = jax.nn.softmax(s.reshape(bsz, C_HEADS, rows, GRID_W, wr * GRID_W), axis=-1).reshape(s.shape)
    o = jnp.einsum('bhrcwk,bhrwkd->bhrcd', p.astype(v.dtype), vg)
    return o.transpose(0, 2, 3, 1, 4).reshape(bsz, n, C_W)


def _fwd_setup_inputs(seed: int = 0) -> dict:
    key = jax.random.key(seed)
    ks = jax.random.split(key, 14)
    f32 = jnp.float32

    def w(k, shape, fan_in):
        return jax.random.normal(k, shape, f32) * (fan_in ** -0.5)

    return {
        'x': jax.random.normal(ks[0], (BATCH, SEQ, D_MODEL), f32),
        'norm1_g': 1.0 + 0.02 * jax.random.normal(ks[1], (DEPTH, D_MODEL), f32),
        'w_in': w(ks[2], (DEPTH, D_MODEL, N_IN), D_MODEL),
        'qk_norm_g': 1.0 + 0.02 * jax.random.normal(ks[3], (DEPTH, 6, HEAD_DIM), f32),
        'sink_a': jax.random.normal(ks[4], (DEPTH, A_Q_HEADS), f32),
        'rpb_c': 0.1 * jax.random.normal(ks[5], (DEPTH, C_HEADS, 2 * C_WIN_ROWS - 1, 2 * C_WIN_COLS - 1), f32),
        'w_br_a': w(ks[6], (DEPTH, A_Q_W, D_MODEL), A_Q_W),
        'w_br_b': w(ks[7], (DEPTH, B_OUT_W, D_MODEL), B_OUT_W),
        'w_br_c': w(ks[8], (DEPTH, C_W, D_MODEL), C_W),
        'w_o': w(ks[9], (DEPTH, D_MODEL, D_MODEL), D_MODEL),
        'norm2_g': 1.0 + 0.02 * jax.random.normal(ks[10], (DEPTH, D_MODEL), f32),
        'w_gate_up': w(ks[11], (DEPTH, D_MODEL, 2 * D_FF), D_MODEL),
        'w_down': w(ks[12], (DEPTH, D_FF, D_MODEL), D_FF),
    }


def _fwd_reference(x, norm1_g, w_in, qk_norm_g, sink_a, rpb_c, w_br_a, w_br_b, w_br_c, w_o,
              norm2_g, w_gate_up, w_down):
    bsz, n, _ = x.shape
    cos, sin = rope_tables(n)
    split_points = []
    acc = 0
    for wdt in IN_WIDTHS[:-1]:
        acc += wdt
        split_points.append(acc)
    for i in range(DEPTH):
        h = rmsnorm(x, norm1_g[i])
        proj = h @ w_in[i]
        qa, ka, va, qb, kb, vb, qc, kc, vc, gl = jnp.split(proj, split_points, axis=-1)
        g = qk_norm_g[i]
        oa = mixer_a(qa, ka, va, cos, sin, g[0], g[1], sink_a[i])
        ob = mixer_b(qb, kb, vb, cos, sin, g[2], g[3])
        oc = mixer_c(qc, kc, vc, g[4], g[5], rpb_c[i])
        gates = jax.nn.sigmoid(gl.astype(jnp.float32)).astype(x.dtype).reshape(bsz, n, N_BRANCH, D_MODEL)
        merged = (gates[:, :, 0] * (oa @ w_br_a[i])
                  + gates[:, :, 1] * (ob @ w_br_b[i])
                  + gates[:, :, 2] * (oc @ w_br_c[i]))
        x = x + merged @ w_o[i]
        h2 = rmsnorm(x, norm2_g[i])
        gt, up = jnp.split(h2 @ w_gate_up[i], 2, axis=-1)
        x = x + (jax.nn.silu(gt) * up) @ w_down[i]
    return x


import jax as _jax
import jax.numpy as _jnp

TWIN_FORMAT = 'train_step'
FWD_PARAMS = ['x', 'norm1_g', 'w_in', 'qk_norm_g', 'sink_a', 'rpb_c', 'w_br_a', 'w_br_b', 'w_br_c', 'w_o', 'norm2_g', 'w_gate_up', 'w_down']
TWIN_WEIGHTS = ['norm1_g', 'w_in', 'qk_norm_g', 'sink_a', 'rpb_c', 'w_br_a', 'w_br_b', 'w_br_c', 'w_o', 'norm2_g', 'w_gate_up', 'w_down']
TWIN_DIFF_INPUT = 'x'
TWIN_INPUTS = ['x', 'norm1_g', 'w_in', 'qk_norm_g', 'sink_a', 'rpb_c', 'w_br_a', 'w_br_b', 'w_br_c', 'w_o', 'norm2_g', 'w_gate_up', 'w_down', 'loss_target', 'm_norm1_g', 'm_w_in', 'm_qk_norm_g', 'm_sink_a', 'm_rpb_c', 'm_w_br_a', 'm_w_br_b', 'm_w_br_c', 'm_w_o', 'm_norm2_g', 'm_w_gate_up', 'm_w_down', 'v_norm1_g', 'v_w_in', 'v_qk_norm_g', 'v_sink_a', 'v_rpb_c', 'v_w_br_a', 'v_w_br_b', 'v_w_br_c', 'v_w_o', 'v_norm2_g', 'v_w_gate_up', 'v_w_down']
TWIN_OUTPUTS = ['loss', 'grad_x', 'grad_norm1_g', 'grad_w_in', 'grad_qk_norm_g', 'grad_sink_a', 'grad_rpb_c', 'grad_w_br_a', 'grad_w_br_b', 'grad_w_br_c', 'grad_w_o', 'grad_norm2_g', 'grad_w_gate_up', 'grad_w_down', 'delta_norm1_g', 'delta_w_in', 'delta_qk_norm_g', 'delta_sink_a', 'delta_rpb_c', 'delta_w_br_a', 'delta_w_br_b', 'delta_w_br_c', 'delta_w_o', 'delta_norm2_g', 'delta_w_gate_up', 'delta_w_down', 'new_m_norm1_g', 'new_m_w_in', 'new_m_qk_norm_g', 'new_m_sink_a', 'new_m_rpb_c', 'new_m_w_br_a', 'new_m_w_br_b', 'new_m_w_br_c', 'new_m_w_o', 'new_m_norm2_g', 'new_m_w_gate_up', 'new_m_w_down', 'new_v_norm1_g', 'new_v_w_in', 'new_v_qk_norm_g', 'new_v_sink_a', 'new_v_rpb_c', 'new_v_w_br_a', 'new_v_w_br_b', 'new_v_w_br_c', 'new_v_w_o', 'new_v_norm2_g', 'new_v_w_gate_up', 'new_v_w_down']
TWIN_LEAF_KINDS = {'loss': 'loss', 'grad_x': 'grad_x', 'grad_norm1_g': 'grad_w', 'grad_w_in': 'grad_w', 'grad_qk_norm_g': 'grad_w', 'grad_sink_a': 'grad_w', 'grad_rpb_c': 'grad_w', 'grad_w_br_a': 'grad_w', 'grad_w_br_b': 'grad_w', 'grad_w_br_c': 'grad_w', 'grad_w_o': 'grad_w', 'grad_norm2_g': 'grad_w', 'grad_w_gate_up': 'grad_w', 'grad_w_down': 'grad_w', 'delta_norm1_g': 'delta_w', 'delta_w_in': 'delta_w', 'delta_qk_norm_g': 'delta_w', 'delta_sink_a': 'delta_w', 'delta_rpb_c': 'delta_w', 'delta_w_br_a': 'delta_w', 'delta_w_br_b': 'delta_w', 'delta_w_br_c': 'delta_w', 'delta_w_o': 'delta_w', 'delta_norm2_g': 'delta_w', 'delta_w_gate_up': 'delta_w', 'delta_w_down': 'delta_w', 'new_m_norm1_g': 'new_m', 'new_m_w_in': 'new_m', 'new_m_qk_norm_g': 'new_m', 'new_m_sink_a': 'new_m', 'new_m_rpb_c': 'new_m', 'new_m_w_br_a': 'new_m', 'new_m_w_br_b': 'new_m', 'new_m_w_br_c': 'new_m', 'new_m_w_o': 'new_m', 'new_m_norm2_g': 'new_m', 'new_m_w_gate_up': 'new_m', 'new_m_w_down': 'new_m', 'new_v_norm1_g': 'new_v', 'new_v_w_in': 'new_v', 'new_v_qk_norm_g': 'new_v', 'new_v_sink_a': 'new_v', 'new_v_rpb_c': 'new_v', 'new_v_w_br_a': 'new_v', 'new_v_w_br_b': 'new_v', 'new_v_w_br_c': 'new_v', 'new_v_w_o': 'new_v', 'new_v_norm2_g': 'new_v', 'new_v_w_gate_up': 'new_v', 'new_v_w_down': 'new_v'}


def _forward(args):
    return _fwd_reference(*[args[k] for k in FWD_PARAMS])


def _output_shape():
    def fwd():
        inp = _fwd_setup_inputs(0)
        return _fwd_reference(*[inp[k] for k in FWD_PARAMS])
    out = _jax.eval_shape(fwd)
    return out.shape, out.dtype

N_MICROBATCH = 1
ADAM_LR = 0.001
ADAM_B1 = 0.9
ADAM_B2 = 0.999
ADAM_EPS = 1e-08
ADAM_WD = 0.01
ADAM_STEP = 10
PER_EXAMPLE_BATCH_AXIS = {'x': 0, 'loss_target': 0}
SHARED_INPUTS = []
_WEIGHT_DTYPES = {'norm1_g': _jnp.float32, 'w_in': _jnp.float32, 'qk_norm_g': _jnp.float32, 'sink_a': _jnp.float32, 'rpb_c': _jnp.float32, 'w_br_a': _jnp.float32, 'w_br_b': _jnp.float32, 'w_br_c': _jnp.float32, 'w_o': _jnp.float32, 'norm2_g': _jnp.float32, 'w_gate_up': _jnp.float32, 'w_down': _jnp.float32}
MOMENT_SCALE = {'norm1_g': 1.444593e-01, 'w_in': 3.146424e-02, 'qk_norm_g': 5.777561e-01, 'sink_a': 4.718306e-02, 'rpb_c': 2.376689e-02, 'w_br_a': 3.047329e-02, 'w_br_b': 2.540336e-02, 'w_br_c': 4.237715e-02, 'w_o': 5.735241e-02, 'norm2_g': 1.234309e+01, 'w_gate_up': 1.050484e-01, 'w_down': 1.687471e-01}


def _to_microbatches(a, axis):
    t = _jnp.moveaxis(a, axis, 0)
    t = t.reshape((N_MICROBATCH, t.shape[0] // N_MICROBATCH) + t.shape[1:])
    return _jnp.moveaxis(t, 1, axis + 1)


def setup_inputs(seed: int = 0) -> dict:
    inp = _fwd_setup_inputs(seed)
    key = _jax.random.fold_in(_jax.random.key(seed), 7919)
    shape, _ = _output_shape()
    out = dict(inp)
    out["loss_target"] = _jax.random.normal(_jax.random.fold_in(key, 0), shape, _jnp.float32)
    for i, name in enumerate(TWIN_WEIGHTS):
        w = inp[name].astype(_jnp.float32)
        if MOMENT_SCALE is None:
            s = _jnp.sqrt(_jnp.mean(_jnp.square(w)) + 1e-30)
        else:
            s = MOMENT_SCALE[name]
        km, kv = _jax.random.split(_jax.random.fold_in(key, i + 1))
        out[name] = w
        out["m_" + name] = s * _jax.random.normal(km, w.shape, _jnp.float32)
        out["v_" + name] = (s * s) * _jax.random.uniform(kv, w.shape, _jnp.float32, 0.5, 1.5)
    if N_MICROBATCH > 1:
        for name, axis in PER_EXAMPLE_BATCH_AXIS.items():
            out[name] = _to_microbatches(out[name], axis)
    return {'x': out['x'], 'norm1_g': out['norm1_g'], 'w_in': out['w_in'], 'qk_norm_g': out['qk_norm_g'], 'sink_a': out['sink_a'], 'rpb_c': out['rpb_c'], 'w_br_a': out['w_br_a'], 'w_br_b': out['w_br_b'], 'w_br_c': out['w_br_c'], 'w_o': out['w_o'], 'norm2_g': out['norm2_g'], 'w_gate_up': out['w_gate_up'], 'w_down': out['w_down'], 'loss_target': out['loss_target'], 'm_norm1_g': out['m_norm1_g'], 'm_w_in': out['m_w_in'], 'm_qk_norm_g': out['m_qk_norm_g'], 'm_sink_a': out['m_sink_a'], 'm_rpb_c': out['m_rpb_c'], 'm_w_br_a': out['m_w_br_a'], 'm_w_br_b': out['m_w_br_b'], 'm_w_br_c': out['m_w_br_c'], 'm_w_o': out['m_w_o'], 'm_norm2_g': out['m_norm2_g'], 'm_w_gate_up': out['m_w_gate_up'], 'm_w_down': out['m_w_down'], 'v_norm1_g': out['v_norm1_g'], 'v_w_in': out['v_w_in'], 'v_qk_norm_g': out['v_qk_norm_g'], 'v_sink_a': out['v_sink_a'], 'v_rpb_c': out['v_rpb_c'], 'v_w_br_a': out['v_w_br_a'], 'v_w_br_b': out['v_w_br_b'], 'v_w_br_c': out['v_w_br_c'], 'v_w_o': out['v_w_o'], 'v_norm2_g': out['v_norm2_g'], 'v_w_gate_up': out['v_w_gate_up'], 'v_w_down': out['v_w_down']}


def _loss(weights, diff, rest, loss_target):
    with _jax.named_scope("forward"):
        args = {**rest, TWIN_DIFF_INPUT: diff, **{k: w.astype(_WEIGHT_DTYPES[k]) for k, w in weights.items()}}
        y = _forward(args)
    with _jax.named_scope("loss_head"):
        err = _jnp.square(y.astype(_jnp.float32) - loss_target)
        return 0.5 * _jnp.sum(_jnp.mean(err, axis=-1)) if err.ndim else 0.5 * err


def _adamw(w, g, m, v):
    m = ADAM_B1 * m + (1.0 - ADAM_B1) * g
    v = ADAM_B2 * v + (1.0 - ADAM_B2) * _jnp.square(g)
    m_hat = m / (1.0 - ADAM_B1 ** ADAM_STEP)
    v_hat = v / (1.0 - ADAM_B2 ** ADAM_STEP)
    delta = -ADAM_LR * (m_hat / (_jnp.sqrt(v_hat) + ADAM_EPS) + ADAM_WD * w)
    return delta, m, v


def reference(x, norm1_g, w_in, qk_norm_g, sink_a, rpb_c, w_br_a, w_br_b, w_br_c, w_o, norm2_g, w_gate_up, w_down, loss_target, m_norm1_g, m_w_in, m_qk_norm_g, m_sink_a, m_rpb_c, m_w_br_a, m_w_br_b, m_w_br_c, m_w_o, m_norm2_g, m_w_gate_up, m_w_down, v_norm1_g, v_w_in, v_qk_norm_g, v_sink_a, v_rpb_c, v_w_br_a, v_w_br_b, v_w_br_c, v_w_o, v_norm2_g, v_w_gate_up, v_w_down):
    given = dict(x=x, norm1_g=norm1_g, w_in=w_in, qk_norm_g=qk_norm_g, sink_a=sink_a, rpb_c=rpb_c, w_br_a=w_br_a, w_br_b=w_br_b, w_br_c=w_br_c, w_o=w_o, norm2_g=norm2_g, w_gate_up=w_gate_up, w_down=w_down, loss_target=loss_target, m_norm1_g=m_norm1_g, m_w_in=m_w_in, m_qk_norm_g=m_qk_norm_g, m_sink_a=m_sink_a, m_rpb_c=m_rpb_c, m_w_br_a=m_w_br_a, m_w_br_b=m_w_br_b, m_w_br_c=m_w_br_c, m_w_o=m_w_o, m_norm2_g=m_norm2_g, m_w_gate_up=m_w_gate_up, m_w_down=m_w_down, v_norm1_g=v_norm1_g, v_w_in=v_w_in, v_qk_norm_g=v_qk_norm_g, v_sink_a=v_sink_a, v_rpb_c=v_rpb_c, v_w_br_a=v_w_br_a, v_w_br_b=v_w_br_b, v_w_br_c=v_w_br_c, v_w_o=v_w_o, v_norm2_g=v_norm2_g, v_w_gate_up=v_w_gate_up, v_w_down=v_w_down)
    weights = {n: given[n] for n in TWIN_WEIGHTS}
    shared = {n: given[n] for n in SHARED_INPUTS}
    per_example = {n: given[n] for n in ['x']}
    grad_fn = _jax.value_and_grad(_loss, argnums=(0, 1))

    def one_microbatch(ex, loss_target):
        ex = dict(ex)
        diff = ex.pop(TWIN_DIFF_INPUT)
        return grad_fn(weights, diff, {**shared, **ex}, loss_target)

    if N_MICROBATCH == 1:
        loss, (grad_w, grad_x) = one_microbatch(per_example, given["loss_target"])
    else:
        def body(carry, xs):
            loss_sum, grad_sum = carry
            l_k, (gw_k, gx_k) = one_microbatch(xs[0], xs[1])
            with _jax.named_scope("update"):
                return (loss_sum + l_k, _jax.tree.map(_jnp.add, grad_sum, gw_k)), gx_k

        init = (_jnp.zeros((), _jnp.float32), _jax.tree.map(_jnp.zeros_like, weights))
        (loss, grad_w), grad_x = _jax.lax.scan(body, init, (per_example, given["loss_target"]))
    with _jax.named_scope("update"):
        delta_w, new_m, new_v = {}, {}, {}
        for n in TWIN_WEIGHTS:
            delta_w[n], new_m[n], new_v[n] = _adamw(weights[n], grad_w[n], given["m_" + n], given["v_" + n])
    return (loss, grad_x, *[grad_w[n] for n in TWIN_WEIGHTS], *[delta_w[n] for n in TWIN_WEIGHTS],
            *[new_m[n] for n in TWIN_WEIGHTS], *[new_v[n] for n in TWIN_WEIGHTS])
```

```python
import functools
import math

import numpy as np
import jax
import jax.numpy as jnp
from jax import lax
from jax.experimental import pallas as pl
from jax.experimental.pallas import tpu as pltpu

F32 = jnp.float32
BF16 = jnp.bfloat16
MESH = pl.DeviceIdType.MESH
N_DEV = 8

HEAD_DIM = 128
NORM_EPS = 1e-6
ROPE_THETA = 10000.0
ATT_SCALE = HEAD_DIM ** -0.5
NEG = -1e30

A_Q_HEADS, A_KV_HEADS, A_RADIUS = 8, 2, 128
A_GROUP = A_Q_HEADS // A_KV_HEADS
B_PATTERNS = ((128, 1), (512, 4), (2048, 16))
B_HG = 4
B_HEADS = len(B_PATTERNS) * B_HG
C_HEADS, GRID_W, C_WIN_ROWS, C_WIN_COLS = 8, 64, 8, 16
C_NREL = 2 * C_WIN_ROWS - 1
C_NCOL = 2 * C_WIN_COLS - 1

PC_QA, PC_KA, PC_VA = 0, 8, 10
PC_QB, PC_KB, PC_VB = 12, 24, 36
PC_QC, PC_KC, PC_VC = 48, 56, 64
N_QKV_CHUNKS = 72
Q_PIECES = ((0, 8, PC_QA), (8, 12, PC_QB), (20, 8, PC_QC))
K_PIECES = ((0, 2, PC_KA), (2, 12, PC_KB), (14, 8, PC_KC))
V_PIECES = ((0, 2, PC_VA), (2, 12, PC_VB), (14, 8, PC_VC))
NQ_CHUNKS, NK_CHUNKS = 28, 22
Q_ROPE_UPTO, K_ROPE_UPTO = 20, 14

ADAM_LR, ADAM_B1, ADAM_B2, ADAM_EPS, ADAM_WD, ADAM_STEP = 0.001, 0.9, 0.999, 1e-08, 0.01, 10

VMEM_LIMIT = 48 * 1024 * 1024


def _tile(dim, prefs):
    for p in prefs:
        if dim % p == 0:
            return p
    return dim


def _params(sem, **kw):
    return pltpu.CompilerParams(dimension_semantics=sem, vmem_limit_bytes=VMEM_LIMIT, **kw)


def _piece_map(pieces):
    def f(c):
        out = c - pieces[0][0] + pieces[0][2]
        for first, _, pfirst in pieces[1:]:
            out = jnp.where(c >= first, c - first + pfirst, out)
        return out
    return f


def _mm_nn(a, w, layer, *, out_dtype, name, res=None):
    M, K = a.shape
    nb, _, Kw, ns = w.shape
    assert Kw == K
    tm = _tile(M, (512, 256))
    tn = _tile(ns, (640, 512, 256, 128)) if ns > 1408 else ns
    tk = _tile(K, (2048, 1408, 1024, 512, 256))
    nj, nk = ns // tn, K // tk

    def body(*refs):
        if res is None:
            a_ref, w_ref, o_ref, acc_ref = refs
            r_ref = None
        else:
            a_ref, w_ref, r_ref, o_ref, acc_ref = refs
        k = pl.program_id(3)
        part = jnp.dot(a_ref[...].astype(BF16), w_ref[...], preferred_element_type=F32)

        @pl.when(k == 0)
        def _():
            acc_ref[...] = part

        @pl.when(k > 0)
        def _():
            acc_ref[...] += part

        @pl.when(k == nk - 1)
        def _():
            r = acc_ref[...]
            if r_ref is not None:
                r = r + r_ref[...]
            o_ref[...] = r.astype(out_dtype)

    in_specs = [pl.BlockSpec((tm, tk), lambda i, b, j, k: (i, k)),
                pl.BlockSpec((None, None, tk, tn), lambda i, b, j, k: (b, layer, k, j))]
    args = [a, w]
    if res is not None:
        in_specs.append(pl.BlockSpec((tm, tn), lambda i, b, j, k: (i, b * nj + j)))
        args.append(res)
    return pl.pallas_call(
        body, name=name, grid=(M // tm, nb, nj, nk), in_specs=in_specs,
        out_specs=pl.BlockSpec((tm, tn), lambda i, b, j, k: (i, b * nj + j)),
        out_shape=jax.ShapeDtypeStruct((M, nb * ns), out_dtype),
        scratch_shapes=[pltpu.VMEM((tm, tn), F32)],
        compiler_params=_params(("parallel", "parallel", "parallel", "arbitrary")),
    )(*args)


def _mm_nt(a, w, layer, *, out_dtype, name):
    M, N = a.shape
    nb, _, K, ns = w.shape
    assert N == nb * ns
    tm = _tile(M, (1024, 512, 256))
    tn = _tile(ns, (640, 512, 256, 128)) if ns > 1408 else ns
    tk = _tile(K, (1024, 512, 256))
    nj = ns // tn
    nred = nb * nj

    def body(a_ref, w_ref, o_ref, acc_ref):
        s = pl.program_id(2) * nj + pl.program_id(3)
        part = lax.dot_general(a_ref[...].astype(BF16), w_ref[...], (((1,), (1,)), ((), ())),
                               preferred_element_type=F32)

        @pl.when(s == 0)
        def _():
            acc_ref[...] = part

        @pl.when(s > 0)
        def _():
            acc_ref[...] += part

        @pl.when(s == nred - 1)
        def _():
            o_ref[...] = acc_ref[...].astype(out_dtype)

    return pl.pallas_call(
        body, name=name, grid=(M // tm, K // tk, nb, nj),
        in_specs=[pl.BlockSpec((tm, tn), lambda i, kk, b, j: (i, b * nj + j)),
                  pl.BlockSpec((None, None, tk, tn), lambda i, kk, b, j: (b, layer, kk, j))],
        out_specs=pl.BlockSpec((tm, tk), lambda i, kk, b, j: (i, kk)),
        out_shape=jax.ShapeDtypeStruct((M, K), out_dtype),
        scratch_shapes=[pltpu.VMEM((tm, tk), F32)],
        compiler_params=_params(("parallel", "parallel", "arbitrary", "arbitrary")),
    )(a, w)


def _mm_tn(a, g, nb, *, name):
    M, Ka = a.shape
    N = g.shape[1]
    ns = N // nb
    tka = _tile(Ka, (1024, 512, 256))
    tn = _tile(ns, (640, 512, 256, 128)) if ns > 1408 else ns
    tm = _tile(M, (1024, 512, 256))
    nj, nm = ns // tn, M // tm

    def body(a_ref, g_ref, o_ref, acc_ref):
        m = pl.program_id(3)
        part = lax.dot_general(a_ref[...].astype(BF16), g_ref[...].astype(BF16), (((0,), (0,)), ((), ())),
                               preferred_element_type=F32)

        @pl.when(m == 0)
        def _():
            acc_ref[...] = part

        @pl.when(m > 0)
        def _():
            acc_ref[...] += part

        @pl.when(m == nm - 1)
        def _():
            o_ref[...] = acc_ref[...].astype(BF16)

    return pl.pallas_call(
        body, name=name, grid=(Ka // tka, nb, nj, nm),
        in_specs=[pl.BlockSpec((tm, tka), lambda ka, b, j, m: (m, ka)),
                  pl.BlockSpec((tm, tn), lambda ka, b, j, m: (m, b * nj + j))],
        out_specs=pl.BlockSpec((None, tka, tn), lambda ka, b, j, m: (b, ka, j)),
        out_shape=jax.ShapeDtypeStruct((nb, Ka, ns), BF16),
        scratch_shapes=[pltpu.VMEM((tka, tn), F32)],
        compiler_params=_params(("parallel", "parallel", "parallel", "arbitrary")),
    )(a, g)


def _exact_mm(a, e, *, name):
    R, K = a.shape
    N = e.shape[1]

    def body(a_ref, e_ref, o_ref):
        x = a_ref[...]
        hi = x.astype(BF16)
        r1 = x - hi.astype(F32)
        mid = r1.astype(BF16)
        lo = (r1 - mid.astype(F32)).astype(BF16)
        ev = e_ref[...]
        o_ref[...] = (jnp.dot(hi, ev, preferred_element_type=F32) + jnp.dot(mid, ev, preferred_element_type=F32)
                      + jnp.dot(lo, ev, preferred_element_type=F32))

    return pl.pallas_call(body, name=name, out_shape=jax.ShapeDtypeStruct((R, N), F32),
                          compiler_params=pltpu.CompilerParams(vmem_limit_bytes=VMEM_LIMIT))(a, e)


def _rms_fwd(x, g, *, name):
    L, D = x.shape
    tl = _tile(L, (256, 128))

    def body(x_ref, g_ref, h_ref):
        xv = x_ref[...]
        rstd = lax.rsqrt(jnp.mean(xv * xv, axis=-1, keepdims=True) + NORM_EPS)
        h_ref[...] = (xv * rstd * g_ref[...]).astype(BF16)

    return pl.pallas_call(
        body, name=name, grid=(L // tl,),
        in_specs=[pl.BlockSpec((tl, D), lambda t: (t, 0)), pl.BlockSpec((1, D), lambda t: (0, 0))],
        out_specs=pl.BlockSpec((tl, D), lambda t: (t, 0)),
        out_shape=jax.ShapeDtypeStruct((L, D), BF16),
        compiler_params=_params(("parallel",)),
    )(x, g)


def _rms_bwd(x, g, dy, dres, *, name):
    L, D = x.shape
    tl = _tile(L, (128,))

    def body(x_ref, g_ref, dy_ref, dres_ref, dx_ref, dxb_ref, dg_ref):
        t = pl.program_id(0)
        xv = x_ref[...]
        rstd = lax.rsqrt(jnp.mean(xv * xv, axis=-1, keepdims=True) + NORM_EPS)
        xhat = xv * rstd
        dyv = dy_ref[...]
        dxhat = dyv * g_ref[...]
        c = jnp.mean(dxhat * xhat, axis=-1, keepdims=True)
        dx = dres_ref[...] + rstd * (dxhat - xhat * c)
        dx_ref[...] = dx
        dxb_ref[...] = dx.astype(BF16)
        dgp = jnp.sum(dyv * xhat, axis=0, keepdims=True)

        @pl.when(t == 0)
        def _():
            dg_ref[...] = dgp

        @pl.when(t > 0)
        def _():
            dg_ref[...] += dgp

    row = pl.BlockSpec((tl, D), lambda t: (t, 0))
    vec = pl.BlockSpec((1, D), lambda t: (0, 0))
    return pl.pallas_call(
        body, name=name, grid=(L // tl,), in_specs=[row, vec, row, row], out_specs=[row, row, vec],
        out_shape=[jax.ShapeDtypeStruct((L, D), F32), jax.ShapeDtypeStruct((L, D), BF16),
                   jax.ShapeDtypeStruct((1, D), F32)],
        compiler_params=_params(("arbitrary",)),
    )(x, g, dy, dres)


def _gate_fwd(proj, ta, tb, tc, *, name):
    L, D = ta.shape
    tl, tcw = _tile(L, (256, 128)), _tile(D, (512, 256, 128))
    off = N_QKV_CHUNKS * HEAD_DIM // tcw
    nd = D // tcw

    def body(g0, g1, g2, a_ref, b_ref, c_ref, o_ref):
        m = (jax.nn.sigmoid(g0[...]) * a_ref[...] + jax.nn.sigmoid(g1[...]) * b_ref[...]
             + jax.nn.sigmoid(g2[...]) * c_ref[...])
        o_ref[...] = m.astype(BF16)

    blk = pl.BlockSpec((tl, tcw), lambda t, j: (t, j))
    gl = [pl.BlockSpec((tl, tcw), functools.partial(lambda t, j, i: (t, off + i * nd + j), i=i)) for i in range(3)]
    return pl.pallas_call(
        body, name=name, grid=(L // tl, nd), in_specs=gl + [blk, blk, blk], out_specs=blk,
        out_shape=jax.ShapeDtypeStruct((L, D), BF16),
        compiler_params=_params(("parallel", "parallel")),
    )(proj, proj, proj, ta, tb, tc)


def _gate_bwd(proj, ta, tb, tc, dmerged, *, name):
    L, D = ta.shape
    ncols = proj.shape[1]
    tl, tcw = _tile(L, (256, 128)), _tile(D, (512, 256, 128))
    off = N_QKV_CHUNKS * HEAD_DIM // tcw
    nd = D // tcw

    def body(g0, g1, g2, a_ref, b_ref, c_ref, dm_ref, da_ref, db_ref, dc_ref, dgl_ref):
        i = pl.program_id(2)
        dm = dm_ref[...]
        s0, s1, s2 = jax.nn.sigmoid(g0[...]), jax.nn.sigmoid(g1[...]), jax.nn.sigmoid(g2[...])

        @pl.when(i == 0)
        def _():
            da_ref[...] = (dm * s0).astype(BF16)
            db_ref[...] = (dm * s1).astype(BF16)
            dc_ref[...] = (dm * s2).astype(BF16)

        sel_s = jnp.where(i == 0, s0, jnp.where(i == 1, s1, s2))
        sel_t = jnp.where(i == 0, a_ref[...], jnp.where(i == 1, b_ref[...], c_ref[...]))
        dgl_ref[...] = (dm * sel_t * sel_s * (1.0 - sel_s)).astype(BF16)

    blk = pl.BlockSpec((tl, tcw), lambda t, j, i: (t, j))
    gl = [pl.BlockSpec((tl, tcw), functools.partial(lambda t, j, i, q: (t, off + q * nd + j), q=q)) for q in range(3)]
    return pl.pallas_call(
        body, name=name, grid=(L // tl, nd, 3), in_specs=gl + [blk, blk, blk, blk],
        out_specs=[blk, blk, blk, pl.BlockSpec((tl, tcw), lambda t, j, i: (t, off + i * nd + j))],
        out_shape=[jax.ShapeDtypeStruct((L, D), BF16)] * 3 + [jax.ShapeDtypeStruct((L, ncols), BF16)],
        compiler_params=_params(("parallel", "parallel", "arbitrary")),
    )(proj, proj, proj, ta, tb, tc, dmerged)


def _swiglu_fwd(gu, *, name):
    L, F2 = gu.shape
    F = F2 // 2
    tl, tcw = _tile(L, (256, 128)), _tile(F, (512, 256, 128))
    nf = F // tcw

    def body(g_ref, u_ref, o_ref):
        gt = g_ref[...]
        o_ref[...] = (gt * jax.nn.sigmoid(gt) * u_ref[...]).astype(BF16)

    return pl.pallas_call(
        body, name=name, grid=(L // tl, nf),
        in_specs=[pl.BlockSpec((tl, tcw), lambda t, j: (t, j)), pl.BlockSpec((tl, tcw), lambda t, j: (t, nf + j))],
        out_specs=pl.BlockSpec((tl, tcw), lambda t, j: (t, j)),
        out_shape=jax.ShapeDtypeStruct((L, F), BF16),
        compiler_params=_params(("parallel", "parallel")),
    )(gu, gu)


def _swiglu_bwd(gu, dact, *, name):
    L, F2 = gu.shape
    F = F2 // 2
    tl, tcw = _tile(L, (256, 128)), _tile(F, (512, 256, 128))
    nf = F // tcw

    def body(g_ref, u_ref, d_ref, o_ref):
        half = pl.program_id(1)
        gt, d = g_ref[...], d_ref[...]
        sg = jax.nn.sigmoid(gt)
        dgt = d * u_ref[...] * sg * (1.0 + gt * (1.0 - sg))
        dup = d * gt * sg
        o_ref[...] = jnp.where(half == 0, dgt, dup).astype(BF16)

    lo = pl.BlockSpec((tl, tcw), lambda t, h, j: (t, j))
    hi = pl.BlockSpec((tl, tcw), lambda t, h, j: (t, nf + j))
    return pl.pallas_call(
        body, name=name, grid=(L // tl, 2, nf), in_specs=[lo, hi, lo],
        out_specs=pl.BlockSpec((tl, tcw), lambda t, h, j: (t, h * nf + j)),
        out_shape=jax.ShapeDtypeStruct((L, F2), BF16),
        compiler_params=_params(("parallel", "arbitrary", "arbitrary")),
    )(gu, gu, dact)


def _loss(y, tgt, *, name):
    L, D = y.shape
    tl = _tile(L, (256, 128))
    nt = L // tl

    def body(y_ref, t_ref, dy_ref, dyb_ref, loss_ref, acc_ref):
        t = pl.program_id(0)
        e = y_ref[...] - t_ref[...]
        dy = e * (1.0 / D)
        dy_ref[...] = dy
        dyb_ref[...] = dy.astype(BF16)
        part = jnp.sum(e * e, axis=0, keepdims=True)

        @pl.when(t == 0)
        def _():
            acc_ref[...] = part

        @pl.when(t > 0)
        def _():
            acc_ref[...] += part

        @pl.when(t == nt - 1)
        def _():
            loss_ref[...] = jnp.broadcast_to(jnp.sum(acc_ref[...], axis=-1, keepdims=True) * (0.5 / D), (1, 128))

    row = pl.BlockSpec((tl, D), lambda t: (t, 0))
    return pl.pallas_call(
        body, name=name, grid=(nt,), in_specs=[row, row],
        out_specs=[row, row, pl.BlockSpec((1, 128), lambda t: (0, 0))],
        out_shape=[jax.ShapeDtypeStruct((L, D), F32), jax.ShapeDtypeStruct((L, D), BF16),
                   jax.ShapeDtypeStruct((1, 128), F32)],
        scratch_shapes=[pltpu.VMEM((1, D), F32)],
        compiler_params=_params(("arbitrary",)),
    )(y, tgt)


def _rope(v, cos, sin_signed):
    return v * cos + pltpu.roll(v, HEAD_DIM // 2, 1) * sin_signed


def _qk_fwd(proj, gtab, cos, sin, pieces, nchunks, rope_upto, *, name):
    L = proj.shape[0]
    tl = _tile(L, (512, 256, 128))
    pmap = _piece_map(pieces)

    def body(p_ref, g_ref, cos_ref, sin_ref, o_ref):
        c = pl.program_id(1)
        x = p_ref[...]
        rstd = lax.rsqrt(jnp.mean(x * x, axis=-1, keepdims=True) + NORM_EPS)
        y = x * rstd * g_ref[...]

        @pl.when(c < rope_upto)
        def _():
            o_ref[...] = _rope(y, cos_ref[...], sin_ref[...])

        @pl.when(c >= rope_upto)
        def _():
            o_ref[...] = y

    pos = pl.BlockSpec((tl, HEAD_DIM), lambda t, c: (t, 0))
    return pl.pallas_call(
        body, name=name, grid=(L // tl, nchunks),
        in_specs=[pl.BlockSpec((tl, HEAD_DIM), lambda t, c: (t, pmap(c))),
                  pl.BlockSpec((None, 1, HEAD_DIM), lambda t, c: (c, 0, 0)), pos, pos],
        out_specs=pl.BlockSpec((tl, HEAD_DIM), lambda t, c: (t, c)),
        out_shape=jax.ShapeDtypeStruct((L, nchunks * HEAD_DIM), F32),
        compiler_params=_params(("parallel", "parallel")),
    )(proj, gtab, cos, sin)


def _qk_bwd(dqk, proj, gtab, cos, sin, dproj, pieces, nchunks, rope_upto, *, name):
    L = proj.shape[0]
    tl = _tile(L, (512, 256, 128))
    pmap = _piece_map(pieces)

    def body(d_ref, p_ref, g_ref, cos_ref, sin_ref, _, o_ref, dg_ref):
        c, t = pl.program_id(0), pl.program_id(1)
        x = p_ref[...]
        rstd = lax.rsqrt(jnp.mean(x * x, axis=-1, keepdims=True) + NORM_EPS)
        xhat = x * rstd
        dy = d_ref[...]
        dy = jnp.where(c < rope_upto, _rope(dy, cos_ref[...], -sin_ref[...]), dy)
        dxhat = dy * g_ref[...]
        cm = jnp.mean(dxhat * xhat, axis=-1, keepdims=True)
        o_ref[...] = (rstd * (dxhat - xhat * cm)).astype(BF16)
        dgp = jnp.sum(dy * xhat, axis=0, keepdims=True)

        @pl.when(t == 0)
        def _():
            dg_ref[...] = dgp

        @pl.when(t > 0)
        def _():
            dg_ref[...] += dgp

    pos = pl.BlockSpec((tl, HEAD_DIM), lambda c, t: (t, 0))
    gspec = pl.BlockSpec((None, 1, HEAD_DIM), lambda c, t: (c, 0, 0))
    out, dg = pl.pallas_call(
        body, name=name, grid=(nchunks, L // tl),
        in_specs=[pl.BlockSpec((tl, HEAD_DIM), lambda c, t: (t, c)),
                  pl.BlockSpec((tl, HEAD_DIM), lambda c, t: (t, pmap(c))), gspec, pos, pos,
                  pl.BlockSpec(memory_space=pl.ANY)],
        out_specs=[pl.BlockSpec((tl, HEAD_DIM), lambda c, t: (t, pmap(c))), gspec],
        out_shape=[jax.ShapeDtypeStruct(dproj.shape, BF16), jax.ShapeDtypeStruct((nchunks, 1, HEAD_DIM), F32)],
        input_output_aliases={5: 0},
        compiler_params=_params(("parallel", "arbitrary")),
    )(dqk, proj, gtab, cos, sin, dproj)
    return out, dg


def _v_bwd(dv, dproj, *, name):
    L = dv.shape[0]
    tl = _tile(L, (512, 256, 128))
    pmap = _piece_map(V_PIECES)

    def body(d_ref, _, o_ref):
        o_ref[...] = d_ref[...].astype(BF16)

    return pl.pallas_call(
        body, name=name, grid=(L // tl, NK_CHUNKS),
        in_specs=[pl.BlockSpec((tl, HEAD_DIM), lambda t, c: (t, c)), pl.BlockSpec(memory_space=pl.ANY)],
        out_specs=pl.BlockSpec((tl, HEAD_DIM), lambda t, c: (t, pmap(c))),
        out_shape=jax.ShapeDtypeStruct(dproj.shape, BF16),
        input_output_aliases={1: 0},
        compiler_params=_params(("parallel", "parallel")),
    )(dv, dproj)


def _band_geometry(L, dil, radius):
    n = L // dil
    bq = min(128, n)
    width = min(bq + 2 * radius, n)
    return n, bq, width


def _band_rows(dil, r, first, count):
    if dil == 1:
        return pl.ds(pl.multiple_of(first, 8), count)
    return pl.ds(r + first * dil, count, stride=dil)


def _band_mask(i, bq, width, radius, ws):
    qpos = i * bq + lax.broadcasted_iota(jnp.int32, (bq, width), 0)
    kpos = ws + lax.broadcasted_iota(jnp.int32, (bq, width), 1)
    return jnp.abs(kpos - qpos) <= radius


def _band_fwd(qn, kn, proj, *, dil, radius, nkv, group, q0, k0, v0, sink=None, name):
    L = qn.shape[0]
    n, bq, width = _band_geometry(L, dil, radius)
    tq = bq * dil
    nh = nkv * group

    def body(*refs):
        if sink is None:
            q_ref, k_ref, v_ref, o_ref, lse_ref = refs
        else:
            q_ref, k_ref, v_ref, s_ref, o_ref, lse_ref = refs
        i = pl.program_id(2)
        ws = jnp.clip(i * bq - radius, 0, n - width)
        valid = _band_mask(i, bq, width, radius, ws)

        def one(r, carry):
            qrows = _band_rows(dil, r, 0, bq)
            krows = _band_rows(dil, r, ws, width)
            q = q_ref[qrows, :].astype(BF16)
            k = k_ref[krows, :].astype(BF16)
            v = v_ref[krows, :].astype(BF16)
            s = lax.dot_general(q, k, (((1,), (1,)), ((), ())), preferred_element_type=F32) * ATT_SCALE
            s = jnp.where(valid, s, NEG)
            m = jnp.max(s, axis=-1, keepdims=True)
            if sink is not None:
                m = jnp.maximum(m, s_ref[...][:, :1])
            p = jnp.exp(s - m)
            denom = jnp.sum(p, axis=-1, keepdims=True)
            if sink is not None:
                denom = denom + jnp.exp(s_ref[...][:, :1] - m)
            pn = (p / denom).astype(BF16)
            o_ref[qrows, :] = jnp.dot(pn, v, preferred_element_type=F32)
            lse_ref[qrows, :] = jnp.broadcast_to(m + jnp.log(denom), (bq, HEAD_DIM))
            return carry

        if dil == 1:
            one(0, 0)
        else:
            lax.fori_loop(0, dil, one, 0)

    qspec = pl.BlockSpec((tq, HEAD_DIM), lambda hk, g, i: (i, q0 + hk * group + g))
    in_specs = [qspec,
                pl.BlockSpec((L, HEAD_DIM), lambda hk, g, i: (0, k0 + hk)),
                pl.BlockSpec((L, HEAD_DIM), lambda hk, g, i: (0, v0 + hk))]
    args = [qn, kn, proj]
    if sink is not None:
        in_specs.append(pl.BlockSpec((None, 1, HEAD_DIM), lambda hk, g, i: (hk * group + g, 0, 0)))
        args.append(sink)
    ospec = pl.BlockSpec((tq, HEAD_DIM), lambda hk, g, i: (i, hk * group + g))
    return pl.pallas_call(
        body, name=name, grid=(nkv, group, n // bq), in_specs=in_specs, out_specs=[ospec, ospec],
        out_shape=[jax.ShapeDtypeStruct((L, nh * HEAD_DIM), F32)] * 2,
        compiler_params=_params(("parallel", "parallel", "arbitrary")),
    )(*args)


def _band_bwd(qn, kn, proj, do, o, lse, dq_buf, dk_buf, dv_buf, *, dil, radius, nkv, group, q0, k0, v0, o0,
              sink=None, name):
    L = qn.shape[0]
    n, bq, width = _band_geometry(L, dil, radius)
    tq = bq * dil
    nh = nkv * group
    n_in = 6 + (1 if sink is not None else 0)

    def body(*refs):
        q_ref, k_ref, v_ref, do_ref, o_ref, lse_ref = refs[:6]
        s_ref = refs[6] if sink is not None else None
        outs = refs[n_in + 3:]
        dq_ref, dk_ref, dv_ref = outs[:3]
        ds_ref = outs[3] if sink is not None else None
        g, i = pl.program_id(1), pl.program_id(2)
        ws = jnp.clip(i * bq - radius, 0, n - width)
        valid = _band_mask(i, bq, width, radius, ws)

        @pl.when((g == 0) & (i == 0))
        def _():
            dk_ref[...] = jnp.zeros_like(dk_ref)
            dv_ref[...] = jnp.zeros_like(dv_ref)

        if sink is not None:
            @pl.when(i == 0)
            def _():
                ds_ref[...] = jnp.zeros_like(ds_ref)

        def one(r, carry):
            qrows = _band_rows(dil, r, 0, bq)
            krows = _band_rows(dil, r, ws, width)
            q = q_ref[qrows, :].astype(BF16)
            k = k_ref[krows, :].astype(BF16)
            v = v_ref[krows, :].astype(BF16)
            dov = do_ref[qrows, :]
            lse_v = lse_ref[qrows, :][:, :1]
            delta = jnp.sum(dov * o_ref[qrows, :], axis=-1, keepdims=True)
            dob = dov.astype(BF16)
            s = lax.dot_general(q, k, (((1,), (1,)), ((), ())), preferred_element_type=F32) * ATT_SCALE
            p = jnp.where(valid, jnp.exp(s - lse_v), 0.0)
            dp = lax.dot_general(dob, v, (((1,), (1,)), ((), ())), preferred_element_type=F32)
            dsb = (p * (dp - delta)).astype(BF16)
            dq_ref[qrows, :] = jnp.dot(dsb, k, preferred_element_type=F32) * ATT_SCALE
            dk_ref[krows, :] += lax.dot_general(dsb, q, (((0,), (0,)), ((), ())),
                                                preferred_element_type=F32) * ATT_SCALE
            dv_ref[krows, :] += lax.dot_general(p.astype(BF16), dob, (((0,), (0,)), ((), ())),
                                                preferred_element_type=F32)
            if sink is not None:
                ps = jnp.exp(s_ref[...][:, :1] - lse_v)
                ds_ref[...] += jnp.broadcast_to(jnp.sum(-ps * delta, axis=0, keepdims=True), (1, HEAD_DIM))
            return carry

        if dil == 1:
            one(0, 0)
        else:
            lax.fori_loop(0, dil, one, 0)

    hspec = pl.BlockSpec((tq, HEAD_DIM), lambda hk, g, i: (i, o0 + hk * group + g))
    qspec = pl.BlockSpec((tq, HEAD_DIM), lambda hk, g, i: (i, q0 + hk * group + g))
    kspec = pl.BlockSpec((L, HEAD_DIM), lambda hk, g, i: (0, k0 + hk))
    any_spec = pl.BlockSpec(memory_space=pl.ANY)
    in_specs = [qspec, kspec, pl.BlockSpec((L, HEAD_DIM), lambda hk, g, i: (0, v0 + hk)), hspec, hspec, hspec]
    args = [qn, kn, proj, do, o, lse]
    if sink is not None:
        in_specs.append(pl.BlockSpec((None, 1, HEAD_DIM), lambda hk, g, i: (hk * group + g, 0, 0)))
        args.append(sink)
    in_specs += [any_spec] * 3
    args += [dq_buf, dk_buf, dv_buf]
    out_specs = [qspec, kspec, kspec]
    out_shape = [jax.ShapeDtypeStruct(dq_buf.shape, F32), jax.ShapeDtypeStruct(dk_buf.shape, F32),
                 jax.ShapeDtypeStruct(dv_buf.shape, F32)]
    if sink is not None:
        out_specs.append(pl.BlockSpec((None, 1, HEAD_DIM), lambda hk, g, i: (hk * group + g, 0, 0)))
        out_shape.append(jax.ShapeDtypeStruct((nh, 1, HEAD_DIM), F32))
    return pl.pallas_call(
        body, name=name, grid=(nkv, group, n // bq), in_specs=in_specs, out_specs=out_specs, out_shape=out_shape,
        input_output_aliases={n_in: 0, n_in + 1: 1, n_in + 2: 2},
        compiler_params=_params(("parallel", "arbitrary", "arbitrary")),
    )(*args)


def _combine_b(os_, lses, *, name):
    L, W = os_[0].shape
    tl = _tile(L, (256, 128))

    def body(o0, o1, o2, l0, l1, l2, out_ref, lt_ref):
        a, b, c = l0[...], l1[...], l2[...]
        m = jnp.maximum(jnp.maximum(a, b), c)
        ea, eb, ec = jnp.exp(a - m), jnp.exp(b - m), jnp.exp(c - m)
        tot = ea + eb + ec
        out_ref[...] = (ea * o0[...] + eb * o1[...] + ec * o2[...]) / tot
        lt_ref[...] = m + jnp.log(tot)

    blk = pl.BlockSpec((tl, W), lambda t: (t, 0))
    return pl.pallas_call(
        body, name=name, grid=(L // tl,), in_specs=[blk] * 6, out_specs=[blk, blk],
        out_shape=[jax.ShapeDtypeStruct((L, W), F32)] * 2, compiler_params=_params(("parallel",)),
    )(*os_, *lses)


def _c_geometry(L):
    rows = L // GRID_W
    assert rows >= C_WIN_ROWS
    return rows


def _c_col_ok():
    cq = lax.broadcasted_iota(jnp.int32, (GRID_W, GRID_W), 0)
    ck = lax.broadcasted_iota(jnp.int32, (GRID_W, GRID_W), 1)
    start = jnp.clip(cq - C_WIN_COLS // 2, 0, GRID_W - C_WIN_COLS)
    return (ck >= start) & (ck < start + C_WIN_COLS)


def _c_fwd(qn, kn, proj, bias_t, *, name):
    L = qn.shape[0]
    rows = _c_geometry(L)

    def body(q_ref, k_ref, v_ref, t_ref, o_ref, lse_ref):
        r = pl.program_id(1)
        rs = jnp.clip(r - C_WIN_ROWS // 2, 0, rows - C_WIN_ROWS)
        base = rs - r + (C_WIN_ROWS - 1)
        ok = _c_col_ok()
        q = q_ref[...].astype(BF16)
        ss = []
        for w in range(C_WIN_ROWS):
            k = k_ref[pl.ds(pl.multiple_of((rs + w) * GRID_W, GRID_W), GRID_W), :].astype(BF16)
            s = lax.dot_general(q, k, (((1,), (1,)), ((), ())), preferred_element_type=F32) * ATT_SCALE
            ss.append(jnp.where(ok, s + t_ref[base + w], NEG))
        m = functools.reduce(jnp.maximum, [jnp.max(s, axis=-1, keepdims=True) for s in ss])
        ps = [jnp.exp(s - m) for s in ss]
        denom = functools.reduce(jnp.add, [jnp.sum(p, axis=-1, keepdims=True) for p in ps])
        acc = jnp.zeros((GRID_W, HEAD_DIM), F32)
        for w in range(C_WIN_ROWS):
            v = v_ref[pl.ds(pl.multiple_of((rs + w) * GRID_W, GRID_W), GRID_W), :].astype(BF16)
            acc = acc + jnp.dot((ps[w] / denom).astype(BF16), v, preferred_element_type=F32)
        o_ref[...] = acc
        lse_ref[...] = jnp.broadcast_to(m + jnp.log(denom), (GRID_W, HEAD_DIM))

    ospec = pl.BlockSpec((GRID_W, HEAD_DIM), lambda h, r: (r, h))
    return pl.pallas_call(
        body, name=name, grid=(C_HEADS, rows),
        in_specs=[pl.BlockSpec((GRID_W, HEAD_DIM), lambda h, r: (r, 20 + h)),
                  pl.BlockSpec((L, HEAD_DIM), lambda h, r: (0, 14 + h)),
                  pl.BlockSpec((L, HEAD_DIM), lambda h, r: (0, PC_VC + h)),
                  pl.BlockSpec((None, C_NREL, GRID_W, GRID_W), lambda h, r: (h, 0, 0, 0))],
        out_specs=[ospec, ospec],
        out_shape=[jax.ShapeDtypeStruct((L, C_HEADS * HEAD_DIM), F32)] * 2,
        compiler_params=_params(("parallel", "arbitrary")),
    )(qn, kn, proj, bias_t)


def _c_bwd(qn, kn, proj, bias_t, do, o, lse, dq_buf, dk_buf, dv_buf, *, name):
    L = qn.shape[0]
    rows = _c_geometry(L)

    def body(q_ref, k_ref, v_ref, t_ref, do_ref, o_ref, lse_ref, _a, _b, _c, dq_ref, dk_ref, dv_ref, dt_ref):
        r = pl.program_id(1)
        rs = jnp.clip(r - C_WIN_ROWS // 2, 0, rows - C_WIN_ROWS)
        base = rs - r + (C_WIN_ROWS - 1)
        ok = _c_col_ok()

        @pl.when(r == 0)
        def _():
            dk_ref[...] = jnp.zeros_like(dk_ref)
            dv_ref[...] = jnp.zeros_like(dv_ref)
            dt_ref[...] = jnp.zeros_like(dt_ref)

        q = q_ref[...].astype(BF16)
        dov = do_ref[...]
        dob = dov.astype(BF16)
        lse_v = lse_ref[...][:, :1]
        delta = jnp.sum(dov * o_ref[...], axis=-1, keepdims=True)
        dq = jnp.zeros((GRID_W, HEAD_DIM), F32)
        for w in range(C_WIN_ROWS):
            krows = pl.ds(pl.multiple_of((rs + w) * GRID_W, GRID_W), GRID_W)
            k = k_ref[krows, :].astype(BF16)
            v = v_ref[krows, :].astype(BF16)
            s = lax.dot_general(q, k, (((1,), (1,)), ((), ())), preferred_element_type=F32) * ATT_SCALE
            p = jnp.where(ok, jnp.exp(s + t_ref[base + w] - lse_v), 0.0)
            dp = lax.dot_general(dob, v, (((1,), (1,)), ((), ())), preferred_element_type=F32)
            ds = p * (dp - delta)
            dt_ref[base + w] += ds
            dsb = ds.astype(BF16)
            dq = dq + jnp.dot(dsb, k, preferred_element_type=F32)
            dk_ref[krows, :] += lax.dot_general(dsb, q, (((0,), (0,)), ((), ())),
                                                preferred_element_type=F32) * ATT_SCALE
            dv_ref[krows, :] += lax.dot_general(p.astype(BF16), dob, (((0,), (0,)), ((), ())),
                                                preferred_element_type=F32)
        dq_ref[...] = dq * ATT_SCALE

    hspec = pl.BlockSpec((GRID_W, HEAD_DIM), lambda h, r: (r, h))
    qspec = pl.BlockSpec((GRID_W, HEAD_DIM), lambda h, r: (r, 20 + h))
    kspec = pl.BlockSpec((L, HEAD_DIM), lambda h, r: (0, 14 + h))
    tspec = pl.BlockSpec((None, C_NREL, GRID_W, GRID_W), lambda h, r: (h, 0, 0, 0))
    any_spec = pl.BlockSpec(memory_space=pl.ANY)
    return pl.pallas_call(
        body, name=name, grid=(C_HEADS, rows),
        in_specs=[qspec, kspec, pl.BlockSpec((L, HEAD_DIM), lambda h, r: (0, PC_VC + h)), tspec,
                  hspec, hspec, hspec, any_spec, any_spec, any_spec],
        out_specs=[qspec, kspec, kspec, tspec],
        out_shape=[jax.ShapeDtypeStruct(dq_buf.shape, F32), jax.ShapeDtypeStruct(dk_buf.shape, F32),
                   jax.ShapeDtypeStruct(dv_buf.shape, F32),
                   jax.ShapeDtypeStruct((C_HEADS, C_NREL, GRID_W, GRID_W), F32)],
        input_output_aliases={7: 0, 8: 1, 9: 2},
        compiler_params=_params(("parallel", "arbitrary")),
    )(qn, kn, proj, bias_t, do, o, lse, dq_buf, dk_buf, dv_buf)


def _c_expand_matrix():
    cq = np.arange(GRID_W)[:, None]
    ck = np.arange(GRID_W)[None, :]
    d = (ck - cq + (C_WIN_COLS - 1)).reshape(-1)
    e = np.zeros((GRID_W * GRID_W, HEAD_DIM), np.float32)
    okd = (d >= 0) & (d < C_NCOL)
    e[np.arange(GRID_W * GRID_W)[okd], d[okd]] = 1.0
    return e


def _peer(p):
    return (p // 4, (p // 2) % 2, p % 2)


def _my_index():
    return 4 * lax.axis_index("x") + 2 * lax.axis_index("y") + lax.axis_index("c")


def _all_gather_weights(shards, kinds):
    nw = len(shards)
    out_shape = []
    for s, kind in zip(shards, kinds):
        if kind == "col":
            out_shape.append(jax.ShapeDtypeStruct((N_DEV,) + s.shape, s.dtype))
        else:
            out_shape.append(jax.ShapeDtypeStruct((s.shape[0], N_DEV * s.shape[1], s.shape[2]), s.dtype))

    def body(*refs):
        ins, outs = refs[:nw], refs[nw:2 * nw]
        send_sems, recv_sems, loc_sems = refs[2 * nw:]
        me = _my_index()

        def slot(w, who):
            if kinds[w] == "col":
                return outs[w].at[who]
            r = shards[w].shape[1]
            return outs[w].at[:, pl.ds(who * r, r), :]

        local = [pltpu.make_async_copy(ins[w], slot(w, me), loc_sems.at[w]) for w in range(nw)]
        for cp in local:
            cp.start()
        sends = []
        for off in range(1, N_DEV):
            to = (me + off) % N_DEV
            for w in range(nw):
                cp = pltpu.make_async_remote_copy(ins[w], slot(w, me), send_sems.at[w, off], recv_sems.at[w, off],
                                                  device_id=_peer(to), device_id_type=MESH)
                cp.start()
                sends.append(cp)
        for off in range(1, N_DEV):
            frm = (me + N_DEV - off) % N_DEV
            for w in range(nw):
                pltpu.make_async_remote_copy(ins[w], slot(w, frm), send_sems.at[w, off], recv_sems.at[w, off],
                                             device_id=_peer(frm), device_id_type=MESH).wait_recv()
        for cp in sends:
            cp.wait_send()
        for cp in local:
            cp.wait()

    hbm = pl.BlockSpec(memory_space=pl.ANY)
    return pl.pallas_call(
        body, name="all_gather_weights", in_specs=[hbm] * nw, out_specs=[hbm] * nw, out_shape=out_shape,
        scratch_shapes=[pltpu.SemaphoreType.DMA((nw, N_DEV)), pltpu.SemaphoreType.DMA((nw, N_DEV)),
                        pltpu.SemaphoreType.DMA((nw,))],
        compiler_params=pltpu.CompilerParams(has_side_effects=True),
    )(*shards)


def _reduce_scatter_send(grads, kinds):
    nw = len(grads)
    out_shape = []
    for g, kind in zip(grads, kinds):
        if kind == "col":
            out_shape.append(jax.ShapeDtypeStruct(g.shape, g.dtype))
        else:
            out_shape.append(jax.ShapeDtypeStruct((N_DEV, g.shape[1] // N_DEV, g.shape[2]), g.dtype))

    def body(*refs):
        ins, outs = refs[:nw], refs[nw:2 * nw]
        send_sems, recv_sems, loc_sems = refs[2 * nw:]
        me = _my_index()

        def block(w, who):
            if kinds[w] == "col":
                return ins[w].at[who]
            r = grads[w].shape[1] // N_DEV
            return ins[w].at[0, pl.ds(who * r, r), :]

        local = [pltpu.make_async_copy(block(w, me), outs[w].at[me], loc_sems.at[w]) for w in range(nw)]
        for cp in local:
            cp.start()
        sends = []
        for off in range(1, N_DEV):
            to = (me + off) % N_DEV
            for w in range(nw):
                cp = pltpu.make_async_remote_copy(block(w, to), outs[w].at[me], send_sems.at[w, off],
                                                  recv_sems.at[w, off], device_id=_peer(to), device_id_type=MESH)
                cp.start()
                sends.append(cp)
        for off in range(1, N_DEV):
            frm = (me + N_DEV - off) % N_DEV
            for w in range(nw):
                pltpu.make_async_remote_copy(block(w, me), outs[w].at[frm], send_sems.at[w, off],
                                             recv_sems.at[w, off], device_id=_peer(frm),
                                             device_id_type=MESH).wait_recv()
        for cp in sends:
            cp.wait_send()
        for cp in local:
            cp.wait()

    hbm = pl.BlockSpec(memory_space=pl.ANY)
    return pl.pallas_call(
        body, name="reduce_scatter_send", in_specs=[hbm] * nw, out_specs=[hbm] * nw, out_shape=out_shape,
        scratch_shapes=[pltpu.SemaphoreType.DMA((nw, N_DEV)), pltpu.SemaphoreType.DMA((nw, N_DEV)),
                        pltpu.SemaphoreType.DMA((nw,))],
        compiler_params=pltpu.CompilerParams(has_side_effects=True),
    )(*grads)


def _all_reduce_small(x):
    R = x.shape[0]

    def body(x_ref, o_ref, gath, send_sems, recv_sems):
        me = _my_index()
        gath[me] = x_ref[...]
        sends = []
        for off in range(1, N_DEV):
            to = (me + off) % N_DEV
            cp = pltpu.make_async_remote_copy(x_ref, gath.at[me], send_sems.at[off], recv_sems.at[off],
                                              device_id=_peer(to), device_id_type=MESH)
            cp.start()
            sends.append(cp)
        for off in range(1, N_DEV):
            frm = (me + N_DEV - off) % N_DEV
            pltpu.make_async_remote_copy(x_ref, gath.at[frm], send_sems.at[off], recv_sems.at[off],
                                         device_id=_peer(frm), device_id_type=MESH).wait_recv()
        for cp in sends:
            cp.wait_send()
        acc = gath[0]
        for s in range(1, N_DEV):
            acc = acc + gath[s]
        o_ref[...] = acc

    vm = pl.BlockSpec(memory_space=pltpu.VMEM)
    return pl.pallas_call(
        body, name="all_reduce_small", in_specs=[vm], out_specs=vm, out_shape=jax.ShapeDtypeStruct((R, 128), F32),
        scratch_shapes=[pltpu.VMEM((N_DEV, R, 128), F32), pltpu.SemaphoreType.DMA((N_DEV,)),
                        pltpu.SemaphoreType.DMA((N_DEV,))],
        compiler_params=pltpu.CompilerParams(has_side_effects=True),
    )(x)


def _adamw_math(w, g, m, v):
    m = ADAM_B1 * m + (1.0 - ADAM_B1) * g
    v = ADAM_B2 * v + (1.0 - ADAM_B2) * (g * g)
    m_hat = m / (1.0 - ADAM_B1 ** ADAM_STEP)
    v_hat = v / (1.0 - ADAM_B2 ** ADAM_STEP)
    delta = -ADAM_LR * (m_hat / (jnp.sqrt(v_hat) + ADAM_EPS) + ADAM_WD * w)
    return delta, m, v


def _adamw_reduce(recv, w, m, v, *, name):
    nl, R, C = w.shape
    tr = _tile(R, (64, 32, 16))

    def body(*refs):
        recv_refs = refs[:nl]
        w_ref, m_ref, v_ref, g_out, d_out, m_out, v_out = refs[nl:]
        layer = pl.program_id(0)
        for j in range(nl):
            @pl.when(layer == j)
            def _(j=j):
                g = recv_refs[j][0].astype(F32)
                for s in range(1, N_DEV):
                    g = g + recv_refs[j][s].astype(F32)
                delta, mn, vn = _adamw_math(w_ref[...], g, m_ref[...], v_ref[...])
                g_out[...] = g
                d_out[...] = delta
                m_out[...] = mn
                v_out[...] = vn

    rspecs = [pl.BlockSpec((N_DEV, tr, C), functools.partial(lambda l, t, j: (0, jnp.where(l == j, t, 0), 0), j=j))
              for j in range(nl)]
    wspec = pl.BlockSpec((None, tr, C), lambda l, t: (l, t, 0))
    return pl.pallas_call(
        body, name=name, grid=(nl, R // tr), in_specs=rspecs + [wspec] * 3, out_specs=[wspec] * 4,
        out_shape=[jax.ShapeDtypeStruct((nl, R, C), F32)] * 4,
        compiler_params=_params(("arbitrary", "arbitrary")),
    )(*recv, w, m, v)


def _adamw_small(g, w, m, v):
    def body(g_ref, w_ref, m_ref, v_ref, d_out, m_out, v_out):
        delta, mn, vn = _adamw_math(w_ref[...], g_ref[...], m_ref[...], v_ref[...])
        d_out[...] = delta
        m_out[...] = mn
        v_out[...] = vn

    return pl.pallas_call(body, name="adamw_small", out_shape=[jax.ShapeDtypeStruct(g.shape, F32)] * 3)(g, w, m, v)


def _pack(arrays, rows):
    flat = jnp.concatenate([a.reshape(-1) for a in arrays])
    return jnp.pad(flat, (0, rows * 128 - flat.shape[0])).reshape(rows, 128)


def _unpack(packed, shapes):
    flat = packed.reshape(-1)
    out, pos = [], 0
    for s in shapes:
        size = int(np.prod(s))
        out.append(flat[pos:pos + size].reshape(s))
        pos += size
    return out


def kernel(x, norm1_g, w_in, qk_norm_g, sink_a, rpb_c, w_br_a, w_br_b, w_br_c, w_o, norm2_g, w_gate_up, w_down, loss_target, m_norm1_g, m_w_in, m_qk_norm_g, m_sink_a, m_rpb_c, m_w_br_a, m_w_br_b, m_w_br_c, m_w_o, m_norm2_g, m_w_gate_up, m_w_down, v_norm1_g, v_w_in, v_qk_norm_g, v_sink_a, v_rpb_c, v_w_br_a, v_w_br_b, v_w_br_c, v_w_o, v_norm2_g, v_w_gate_up, v_w_down):
    nl = w_in.shape[0]
    L, D = x.shape[1], x.shape[2]
    x0 = x.reshape(L, D)
    tgt = loss_target.reshape(L, D)

    big = [w_in, w_br_a, w_br_b, w_br_c, w_o, w_gate_up, w_down]
    kinds = ["col", "col", "col", "col", "row", "col", "row"]
    gathered = _all_gather_weights([w.astype(BF16) for w in big], kinds)
    gw = [g if k == "col" else g.reshape((1,) + g.shape) for g, k in zip(gathered, kinds)]
    gw_in, gw_a, gw_b, gw_c, gw_o, gw_gu, gw_d = gw

    half = HEAD_DIM // 2
    inv_freq = ROPE_THETA ** (-jnp.arange(half, dtype=F32) * 2.0 / HEAD_DIM)
    ang = jnp.arange(L, dtype=F32)[:, None] * inv_freq[None, :]
    cos = jnp.concatenate([jnp.cos(ang), jnp.cos(ang)], axis=-1)
    sin = jnp.concatenate([-jnp.sin(ang), jnp.sin(ang)], axis=-1)
    expand = jnp.asarray(_c_expand_matrix(), BF16)
    expand_t = jnp.asarray(_c_expand_matrix().T, BF16)

    def gain_tables(i):
        g = qk_norm_g[i]
        gq = jnp.concatenate([jnp.tile(g[0][None], (8, 1)), jnp.tile(g[2][None], (12, 1)), jnp.tile(g[4][None], (8, 1))])
        gk = jnp.concatenate([jnp.tile(g[1][None], (2, 1)), jnp.tile(g[3][None], (12, 1)), jnp.tile(g[5][None], (8, 1))])
        return gq.reshape(NQ_CHUNKS, 1, HEAD_DIM), gk.reshape(NK_CHUNKS, 1, HEAD_DIM)

    def bias_table(i):
        rp = jnp.pad(rpb_c[i].reshape(C_HEADS * C_NREL, C_NCOL), ((0, 0), (0, HEAD_DIM - C_NCOL)))
        t = _exact_mm(rp, expand_t, name="c_bias_expand")
        return t.reshape(C_HEADS, C_NREL, GRID_W, GRID_W)

    def sink_table(i):
        return jnp.broadcast_to(sink_a[i][:, None, None], (A_Q_HEADS, 1, HEAD_DIM))

    saved = []
    xi = x0
    for i in range(nl):
        gq, gk = gain_tables(i)
        bias_t = bias_table(i)
        sink = sink_table(i)
        h1 = _rms_fwd(xi, norm1_g[i][None], name="rms1_fwd")
        proj = _mm_nn(h1, gw_in, i, out_dtype=F32, name="proj_fwd")
        qn = _qk_fwd(proj, gq, cos, sin, Q_PIECES, NQ_CHUNKS, Q_ROPE_UPTO, name="qnorm_fwd")
        kn = _qk_fwd(proj, gk, cos, sin, K_PIECES, NK_CHUNKS, K_ROPE_UPTO, name="knorm_fwd")
        oa, lse_a = _band_fwd(qn, kn, proj, dil=1, radius=A_RADIUS, nkv=A_KV_HEADS, group=A_GROUP,
                              q0=0, k0=0, v0=PC_VA, sink=sink, name="attn_a_fwd")
        obs, lbs = [], []
        for g, (window, dil) in enumerate(B_PATTERNS):
            o_g, l_g = _band_fwd(qn, kn, proj, dil=dil, radius=window // (2 * dil), nkv=B_HG, group=1,
                                 q0=8 + g * B_HG, k0=2 + g * B_HG, v0=PC_VB + g * B_HG, name=f"attn_b{g}_fwd")
            obs.append(o_g)
            lbs.append(l_g)
        ob, lse_b = _combine_b(obs, lbs, name="attn_b_combine")
        oc, lse_c = _c_fwd(qn, kn, proj, bias_t, name="attn_c_fwd")
        ta = _mm_nn(oa, gw_a, i, out_dtype=F32, name="br_a_fwd")
        tb = _mm_nn(ob, gw_b, i, out_dtype=F32, name="br_b_fwd")
        tc = _mm_nn(oc, gw_c, i, out_dtype=F32, name="br_c_fwd")
        merged = _gate_fwd(proj, ta, tb, tc, name="gate_fwd")
        x1 = _mm_nn(merged, gw_o, i, out_dtype=F32, name="wo_fwd", res=xi)
        h2 = _rms_fwd(x1, norm2_g[i][None], name="rms2_fwd")
        gu = _mm_nn(h2, gw_gu, i, out_dtype=F32, name="gate_up_fwd")
        act = _swiglu_fwd(gu, name="swiglu_fwd")
        x2 = _mm_nn(act, gw_d, i, out_dtype=F32, name="down_fwd", res=x1)
        saved.append(dict(x=xi, h1=h1, proj=proj, qn=qn, kn=kn, oa=oa, lse_a=lse_a, ob=ob, lse_b=lse_b, oc=oc,
                          lse_c=lse_c, ta=ta, tb=tb, tc=tc, merged=merged, x1=x1, h2=h2, gu=gu, act=act,
                          gq=gq, gk=gk, bias_t=bias_t, sink=sink))
        xi = x2

    dx, dxb, loss_row = _loss(xi, tgt, name="loss")

    small_grads = [None] * nl
    recv = [None] * nl
    for i in reversed(range(nl)):
        s = saved[i]
        dact = _mm_nt(dxb, gw_d, i, out_dtype=F32, name="down_bwd_x")
        g_down = _mm_tn(s["act"], dxb, 1, name="down_bwd_w")
        dgu = _swiglu_bwd(s["gu"], dact, name="swiglu_bwd")
        g_gu = _mm_tn(s["h2"], dgu, N_DEV, name="gate_up_bwd_w")
        dh2 = _mm_nt(dgu, gw_gu, i, out_dtype=F32, name="gate_up_bwd_x")
        dx1, dx1b, dg2 = _rms_bwd(s["x1"], norm2_g[i][None], dh2, dx, name="rms2_bwd")
        dmerged = _mm_nt(dx1b, gw_o, i, out_dtype=F32, name="wo_bwd_x")
        g_o = _mm_tn(s["merged"], dx1b, 1, name="wo_bwd_w")
        dta, dtb, dtc, dproj = _gate_bwd(s["proj"], s["ta"], s["tb"], s["tc"], dmerged, name="gate_bwd")
        g_a = _mm_tn(s["oa"], dta, N_DEV, name="br_a_bwd_w")
        g_b = _mm_tn(s["ob"], dtb, N_DEV, name="br_b_bwd_w")
        g_c = _mm_tn(s["oc"], dtc, N_DEV, name="br_c_bwd_w")
        doa = _mm_nt(dta, gw_a, i, out_dtype=F32, name="br_a_bwd_x")
        dob = _mm_nt(dtb, gw_b, i, out_dtype=F32, name="br_b_bwd_x")
        doc = _mm_nt(dtc, gw_c, i, out_dtype=F32, name="br_c_bwd_x")
        dq_buf = lax.empty((L, NQ_CHUNKS * HEAD_DIM), F32)
        dk_buf = lax.empty((L, NK_CHUNKS * HEAD_DIM), F32)
        dv_buf = lax.empty((L, NK_CHUNKS * HEAD_DIM), F32)
        dq_buf, dk_buf, dv_buf, dsink = _band_bwd(
            s["qn"], s["kn"], s["proj"], doa, s["oa"], s["lse_a"], dq_buf, dk_buf, dv_buf, dil=1, radius=A_RADIUS,
            nkv=A_KV_HEADS, group=A_GROUP, q0=0, k0=0, v0=PC_VA, o0=0, sink=s["sink"], name="attn_a_bwd")
        for g, (window, dil) in enumerate(B_PATTERNS):
            dq_buf, dk_buf, dv_buf = _band_bwd(
                s["qn"], s["kn"], s["proj"], dob, s["ob"], s["lse_b"], dq_buf, dk_buf, dv_buf, dil=dil,
                radius=window // (2 * dil), nkv=B_HG, group=1, q0=8 + g * B_HG, k0=2 + g * B_HG,
                v0=PC_VB + g * B_HG, o0=0, name=f"attn_b{g}_bwd")
        dq_buf, dk_buf, dv_buf, dbias_t = _c_bwd(s["qn"], s["kn"], s["proj"], s["bias_t"], doc, s["oc"], s["lse_c"],
                                                 dq_buf, dk_buf, dv_buf, name="attn_c_bwd")
        dproj, dgq = _qk_bwd(dq_buf, s["proj"], s["gq"], cos, sin, dproj, Q_PIECES, NQ_CHUNKS, Q_ROPE_UPTO,
                             name="qnorm_bwd")
        dproj, dgk = _qk_bwd(dk_buf, s["proj"], s["gk"], cos, sin, dproj, K_PIECES, NK_CHUNKS, K_ROPE_UPTO,
                             name="knorm_bwd")
        dproj = _v_bwd(dv_buf, dproj, name="v_bwd")
        g_in = _mm_tn(s["h1"], dproj, N_DEV, name="proj_bwd_w")
        dh1 = _mm_nt(dproj, gw_in, i, out_dtype=F32, name="proj_bwd_x")
        dx, dxb, dg1 = _rms_bwd(s["x"], norm1_g[i][None], dh1, dx1, name="rms1_bwd")

        drpb = _exact_mm(dbias_t.reshape(C_HEADS * C_NREL, GRID_W * GRID_W), expand, name="c_bias_reduce")
        dgq, dgk = dgq.reshape(NQ_CHUNKS, HEAD_DIM), dgk.reshape(NK_CHUNKS, HEAD_DIM)
        dqk_g = jnp.stack([dgq[0:8].sum(0), dgk[0:2].sum(0), dgq[8:20].sum(0), dgk[2:14].sum(0),
                           dgq[20:28].sum(0), dgk[14:22].sum(0)])
        small_grads[i] = (dg1.reshape(D), dqk_g, dsink[:, 0, 0],
                          drpb[:, :C_NCOL].reshape(C_HEADS, C_NREL, C_NCOL), dg2.reshape(D))
        recv[i] = _reduce_scatter_send([g_in, g_a, g_b, g_c, g_o, g_gu, g_down], kinds)

    small_names = [norm1_g, qk_norm_g, sink_a, rpb_c, norm2_g]
    small_m = [m_norm1_g, m_qk_norm_g, m_sink_a, m_rpb_c, m_norm2_g]
    small_v = [v_norm1_g, v_qk_norm_g, v_sink_a, v_rpb_c, v_norm2_g]
    shapes = [a.shape for a in small_names]
    total = sum(int(np.prod(sh)) for sh in shapes) + 128
    rows = -(-total // 1024) * 8
    stacked = [jnp.stack([small_grads[i][j] for i in range(nl)]) for j in range(5)]
    packed = _pack([loss_row.reshape(-1)] + stacked, rows)
    summed = _all_reduce_small(packed)
    loss = summed[0, 0]
    zero_row = jnp.zeros((128,), F32)
    d_s, m_s, v_s = _adamw_small(summed, _pack([zero_row] + small_names, rows), _pack([zero_row] + small_m, rows),
                                 _pack([zero_row] + small_v, rows))
    shapes1 = [(128,)] + shapes
    g_small = _unpack(summed, shapes1)[1:]
    d_small = _unpack(d_s, shapes1)[1:]
    m_small = _unpack(m_s, shapes1)[1:]
    v_small = _unpack(v_s, shapes1)[1:]

    big_m = [m_w_in, m_w_br_a, m_w_br_b, m_w_br_c, m_w_o, m_w_gate_up, m_w_down]
    big_v = [v_w_in, v_w_br_a, v_w_br_b, v_w_br_c, v_w_o, v_w_gate_up, v_w_down]
    big_names = ["w_in", "w_br_a", "w_br_b", "w_br_c", "w_o", "w_gate_up", "w_down"]
    big_out = [_adamw_reduce([recv[i][j] for i in range(nl)], big[j], big_m[j], big_v[j], name="adamw_" + big_names[j])
               for j in range(len(big))]

    order = ["norm1_g", "w_in", "qk_norm_g", "sink_a", "rpb_c", "w_br_a", "w_br_b", "w_br_c", "w_o", "norm2_g",
             "w_gate_up", "w_down"]
    small_idx = {"norm1_g": 0, "qk_norm_g": 1, "sink_a": 2, "rpb_c": 3, "norm2_g": 4}
    big_idx = {n: j for j, n in enumerate(big_names)}

    def pick(kind):
        out = []
        for n in order:
            if n in small_idx:
                out.append([g_small, d_small, m_small, v_small][kind][small_idx[n]])
            else:
                out.append(big_out[big_idx[n]][kind])
        return out

    return (loss, dx.reshape(1, L, D), *pick(0), *pick(1), *pick(2), *pick(3))
```

```python
import functools
import math

import numpy as np
import jax
import jax.numpy as jnp
from jax import lax
from jax.experimental import pallas as pl
from jax.experimental.pallas import tpu as pltpu

F32 = jnp.float32
BF16 = jnp.bfloat16
MESH = pl.DeviceIdType.MESH
N_DEV = 8

HEAD_DIM = 128
NORM_EPS = 1e-6
ROPE_THETA = 10000.0
ATT_SCALE = HEAD_DIM ** -0.5
NEG = -1e30

A_Q_HEADS, A_KV_HEADS, A_RADIUS = 8, 2, 128
A_GROUP = A_Q_HEADS // A_KV_HEADS
B_PATTERNS = ((128, 1), (512, 4), (2048, 16))
B_HG = 4
B_HEADS = len(B_PATTERNS) * B_HG
C_HEADS, GRID_W, C_WIN_ROWS, C_WIN_COLS = 8, 64, 8, 16
C_NREL = 2 * C_WIN_ROWS - 1
C_NCOL = 2 * C_WIN_COLS - 1

PC_QA, PC_KA, PC_VA = 0, 8, 10
PC_QB, PC_KB, PC_VB = 12, 24, 36
PC_QC, PC_KC, PC_VC = 48, 56, 64
N_QKV_CHUNKS = 72
Q_PIECES = ((0, 8, PC_QA), (8, 12, PC_QB), (20, 8, PC_QC))
K_PIECES = ((0, 2, PC_KA), (2, 12, PC_KB), (14, 8, PC_KC))
V_PIECES = ((0, 2, PC_VA), (2, 12, PC_VB), (14, 8, PC_VC))
NQ_CHUNKS, NK_CHUNKS = 28, 22
Q_ROPE_UPTO, K_ROPE_UPTO = 20, 14

ADAM_LR, ADAM_B1, ADAM_B2, ADAM_EPS, ADAM_WD, ADAM_STEP = 0.001, 0.9, 0.999, 1e-08, 0.01, 10

VMEM_LIMIT = 48 * 1024 * 1024


def _tile(dim, prefs):
    for p in prefs:
        if dim % p == 0:
            return p
    return dim


def _params(sem, **kw):
    return pltpu.CompilerParams(dimension_semantics=sem, vmem_limit_bytes=VMEM_LIMIT, **kw)


def _piece_map(pieces):
    def f(c):
        out = c - pieces[0][0] + pieces[0][2]
        for first, _, pfirst in pieces[1:]:
            out = jnp.where(c >= first, c - first + pfirst, out)
        return out
    return f


def _mm_nn(a, w, layer, *, out_dtype, name, res=None):
    M, K = a.shape
    nb, _, Kw, ns = w.shape
    assert Kw == K
    tm = _tile(M, (512, 256))
    tn = _tile(ns, (640, 512, 256, 128)) if ns > 1408 else ns
    tk = _tile(K, (2048, 1408, 1024, 512, 256))
    nj, nk = ns // tn, K // tk

    def body(*refs):
        if res is None:
            a_ref, w_ref, o_ref, acc_ref = refs
            r_ref = None
        else:
            a_ref, w_ref, r_ref, o_ref, acc_ref = refs
        k = pl.program_id(3)
        part = jnp.dot(a_ref[...].astype(BF16), w_ref[...], preferred_element_type=F32)

        @pl.when(k == 0)
        def _():
            acc_ref[...] = part

        @pl.when(k > 0)
        def _():
            acc_ref[...] += part

        @pl.when(k == nk - 1)
        def _():
            r = acc_ref[...]
            if r_ref is not None:
                r = r + r_ref[...]
            o_ref[...] = r.astype(out_dtype)

    in_specs = [pl.BlockSpec((tm, tk), lambda i, b, j, k: (i, k)),
                pl.BlockSpec((None, None, tk, tn), lambda i, b, j, k: (b, layer, k, j))]
    args = [a, w]
    if res is not None:
        in_specs.append(pl.BlockSpec((tm, tn), lambda i, b, j, k: (i, b * nj + j)))
        args.append(res)
    return pl.pallas_call(
        body, name=name, grid=(M // tm, nb, nj, nk), in_specs=in_specs,
        out_specs=pl.BlockSpec((tm, tn), lambda i, b, j, k: (i, b * nj + j)),
        out_shape=jax.ShapeDtypeStruct((M, nb * ns), out_dtype),
        scratch_shapes=[pltpu.VMEM((tm, tn), F32)],
        compiler_params=_params(("parallel", "parallel", "parallel", "arbitrary")),
    )(*args)


def _mm_nt(a, w, layer, *, out_dtype, name, dep=None):
    M, N = a.shape
    nb, _, K, ns = w.shape
    assert N == nb * ns
    tm = _tile(M, (1024, 512, 256))
    tn = _tile(ns, (640, 512, 256, 128)) if ns > 1408 else ns
    tk = _tile(K, (1024, 512, 256))
    nj = ns // tn
    nred = nb * nj

    def body(*refs):
        a_ref, w_ref = refs[:2]
        o_ref, acc_ref = refs[-2:]
        s = pl.program_id(2) * nj + pl.program_id(3)
        part = lax.dot_general(a_ref[...].astype(BF16), w_ref[...], (((1,), (1,)), ((), ())),
                               preferred_element_type=F32)

        @pl.when(s == 0)
        def _():
            acc_ref[...] = part

        @pl.when(s > 0)
        def _():
            acc_ref[...] += part

        @pl.when(s == nred - 1)
        def _():
            o_ref[...] = acc_ref[...].astype(out_dtype)

    in_specs = [pl.BlockSpec((tm, tn), lambda i, kk, b, j: (i, b * nj + j)),
                pl.BlockSpec((None, None, tk, tn), lambda i, kk, b, j: (b, layer, kk, j))]
    args = [a, w]
    if dep is not None:
        in_specs.append(ANY_SPEC)
        args.append(dep)
    return pl.pallas_call(
        body, name=name, grid=(M // tm, K // tk, nb, nj), in_specs=in_specs,
        out_specs=pl.BlockSpec((tm, tk), lambda i, kk, b, j: (i, kk)),
        out_shape=jax.ShapeDtypeStruct((M, K), out_dtype),
        scratch_shapes=[pltpu.VMEM((tm, tk), F32)],
        compiler_params=_params(("parallel", "parallel", "arbitrary", "arbitrary")),
    )(*args)


def _mm_tn(a, g, nb, *, name):
    M, Ka = a.shape
    N = g.shape[1]
    ns = N // nb
    tka = _tile(Ka, (1024, 512, 256))
    tn = _tile(ns, (640, 512, 256, 128)) if ns > 1408 else ns
    tm = _tile(M, (1024, 512, 256))
    nj, nm = ns // tn, M // tm

    def body(a_ref, g_ref, o_ref, acc_ref):
        m = pl.program_id(3)
        part = lax.dot_general(a_ref[...].astype(BF16), g_ref[...].astype(BF16), (((0,), (0,)), ((), ())),
                               preferred_element_type=F32)

        @pl.when(m == 0)
        def _():
            acc_ref[...] = part

        @pl.when(m > 0)
        def _():
            acc_ref[...] += part

        @pl.when(m == nm - 1)
        def _():
            o_ref[...] = acc_ref[...].astype(BF16)

    return pl.pallas_call(
        body, name=name, grid=(Ka // tka, nb, nj, nm),
        in_specs=[pl.BlockSpec((tm, tka), lambda ka, b, j, m: (m, ka)),
                  pl.BlockSpec((tm, tn), lambda ka, b, j, m: (m, b * nj + j))],
        out_specs=pl.BlockSpec((None, tka, tn), lambda ka, b, j, m: (b, ka, j)),
        out_shape=jax.ShapeDtypeStruct((nb, Ka, ns), BF16),
        scratch_shapes=[pltpu.VMEM((tka, tn), F32)],
        compiler_params=_params(("parallel", "parallel", "parallel", "arbitrary")),
    )(a, g)


def _exact_mm(a, e, *, name):
    R, K = a.shape
    N = e.shape[1]

    def body(a_ref, e_ref, o_ref):
        x = a_ref[...]
        hi = x.astype(BF16)
        r1 = x - hi.astype(F32)
        mid = r1.astype(BF16)
        lo = (r1 - mid.astype(F32)).astype(BF16)
        ev = e_ref[...]
        o_ref[...] = (jnp.dot(hi, ev, preferred_element_type=F32) + jnp.dot(mid, ev, preferred_element_type=F32)
                      + jnp.dot(lo, ev, preferred_element_type=F32))

    return pl.pallas_call(body, name=name, out_shape=jax.ShapeDtypeStruct((R, N), F32),
                          compiler_params=pltpu.CompilerParams(vmem_limit_bytes=VMEM_LIMIT))(a, e)


def _rms_fwd(x, g, dep, *, name):
    L, D = x.shape
    tl = _tile(L, (256, 128))

    def body(x_ref, g_ref, _dep, h_ref):
        xv = x_ref[...]
        rstd = lax.rsqrt(jnp.mean(xv * xv, axis=-1, keepdims=True) + NORM_EPS)
        h_ref[...] = (xv * rstd * g_ref[...]).astype(BF16)

    return pl.pallas_call(
        body, name=name, grid=(L // tl,),
        in_specs=[pl.BlockSpec((tl, D), lambda t: (t, 0)), pl.BlockSpec((1, D), lambda t: (0, 0)), ANY_SPEC],
        out_specs=pl.BlockSpec((tl, D), lambda t: (t, 0)),
        out_shape=jax.ShapeDtypeStruct((L, D), BF16),
        compiler_params=_params(("parallel",)),
    )(x, g, dep)


def _rms_bwd(x, g, dy, dres, *, name):
    L, D = x.shape
    tl = _tile(L, (128,))

    def body(x_ref, g_ref, dy_ref, dres_ref, dx_ref, dxb_ref, dg_ref):
        t = pl.program_id(0)
        xv = x_ref[...]
        rstd = lax.rsqrt(jnp.mean(xv * xv, axis=-1, keepdims=True) + NORM_EPS)
        xhat = xv * rstd
        dyv = dy_ref[...]
        dxhat = dyv * g_ref[...]
        c = jnp.mean(dxhat * xhat, axis=-1, keepdims=True)
        dx = dres_ref[...] + rstd * (dxhat - xhat * c)
        dx_ref[...] = dx
        dxb_ref[...] = dx.astype(BF16)
        dgp = jnp.sum(dyv * xhat, axis=0, keepdims=True)

        @pl.when(t == 0)
        def _():
            dg_ref[...] = dgp

        @pl.when(t > 0)
        def _():
            dg_ref[...] += dgp

    row = pl.BlockSpec((tl, D), lambda t: (t, 0))
    vec = pl.BlockSpec((1, D), lambda t: (0, 0))
    return pl.pallas_call(
        body, name=name, grid=(L // tl,), in_specs=[row, vec, row, row], out_specs=[row, row, vec],
        out_shape=[jax.ShapeDtypeStruct((L, D), F32), jax.ShapeDtypeStruct((L, D), BF16),
                   jax.ShapeDtypeStruct((1, D), F32)],
        compiler_params=_params(("arbitrary",)),
    )(x, g, dy, dres)


def _gate_fwd(proj, ta, tb, tc, *, name):
    L, D = ta.shape
    tl, tcw = _tile(L, (256, 128)), _tile(D, (512, 256, 128))
    off = N_QKV_CHUNKS * HEAD_DIM // tcw
    nd = D // tcw

    def body(g0, g1, g2, a_ref, b_ref, c_ref, o_ref):
        m = (jax.nn.sigmoid(g0[...]) * a_ref[...] + jax.nn.sigmoid(g1[...]) * b_ref[...]
             + jax.nn.sigmoid(g2[...]) * c_ref[...])
        o_ref[...] = m.astype(BF16)

    blk = pl.BlockSpec((tl, tcw), lambda t, j: (t, j))
    gl = [pl.BlockSpec((tl, tcw), functools.partial(lambda t, j, i: (t, off + i * nd + j), i=i)) for i in range(3)]
    return pl.pallas_call(
        body, name=name, grid=(L // tl, nd), in_specs=gl + [blk, blk, blk], out_specs=blk,
        out_shape=jax.ShapeDtypeStruct((L, D), BF16),
        compiler_params=_params(("parallel", "parallel")),
    )(proj, proj, proj, ta, tb, tc)


def _gate_bwd(proj, ta, tb, tc, dmerged, *, name):
    L, D = ta.shape
    ncols = proj.shape[1]
    tl, tcw = _tile(L, (256, 128)), _tile(D, (512, 256, 128))
    off = N_QKV_CHUNKS * HEAD_DIM // tcw
    nd = D // tcw

    def body(g0, g1, g2, a_ref, b_ref, c_ref, dm_ref, da_ref, db_ref, dc_ref, dgl_ref):
        i = pl.program_id(2)
        sg = jax.nn.sigmoid(jnp.where(i == 0, g0[...], jnp.where(i == 1, g1[...], g2[...])))
        sel_t = jnp.where(i == 0, a_ref[...], jnp.where(i == 1, b_ref[...], c_ref[...]))
        dt = dm_ref[...] * sg
        dtb = dt.astype(BF16)

        @pl.when(i == 0)
        def _():
            da_ref[...] = dtb

        @pl.when(i == 1)
        def _():
            db_ref[...] = dtb

        @pl.when(i == 2)
        def _():
            dc_ref[...] = dtb

        dgl_ref[...] = (dt * sel_t * (1.0 - sg)).astype(BF16)

    blk = pl.BlockSpec((tl, tcw), lambda t, j, i: (t, j))
    gl = [pl.BlockSpec((tl, tcw), functools.partial(lambda t, j, i, q: (t, off + q * nd + j), q=q)) for q in range(3)]
    return pl.pallas_call(
        body, name=name, grid=(L // tl, nd, 3), in_specs=gl + [blk, blk, blk, blk],
        out_specs=[blk, blk, blk, pl.BlockSpec((tl, tcw), lambda t, j, i: (t, off + i * nd + j))],
        out_shape=[jax.ShapeDtypeStruct((L, D), BF16)] * 3 + [jax.ShapeDtypeStruct((L, ncols), BF16)],
        compiler_params=_params(("parallel", "parallel", "arbitrary")),
    )(proj, proj, proj, ta, tb, tc, dmerged)


def _swiglu_fwd(gu, *, name):
    L, F2 = gu.shape
    F = F2 // 2
    tl, tcw = _tile(L, (256, 128)), _tile(F, (512, 256, 128))
    nf = F // tcw

    def body(g_ref, u_ref, o_ref):
        gt = g_ref[...]
        o_ref[...] = (gt * jax.nn.sigmoid(gt) * u_ref[...]).astype(BF16)

    return pl.pallas_call(
        body, name=name, grid=(L // tl, nf),
        in_specs=[pl.BlockSpec((tl, tcw), lambda t, j: (t, j)), pl.BlockSpec((tl, tcw), lambda t, j: (t, nf + j))],
        out_specs=pl.BlockSpec((tl, tcw), lambda t, j: (t, j)),
        out_shape=jax.ShapeDtypeStruct((L, F), BF16),
        compiler_params=_params(("parallel", "parallel")),
    )(gu, gu)


def _swiglu_bwd(gu, dact, *, name):
    L, F2 = gu.shape
    F = F2 // 2
    tl, tcw = _tile(L, (256, 128)), _tile(F, (512, 256, 128))
    nf = F // tcw

    def body(g_ref, u_ref, d_ref, o_ref):
        half = pl.program_id(1)
        gt, d = g_ref[...], d_ref[...]
        sg = jax.nn.sigmoid(gt)
        dgt = d * u_ref[...] * sg * (1.0 + gt * (1.0 - sg))
        dup = d * gt * sg
        o_ref[...] = jnp.where(half == 0, dgt, dup).astype(BF16)

    lo = pl.BlockSpec((tl, tcw), lambda t, h, j: (t, j))
    hi = pl.BlockSpec((tl, tcw), lambda t, h, j: (t, nf + j))
    return pl.pallas_call(
        body, name=name, grid=(L // tl, 2, nf), in_specs=[lo, hi, lo],
        out_specs=pl.BlockSpec((tl, tcw), lambda t, h, j: (t, h * nf + j)),
        out_shape=jax.ShapeDtypeStruct((L, F2), BF16),
        compiler_params=_params(("parallel", "arbitrary", "arbitrary")),
    )(gu, gu, dact)


def _loss(y, tgt, *, name):
    L, D = y.shape
    tl = _tile(L, (256, 128))
    nt = L // tl

    def body(y_ref, t_ref, dy_ref, dyb_ref, loss_ref, acc_ref):
        t = pl.program_id(0)
        e = y_ref[...] - t_ref[...]
        dy = e * (1.0 / D)
        dy_ref[...] = dy
        dyb_ref[...] = dy.astype(BF16)
        part = jnp.sum(e * e, axis=0, keepdims=True)

        @pl.when(t == 0)
        def _():
            acc_ref[...] = part

        @pl.when(t > 0)
        def _():
            acc_ref[...] += part

        @pl.when(t == nt - 1)
        def _():
            loss_ref[...] = jnp.broadcast_to(jnp.sum(acc_ref[...], axis=-1, keepdims=True) * (0.5 / D), (1, 128))

    row = pl.BlockSpec((tl, D), lambda t: (t, 0))
    return pl.pallas_call(
        body, name=name, grid=(nt,), in_specs=[row, row],
        out_specs=[row, row, pl.BlockSpec((1, 128), lambda t: (0, 0))],
        out_shape=[jax.ShapeDtypeStruct((L, D), F32), jax.ShapeDtypeStruct((L, D), BF16),
                   jax.ShapeDtypeStruct((1, 128), F32)],
        scratch_shapes=[pltpu.VMEM((1, D), F32)],
        compiler_params=_params(("arbitrary",)),
    )(y, tgt)


def _rope(v, cos, sin_signed):
    return v * cos + pltpu.roll(v, HEAD_DIM // 2, 1) * sin_signed


def _qk_fwd(proj, gtab, cos, sin, pieces, nchunks, rope_upto, *, name):
    L = proj.shape[0]
    tl = _tile(L, (512, 256, 128))
    pmap = _piece_map(pieces)

    def body(p_ref, g_ref, cos_ref, sin_ref, o_ref):
        c = pl.program_id(1)
        x = p_ref[...]
        rstd = lax.rsqrt(jnp.mean(x * x, axis=-1, keepdims=True) + NORM_EPS)
        y = x * rstd * g_ref[...]

        @pl.when(c < rope_upto)
        def _():
            o_ref[...] = _rope(y, cos_ref[...], sin_ref[...])

        @pl.when(c >= rope_upto)
        def _():
            o_ref[...] = y

    pos = pl.BlockSpec((tl, HEAD_DIM), lambda t, c: (t, 0))
    return pl.pallas_call(
        body, name=name, grid=(L // tl, nchunks),
        in_specs=[pl.BlockSpec((tl, HEAD_DIM), lambda t, c: (t, pmap(c))),
                  pl.BlockSpec((None, 1, HEAD_DIM), lambda t, c: (c, 0, 0)), pos, pos],
        out_specs=pl.BlockSpec((tl, HEAD_DIM), lambda t, c: (t, c)),
        out_shape=jax.ShapeDtypeStruct((L, nchunks * HEAD_DIM), F32),
        compiler_params=_params(("parallel", "parallel")),
    )(proj, gtab, cos, sin)


def _qk_bwd(dqk, proj, gtab, cos, sin, dproj, pieces, nchunks, rope_upto, *, name):
    L = proj.shape[0]
    tl = _tile(L, (512, 256, 128))
    pmap = _piece_map(pieces)

    def body(d_ref, p_ref, g_ref, cos_ref, sin_ref, _, o_ref, dg_ref):
        c, t = pl.program_id(0), pl.program_id(1)
        x = p_ref[...]
        rstd = lax.rsqrt(jnp.mean(x * x, axis=-1, keepdims=True) + NORM_EPS)
        xhat = x * rstd
        dy = d_ref[...]
        dy = jnp.where(c < rope_upto, _rope(dy, cos_ref[...], -sin_ref[...]), dy)
        dxhat = dy * g_ref[...]
        cm = jnp.mean(dxhat * xhat, axis=-1, keepdims=True)
        o_ref[...] = (rstd * (dxhat - xhat * cm)).astype(BF16)
        dgp = jnp.sum(dy * xhat, axis=0, keepdims=True)

        @pl.when(t == 0)
        def _():
            dg_ref[...] = dgp

        @pl.when(t > 0)
        def _():
            dg_ref[...] += dgp

    pos = pl.BlockSpec((tl, HEAD_DIM), lambda c, t: (t, 0))
    gspec = pl.BlockSpec((None, 1, HEAD_DIM), lambda c, t: (c, 0, 0))
    out, dg = pl.pallas_call(
        body, name=name, grid=(nchunks, L // tl),
        in_specs=[pl.BlockSpec((tl, HEAD_DIM), lambda c, t: (t, c)),
                  pl.BlockSpec((tl, HEAD_DIM), lambda c, t: (t, pmap(c))), gspec, pos, pos,
                  pl.BlockSpec(memory_space=pl.ANY)],
        out_specs=[pl.BlockSpec((tl, HEAD_DIM), lambda c, t: (t, pmap(c))), gspec],
        out_shape=[jax.ShapeDtypeStruct(dproj.shape, BF16), jax.ShapeDtypeStruct((nchunks, 1, HEAD_DIM), F32)],
        input_output_aliases={5: 0},
        compiler_params=_params(("parallel", "arbitrary")),
    )(dqk, proj, gtab, cos, sin, dproj)
    return out, dg


def _v_bwd(dv, dproj, *, name):
    L = dv.shape[0]
    tl = _tile(L, (512, 256, 128))
    pmap = _piece_map(tuple((a // 2, n // 2, p // 2) for a, n, p in V_PIECES))

    def body(d_ref, _, o_ref):
        o_ref[...] = d_ref[...].astype(BF16)

    return pl.pallas_call(
        body, name=name, grid=(L // tl, NK_CHUNKS // 2),
        in_specs=[pl.BlockSpec((tl, 2 * HEAD_DIM), lambda t, c: (t, c)), pl.BlockSpec(memory_space=pl.ANY)],
        out_specs=pl.BlockSpec((tl, 2 * HEAD_DIM), lambda t, c: (t, pmap(c))),
        out_shape=jax.ShapeDtypeStruct(dproj.shape, BF16),
        input_output_aliases={1: 0},
        compiler_params=_params(("parallel", "parallel")),
    )(dv, dproj)


def _band_geometry(L, dil, radius):
    n = L // dil
    bq = min(128, n)
    width = min(bq + 2 * radius, n)
    return n, bq, width


def _band_rows(dil, r, first, count):
    if dil == 1:
        return pl.ds(pl.multiple_of(first, 8), count)
    return pl.ds(r + first * dil, count, stride=dil)


def _band_mask(i, bq, width, radius, ws):
    qpos = i * bq + lax.broadcasted_iota(jnp.int32, (bq, width), 0)
    kpos = ws + lax.broadcasted_iota(jnp.int32, (bq, width), 1)
    return jnp.abs(kpos - qpos) <= radius


def _band_fwd(qn, kn, proj, *, dil, radius, nkv, group, q0, k0, v0, sink=None, name):
    L = qn.shape[0]
    n, bq, width = _band_geometry(L, dil, radius)
    tq = bq * dil
    nh = nkv * group

    def body(*refs):
        if sink is None:
            q_ref, k_ref, v_ref, o_ref, lse_ref = refs
        else:
            q_ref, k_ref, v_ref, s_ref, o_ref, lse_ref = refs
        i = pl.program_id(2)
        ws = jnp.clip(i * bq - radius, 0, n - width)
        valid = _band_mask(i, bq, width, radius, ws)

        def one(r, carry):
            qrows = _band_rows(dil, r, 0, bq)
            krows = _band_rows(dil, r, ws, width)
            q = q_ref[qrows, :].astype(BF16)
            k = k_ref[krows, :].astype(BF16)
            v = v_ref[krows, :].astype(BF16)
            s = lax.dot_general(q, k, (((1,), (1,)), ((), ())), preferred_element_type=F32) * ATT_SCALE
            s = jnp.where(valid, s, NEG)
            m = jnp.max(s, axis=-1, keepdims=True)
            if sink is not None:
                m = jnp.maximum(m, s_ref[...][:, :1])
            p = jnp.exp(s - m)
            denom = jnp.sum(p, axis=-1, keepdims=True)
            if sink is not None:
                denom = denom + jnp.exp(s_ref[...][:, :1] - m)
            pn = (p / denom).astype(BF16)
            o_ref[qrows, :] = jnp.dot(pn, v, preferred_element_type=F32)
            lse_ref[qrows, :] = jnp.broadcast_to(m + jnp.log(denom), (bq, HEAD_DIM))
            return carry

        if dil == 1:
            one(0, 0)
        else:
            lax.fori_loop(0, dil, one, 0)

    qspec = pl.BlockSpec((tq, HEAD_DIM), lambda hk, g, i: (i, q0 + hk * group + g))
    in_specs = [qspec,
                pl.BlockSpec((L, HEAD_DIM), lambda hk, g, i: (0, k0 + hk)),
                pl.BlockSpec((L, HEAD_DIM), lambda hk, g, i: (0, v0 + hk))]
    args = [qn, kn, proj]
    if sink is not None:
        in_specs.append(pl.BlockSpec((None, 1, HEAD_DIM), lambda hk, g, i: (hk * group + g, 0, 0)))
        args.append(sink)
    ospec = pl.BlockSpec((tq, HEAD_DIM), lambda hk, g, i: (i, hk * group + g))
    return pl.pallas_call(
        body, name=name, grid=(nkv, group, n // bq), in_specs=in_specs, out_specs=[ospec, ospec],
        out_shape=[jax.ShapeDtypeStruct((L, nh * HEAD_DIM), F32)] * 2,
        compiler_params=_params(("parallel", "parallel", "arbitrary")),
    )(*args)


def _band_bwd(qn, kn, proj, do, o, lse, dq_buf, dk_buf, dv_buf, *, dil, radius, nkv, group, q0, k0, v0, o0,
              sink=None, name):
    L = qn.shape[0]
    n, bq, width = _band_geometry(L, dil, radius)
    tq = bq * dil
    nh = nkv * group
    n_in = 6 + (1 if sink is not None else 0)

    def body(*refs):
        q_ref, k_ref, v_ref, do_ref, o_ref, lse_ref = refs[:6]
        s_ref = refs[6] if sink is not None else None
        outs = refs[n_in + 3:]
        dq_ref, dk_ref, dv_ref = outs[:3]
        ds_ref = outs[3] if sink is not None else None
        g, i = pl.program_id(1), pl.program_id(2)
        ws = jnp.clip(i * bq - radius, 0, n - width)
        valid = _band_mask(i, bq, width, radius, ws)

        @pl.when((g == 0) & (i == 0))
        def _():
            dk_ref[...] = jnp.zeros_like(dk_ref)
            dv_ref[...] = jnp.zeros_like(dv_ref)

        if sink is not None:
            @pl.when(i == 0)
            def _():
                ds_ref[...] = jnp.zeros_like(ds_ref)

        def one(r, carry):
            qrows = _band_rows(dil, r, 0, bq)
            krows = _band_rows(dil, r, ws, width)
            q = q_ref[qrows, :].astype(BF16)
            k = k_ref[krows, :].astype(BF16)
            v = v_ref[krows, :].astype(BF16)
            dov = do_ref[qrows, :]
            lse_v = lse_ref[qrows, :][:, :1]
            delta = jnp.sum(dov * o_ref[qrows, :], axis=-1, keepdims=True)
            dob = dov.astype(BF16)
            s = lax.dot_general(q, k, (((1,), (1,)), ((), ())), preferred_element_type=F32) * ATT_SCALE
            p = jnp.where(valid, jnp.exp(s - lse_v), 0.0)
            dp = lax.dot_general(dob, v, (((1,), (1,)), ((), ())), preferred_element_type=F32)
            dsb = (p * (dp - delta)).astype(BF16)
            dq_ref[qrows, :] = jnp.dot(dsb, k, preferred_element_type=F32) * ATT_SCALE
            dk_ref[krows, :] += lax.dot_general(dsb, q, (((0,), (0,)), ((), ())),
                                                preferred_element_type=F32) * ATT_SCALE
            dv_ref[krows, :] += lax.dot_general(p.astype(BF16), dob, (((0,), (0,)), ((), ())),
                                                preferred_element_type=F32)
            if sink is not None:
                ps = jnp.exp(s_ref[...][:, :1] - lse_v)
                ds_ref[...] += jnp.broadcast_to(jnp.sum(-ps * delta, axis=0, keepdims=True), (1, HEAD_DIM))
            return carry

        if dil == 1:
            one(0, 0)
        else:
            lax.fori_loop(0, dil, one, 0)

    hspec = pl.BlockSpec((tq, HEAD_DIM), lambda hk, g, i: (i, o0 + hk * group + g))
    qspec = pl.BlockSpec((tq, HEAD_DIM), lambda hk, g, i: (i, q0 + hk * group + g))
    kspec = pl.BlockSpec((L, HEAD_DIM), lambda hk, g, i: (0, k0 + hk))
    any_spec = pl.BlockSpec(memory_space=pl.ANY)
    in_specs = [qspec, kspec, pl.BlockSpec((L, HEAD_DIM), lambda hk, g, i: (0, v0 + hk)), hspec, hspec, hspec]
    args = [qn, kn, proj, do, o, lse]
    if sink is not None:
        in_specs.append(pl.BlockSpec((None, 1, HEAD_DIM), lambda hk, g, i: (hk * group + g, 0, 0)))
        args.append(sink)
    in_specs += [any_spec] * 3
    args += [dq_buf, dk_buf, dv_buf]
    out_specs = [qspec, kspec, kspec]
    out_shape = [jax.ShapeDtypeStruct(dq_buf.shape, F32), jax.ShapeDtypeStruct(dk_buf.shape, F32),
                 jax.ShapeDtypeStruct(dv_buf.shape, F32)]
    if sink is not None:
        out_specs.append(pl.BlockSpec((None, 1, HEAD_DIM), lambda hk, g, i: (hk * group + g, 0, 0)))
        out_shape.append(jax.ShapeDtypeStruct((nh, 1, HEAD_DIM), F32))
    return pl.pallas_call(
        body, name=name, grid=(nkv, group, n // bq), in_specs=in_specs, out_specs=out_specs, out_shape=out_shape,
        input_output_aliases={n_in: 0, n_in + 1: 1, n_in + 2: 2},
        compiler_params=_params(("parallel", "arbitrary", "arbitrary")),
    )(*args)


def _combine_b(os_, lses, *, name):
    L, W = os_[0].shape
    tl = _tile(L, (256, 128))

    def body(o0, o1, o2, l0, l1, l2, out_ref, lt_ref):
        a, b, c = l0[...], l1[...], l2[...]
        m = jnp.maximum(jnp.maximum(a, b), c)
        ea, eb, ec = jnp.exp(a - m), jnp.exp(b - m), jnp.exp(c - m)
        tot = ea + eb + ec
        out_ref[...] = (ea * o0[...] + eb * o1[...] + ec * o2[...]) / tot
        lt_ref[...] = m + jnp.log(tot)

    blk = pl.BlockSpec((tl, W), lambda t: (t, 0))
    return pl.pallas_call(
        body, name=name, grid=(L // tl,), in_specs=[blk] * 6, out_specs=[blk, blk],
        out_shape=[jax.ShapeDtypeStruct((L, W), F32)] * 2, compiler_params=_params(("parallel",)),
    )(*os_, *lses)


def _c_geometry(L):
    rows = L // GRID_W
    assert rows >= C_WIN_ROWS
    return rows


C_KEYS = C_WIN_ROWS * GRID_W


def _c_row_start(r, rows):
    return jnp.clip(r - C_WIN_ROWS // 2, 0, rows - C_WIN_ROWS)


def _c_bias_tiles(bias_t):
    tiles = jnp.stack([jnp.concatenate([bias_t[:, b + w] for w in range(C_WIN_ROWS)], axis=-1)
                       for b in range(C_WIN_ROWS)], axis=1)
    cq = np.arange(GRID_W)[:, None]
    ck = np.arange(GRID_W)[None, :]
    start = np.clip(cq - C_WIN_COLS // 2, 0, GRID_W - C_WIN_COLS)
    ok = np.tile((ck >= start) & (ck < start + C_WIN_COLS), (1, C_WIN_ROWS))
    return jnp.where(jnp.asarray(ok), tiles, NEG)


def _c_fwd(qn, kn, proj, tiles, *, name):
    L = qn.shape[0]
    rows = _c_geometry(L)

    def body(q_ref, k_ref, v_ref, t_ref, o_ref, lse_ref):
        rs = _c_row_start(pl.program_id(1), rows)
        krows = pl.ds(pl.multiple_of(rs * GRID_W, GRID_W), C_KEYS)
        q = q_ref[...].astype(BF16)
        k = k_ref[krows, :].astype(BF16)
        v = v_ref[krows, :].astype(BF16)
        s = lax.dot_general(q, k, (((1,), (1,)), ((), ())), preferred_element_type=F32) * ATT_SCALE + t_ref[...]
        m = jnp.max(s, axis=-1, keepdims=True)
        p = jnp.exp(s - m)
        denom = jnp.sum(p, axis=-1, keepdims=True)
        o_ref[...] = jnp.dot((p / denom).astype(BF16), v, preferred_element_type=F32)
        lse_ref[...] = jnp.broadcast_to(m + jnp.log(denom), (GRID_W, HEAD_DIM))

    def tile_index(h, r):
        return (h, _c_row_start(r, rows) - r + (C_WIN_ROWS - 1), 0, 0)

    ospec = pl.BlockSpec((GRID_W, HEAD_DIM), lambda h, r: (r, h))
    return pl.pallas_call(
        body, name=name, grid=(C_HEADS, rows),
        in_specs=[pl.BlockSpec((GRID_W, HEAD_DIM), lambda h, r: (r, 20 + h)),
                  pl.BlockSpec((L, HEAD_DIM), lambda h, r: (0, 14 + h)),
                  pl.BlockSpec((L, HEAD_DIM), lambda h, r: (0, PC_VC + h)),
                  pl.BlockSpec((None, None, GRID_W, C_KEYS), tile_index)],
        out_specs=[ospec, ospec],
        out_shape=[jax.ShapeDtypeStruct((L, C_HEADS * HEAD_DIM), F32)] * 2,
        compiler_params=_params(("parallel", "arbitrary")),
    )(qn, kn, proj, tiles)


def _c_bwd(qn, kn, proj, tiles, do, o, lse, dq_buf, dk_buf, dv_buf, *, name):
    L = qn.shape[0]
    rows = _c_geometry(L)

    def body(q_ref, k_ref, v_ref, t_ref, do_ref, o_ref, lse_ref, _a, _b, _c, dq_ref, dk_ref, dv_ref, dt_ref):
        r = pl.program_id(1)
        rs = _c_row_start(r, rows)
        base = rs - r + (C_WIN_ROWS - 1)
        krows = pl.ds(pl.multiple_of(rs * GRID_W, GRID_W), C_KEYS)

        @pl.when(r == 0)
        def _():
            dk_ref[...] = jnp.zeros_like(dk_ref)
            dv_ref[...] = jnp.zeros_like(dv_ref)
            dt_ref[...] = jnp.zeros_like(dt_ref)

        q = q_ref[...].astype(BF16)
        k = k_ref[krows, :].astype(BF16)
        v = v_ref[krows, :].astype(BF16)
        dov = do_ref[...]
        dob = dov.astype(BF16)
        delta = jnp.sum(dov * o_ref[...], axis=-1, keepdims=True)
        s = lax.dot_general(q, k, (((1,), (1,)), ((), ())), preferred_element_type=F32) * ATT_SCALE + t_ref[...]
        p = jnp.exp(s - lse_ref[...][:, :1])
        dp = lax.dot_general(dob, v, (((1,), (1,)), ((), ())), preferred_element_type=F32)
        ds = p * (dp - delta)
        for w in range(C_WIN_ROWS):
            dt_ref[base + w] += ds[:, w * GRID_W:(w + 1) * GRID_W]
        dsb = ds.astype(BF16)
        dq_ref[...] = jnp.dot(dsb, k, preferred_element_type=F32) * ATT_SCALE
        dk_ref[krows, :] += lax.dot_general(dsb, q, (((0,), (0,)), ((), ())), preferred_element_type=F32) * ATT_SCALE
        dv_ref[krows, :] += lax.dot_general(p.astype(BF16), dob, (((0,), (0,)), ((), ())),
                                            preferred_element_type=F32)

    def tile_index(h, r):
        return (h, _c_row_start(r, rows) - r + (C_WIN_ROWS - 1), 0, 0)

    hspec = pl.BlockSpec((GRID_W, HEAD_DIM), lambda h, r: (r, h))
    qspec = pl.BlockSpec((GRID_W, HEAD_DIM), lambda h, r: (r, 20 + h))
    kspec = pl.BlockSpec((L, HEAD_DIM), lambda h, r: (0, 14 + h))
    any_spec = pl.BlockSpec(memory_space=pl.ANY)
    return pl.pallas_call(
        body, name=name, grid=(C_HEADS, rows),
        in_specs=[qspec, kspec, pl.BlockSpec((L, HEAD_DIM), lambda h, r: (0, PC_VC + h)),
                  pl.BlockSpec((None, None, GRID_W, C_KEYS), tile_index),
                  hspec, hspec, hspec, any_spec, any_spec, any_spec],
        out_specs=[qspec, kspec, kspec,
                   pl.BlockSpec((None, C_NREL, GRID_W, GRID_W), lambda h, r: (h, 0, 0, 0))],
        out_shape=[jax.ShapeDtypeStruct(dq_buf.shape, F32), jax.ShapeDtypeStruct(dk_buf.shape, F32),
                   jax.ShapeDtypeStruct(dv_buf.shape, F32),
                   jax.ShapeDtypeStruct((C_HEADS, C_NREL, GRID_W, GRID_W), F32)],
        input_output_aliases={7: 0, 8: 1, 9: 2},
        compiler_params=_params(("parallel", "arbitrary")),
    )(qn, kn, proj, tiles, do, o, lse, dq_buf, dk_buf, dv_buf)


def _c_expand_matrix():
    cq = np.arange(GRID_W)[:, None]
    ck = np.arange(GRID_W)[None, :]
    d = (ck - cq + (C_WIN_COLS - 1)).reshape(-1)
    e = np.zeros((GRID_W * GRID_W, HEAD_DIM), np.float32)
    okd = (d >= 0) & (d < C_NCOL)
    e[np.arange(GRID_W * GRID_W)[okd], d[okd]] = 1.0
    return e


def _peer(p):
    return (p // 4, (p // 2) % 2, p % 2)


def _my_index():
    return 4 * lax.axis_index("x") + 2 * lax.axis_index("y") + lax.axis_index("c")


HBM_SPEC = pl.BlockSpec(memory_space=pltpu.HBM)
SEM_SPEC = pl.BlockSpec(memory_space=pltpu.SEMAPHORE)
ANY_SPEC = pl.BlockSpec(memory_space=pl.ANY)
DATAFLOW = pltpu.SideEffectType.DATAFLOW_SIDE_EFFECTING


def _exchange_views(mode, kinds, srcs):
    rows, lands = [], []
    for s, kind in zip(srcs, kinds):
        if mode == "gather":
            rows.append(s.shape[0])
            lands.append((N_DEV,) + s.shape if kind == "col" else (N_DEV * s.shape[0], s.shape[1]))
        else:
            rows.append(s.shape[1] // N_DEV)
            lands.append(s.shape if kind == "col" else (N_DEV, s.shape[1] // N_DEV, s.shape[2]))

    def src(ref, w, to):
        if mode == "gather":
            return ref
        return ref.at[to] if kinds[w] == "col" else ref.at[0, pl.ds(to * rows[w], rows[w]), :]

    def dst(ref, w, who):
        if mode == "scatter" or kinds[w] == "col":
            return ref.at[who]
        return ref.at[pl.ds(who * rows[w], rows[w]), :]

    return lands, src, dst


def _place_own(mode, srcs, kinds, *, name):
    nw = len(srcs)
    shapes, src, dst = _exchange_views(mode, kinds, srcs)

    def body(*refs):
        s_refs, outs, sems = refs[:nw], refs[nw:2 * nw], refs[2 * nw]
        me = _my_index()
        cps = [pltpu.make_async_copy(src(s_refs[w], w, me), dst(outs[w], w, me), sems.at[w]) for w in range(nw)]
        for cp in cps:
            cp.start()
        for cp in cps:
            cp.wait()

    return pl.pallas_call(
        body, name=name, in_specs=[ANY_SPEC] * nw, out_specs=[ANY_SPEC] * nw,
        out_shape=[jax.ShapeDtypeStruct(sh, s.dtype) for sh, s in zip(shapes, srcs)],
        scratch_shapes=[pltpu.SemaphoreType.DMA((nw,))],
        compiler_params=pltpu.CompilerParams(has_side_effects=True),
    )(*srcs)


def _exchange_start(mode, srcs, lands, kinds, after, *, name):
    nw = len(srcs)
    _, src, dst = _exchange_views(mode, kinds, srcs)

    def body(*refs):
        s_refs, l_refs = refs[:nw], refs[nw:2 * nw]
        send_sems, recv_sems = refs[2 * nw + 1], refs[2 * nw + 2]
        token = refs[-1]
        me = _my_index()
        for off in range(1, N_DEV):
            to = (me + off) % N_DEV
            for w in range(nw):
                pltpu.make_async_remote_copy(src(s_refs[w], w, to), dst(l_refs[w], w, me),
                                             send_sems.at[w * N_DEV + off], recv_sems.at[w * N_DEV + off],
                                             device_id=_peer(to), device_id_type=MESH).start()
        token[...] = jnp.zeros_like(token)

    thru = [pltpu.HBM(a.shape, a.dtype) for a in list(srcs) + list(lands)]
    outs = pl.pallas_call(
        body, name=name,
        out_shape=(pltpu.SemaphoreType.DMA((nw * N_DEV,)), pltpu.SemaphoreType.DMA((nw * N_DEV,)), *thru,
                   jax.ShapeDtypeStruct((8, 128), F32)),
        in_specs=[HBM_SPEC] * (2 * nw) + [ANY_SPEC],
        out_specs=(SEM_SPEC, SEM_SPEC, *([HBM_SPEC] * (2 * nw)), pl.BlockSpec(memory_space=pltpu.VMEM)),
        input_output_aliases={k: 2 + k for k in range(2 * nw)},
        compiler_params=pltpu.CompilerParams(has_side_effects=DATAFLOW),
    )(*[pltpu.with_memory_space_constraint(a, pltpu.HBM) for a in list(srcs) + list(lands)], after)
    return outs[0], outs[1], outs[2:2 + nw], outs[2 + nw:2 + 2 * nw], outs[-1]


def _exchange_wait(mode, started, kinds, after, *, name):
    send_sems, recv_sems, srcs, lands, _ = started
    nw = len(srcs)
    _, src, dst = _exchange_views(mode, kinds, srcs)

    def body(*refs):
        s_refs, l_refs = refs[:nw], refs[nw:2 * nw]
        send_ref, recv_ref = refs[2 * nw], refs[2 * nw + 1]
        me = _my_index()
        for off in range(1, N_DEV):
            to = (me + off) % N_DEV
            frm = (me + N_DEV - off) % N_DEV
            for w in range(nw):
                cp = pltpu.make_async_remote_copy(src(s_refs[w], w, to), dst(l_refs[w], w, frm),
                                                  send_ref.at[w * N_DEV + off], recv_ref.at[w * N_DEV + off],
                                                  device_id=_peer(frm), device_id_type=MESH)
                cp.wait_send()
                cp.wait_recv()

    outs = pl.pallas_call(
        body, name=name, out_shape=[pltpu.HBM(a.shape, a.dtype) for a in list(srcs) + list(lands)],
        in_specs=[HBM_SPEC] * (2 * nw) + [SEM_SPEC, SEM_SPEC, ANY_SPEC], out_specs=[HBM_SPEC] * (2 * nw),
        input_output_aliases={k: k for k in range(2 * nw)},
        compiler_params=pltpu.CompilerParams(has_side_effects=DATAFLOW),
    )(*srcs, *lands, send_sems, recv_sems, after)
    return outs[nw:]


def _all_reduce_small(x):
    R = x.shape[0]

    def body(x_ref, o_ref, gath, send_sems, recv_sems):
        me = _my_index()
        gath[me] = x_ref[...]
        sends = []
        for off in range(1, N_DEV):
            to = (me + off) % N_DEV
            cp = pltpu.make_async_remote_copy(x_ref, gath.at[me], send_sems.at[off], recv_sems.at[off],
                                              device_id=_peer(to), device_id_type=MESH)
            cp.start()
            sends.append(cp)
        for off in range(1, N_DEV):
            frm = (me + N_DEV - off) % N_DEV
            pltpu.make_async_remote_copy(x_ref, gath.at[frm], send_sems.at[off], recv_sems.at[off],
                                         device_id=_peer(frm), device_id_type=MESH).wait_recv()
        for cp in sends:
            cp.wait_send()
        acc = gath[0]
        for s in range(1, N_DEV):
            acc = acc + gath[s]
        o_ref[...] = acc

    vm = pl.BlockSpec(memory_space=pltpu.VMEM)
    return pl.pallas_call(
        body, name="all_reduce_small", in_specs=[vm], out_specs=vm, out_shape=jax.ShapeDtypeStruct((R, 128), F32),
        scratch_shapes=[pltpu.VMEM((N_DEV, R, 128), F32), pltpu.SemaphoreType.DMA((N_DEV,)),
                        pltpu.SemaphoreType.DMA((N_DEV,))],
        compiler_params=pltpu.CompilerParams(has_side_effects=True),
    )(x)


def _adamw_math(w, g, m, v):
    m = ADAM_B1 * m + (1.0 - ADAM_B1) * g
    v = ADAM_B2 * v + (1.0 - ADAM_B2) * (g * g)
    m_hat = m / (1.0 - ADAM_B1 ** ADAM_STEP)
    v_hat = v / (1.0 - ADAM_B2 ** ADAM_STEP)
    delta = -ADAM_LR * (m_hat / (jnp.sqrt(v_hat) + ADAM_EPS) + ADAM_WD * w)
    return delta, m, v


def _adamw_layer(recv, w, m, v, outs, layer, dep, *, name):
    nl, R, C = w.shape
    tr = _tile(R, (128, 64, 32, 16))

    def body(r_ref, w_ref, m_ref, v_ref, _0, _1, _2, _3, _dep, g_out, d_out, m_out, v_out):
        g = r_ref[0].astype(F32)
        for s in range(1, N_DEV):
            g = g + r_ref[s].astype(F32)
        delta, mn, vn = _adamw_math(w_ref[...], g, m_ref[...], v_ref[...])
        g_out[...] = g
        d_out[...] = delta
        m_out[...] = mn
        v_out[...] = vn

    wspec = pl.BlockSpec((None, tr, C), lambda t: (layer, t, 0))
    return pl.pallas_call(
        body, name=name, grid=(R // tr,),
        in_specs=[pl.BlockSpec((N_DEV, tr, C), lambda t: (0, t, 0))] + [wspec] * 3 + [ANY_SPEC] * 5,
        out_specs=[wspec] * 4, out_shape=[jax.ShapeDtypeStruct((nl, R, C), F32)] * 4,
        input_output_aliases={4: 0, 5: 1, 6: 2, 7: 3},
        compiler_params=_params(("parallel",)),
    )(recv, w, m, v, *outs, dep)


def _adamw_small(g, w, m, v):
    def body(g_ref, w_ref, m_ref, v_ref, d_out, m_out, v_out):
        delta, mn, vn = _adamw_math(w_ref[...], g_ref[...], m_ref[...], v_ref[...])
        d_out[...] = delta
        m_out[...] = mn
        v_out[...] = vn

    return pl.pallas_call(body, name="adamw_small", out_shape=[jax.ShapeDtypeStruct(g.shape, F32)] * 3)(g, w, m, v)


def _pack(arrays, rows):
    flat = jnp.concatenate([a.reshape(-1) for a in arrays])
    return jnp.pad(flat, (0, rows * 128 - flat.shape[0])).reshape(rows, 128)


def _unpack(packed, shapes):
    flat = packed.reshape(-1)
    out, pos = [], 0
    for s in shapes:
        size = int(np.prod(s))
        out.append(flat[pos:pos + size].reshape(s))
        pos += size
    return out


def kernel(x, norm1_g, w_in, qk_norm_g, sink_a, rpb_c, w_br_a, w_br_b, w_br_c, w_o, norm2_g, w_gate_up, w_down, loss_target, m_norm1_g, m_w_in, m_qk_norm_g, m_sink_a, m_rpb_c, m_w_br_a, m_w_br_b, m_w_br_c, m_w_o, m_norm2_g, m_w_gate_up, m_w_down, v_norm1_g, v_w_in, v_qk_norm_g, v_sink_a, v_rpb_c, v_w_br_a, v_w_br_b, v_w_br_c, v_w_o, v_norm2_g, v_w_gate_up, v_w_down):
    nl = w_in.shape[0]
    L, D = x.shape[1], x.shape[2]
    x0 = x.reshape(L, D)
    tgt = loss_target.reshape(L, D)

    big = [w_in, w_br_a, w_br_b, w_br_c, w_o, w_gate_up, w_down]
    kinds = ["col", "col", "col", "col", "row", "col", "row"]

    def gather_start(i, after):
        shards = [w[i].astype(BF16) for w in big]
        lands = _place_own("gather", shards, kinds, name="gather_place")
        return _exchange_start("gather", shards, lands, kinds, after, name="gather_start")

    def gather_wait(started, after):
        lands = _exchange_wait("gather", started, kinds, after, name="gather_wait")
        return [g.reshape((N_DEV, 1) + g.shape[1:]) if k == "col" else g.reshape((1, 1) + g.shape)
                for g, k in zip(lands, kinds)]

    half = HEAD_DIM // 2
    inv_freq = ROPE_THETA ** (-jnp.arange(half, dtype=F32) * 2.0 / HEAD_DIM)
    ang = jnp.arange(L, dtype=F32)[:, None] * inv_freq[None, :]
    cos = jnp.concatenate([jnp.cos(ang), jnp.cos(ang)], axis=-1)
    sin = jnp.concatenate([-jnp.sin(ang), jnp.sin(ang)], axis=-1)
    expand = jnp.asarray(_c_expand_matrix(), BF16)
    expand_t = jnp.asarray(_c_expand_matrix().T, BF16)

    def gain_tables(i):
        g = qk_norm_g[i]
        gq = jnp.concatenate([jnp.tile(g[0][None], (8, 1)), jnp.tile(g[2][None], (12, 1)), jnp.tile(g[4][None], (8, 1))])
        gk = jnp.concatenate([jnp.tile(g[1][None], (2, 1)), jnp.tile(g[3][None], (12, 1)), jnp.tile(g[5][None], (8, 1))])
        return gq.reshape(NQ_CHUNKS, 1, HEAD_DIM), gk.reshape(NK_CHUNKS, 1, HEAD_DIM)

    def bias_table(i):
        rp = jnp.pad(rpb_c[i].reshape(C_HEADS * C_NREL, C_NCOL), ((0, 0), (0, HEAD_DIM - C_NCOL)))
        t = _exact_mm(rp, expand_t, name="c_bias_expand")
        return _c_bias_tiles(t.reshape(C_HEADS, C_NREL, GRID_W, GRID_W))

    def sink_table(i):
        return jnp.broadcast_to(sink_a[i][:, None, None], (A_Q_HEADS, 1, HEAD_DIM))

    saved = []
    gws = [None] * nl
    xi = x0
    started = gather_start(0, x0)
    gws[0] = gather_wait(started, x0)
    for i in range(nl):
        dep = xi
        if i + 1 < nl:
            started = gather_start(i + 1, gws[i][0])
            dep = started[4]
        gw_in, gw_a, gw_b, gw_c, gw_o, gw_gu, gw_d = gws[i]
        gq, gk = gain_tables(i)
        bias_t = bias_table(i)
        sink = sink_table(i)
        h1 = _rms_fwd(xi, norm1_g[i][None], dep, name="rms1_fwd")
        proj = _mm_nn(h1, gw_in, 0, out_dtype=F32, name="proj_fwd")
        qn = _qk_fwd(proj, gq, cos, sin, Q_PIECES, NQ_CHUNKS, Q_ROPE_UPTO, name="qnorm_fwd")
        kn = _qk_fwd(proj, gk, cos, sin, K_PIECES, NK_CHUNKS, K_ROPE_UPTO, name="knorm_fwd")
        oa, lse_a = _band_fwd(qn, kn, proj, dil=1, radius=A_RADIUS, nkv=A_KV_HEADS, group=A_GROUP,
                              q0=0, k0=0, v0=PC_VA, sink=sink, name="attn_a_fwd")
        obs, lbs = [], []
        for g, (window, dil) in enumerate(B_PATTERNS):
            o_g, l_g = _band_fwd(qn, kn, proj, dil=dil, radius=window // (2 * dil), nkv=B_HG, group=1,
                                 q0=8 + g * B_HG, k0=2 + g * B_HG, v0=PC_VB + g * B_HG, name=f"attn_b{g}_fwd")
            obs.append(o_g)
            lbs.append(l_g)
        ob, lse_b = _combine_b(obs, lbs, name="attn_b_combine")
        oc, lse_c = _c_fwd(qn, kn, proj, bias_t, name="attn_c_fwd")
        ta = _mm_nn(oa, gw_a, 0, out_dtype=F32, name="br_a_fwd")
        tb = _mm_nn(ob, gw_b, 0, out_dtype=F32, name="br_b_fwd")
        tc = _mm_nn(oc, gw_c, 0, out_dtype=F32, name="br_c_fwd")
        merged = _gate_fwd(proj, ta, tb, tc, name="gate_fwd")
        x1 = _mm_nn(merged, gw_o, 0, out_dtype=F32, name="wo_fwd", res=xi)
        h2 = _rms_fwd(x1, norm2_g[i][None], x1, name="rms2_fwd")
        gu = _mm_nn(h2, gw_gu, 0, out_dtype=F32, name="gate_up_fwd")
        act = _swiglu_fwd(gu, name="swiglu_fwd")
        x2 = _mm_nn(act, gw_d, 0, out_dtype=F32, name="down_fwd", res=x1)
        saved.append(dict(x=xi, h1=h1, proj=proj, qn=qn, kn=kn, oa=oa, lse_a=lse_a, ob=ob, lse_b=lse_b, oc=oc,
                          lse_c=lse_c, ta=ta, tb=tb, tc=tc, merged=merged, x1=x1, h2=h2, gu=gu, act=act,
                          gq=gq, gk=gk, bias_t=bias_t, sink=sink))
        xi = x2
        if i + 1 < nl:
            gws[i + 1] = gather_wait(started, x2)

    dx, dxb, loss_row = _loss(xi, tgt, name="loss")

    def scatter_start(grads, after):
        lands = _place_own("scatter", grads, kinds, name="scatter_place")
        return _exchange_start("scatter", grads, lands, kinds, after, name="scatter_start")

    small_grads = [None] * nl
    recv = [None] * nl
    started = None
    for i in reversed(range(nl)):
        s = saved[i]
        gw_in, gw_a, gw_b, gw_c, gw_o, gw_gu, gw_d = gws[i]
        dact = _mm_nt(dxb, gw_d, 0, out_dtype=F32, name="down_bwd_x", dep=None if started is None else started[4])
        g_down = _mm_tn(s["act"], dxb, 1, name="down_bwd_w")
        dgu = _swiglu_bwd(s["gu"], dact, name="swiglu_bwd")
        g_gu = _mm_tn(s["h2"], dgu, N_DEV, name="gate_up_bwd_w")
        dh2 = _mm_nt(dgu, gw_gu, 0, out_dtype=F32, name="gate_up_bwd_x")
        dx1, dx1b, dg2 = _rms_bwd(s["x1"], norm2_g[i][None], dh2, dx, name="rms2_bwd")
        dmerged = _mm_nt(dx1b, gw_o, 0, out_dtype=F32, name="wo_bwd_x")
        g_o = _mm_tn(s["merged"], dx1b, 1, name="wo_bwd_w")
        dta, dtb, dtc, dproj = _gate_bwd(s["proj"], s["ta"], s["tb"], s["tc"], dmerged, name="gate_bwd")
        g_a = _mm_tn(s["oa"], dta, N_DEV, name="br_a_bwd_w")
        g_b = _mm_tn(s["ob"], dtb, N_DEV, name="br_b_bwd_w")
        g_c = _mm_tn(s["oc"], dtc, N_DEV, name="br_c_bwd_w")
        doa = _mm_nt(dta, gw_a, 0, out_dtype=F32, name="br_a_bwd_x")
        dob = _mm_nt(dtb, gw_b, 0, out_dtype=F32, name="br_b_bwd_x")
        doc = _mm_nt(dtc, gw_c, 0, out_dtype=F32, name="br_c_bwd_x")
        dq_buf = lax.empty((L, NQ_CHUNKS * HEAD_DIM), F32)
        dk_buf = lax.empty((L, NK_CHUNKS * HEAD_DIM), F32)
        dv_buf = lax.empty((L, NK_CHUNKS * HEAD_DIM), F32)
        dq_buf, dk_buf, dv_buf, dsink = _band_bwd(
            s["qn"], s["kn"], s["proj"], doa, s["oa"], s["lse_a"], dq_buf, dk_buf, dv_buf, dil=1, radius=A_RADIUS,
            nkv=A_KV_HEADS, group=A_GROUP, q0=0, k0=0, v0=PC_VA, o0=0, sink=s["sink"], name="attn_a_bwd")
        for g, (window, dil) in enumerate(B_PATTERNS):
            dq_buf, dk_buf, dv_buf = _band_bwd(
                s["qn"], s["kn"], s["proj"], dob, s["ob"], s["lse_b"], dq_buf, dk_buf, dv_buf, dil=dil,
                radius=window // (2 * dil), nkv=B_HG, group=1, q0=8 + g * B_HG, k0=2 + g * B_HG,
                v0=PC_VB + g * B_HG, o0=0, name=f"attn_b{g}_bwd")
        dq_buf, dk_buf, dv_buf, dbias_t = _c_bwd(s["qn"], s["kn"], s["proj"], s["bias_t"], doc, s["oc"], s["lse_c"],
                                                 dq_buf, dk_buf, dv_buf, name="attn_c_bwd")
        dproj, dgq = _qk_bwd(dq_buf, s["proj"], s["gq"], cos, sin, dproj, Q_PIECES, NQ_CHUNKS, Q_ROPE_UPTO,
                             name="qnorm_bwd")
        dproj, dgk = _qk_bwd(dk_buf, s["proj"], s["gk"], cos, sin, dproj, K_PIECES, NK_CHUNKS, K_ROPE_UPTO,
                             name="knorm_bwd")
        dproj = _v_bwd(dv_buf, dproj, name="v_bwd")
        g_in = _mm_tn(s["h1"], dproj, N_DEV, name="proj_bwd_w")
        dh1 = _mm_nt(dproj, gw_in, 0, out_dtype=F32, name="proj_bwd_x")
        dx, dxb, dg1 = _rms_bwd(s["x"], norm1_g[i][None], dh1, dx1, name="rms1_bwd")
        if started is not None:
            recv[i + 1] = _exchange_wait("scatter", started, kinds, dx, name="scatter_wait")
        started = scatter_start([g_in, g_a, g_b, g_c, g_o, g_gu, g_down], dx)

        drpb = _exact_mm(dbias_t.reshape(C_HEADS * C_NREL, GRID_W * GRID_W), expand, name="c_bias_reduce")
        dgq, dgk = dgq.reshape(NQ_CHUNKS, HEAD_DIM), dgk.reshape(NK_CHUNKS, HEAD_DIM)
        dqk_g = jnp.stack([dgq[0:8].sum(0), dgk[0:2].sum(0), dgq[8:20].sum(0), dgk[2:14].sum(0),
                           dgq[20:28].sum(0), dgk[14:22].sum(0)])
        small_grads[i] = (dg1.reshape(D), dqk_g, dsink[:, 0, 0],
                          drpb[:, :C_NCOL].reshape(C_HEADS, C_NREL, C_NCOL), dg2.reshape(D))

    small_names = [norm1_g, qk_norm_g, sink_a, rpb_c, norm2_g]
    small_m = [m_norm1_g, m_qk_norm_g, m_sink_a, m_rpb_c, m_norm2_g]
    small_v = [v_norm1_g, v_qk_norm_g, v_sink_a, v_rpb_c, v_norm2_g]
    shapes = [a.shape for a in small_names]
    total = sum(int(np.prod(sh)) for sh in shapes) + 128
    rows = -(-total // 1024) * 8
    stacked = [jnp.stack([small_grads[i][j] for i in range(nl)]) for j in range(5)]
    packed = _pack([loss_row.reshape(-1)] + stacked, rows)
    summed = _all_reduce_small(packed)
    loss = summed[0, 0]
    zero_row = jnp.zeros((128,), F32)
    d_s, m_s, v_s = _adamw_small(summed, _pack([zero_row] + small_names, rows), _pack([zero_row] + small_m, rows),
                                 _pack([zero_row] + small_v, rows))
    shapes1 = [(128,)] + shapes
    g_small = _unpack(summed, shapes1)[1:]
    d_small = _unpack(d_s, shapes1)[1:]
    m_small = _unpack(m_s, shapes1)[1:]
    v_small = _unpack(v_s, shapes1)[1:]

    big_m = [m_w_in, m_w_br_a, m_w_br_b, m_w_br_c, m_w_o, m_w_gate_up, m_w_down]
    big_v = [v_w_in, v_w_br_a, v_w_br_b, v_w_br_c, v_w_o, v_w_gate_up, v_w_down]
    big_names = ["w_in", "w_br_a", "w_br_b", "w_br_c", "w_o", "w_gate_up", "w_down"]
    big_out = [[lax.empty(w.shape, F32) for _ in range(4)] for w in big]
    token = started[4]
    for i in list(range(nl - 1, 0, -1)) + [0]:
        if i == 0:
            recv[0] = _exchange_wait("scatter", started, kinds, big_out[-1][0] if nl > 1 else dx, name="scatter_wait")
        for j in range(len(big)):
            big_out[j] = _adamw_layer(recv[i][j], big[j], big_m[j], big_v[j], big_out[j], i, token,
                                      name="adamw_" + big_names[j])

    order = ["norm1_g", "w_in", "qk_norm_g", "sink_a", "rpb_c", "w_br_a", "w_br_b", "w_br_c", "w_o", "norm2_g",
             "w_gate_up", "w_down"]
    small_idx = {"norm1_g": 0, "qk_norm_g": 1, "sink_a": 2, "rpb_c": 3, "norm2_g": 4}
    big_idx = {n: j for j, n in enumerate(big_names)}

    def pick(kind):
        out = []
        for n in order:
            if n in small_idx:
                out.append([g_small, d_small, m_small, v_small][kind][small_idx[n]])
            else:
                out.append(big_out[big_idx[n]][kind])
        return out

    return (loss, dx.reshape(1, L, D), *pick(0), *pick(1), *pick(2), *pick(3))
```

```python
import functools
import math

import numpy as np
import jax
import jax.numpy as jnp
from jax import lax
from jax.experimental import pallas as pl
from jax.experimental.pallas import tpu as pltpu

F32 = jnp.float32
BF16 = jnp.bfloat16
MESH = pl.DeviceIdType.MESH
N_DEV = 8

HEAD_DIM = 128
NORM_EPS = 1e-6
ROPE_THETA = 10000.0
ATT_SCALE = HEAD_DIM ** -0.5
NEG = -1e30

A_Q_HEADS, A_KV_HEADS, A_RADIUS = 8, 2, 128
A_GROUP = A_Q_HEADS // A_KV_HEADS
B_PATTERNS = ((128, 1), (512, 4), (2048, 16))
B_HG = 4
B_HEADS = len(B_PATTERNS) * B_HG
C_HEADS, GRID_W, C_WIN_ROWS, C_WIN_COLS = 8, 64, 8, 16
C_NREL = 2 * C_WIN_ROWS - 1
C_NCOL = 2 * C_WIN_COLS - 1

PC_QA, PC_KA, PC_VA = 0, 8, 10
PC_QB, PC_KB, PC_VB = 12, 24, 36
PC_QC, PC_KC, PC_VC = 48, 56, 64
N_QKV_CHUNKS = 72
Q_PIECES = ((0, 8, PC_QA), (8, 12, PC_QB), (20, 8, PC_QC))
K_PIECES = ((0, 2, PC_KA), (2, 12, PC_KB), (14, 8, PC_KC))
V_PIECES = ((0, 2, PC_VA), (2, 12, PC_VB), (14, 8, PC_VC))
NQ_CHUNKS, NK_CHUNKS = 28, 22
Q_ROPE_UPTO, K_ROPE_UPTO = 20, 14
Q_CG, K_CG = 4, 2

ADAM_LR, ADAM_B1, ADAM_B2, ADAM_EPS, ADAM_WD, ADAM_STEP = 0.001, 0.9, 0.999, 1e-08, 0.01, 10

VMEM_LIMIT = 48 * 1024 * 1024


def _tile(dim, prefs):
    for p in prefs:
        if dim % p == 0:
            return p
    return dim


def _params(sem, **kw):
    return pltpu.CompilerParams(dimension_semantics=sem, vmem_limit_bytes=VMEM_LIMIT, **kw)


def _piece_map(pieces):
    def f(c):
        out = c - pieces[0][0] + pieces[0][2]
        for first, _, pfirst in pieces[1:]:
            out = jnp.where(c >= first, c - first + pfirst, out)
        return out
    return f


def _mm_nn(a, w, layer, *, out_dtype, name, res=None):
    M, K = a.shape
    nb, _, Kw, ns = w.shape
    assert Kw == K
    tm = _tile(M, (512, 256))
    tn = _tile(ns, (640, 512, 256, 128)) if ns > 1408 else ns
    tk = _tile(K, (2048, 1408, 1024, 512, 256))
    nj, nk = ns // tn, K // tk

    def body(*refs):
        if res is None:
            a_ref, w_ref, o_ref, acc_ref = refs
            r_ref = None
        else:
            a_ref, w_ref, r_ref, o_ref, acc_ref = refs
        k = pl.program_id(3)
        part = jnp.dot(a_ref[...].astype(BF16), w_ref[...], preferred_element_type=F32)

        @pl.when(k == 0)
        def _():
            acc_ref[...] = part

        @pl.when(k > 0)
        def _():
            acc_ref[...] += part

        @pl.when(k == nk - 1)
        def _():
            r = acc_ref[...]
            if r_ref is not None:
                r = r + r_ref[...]
            o_ref[...] = r.astype(out_dtype)

    in_specs = [pl.BlockSpec((tm, tk), lambda i, b, j, k: (i, k)),
                pl.BlockSpec((None, None, tk, tn), lambda i, b, j, k: (b, layer, k, j))]
    args = [a, w]
    if res is not None:
        in_specs.append(pl.BlockSpec((tm, tn), lambda i, b, j, k: (i, b * nj + j)))
        args.append(res)
    return pl.pallas_call(
        body, name=name, grid=(M // tm, nb, nj, nk), in_specs=in_specs,
        out_specs=pl.BlockSpec((tm, tn), lambda i, b, j, k: (i, b * nj + j)),
        out_shape=jax.ShapeDtypeStruct((M, nb * ns), out_dtype),
        scratch_shapes=[pltpu.VMEM((tm, tn), F32)],
        compiler_params=_params(("parallel", "parallel", "parallel", "arbitrary")),
    )(*args)


def _mm_nt(a, w, layer, *, out_dtype, name, dep=None):
    M, N = a.shape
    nb, _, K, ns = w.shape
    assert N == nb * ns
    tm = _tile(M, (1024, 512, 256))
    tn = _tile(ns, (640, 512, 256, 128)) if ns > 1408 else ns
    tk = _tile(K, (1024, 512, 256))
    nj = ns // tn
    nred = nb * nj

    def body(*refs):
        a_ref, w_ref = refs[:2]
        o_ref, acc_ref = refs[-2:]
        s = pl.program_id(2) * nj + pl.program_id(3)
        part = lax.dot_general(a_ref[...].astype(BF16), w_ref[...], (((1,), (1,)), ((), ())),
                               preferred_element_type=F32)

        @pl.when(s == 0)
        def _():
            acc_ref[...] = part

        @pl.when(s > 0)
        def _():
            acc_ref[...] += part

        @pl.when(s == nred - 1)
        def _():
            o_ref[...] = acc_ref[...].astype(out_dtype)

    in_specs = [pl.BlockSpec((tm, tn), lambda i, kk, b, j: (i, b * nj + j)),
                pl.BlockSpec((None, None, tk, tn), lambda i, kk, b, j: (b, layer, kk, j))]
    args = [a, w]
    if dep is not None:
        in_specs.append(ANY_SPEC)
        args.append(dep)
    return pl.pallas_call(
        body, name=name, grid=(M // tm, K // tk, nb, nj), in_specs=in_specs,
        out_specs=pl.BlockSpec((tm, tk), lambda i, kk, b, j: (i, kk)),
        out_shape=jax.ShapeDtypeStruct((M, K), out_dtype),
        scratch_shapes=[pltpu.VMEM((tm, tk), F32)],
        compiler_params=_params(("parallel", "parallel", "arbitrary", "arbitrary")),
    )(*args)


def _mm_tn(a, g, nb, *, name):
    M, Ka = a.shape
    N = g.shape[1]
    ns = N // nb
    tka = _tile(Ka, (1024, 512, 256))
    tn = _tile(ns, (640, 512, 256, 128)) if ns > 1408 else ns
    tm = _tile(M, (1024, 512, 256))
    nj, nm = ns // tn, M // tm

    def body(a_ref, g_ref, o_ref, acc_ref):
        m = pl.program_id(3)
        part = lax.dot_general(a_ref[...].astype(BF16), g_ref[...].astype(BF16), (((0,), (0,)), ((), ())),
                               preferred_element_type=F32)

        @pl.when(m == 0)
        def _():
            acc_ref[...] = part

        @pl.when(m > 0)
        def _():
            acc_ref[...] += part

        @pl.when(m == nm - 1)
        def _():
            o_ref[...] = acc_ref[...].astype(BF16)

    return pl.pallas_call(
        body, name=name, grid=(Ka // tka, nb, nj, nm),
        in_specs=[pl.BlockSpec((tm, tka), lambda ka, b, j, m: (m, ka)),
                  pl.BlockSpec((tm, tn), lambda ka, b, j, m: (m, b * nj + j))],
        out_specs=pl.BlockSpec((None, tka, tn), lambda ka, b, j, m: (b, ka, j)),
        out_shape=jax.ShapeDtypeStruct((nb, Ka, ns), BF16),
        scratch_shapes=[pltpu.VMEM((tka, tn), F32)],
        compiler_params=_params(("parallel", "parallel", "parallel", "arbitrary")),
    )(a, g)


def _exact_mm(a, e, *, name):
    R, K = a.shape
    N = e.shape[1]

    def body(a_ref, e_ref, o_ref):
        x = a_ref[...]
        hi = x.astype(BF16)
        r1 = x - hi.astype(F32)
        mid = r1.astype(BF16)
        lo = (r1 - mid.astype(F32)).astype(BF16)
        ev = e_ref[...]
        o_ref[...] = (jnp.dot(hi, ev, preferred_element_type=F32) + jnp.dot(mid, ev, preferred_element_type=F32)
                      + jnp.dot(lo, ev, preferred_element_type=F32))

    return pl.pallas_call(body, name=name, out_shape=jax.ShapeDtypeStruct((R, N), F32),
                          compiler_params=pltpu.CompilerParams(vmem_limit_bytes=VMEM_LIMIT))(a, e)


def _rms_fwd(x, g, dep, *, name):
    L, D = x.shape
    tl = _tile(L, (256, 128))

    def body(x_ref, g_ref, _dep, h_ref):
        xv = x_ref[...]
        rstd = lax.rsqrt(jnp.mean(xv * xv, axis=-1, keepdims=True) + NORM_EPS)
        h_ref[...] = (xv * rstd * g_ref[...]).astype(BF16)

    return pl.pallas_call(
        body, name=name, grid=(L // tl,),
        in_specs=[pl.BlockSpec((tl, D), lambda t: (t, 0)), pl.BlockSpec((1, D), lambda t: (0, 0)), ANY_SPEC],
        out_specs=pl.BlockSpec((tl, D), lambda t: (t, 0)),
        out_shape=jax.ShapeDtypeStruct((L, D), BF16),
        compiler_params=_params(("parallel",)),
    )(x, g, dep)


def _rms_bwd(x, g, dy, dres, *, name):
    L, D = x.shape
    tl = _tile(L, (128,))

    def body(x_ref, g_ref, dy_ref, dres_ref, dx_ref, dxb_ref, dg_ref):
        t = pl.program_id(0)
        xv = x_ref[...]
        rstd = lax.rsqrt(jnp.mean(xv * xv, axis=-1, keepdims=True) + NORM_EPS)
        xhat = xv * rstd
        dyv = dy_ref[...]
        dxhat = dyv * g_ref[...]
        c = jnp.mean(dxhat * xhat, axis=-1, keepdims=True)
        dx = dres_ref[...] + rstd * (dxhat - xhat * c)
        dx_ref[...] = dx
        dxb_ref[...] = dx.astype(BF16)
        dgp = jnp.sum(dyv * xhat, axis=0, keepdims=True)

        @pl.when(t == 0)
        def _():
            dg_ref[...] = dgp

        @pl.when(t > 0)
        def _():
            dg_ref[...] += dgp

    row = pl.BlockSpec((tl, D), lambda t: (t, 0))
    vec = pl.BlockSpec((1, D), lambda t: (0, 0))
    return pl.pallas_call(
        body, name=name, grid=(L // tl,), in_specs=[row, vec, row, row], out_specs=[row, row, vec],
        out_shape=[jax.ShapeDtypeStruct((L, D), F32), jax.ShapeDtypeStruct((L, D), BF16),
                   jax.ShapeDtypeStruct((1, D), F32)],
        compiler_params=_params(("arbitrary",)),
    )(x, g, dy, dres)


def _gate_fwd(proj, ta, tb, tc, *, name):
    L, D = ta.shape
    tl, tcw = _tile(L, (256, 128)), _tile(D, (512, 256, 128))
    off = N_QKV_CHUNKS * HEAD_DIM // tcw
    nd = D // tcw

    def body(g0, g1, g2, a_ref, b_ref, c_ref, o_ref):
        m = (jax.nn.sigmoid(g0[...]) * a_ref[...] + jax.nn.sigmoid(g1[...]) * b_ref[...]
             + jax.nn.sigmoid(g2[...]) * c_ref[...])
        o_ref[...] = m.astype(BF16)

    blk = pl.BlockSpec((tl, tcw), lambda t, j: (t, j))
    gl = [pl.BlockSpec((tl, tcw), functools.partial(lambda t, j, i: (t, off + i * nd + j), i=i)) for i in range(3)]
    return pl.pallas_call(
        body, name=name, grid=(L // tl, nd), in_specs=gl + [blk, blk, blk], out_specs=blk,
        out_shape=jax.ShapeDtypeStruct((L, D), BF16),
        compiler_params=_params(("parallel", "parallel")),
    )(proj, proj, proj, ta, tb, tc)


def _gate_bwd(proj, ta, tb, tc, dmerged, *, name):
    L, D = ta.shape
    ncols = proj.shape[1]
    tl, tcw = _tile(L, (256, 128)), _tile(D, (512, 256, 128))
    off = N_QKV_CHUNKS * HEAD_DIM // tcw
    nd = D // tcw

    def body(g0, g1, g2, a_ref, b_ref, c_ref, dm_ref, da_ref, db_ref, dc_ref, dgl_ref):
        i = pl.program_id(2)
        sg = jax.nn.sigmoid(jnp.where(i == 0, g0[...], jnp.where(i == 1, g1[...], g2[...])))
        sel_t = jnp.where(i == 0, a_ref[...], jnp.where(i == 1, b_ref[...], c_ref[...]))
        dt = dm_ref[...] * sg
        dtb = dt.astype(BF16)

        @pl.when(i == 0)
        def _():
            da_ref[...] = dtb

        @pl.when(i == 1)
        def _():
            db_ref[...] = dtb

        @pl.when(i == 2)
        def _():
            dc_ref[...] = dtb

        dgl_ref[...] = (dt * sel_t * (1.0 - sg)).astype(BF16)

    blk = pl.BlockSpec((tl, tcw), lambda t, j, i: (t, j))
    gl = [pl.BlockSpec((tl, tcw), functools.partial(lambda t, j, i, q: (t, off + q * nd + j), q=q)) for q in range(3)]
    return pl.pallas_call(
        body, name=name, grid=(L // tl, nd, 3), in_specs=gl + [blk, blk, blk, blk],
        out_specs=[blk, blk, blk, pl.BlockSpec((tl, tcw), lambda t, j, i: (t, off + i * nd + j))],
        out_shape=[jax.ShapeDtypeStruct((L, D), BF16)] * 3 + [jax.ShapeDtypeStruct((L, ncols), BF16)],
        compiler_params=_params(("parallel", "parallel", "arbitrary")),
    )(proj, proj, proj, ta, tb, tc, dmerged)


def _swiglu_fwd(gu, *, name):
    L, F2 = gu.shape
    F = F2 // 2
    tl, tcw = _tile(L, (256, 128)), _tile(F, (512, 256, 128))
    nf = F // tcw

    def body(g_ref, u_ref, o_ref):
        gt = g_ref[...]
        o_ref[...] = (gt * jax.nn.sigmoid(gt) * u_ref[...]).astype(BF16)

    return pl.pallas_call(
        body, name=name, grid=(L // tl, nf),
        in_specs=[pl.BlockSpec((tl, tcw), lambda t, j: (t, j)), pl.BlockSpec((tl, tcw), lambda t, j: (t, nf + j))],
        out_specs=pl.BlockSpec((tl, tcw), lambda t, j: (t, j)),
        out_shape=jax.ShapeDtypeStruct((L, F), BF16),
        compiler_params=_params(("parallel", "parallel")),
    )(gu, gu)


def _swiglu_bwd(gu, dact, *, name):
    L, F2 = gu.shape
    F = F2 // 2
    tl = _tile(L, (64,))

    def body(gu_ref, d_ref, o_ref):
        gt, up, d = gu_ref[:, :F], gu_ref[:, F:], d_ref[...]
        sg = jax.nn.sigmoid(gt)
        o_ref[:, :F] = (d * up * sg * (1.0 + gt * (1.0 - sg))).astype(BF16)
        o_ref[:, F:] = (d * gt * sg).astype(BF16)

    return pl.pallas_call(
        body, name=name, grid=(L // tl,),
        in_specs=[pl.BlockSpec((tl, F2), lambda t: (t, 0)), pl.BlockSpec((tl, F), lambda t: (t, 0))],
        out_specs=pl.BlockSpec((tl, F2), lambda t: (t, 0)),
        out_shape=jax.ShapeDtypeStruct((L, F2), BF16),
        compiler_params=_params(("parallel",)),
    )(gu, dact)


def _loss(y, tgt, *, name):
    L, D = y.shape
    tl = _tile(L, (256, 128))
    nt = L // tl

    def body(y_ref, t_ref, dy_ref, dyb_ref, loss_ref, acc_ref):
        t = pl.program_id(0)
        e = y_ref[...] - t_ref[...]
        dy = e * (1.0 / D)
        dy_ref[...] = dy
        dyb_ref[...] = dy.astype(BF16)
        part = jnp.sum(e * e, axis=0, keepdims=True)

        @pl.when(t == 0)
        def _():
            acc_ref[...] = part

        @pl.when(t > 0)
        def _():
            acc_ref[...] += part

        @pl.when(t == nt - 1)
        def _():
            loss_ref[...] = jnp.broadcast_to(jnp.sum(acc_ref[...], axis=-1, keepdims=True) * (0.5 / D), (1, 128))

    row = pl.BlockSpec((tl, D), lambda t: (t, 0))
    return pl.pallas_call(
        body, name=name, grid=(nt,), in_specs=[row, row],
        out_specs=[row, row, pl.BlockSpec((1, 128), lambda t: (0, 0))],
        out_shape=[jax.ShapeDtypeStruct((L, D), F32), jax.ShapeDtypeStruct((L, D), BF16),
                   jax.ShapeDtypeStruct((1, 128), F32)],
        scratch_shapes=[pltpu.VMEM((1, D), F32)],
        compiler_params=_params(("arbitrary",)),
    )(y, tgt)


def _rope(v, cos, sin_signed):
    return v * cos + pltpu.roll(v, HEAD_DIM // 2, 1) * sin_signed


def _qk_fwd(proj, gtab, cos, sin, pieces, nchunks, rope_upto, cg, *, name):
    L = proj.shape[0]
    tl = _tile(L, (512, 256, 128))
    W = cg * HEAD_DIM
    pmap = _piece_map(tuple((a // cg, n // cg, p // cg) for a, n, p in pieces))

    def body(p_ref, g_ref, cos_ref, sin_ref, o_ref):
        c = pl.program_id(1)

        def norm(j):
            cols = slice(j * HEAD_DIM, (j + 1) * HEAD_DIM)
            x = p_ref[:, cols]
            rstd = lax.rsqrt(jnp.mean(x * x, axis=-1, keepdims=True) + NORM_EPS)
            return cols, x * rstd * g_ref[:, cols]

        @pl.when(c < rope_upto // cg)
        def _():
            for j in range(cg):
                cols, y = norm(j)
                o_ref[:, cols] = _rope(y, cos_ref[...], sin_ref[...])

        @pl.when(c >= rope_upto // cg)
        def _():
            for j in range(cg):
                cols, y = norm(j)
                o_ref[:, cols] = y

    pos = pl.BlockSpec((tl, HEAD_DIM), lambda t, c: (t, 0))
    return pl.pallas_call(
        body, name=name, grid=(L // tl, nchunks // cg),
        in_specs=[pl.BlockSpec((tl, W), lambda t, c: (t, pmap(c))),
                  pl.BlockSpec((None, 1, W), lambda t, c: (c, 0, 0)), pos, pos],
        out_specs=pl.BlockSpec((tl, W), lambda t, c: (t, c)),
        out_shape=jax.ShapeDtypeStruct((L, nchunks * HEAD_DIM), F32),
        compiler_params=_params(("parallel", "parallel")),
    )(proj, gtab, cos, sin)


def _qk_bwd(dqk, proj, gtab, cos, sin, dproj, pieces, nchunks, rope_upto, cg, *, name):
    L = proj.shape[0]
    tl = _tile(L, (512, 256, 128))
    W = cg * HEAD_DIM
    pmap = _piece_map(tuple((a // cg, n // cg, p // cg) for a, n, p in pieces))

    def body(d_ref, p_ref, g_ref, cos_ref, sin_ref, _, o_ref, dg_ref):
        c, t = pl.program_id(0), pl.program_id(1)

        @pl.when(t == 0)
        def _():
            dg_ref[...] = jnp.zeros_like(dg_ref)

        for j in range(cg):
            cols = slice(j * HEAD_DIM, (j + 1) * HEAD_DIM)
            x = p_ref[:, cols]
            rstd = lax.rsqrt(jnp.mean(x * x, axis=-1, keepdims=True) + NORM_EPS)
            xhat = x * rstd
            dy = d_ref[:, cols]
            dy = jnp.where(c < rope_upto // cg, _rope(dy, cos_ref[...], -sin_ref[...]), dy)
            dxhat = dy * g_ref[:, cols]
            cm = jnp.mean(dxhat * xhat, axis=-1, keepdims=True)
            o_ref[:, cols] = (rstd * (dxhat - xhat * cm)).astype(BF16)
            dg_ref[:, cols] += jnp.sum(dy * xhat, axis=0, keepdims=True)

    pos = pl.BlockSpec((tl, HEAD_DIM), lambda c, t: (t, 0))
    gspec = pl.BlockSpec((None, 1, W), lambda c, t: (c, 0, 0))
    out, dg = pl.pallas_call(
        body, name=name, grid=(nchunks // cg, L // tl),
        in_specs=[pl.BlockSpec((tl, W), lambda c, t: (t, c)),
                  pl.BlockSpec((tl, W), lambda c, t: (t, pmap(c))), gspec, pos, pos,
                  pl.BlockSpec(memory_space=pl.ANY)],
        out_specs=[pl.BlockSpec((tl, W), lambda c, t: (t, pmap(c))), gspec],
        out_shape=[jax.ShapeDtypeStruct(dproj.shape, BF16), jax.ShapeDtypeStruct((nchunks // cg, 1, W), F32)],
        input_output_aliases={5: 0},
        compiler_params=_params(("parallel", "arbitrary")),
    )(dqk, proj, gtab, cos, sin, dproj)
    return out, dg


def _v_bwd(dv, dproj, *, name):
    L = dv.shape[0]
    tl = _tile(L, (512, 256, 128))
    pmap = _piece_map(tuple((a // 2, n // 2, p // 2) for a, n, p in V_PIECES))

    def body(d_ref, _, o_ref):
        o_ref[...] = d_ref[...].astype(BF16)

    return pl.pallas_call(
        body, name=name, grid=(L // tl, NK_CHUNKS // 2),
        in_specs=[pl.BlockSpec((tl, 2 * HEAD_DIM), lambda t, c: (t, c)), pl.BlockSpec(memory_space=pl.ANY)],
        out_specs=pl.BlockSpec((tl, 2 * HEAD_DIM), lambda t, c: (t, pmap(c))),
        out_shape=jax.ShapeDtypeStruct(dproj.shape, BF16),
        input_output_aliases={1: 0},
        compiler_params=_params(("parallel", "parallel")),
    )(dv, dproj)


def _band_geometry(L, dil, radius):
    n = L // dil
    bq = min(128, n)
    width = min(bq + 2 * radius, n)
    return n, bq, width


def _band_rows(dil, r, first, count):
    if dil == 1:
        return pl.ds(pl.multiple_of(first, 8), count)
    return pl.ds(r + first * dil, count, stride=dil)


def _band_mask(i, bq, width, radius, ws):
    qpos = i * bq + lax.broadcasted_iota(jnp.int32, (bq, width), 0)
    kpos = ws + lax.broadcasted_iota(jnp.int32, (bq, width), 1)
    return jnp.abs(kpos - qpos) <= radius


def _band_fwd(qn, kn, proj, *, dil, radius, nkv, group, q0, k0, v0, sink=None, name):
    L = qn.shape[0]
    n, bq, width = _band_geometry(L, dil, radius)
    tq = bq * dil
    nh = nkv * group

    def body(*refs):
        if sink is None:
            q_ref, k_ref, v_ref, o_ref, lse_ref = refs
        else:
            q_ref, k_ref, v_ref, s_ref, o_ref, lse_ref = refs
        i = pl.program_id(2)
        ws = jnp.clip(i * bq - radius, 0, n - width)
        valid = _band_mask(i, bq, width, radius, ws)

        def one(r, carry):
            qrows = _band_rows(dil, r, 0, bq)
            krows = _band_rows(dil, r, ws, width)
            q = q_ref[qrows, :].astype(BF16)
            k = k_ref[krows, :].astype(BF16)
            v = v_ref[krows, :].astype(BF16)
            s = lax.dot_general(q, k, (((1,), (1,)), ((), ())), preferred_element_type=F32) * ATT_SCALE
            s = jnp.where(valid, s, NEG)
            m = jnp.max(s, axis=-1, keepdims=True)
            if sink is not None:
                m = jnp.maximum(m, s_ref[...][:, :1])
            p = jnp.exp(s - m)
            denom = jnp.sum(p, axis=-1, keepdims=True)
            if sink is not None:
                denom = denom + jnp.exp(s_ref[...][:, :1] - m)
            pn = (p / denom).astype(BF16)
            o_ref[qrows, :] = jnp.dot(pn, v, preferred_element_type=F32)
            lse_ref[qrows, :] = jnp.broadcast_to(m + jnp.log(denom), (bq, HEAD_DIM))
            return carry

        if dil == 1:
            one(0, 0)
        else:
            lax.fori_loop(0, dil, one, 0)

    qspec = pl.BlockSpec((tq, HEAD_DIM), lambda hk, g, i: (i, q0 + hk * group + g))
    in_specs = [qspec,
                pl.BlockSpec((L, HEAD_DIM), lambda hk, g, i: (0, k0 + hk)),
                pl.BlockSpec((L, HEAD_DIM), lambda hk, g, i: (0, v0 + hk))]
    args = [qn, kn, proj]
    if sink is not None:
        in_specs.append(pl.BlockSpec((None, 1, HEAD_DIM), lambda hk, g, i: (hk * group + g, 0, 0)))
        args.append(sink)
    ospec = pl.BlockSpec((tq, HEAD_DIM), lambda hk, g, i: (i, hk * group + g))
    return pl.pallas_call(
        body, name=name, grid=(nkv, group, n // bq), in_specs=in_specs, out_specs=[ospec, ospec],
        out_shape=[jax.ShapeDtypeStruct((L, nh * HEAD_DIM), F32)] * 2,
        compiler_params=_params(("parallel", "parallel", "arbitrary")),
    )(*args)


def _band_bwd(qn, kn, proj, do, o, lse, dq_buf, dk_buf, dv_buf, *, dil, radius, nkv, group, q0, k0, v0, o0,
              sink=None, name):
    L = qn.shape[0]
    n, bq, width = _band_geometry(L, dil, radius)
    tq = bq * dil
    nh = nkv * group
    n_in = 6 + (1 if sink is not None else 0)

    def body(*refs):
        q_ref, k_ref, v_ref, do_ref, o_ref, lse_ref = refs[:6]
        s_ref = refs[6] if sink is not None else None
        outs = refs[n_in + 3:]
        dq_ref, dk_ref, dv_ref = outs[:3]
        ds_ref = outs[3] if sink is not None else None
        g, i = pl.program_id(1), pl.program_id(2)
        ws = jnp.clip(i * bq - radius, 0, n - width)
        valid = _band_mask(i, bq, width, radius, ws)

        @pl.when((g == 0) & (i == 0))
        def _():
            dk_ref[...] = jnp.zeros_like(dk_ref)
            dv_ref[...] = jnp.zeros_like(dv_ref)

        if sink is not None:
            @pl.when(i == 0)
            def _():
                ds_ref[...] = jnp.zeros_like(ds_ref)

        def one(r, carry):
            qrows = _band_rows(dil, r, 0, bq)
            krows = _band_rows(dil, r, ws, width)
            q = q_ref[qrows, :].astype(BF16)
            k = k_ref[krows, :].astype(BF16)
            v = v_ref[krows, :].astype(BF16)
            dov = do_ref[qrows, :]
            lse_v = lse_ref[qrows, :][:, :1]
            delta = jnp.sum(dov * o_ref[qrows, :], axis=-1, keepdims=True)
            dob = dov.astype(BF16)
            s = lax.dot_general(q, k, (((1,), (1,)), ((), ())), preferred_element_type=F32) * ATT_SCALE
            p = jnp.where(valid, jnp.exp(s - lse_v), 0.0)
            dp = lax.dot_general(dob, v, (((1,), (1,)), ((), ())), preferred_element_type=F32)
            dsb = (p * (dp - delta)).astype(BF16)
            dq_ref[qrows, :] = jnp.dot(dsb, k, preferred_element_type=F32) * ATT_SCALE
            dk_ref[krows, :] += lax.dot_general(dsb, q, (((0,), (0,)), ((), ())),
                                                preferred_element_type=F32) * ATT_SCALE
            dv_ref[krows, :] += lax.dot_general(p.astype(BF16), dob, (((0,), (0,)), ((), ())),
                                                preferred_element_type=F32)
            if sink is not None:
                ps = jnp.exp(s_ref[...][:, :1] - lse_v)
                ds_ref[...] += jnp.broadcast_to(jnp.sum(-ps * delta, axis=0, keepdims=True), (1, HEAD_DIM))
            return carry

        if dil == 1:
            one(0, 0)
        else:
            lax.fori_loop(0, dil, one, 0)

    hspec = pl.BlockSpec((tq, HEAD_DIM), lambda hk, g, i: (i, o0 + hk * group + g))
    qspec = pl.BlockSpec((tq, HEAD_DIM), lambda hk, g, i: (i, q0 + hk * group + g))
    kspec = pl.BlockSpec((L, HEAD_DIM), lambda hk, g, i: (0, k0 + hk))
    any_spec = pl.BlockSpec(memory_space=pl.ANY)
    in_specs = [qspec, kspec, pl.BlockSpec((L, HEAD_DIM), lambda hk, g, i: (0, v0 + hk)), hspec, hspec, hspec]
    args = [qn, kn, proj, do, o, lse]
    if sink is not None:
        in_specs.append(pl.BlockSpec((None, 1, HEAD_DIM), lambda hk, g, i: (hk * group + g, 0, 0)))
        args.append(sink)
    in_specs += [any_spec] * 3
    args += [dq_buf, dk_buf, dv_buf]
    out_specs = [qspec, kspec, kspec]
    out_shape = [jax.ShapeDtypeStruct(dq_buf.shape, F32), jax.ShapeDtypeStruct(dk_buf.shape, F32),
                 jax.ShapeDtypeStruct(dv_buf.shape, F32)]
    if sink is not None:
        out_specs.append(pl.BlockSpec((None, 1, HEAD_DIM), lambda hk, g, i: (hk * group + g, 0, 0)))
        out_shape.append(jax.ShapeDtypeStruct((nh, 1, HEAD_DIM), F32))
    return pl.pallas_call(
        body, name=name, grid=(nkv, group, n // bq), in_specs=in_specs, out_specs=out_specs, out_shape=out_shape,
        input_output_aliases={n_in: 0, n_in + 1: 1, n_in + 2: 2},
        compiler_params=_params(("parallel", "arbitrary", "arbitrary")),
    )(*args)


def _combine_b(os_, lses, *, name):
    L, W = os_[0].shape
    tl = _tile(L, (256, 128))

    def body(o0, o1, o2, l0, l1, l2, out_ref, lt_ref):
        a, b, c = l0[...], l1[...], l2[...]
        m = jnp.maximum(jnp.maximum(a, b), c)
        ea, eb, ec = jnp.exp(a - m), jnp.exp(b - m), jnp.exp(c - m)
        tot = ea + eb + ec
        out_ref[...] = (ea * o0[...] + eb * o1[...] + ec * o2[...]) / tot
        lt_ref[...] = m + jnp.log(tot)

    blk = pl.BlockSpec((tl, W), lambda t: (t, 0))
    return pl.pallas_call(
        body, name=name, grid=(L // tl,), in_specs=[blk] * 6, out_specs=[blk, blk],
        out_shape=[jax.ShapeDtypeStruct((L, W), F32)] * 2, compiler_params=_params(("parallel",)),
    )(*os_, *lses)


def _c_geometry(L):
    rows = L // GRID_W
    assert rows >= C_WIN_ROWS
    return rows


C_KEYS = C_WIN_ROWS * GRID_W


def _c_row_start(r, rows):
    return jnp.clip(r - C_WIN_ROWS // 2, 0, rows - C_WIN_ROWS)


def _c_bias_tiles(bias_t):
    tiles = jnp.stack([jnp.concatenate([bias_t[:, b + w] for w in range(C_WIN_ROWS)], axis=-1)
                       for b in range(C_WIN_ROWS)], axis=1)
    cq = np.arange(GRID_W)[:, None]
    ck = np.arange(GRID_W)[None, :]
    start = np.clip(cq - C_WIN_COLS // 2, 0, GRID_W - C_WIN_COLS)
    ok = np.tile((ck >= start) & (ck < start + C_WIN_COLS), (1, C_WIN_ROWS))
    return jnp.where(jnp.asarray(ok), tiles, NEG)


def _c_fwd(qn, kn, proj, tiles, *, name):
    L = qn.shape[0]
    rows = _c_geometry(L)

    def body(q_ref, k_ref, v_ref, t_ref, o_ref, lse_ref):
        rs = _c_row_start(pl.program_id(1), rows)
        krows = pl.ds(pl.multiple_of(rs * GRID_W, GRID_W), C_KEYS)
        q = q_ref[...].astype(BF16)
        k = k_ref[krows, :].astype(BF16)
        v = v_ref[krows, :].astype(BF16)
        s = lax.dot_general(q, k, (((1,), (1,)), ((), ())), preferred_element_type=F32) * ATT_SCALE + t_ref[...]
        m = jnp.max(s, axis=-1, keepdims=True)
        p = jnp.exp(s - m)
        denom = jnp.sum(p, axis=-1, keepdims=True)
        o_ref[...] = jnp.dot((p / denom).astype(BF16), v, preferred_element_type=F32)
        lse_ref[...] = jnp.broadcast_to(m + jnp.log(denom), (GRID_W, HEAD_DIM))

    def tile_index(h, r):
        return (h, _c_row_start(r, rows) - r + (C_WIN_ROWS - 1), 0, 0)

    ospec = pl.BlockSpec((GRID_W, HEAD_DIM), lambda h, r: (r, h))
    return pl.pallas_call(
        body, name=name, grid=(C_HEADS, rows),
        in_specs=[pl.BlockSpec((GRID_W, HEAD_DIM), lambda h, r: (r, 20 + h)),
                  pl.BlockSpec((L, HEAD_DIM), lambda h, r: (0, 14 + h)),
                  pl.BlockSpec((L, HEAD_DIM), lambda h, r: (0, PC_VC + h)),
                  pl.BlockSpec((None, None, GRID_W, C_KEYS), tile_index)],
        out_specs=[ospec, ospec],
        out_shape=[jax.ShapeDtypeStruct((L, C_HEADS * HEAD_DIM), F32)] * 2,
        compiler_params=_params(("parallel", "arbitrary")),
    )(qn, kn, proj, tiles)


def _c_bwd(qn, kn, proj, tiles, do, o, lse, dq_buf, dk_buf, dv_buf, *, name):
    L = qn.shape[0]
    rows = _c_geometry(L)

    def body(q_ref, k_ref, v_ref, t_ref, do_ref, o_ref, lse_ref, _a, _b, _c, dq_ref, dk_ref, dv_ref, dt_ref):
        r = pl.program_id(1)
        rs = _c_row_start(r, rows)
        base = rs - r + (C_WIN_ROWS - 1)
        krows = pl.ds(pl.multiple_of(rs * GRID_W, GRID_W), C_KEYS)

        @pl.when(r == 0)
        def _():
            dk_ref[...] = jnp.zeros_like(dk_ref)
            dv_ref[...] = jnp.zeros_like(dv_ref)
            dt_ref[...] = jnp.zeros_like(dt_ref)

        q = q_ref[...].astype(BF16)
        k = k_ref[krows, :].astype(BF16)
        v = v_ref[krows, :].astype(BF16)
        dov = do_ref[...]
        dob = dov.astype(BF16)
        delta = jnp.sum(dov * o_ref[...], axis=-1, keepdims=True)
        s = lax.dot_general(q, k, (((1,), (1,)), ((), ())), preferred_element_type=F32) * ATT_SCALE + t_ref[...]
        p = jnp.exp(s - lse_ref[...][:, :1])
        dp = lax.dot_general(dob, v, (((1,), (1,)), ((), ())), preferred_element_type=F32)
        ds = p * (dp - delta)
        for w in range(C_WIN_ROWS):
            dt_ref[base + w] += ds[:, w * GRID_W:(w + 1) * GRID_W]
        dsb = ds.astype(BF16)
        dq_ref[...] = jnp.dot(dsb, k, preferred_element_type=F32) * ATT_SCALE
        dk_ref[krows, :] += lax.dot_general(dsb, q, (((0,), (0,)), ((), ())), preferred_element_type=F32) * ATT_SCALE
        dv_ref[krows, :] += lax.dot_general(p.astype(BF16), dob, (((0,), (0,)), ((), ())),
                                            preferred_element_type=F32)

    def tile_index(h, r):
        return (h, _c_row_start(r, rows) - r + (C_WIN_ROWS - 1), 0, 0)

    hspec = pl.BlockSpec((GRID_W, HEAD_DIM), lambda h, r: (r, h))
    qspec = pl.BlockSpec((GRID_W, HEAD_DIM), lambda h, r: (r, 20 + h))
    kspec = pl.BlockSpec((L, HEAD_DIM), lambda h, r: (0, 14 + h))
    any_spec = pl.BlockSpec(memory_space=pl.ANY)
    return pl.pallas_call(
        body, name=name, grid=(C_HEADS, rows),
        in_specs=[qspec, kspec, pl.BlockSpec((L, HEAD_DIM), lambda h, r: (0, PC_VC + h)),
                  pl.BlockSpec((None, None, GRID_W, C_KEYS), tile_index),
                  hspec, hspec, hspec, any_spec, any_spec, any_spec],
        out_specs=[qspec, kspec, kspec,
                   pl.BlockSpec((None, C_NREL, GRID_W, GRID_W), lambda h, r: (h, 0, 0, 0))],
        out_shape=[jax.ShapeDtypeStruct(dq_buf.shape, F32), jax.ShapeDtypeStruct(dk_buf.shape, F32),
                   jax.ShapeDtypeStruct(dv_buf.shape, F32),
                   jax.ShapeDtypeStruct((C_HEADS, C_NREL, GRID_W, GRID_W), F32)],
        input_output_aliases={7: 0, 8: 1, 9: 2},
        compiler_params=_params(("parallel", "arbitrary")),
    )(qn, kn, proj, tiles, do, o, lse, dq_buf, dk_buf, dv_buf)


def _c_expand_matrix():
    cq = np.arange(GRID_W)[:, None]
    ck = np.arange(GRID_W)[None, :]
    d = (ck - cq + (C_WIN_COLS - 1)).reshape(-1)
    e = np.zeros((GRID_W * GRID_W, HEAD_DIM), np.float32)
    okd = (d >= 0) & (d < C_NCOL)
    e[np.arange(GRID_W * GRID_W)[okd], d[okd]] = 1.0
    return e


def _peer(p):
    return (p // 4, (p // 2) % 2, p % 2)


def _my_index():
    return 4 * lax.axis_index("x") + 2 * lax.axis_index("y") + lax.axis_index("c")


HBM_SPEC = pl.BlockSpec(memory_space=pltpu.HBM)
SEM_SPEC = pl.BlockSpec(memory_space=pltpu.SEMAPHORE)
ANY_SPEC = pl.BlockSpec(memory_space=pl.ANY)
DATAFLOW = pltpu.SideEffectType.DATAFLOW_SIDE_EFFECTING


def _exchange_views(mode, kinds, arrays):
    nw = len(kinds)
    if mode == "gather":
        rows = [a.shape[0] // N_DEV for a in arrays[:nw]]
    else:
        rows = [a.shape[1] // N_DEV for a in arrays[:nw]]

    def gather_slot(ref, w, who):
        return ref.at[who] if kinds[w] == "col" else ref.at[pl.ds(who * rows[w], rows[w]), :]

    def src(ref, w, to):
        if mode == "gather":
            return gather_slot(ref, w, _my_index())
        return ref.at[to] if kinds[w] == "col" else ref.at[0, pl.ds(to * rows[w], rows[w]), :]

    def dst(ref, w, who):
        return gather_slot(ref, w, who) if mode == "gather" else ref.at[who]

    return src, dst


def _place_cast(w, layer, kind, *, name):
    _, R, C = w.shape
    tr = _tile(R, (256, 128, 64, 32, 16))

    def body(w_ref, o_ref):
        o_ref[...] = w_ref[...].astype(BF16)

    if kind == "col":
        out_shape = jax.ShapeDtypeStruct((N_DEV, R, C), BF16)
        out_spec = pl.BlockSpec((None, tr, C), lambda t: (_my_index(), t, 0))
    else:
        out_shape = jax.ShapeDtypeStruct((N_DEV * R, C), BF16)
        out_spec = pl.BlockSpec((tr, C), lambda t: (_my_index() * (R // tr) + t, 0))
    return pl.pallas_call(
        body, name=name, grid=(R // tr,), in_specs=[pl.BlockSpec((None, tr, C), lambda t: (layer, t, 0))],
        out_specs=out_spec, out_shape=out_shape, compiler_params=_params(("parallel",)),
    )(w)


def _exchange_start(mode, srcs, lands, kinds, after, *, name):
    nw = len(lands)
    ns = len(srcs)
    src, dst = _exchange_views(mode, kinds, list(srcs) + list(lands))

    def body(*refs):
        l_refs = refs[ns:ns + nw]
        s_refs = refs[:ns] if ns else l_refs
        send_sems, recv_sems = refs[ns + nw + 1], refs[ns + nw + 2]
        token = refs[-1]
        me = _my_index()
        for off in range(1, N_DEV):
            to = (me + off) % N_DEV
            for w in range(nw):
                pltpu.make_async_remote_copy(src(s_refs[w], w, to), dst(l_refs[w], w, me),
                                             send_sems.at[w * N_DEV + off], recv_sems.at[w * N_DEV + off],
                                             device_id=_peer(to), device_id_type=MESH).start()
        token[...] = jnp.zeros_like(token)

    arrays = list(srcs) + list(lands)
    na = len(arrays)
    outs = pl.pallas_call(
        body, name=name,
        out_shape=(pltpu.SemaphoreType.DMA((nw * N_DEV,)), pltpu.SemaphoreType.DMA((nw * N_DEV,)),
                   *[pltpu.HBM(a.shape, a.dtype) for a in arrays], jax.ShapeDtypeStruct((8, 128), F32)),
        in_specs=[HBM_SPEC] * na + [ANY_SPEC],
        out_specs=(SEM_SPEC, SEM_SPEC, *([HBM_SPEC] * na), pl.BlockSpec(memory_space=pltpu.VMEM)),
        input_output_aliases={k: 2 + k for k in range(na)},
        compiler_params=pltpu.CompilerParams(has_side_effects=DATAFLOW),
    )(*[pltpu.with_memory_space_constraint(a, pltpu.HBM) for a in arrays], after)
    return outs[0], outs[1], outs[2:2 + ns], outs[2 + ns:2 + na], outs[-1]


def _exchange_wait(mode, started, kinds, after, *, name):
    send_sems, recv_sems, srcs, lands, _ = started
    nw = len(lands)
    ns = len(srcs)
    arrays = list(srcs) + list(lands)
    na = len(arrays)
    src, dst = _exchange_views(mode, kinds, arrays)

    def body(*refs):
        l_refs = refs[ns:na]
        s_refs = refs[:ns] if ns else l_refs
        send_ref, recv_ref = refs[na], refs[na + 1]
        me = _my_index()
        for off in range(1, N_DEV):
            to = (me + off) % N_DEV
            frm = (me + N_DEV - off) % N_DEV
            for w in range(nw):
                cp = pltpu.make_async_remote_copy(src(s_refs[w], w, to), dst(l_refs[w], w, frm),
                                                  send_ref.at[w * N_DEV + off], recv_ref.at[w * N_DEV + off],
                                                  device_id=_peer(frm), device_id_type=MESH)
                cp.wait_send()
                cp.wait_recv()

    outs = pl.pallas_call(
        body, name=name, out_shape=[pltpu.HBM(a.shape, a.dtype) for a in arrays],
        in_specs=[HBM_SPEC] * na + [SEM_SPEC, SEM_SPEC, ANY_SPEC], out_specs=[HBM_SPEC] * na,
        input_output_aliases={k: k for k in range(na)},
        compiler_params=pltpu.CompilerParams(has_side_effects=DATAFLOW),
    )(*arrays, send_sems, recv_sems, after)
    return outs[:ns], outs[ns:]


def _all_reduce_small(x):
    R = x.shape[0]

    def body(x_ref, o_ref, gath, send_sems, recv_sems):
        me = _my_index()
        gath[me] = x_ref[...]
        sends = []
        for off in range(1, N_DEV):
            to = (me + off) % N_DEV
            cp = pltpu.make_async_remote_copy(x_ref, gath.at[me], send_sems.at[off], recv_sems.at[off],
                                              device_id=_peer(to), device_id_type=MESH)
            cp.start()
            sends.append(cp)
        for off in range(1, N_DEV):
            frm = (me + N_DEV - off) % N_DEV
            pltpu.make_async_remote_copy(x_ref, gath.at[frm], send_sems.at[off], recv_sems.at[off],
                                         device_id=_peer(frm), device_id_type=MESH).wait_recv()
        for cp in sends:
            cp.wait_send()
        acc = gath[0]
        for s in range(1, N_DEV):
            acc = acc + gath[s]
        o_ref[...] = acc

    vm = pl.BlockSpec(memory_space=pltpu.VMEM)
    return pl.pallas_call(
        body, name="all_reduce_small", in_specs=[vm], out_specs=vm, out_shape=jax.ShapeDtypeStruct((R, 128), F32),
        scratch_shapes=[pltpu.VMEM((N_DEV, R, 128), F32), pltpu.SemaphoreType.DMA((N_DEV,)),
                        pltpu.SemaphoreType.DMA((N_DEV,))],
        compiler_params=pltpu.CompilerParams(has_side_effects=True),
    )(x)


def _adamw_math(w, g, m, v):
    m = ADAM_B1 * m + (1.0 - ADAM_B1) * g
    v = ADAM_B2 * v + (1.0 - ADAM_B2) * (g * g)
    m_hat = m / (1.0 - ADAM_B1 ** ADAM_STEP)
    v_hat = v / (1.0 - ADAM_B2 ** ADAM_STEP)
    delta = -ADAM_LR * (m_hat / (jnp.sqrt(v_hat) + ADAM_EPS) + ADAM_WD * w)
    return delta, m, v


def _adamw_layer(recv, own, kind, w, m, v, outs, layer, dep, *, name):
    nl, R, C = w.shape
    tr = _tile(R, (128, 64, 32, 16))

    def body(r_ref, o_ref, w_ref, m_ref, v_ref, _0, _1, _2, _3, _dep, g_out, d_out, m_out, v_out):
        me = _my_index()
        mine = o_ref[...].astype(F32)
        g = jnp.where(me == 0, mine, r_ref[0].astype(F32))
        for s in range(1, N_DEV):
            g = g + jnp.where(me == s, mine, r_ref[s].astype(F32))
        delta, mn, vn = _adamw_math(w_ref[...], g, m_ref[...], v_ref[...])
        g_out[...] = g
        d_out[...] = delta
        m_out[...] = mn
        v_out[...] = vn

    if kind == "col":
        own_spec = pl.BlockSpec((None, tr, C), lambda t: (_my_index(), t, 0))
    else:
        own_spec = pl.BlockSpec((None, tr, C), lambda t: (0, _my_index() * (R // tr) + t, 0))
    wspec = pl.BlockSpec((None, tr, C), lambda t: (layer, t, 0))
    return pl.pallas_call(
        body, name=name, grid=(R // tr,),
        in_specs=[pl.BlockSpec((N_DEV, tr, C), lambda t: (0, t, 0)), own_spec] + [wspec] * 3 + [ANY_SPEC] * 5,
        out_specs=[wspec] * 4, out_shape=[jax.ShapeDtypeStruct((nl, R, C), F32)] * 4,
        input_output_aliases={5: 0, 6: 1, 7: 2, 8: 3},
        compiler_params=_params(("parallel",)),
    )(recv, own, w, m, v, *outs, dep)


def _adamw_small(g, w, m, v):
    def body(g_ref, w_ref, m_ref, v_ref, d_out, m_out, v_out):
        delta, mn, vn = _adamw_math(w_ref[...], g_ref[...], m_ref[...], v_ref[...])
        d_out[...] = delta
        m_out[...] = mn
        v_out[...] = vn

    return pl.pallas_call(body, name="adamw_small", out_shape=[jax.ShapeDtypeStruct(g.shape, F32)] * 3)(g, w, m, v)


def _pack(arrays, rows):
    flat = jnp.concatenate([a.reshape(-1) for a in arrays])
    return jnp.pad(flat, (0, rows * 128 - flat.shape[0])).reshape(rows, 128)


def _unpack(packed, shapes):
    flat = packed.reshape(-1)
    out, pos = [], 0
    for s in shapes:
        size = int(np.prod(s))
        out.append(flat[pos:pos + size].reshape(s))
        pos += size
    return out


def kernel(x, norm1_g, w_in, qk_norm_g, sink_a, rpb_c, w_br_a, w_br_b, w_br_c, w_o, norm2_g, w_gate_up, w_down, loss_target, m_norm1_g, m_w_in, m_qk_norm_g, m_sink_a, m_rpb_c, m_w_br_a, m_w_br_b, m_w_br_c, m_w_o, m_norm2_g, m_w_gate_up, m_w_down, v_norm1_g, v_w_in, v_qk_norm_g, v_sink_a, v_rpb_c, v_w_br_a, v_w_br_b, v_w_br_c, v_w_o, v_norm2_g, v_w_gate_up, v_w_down):
    nl = w_in.shape[0]
    L, D = x.shape[1], x.shape[2]
    x0 = x.reshape(L, D)
    tgt = loss_target.reshape(L, D)

    big = [w_in, w_br_a, w_br_b, w_br_c, w_o, w_gate_up, w_down]
    kinds = ["col", "col", "col", "col", "row", "col", "row"]

    big_names = ["w_in", "w_br_a", "w_br_b", "w_br_c", "w_o", "w_gate_up", "w_down"]
    ALL = list(range(len(big)))
    REST = ALL[1:]

    def gather_place(i):
        return [_place_cast(w, i, k, name="gather_place_" + n) for w, k, n in zip(big, kinds, big_names)]

    def gather_start(lands, sub, after, tag):
        return _exchange_start("gather", [], [lands[j] for j in sub], [kinds[j] for j in sub], after,
                               name="gather_start" + tag)

    def gather_wait(started, sub, after, tag):
        _, lands = _exchange_wait("gather", started, [kinds[j] for j in sub], after, name="gather_wait" + tag)
        return [g.reshape((N_DEV, 1) + g.shape[1:]) if kinds[j] == "col" else g.reshape((1, 1) + g.shape)
                for g, j in zip(lands, sub)]

    half = HEAD_DIM // 2
    inv_freq = ROPE_THETA ** (-jnp.arange(half, dtype=F32) * 2.0 / HEAD_DIM)
    ang = jnp.arange(L, dtype=F32)[:, None] * inv_freq[None, :]
    cos = jnp.concatenate([jnp.cos(ang), jnp.cos(ang)], axis=-1)
    sin = jnp.concatenate([-jnp.sin(ang), jnp.sin(ang)], axis=-1)
    expand = jnp.asarray(_c_expand_matrix(), BF16)
    expand_t = jnp.asarray(_c_expand_matrix().T, BF16)

    def gain_tables(i):
        g = qk_norm_g[i]
        gq = jnp.concatenate([jnp.tile(g[0][None], (8, 1)), jnp.tile(g[2][None], (12, 1)), jnp.tile(g[4][None], (8, 1))])
        gk = jnp.concatenate([jnp.tile(g[1][None], (2, 1)), jnp.tile(g[3][None], (12, 1)), jnp.tile(g[5][None], (8, 1))])
        return (gq.reshape(NQ_CHUNKS // Q_CG, 1, Q_CG * HEAD_DIM), gk.reshape(NK_CHUNKS // K_CG, 1, K_CG * HEAD_DIM))

    def bias_table(i):
        rp = jnp.pad(rpb_c[i].reshape(C_HEADS * C_NREL, C_NCOL), ((0, 0), (0, HEAD_DIM - C_NCOL)))
        t = _exact_mm(rp, expand_t, name="c_bias_expand")
        return _c_bias_tiles(t.reshape(C_HEADS, C_NREL, GRID_W, GRID_W))

    def sink_table(i):
        return jnp.broadcast_to(sink_a[i][:, None, None], (A_Q_HEADS, 1, HEAD_DIM))

    saved = []
    gws = [None] * nl
    xi = x0
    lands0 = gather_place(0)
    first = gather_start(lands0, [0], x0, "_first")
    gws[0] = gather_wait(first, [0], x0, "_first")
    rest0 = gather_start(lands0, REST, gws[0][0], "_rest")
    dep = rest0[4]
    for i in range(nl):
        if i + 1 < nl:
            started = gather_start(gather_place(i + 1), ALL, dep, "")
            dep = started[4]
        gw_in = gws[i][0]
        gq, gk = gain_tables(i)
        bias_t = bias_table(i)
        sink = sink_table(i)
        h1 = _rms_fwd(xi, norm1_g[i][None], dep, name="rms1_fwd")
        proj = _mm_nn(h1, gw_in, 0, out_dtype=F32, name="proj_fwd")
        qn = _qk_fwd(proj, gq, cos, sin, Q_PIECES, NQ_CHUNKS, Q_ROPE_UPTO, Q_CG, name="qnorm_fwd")
        kn = _qk_fwd(proj, gk, cos, sin, K_PIECES, NK_CHUNKS, K_ROPE_UPTO, K_CG, name="knorm_fwd")
        oa, lse_a = _band_fwd(qn, kn, proj, dil=1, radius=A_RADIUS, nkv=A_KV_HEADS, group=A_GROUP,
                              q0=0, k0=0, v0=PC_VA, sink=sink, name="attn_a_fwd")
        obs, lbs = [], []
        for g, (window, dil) in enumerate(B_PATTERNS):
            o_g, l_g = _band_fwd(qn, kn, proj, dil=dil, radius=window // (2 * dil), nkv=B_HG, group=1,
                                 q0=8 + g * B_HG, k0=2 + g * B_HG, v0=PC_VB + g * B_HG, name=f"attn_b{g}_fwd")
            obs.append(o_g)
            lbs.append(l_g)
        ob, lse_b = _combine_b(obs, lbs, name="attn_b_combine")
        oc, lse_c = _c_fwd(qn, kn, proj, bias_t, name="attn_c_fwd")
        if i == 0:
            gws[0] = gws[0] + gather_wait(rest0, REST, oc, "_rest")
        _, gw_a, gw_b, gw_c, gw_o, gw_gu, gw_d = gws[i]
        ta = _mm_nn(oa, gw_a, 0, out_dtype=F32, name="br_a_fwd")
        tb = _mm_nn(ob, gw_b, 0, out_dtype=F32, name="br_b_fwd")
        tc = _mm_nn(oc, gw_c, 0, out_dtype=F32, name="br_c_fwd")
        merged = _gate_fwd(proj, ta, tb, tc, name="gate_fwd")
        x1 = _mm_nn(merged, gw_o, 0, out_dtype=F32, name="wo_fwd", res=xi)
        h2 = _rms_fwd(x1, norm2_g[i][None], x1, name="rms2_fwd")
        gu = _mm_nn(h2, gw_gu, 0, out_dtype=F32, name="gate_up_fwd")
        act = _swiglu_fwd(gu, name="swiglu_fwd")
        x2 = _mm_nn(act, gw_d, 0, out_dtype=F32, name="down_fwd", res=x1)
        saved.append(dict(x=xi, h1=h1, proj=proj, qn=qn, kn=kn, oa=oa, lse_a=lse_a, ob=ob, lse_b=lse_b, oc=oc,
                          lse_c=lse_c, ta=ta, tb=tb, tc=tc, merged=merged, x1=x1, h2=h2, gu=gu, act=act,
                          gq=gq, gk=gk, bias_t=bias_t, sink=sink))
        xi = x2
        dep = x2
        if i + 1 < nl:
            gws[i + 1] = gather_wait(started, ALL, x2, "")

    dx, dxb, loss_row = _loss(xi, tgt, name="loss")

    def scatter_start(grads, sub, after, tag):
        lands = []
        for g, j in zip(grads, sub):
            shape = g.shape if kinds[j] == "col" else (N_DEV, g.shape[1] // N_DEV, g.shape[2])
            lands.append(lax.empty(shape, BF16))
        return _exchange_start("scatter", grads, lands, [kinds[j] for j in sub], after, name="scatter_start" + tag)

    def scatter_wait(pair, after):
        own_a, recv_a = _exchange_wait("scatter", pair[0], [kinds[0]], after, name="scatter_wait_in")
        own_b, recv_b = _exchange_wait("scatter", pair[1], [kinds[j] for j in REST], after, name="scatter_wait_rest")
        return list(recv_a) + list(recv_b), list(own_a) + list(own_b)

    small_grads = [None] * nl
    recv = [None] * nl
    own = [None] * nl
    pending = None
    for i in reversed(range(nl)):
        s = saved[i]
        gw_in, gw_a, gw_b, gw_c, gw_o, gw_gu, gw_d = gws[i]
        dact = _mm_nt(dxb, gw_d, 0, out_dtype=F32, name="down_bwd_x", dep=None if pending is None else pending[0][4])
        g_down = _mm_tn(s["act"], dxb, 1, name="down_bwd_w")
        dgu = _swiglu_bwd(s["gu"], dact, name="swiglu_bwd")
        g_gu = _mm_tn(s["h2"], dgu, N_DEV, name="gate_up_bwd_w")
        dh2 = _mm_nt(dgu, gw_gu, 0, out_dtype=F32, name="gate_up_bwd_x")
        dx1, dx1b, dg2 = _rms_bwd(s["x1"], norm2_g[i][None], dh2, dx, name="rms2_bwd")
        dmerged = _mm_nt(dx1b, gw_o, 0, out_dtype=F32, name="wo_bwd_x")
        g_o = _mm_tn(s["merged"], dx1b, 1, name="wo_bwd_w")
        dta, dtb, dtc, dproj = _gate_bwd(s["proj"], s["ta"], s["tb"], s["tc"], dmerged, name="gate_bwd")
        g_a = _mm_tn(s["oa"], dta, N_DEV, name="br_a_bwd_w")
        g_b = _mm_tn(s["ob"], dtb, N_DEV, name="br_b_bwd_w")
        g_c = _mm_tn(s["oc"], dtc, N_DEV, name="br_c_bwd_w")
        rest = scatter_start([g_a, g_b, g_c, g_o, g_gu, g_down], REST, g_c, "_rest")
        doa = _mm_nt(dta, gw_a, 0, out_dtype=F32, name="br_a_bwd_x", dep=rest[4])
        dob = _mm_nt(dtb, gw_b, 0, out_dtype=F32, name="br_b_bwd_x")
        doc = _mm_nt(dtc, gw_c, 0, out_dtype=F32, name="br_c_bwd_x")
        dq_buf = lax.empty((L, NQ_CHUNKS * HEAD_DIM), F32)
        dk_buf = lax.empty((L, NK_CHUNKS * HEAD_DIM), F32)
        dv_buf = lax.empty((L, NK_CHUNKS * HEAD_DIM), F32)
        dq_buf, dk_buf, dv_buf, dsink = _band_bwd(
            s["qn"], s["kn"], s["proj"], doa, s["oa"], s["lse_a"], dq_buf, dk_buf, dv_buf, dil=1, radius=A_RADIUS,
            nkv=A_KV_HEADS, group=A_GROUP, q0=0, k0=0, v0=PC_VA, o0=0, sink=s["sink"], name="attn_a_bwd")
        for g, (window, dil) in enumerate(B_PATTERNS):
            dq_buf, dk_buf, dv_buf = _band_bwd(
                s["qn"], s["kn"], s["proj"], dob, s["ob"], s["lse_b"], dq_buf, dk_buf, dv_buf, dil=dil,
                radius=window // (2 * dil), nkv=B_HG, group=1, q0=8 + g * B_HG, k0=2 + g * B_HG,
                v0=PC_VB + g * B_HG, o0=0, name=f"attn_b{g}_bwd")
        dq_buf, dk_buf, dv_buf, dbias_t = _c_bwd(s["qn"], s["kn"], s["proj"], s["bias_t"], doc, s["oc"], s["lse_c"],
                                                 dq_buf, dk_buf, dv_buf, name="attn_c_bwd")
        dproj, dgq = _qk_bwd(dq_buf, s["proj"], s["gq"], cos, sin, dproj, Q_PIECES, NQ_CHUNKS, Q_ROPE_UPTO, Q_CG,
                             name="qnorm_bwd")
        dproj, dgk = _qk_bwd(dk_buf, s["proj"], s["gk"], cos, sin, dproj, K_PIECES, NK_CHUNKS, K_ROPE_UPTO, K_CG,
                             name="knorm_bwd")
        dproj = _v_bwd(dv_buf, dproj, name="v_bwd")
        g_in = _mm_tn(s["h1"], dproj, N_DEV, name="proj_bwd_w")
        dh1 = _mm_nt(dproj, gw_in, 0, out_dtype=F32, name="proj_bwd_x")
        dx, dxb, dg1 = _rms_bwd(s["x"], norm1_g[i][None], dh1, dx1, name="rms1_bwd")
        if pending is not None:
            recv[i + 1], own[i + 1] = scatter_wait(pending, dx)
        pending = (scatter_start([g_in], [0], dx, "_in"), rest)

        drpb = _exact_mm(dbias_t.reshape(C_HEADS * C_NREL, GRID_W * GRID_W), expand, name="c_bias_reduce")
        dgq, dgk = dgq.reshape(NQ_CHUNKS, HEAD_DIM), dgk.reshape(NK_CHUNKS, HEAD_DIM)
        dqk_g = jnp.stack([dgq[0:8].sum(0), dgk[0:2].sum(0), dgq[8:20].sum(0), dgk[2:14].sum(0),
                           dgq[20:28].sum(0), dgk[14:22].sum(0)])
        small_grads[i] = (dg1.reshape(D), dqk_g, dsink[:, 0, 0],
                          drpb[:, :C_NCOL].reshape(C_HEADS, C_NREL, C_NCOL), dg2.reshape(D))

    small_names = [norm1_g, qk_norm_g, sink_a, rpb_c, norm2_g]
    small_m = [m_norm1_g, m_qk_norm_g, m_sink_a, m_rpb_c, m_norm2_g]
    small_v = [v_norm1_g, v_qk_norm_g, v_sink_a, v_rpb_c, v_norm2_g]
    shapes = [a.shape for a in small_names]
    total = sum(int(np.prod(sh)) for sh in shapes) + 128
    rows = -(-total // 1024) * 8
    stacked = [jnp.stack([small_grads[i][j] for i in range(nl)]) for j in range(5)]
    packed = _pack([loss_row.reshape(-1)] + stacked, rows)
    summed = _all_reduce_small(packed)
    loss = summed[0, 0]
    zero_row = jnp.zeros((128,), F32)
    d_s, m_s, v_s = _adamw_small(summed, _pack([zero_row] + small_names, rows), _pack([zero_row] + small_m, rows),
                                 _pack([zero_row] + small_v, rows))
    shapes1 = [(128,)] + shapes
    g_small = _unpack(summed, shapes1)[1:]
    d_small = _unpack(d_s, shapes1)[1:]
    m_small = _unpack(m_s, shapes1)[1:]
    v_small = _unpack(v_s, shapes1)[1:]

    big_m = [m_w_in, m_w_br_a, m_w_br_b, m_w_br_c, m_w_o, m_w_gate_up, m_w_down]
    big_v = [v_w_in, v_w_br_a, v_w_br_b, v_w_br_c, v_w_o, v_w_gate_up, v_w_down]
    big_out = [[lax.empty(w.shape, F32) for _ in range(4)] for w in big]
    token = pending[0][4]
    for i in list(range(nl - 1, 0, -1)) + [0]:
        if i == 0:
            recv[0], own[0] = scatter_wait(pending, big_out[-1][0] if nl > 1 else dx)
        for j in range(len(big)):
            big_out[j] = _adamw_layer(recv[i][j], own[i][j], kinds[j], big[j], big_m[j], big_v[j], big_out[j], i,
                                      token, name="adamw_" + big_names[j])

    order = ["norm1_g", "w_in", "qk_norm_g", "sink_a", "rpb_c", "w_br_a", "w_br_b", "w_br_c", "w_o", "norm2_g",
             "w_gate_up", "w_down"]
    small_idx = {"norm1_g": 0, "qk_norm_g": 1, "sink_a": 2, "rpb_c": 3, "norm2_g": 4}
    big_idx = {n: j for j, n in enumerate(big_names)}

    def pick(kind):
        out = []
        for n in order:
            if n in small_idx:
                out.append([g_small, d_small, m_small, v_small][kind][small_idx[n]])
            else:
                out.append(big_out[big_idx[n]][kind])
        return out

    return (loss, dx.reshape(1, L, D), *pick(0), *pick(1), *pick(2), *pick(3))
```

```python
import functools
import math

import numpy as np
import jax
import jax.numpy as jnp
from jax import lax
from jax.experimental import pallas as pl
from jax.experimental.pallas import tpu as pltpu

F32 = jnp.float32
BF16 = jnp.bfloat16
MESH = pl.DeviceIdType.MESH
N_DEV = 8

HEAD_DIM = 128
NORM_EPS = 1e-6
ROPE_THETA = 10000.0
ATT_SCALE = HEAD_DIM ** -0.5
NEG = -1e30

A_Q_HEADS, A_KV_HEADS, A_RADIUS = 8, 2, 128
A_GROUP = A_Q_HEADS // A_KV_HEADS
B_PATTERNS = ((128, 1), (512, 4), (2048, 16))
B_HG = 4
B_HEADS = len(B_PATTERNS) * B_HG
C_HEADS, GRID_W, C_WIN_ROWS, C_WIN_COLS = 8, 64, 8, 16
C_NREL = 2 * C_WIN_ROWS - 1
C_NCOL = 2 * C_WIN_COLS - 1

PC_QA, PC_KA, PC_VA = 0, 8, 10
PC_QB, PC_KB, PC_VB = 12, 24, 36
PC_QC, PC_KC, PC_VC = 48, 56, 64
N_QKV_CHUNKS = 72
Q_PIECES = ((0, 8, PC_QA), (8, 12, PC_QB), (20, 8, PC_QC))
K_PIECES = ((0, 2, PC_KA), (2, 12, PC_KB), (14, 8, PC_KC))
V_PIECES = ((0, 2, PC_VA), (2, 12, PC_VB), (14, 8, PC_VC))
NQ_CHUNKS, NK_CHUNKS = 28, 22
Q_ROPE_UPTO, K_ROPE_UPTO = 20, 14
Q_CG, K_CG = 4, 2

ADAM_LR, ADAM_B1, ADAM_B2, ADAM_EPS, ADAM_WD, ADAM_STEP = 0.001, 0.9, 0.999, 1e-08, 0.01, 10

VMEM_LIMIT = 48 * 1024 * 1024


def _tile(dim, prefs):
    for p in prefs:
        if dim % p == 0:
            return p
    return dim


def _params(sem, **kw):
    return pltpu.CompilerParams(dimension_semantics=sem, vmem_limit_bytes=VMEM_LIMIT, **kw)


def _piece_map(pieces):
    def f(c):
        out = c - pieces[0][0] + pieces[0][2]
        for first, _, pfirst in pieces[1:]:
            out = jnp.where(c >= first, c - first + pfirst, out)
        return out
    return f


def _mm_nn(a, w, layer, *, out_dtype, name, res=None):
    M, K = a.shape
    nb, _, Kw, ns = w.shape
    assert Kw == K
    tm = _tile(M, (512, 256))
    tn = _tile(ns, (640, 512, 256, 128)) if ns > 1408 else ns
    tk = _tile(K, (2048, 1408, 1024, 512, 256))
    nj, nk = ns // tn, K // tk

    def body(*refs):
        if res is None:
            a_ref, w_ref, o_ref, acc_ref = refs
            r_ref = None
        else:
            a_ref, w_ref, r_ref, o_ref, acc_ref = refs
        k = pl.program_id(3)
        part = jnp.dot(a_ref[...].astype(BF16), w_ref[...], preferred_element_type=F32)

        @pl.when(k == 0)
        def _():
            acc_ref[...] = part

        @pl.when(k > 0)
        def _():
            acc_ref[...] += part

        @pl.when(k == nk - 1)
        def _():
            r = acc_ref[...]
            if r_ref is not None:
                r = r + r_ref[...]
            o_ref[...] = r.astype(out_dtype)

    in_specs = [pl.BlockSpec((tm, tk), lambda i, b, j, k: (i, k)),
                pl.BlockSpec((None, None, tk, tn), lambda i, b, j, k: (b, layer, k, j))]
    args = [a, w]
    if res is not None:
        in_specs.append(pl.BlockSpec((tm, tn), lambda i, b, j, k: (i, b * nj + j)))
        args.append(res)
    return pl.pallas_call(
        body, name=name, grid=(M // tm, nb, nj, nk), in_specs=in_specs,
        out_specs=pl.BlockSpec((tm, tn), lambda i, b, j, k: (i, b * nj + j)),
        out_shape=jax.ShapeDtypeStruct((M, nb * ns), out_dtype),
        scratch_shapes=[pltpu.VMEM((tm, tn), F32)],
        compiler_params=_params(("parallel", "parallel", "parallel", "arbitrary")),
    )(*args)


def _mm_nt(a, w, layer, *, out_dtype, name, dep=None):
    M, N = a.shape
    nb, _, K, ns = w.shape
    assert N == nb * ns
    tm = _tile(M, (1024, 512, 256))
    tn = _tile(ns, (640, 512, 256, 128)) if ns > 1408 else ns
    tk = _tile(K, (1024, 512, 256))
    nj = ns // tn
    nred = nb * nj

    def body(*refs):
        a_ref, w_ref = refs[:2]
        o_ref, acc_ref = refs[-2:]
        s = pl.program_id(2) * nj + pl.program_id(3)
        part = lax.dot_general(a_ref[...].astype(BF16), w_ref[...], (((1,), (1,)), ((), ())),
                               preferred_element_type=F32)

        @pl.when(s == 0)
        def _():
            acc_ref[...] = part

        @pl.when(s > 0)
        def _():
            acc_ref[...] += part

        @pl.when(s == nred - 1)
        def _():
            o_ref[...] = acc_ref[...].astype(out_dtype)

    in_specs = [pl.BlockSpec((tm, tn), lambda i, kk, b, j: (i, b * nj + j)),
                pl.BlockSpec((None, None, tk, tn), lambda i, kk, b, j: (b, layer, kk, j))]
    args = [a, w]
    if dep is not None:
        in_specs.append(ANY_SPEC)
        args.append(dep)
    return pl.pallas_call(
        body, name=name, grid=(M // tm, K // tk, nb, nj), in_specs=in_specs,
        out_specs=pl.BlockSpec((tm, tk), lambda i, kk, b, j: (i, kk)),
        out_shape=jax.ShapeDtypeStruct((M, K), out_dtype),
        scratch_shapes=[pltpu.VMEM((tm, tk), F32)],
        compiler_params=_params(("parallel", "parallel", "arbitrary", "arbitrary")),
    )(*args)


def _mm_tn(a, g, nb, *, name):
    M, Ka = a.shape
    N = g.shape[1]
    ns = N // nb
    tka = _tile(Ka, (1024, 512, 256))
    tn = _tile(ns, (640, 512, 256, 128)) if ns > 1408 else ns
    tm = _tile(M, (1024, 512, 256))
    nj, nm = ns // tn, M // tm

    def body(a_ref, g_ref, o_ref, acc_ref):
        m = pl.program_id(3)
        part = lax.dot_general(a_ref[...].astype(BF16), g_ref[...].astype(BF16), (((0,), (0,)), ((), ())),
                               preferred_element_type=F32)

        @pl.when(m == 0)
        def _():
            acc_ref[...] = part

        @pl.when(m > 0)
        def _():
            acc_ref[...] += part

        @pl.when(m == nm - 1)
        def _():
            o_ref[...] = acc_ref[...].astype(BF16)

    return pl.pallas_call(
        body, name=name, grid=(Ka // tka, nb, nj, nm),
        in_specs=[pl.BlockSpec((tm, tka), lambda ka, b, j, m: (m, ka)),
                  pl.BlockSpec((tm, tn), lambda ka, b, j, m: (m, b * nj + j))],
        out_specs=pl.BlockSpec((None, tka, tn), lambda ka, b, j, m: (b, ka, j)),
        out_shape=jax.ShapeDtypeStruct((nb, Ka, ns), BF16),
        scratch_shapes=[pltpu.VMEM((tka, tn), F32)],
        compiler_params=_params(("parallel", "parallel", "parallel", "arbitrary")),
    )(a, g)


def _exact_mm(a, e, *, name):
    R, K = a.shape
    N = e.shape[1]

    def body(a_ref, e_ref, o_ref):
        x = a_ref[...]
        hi = x.astype(BF16)
        r1 = x - hi.astype(F32)
        mid = r1.astype(BF16)
        lo = (r1 - mid.astype(F32)).astype(BF16)
        ev = e_ref[...]
        o_ref[...] = (jnp.dot(hi, ev, preferred_element_type=F32) + jnp.dot(mid, ev, preferred_element_type=F32)
                      + jnp.dot(lo, ev, preferred_element_type=F32))

    return pl.pallas_call(body, name=name, out_shape=jax.ShapeDtypeStruct((R, N), F32),
                          compiler_params=pltpu.CompilerParams(vmem_limit_bytes=VMEM_LIMIT))(a, e)


def _rms_fwd(x, g, dep, *, name):
    L, D = x.shape
    tl = _tile(L, (256, 128))

    def body(x_ref, g_ref, _dep, h_ref):
        xv = x_ref[...]
        rstd = lax.rsqrt(jnp.mean(xv * xv, axis=-1, keepdims=True) + NORM_EPS)
        h_ref[...] = (xv * rstd * g_ref[...]).astype(BF16)

    return pl.pallas_call(
        body, name=name, grid=(L // tl,),
        in_specs=[pl.BlockSpec((tl, D), lambda t: (t, 0)), pl.BlockSpec((1, D), lambda t: (0, 0)), ANY_SPEC],
        out_specs=pl.BlockSpec((tl, D), lambda t: (t, 0)),
        out_shape=jax.ShapeDtypeStruct((L, D), BF16),
        compiler_params=_params(("parallel",)),
    )(x, g, dep)


def _rms_bwd(x, g, dy, dres, *, name):
    L, D = x.shape
    tl = _tile(L, (128,))

    def body(x_ref, g_ref, dy_ref, dres_ref, dx_ref, dxb_ref, dg_ref):
        t = pl.program_id(0)
        xv = x_ref[...]
        rstd = lax.rsqrt(jnp.mean(xv * xv, axis=-1, keepdims=True) + NORM_EPS)
        xhat = xv * rstd
        dyv = dy_ref[...]
        dxhat = dyv * g_ref[...]
        c = jnp.mean(dxhat * xhat, axis=-1, keepdims=True)
        dx = dres_ref[...] + rstd * (dxhat - xhat * c)
        dx_ref[...] = dx
        dxb_ref[...] = dx.astype(BF16)
        dgp = jnp.sum(dyv * xhat, axis=0, keepdims=True)

        @pl.when(t == 0)
        def _():
            dg_ref[...] = dgp

        @pl.when(t > 0)
        def _():
            dg_ref[...] += dgp

    row = pl.BlockSpec((tl, D), lambda t: (t, 0))
    vec = pl.BlockSpec((1, D), lambda t: (0, 0))
    return pl.pallas_call(
        body, name=name, grid=(L // tl,), in_specs=[row, vec, row, row], out_specs=[row, row, vec],
        out_shape=[jax.ShapeDtypeStruct((L, D), F32), jax.ShapeDtypeStruct((L, D), BF16),
                   jax.ShapeDtypeStruct((1, D), F32)],
        compiler_params=_params(("arbitrary",)),
    )(x, g, dy, dres)


def _gate_fwd(proj, ta, tb, tc, *, name):
    L, D = ta.shape
    tl, tcw = _tile(L, (256, 128)), _tile(D, (512, 256, 128))
    off = N_QKV_CHUNKS * HEAD_DIM // tcw
    nd = D // tcw

    def body(g0, g1, g2, a_ref, b_ref, c_ref, o_ref):
        m = (jax.nn.sigmoid(g0[...]) * a_ref[...] + jax.nn.sigmoid(g1[...]) * b_ref[...]
             + jax.nn.sigmoid(g2[...]) * c_ref[...])
        o_ref[...] = m.astype(BF16)

    blk = pl.BlockSpec((tl, tcw), lambda t, j: (t, j))
    gl = [pl.BlockSpec((tl, tcw), functools.partial(lambda t, j, i: (t, off + i * nd + j), i=i)) for i in range(3)]
    return pl.pallas_call(
        body, name=name, grid=(L // tl, nd), in_specs=gl + [blk, blk, blk], out_specs=blk,
        out_shape=jax.ShapeDtypeStruct((L, D), BF16),
        compiler_params=_params(("parallel", "parallel")),
    )(proj, proj, proj, ta, tb, tc)


def _gate_bwd(proj, ta, tb, tc, dmerged, *, name):
    L, D = ta.shape
    ncols = proj.shape[1]
    tl, tcw = _tile(L, (256, 128)), _tile(D, (512, 256, 128))
    off = N_QKV_CHUNKS * HEAD_DIM // tcw
    nd = D // tcw

    def body(g0, g1, g2, a_ref, b_ref, c_ref, dm_ref, da_ref, db_ref, dc_ref, dgl_ref):
        i = pl.program_id(2)
        sg = jax.nn.sigmoid(jnp.where(i == 0, g0[...], jnp.where(i == 1, g1[...], g2[...])))
        sel_t = jnp.where(i == 0, a_ref[...], jnp.where(i == 1, b_ref[...], c_ref[...]))
        dt = dm_ref[...] * sg
        dtb = dt.astype(BF16)

        @pl.when(i == 0)
        def _():
            da_ref[...] = dtb

        @pl.when(i == 1)
        def _():
            db_ref[...] = dtb

        @pl.when(i == 2)
        def _():
            dc_ref[...] = dtb

        dgl_ref[...] = (dt * sel_t * (1.0 - sg)).astype(BF16)

    blk = pl.BlockSpec((tl, tcw), lambda t, j, i: (t, j))
    gl = [pl.BlockSpec((tl, tcw), functools.partial(lambda t, j, i, q: (t, off + q * nd + j), q=q)) for q in range(3)]
    return pl.pallas_call(
        body, name=name, grid=(L // tl, nd, 3), in_specs=gl + [blk, blk, blk, blk],
        out_specs=[blk, blk, blk, pl.BlockSpec((tl, tcw), lambda t, j, i: (t, off + i * nd + j))],
        out_shape=[jax.ShapeDtypeStruct((L, D), BF16)] * 3 + [jax.ShapeDtypeStruct((L, ncols), BF16)],
        compiler_params=_params(("parallel", "parallel", "arbitrary")),
    )(proj, proj, proj, ta, tb, tc, dmerged)


def _swiglu_fwd(gu, *, name):
    L, F2 = gu.shape
    F = F2 // 2
    tl, tcw = _tile(L, (256, 128)), _tile(F, (512, 256, 128))
    nf = F // tcw

    def body(g_ref, u_ref, o_ref):
        gt = g_ref[...]
        o_ref[...] = (gt * jax.nn.sigmoid(gt) * u_ref[...]).astype(BF16)

    return pl.pallas_call(
        body, name=name, grid=(L // tl, nf),
        in_specs=[pl.BlockSpec((tl, tcw), lambda t, j: (t, j)), pl.BlockSpec((tl, tcw), lambda t, j: (t, nf + j))],
        out_specs=pl.BlockSpec((tl, tcw), lambda t, j: (t, j)),
        out_shape=jax.ShapeDtypeStruct((L, F), BF16),
        compiler_params=_params(("parallel", "parallel")),
    )(gu, gu)


def _swiglu_bwd(gu, dact, *, name):
    L, F2 = gu.shape
    F = F2 // 2
    tl = _tile(L, (64,))

    def body(gu_ref, d_ref, o_ref):
        gt, up, d = gu_ref[:, :F], gu_ref[:, F:], d_ref[...]
        sg = jax.nn.sigmoid(gt)
        o_ref[:, :F] = (d * up * sg * (1.0 + gt * (1.0 - sg))).astype(BF16)
        o_ref[:, F:] = (d * gt * sg).astype(BF16)

    return pl.pallas_call(
        body, name=name, grid=(L // tl,),
        in_specs=[pl.BlockSpec((tl, F2), lambda t: (t, 0)), pl.BlockSpec((tl, F), lambda t: (t, 0))],
        out_specs=pl.BlockSpec((tl, F2), lambda t: (t, 0)),
        out_shape=jax.ShapeDtypeStruct((L, F2), BF16),
        compiler_params=_params(("parallel",)),
    )(gu, dact)


def _loss(y, tgt, *, name):
    L, D = y.shape
    tl = _tile(L, (256, 128))
    nt = L // tl

    def body(y_ref, t_ref, dy_ref, dyb_ref, loss_ref, acc_ref):
        t = pl.program_id(0)
        e = y_ref[...] - t_ref[...]
        dy = e * (1.0 / D)
        dy_ref[...] = dy
        dyb_ref[...] = dy.astype(BF16)
        part = jnp.sum(e * e, axis=0, keepdims=True)

        @pl.when(t == 0)
        def _():
            acc_ref[...] = part

        @pl.when(t > 0)
        def _():
            acc_ref[...] += part

        @pl.when(t == nt - 1)
        def _():
            loss_ref[...] = jnp.broadcast_to(jnp.sum(acc_ref[...], axis=-1, keepdims=True) * (0.5 / D), (1, 128))

    row = pl.BlockSpec((tl, D), lambda t: (t, 0))
    return pl.pallas_call(
        body, name=name, grid=(nt,), in_specs=[row, row],
        out_specs=[row, row, pl.BlockSpec((1, 128), lambda t: (0, 0))],
        out_shape=[jax.ShapeDtypeStruct((L, D), F32), jax.ShapeDtypeStruct((L, D), BF16),
                   jax.ShapeDtypeStruct((1, 128), F32)],
        scratch_shapes=[pltpu.VMEM((1, D), F32)],
        compiler_params=_params(("arbitrary",)),
    )(y, tgt)


def _rope(v, cos, sin_signed):
    return v * cos + pltpu.roll(v, HEAD_DIM // 2, 1) * sin_signed


def _qk_fwd(proj, gtab, cos, sin, pieces, nchunks, rope_upto, cg, *, name):
    L = proj.shape[0]
    tl = _tile(L, (512, 256, 128))
    W = cg * HEAD_DIM
    pmap = _piece_map(tuple((a // cg, n // cg, p // cg) for a, n, p in pieces))

    def body(p_ref, g_ref, cos_ref, sin_ref, o_ref):
        c = pl.program_id(1)

        def norm(j):
            cols = slice(j * HEAD_DIM, (j + 1) * HEAD_DIM)
            x = p_ref[:, cols]
            rstd = lax.rsqrt(jnp.mean(x * x, axis=-1, keepdims=True) + NORM_EPS)
            return cols, x * rstd * g_ref[:, cols]

        @pl.when(c < rope_upto // cg)
        def _():
            for j in range(cg):
                cols, y = norm(j)
                o_ref[:, cols] = _rope(y, cos_ref[...], sin_ref[...])

        @pl.when(c >= rope_upto // cg)
        def _():
            for j in range(cg):
                cols, y = norm(j)
                o_ref[:, cols] = y

    pos = pl.BlockSpec((tl, HEAD_DIM), lambda t, c: (t, 0))
    return pl.pallas_call(
        body, name=name, grid=(L // tl, nchunks // cg),
        in_specs=[pl.BlockSpec((tl, W), lambda t, c: (t, pmap(c))),
                  pl.BlockSpec((None, 1, W), lambda t, c: (c, 0, 0)), pos, pos],
        out_specs=pl.BlockSpec((tl, W), lambda t, c: (t, c)),
        out_shape=jax.ShapeDtypeStruct((L, nchunks * HEAD_DIM), F32),
        compiler_params=_params(("parallel", "parallel")),
    )(proj, gtab, cos, sin)


def _qk_bwd(dqk, proj, gtab, cos, sin, dproj, pieces, nchunks, rope_upto, cg, *, name):
    L = proj.shape[0]
    tl = _tile(L, (512, 256, 128))
    W = cg * HEAD_DIM
    pmap = _piece_map(tuple((a // cg, n // cg, p // cg) for a, n, p in pieces))

    def body(d_ref, p_ref, g_ref, cos_ref, sin_ref, _, o_ref, dg_ref):
        c, t = pl.program_id(0), pl.program_id(1)

        @pl.when(t == 0)
        def _():
            dg_ref[...] = jnp.zeros_like(dg_ref)

        for j in range(cg):
            cols = slice(j * HEAD_DIM, (j + 1) * HEAD_DIM)
            x = p_ref[:, cols]
            rstd = lax.rsqrt(jnp.mean(x * x, axis=-1, keepdims=True) + NORM_EPS)
            xhat = x * rstd
            dy = d_ref[:, cols]
            dy = jnp.where(c < rope_upto // cg, _rope(dy, cos_ref[...], -sin_ref[...]), dy)
            dxhat = dy * g_ref[:, cols]
            cm = jnp.mean(dxhat * xhat, axis=-1, keepdims=True)
            o_ref[:, cols] = (rstd * (dxhat - xhat * cm)).astype(BF16)
            dg_ref[:, cols] += jnp.sum(dy * xhat, axis=0, keepdims=True)

    pos = pl.BlockSpec((tl, HEAD_DIM), lambda c, t: (t, 0))
    gspec = pl.BlockSpec((None, 1, W), lambda c, t: (c, 0, 0))
    out, dg = pl.pallas_call(
        body, name=name, grid=(nchunks // cg, L // tl),
        in_specs=[pl.BlockSpec((tl, W), lambda c, t: (t, c)),
                  pl.BlockSpec((tl, W), lambda c, t: (t, pmap(c))), gspec, pos, pos,
                  pl.BlockSpec(memory_space=pl.ANY)],
        out_specs=[pl.BlockSpec((tl, W), lambda c, t: (t, pmap(c))), gspec],
        out_shape=[jax.ShapeDtypeStruct(dproj.shape, BF16), jax.ShapeDtypeStruct((nchunks // cg, 1, W), F32)],
        input_output_aliases={5: 0},
        compiler_params=_params(("parallel", "arbitrary")),
    )(dqk, proj, gtab, cos, sin, dproj)
    return out, dg


def _v_bwd(dv, dproj, *, name):
    L = dv.shape[0]
    tl = _tile(L, (512, 256, 128))
    pmap = _piece_map(tuple((a // 2, n // 2, p // 2) for a, n, p in V_PIECES))

    def body(d_ref, _, o_ref):
        o_ref[...] = d_ref[...].astype(BF16)

    return pl.pallas_call(
        body, name=name, grid=(L // tl, NK_CHUNKS // 2),
        in_specs=[pl.BlockSpec((tl, 2 * HEAD_DIM), lambda t, c: (t, c)), pl.BlockSpec(memory_space=pl.ANY)],
        out_specs=pl.BlockSpec((tl, 2 * HEAD_DIM), lambda t, c: (t, pmap(c))),
        out_shape=jax.ShapeDtypeStruct(dproj.shape, BF16),
        input_output_aliases={1: 0},
        compiler_params=_params(("parallel", "parallel")),
    )(dv, dproj)


def _band_geometry(L, dil, radius):
    n = L // dil
    bq = min(256, max(n // 2, 64), n)
    width = min(bq + 2 * radius, n)
    return n, bq, width


def _band_rows(dil, r, first, count):
    if dil == 1:
        return pl.ds(pl.multiple_of(first, 8), count)
    return pl.ds(r + first * dil, count, stride=dil)


def _band_mask(i, bq, width, radius, ws):
    qpos = i * bq + lax.broadcasted_iota(jnp.int32, (bq, width), 0)
    kpos = ws + lax.broadcasted_iota(jnp.int32, (bq, width), 1)
    return jnp.abs(kpos - qpos) <= radius


def _band_fwd(qn, kn, proj, *, dil, radius, nkv, group, q0, k0, v0, sink=None, name):
    L = qn.shape[0]
    n, bq, width = _band_geometry(L, dil, radius)
    tq = bq * dil
    nh = nkv * group

    def body(*refs):
        if sink is None:
            q_ref, k_ref, v_ref, o_ref, lse_ref = refs
        else:
            q_ref, k_ref, v_ref, s_ref, o_ref, lse_ref = refs
        i = pl.program_id(2)
        ws = jnp.clip(i * bq - radius, 0, n - width)
        valid = _band_mask(i, bq, width, radius, ws)

        def one(r, carry):
            qrows = _band_rows(dil, r, 0, bq)
            krows = _band_rows(dil, r, ws, width)
            q = q_ref[qrows, :].astype(BF16)
            k = k_ref[krows, :].astype(BF16)
            v = v_ref[krows, :].astype(BF16)
            s = lax.dot_general(q, k, (((1,), (1,)), ((), ())), preferred_element_type=F32) * ATT_SCALE
            s = jnp.where(valid, s, NEG)
            m = jnp.max(s, axis=-1, keepdims=True)
            if sink is not None:
                m = jnp.maximum(m, s_ref[...][:, :1])
            p = jnp.exp(s - m)
            denom = jnp.sum(p, axis=-1, keepdims=True)
            if sink is not None:
                denom = denom + jnp.exp(s_ref[...][:, :1] - m)
            pn = (p / denom).astype(BF16)
            o_ref[qrows, :] = jnp.dot(pn, v, preferred_element_type=F32)
            lse_ref[qrows, :] = jnp.broadcast_to(m + jnp.log(denom), (bq, HEAD_DIM))
            return carry

        if dil == 1:
            one(0, 0)
        else:
            lax.fori_loop(0, dil, one, 0)

    qspec = pl.BlockSpec((tq, HEAD_DIM), lambda hk, g, i: (i, q0 + hk * group + g))
    in_specs = [qspec,
                pl.BlockSpec((L, HEAD_DIM), lambda hk, g, i: (0, k0 + hk)),
                pl.BlockSpec((L, HEAD_DIM), lambda hk, g, i: (0, v0 + hk))]
    args = [qn, kn, proj]
    if sink is not None:
        in_specs.append(pl.BlockSpec((None, 1, HEAD_DIM), lambda hk, g, i: (hk * group + g, 0, 0)))
        args.append(sink)
    ospec = pl.BlockSpec((tq, HEAD_DIM), lambda hk, g, i: (i, hk * group + g))
    return pl.pallas_call(
        body, name=name, grid=(nkv, group, n // bq), in_specs=in_specs, out_specs=[ospec, ospec],
        out_shape=[jax.ShapeDtypeStruct((L, nh * HEAD_DIM), F32)] * 2,
        compiler_params=_params(("parallel", "parallel", "arbitrary")),
    )(*args)


def _band_bwd(qn, kn, proj, do, o, lse, dq_buf, dk_buf, dv_buf, *, dil, radius, nkv, group, q0, k0, v0, o0,
              sink=None, name):
    L = qn.shape[0]
    n, bq, width = _band_geometry(L, dil, radius)
    tq = bq * dil
    nh = nkv * group
    n_in = 6 + (1 if sink is not None else 0)

    def body(*refs):
        q_ref, k_ref, v_ref, do_ref, o_ref, lse_ref = refs[:6]
        s_ref = refs[6] if sink is not None else None
        outs = refs[n_in + 3:]
        dq_ref, dk_ref, dv_ref = outs[:3]
        ds_ref = outs[3] if sink is not None else None
        g, i = pl.program_id(1), pl.program_id(2)
        ws = jnp.clip(i * bq - radius, 0, n - width)
        valid = _band_mask(i, bq, width, radius, ws)

        @pl.when((g == 0) & (i == 0))
        def _():
            dk_ref[...] = jnp.zeros_like(dk_ref)
            dv_ref[...] = jnp.zeros_like(dv_ref)

        if sink is not None:
            @pl.when(i == 0)
            def _():
                ds_ref[...] = jnp.zeros_like(ds_ref)

        def one(r, carry):
            qrows = _band_rows(dil, r, 0, bq)
            krows = _band_rows(dil, r, ws, width)
            q = q_ref[qrows, :].astype(BF16)
            k = k_ref[krows, :].astype(BF16)
            v = v_ref[krows, :].astype(BF16)
            dov = do_ref[qrows, :]
            lse_v = lse_ref[qrows, :][:, :1]
            delta = jnp.sum(dov * o_ref[qrows, :], axis=-1, keepdims=True)
            dob = dov.astype(BF16)
            s = lax.dot_general(q, k, (((1,), (1,)), ((), ())), preferred_element_type=F32) * ATT_SCALE
            p = jnp.where(valid, jnp.exp(s - lse_v), 0.0)
            dp = lax.dot_general(dob, v, (((1,), (1,)), ((), ())), preferred_element_type=F32)
            dsb = (p * (dp - delta)).astype(BF16)
            dq_ref[qrows, :] = jnp.dot(dsb, k, preferred_element_type=F32) * ATT_SCALE
            dk_ref[krows, :] += lax.dot_general(dsb, q, (((0,), (0,)), ((), ())),
                                                preferred_element_type=F32) * ATT_SCALE
            dv_ref[krows, :] += lax.dot_general(p.astype(BF16), dob, (((0,), (0,)), ((), ())),
                                                preferred_element_type=F32)
            if sink is not None:
                ps = jnp.exp(s_ref[...][:, :1] - lse_v)
                ds_ref[...] += jnp.broadcast_to(jnp.sum(-ps * delta, axis=0, keepdims=True), (1, HEAD_DIM))
            return carry

        if dil == 1:
            one(0, 0)
        else:
            lax.fori_loop(0, dil, one, 0)

    hspec = pl.BlockSpec((tq, HEAD_DIM), lambda hk, g, i: (i, o0 + hk * group + g))
    qspec = pl.BlockSpec((tq, HEAD_DIM), lambda hk, g, i: (i, q0 + hk * group + g))
    kspec = pl.BlockSpec((L, HEAD_DIM), lambda hk, g, i: (0, k0 + hk))
    any_spec = pl.BlockSpec(memory_space=pl.ANY)
    in_specs = [qspec, kspec, pl.BlockSpec((L, HEAD_DIM), lambda hk, g, i: (0, v0 + hk)), hspec, hspec, hspec]
    args = [qn, kn, proj, do, o, lse]
    if sink is not None:
        in_specs.append(pl.BlockSpec((None, 1, HEAD_DIM), lambda hk, g, i: (hk * group + g, 0, 0)))
        args.append(sink)
    in_specs += [any_spec] * 3
    args += [dq_buf, dk_buf, dv_buf]
    out_specs = [qspec, kspec, kspec]
    out_shape = [jax.ShapeDtypeStruct(dq_buf.shape, F32), jax.ShapeDtypeStruct(dk_buf.shape, F32),
                 jax.ShapeDtypeStruct(dv_buf.shape, F32)]
    if sink is not None:
        out_specs.append(pl.BlockSpec((None, 1, HEAD_DIM), lambda hk, g, i: (hk * group + g, 0, 0)))
        out_shape.append(jax.ShapeDtypeStruct((nh, 1, HEAD_DIM), F32))
    return pl.pallas_call(
        body, name=name, grid=(nkv, group, n // bq), in_specs=in_specs, out_specs=out_specs, out_shape=out_shape,
        input_output_aliases={n_in: 0, n_in + 1: 1, n_in + 2: 2},
        compiler_params=_params(("parallel", "arbitrary", "arbitrary")),
    )(*args)


def _combine_b(os_, lses, *, name):
    L, W = os_[0].shape
    tl = _tile(L, (256, 128))

    def body(o0, o1, o2, l0, l1, l2, out_ref, lt_ref):
        a, b, c = l0[...], l1[...], l2[...]
        m = jnp.maximum(jnp.maximum(a, b), c)
        ea, eb, ec = jnp.exp(a - m), jnp.exp(b - m), jnp.exp(c - m)
        tot = ea + eb + ec
        out_ref[...] = (ea * o0[...] + eb * o1[...] + ec * o2[...]) / tot
        lt_ref[...] = m + jnp.log(tot)

    blk = pl.BlockSpec((tl, W), lambda t: (t, 0))
    return pl.pallas_call(
        body, name=name, grid=(L // tl,), in_specs=[blk] * 6, out_specs=[blk, blk],
        out_shape=[jax.ShapeDtypeStruct((L, W), F32)] * 2, compiler_params=_params(("parallel",)),
    )(*os_, *lses)


C_QROWS = 4
C_KROWS = C_QROWS + C_WIN_ROWS
C_QUERIES, C_KEYS = C_QROWS * GRID_W, C_KROWS * GRID_W
_C_KIND_OFFSETS = (C_WIN_ROWS - 1, C_WIN_ROWS - 1 - C_WIN_ROWS // 2, C_WIN_ROWS - 1 - (C_KROWS - C_QROWS))


def _c_geometry(L):
    rows = L // GRID_W
    assert rows >= C_KROWS and rows % C_QROWS == 0
    return rows


def _c_bias_tiles(bias_t):
    cq = np.arange(GRID_W)[:, None]
    ck = np.arange(GRID_W)[None, :]
    start = np.clip(cq - C_WIN_COLS // 2, 0, GRID_W - C_WIN_COLS)
    masked = jnp.where(jnp.asarray((ck >= start) & (ck < start + C_WIN_COLS)), bias_t, NEG)
    blank = jnp.full((C_HEADS, GRID_W, GRID_W), NEG, F32)
    kinds = []
    for kind in range(3):
        off = _C_KIND_OFFSETS[kind]
        row_blocks = []
        for a in range(C_QROWS):
            lo = (0, a, C_KROWS - C_WIN_ROWS)[kind]
            row_blocks.append(jnp.concatenate(
                [masked[:, b - a + off] if lo <= b < lo + C_WIN_ROWS else blank for b in range(C_KROWS)], axis=-1))
        kinds.append(jnp.concatenate(row_blocks, axis=-2))
    return jnp.stack(kinds, axis=1)


def _c_block(g, rows):
    r0 = g * C_QROWS
    k0 = jnp.clip(r0 - C_WIN_ROWS // 2, 0, rows - C_KROWS)
    kind = jnp.where(g == 0, 0, jnp.where(g == rows // C_QROWS - 1, 2, 1))
    return k0, kind, k0 - r0 + (C_WIN_ROWS - 1)


def _c_fwd(qn, kn, proj, tiles, *, name):
    L = qn.shape[0]
    rows = _c_geometry(L)

    def body(q_ref, k_ref, v_ref, t_ref, o_ref, lse_ref):
        k0, _, _ = _c_block(pl.program_id(1), rows)
        krows = pl.ds(pl.multiple_of(k0 * GRID_W, GRID_W), C_KEYS)
        q = q_ref[...].astype(BF16)
        k = k_ref[krows, :].astype(BF16)
        v = v_ref[krows, :].astype(BF16)
        s = lax.dot_general(q, k, (((1,), (1,)), ((), ())), preferred_element_type=F32) * ATT_SCALE + t_ref[...]
        m = jnp.max(s, axis=-1, keepdims=True)
        p = jnp.exp(s - m)
        denom = jnp.sum(p, axis=-1, keepdims=True)
        o_ref[...] = jnp.dot((p / denom).astype(BF16), v, preferred_element_type=F32)
        lse_ref[...] = jnp.broadcast_to(m + jnp.log(denom), (C_QUERIES, HEAD_DIM))

    def tile_index(h, g):
        return (h, _c_block(g, rows)[1], 0, 0)

    ospec = pl.BlockSpec((C_QUERIES, HEAD_DIM), lambda h, g: (g, h))
    return pl.pallas_call(
        body, name=name, grid=(C_HEADS, rows // C_QROWS),
        in_specs=[pl.BlockSpec((C_QUERIES, HEAD_DIM), lambda h, g: (g, 20 + h)),
                  pl.BlockSpec((L, HEAD_DIM), lambda h, g: (0, 14 + h)),
                  pl.BlockSpec((L, HEAD_DIM), lambda h, g: (0, PC_VC + h)),
                  pl.BlockSpec((None, None, C_QUERIES, C_KEYS), tile_index)],
        out_specs=[ospec, ospec],
        out_shape=[jax.ShapeDtypeStruct((L, C_HEADS * HEAD_DIM), F32)] * 2,
        compiler_params=_params(("parallel", "arbitrary")),
    )(qn, kn, proj, tiles)


def _c_bwd(qn, kn, proj, tiles, do, o, lse, dq_buf, dk_buf, dv_buf, *, name):
    L = qn.shape[0]
    rows = _c_geometry(L)

    def body(q_ref, k_ref, v_ref, t_ref, do_ref, o_ref, lse_ref, _a, _b, _c, dq_ref, dk_ref, dv_ref, dt_ref):
        g = pl.program_id(1)
        k0, _, off = _c_block(g, rows)
        krows = pl.ds(pl.multiple_of(k0 * GRID_W, GRID_W), C_KEYS)

        @pl.when(g == 0)
        def _():
            dk_ref[...] = jnp.zeros_like(dk_ref)
            dv_ref[...] = jnp.zeros_like(dv_ref)
            dt_ref[...] = jnp.zeros_like(dt_ref)

        q = q_ref[...].astype(BF16)
        k = k_ref[krows, :].astype(BF16)
        v = v_ref[krows, :].astype(BF16)
        dov = do_ref[...]
        dob = dov.astype(BF16)
        delta = jnp.sum(dov * o_ref[...], axis=-1, keepdims=True)
        s = lax.dot_general(q, k, (((1,), (1,)), ((), ())), preferred_element_type=F32) * ATT_SCALE + t_ref[...]
        p = jnp.exp(s - lse_ref[...][:, :1])
        dp = lax.dot_general(dob, v, (((1,), (1,)), ((), ())), preferred_element_type=F32)
        ds = p * (dp - delta)
        for a in range(C_QROWS):
            for b in range(C_KROWS):
                rel = jnp.clip(b - a + off, 0, C_NREL - 1)
                dt_ref[rel] += ds[a * GRID_W:(a + 1) * GRID_W, b * GRID_W:(b + 1) * GRID_W]
        dsb = ds.astype(BF16)
        dq_ref[...] = jnp.dot(dsb, k, preferred_element_type=F32) * ATT_SCALE
        dk_ref[krows, :] += lax.dot_general(dsb, q, (((0,), (0,)), ((), ())), preferred_element_type=F32) * ATT_SCALE
        dv_ref[krows, :] += lax.dot_general(p.astype(BF16), dob, (((0,), (0,)), ((), ())),
                                            preferred_element_type=F32)

    def tile_index(h, g):
        return (h, _c_block(g, rows)[1], 0, 0)

    hspec = pl.BlockSpec((C_QUERIES, HEAD_DIM), lambda h, g: (g, h))
    qspec = pl.BlockSpec((C_QUERIES, HEAD_DIM), lambda h, g: (g, 20 + h))
    kspec = pl.BlockSpec((L, HEAD_DIM), lambda h, g: (0, 14 + h))
    any_spec = pl.BlockSpec(memory_space=pl.ANY)
    return pl.pallas_call(
        body, name=name, grid=(C_HEADS, rows // C_QROWS),
        in_specs=[qspec, kspec, pl.BlockSpec((L, HEAD_DIM), lambda h, g: (0, PC_VC + h)),
                  pl.BlockSpec((None, None, C_QUERIES, C_KEYS), tile_index),
                  hspec, hspec, hspec, any_spec, any_spec, any_spec],
        out_specs=[qspec, kspec, kspec,
                   pl.BlockSpec((None, C_NREL, GRID_W, GRID_W), lambda h, r: (h, 0, 0, 0))],
        out_shape=[jax.ShapeDtypeStruct(dq_buf.shape, F32), jax.ShapeDtypeStruct(dk_buf.shape, F32),
                   jax.ShapeDtypeStruct(dv_buf.shape, F32),
                   jax.ShapeDtypeStruct((C_HEADS, C_NREL, GRID_W, GRID_W), F32)],
        input_output_aliases={7: 0, 8: 1, 9: 2},
        compiler_params=_params(("parallel", "arbitrary")),
    )(qn, kn, proj, tiles, do, o, lse, dq_buf, dk_buf, dv_buf)


def _c_expand_matrix():
    cq = np.arange(GRID_W)[:, None]
    ck = np.arange(GRID_W)[None, :]
    d = (ck - cq + (C_WIN_COLS - 1)).reshape(-1)
    e = np.zeros((GRID_W * GRID_W, HEAD_DIM), np.float32)
    okd = (d >= 0) & (d < C_NCOL)
    e[np.arange(GRID_W * GRID_W)[okd], d[okd]] = 1.0
    return e


def _peer(p):
    return (p // 4, (p // 2) % 2, p % 2)


def _my_index():
    return 4 * lax.axis_index("x") + 2 * lax.axis_index("y") + lax.axis_index("c")


HBM_SPEC = pl.BlockSpec(memory_space=pltpu.HBM)
SEM_SPEC = pl.BlockSpec(memory_space=pltpu.SEMAPHORE)
ANY_SPEC = pl.BlockSpec(memory_space=pl.ANY)
DATAFLOW = pltpu.SideEffectType.DATAFLOW_SIDE_EFFECTING


def _exchange_views(mode, kinds, arrays):
    nw = len(kinds)
    if mode == "gather":
        rows = [a.shape[0] // N_DEV for a in arrays[:nw]]
    else:
        rows = [a.shape[1] // N_DEV for a in arrays[:nw]]

    def gather_slot(ref, w, who):
        return ref.at[who] if kinds[w] == "col" else ref.at[pl.ds(who * rows[w], rows[w]), :]

    def src(ref, w, to):
        if mode == "gather":
            return gather_slot(ref, w, _my_index())
        return ref.at[to] if kinds[w] == "col" else ref.at[0, pl.ds(to * rows[w], rows[w]), :]

    def dst(ref, w, who):
        return gather_slot(ref, w, who) if mode == "gather" else ref.at[who]

    return src, dst


def _place_cast(w, layer, kind, *, name):
    _, R, C = w.shape
    tr = _tile(R, (256, 128, 64, 32, 16))

    def body(w_ref, o_ref):
        o_ref[...] = w_ref[...].astype(BF16)

    if kind == "col":
        out_shape = jax.ShapeDtypeStruct((N_DEV, R, C), BF16)
        out_spec = pl.BlockSpec((None, tr, C), lambda t: (_my_index(), t, 0))
    else:
        out_shape = jax.ShapeDtypeStruct((N_DEV * R, C), BF16)
        out_spec = pl.BlockSpec((tr, C), lambda t: (_my_index() * (R // tr) + t, 0))
    return pl.pallas_call(
        body, name=name, grid=(R // tr,), in_specs=[pl.BlockSpec((None, tr, C), lambda t: (layer, t, 0))],
        out_specs=out_spec, out_shape=out_shape, compiler_params=_params(("parallel",)),
    )(w)


def _exchange_start(mode, srcs, lands, kinds, after, *, name):
    nw = len(lands)
    ns = len(srcs)
    src, dst = _exchange_views(mode, kinds, list(srcs) + list(lands))

    def body(*refs):
        l_refs = refs[ns:ns + nw]
        s_refs = refs[:ns] if ns else l_refs
        send_sems, recv_sems = refs[ns + nw + 1], refs[ns + nw + 2]
        token = refs[-1]
        me = _my_index()
        for off in range(1, N_DEV):
            to = (me + off) % N_DEV
            for w in range(nw):
                pltpu.make_async_remote_copy(src(s_refs[w], w, to), dst(l_refs[w], w, me),
                                             send_sems.at[w * N_DEV + off], recv_sems.at[w * N_DEV + off],
                                             device_id=_peer(to), device_id_type=MESH).start()
        token[...] = jnp.zeros_like(token)

    arrays = list(srcs) + list(lands)
    na = len(arrays)
    outs = pl.pallas_call(
        body, name=name,
        out_shape=(pltpu.SemaphoreType.DMA((nw * N_DEV,)), pltpu.SemaphoreType.DMA((nw * N_DEV,)),
                   *[pltpu.HBM(a.shape, a.dtype) for a in arrays], jax.ShapeDtypeStruct((8, 128), F32)),
        in_specs=[HBM_SPEC] * na + [ANY_SPEC],
        out_specs=(SEM_SPEC, SEM_SPEC, *([HBM_SPEC] * na), pl.BlockSpec(memory_space=pltpu.VMEM)),
        input_output_aliases={k: 2 + k for k in range(na)},
        compiler_params=pltpu.CompilerParams(has_side_effects=DATAFLOW),
    )(*[pltpu.with_memory_space_constraint(a, pltpu.HBM) for a in arrays], after)
    return outs[0], outs[1], outs[2:2 + ns], outs[2 + ns:2 + na], outs[-1]


def _exchange_wait(mode, started, kinds, after, *, name):
    send_sems, recv_sems, srcs, lands, _ = started
    nw = len(lands)
    ns = len(srcs)
    arrays = list(srcs) + list(lands)
    na = len(arrays)
    src, dst = _exchange_views(mode, kinds, arrays)

    def body(*refs):
        l_refs = refs[ns:na]
        s_refs = refs[:ns] if ns else l_refs
        send_ref, recv_ref = refs[na], refs[na + 1]
        me = _my_index()
        for off in range(1, N_DEV):
            to = (me + off) % N_DEV
            frm = (me + N_DEV - off) % N_DEV
            for w in range(nw):
                cp = pltpu.make_async_remote_copy(src(s_refs[w], w, to), dst(l_refs[w], w, frm),
                                                  send_ref.at[w * N_DEV + off], recv_ref.at[w * N_DEV + off],
                                                  device_id=_peer(frm), device_id_type=MESH)
                cp.wait_send()
                cp.wait_recv()

    outs = pl.pallas_call(
        body, name=name, out_shape=[pltpu.HBM(a.shape, a.dtype) for a in arrays],
        in_specs=[HBM_SPEC] * na + [SEM_SPEC, SEM_SPEC, ANY_SPEC], out_specs=[HBM_SPEC] * na,
        input_output_aliases={k: k for k in range(na)},
        compiler_params=pltpu.CompilerParams(has_side_effects=DATAFLOW),
    )(*arrays, send_sems, recv_sems, after)
    return outs[:ns], outs[ns:]


def _all_reduce_small(x):
    R = x.shape[0]

    def body(x_ref, o_ref, gath, send_sems, recv_sems):
        me = _my_index()
        gath[me] = x_ref[...]
        sends = []
        for off in range(1, N_DEV):
            to = (me + off) % N_DEV
            cp = pltpu.make_async_remote_copy(x_ref, gath.at[me], send_sems.at[off], recv_sems.at[off],
                                              device_id=_peer(to), device_id_type=MESH)
            cp.start()
            sends.append(cp)
        for off in range(1, N_DEV):
            frm = (me + N_DEV - off) % N_DEV
            pltpu.make_async_remote_copy(x_ref, gath.at[frm], send_sems.at[off], recv_sems.at[off],
                                         device_id=_peer(frm), device_id_type=MESH).wait_recv()
        for cp in sends:
            cp.wait_send()
        acc = gath[0]
        for s in range(1, N_DEV):
            acc = acc + gath[s]
        o_ref[...] = acc

    vm = pl.BlockSpec(memory_space=pltpu.VMEM)
    return pl.pallas_call(
        body, name="all_reduce_small", in_specs=[vm], out_specs=vm, out_shape=jax.ShapeDtypeStruct((R, 128), F32),
        scratch_shapes=[pltpu.VMEM((N_DEV, R, 128), F32), pltpu.SemaphoreType.DMA((N_DEV,)),
                        pltpu.SemaphoreType.DMA((N_DEV,))],
        compiler_params=pltpu.CompilerParams(has_side_effects=True),
    )(x)


def _adamw_math(w, g, m, v):
    m = ADAM_B1 * m + (1.0 - ADAM_B1) * g
    v = ADAM_B2 * v + (1.0 - ADAM_B2) * (g * g)
    m_hat = m / (1.0 - ADAM_B1 ** ADAM_STEP)
    v_hat = v / (1.0 - ADAM_B2 ** ADAM_STEP)
    delta = -ADAM_LR * (m_hat / (jnp.sqrt(v_hat) + ADAM_EPS) + ADAM_WD * w)
    return delta, m, v


def _adamw_layer(recv, own, kind, w, m, v, outs, layer, dep, *, name):
    nl, R, C = w.shape
    tr = _tile(R, (128, 64, 32, 16))

    def body(r_ref, o_ref, w_ref, m_ref, v_ref, _0, _1, _2, _3, _dep, g_out, d_out, m_out, v_out, token):
        token[...] = jnp.zeros_like(token)
        me = _my_index()
        mine = o_ref[...].astype(F32)
        g = jnp.where(me == 0, mine, r_ref[0].astype(F32))
        for s in range(1, N_DEV):
            g = g + jnp.where(me == s, mine, r_ref[s].astype(F32))
        delta, mn, vn = _adamw_math(w_ref[...], g, m_ref[...], v_ref[...])
        g_out[...] = g
        d_out[...] = delta
        m_out[...] = mn
        v_out[...] = vn

    if kind == "col":
        own_spec = pl.BlockSpec((None, tr, C), lambda t: (_my_index(), t, 0))
    else:
        own_spec = pl.BlockSpec((None, tr, C), lambda t: (0, _my_index() * (R // tr) + t, 0))
    wspec = pl.BlockSpec((None, tr, C), lambda t: (layer, t, 0))
    res = pl.pallas_call(
        body, name=name, grid=(R // tr,),
        in_specs=[pl.BlockSpec((N_DEV, tr, C), lambda t: (0, t, 0)), own_spec] + [wspec] * 3 + [ANY_SPEC] * 5,
        out_specs=[wspec] * 4 + [pl.BlockSpec((8, 128), lambda t: (0, 0))],
        out_shape=[jax.ShapeDtypeStruct((nl, R, C), F32)] * 4 + [jax.ShapeDtypeStruct((8, 128), F32)],
        input_output_aliases={5: 0, 6: 1, 7: 2, 8: 3},
        compiler_params=_params(("arbitrary",)),
    )(recv, own, w, m, v, *outs, dep)
    return res[:4], res[4]


def _adamw_small(g, w, m, v):
    def body(g_ref, w_ref, m_ref, v_ref, d_out, m_out, v_out):
        delta, mn, vn = _adamw_math(w_ref[...], g_ref[...], m_ref[...], v_ref[...])
        d_out[...] = delta
        m_out[...] = mn
        v_out[...] = vn

    return pl.pallas_call(body, name="adamw_small", out_shape=[jax.ShapeDtypeStruct(g.shape, F32)] * 3)(g, w, m, v)


def _pack(arrays, rows):
    flat = jnp.concatenate([a.reshape(-1) for a in arrays])
    return jnp.pad(flat, (0, rows * 128 - flat.shape[0])).reshape(rows, 128)


def _unpack(packed, shapes):
    flat = packed.reshape(-1)
    out, pos = [], 0
    for s in shapes:
        size = int(np.prod(s))
        out.append(flat[pos:pos + size].reshape(s))
        pos += size
    return out


def kernel(x, norm1_g, w_in, qk_norm_g, sink_a, rpb_c, w_br_a, w_br_b, w_br_c, w_o, norm2_g, w_gate_up, w_down, loss_target, m_norm1_g, m_w_in, m_qk_norm_g, m_sink_a, m_rpb_c, m_w_br_a, m_w_br_b, m_w_br_c, m_w_o, m_norm2_g, m_w_gate_up, m_w_down, v_norm1_g, v_w_in, v_qk_norm_g, v_sink_a, v_rpb_c, v_w_br_a, v_w_br_b, v_w_br_c, v_w_o, v_norm2_g, v_w_gate_up, v_w_down):
    nl = w_in.shape[0]
    L, D = x.shape[1], x.shape[2]
    x0 = x.reshape(L, D)
    tgt = loss_target.reshape(L, D)

    big = [w_in, w_br_a, w_br_b, w_br_c, w_o, w_gate_up, w_down]
    kinds = ["col", "col", "col", "col", "row", "col", "row"]

    big_names = ["w_in", "w_br_a", "w_br_b", "w_br_c", "w_o", "w_gate_up", "w_down"]
    ALL = list(range(len(big)))
    REST = ALL[1:]

    def gather_place(i):
        return [_place_cast(w, i, k, name="gather_place_" + n) for w, k, n in zip(big, kinds, big_names)]

    def gather_start(lands, sub, after, tag):
        return _exchange_start("gather", [], [lands[j] for j in sub], [kinds[j] for j in sub], after,
                               name="gather_start" + tag)

    def gather_wait(started, sub, after, tag):
        _, lands = _exchange_wait("gather", started, [kinds[j] for j in sub], after, name="gather_wait" + tag)
        return [g.reshape((N_DEV, 1) + g.shape[1:]) if kinds[j] == "col" else g.reshape((1, 1) + g.shape)
                for g, j in zip(lands, sub)]

    half = HEAD_DIM // 2
    inv_freq = ROPE_THETA ** (-jnp.arange(half, dtype=F32) * 2.0 / HEAD_DIM)
    ang = jnp.arange(L, dtype=F32)[:, None] * inv_freq[None, :]
    cos = jnp.concatenate([jnp.cos(ang), jnp.cos(ang)], axis=-1)
    sin = jnp.concatenate([-jnp.sin(ang), jnp.sin(ang)], axis=-1)
    expand = jnp.asarray(_c_expand_matrix(), BF16)
    expand_t = jnp.asarray(_c_expand_matrix().T, BF16)

    def gain_tables(i):
        g = qk_norm_g[i]
        gq = jnp.concatenate([jnp.tile(g[0][None], (8, 1)), jnp.tile(g[2][None], (12, 1)), jnp.tile(g[4][None], (8, 1))])
        gk = jnp.concatenate([jnp.tile(g[1][None], (2, 1)), jnp.tile(g[3][None], (12, 1)), jnp.tile(g[5][None], (8, 1))])
        return (gq.reshape(NQ_CHUNKS // Q_CG, 1, Q_CG * HEAD_DIM), gk.reshape(NK_CHUNKS // K_CG, 1, K_CG * HEAD_DIM))

    def bias_table(i):
        rp = jnp.pad(rpb_c[i].reshape(C_HEADS * C_NREL, C_NCOL), ((0, 0), (0, HEAD_DIM - C_NCOL)))
        t = _exact_mm(rp, expand_t, name="c_bias_expand")
        return _c_bias_tiles(t.reshape(C_HEADS, C_NREL, GRID_W, GRID_W))

    def sink_table(i):
        return jnp.broadcast_to(sink_a[i][:, None, None], (A_Q_HEADS, 1, HEAD_DIM))

    saved = []
    gws = [None] * nl
    xi = x0
    lands0 = gather_place(0)
    first = gather_start(lands0, [0], x0, "_first")
    gws[0] = gather_wait(first, [0], x0, "_first")
    rest0 = gather_start(lands0, REST, gws[0][0], "_rest")
    dep = rest0[4]
    for i in range(nl):
        if i + 1 < nl:
            started = gather_start(gather_place(i + 1), ALL, dep, "")
            dep = started[4]
        gw_in = gws[i][0]
        gq, gk = gain_tables(i)
        bias_t = bias_table(i)
        sink = sink_table(i)
        h1 = _rms_fwd(xi, norm1_g[i][None], dep, name="rms1_fwd")
        proj = _mm_nn(h1, gw_in, 0, out_dtype=F32, name="proj_fwd")
        qn = _qk_fwd(proj, gq, cos, sin, Q_PIECES, NQ_CHUNKS, Q_ROPE_UPTO, Q_CG, name="qnorm_fwd")
        kn = _qk_fwd(proj, gk, cos, sin, K_PIECES, NK_CHUNKS, K_ROPE_UPTO, K_CG, name="knorm_fwd")
        oa, lse_a = _band_fwd(qn, kn, proj, dil=1, radius=A_RADIUS, nkv=A_KV_HEADS, group=A_GROUP,
                              q0=0, k0=0, v0=PC_VA, sink=sink, name="attn_a_fwd")
        obs, lbs = [], []
        for g, (window, dil) in enumerate(B_PATTERNS):
            o_g, l_g = _band_fwd(qn, kn, proj, dil=dil, radius=window // (2 * dil), nkv=B_HG, group=1,
                                 q0=8 + g * B_HG, k0=2 + g * B_HG, v0=PC_VB + g * B_HG, name=f"attn_b{g}_fwd")
            obs.append(o_g)
            lbs.append(l_g)
        ob, lse_b = _combine_b(obs, lbs, name="attn_b_combine")
        oc, lse_c = _c_fwd(qn, kn, proj, bias_t, name="attn_c_fwd")
        if i == 0:
            gws[0] = gws[0] + gather_wait(rest0, REST, oc, "_rest")
        _, gw_a, gw_b, gw_c, gw_o, gw_gu, gw_d = gws[i]
        ta = _mm_nn(oa, gw_a, 0, out_dtype=F32, name="br_a_fwd")
        tb = _mm_nn(ob, gw_b, 0, out_dtype=F32, name="br_b_fwd")
        tc = _mm_nn(oc, gw_c, 0, out_dtype=F32, name="br_c_fwd")
        merged = _gate_fwd(proj, ta, tb, tc, name="gate_fwd")
        x1 = _mm_nn(merged, gw_o, 0, out_dtype=F32, name="wo_fwd", res=xi)
        h2 = _rms_fwd(x1, norm2_g[i][None], x1, name="rms2_fwd")
        gu = _mm_nn(h2, gw_gu, 0, out_dtype=F32, name="gate_up_fwd")
        act = _swiglu_fwd(gu, name="swiglu_fwd")
        x2 = _mm_nn(act, gw_d, 0, out_dtype=F32, name="down_fwd", res=x1)
        saved.append(dict(x=xi, h1=h1, proj=proj, qn=qn, kn=kn, oa=oa, lse_a=lse_a, ob=ob, lse_b=lse_b, oc=oc,
                          lse_c=lse_c, ta=ta, tb=tb, tc=tc, merged=merged, x1=x1, h2=h2, gu=gu, act=act,
                          gq=gq, gk=gk, bias_t=bias_t, sink=sink))
        xi = x2
        dep = x2
        if i + 1 < nl:
            gws[i + 1] = gather_wait(started, ALL, x2, "")

    dx, dxb, loss_row = _loss(xi, tgt, name="loss")

    def scatter_start(grads, sub, after, tag):
        lands = []
        for g, j in zip(grads, sub):
            shape = g.shape if kinds[j] == "col" else (N_DEV, g.shape[1] // N_DEV, g.shape[2])
            lands.append(lax.empty(shape, BF16))
        return _exchange_start("scatter", grads, lands, [kinds[j] for j in sub], after, name="scatter_start" + tag)

    def scatter_wait(pair, after):
        own_a, recv_a = _exchange_wait("scatter", pair[0], [kinds[0]], after, name="scatter_wait_in")
        own_b, recv_b = _exchange_wait("scatter", pair[1], [kinds[j] for j in REST], after, name="scatter_wait_rest")
        return list(recv_a) + list(recv_b), list(own_a) + list(own_b)

    small_grads = [None] * nl
    recv = [None] * nl
    own = [None] * nl
    pending = None
    for i in reversed(range(nl)):
        s = saved[i]
        gw_in, gw_a, gw_b, gw_c, gw_o, gw_gu, gw_d = gws[i]
        dact = _mm_nt(dxb, gw_d, 0, out_dtype=F32, name="down_bwd_x", dep=None if pending is None else pending[0][4])
        g_down = _mm_tn(s["act"], dxb, 1, name="down_bwd_w")
        dgu = _swiglu_bwd(s["gu"], dact, name="swiglu_bwd")
        g_gu = _mm_tn(s["h2"], dgu, N_DEV, name="gate_up_bwd_w")
        dh2 = _mm_nt(dgu, gw_gu, 0, out_dtype=F32, name="gate_up_bwd_x")
        dx1, dx1b, dg2 = _rms_bwd(s["x1"], norm2_g[i][None], dh2, dx, name="rms2_bwd")
        dmerged = _mm_nt(dx1b, gw_o, 0, out_dtype=F32, name="wo_bwd_x")
        g_o = _mm_tn(s["merged"], dx1b, 1, name="wo_bwd_w")
        dta, dtb, dtc, dproj = _gate_bwd(s["proj"], s["ta"], s["tb"], s["tc"], dmerged, name="gate_bwd")
        g_a = _mm_tn(s["oa"], dta, N_DEV, name="br_a_bwd_w")
        g_b = _mm_tn(s["ob"], dtb, N_DEV, name="br_b_bwd_w")
        g_c = _mm_tn(s["oc"], dtc, N_DEV, name="br_c_bwd_w")
        rest = scatter_start([g_a, g_b, g_c, g_o, g_gu, g_down], REST, g_c, "_rest")
        doa = _mm_nt(dta, gw_a, 0, out_dtype=F32, name="br_a_bwd_x", dep=rest[4])
        dob = _mm_nt(dtb, gw_b, 0, out_dtype=F32, name="br_b_bwd_x")
        doc = _mm_nt(dtc, gw_c, 0, out_dtype=F32, name="br_c_bwd_x")
        dq_buf = lax.empty((L, NQ_CHUNKS * HEAD_DIM), F32)
        dk_buf = lax.empty((L, NK_CHUNKS * HEAD_DIM), F32)
        dv_buf = lax.empty((L, NK_CHUNKS * HEAD_DIM), F32)
        dq_buf, dk_buf, dv_buf, dsink = _band_bwd(
            s["qn"], s["kn"], s["proj"], doa, s["oa"], s["lse_a"], dq_buf, dk_buf, dv_buf, dil=1, radius=A_RADIUS,
            nkv=A_KV_HEADS, group=A_GROUP, q0=0, k0=0, v0=PC_VA, o0=0, sink=s["sink"], name="attn_a_bwd")
        for g, (window, dil) in enumerate(B_PATTERNS):
            dq_buf, dk_buf, dv_buf = _band_bwd(
                s["qn"], s["kn"], s["proj"], dob, s["ob"], s["lse_b"], dq_buf, dk_buf, dv_buf, dil=dil,
                radius=window // (2 * dil), nkv=B_HG, group=1, q0=8 + g * B_HG, k0=2 + g * B_HG,
                v0=PC_VB + g * B_HG, o0=0, name=f"attn_b{g}_bwd")
        dq_buf, dk_buf, dv_buf, dbias_t = _c_bwd(s["qn"], s["kn"], s["proj"], s["bias_t"], doc, s["oc"], s["lse_c"],
                                                 dq_buf, dk_buf, dv_buf, name="attn_c_bwd")
        dproj, dgq = _qk_bwd(dq_buf, s["proj"], s["gq"], cos, sin, dproj, Q_PIECES, NQ_CHUNKS, Q_ROPE_UPTO, Q_CG,
                             name="qnorm_bwd")
        dproj, dgk = _qk_bwd(dk_buf, s["proj"], s["gk"], cos, sin, dproj, K_PIECES, NK_CHUNKS, K_ROPE_UPTO, K_CG,
                             name="knorm_bwd")
        dproj = _v_bwd(dv_buf, dproj, name="v_bwd")
        g_in = _mm_tn(s["h1"], dproj, N_DEV, name="proj_bwd_w")
        dh1 = _mm_nt(dproj, gw_in, 0, out_dtype=F32, name="proj_bwd_x")
        dx, dxb, dg1 = _rms_bwd(s["x"], norm1_g[i][None], dh1, dx1, name="rms1_bwd")
        if pending is not None:
            recv[i + 1], own[i + 1] = scatter_wait(pending, dx)
        pending = (scatter_start([g_in], [0], dx, "_in"), rest)

        drpb = _exact_mm(dbias_t.reshape(C_HEADS * C_NREL, GRID_W * GRID_W), expand, name="c_bias_reduce")
        dgq, dgk = dgq.reshape(NQ_CHUNKS, HEAD_DIM), dgk.reshape(NK_CHUNKS, HEAD_DIM)
        dqk_g = jnp.stack([dgq[0:8].sum(0), dgk[0:2].sum(0), dgq[8:20].sum(0), dgk[2:14].sum(0),
                           dgq[20:28].sum(0), dgk[14:22].sum(0)])
        small_grads[i] = (dg1.reshape(D), dqk_g, dsink[:, 0, 0],
                          drpb[:, :C_NCOL].reshape(C_HEADS, C_NREL, C_NCOL), dg2.reshape(D))

    small_names = [norm1_g, qk_norm_g, sink_a, rpb_c, norm2_g]
    small_m = [m_norm1_g, m_qk_norm_g, m_sink_a, m_rpb_c, m_norm2_g]
    small_v = [v_norm1_g, v_qk_norm_g, v_sink_a, v_rpb_c, v_norm2_g]
    shapes = [a.shape for a in small_names]
    total = sum(int(np.prod(sh)) for sh in shapes) + 128
    rows = -(-total // 1024) * 8
    stacked = [jnp.stack([small_grads[i][j] for i in range(nl)]) for j in range(5)]
    packed = _pack([loss_row.reshape(-1)] + stacked, rows)
    summed = _all_reduce_small(packed)
    loss = summed[0, 0]
    zero_row = jnp.zeros((128,), F32)
    d_s, m_s, v_s = _adamw_small(summed, _pack([zero_row] + small_names, rows), _pack([zero_row] + small_m, rows),
                                 _pack([zero_row] + small_v, rows))
    shapes1 = [(128,)] + shapes
    g_small = _unpack(summed, shapes1)[1:]
    d_small = _unpack(d_s, shapes1)[1:]
    m_small = _unpack(m_s, shapes1)[1:]
    v_small = _unpack(v_s, shapes1)[1:]

    big_m = [m_w_in, m_w_br_a, m_w_br_b, m_w_br_c, m_w_o, m_w_gate_up, m_w_down]
    big_v = [v_w_in, v_w_br_a, v_w_br_b, v_w_br_c, v_w_o, v_w_gate_up, v_w_down]
    big_out = [[lax.empty(w.shape, F32) for _ in range(4)] for w in big]
    token = pending[0][4]
    for i in list(range(nl - 1, 0, -1)) + [0]:
        if i == 0:
            recv[0], own[0] = scatter_wait(pending, token)
        for j in range(len(big)):
            big_out[j], token = _adamw_layer(recv[i][j], own[i][j], kinds[j], big[j], big_m[j], big_v[j],
                                             big_out[j], i, token, name="adamw_" + big_names[j])

    order = ["norm1_g", "w_in", "qk_norm_g", "sink_a", "rpb_c", "w_br_a", "w_br_b", "w_br_c", "w_o", "norm2_g",
             "w_gate_up", "w_down"]
    small_idx = {"norm1_g": 0, "qk_norm_g": 1, "sink_a": 2, "rpb_c": 3, "norm2_g": 4}
    big_idx = {n: j for j, n in enumerate(big_names)}

    def pick(kind):
        out = []
        for n in order:
            if n in small_idx:
                out.append([g_small, d_small, m_small, v_small][kind][small_idx[n]])
            else:
                out.append(big_out[big_idx[n]][kind])
        return out

    return (loss, dx.reshape(1, L, D), *pick(0), *pick(1), *pick(2), *pick(3))
```

```python
import functools
import math

import numpy as np
import jax
import jax.numpy as jnp
from jax import lax
from jax.experimental import pallas as pl
from jax.experimental.pallas import tpu as pltpu

F32 = jnp.float32
BF16 = jnp.bfloat16
MESH = pl.DeviceIdType.MESH
N_DEV = 8

HEAD_DIM = 128
NORM_EPS = 1e-6
ROPE_THETA = 10000.0
ATT_SCALE = HEAD_DIM ** -0.5
NEG = -1e30

A_Q_HEADS, A_KV_HEADS, A_RADIUS = 8, 2, 128
A_GROUP = A_Q_HEADS // A_KV_HEADS
B_PATTERNS = ((128, 1), (512, 4), (2048, 16))
B_HG = 4
B_HEADS = len(B_PATTERNS) * B_HG
C_HEADS, GRID_W, C_WIN_ROWS, C_WIN_COLS = 8, 64, 8, 16
C_NREL = 2 * C_WIN_ROWS - 1
C_NCOL = 2 * C_WIN_COLS - 1

PC_QA, PC_KA, PC_VA = 0, 8, 10
PC_QB, PC_KB, PC_VB = 12, 24, 36
PC_QC, PC_KC, PC_VC = 48, 56, 64
N_QKV_CHUNKS = 72
Q_PIECES = ((0, 8, PC_QA), (8, 12, PC_QB), (20, 8, PC_QC))
K_PIECES = ((0, 2, PC_KA), (2, 12, PC_KB), (14, 8, PC_KC))
V_PIECES = ((0, 2, PC_VA), (2, 12, PC_VB), (14, 8, PC_VC))
NQ_CHUNKS, NK_CHUNKS = 28, 22
Q_ROPE_UPTO, K_ROPE_UPTO = 20, 14
Q_CG, K_CG = 4, 2

ADAM_LR, ADAM_B1, ADAM_B2, ADAM_EPS, ADAM_WD, ADAM_STEP = 0.001, 0.9, 0.999, 1e-08, 0.01, 10

VMEM_LIMIT = 48 * 1024 * 1024


def _tile(dim, prefs):
    for p in prefs:
        if dim % p == 0:
            return p
    return dim


def _params(sem, **kw):
    return pltpu.CompilerParams(dimension_semantics=sem, vmem_limit_bytes=VMEM_LIMIT, **kw)


def _piece_map(pieces):
    def f(c):
        out = c - pieces[0][0] + pieces[0][2]
        for first, _, pfirst in pieces[1:]:
            out = jnp.where(c >= first, c - first + pfirst, out)
        return out
    return f


def _mm_nn(a, w, layer, *, out_dtype, name, res=None):
    M, K = a.shape
    nb, _, Kw, ns = w.shape
    assert Kw == K
    tm = _tile(M, (512, 256))
    tn = _tile(ns, (640, 512, 256, 128)) if ns > 1408 else ns
    tk = _tile(K, (2048, 1408, 1024, 512, 256))
    nj, nk = ns // tn, K // tk

    def body(*refs):
        if res is None:
            a_ref, w_ref, o_ref, acc_ref = refs
            r_ref = None
        else:
            a_ref, w_ref, r_ref, o_ref, acc_ref = refs
        k = pl.program_id(3)
        part = jnp.dot(a_ref[...].astype(BF16), w_ref[...], preferred_element_type=F32)

        @pl.when(k == 0)
        def _():
            acc_ref[...] = part

        @pl.when(k > 0)
        def _():
            acc_ref[...] += part

        @pl.when(k == nk - 1)
        def _():
            r = acc_ref[...]
            if r_ref is not None:
                r = r + r_ref[...]
            o_ref[...] = r.astype(out_dtype)

    in_specs = [pl.BlockSpec((tm, tk), lambda i, b, j, k: (i, k)),
                pl.BlockSpec((None, None, tk, tn), lambda i, b, j, k: (b, layer, k, j))]
    args = [a, w]
    if res is not None:
        in_specs.append(pl.BlockSpec((tm, tn), lambda i, b, j, k: (i, b * nj + j)))
        args.append(res)
    return pl.pallas_call(
        body, name=name, grid=(M // tm, nb, nj, nk), in_specs=in_specs,
        out_specs=pl.BlockSpec((tm, tn), lambda i, b, j, k: (i, b * nj + j)),
        out_shape=jax.ShapeDtypeStruct((M, nb * ns), out_dtype),
        scratch_shapes=[pltpu.VMEM((tm, tn), F32)],
        compiler_params=_params(("parallel", "parallel", "parallel", "arbitrary")),
    )(*args)


def _mm_nt(a, w, layer, *, out_dtype, name, dep=None):
    M, N = a.shape
    nb, _, K, ns = w.shape
    assert N == nb * ns
    tm = _tile(M, (1024, 512, 256))
    tn = _tile(ns, (640, 512, 256, 128)) if ns > 1408 else ns
    tk = _tile(K, (1024, 512, 256))
    nj = ns // tn
    nred = nb * nj

    def body(*refs):
        a_ref, w_ref = refs[:2]
        o_ref, acc_ref = refs[-2:]
        s = pl.program_id(2) * nj + pl.program_id(3)
        part = lax.dot_general(a_ref[...].astype(BF16), w_ref[...], (((1,), (1,)), ((), ())),
                               preferred_element_type=F32)

        @pl.when(s == 0)
        def _():
            acc_ref[...] = part

        @pl.when(s > 0)
        def _():
            acc_ref[...] += part

        @pl.when(s == nred - 1)
        def _():
            o_ref[...] = acc_ref[...].astype(out_dtype)

    in_specs = [pl.BlockSpec((tm, tn), lambda i, kk, b, j: (i, b * nj + j)),
                pl.BlockSpec((None, None, tk, tn), lambda i, kk, b, j: (b, layer, kk, j))]
    args = [a, w]
    if dep is not None:
        in_specs.append(ANY_SPEC)
        args.append(dep)
    return pl.pallas_call(
        body, name=name, grid=(M // tm, K // tk, nb, nj), in_specs=in_specs,
        out_specs=pl.BlockSpec((tm, tk), lambda i, kk, b, j: (i, kk)),
        out_shape=jax.ShapeDtypeStruct((M, K), out_dtype),
        scratch_shapes=[pltpu.VMEM((tm, tk), F32)],
        compiler_params=_params(("parallel", "parallel", "arbitrary", "arbitrary")),
    )(*args)


def _mm_tn(a, g, nb, *, name):
    M, Ka = a.shape
    N = g.shape[1]
    ns = N // nb
    tka = _tile(Ka, (1024, 512, 256))
    tn = _tile(ns, (640, 512, 256, 128)) if ns > 1408 else ns
    tm = _tile(M, (1024, 512, 256))
    nj, nm = ns // tn, M // tm

    def body(a_ref, g_ref, o_ref, acc_ref):
        m = pl.program_id(3)
        part = lax.dot_general(a_ref[...].astype(BF16), g_ref[...].astype(BF16), (((0,), (0,)), ((), ())),
                               preferred_element_type=F32)

        @pl.when(m == 0)
        def _():
            acc_ref[...] = part

        @pl.when(m > 0)
        def _():
            acc_ref[...] += part

        @pl.when(m == nm - 1)
        def _():
            o_ref[...] = acc_ref[...].astype(BF16)

    return pl.pallas_call(
        body, name=name, grid=(Ka // tka, nb, nj, nm),
        in_specs=[pl.BlockSpec((tm, tka), lambda ka, b, j, m: (m, ka)),
                  pl.BlockSpec((tm, tn), lambda ka, b, j, m: (m, b * nj + j))],
        out_specs=pl.BlockSpec((None, tka, tn), lambda ka, b, j, m: (b, ka, j)),
        out_shape=jax.ShapeDtypeStruct((nb, Ka, ns), BF16),
        scratch_shapes=[pltpu.VMEM((tka, tn), F32)],
        compiler_params=_params(("parallel", "parallel", "parallel", "arbitrary")),
    )(a, g)


def _exact_mm(a, e, *, name):
    R, K = a.shape
    N = e.shape[1]

    def body(a_ref, e_ref, o_ref):
        x = a_ref[...]
        hi = x.astype(BF16)
        r1 = x - hi.astype(F32)
        mid = r1.astype(BF16)
        lo = (r1 - mid.astype(F32)).astype(BF16)
        ev = e_ref[...]
        o_ref[...] = (jnp.dot(hi, ev, preferred_element_type=F32) + jnp.dot(mid, ev, preferred_element_type=F32)
                      + jnp.dot(lo, ev, preferred_element_type=F32))

    return pl.pallas_call(body, name=name, out_shape=jax.ShapeDtypeStruct((R, N), F32),
                          compiler_params=pltpu.CompilerParams(vmem_limit_bytes=VMEM_LIMIT))(a, e)


def _rms_fwd(x, g, dep, *, name):
    L, D = x.shape
    tl = _tile(L, (256, 128))

    def body(x_ref, g_ref, _dep, h_ref):
        xv = x_ref[...]
        rstd = lax.rsqrt(jnp.mean(xv * xv, axis=-1, keepdims=True) + NORM_EPS)
        h_ref[...] = (xv * rstd * g_ref[...]).astype(BF16)

    return pl.pallas_call(
        body, name=name, grid=(L // tl,),
        in_specs=[pl.BlockSpec((tl, D), lambda t: (t, 0)), pl.BlockSpec((1, D), lambda t: (0, 0)), ANY_SPEC],
        out_specs=pl.BlockSpec((tl, D), lambda t: (t, 0)),
        out_shape=jax.ShapeDtypeStruct((L, D), BF16),
        compiler_params=_params(("parallel",)),
    )(x, g, dep)


def _rms_bwd(x, g, dy, dres, *, name):
    L, D = x.shape
    tl = _tile(L, (128,))

    def body(x_ref, g_ref, dy_ref, dres_ref, dx_ref, dxb_ref, dg_ref):
        t = pl.program_id(0)
        xv = x_ref[...]
        rstd = lax.rsqrt(jnp.mean(xv * xv, axis=-1, keepdims=True) + NORM_EPS)
        xhat = xv * rstd
        dyv = dy_ref[...]
        dxhat = dyv * g_ref[...]
        c = jnp.mean(dxhat * xhat, axis=-1, keepdims=True)
        dx = dres_ref[...] + rstd * (dxhat - xhat * c)
        dx_ref[...] = dx
        dxb_ref[...] = dx.astype(BF16)
        dgp = jnp.sum(dyv * xhat, axis=0, keepdims=True)

        @pl.when(t == 0)
        def _():
            dg_ref[...] = dgp

        @pl.when(t > 0)
        def _():
            dg_ref[...] += dgp

    row = pl.BlockSpec((tl, D), lambda t: (t, 0))
    vec = pl.BlockSpec((1, D), lambda t: (0, 0))
    return pl.pallas_call(
        body, name=name, grid=(L // tl,), in_specs=[row, vec, row, row], out_specs=[row, row, vec],
        out_shape=[jax.ShapeDtypeStruct((L, D), F32), jax.ShapeDtypeStruct((L, D), BF16),
                   jax.ShapeDtypeStruct((1, D), F32)],
        compiler_params=_params(("arbitrary",)),
    )(x, g, dy, dres)


def _gate_fwd(proj, ta, tb, tc, *, name):
    L, D = ta.shape
    tl, tcw = _tile(L, (256, 128)), _tile(D, (512, 256, 128))
    off = N_QKV_CHUNKS * HEAD_DIM // tcw
    nd = D // tcw

    def body(g0, g1, g2, a_ref, b_ref, c_ref, o_ref):
        m = (jax.nn.sigmoid(g0[...]) * a_ref[...] + jax.nn.sigmoid(g1[...]) * b_ref[...]
             + jax.nn.sigmoid(g2[...]) * c_ref[...])
        o_ref[...] = m.astype(BF16)

    blk = pl.BlockSpec((tl, tcw), lambda t, j: (t, j))
    gl = [pl.BlockSpec((tl, tcw), functools.partial(lambda t, j, i: (t, off + i * nd + j), i=i)) for i in range(3)]
    return pl.pallas_call(
        body, name=name, grid=(L // tl, nd), in_specs=gl + [blk, blk, blk], out_specs=blk,
        out_shape=jax.ShapeDtypeStruct((L, D), BF16),
        compiler_params=_params(("parallel", "parallel")),
    )(proj, proj, proj, ta, tb, tc)


def _gate_bwd(proj, ta, tb, tc, dmerged, *, name):
    L, D = ta.shape
    ncols = proj.shape[1]
    tl, tcw = _tile(L, (256, 128)), _tile(D, (512, 256, 128))
    off = N_QKV_CHUNKS * HEAD_DIM // tcw
    nd = D // tcw

    def body(g0, g1, g2, a_ref, b_ref, c_ref, dm_ref, da_ref, db_ref, dc_ref, dgl_ref):
        i = pl.program_id(2)
        sg = jax.nn.sigmoid(jnp.where(i == 0, g0[...], jnp.where(i == 1, g1[...], g2[...])))
        sel_t = jnp.where(i == 0, a_ref[...], jnp.where(i == 1, b_ref[...], c_ref[...]))
        dt = dm_ref[...] * sg
        dtb = dt.astype(BF16)

        @pl.when(i == 0)
        def _():
            da_ref[...] = dtb

        @pl.when(i == 1)
        def _():
            db_ref[...] = dtb

        @pl.when(i == 2)
        def _():
            dc_ref[...] = dtb

        dgl_ref[...] = (dt * sel_t * (1.0 - sg)).astype(BF16)

    blk = pl.BlockSpec((tl, tcw), lambda t, j, i: (t, j))
    gl = [pl.BlockSpec((tl, tcw), functools.partial(lambda t, j, i, q: (t, off + q * nd + j), q=q)) for q in range(3)]
    return pl.pallas_call(
        body, name=name, grid=(L // tl, nd, 3), in_specs=gl + [blk, blk, blk, blk],
        out_specs=[blk, blk, blk, pl.BlockSpec((tl, tcw), lambda t, j, i: (t, off + i * nd + j))],
        out_shape=[jax.ShapeDtypeStruct((L, D), BF16)] * 3 + [jax.ShapeDtypeStruct((L, ncols), BF16)],
        compiler_params=_params(("parallel", "parallel", "arbitrary")),
    )(proj, proj, proj, ta, tb, tc, dmerged)


def _swiglu_fwd(gu, *, name):
    L, F2 = gu.shape
    F = F2 // 2
    tl = _tile(L, (128, 64))

    def body(gu_ref, o_ref):
        gt = gu_ref[:, :F]
        o_ref[...] = (gt * jax.nn.sigmoid(gt) * gu_ref[:, F:]).astype(BF16)

    return pl.pallas_call(
        body, name=name, grid=(L // tl,), in_specs=[pl.BlockSpec((tl, F2), lambda t: (t, 0))],
        out_specs=pl.BlockSpec((tl, F), lambda t: (t, 0)),
        out_shape=jax.ShapeDtypeStruct((L, F), BF16),
        compiler_params=_params(("parallel",)),
    )(gu)


def _swiglu_bwd(gu, dact, *, name):
    L, F2 = gu.shape
    F = F2 // 2
    tl = _tile(L, (64,))

    def body(gu_ref, d_ref, o_ref):
        gt, up, d = gu_ref[:, :F], gu_ref[:, F:], d_ref[...]
        sg = jax.nn.sigmoid(gt)
        o_ref[:, :F] = (d * up * sg * (1.0 + gt * (1.0 - sg))).astype(BF16)
        o_ref[:, F:] = (d * gt * sg).astype(BF16)

    return pl.pallas_call(
        body, name=name, grid=(L // tl,),
        in_specs=[pl.BlockSpec((tl, F2), lambda t: (t, 0)), pl.BlockSpec((tl, F), lambda t: (t, 0))],
        out_specs=pl.BlockSpec((tl, F2), lambda t: (t, 0)),
        out_shape=jax.ShapeDtypeStruct((L, F2), BF16),
        compiler_params=_params(("parallel",)),
    )(gu, dact)


def _loss(y, tgt, *, name):
    L, D = y.shape
    tl = _tile(L, (256, 128))
    nt = L // tl

    def body(y_ref, t_ref, dy_ref, dyb_ref, loss_ref, acc_ref):
        t = pl.program_id(0)
        e = y_ref[...] - t_ref[...]
        dy = e * (1.0 / D)
        dy_ref[...] = dy
        dyb_ref[...] = dy.astype(BF16)
        part = jnp.sum(e * e, axis=0, keepdims=True)

        @pl.when(t == 0)
        def _():
            acc_ref[...] = part

        @pl.when(t > 0)
        def _():
            acc_ref[...] += part

        @pl.when(t == nt - 1)
        def _():
            loss_ref[...] = jnp.broadcast_to(jnp.sum(acc_ref[...], axis=-1, keepdims=True) * (0.5 / D), (1, 128))

    row = pl.BlockSpec((tl, D), lambda t: (t, 0))
    return pl.pallas_call(
        body, name=name, grid=(nt,), in_specs=[row, row],
        out_specs=[row, row, pl.BlockSpec((1, 128), lambda t: (0, 0))],
        out_shape=[jax.ShapeDtypeStruct((L, D), F32), jax.ShapeDtypeStruct((L, D), BF16),
                   jax.ShapeDtypeStruct((1, 128), F32)],
        scratch_shapes=[pltpu.VMEM((1, D), F32)],
        compiler_params=_params(("arbitrary",)),
    )(y, tgt)


def _rope(v, cos, sin_signed):
    return v * cos + pltpu.roll(v, HEAD_DIM // 2, 1) * sin_signed


def _qk_fwd(proj, gtab, cos, sin, pieces, nchunks, rope_upto, cg, *, name, dep=None):
    L = proj.shape[0]
    tl = _tile(L, (512, 256, 128))
    W = cg * HEAD_DIM
    pmap = _piece_map(tuple((a // cg, n // cg, p // cg) for a, n, p in pieces))

    def body(*refs):
        p_ref, g_ref, cos_ref, sin_ref = refs[:4]
        o_ref = refs[-1]
        c = pl.program_id(1)

        def norm(j):
            cols = slice(j * HEAD_DIM, (j + 1) * HEAD_DIM)
            x = p_ref[:, cols]
            rstd = lax.rsqrt(jnp.mean(x * x, axis=-1, keepdims=True) + NORM_EPS)
            return cols, x * rstd * g_ref[:, cols]

        @pl.when(c < rope_upto // cg)
        def _():
            for j in range(cg):
                cols, y = norm(j)
                o_ref[:, cols] = _rope(y, cos_ref[...], sin_ref[...])

        @pl.when(c >= rope_upto // cg)
        def _():
            for j in range(cg):
                cols, y = norm(j)
                o_ref[:, cols] = y

    pos = pl.BlockSpec((tl, HEAD_DIM), lambda t, c: (t, 0))
    in_specs = [pl.BlockSpec((tl, W), lambda t, c: (t, pmap(c))),
                pl.BlockSpec((None, 1, W), lambda t, c: (c, 0, 0)), pos, pos]
    args = [proj, gtab, cos, sin]
    if dep is not None:
        in_specs.append(ANY_SPEC)
        args.append(dep)
    return pl.pallas_call(
        body, name=name, grid=(L // tl, nchunks // cg), in_specs=in_specs,
        out_specs=pl.BlockSpec((tl, W), lambda t, c: (t, c)),
        out_shape=jax.ShapeDtypeStruct((L, nchunks * HEAD_DIM), F32),
        compiler_params=_params(("parallel", "parallel")),
    )(*args)


def _qk_bwd(dqk, proj, gtab, cos, sin, dproj, pieces, nchunks, rope_upto, cg, *, name):
    L = proj.shape[0]
    tl = _tile(L, (512, 256, 128))
    W = cg * HEAD_DIM
    pmap = _piece_map(tuple((a // cg, n // cg, p // cg) for a, n, p in pieces))

    def body(d_ref, p_ref, g_ref, cos_ref, sin_ref, _, o_ref, dg_ref):
        c, t = pl.program_id(0), pl.program_id(1)

        @pl.when(t == 0)
        def _():
            dg_ref[...] = jnp.zeros_like(dg_ref)

        for j in range(cg):
            cols = slice(j * HEAD_DIM, (j + 1) * HEAD_DIM)
            x = p_ref[:, cols]
            rstd = lax.rsqrt(jnp.mean(x * x, axis=-1, keepdims=True) + NORM_EPS)
            xhat = x * rstd
            dy = d_ref[:, cols]
            dy = jnp.where(c < rope_upto // cg, _rope(dy, cos_ref[...], -sin_ref[...]), dy)
            dxhat = dy * g_ref[:, cols]
            cm = jnp.mean(dxhat * xhat, axis=-1, keepdims=True)
            o_ref[:, cols] = (rstd * (dxhat - xhat * cm)).astype(BF16)
            dg_ref[:, cols] += jnp.sum(dy * xhat, axis=0, keepdims=True)

    pos = pl.BlockSpec((tl, HEAD_DIM), lambda c, t: (t, 0))
    gspec = pl.BlockSpec((None, 1, W), lambda c, t: (c, 0, 0))
    out, dg = pl.pallas_call(
        body, name=name, grid=(nchunks // cg, L // tl),
        in_specs=[pl.BlockSpec((tl, W), lambda c, t: (t, c)),
                  pl.BlockSpec((tl, W), lambda c, t: (t, pmap(c))), gspec, pos, pos,
                  pl.BlockSpec(memory_space=pl.ANY)],
        out_specs=[pl.BlockSpec((tl, W), lambda c, t: (t, pmap(c))), gspec],
        out_shape=[jax.ShapeDtypeStruct(dproj.shape, BF16), jax.ShapeDtypeStruct((nchunks // cg, 1, W), F32)],
        input_output_aliases={5: 0},
        compiler_params=_params(("parallel", "arbitrary")),
    )(dqk, proj, gtab, cos, sin, dproj)
    return out, dg


def _v_bwd(dv, dproj, *, name):
    L = dv.shape[0]
    tl = _tile(L, (512, 256, 128))
    pmap = _piece_map(tuple((a // 2, n // 2, p // 2) for a, n, p in V_PIECES))

    def body(d_ref, _, o_ref):
        o_ref[...] = d_ref[...].astype(BF16)

    return pl.pallas_call(
        body, name=name, grid=(L // tl, NK_CHUNKS // 2),
        in_specs=[pl.BlockSpec((tl, 2 * HEAD_DIM), lambda t, c: (t, c)), pl.BlockSpec(memory_space=pl.ANY)],
        out_specs=pl.BlockSpec((tl, 2 * HEAD_DIM), lambda t, c: (t, pmap(c))),
        out_shape=jax.ShapeDtypeStruct(dproj.shape, BF16),
        input_output_aliases={1: 0},
        compiler_params=_params(("parallel", "parallel")),
    )(dv, dproj)


def _band_geometry(L, dil, radius):
    n = L // dil
    bq = min(256, max(n // 2, 64), n)
    width = min(bq + 2 * radius, n)
    return n, bq, width


def _band_rows(dil, r, first, count):
    if dil == 1:
        return pl.ds(pl.multiple_of(first, 8), count)
    return pl.ds(r + first * dil, count, stride=dil)


def _band_mask(i, bq, width, radius, ws):
    qpos = i * bq + lax.broadcasted_iota(jnp.int32, (bq, width), 0)
    kpos = ws + lax.broadcasted_iota(jnp.int32, (bq, width), 1)
    return jnp.abs(kpos - qpos) <= radius


def _band_fwd(qn, kn, proj, *, dil, radius, nkv, group, q0, k0, v0, sink=None, name):
    L = qn.shape[0]
    n, bq, width = _band_geometry(L, dil, radius)
    tq = bq * dil
    nh = nkv * group

    def body(*refs):
        if sink is None:
            q_ref, k_ref, v_ref, o_ref, lse_ref = refs
        else:
            q_ref, k_ref, v_ref, s_ref, o_ref, lse_ref = refs
        i = pl.program_id(2)
        ws = jnp.clip(i * bq - radius, 0, n - width)
        valid = _band_mask(i, bq, width, radius, ws)

        def one(r, carry):
            qrows = _band_rows(dil, r, 0, bq)
            krows = _band_rows(dil, r, ws, width)
            q = q_ref[qrows, :].astype(BF16)
            k = k_ref[krows, :].astype(BF16)
            v = v_ref[krows, :].astype(BF16)
            s = lax.dot_general(q, k, (((1,), (1,)), ((), ())), preferred_element_type=F32) * ATT_SCALE
            s = jnp.where(valid, s, NEG)
            m = jnp.max(s, axis=-1, keepdims=True)
            if sink is not None:
                m = jnp.maximum(m, s_ref[...][:, :1])
            p = jnp.exp(s - m)
            denom = jnp.sum(p, axis=-1, keepdims=True)
            if sink is not None:
                denom = denom + jnp.exp(s_ref[...][:, :1] - m)
            pn = (p / denom).astype(BF16)
            o_ref[qrows, :] = jnp.dot(pn, v, preferred_element_type=F32)
            lse_ref[qrows, :] = jnp.broadcast_to(m + jnp.log(denom), (bq, HEAD_DIM))
            return carry

        if dil == 1:
            one(0, 0)
        else:
            lax.fori_loop(0, dil, one, 0)

    qspec = pl.BlockSpec((tq, HEAD_DIM), lambda hk, g, i: (i, q0 + hk * group + g))
    in_specs = [qspec,
                pl.BlockSpec((L, HEAD_DIM), lambda hk, g, i: (0, k0 + hk)),
                pl.BlockSpec((L, HEAD_DIM), lambda hk, g, i: (0, v0 + hk))]
    args = [qn, kn, proj]
    if sink is not None:
        in_specs.append(pl.BlockSpec((None, 1, HEAD_DIM), lambda hk, g, i: (hk * group + g, 0, 0)))
        args.append(sink)
    ospec = pl.BlockSpec((tq, HEAD_DIM), lambda hk, g, i: (i, hk * group + g))
    return pl.pallas_call(
        body, name=name, grid=(nkv, group, n // bq), in_specs=in_specs, out_specs=[ospec, ospec],
        out_shape=[jax.ShapeDtypeStruct((L, nh * HEAD_DIM), F32)] * 2,
        compiler_params=_params(("parallel", "parallel", "arbitrary")),
    )(*args)


def _band_bwd(qn, kn, proj, do, o, lse, dq_buf, dk_buf, dv_buf, *, dil, radius, nkv, group, q0, k0, v0, o0,
              sink=None, name):
    L = qn.shape[0]
    n, bq, width = _band_geometry(L, dil, radius)
    tq = bq * dil
    nh = nkv * group
    n_in = 6 + (1 if sink is not None else 0)

    def body(*refs):
        q_ref, k_ref, v_ref, do_ref, o_ref, lse_ref = refs[:6]
        s_ref = refs[6] if sink is not None else None
        outs = refs[n_in + 3:]
        dq_ref, dk_ref, dv_ref = outs[:3]
        ds_ref = outs[3] if sink is not None else None
        g, i = pl.program_id(1), pl.program_id(2)
        ws = jnp.clip(i * bq - radius, 0, n - width)
        valid = _band_mask(i, bq, width, radius, ws)

        @pl.when((g == 0) & (i == 0))
        def _():
            dk_ref[...] = jnp.zeros_like(dk_ref)
            dv_ref[...] = jnp.zeros_like(dv_ref)

        if sink is not None:
            @pl.when(i == 0)
            def _():
                ds_ref[...] = jnp.zeros_like(ds_ref)

        def one(r, carry):
            qrows = _band_rows(dil, r, 0, bq)
            krows = _band_rows(dil, r, ws, width)
            q = q_ref[qrows, :].astype(BF16)
            k = k_ref[krows, :].astype(BF16)
            v = v_ref[krows, :].astype(BF16)
            dov = do_ref[qrows, :]
            lse_v = lse_ref[qrows, :][:, :1]
            delta = jnp.sum(dov * o_ref[qrows, :], axis=-1, keepdims=True)
            dob = dov.astype(BF16)
            s = lax.dot_general(q, k, (((1,), (1,)), ((), ())), preferred_element_type=F32) * ATT_SCALE
            p = jnp.where(valid, jnp.exp(s - lse_v), 0.0)
            dp = lax.dot_general(dob, v, (((1,), (1,)), ((), ())), preferred_element_type=F32)
            dsb = (p * (dp - delta)).astype(BF16)
            dq_ref[qrows, :] = jnp.dot(dsb, k, preferred_element_type=F32) * ATT_SCALE
            dk_ref[krows, :] += lax.dot_general(dsb, q, (((0,), (0,)), ((), ())),
                                                preferred_element_type=F32) * ATT_SCALE
            dv_ref[krows, :] += lax.dot_general(p.astype(BF16), dob, (((0,), (0,)), ((), ())),
                                                preferred_element_type=F32)
            if sink is not None:
                ps = jnp.exp(s_ref[...][:, :1] - lse_v)
                ds_ref[...] += jnp.broadcast_to(jnp.sum(-ps * delta, axis=0, keepdims=True), (1, HEAD_DIM))
            return carry

        if dil == 1:
            one(0, 0)
        else:
            lax.fori_loop(0, dil, one, 0)

    hspec = pl.BlockSpec((tq, HEAD_DIM), lambda hk, g, i: (i, o0 + hk * group + g))
    qspec = pl.BlockSpec((tq, HEAD_DIM), lambda hk, g, i: (i, q0 + hk * group + g))
    kspec = pl.BlockSpec((L, HEAD_DIM), lambda hk, g, i: (0, k0 + hk))
    any_spec = pl.BlockSpec(memory_space=pl.ANY)
    in_specs = [qspec, kspec, pl.BlockSpec((L, HEAD_DIM), lambda hk, g, i: (0, v0 + hk)), hspec, hspec, hspec]
    args = [qn, kn, proj, do, o, lse]
    if sink is not None:
        in_specs.append(pl.BlockSpec((None, 1, HEAD_DIM), lambda hk, g, i: (hk * group + g, 0, 0)))
        args.append(sink)
    in_specs += [any_spec] * 3
    args += [dq_buf, dk_buf, dv_buf]
    out_specs = [qspec, kspec, kspec]
    out_shape = [jax.ShapeDtypeStruct(dq_buf.shape, F32), jax.ShapeDtypeStruct(dk_buf.shape, F32),
                 jax.ShapeDtypeStruct(dv_buf.shape, F32)]
    if sink is not None:
        out_specs.append(pl.BlockSpec((None, 1, HEAD_DIM), lambda hk, g, i: (hk * group + g, 0, 0)))
        out_shape.append(jax.ShapeDtypeStruct((nh, 1, HEAD_DIM), F32))
    return pl.pallas_call(
        body, name=name, grid=(nkv, group, n // bq), in_specs=in_specs, out_specs=out_specs, out_shape=out_shape,
        input_output_aliases={n_in: 0, n_in + 1: 1, n_in + 2: 2},
        compiler_params=_params(("parallel", "arbitrary", "arbitrary")),
    )(*args)


def _combine_b(os_, lses, *, name):
    L, W = os_[0].shape
    tl = _tile(L, (256, 128))

    def body(o0, o1, o2, l0, l1, l2, out_ref, lt_ref):
        a, b, c = l0[...], l1[...], l2[...]
        m = jnp.maximum(jnp.maximum(a, b), c)
        ea, eb, ec = jnp.exp(a - m), jnp.exp(b - m), jnp.exp(c - m)
        tot = ea + eb + ec
        out_ref[...] = (ea * o0[...] + eb * o1[...] + ec * o2[...]) / tot
        lt_ref[...] = m + jnp.log(tot)

    blk = pl.BlockSpec((tl, W), lambda t: (t, 0))
    return pl.pallas_call(
        body, name=name, grid=(L // tl,), in_specs=[blk] * 6, out_specs=[blk, blk],
        out_shape=[jax.ShapeDtypeStruct((L, W), F32)] * 2, compiler_params=_params(("parallel",)),
    )(*os_, *lses)


C_QROWS = 4
C_KROWS = C_QROWS + C_WIN_ROWS
C_QUERIES, C_KEYS = C_QROWS * GRID_W, C_KROWS * GRID_W
_C_KIND_OFFSETS = (C_WIN_ROWS - 1, C_WIN_ROWS - 1 - C_WIN_ROWS // 2, C_WIN_ROWS - 1 - (C_KROWS - C_QROWS))


def _c_geometry(L):
    rows = L // GRID_W
    assert rows >= C_KROWS and rows % C_QROWS == 0
    return rows


def _c_bias_tiles(bias_t):
    cq = np.arange(GRID_W)[:, None]
    ck = np.arange(GRID_W)[None, :]
    start = np.clip(cq - C_WIN_COLS // 2, 0, GRID_W - C_WIN_COLS)
    masked = jnp.where(jnp.asarray((ck >= start) & (ck < start + C_WIN_COLS)), bias_t, NEG)
    blank = jnp.full((C_HEADS, GRID_W, GRID_W), NEG, F32)
    kinds = []
    for kind in range(3):
        off = _C_KIND_OFFSETS[kind]
        row_blocks = []
        for a in range(C_QROWS):
            lo = (0, a, C_KROWS - C_WIN_ROWS)[kind]
            row_blocks.append(jnp.concatenate(
                [masked[:, b - a + off] if lo <= b < lo + C_WIN_ROWS else blank for b in range(C_KROWS)], axis=-1))
        kinds.append(jnp.concatenate(row_blocks, axis=-2))
    return jnp.stack(kinds, axis=1)


def _c_block(g, rows):
    r0 = g * C_QROWS
    k0 = jnp.clip(r0 - C_WIN_ROWS // 2, 0, rows - C_KROWS)
    kind = jnp.where(g == 0, 0, jnp.where(g == rows // C_QROWS - 1, 2, 1))
    return k0, kind, k0 - r0 + (C_WIN_ROWS - 1)


def _c_fwd(qn, kn, proj, tiles, *, name):
    L = qn.shape[0]
    rows = _c_geometry(L)

    def body(q_ref, k_ref, v_ref, t_ref, o_ref, lse_ref):
        k0, _, _ = _c_block(pl.program_id(1), rows)
        krows = pl.ds(pl.multiple_of(k0 * GRID_W, GRID_W), C_KEYS)
        q = q_ref[...].astype(BF16)
        k = k_ref[krows, :].astype(BF16)
        v = v_ref[krows, :].astype(BF16)
        s = lax.dot_general(q, k, (((1,), (1,)), ((), ())), preferred_element_type=F32) * ATT_SCALE + t_ref[...]
        m = jnp.max(s, axis=-1, keepdims=True)
        p = jnp.exp(s - m)
        denom = jnp.sum(p, axis=-1, keepdims=True)
        o_ref[...] = jnp.dot((p / denom).astype(BF16), v, preferred_element_type=F32)
        lse_ref[...] = jnp.broadcast_to(m + jnp.log(denom), (C_QUERIES, HEAD_DIM))

    def tile_index(h, g):
        return (h, _c_block(g, rows)[1], 0, 0)

    ospec = pl.BlockSpec((C_QUERIES, HEAD_DIM), lambda h, g: (g, h))
    return pl.pallas_call(
        body, name=name, grid=(C_HEADS, rows // C_QROWS),
        in_specs=[pl.BlockSpec((C_QUERIES, HEAD_DIM), lambda h, g: (g, 20 + h)),
                  pl.BlockSpec((L, HEAD_DIM), lambda h, g: (0, 14 + h)),
                  pl.BlockSpec((L, HEAD_DIM), lambda h, g: (0, PC_VC + h)),
                  pl.BlockSpec((None, None, C_QUERIES, C_KEYS), tile_index)],
        out_specs=[ospec, ospec],
        out_shape=[jax.ShapeDtypeStruct((L, C_HEADS * HEAD_DIM), F32)] * 2,
        compiler_params=_params(("parallel", "arbitrary")),
    )(qn, kn, proj, tiles)


def _c_bwd(qn, kn, proj, tiles, do, o, lse, dq_buf, dk_buf, dv_buf, *, name):
    L = qn.shape[0]
    rows = _c_geometry(L)

    def body(q_ref, k_ref, v_ref, t_ref, do_ref, o_ref, lse_ref, _a, _b, _c, dq_ref, dk_ref, dv_ref, dt_ref):
        g = pl.program_id(1)
        k0, _, off = _c_block(g, rows)
        krows = pl.ds(pl.multiple_of(k0 * GRID_W, GRID_W), C_KEYS)

        @pl.when(g == 0)
        def _():
            dk_ref[...] = jnp.zeros_like(dk_ref)
            dv_ref[...] = jnp.zeros_like(dv_ref)
            dt_ref[...] = jnp.zeros_like(dt_ref)

        q = q_ref[...].astype(BF16)
        k = k_ref[krows, :].astype(BF16)
        v = v_ref[krows, :].astype(BF16)
        dov = do_ref[...]
        dob = dov.astype(BF16)
        delta = jnp.sum(dov * o_ref[...], axis=-1, keepdims=True)
        s = lax.dot_general(q, k, (((1,), (1,)), ((), ())), preferred_element_type=F32) * ATT_SCALE + t_ref[...]
        p = jnp.exp(s - lse_ref[...][:, :1])
        dp = lax.dot_general(dob, v, (((1,), (1,)), ((), ())), preferred_element_type=F32)
        ds = p * (dp - delta)
        for a in range(C_QROWS):
            for b in range(C_KROWS):
                rel = jnp.clip(b - a + off, 0, C_NREL - 1)
                dt_ref[rel] += ds[a * GRID_W:(a + 1) * GRID_W, b * GRID_W:(b + 1) * GRID_W]
        dsb = ds.astype(BF16)
        dq_ref[...] = jnp.dot(dsb, k, preferred_element_type=F32) * ATT_SCALE
        dk_ref[krows, :] += lax.dot_general(dsb, q, (((0,), (0,)), ((), ())), preferred_element_type=F32) * ATT_SCALE
        dv_ref[krows, :] += lax.dot_general(p.astype(BF16), dob, (((0,), (0,)), ((), ())),
                                            preferred_element_type=F32)

    def tile_index(h, g):
        return (h, _c_block(g, rows)[1], 0, 0)

    hspec = pl.BlockSpec((C_QUERIES, HEAD_DIM), lambda h, g: (g, h))
    qspec = pl.BlockSpec((C_QUERIES, HEAD_DIM), lambda h, g: (g, 20 + h))
    kspec = pl.BlockSpec((L, HEAD_DIM), lambda h, g: (0, 14 + h))
    any_spec = pl.BlockSpec(memory_space=pl.ANY)
    return pl.pallas_call(
        body, name=name, grid=(C_HEADS, rows // C_QROWS),
        in_specs=[qspec, kspec, pl.BlockSpec((L, HEAD_DIM), lambda h, g: (0, PC_VC + h)),
                  pl.BlockSpec((None, None, C_QUERIES, C_KEYS), tile_index),
                  hspec, hspec, hspec, any_spec, any_spec, any_spec],
        out_specs=[qspec, kspec, kspec,
                   pl.BlockSpec((None, C_NREL, GRID_W, GRID_W), lambda h, r: (h, 0, 0, 0))],
        out_shape=[jax.ShapeDtypeStruct(dq_buf.shape, F32), jax.ShapeDtypeStruct(dk_buf.shape, F32),
                   jax.ShapeDtypeStruct(dv_buf.shape, F32),
                   jax.ShapeDtypeStruct((C_HEADS, C_NREL, GRID_W, GRID_W), F32)],
        input_output_aliases={7: 0, 8: 1, 9: 2},
        compiler_params=_params(("parallel", "arbitrary")),
    )(qn, kn, proj, tiles, do, o, lse, dq_buf, dk_buf, dv_buf)


def _c_expand_matrix():
    cq = np.arange(GRID_W)[:, None]
    ck = np.arange(GRID_W)[None, :]
    d = (ck - cq + (C_WIN_COLS - 1)).reshape(-1)
    e = np.zeros((GRID_W * GRID_W, HEAD_DIM), np.float32)
    okd = (d >= 0) & (d < C_NCOL)
    e[np.arange(GRID_W * GRID_W)[okd], d[okd]] = 1.0
    return e


def _peer(p):
    return (p // 4, (p // 2) % 2, p % 2)


def _my_index():
    return 4 * lax.axis_index("x") + 2 * lax.axis_index("y") + lax.axis_index("c")


HBM_SPEC = pl.BlockSpec(memory_space=pltpu.HBM)
SEM_SPEC = pl.BlockSpec(memory_space=pltpu.SEMAPHORE)
ANY_SPEC = pl.BlockSpec(memory_space=pl.ANY)
DATAFLOW = pltpu.SideEffectType.DATAFLOW_SIDE_EFFECTING


_EXCHANGE_TRANSFERS = {"scatter": N_DEV - 1, "gather1": 4, "gather2": 3}


def _exchange_views(mode, kinds, arrays):
    nw = len(kinds)
    gather = mode != "scatter"
    if gather:
        rows = [a.shape[0] // N_DEV for a in arrays[:nw]]
    else:
        rows = [a.shape[1] // N_DEV for a in arrays[:nw]]

    def gather_slot(ref, w, who):
        return ref.at[who] if kinds[w] == "col" else ref.at[pl.ds(who * rows[w], rows[w]), :]

    x, y, c = lax.axis_index("x"), lax.axis_index("y"), lax.axis_index("c")
    me = 4 * x + 2 * y + c
    chips = [(1 - x, y), (x, 1 - y), (1 - x, 1 - y)]

    def index(px, py, pc):
        return 4 * px + 2 * py + pc

    if mode == "scatter":
        plan = [(_peer((me + off) % N_DEV), (me + off) % N_DEV, (me + N_DEV - off) % N_DEV)
                for off in range(1, N_DEV)]
    elif mode == "gather1":
        plan = [((x, y, 1 - c), me, index(x, y, 1 - c))] + [((px, py, c), me, index(px, py, c)) for px, py in chips]
    else:
        plan = [((x, y, 1 - c), index(px, py, c), index(px, py, 1 - c)) for px, py in chips]

    def src(ref, w, j):
        sent = plan[j][1]
        if gather:
            return gather_slot(ref, w, sent)
        return ref.at[sent] if kinds[w] == "col" else ref.at[0, pl.ds(sent * rows[w], rows[w]), :]

    def dst(ref, w, j):
        return gather_slot(ref, w, plan[j][1]) if gather else ref.at[me]

    def arrival(ref, w, j):
        return gather_slot(ref, w, plan[j][2]) if gather else ref.at[plan[j][2]]

    return [p[0] for p in plan], src, dst, arrival


def _place_cast(w, layer, kind, *, name):
    _, R, C = w.shape
    tr = _tile(R, (256, 128, 64, 32, 16))

    def body(w_ref, o_ref):
        o_ref[...] = w_ref[...].astype(BF16)

    if kind == "col":
        out_shape = jax.ShapeDtypeStruct((N_DEV, R, C), BF16)
        out_spec = pl.BlockSpec((None, tr, C), lambda t: (_my_index(), t, 0))
    else:
        out_shape = jax.ShapeDtypeStruct((N_DEV * R, C), BF16)
        out_spec = pl.BlockSpec((tr, C), lambda t: (_my_index() * (R // tr) + t, 0))
    return pl.pallas_call(
        body, name=name, grid=(R // tr,), in_specs=[pl.BlockSpec((None, tr, C), lambda t: (layer, t, 0))],
        out_specs=out_spec, out_shape=out_shape, compiler_params=_params(("parallel",)),
    )(w)


def _exchange_start(mode, srcs, lands, kinds, after, *, name):
    nw = len(lands)
    ns = len(srcs)
    arrays = list(srcs) + list(lands)
    na = len(arrays)
    nx = _EXCHANGE_TRANSFERS[mode]

    def body(*refs):
        l_refs = refs[ns:ns + nw]
        s_refs = refs[:ns] if ns else l_refs
        send_sems, recv_sems = refs[ns + nw + 1], refs[ns + nw + 2]
        token = refs[-1]
        peers, src, dst, _ = _exchange_views(mode, kinds, arrays)
        for j in range(nx):
            for w in range(nw):
                pltpu.make_async_remote_copy(src(s_refs[w], w, j), dst(l_refs[w], w, j),
                                             send_sems.at[w * nx + j], recv_sems.at[w * nx + j],
                                             device_id=peers[j], device_id_type=MESH).start()
        token[...] = jnp.zeros_like(token)

    outs = pl.pallas_call(
        body, name=name,
        out_shape=(pltpu.SemaphoreType.DMA((nw * nx,)), pltpu.SemaphoreType.DMA((nw * nx,)),
                   *[pltpu.HBM(a.shape, a.dtype) for a in arrays], jax.ShapeDtypeStruct((8, 128), F32)),
        in_specs=[HBM_SPEC] * na + [ANY_SPEC],
        out_specs=(SEM_SPEC, SEM_SPEC, *([HBM_SPEC] * na), pl.BlockSpec(memory_space=pltpu.VMEM)),
        input_output_aliases={k: 2 + k for k in range(na)},
        compiler_params=pltpu.CompilerParams(has_side_effects=DATAFLOW),
    )(*[pltpu.with_memory_space_constraint(a, pltpu.HBM) for a in arrays], after)
    return outs[0], outs[1], outs[2:2 + ns], outs[2 + ns:2 + na], outs[-1]


def _exchange_wait(mode, started, kinds, after, *, name):
    send_sems, recv_sems, srcs, lands, _ = started
    nw = len(lands)
    ns = len(srcs)
    arrays = list(srcs) + list(lands)
    na = len(arrays)
    nx = _EXCHANGE_TRANSFERS[mode]

    def body(*refs):
        l_refs = refs[ns:na]
        s_refs = refs[:ns] if ns else l_refs
        send_ref, recv_ref = refs[na], refs[na + 1]
        peers, src, _, arrival = _exchange_views(mode, kinds, arrays)
        for j in range(nx):
            for w in range(nw):
                cp = pltpu.make_async_remote_copy(src(s_refs[w], w, j), arrival(l_refs[w], w, j),
                                                  send_ref.at[w * nx + j], recv_ref.at[w * nx + j],
                                                  device_id=peers[j], device_id_type=MESH)
                cp.wait_send()
                cp.wait_recv()

    outs = pl.pallas_call(
        body, name=name, out_shape=[pltpu.HBM(a.shape, a.dtype) for a in arrays],
        in_specs=[HBM_SPEC] * na + [SEM_SPEC, SEM_SPEC, ANY_SPEC], out_specs=[HBM_SPEC] * na,
        input_output_aliases={k: k for k in range(na)},
        compiler_params=pltpu.CompilerParams(has_side_effects=DATAFLOW),
    )(*arrays, send_sems, recv_sems, after)
    return outs[:ns], outs[ns:]


def _all_reduce_small(x):
    R = x.shape[0]

    def body(x_ref, o_ref, gath, send_sems, recv_sems):
        me = _my_index()
        gath[me] = x_ref[...]
        sends = []
        for off in range(1, N_DEV):
            to = (me + off) % N_DEV
            cp = pltpu.make_async_remote_copy(x_ref, gath.at[me], send_sems.at[off], recv_sems.at[off],
                                              device_id=_peer(to), device_id_type=MESH)
            cp.start()
            sends.append(cp)
        for off in range(1, N_DEV):
            frm = (me + N_DEV - off) % N_DEV
            pltpu.make_async_remote_copy(x_ref, gath.at[frm], send_sems.at[off], recv_sems.at[off],
                                         device_id=_peer(frm), device_id_type=MESH).wait_recv()
        for cp in sends:
            cp.wait_send()
        acc = gath[0]
        for s in range(1, N_DEV):
            acc = acc + gath[s]
        o_ref[...] = acc

    vm = pl.BlockSpec(memory_space=pltpu.VMEM)
    return pl.pallas_call(
        body, name="all_reduce_small", in_specs=[vm], out_specs=vm, out_shape=jax.ShapeDtypeStruct((R, 128), F32),
        scratch_shapes=[pltpu.VMEM((N_DEV, R, 128), F32), pltpu.SemaphoreType.DMA((N_DEV,)),
                        pltpu.SemaphoreType.DMA((N_DEV,))],
        compiler_params=pltpu.CompilerParams(has_side_effects=True),
    )(x)


def _adamw_math(w, g, m, v):
    m = ADAM_B1 * m + (1.0 - ADAM_B1) * g
    v = ADAM_B2 * v + (1.0 - ADAM_B2) * (g * g)
    m_hat = m / (1.0 - ADAM_B1 ** ADAM_STEP)
    v_hat = v / (1.0 - ADAM_B2 ** ADAM_STEP)
    delta = -ADAM_LR * (m_hat / (jnp.sqrt(v_hat) + ADAM_EPS) + ADAM_WD * w)
    return delta, m, v


def _adamw_layer(recv, own, kind, w, m, v, outs, layer, dep, *, name):
    nl, R, C = w.shape
    tr = _tile(R, (128, 64, 32, 16))

    def body(r_ref, o_ref, w_ref, m_ref, v_ref, _0, _1, _2, _3, _dep, g_out, d_out, m_out, v_out, token):
        token[...] = jnp.zeros_like(token)
        me = _my_index()
        mine = o_ref[...].astype(F32)
        g = jnp.where(me == 0, mine, r_ref[0].astype(F32))
        for s in range(1, N_DEV):
            g = g + jnp.where(me == s, mine, r_ref[s].astype(F32))
        delta, mn, vn = _adamw_math(w_ref[...], g, m_ref[...], v_ref[...])
        g_out[...] = g
        d_out[...] = delta
        m_out[...] = mn
        v_out[...] = vn

    if kind == "col":
        own_spec = pl.BlockSpec((None, tr, C), lambda t: (_my_index(), t, 0))
    else:
        own_spec = pl.BlockSpec((None, tr, C), lambda t: (0, _my_index() * (R // tr) + t, 0))
    wspec = pl.BlockSpec((None, tr, C), lambda t: (layer, t, 0))
    res = pl.pallas_call(
        body, name=name, grid=(R // tr,),
        in_specs=[pl.BlockSpec((N_DEV, tr, C), lambda t: (0, t, 0)), own_spec] + [wspec] * 3 + [ANY_SPEC] * 5,
        out_specs=[wspec] * 4 + [pl.BlockSpec((8, 128), lambda t: (0, 0))],
        out_shape=[jax.ShapeDtypeStruct((nl, R, C), F32)] * 4 + [jax.ShapeDtypeStruct((8, 128), F32)],
        input_output_aliases={5: 0, 6: 1, 7: 2, 8: 3},
        compiler_params=_params(("arbitrary",)),
    )(recv, own, w, m, v, *outs, dep)
    return res[:4], res[4]


def _adamw_small(g, w, m, v):
    def body(g_ref, w_ref, m_ref, v_ref, d_out, m_out, v_out):
        delta, mn, vn = _adamw_math(w_ref[...], g_ref[...], m_ref[...], v_ref[...])
        d_out[...] = delta
        m_out[...] = mn
        v_out[...] = vn

    return pl.pallas_call(body, name="adamw_small", out_shape=[jax.ShapeDtypeStruct(g.shape, F32)] * 3)(g, w, m, v)


def _pack(arrays, rows):
    flat = jnp.concatenate([a.reshape(-1) for a in arrays])
    return jnp.pad(flat, (0, rows * 128 - flat.shape[0])).reshape(rows, 128)


def _unpack(packed, shapes):
    flat = packed.reshape(-1)
    out, pos = [], 0
    for s in shapes:
        size = int(np.prod(s))
        out.append(flat[pos:pos + size].reshape(s))
        pos += size
    return out


def kernel(x, norm1_g, w_in, qk_norm_g, sink_a, rpb_c, w_br_a, w_br_b, w_br_c, w_o, norm2_g, w_gate_up, w_down, loss_target, m_norm1_g, m_w_in, m_qk_norm_g, m_sink_a, m_rpb_c, m_w_br_a, m_w_br_b, m_w_br_c, m_w_o, m_norm2_g, m_w_gate_up, m_w_down, v_norm1_g, v_w_in, v_qk_norm_g, v_sink_a, v_rpb_c, v_w_br_a, v_w_br_b, v_w_br_c, v_w_o, v_norm2_g, v_w_gate_up, v_w_down):
    nl = w_in.shape[0]
    L, D = x.shape[1], x.shape[2]
    x0 = x.reshape(L, D)
    tgt = loss_target.reshape(L, D)

    big = [w_in, w_br_a, w_br_b, w_br_c, w_o, w_gate_up, w_down]
    kinds = ["col", "col", "col", "col", "row", "col", "row"]

    big_names = ["w_in", "w_br_a", "w_br_b", "w_br_c", "w_o", "w_gate_up", "w_down"]
    ALL = list(range(len(big)))
    REST = ALL[1:]

    def gather_place(i):
        return [_place_cast(w, i, k, name="gather_place_" + n) for w, k, n in zip(big, kinds, big_names)]

    def gather_start(mode, lands, sub, after, tag):
        return _exchange_start(mode, [], lands, [kinds[j] for j in sub], after, name=mode + "_start" + tag)

    def gather_wait(mode, started, sub, after, tag):
        return _exchange_wait(mode, started, [kinds[j] for j in sub], after, name=mode + "_wait" + tag)[1]

    def matmul_views(lands, sub):
        return [g.reshape((N_DEV, 1) + g.shape[1:]) if kinds[j] == "col" else g.reshape((1, 1) + g.shape)
                for g, j in zip(lands, sub)]

    half = HEAD_DIM // 2
    inv_freq = ROPE_THETA ** (-jnp.arange(half, dtype=F32) * 2.0 / HEAD_DIM)
    ang = jnp.arange(L, dtype=F32)[:, None] * inv_freq[None, :]
    cos = jnp.concatenate([jnp.cos(ang), jnp.cos(ang)], axis=-1)
    sin = jnp.concatenate([-jnp.sin(ang), jnp.sin(ang)], axis=-1)
    expand = jnp.asarray(_c_expand_matrix(), BF16)
    expand_t = jnp.asarray(_c_expand_matrix().T, BF16)

    def gain_tables(i):
        g = qk_norm_g[i]
        gq = jnp.concatenate([jnp.tile(g[0][None], (8, 1)), jnp.tile(g[2][None], (12, 1)), jnp.tile(g[4][None], (8, 1))])
        gk = jnp.concatenate([jnp.tile(g[1][None], (2, 1)), jnp.tile(g[3][None], (12, 1)), jnp.tile(g[5][None], (8, 1))])
        return (gq.reshape(NQ_CHUNKS // Q_CG, 1, Q_CG * HEAD_DIM), gk.reshape(NK_CHUNKS // K_CG, 1, K_CG * HEAD_DIM))

    def bias_table(i):
        rp = jnp.pad(rpb_c[i].reshape(C_HEADS * C_NREL, C_NCOL), ((0, 0), (0, HEAD_DIM - C_NCOL)))
        t = _exact_mm(rp, expand_t, name="c_bias_expand")
        return _c_bias_tiles(t.reshape(C_HEADS, C_NREL, GRID_W, GRID_W))

    def sink_table(i):
        return jnp.broadcast_to(sink_a[i][:, None, None], (A_Q_HEADS, 1, HEAD_DIM))

    saved = []
    gws = [None] * nl
    xi = x0
    lands0 = gather_place(0)
    lvl1 = gather_start("gather1", lands0[:1], [0], x0, "_first")
    lvl2 = gather_start("gather2", gather_wait("gather1", lvl1, [0], x0, "_first"), [0], x0, "_first")
    gws[0] = matmul_views(gather_wait("gather2", lvl2, [0], x0, "_first"), [0])
    rest1 = gather_start("gather1", lands0[1:], REST, gws[0][0], "_rest")
    dep = rest1[4]
    for i in range(nl):
        qk_dep = None
        if i >= 1 and i + 1 < nl:
            nxt1 = gather_start("gather1", gather_place(i + 1), ALL, dep, "")
            dep = nxt1[4]
        gw_in = gws[i][0]
        gq, gk = gain_tables(i)
        bias_t = bias_table(i)
        sink = sink_table(i)
        h1 = _rms_fwd(xi, norm1_g[i][None], dep, name="rms1_fwd")
        proj = _mm_nn(h1, gw_in, 0, out_dtype=F32, name="proj_fwd")
        if i == 0:
            rest2 = gather_start("gather2", gather_wait("gather1", rest1, REST, proj, "_rest"), REST, proj, "_rest")
            qk_dep = rest2[4]
            if nl > 1:
                nxt1 = gather_start("gather1", gather_place(1), ALL, rest2[4], "")
                qk_dep = nxt1[4]
        qn = _qk_fwd(proj, gq, cos, sin, Q_PIECES, NQ_CHUNKS, Q_ROPE_UPTO, Q_CG, name="qnorm_fwd", dep=qk_dep)
        kn = _qk_fwd(proj, gk, cos, sin, K_PIECES, NK_CHUNKS, K_ROPE_UPTO, K_CG, name="knorm_fwd")
        oa, lse_a = _band_fwd(qn, kn, proj, dil=1, radius=A_RADIUS, nkv=A_KV_HEADS, group=A_GROUP,
                              q0=0, k0=0, v0=PC_VA, sink=sink, name="attn_a_fwd")
        obs, lbs = [], []
        for g, (window, dil) in enumerate(B_PATTERNS):
            o_g, l_g = _band_fwd(qn, kn, proj, dil=dil, radius=window // (2 * dil), nkv=B_HG, group=1,
                                 q0=8 + g * B_HG, k0=2 + g * B_HG, v0=PC_VB + g * B_HG, name=f"attn_b{g}_fwd")
            obs.append(o_g)
            lbs.append(l_g)
        ob, lse_b = _combine_b(obs, lbs, name="attn_b_combine")
        oc, lse_c = _c_fwd(qn, kn, proj, bias_t, name="attn_c_fwd")
        if i == 0:
            gws[0] = gws[0] + matmul_views(gather_wait("gather2", rest2, REST, oc, "_rest"), REST)
        _, gw_a, gw_b, gw_c, gw_o, gw_gu, gw_d = gws[i]
        ta = _mm_nn(oa, gw_a, 0, out_dtype=F32, name="br_a_fwd")
        tb = _mm_nn(ob, gw_b, 0, out_dtype=F32, name="br_b_fwd")
        tc = _mm_nn(oc, gw_c, 0, out_dtype=F32, name="br_c_fwd")
        merged = _gate_fwd(proj, ta, tb, tc, name="gate_fwd")
        x1 = _mm_nn(merged, gw_o, 0, out_dtype=F32, name="wo_fwd", res=xi)
        dep = x1
        if i + 1 < nl:
            nxt2 = gather_start("gather2", gather_wait("gather1", nxt1, ALL, x1, ""), ALL, x1, "")
            dep = nxt2[4]
        h2 = _rms_fwd(x1, norm2_g[i][None], dep, name="rms2_fwd")
        gu = _mm_nn(h2, gw_gu, 0, out_dtype=F32, name="gate_up_fwd")
        act = _swiglu_fwd(gu, name="swiglu_fwd")
        x2 = _mm_nn(act, gw_d, 0, out_dtype=F32, name="down_fwd", res=x1)
        saved.append(dict(x=xi, h1=h1, proj=proj, qn=qn, kn=kn, oa=oa, lse_a=lse_a, ob=ob, lse_b=lse_b, oc=oc,
                          lse_c=lse_c, ta=ta, tb=tb, tc=tc, merged=merged, x1=x1, h2=h2, gu=gu, act=act,
                          gq=gq, gk=gk, bias_t=bias_t, sink=sink))
        xi = x2
        dep = x2
        if i + 1 < nl:
            gws[i + 1] = matmul_views(gather_wait("gather2", nxt2, ALL, x2, ""), ALL)

    dx, dxb, loss_row = _loss(xi, tgt, name="loss")

    def scatter_start(grads, sub, after, tag):
        lands = []
        for g, j in zip(grads, sub):
            shape = g.shape if kinds[j] == "col" else (N_DEV, g.shape[1] // N_DEV, g.shape[2])
            lands.append(lax.empty(shape, BF16))
        return _exchange_start("scatter", grads, lands, [kinds[j] for j in sub], after, name="scatter_start" + tag)

    def scatter_wait(pair, after):
        own_a, recv_a = _exchange_wait("scatter", pair[0], [kinds[0]], after, name="scatter_wait_in")
        own_b, recv_b = _exchange_wait("scatter", pair[1], [kinds[j] for j in REST], after, name="scatter_wait_rest")
        return list(recv_a) + list(recv_b), list(own_a) + list(own_b)

    small_grads = [None] * nl
    recv = [None] * nl
    own = [None] * nl
    pending = None
    for i in reversed(range(nl)):
        s = saved[i]
        gw_in, gw_a, gw_b, gw_c, gw_o, gw_gu, gw_d = gws[i]
        dact = _mm_nt(dxb, gw_d, 0, out_dtype=F32, name="down_bwd_x", dep=None if pending is None else pending[0][4])
        g_down = _mm_tn(s["act"], dxb, 1, name="down_bwd_w")
        dgu = _swiglu_bwd(s["gu"], dact, name="swiglu_bwd")
        g_gu = _mm_tn(s["h2"], dgu, N_DEV, name="gate_up_bwd_w")
        dh2 = _mm_nt(dgu, gw_gu, 0, out_dtype=F32, name="gate_up_bwd_x")
        dx1, dx1b, dg2 = _rms_bwd(s["x1"], norm2_g[i][None], dh2, dx, name="rms2_bwd")
        dmerged = _mm_nt(dx1b, gw_o, 0, out_dtype=F32, name="wo_bwd_x")
        g_o = _mm_tn(s["merged"], dx1b, 1, name="wo_bwd_w")
        dta, dtb, dtc, dproj = _gate_bwd(s["proj"], s["ta"], s["tb"], s["tc"], dmerged, name="gate_bwd")
        g_a = _mm_tn(s["oa"], dta, N_DEV, name="br_a_bwd_w")
        g_b = _mm_tn(s["ob"], dtb, N_DEV, name="br_b_bwd_w")
        g_c = _mm_tn(s["oc"], dtc, N_DEV, name="br_c_bwd_w")
        rest = scatter_start([g_a, g_b, g_c, g_o, g_gu, g_down], REST, g_c, "_rest")
        doa = _mm_nt(dta, gw_a, 0, out_dtype=F32, name="br_a_bwd_x", dep=rest[4])
        dob = _mm_nt(dtb, gw_b, 0, out_dtype=F32, name="br_b_bwd_x")
        doc = _mm_nt(dtc, gw_c, 0, out_dtype=F32, name="br_c_bwd_x")
        dq_buf = lax.empty((L, NQ_CHUNKS * HEAD_DIM), F32)
        dk_buf = lax.empty((L, NK_CHUNKS * HEAD_DIM), F32)
        dv_buf = lax.empty((L, NK_CHUNKS * HEAD_DIM), F32)
        dq_buf, dk_buf, dv_buf, dsink = _band_bwd(
            s["qn"], s["kn"], s["proj"], doa, s["oa"], s["lse_a"], dq_buf, dk_buf, dv_buf, dil=1, radius=A_RADIUS,
            nkv=A_KV_HEADS, group=A_GROUP, q0=0, k0=0, v0=PC_VA, o0=0, sink=s["sink"], name="attn_a_bwd")
        for g, (window, dil) in enumerate(B_PATTERNS):
            dq_buf, dk_buf, dv_buf = _band_bwd(
                s["qn"], s["kn"], s["proj"], dob, s["ob"], s["lse_b"], dq_buf, dk_buf, dv_buf, dil=dil,
                radius=window // (2 * dil), nkv=B_HG, group=1, q0=8 + g * B_HG, k0=2 + g * B_HG,
                v0=PC_VB + g * B_HG, o0=0, name=f"attn_b{g}_bwd")
        dq_buf, dk_buf, dv_buf, dbias_t = _c_bwd(s["qn"], s["kn"], s["proj"], s["bias_t"], doc, s["oc"], s["lse_c"],
                                                 dq_buf, dk_buf, dv_buf, name="attn_c_bwd")
        dproj, dgq = _qk_bwd(dq_buf, s["proj"], s["gq"], cos, sin, dproj, Q_PIECES, NQ_CHUNKS, Q_ROPE_UPTO, Q_CG,
                             name="qnorm_bwd")
        dproj, dgk = _qk_bwd(dk_buf, s["proj"], s["gk"], cos, sin, dproj, K_PIECES, NK_CHUNKS, K_ROPE_UPTO, K_CG,
                             name="knorm_bwd")
        dproj = _v_bwd(dv_buf, dproj, name="v_bwd")
        g_in = _mm_tn(s["h1"], dproj, N_DEV, name="proj_bwd_w")
        dh1 = _mm_nt(dproj, gw_in, 0, out_dtype=F32, name="proj_bwd_x")
        dx, dxb, dg1 = _rms_bwd(s["x"], norm1_g[i][None], dh1, dx1, name="rms1_bwd")
        if pending is not None:
            recv[i + 1], own[i + 1] = scatter_wait(pending, dx)
        pending = (scatter_start([g_in], [0], dx, "_in"), rest)

        drpb = _exact_mm(dbias_t.reshape(C_HEADS * C_NREL, GRID_W * GRID_W), expand, name="c_bias_reduce")
        dgq, dgk = dgq.reshape(NQ_CHUNKS, HEAD_DIM), dgk.reshape(NK_CHUNKS, HEAD_DIM)
        dqk_g = jnp.stack([dgq[0:8].sum(0), dgk[0:2].sum(0), dgq[8:20].sum(0), dgk[2:14].sum(0),
                           dgq[20:28].sum(0), dgk[14:22].sum(0)])
        small_grads[i] = (dg1.reshape(D), dqk_g, dsink[:, 0, 0],
                          drpb[:, :C_NCOL].reshape(C_HEADS, C_NREL, C_NCOL), dg2.reshape(D))

    small_names = [norm1_g, qk_norm_g, sink_a, rpb_c, norm2_g]
    small_m = [m_norm1_g, m_qk_norm_g, m_sink_a, m_rpb_c, m_norm2_g]
    small_v = [v_norm1_g, v_qk_norm_g, v_sink_a, v_rpb_c, v_norm2_g]
    shapes = [a.shape for a in small_names]
    total = sum(int(np.prod(sh)) for sh in shapes) + 128
    rows = -(-total // 1024) * 8
    stacked = [jnp.stack([small_grads[i][j] for i in range(nl)]) for j in range(5)]
    packed = _pack([loss_row.reshape(-1)] + stacked, rows)
    summed = _all_reduce_small(packed)
    loss = summed[0, 0]
    zero_row = jnp.zeros((128,), F32)
    d_s, m_s, v_s = _adamw_small(summed, _pack([zero_row] + small_names, rows), _pack([zero_row] + small_m, rows),
                                 _pack([zero_row] + small_v, rows))
    shapes1 = [(128,)] + shapes
    g_small = _unpack(summed, shapes1)[1:]
    d_small = _unpack(d_s, shapes1)[1:]
    m_small = _unpack(m_s, shapes1)[1:]
    v_small = _unpack(v_s, shapes1)[1:]

    big_m = [m_w_in, m_w_br_a, m_w_br_b, m_w_br_c, m_w_o, m_w_gate_up, m_w_down]
    big_v = [v_w_in, v_w_br_a, v_w_br_b, v_w_br_c, v_w_o, v_w_gate_up, v_w_down]
    big_out = [[lax.empty(w.shape, F32) for _ in range(4)] for w in big]
    token = pending[0][4]
    for i in list(range(nl - 1, 0, -1)) + [0]:
        if i == 0:
            recv[0], own[0] = scatter_wait(pending, token)
        for j in range(len(big)):
            big_out[j], token = _adamw_layer(recv[i][j], own[i][j], kinds[j], big[j], big_m[j], big_v[j],
                                             big_out[j], i, token, name="adamw_" + big_names[j])

    order = ["norm1_g", "w_in", "qk_norm_g", "sink_a", "rpb_c", "w_br_a", "w_br_b", "w_br_c", "w_o", "norm2_g",
             "w_gate_up", "w_down"]
    small_idx = {"norm1_g": 0, "qk_norm_g": 1, "sink_a": 2, "rpb_c": 3, "norm2_g": 4}
    big_idx = {n: j for j, n in enumerate(big_names)}

    def pick(kind):
        out = []
        for n in order:
            if n in small_idx:
                out.append([g_small, d_small, m_small, v_small][kind][small_idx[n]])
            else:
                out.append(big_out[big_idx[n]][kind])
        return out

    return (loss, dx.reshape(1, L, D), *pick(0), *pick(1), *pick(2), *pick(3))
```

```python
import functools
import math

import numpy as np
import jax
import jax.numpy as jnp
from jax import lax
from jax.experimental import pallas as pl
from jax.experimental.pallas import tpu as pltpu

F32 = jnp.float32
BF16 = jnp.bfloat16
MESH = pl.DeviceIdType.MESH
N_DEV = 8

HEAD_DIM = 128
NORM_EPS = 1e-6
ROPE_THETA = 10000.0
ATT_SCALE = HEAD_DIM ** -0.5
NEG = -1e30

A_Q_HEADS, A_KV_HEADS, A_RADIUS = 8, 2, 128
A_GROUP = A_Q_HEADS // A_KV_HEADS
B_PATTERNS = ((128, 1), (512, 4), (2048, 16))
B_HG = 4
B_HEADS = len(B_PATTERNS) * B_HG
C_HEADS, GRID_W, C_WIN_ROWS, C_WIN_COLS = 8, 64, 8, 16
C_NREL = 2 * C_WIN_ROWS - 1
C_NCOL = 2 * C_WIN_COLS - 1

PC_QA, PC_KA, PC_VA = 0, 8, 10
PC_QB, PC_KB, PC_VB = 12, 24, 36
PC_QC, PC_KC, PC_VC = 48, 56, 64
N_QKV_CHUNKS = 72
Q_PIECES = ((0, 8, PC_QA), (8, 12, PC_QB), (20, 8, PC_QC))
K_PIECES = ((0, 2, PC_KA), (2, 12, PC_KB), (14, 8, PC_KC))
V_PIECES = ((0, 2, PC_VA), (2, 12, PC_VB), (14, 8, PC_VC))
NQ_CHUNKS, NK_CHUNKS = 28, 22
Q_ROPE_UPTO, K_ROPE_UPTO = 20, 14
Q_CG, K_CG = 4, 2

ADAM_LR, ADAM_B1, ADAM_B2, ADAM_EPS, ADAM_WD, ADAM_STEP = 0.001, 0.9, 0.999, 1e-08, 0.01, 10

VMEM_LIMIT = 48 * 1024 * 1024


def _tile(dim, prefs):
    for p in prefs:
        if dim % p == 0:
            return p
    return dim


NN_WEIGHT_TILE_BYTES = 8 * 1024 * 1024
NT_WEIGHT_TILE_BYTES = 4 * 1024 * 1024
TN_ACC_BYTES = 6 * 1024 * 1024
MAX_COL_TILE = 2048


def _col_tile(ns):
    return ns if ns <= MAX_COL_TILE else _tile(ns, (MAX_COL_TILE, 1024, 512, 256, 128))


def _params(sem, **kw):
    return pltpu.CompilerParams(dimension_semantics=sem, vmem_limit_bytes=VMEM_LIMIT, **kw)


def _piece_map(pieces):
    def f(c):
        out = c - pieces[0][0] + pieces[0][2]
        for first, _, pfirst in pieces[1:]:
            out = jnp.where(c >= first, c - first + pfirst, out)
        return out
    return f


def _mm_nn(a, w, layer, *, out_dtype, name, res=None):
    M, K = a.shape
    nb, _, Kw, ns = w.shape
    assert Kw == K
    tm = _tile(M, (512, 256))
    tn = _col_tile(ns)
    tk = _tile(K, tuple(t for t in (2048, 1408, 1024, 512, 256) if t * tn * 2 <= NN_WEIGHT_TILE_BYTES))
    nj, nk = ns // tn, K // tk

    def body(*refs):
        a_ref, w_ref = refs[:2]
        r_ref = None if res is None else refs[2]
        o_ref = refs[2 if res is None else 3]
        part = jnp.dot(a_ref[...].astype(BF16), w_ref[...], preferred_element_type=F32)
        if nk == 1:
            if r_ref is not None:
                part = part + r_ref[...]
            o_ref[...] = part.astype(out_dtype)
            return
        acc_ref = refs[-1]
        k = pl.program_id(3)

        @pl.when(k == 0)
        def _():
            acc_ref[...] = part

        @pl.when(k > 0)
        def _():
            acc_ref[...] += part

        @pl.when(k == nk - 1)
        def _():
            r = acc_ref[...]
            if r_ref is not None:
                r = r + r_ref[...]
            o_ref[...] = r.astype(out_dtype)

    in_specs = [pl.BlockSpec((tm, tk), lambda i, b, j, k: (i, k)),
                pl.BlockSpec((None, None, tk, tn), lambda i, b, j, k: (b, layer, k, j))]
    args = [a, w]
    if res is not None:
        in_specs.append(pl.BlockSpec((tm, tn), lambda i, b, j, k: (i, b * nj + j)))
        args.append(res)
    return pl.pallas_call(
        body, name=name, grid=(M // tm, nb, nj, nk), in_specs=in_specs,
        out_specs=pl.BlockSpec((tm, tn), lambda i, b, j, k: (i, b * nj + j)),
        out_shape=jax.ShapeDtypeStruct((M, nb * ns), out_dtype),
        scratch_shapes=[] if nk == 1 else [pltpu.VMEM((tm, tn), F32)],
        compiler_params=_params(("parallel", "parallel", "parallel", "arbitrary")),
    )(*args)


def _mm_nt(a, w, layer, *, out_dtype, name, dep=None):
    M, N = a.shape
    nb, _, K, ns = w.shape
    assert N == nb * ns
    tm = _tile(M, (1024, 512, 256))
    tn = _col_tile(ns)
    tk = _tile(K, tuple(t for t in (1024, 512, 256) if t * tn * 2 <= NT_WEIGHT_TILE_BYTES))
    nj = ns // tn
    nred = nb * nj

    def body(*refs):
        a_ref, w_ref = refs[:2]
        o_ref, acc_ref = refs[-2:]
        s = pl.program_id(2) * nj + pl.program_id(3)
        part = lax.dot_general(a_ref[...].astype(BF16), w_ref[...], (((1,), (1,)), ((), ())),
                               preferred_element_type=F32)

        @pl.when(s == 0)
        def _():
            acc_ref[...] = part

        @pl.when(s > 0)
        def _():
            acc_ref[...] += part

        @pl.when(s == nred - 1)
        def _():
            o_ref[...] = acc_ref[...].astype(out_dtype)

    in_specs = [pl.BlockSpec((tm, tn), lambda i, kk, b, j: (i, b * nj + j)),
                pl.BlockSpec((None, None, tk, tn), lambda i, kk, b, j: (b, layer, kk, j))]
    args = [a, w]
    if dep is not None:
        in_specs.append(ANY_SPEC)
        args.append(dep)
    return pl.pallas_call(
        body, name=name, grid=(M // tm, K // tk, nb, nj), in_specs=in_specs,
        out_specs=pl.BlockSpec((tm, tk), lambda i, kk, b, j: (i, kk)),
        out_shape=jax.ShapeDtypeStruct((M, K), out_dtype),
        scratch_shapes=[pltpu.VMEM((tm, tk), F32)],
        compiler_params=_params(("parallel", "parallel", "arbitrary", "arbitrary")),
    )(*args)


def _mm_tn(a, g, nb, *, name):
    M, Ka = a.shape
    N = g.shape[1]
    ns = N // nb
    tn = _col_tile(ns)
    tka = _tile(Ka, tuple(t for t in (1024, 512, 256) if t * tn * 4 <= TN_ACC_BYTES))
    tm = _tile(M, (2048, 1024, 512, 256))
    nj, nm = ns // tn, M // tm

    def body(a_ref, g_ref, o_ref, acc_ref):
        m = pl.program_id(3)
        part = lax.dot_general(a_ref[...].astype(BF16), g_ref[...].astype(BF16), (((0,), (0,)), ((), ())),
                               preferred_element_type=F32)

        @pl.when(m == 0)
        def _():
            acc_ref[...] = part

        @pl.when(m > 0)
        def _():
            acc_ref[...] += part

        @pl.when(m == nm - 1)
        def _():
            o_ref[...] = acc_ref[...].astype(BF16)

    return pl.pallas_call(
        body, name=name, grid=(Ka // tka, nb, nj, nm),
        in_specs=[pl.BlockSpec((tm, tka), lambda ka, b, j, m: (m, ka)),
                  pl.BlockSpec((tm, tn), lambda ka, b, j, m: (m, b * nj + j))],
        out_specs=pl.BlockSpec((None, tka, tn), lambda ka, b, j, m: (b, ka, j)),
        out_shape=jax.ShapeDtypeStruct((nb, Ka, ns), BF16),
        scratch_shapes=[pltpu.VMEM((tka, tn), F32)],
        compiler_params=_params(("parallel", "parallel", "parallel", "arbitrary")),
    )(a, g)


def _exact_mm(a, e, *, name):
    R, K = a.shape
    N = e.shape[1]

    def body(a_ref, e_ref, o_ref):
        x = a_ref[...]
        hi = x.astype(BF16)
        r1 = x - hi.astype(F32)
        mid = r1.astype(BF16)
        lo = (r1 - mid.astype(F32)).astype(BF16)
        ev = e_ref[...]
        o_ref[...] = (jnp.dot(hi, ev, preferred_element_type=F32) + jnp.dot(mid, ev, preferred_element_type=F32)
                      + jnp.dot(lo, ev, preferred_element_type=F32))

    return pl.pallas_call(body, name=name, out_shape=jax.ShapeDtypeStruct((R, N), F32),
                          compiler_params=pltpu.CompilerParams(vmem_limit_bytes=VMEM_LIMIT))(a, e)


def _rms_fwd(x, g, dep, *, name):
    L, D = x.shape
    tl = _tile(L, (256, 128))

    def body(x_ref, g_ref, _dep, h_ref):
        xv = x_ref[...]
        rstd = lax.rsqrt(jnp.mean(xv * xv, axis=-1, keepdims=True) + NORM_EPS)
        h_ref[...] = (xv * rstd * g_ref[...]).astype(BF16)

    return pl.pallas_call(
        body, name=name, grid=(L // tl,),
        in_specs=[pl.BlockSpec((tl, D), lambda t: (t, 0)), pl.BlockSpec((1, D), lambda t: (0, 0)), ANY_SPEC],
        out_specs=pl.BlockSpec((tl, D), lambda t: (t, 0)),
        out_shape=jax.ShapeDtypeStruct((L, D), BF16),
        compiler_params=_params(("parallel",)),
    )(x, g, dep)


def _rms_bwd(x, g, dy, dres, *, name):
    L, D = x.shape
    tl = _tile(L, (128,))

    def body(x_ref, g_ref, dy_ref, dres_ref, dx_ref, dxb_ref, dg_ref):
        t = pl.program_id(0)
        xv = x_ref[...]
        rstd = lax.rsqrt(jnp.mean(xv * xv, axis=-1, keepdims=True) + NORM_EPS)
        xhat = xv * rstd
        dyv = dy_ref[...]
        dxhat = dyv * g_ref[...]
        c = jnp.mean(dxhat * xhat, axis=-1, keepdims=True)
        dx = dres_ref[...] + rstd * (dxhat - xhat * c)
        dx_ref[...] = dx
        dxb_ref[...] = dx.astype(BF16)
        dgp = jnp.sum(dyv * xhat, axis=0, keepdims=True)

        @pl.when(t == 0)
        def _():
            dg_ref[...] = dgp

        @pl.when(t > 0)
        def _():
            dg_ref[...] += dgp

    row = pl.BlockSpec((tl, D), lambda t: (t, 0))
    vec = pl.BlockSpec((1, D), lambda t: (0, 0))
    return pl.pallas_call(
        body, name=name, grid=(L // tl,), in_specs=[row, vec, row, row], out_specs=[row, row, vec],
        out_shape=[jax.ShapeDtypeStruct((L, D), F32), jax.ShapeDtypeStruct((L, D), BF16),
                   jax.ShapeDtypeStruct((1, D), F32)],
        compiler_params=_params(("arbitrary",)),
    )(x, g, dy, dres)


def _gate_fwd(proj, ta, tb, tc, *, name):
    L, D = ta.shape
    tl, tcw = _tile(L, (256, 128)), _tile(D, (1024, 512, 256, 128))
    off = N_QKV_CHUNKS * HEAD_DIM // tcw
    nd = D // tcw

    def body(g0, g1, g2, a_ref, b_ref, c_ref, o_ref):
        m = (jax.nn.sigmoid(g0[...]) * a_ref[...] + jax.nn.sigmoid(g1[...]) * b_ref[...]
             + jax.nn.sigmoid(g2[...]) * c_ref[...])
        o_ref[...] = m.astype(BF16)

    blk = pl.BlockSpec((tl, tcw), lambda t, j: (t, j))
    gl = [pl.BlockSpec((tl, tcw), functools.partial(lambda t, j, i: (t, off + i * nd + j), i=i)) for i in range(3)]
    return pl.pallas_call(
        body, name=name, grid=(L // tl, nd), in_specs=gl + [blk, blk, blk], out_specs=blk,
        out_shape=jax.ShapeDtypeStruct((L, D), BF16),
        compiler_params=_params(("parallel", "parallel")),
    )(proj, proj, proj, ta, tb, tc)


def _gate_bwd(proj, ta, tb, tc, dmerged, *, name):
    L, D = ta.shape
    ncols = proj.shape[1]
    tl, tcw = _tile(L, (256, 128)), _tile(D, (1024, 512, 256, 128))
    off = N_QKV_CHUNKS * HEAD_DIM // tcw
    nd = D // tcw

    def body(g0, g1, g2, a_ref, b_ref, c_ref, dm_ref, da_ref, db_ref, dc_ref, dgl_ref):
        i = pl.program_id(2)
        sg = jax.nn.sigmoid(jnp.where(i == 0, g0[...], jnp.where(i == 1, g1[...], g2[...])))
        sel_t = jnp.where(i == 0, a_ref[...], jnp.where(i == 1, b_ref[...], c_ref[...]))
        dt = dm_ref[...] * sg
        dtb = dt.astype(BF16)

        @pl.when(i == 0)
        def _():
            da_ref[...] = dtb

        @pl.when(i == 1)
        def _():
            db_ref[...] = dtb

        @pl.when(i == 2)
        def _():
            dc_ref[...] = dtb

        dgl_ref[...] = (dt * sel_t * (1.0 - sg)).astype(BF16)

    blk = pl.BlockSpec((tl, tcw), lambda t, j, i: (t, j))
    gl = [pl.BlockSpec((tl, tcw), functools.partial(lambda t, j, i, q: (t, off + q * nd + j), q=q)) for q in range(3)]
    return pl.pallas_call(
        body, name=name, grid=(L // tl, nd, 3), in_specs=gl + [blk, blk, blk, blk],
        out_specs=[blk, blk, blk, pl.BlockSpec((tl, tcw), lambda t, j, i: (t, off + i * nd + j))],
        out_shape=[jax.ShapeDtypeStruct((L, D), BF16)] * 3 + [jax.ShapeDtypeStruct((L, ncols), BF16)],
        compiler_params=_params(("parallel", "parallel", "arbitrary")),
    )(proj, proj, proj, ta, tb, tc, dmerged)


def _swiglu_fwd(gu, *, name):
    L, F2 = gu.shape
    F = F2 // 2
    tl = _tile(L, (128, 64))

    def body(gu_ref, o_ref):
        gt = gu_ref[:, :F]
        o_ref[...] = (gt * jax.nn.sigmoid(gt) * gu_ref[:, F:]).astype(BF16)

    return pl.pallas_call(
        body, name=name, grid=(L // tl,), in_specs=[pl.BlockSpec((tl, F2), lambda t: (t, 0))],
        out_specs=pl.BlockSpec((tl, F), lambda t: (t, 0)),
        out_shape=jax.ShapeDtypeStruct((L, F), BF16),
        compiler_params=_params(("parallel",)),
    )(gu)


def _swiglu_bwd(gu, dact, *, name):
    L, F2 = gu.shape
    F = F2 // 2
    tl = _tile(L, (64,))

    def body(gu_ref, d_ref, o_ref):
        gt, up, d = gu_ref[:, :F], gu_ref[:, F:], d_ref[...]
        sg = jax.nn.sigmoid(gt)
        o_ref[:, :F] = (d * up * sg * (1.0 + gt * (1.0 - sg))).astype(BF16)
        o_ref[:, F:] = (d * gt * sg).astype(BF16)

    return pl.pallas_call(
        body, name=name, grid=(L // tl,),
        in_specs=[pl.BlockSpec((tl, F2), lambda t: (t, 0)), pl.BlockSpec((tl, F), lambda t: (t, 0))],
        out_specs=pl.BlockSpec((tl, F2), lambda t: (t, 0)),
        out_shape=jax.ShapeDtypeStruct((L, F2), BF16),
        compiler_params=_params(("parallel",)),
    )(gu, dact)


def _loss(y, tgt, *, name):
    L, D = y.shape
    tl = _tile(L, (256, 128))
    nt = L // tl

    def body(y_ref, t_ref, dy_ref, dyb_ref, loss_ref, acc_ref):
        t = pl.program_id(0)
        e = y_ref[...] - t_ref[...]
        dy = e * (1.0 / D)
        dy_ref[...] = dy
        dyb_ref[...] = dy.astype(BF16)
        part = jnp.sum(e * e, axis=0, keepdims=True)

        @pl.when(t == 0)
        def _():
            acc_ref[...] = part

        @pl.when(t > 0)
        def _():
            acc_ref[...] += part

        @pl.when(t == nt - 1)
        def _():
            loss_ref[...] = jnp.broadcast_to(jnp.sum(acc_ref[...], axis=-1, keepdims=True) * (0.5 / D), (1, 128))

    row = pl.BlockSpec((tl, D), lambda t: (t, 0))
    return pl.pallas_call(
        body, name=name, grid=(nt,), in_specs=[row, row],
        out_specs=[row, row, pl.BlockSpec((1, 128), lambda t: (0, 0))],
        out_shape=[jax.ShapeDtypeStruct((L, D), F32), jax.ShapeDtypeStruct((L, D), BF16),
                   jax.ShapeDtypeStruct((1, 128), F32)],
        scratch_shapes=[pltpu.VMEM((1, D), F32)],
        compiler_params=_params(("arbitrary",)),
    )(y, tgt)


def _rope(v, cos, sin_signed):
    return v * cos + pltpu.roll(v, HEAD_DIM // 2, 1) * sin_signed


def _qk_fwd(proj, gtab, cos, sin, pieces, nchunks, rope_upto, cg, *, name, dep=None):
    L = proj.shape[0]
    tl = _tile(L, (512, 256, 128))
    W = cg * HEAD_DIM
    pmap = _piece_map(tuple((a // cg, n // cg, p // cg) for a, n, p in pieces))

    def body(*refs):
        p_ref, g_ref, cos_ref, sin_ref = refs[:4]
        o_ref = refs[-1]
        c = pl.program_id(1)

        def norm(j):
            cols = slice(j * HEAD_DIM, (j + 1) * HEAD_DIM)
            x = p_ref[:, cols]
            rstd = lax.rsqrt(jnp.mean(x * x, axis=-1, keepdims=True) + NORM_EPS)
            return cols, x * rstd * g_ref[:, cols]

        @pl.when(c < rope_upto // cg)
        def _():
            for j in range(cg):
                cols, y = norm(j)
                o_ref[:, cols] = _rope(y, cos_ref[...], sin_ref[...])

        @pl.when(c >= rope_upto // cg)
        def _():
            for j in range(cg):
                cols, y = norm(j)
                o_ref[:, cols] = y

    pos = pl.BlockSpec((tl, HEAD_DIM), lambda t, c: (t, 0))
    in_specs = [pl.BlockSpec((tl, W), lambda t, c: (t, pmap(c))),
                pl.BlockSpec((None, 1, W), lambda t, c: (c, 0, 0)), pos, pos]
    args = [proj, gtab, cos, sin]
    if dep is not None:
        in_specs.append(ANY_SPEC)
        args.append(dep)
    return pl.pallas_call(
        body, name=name, grid=(L // tl, nchunks // cg), in_specs=in_specs,
        out_specs=pl.BlockSpec((tl, W), lambda t, c: (t, c)),
        out_shape=jax.ShapeDtypeStruct((L, nchunks * HEAD_DIM), F32),
        compiler_params=_params(("parallel", "parallel")),
    )(*args)


def _qk_bwd(dqk, proj, gtab, cos, sin, dproj, pieces, nchunks, rope_upto, cg, *, name):
    L = proj.shape[0]
    tl = _tile(L, (512, 256, 128))
    W = cg * HEAD_DIM
    pmap = _piece_map(tuple((a // cg, n // cg, p // cg) for a, n, p in pieces))

    def body(d_ref, p_ref, g_ref, cos_ref, sin_ref, _, o_ref, dg_ref):
        c, t = pl.program_id(0), pl.program_id(1)

        @pl.when(t == 0)
        def _():
            dg_ref[...] = jnp.zeros_like(dg_ref)

        for j in range(cg):
            cols = slice(j * HEAD_DIM, (j + 1) * HEAD_DIM)
            x = p_ref[:, cols]
            rstd = lax.rsqrt(jnp.mean(x * x, axis=-1, keepdims=True) + NORM_EPS)
            xhat = x * rstd
            dy = d_ref[:, cols]
            dy = jnp.where(c < rope_upto // cg, _rope(dy, cos_ref[...], -sin_ref[...]), dy)
            dxhat = dy * g_ref[:, cols]
            cm = jnp.mean(dxhat * xhat, axis=-1, keepdims=True)
            o_ref[:, cols] = (rstd * (dxhat - xhat * cm)).astype(BF16)
            dg_ref[:, cols] += jnp.sum(dy * xhat, axis=0, keepdims=True)

    pos = pl.BlockSpec((tl, HEAD_DIM), lambda c, t: (t, 0))
    gspec = pl.BlockSpec((None, 1, W), lambda c, t: (c, 0, 0))
    out, dg = pl.pallas_call(
        body, name=name, grid=(nchunks // cg, L // tl),
        in_specs=[pl.BlockSpec((tl, W), lambda c, t: (t, c)),
                  pl.BlockSpec((tl, W), lambda c, t: (t, pmap(c))), gspec, pos, pos,
                  pl.BlockSpec(memory_space=pl.ANY)],
        out_specs=[pl.BlockSpec((tl, W), lambda c, t: (t, pmap(c))), gspec],
        out_shape=[jax.ShapeDtypeStruct(dproj.shape, BF16), jax.ShapeDtypeStruct((nchunks // cg, 1, W), F32)],
        input_output_aliases={5: 0},
        compiler_params=_params(("parallel", "arbitrary")),
    )(dqk, proj, gtab, cos, sin, dproj)
    return out, dg


def _v_bwd(dv, dproj, *, name):
    L = dv.shape[0]
    tl = _tile(L, (512, 256, 128))
    pmap = _piece_map(tuple((a // 2, n // 2, p // 2) for a, n, p in V_PIECES))

    def body(d_ref, _, o_ref):
        o_ref[...] = d_ref[...].astype(BF16)

    return pl.pallas_call(
        body, name=name, grid=(L // tl, NK_CHUNKS // 2),
        in_specs=[pl.BlockSpec((tl, 2 * HEAD_DIM), lambda t, c: (t, c)), pl.BlockSpec(memory_space=pl.ANY)],
        out_specs=pl.BlockSpec((tl, 2 * HEAD_DIM), lambda t, c: (t, pmap(c))),
        out_shape=jax.ShapeDtypeStruct(dproj.shape, BF16),
        input_output_aliases={1: 0},
        compiler_params=_params(("parallel", "parallel")),
    )(dv, dproj)


def _band_geometry(L, dil, radius):
    n = L // dil
    bq = min(256, max(n // 2, 64), n)
    width = min(bq + 2 * radius, n)
    return n, bq, width


def _band_rows(dil, r, first, count):
    if dil == 1:
        return pl.ds(pl.multiple_of(first, 8), count)
    return pl.ds(r + first * dil, count, stride=dil)


def _band_mask(i, bq, width, radius, ws):
    qpos = i * bq + lax.broadcasted_iota(jnp.int32, (bq, width), 0)
    kpos = ws + lax.broadcasted_iota(jnp.int32, (bq, width), 1)
    return jnp.abs(kpos - qpos) <= radius


def _band_fwd(qn, kn, proj, *, dil, radius, nkv, group, q0, k0, v0, sink=None, name):
    L = qn.shape[0]
    n, bq, width = _band_geometry(L, dil, radius)
    tq = bq * dil
    nh = nkv * group

    def body(*refs):
        if sink is None:
            q_ref, k_ref, v_ref, o_ref, lse_ref = refs
        else:
            q_ref, k_ref, v_ref, s_ref, o_ref, lse_ref = refs
        i = pl.program_id(2)
        ws = jnp.clip(i * bq - radius, 0, n - width)
        valid = _band_mask(i, bq, width, radius, ws)

        def one(r, carry):
            qrows = _band_rows(dil, r, 0, bq)
            krows = _band_rows(dil, r, ws, width)
            q = q_ref[qrows, :].astype(BF16)
            k = k_ref[krows, :].astype(BF16)
            v = v_ref[krows, :].astype(BF16)
            s = lax.dot_general(q, k, (((1,), (1,)), ((), ())), preferred_element_type=F32) * ATT_SCALE
            s = jnp.where(valid, s, NEG)
            m = jnp.max(s, axis=-1, keepdims=True)
            if sink is not None:
                m = jnp.maximum(m, s_ref[...][:, :1])
            p = jnp.exp(s - m)
            denom = jnp.sum(p, axis=-1, keepdims=True)
            if sink is not None:
                denom = denom + jnp.exp(s_ref[...][:, :1] - m)
            pn = (p / denom).astype(BF16)
            o_ref[qrows, :] = jnp.dot(pn, v, preferred_element_type=F32)
            lse_ref[qrows, :] = jnp.broadcast_to(m + jnp.log(denom), (bq, HEAD_DIM))
            return carry

        if dil == 1:
            one(0, 0)
        else:
            lax.fori_loop(0, dil, one, 0)

    qspec = pl.BlockSpec((tq, HEAD_DIM), lambda hk, g, i: (i, q0 + hk * group + g))
    in_specs = [qspec,
                pl.BlockSpec((L, HEAD_DIM), lambda hk, g, i: (0, k0 + hk)),
                pl.BlockSpec((L, HEAD_DIM), lambda hk, g, i: (0, v0 + hk))]
    args = [qn, kn, proj]
    if sink is not None:
        in_specs.append(pl.BlockSpec((None, 1, HEAD_DIM), lambda hk, g, i: (hk * group + g, 0, 0)))
        args.append(sink)
    ospec = pl.BlockSpec((tq, HEAD_DIM), lambda hk, g, i: (i, hk * group + g))
    return pl.pallas_call(
        body, name=name, grid=(nkv, group, n // bq), in_specs=in_specs, out_specs=[ospec, ospec],
        out_shape=[jax.ShapeDtypeStruct((L, nh * HEAD_DIM), F32)] * 2,
        compiler_params=_params(("parallel", "parallel", "arbitrary")),
    )(*args)


def _band_bwd(qn, kn, proj, do, o, lse, dq_buf, dk_buf, dv_buf, *, dil, radius, nkv, group, q0, k0, v0, o0,
              sink=None, name):
    L = qn.shape[0]
    n, bq, width = _band_geometry(L, dil, radius)
    tq = bq * dil
    nh = nkv * group
    n_in = 6 + (1 if sink is not None else 0)

    def body(*refs):
        q_ref, k_ref, v_ref, do_ref, o_ref, lse_ref = refs[:6]
        s_ref = refs[6] if sink is not None else None
        outs = refs[n_in + 3:]
        dq_ref, dk_ref, dv_ref = outs[:3]
        ds_ref = outs[3] if sink is not None else None
        g, i = pl.program_id(1), pl.program_id(2)
        ws = jnp.clip(i * bq - radius, 0, n - width)
        valid = _band_mask(i, bq, width, radius, ws)

        @pl.when((g == 0) & (i == 0))
        def _():
            dk_ref[...] = jnp.zeros_like(dk_ref)
            dv_ref[...] = jnp.zeros_like(dv_ref)

        if sink is not None:
            @pl.when(i == 0)
            def _():
                ds_ref[...] = jnp.zeros_like(ds_ref)

        def one(r, carry):
            qrows = _band_rows(dil, r, 0, bq)
            krows = _band_rows(dil, r, ws, width)
            q = q_ref[qrows, :].astype(BF16)
            k = k_ref[krows, :].astype(BF16)
            v = v_ref[krows, :].astype(BF16)
            dov = do_ref[qrows, :]
            lse_v = lse_ref[qrows, :][:, :1]
            delta = jnp.sum(dov * o_ref[qrows, :], axis=-1, keepdims=True)
            dob = dov.astype(BF16)
            s = lax.dot_general(q, k, (((1,), (1,)), ((), ())), preferred_element_type=F32) * ATT_SCALE
            p = jnp.where(valid, jnp.exp(s - lse_v), 0.0)
            dp = lax.dot_general(dob, v, (((1,), (1,)), ((), ())), preferred_element_type=F32)
            dsb = (p * (dp - delta)).astype(BF16)
            dq_ref[qrows, :] = jnp.dot(dsb, k, preferred_element_type=F32) * ATT_SCALE
            dk_ref[krows, :] += lax.dot_general(dsb, q, (((0,), (0,)), ((), ())),
                                                preferred_element_type=F32) * ATT_SCALE
            dv_ref[krows, :] += lax.dot_general(p.astype(BF16), dob, (((0,), (0,)), ((), ())),
                                                preferred_element_type=F32)
            if sink is not None:
                ps = jnp.exp(s_ref[...][:, :1] - lse_v)
                ds_ref[...] += jnp.broadcast_to(jnp.sum(-ps * delta, axis=0, keepdims=True), (1, HEAD_DIM))
            return carry

        if dil == 1:
            one(0, 0)
        else:
            lax.fori_loop(0, dil, one, 0)

    hspec = pl.BlockSpec((tq, HEAD_DIM), lambda hk, g, i: (i, o0 + hk * group + g))
    qspec = pl.BlockSpec((tq, HEAD_DIM), lambda hk, g, i: (i, q0 + hk * group + g))
    kspec = pl.BlockSpec((L, HEAD_DIM), lambda hk, g, i: (0, k0 + hk))
    any_spec = pl.BlockSpec(memory_space=pl.ANY)
    in_specs = [qspec, kspec, pl.BlockSpec((L, HEAD_DIM), lambda hk, g, i: (0, v0 + hk)), hspec, hspec, hspec]
    args = [qn, kn, proj, do, o, lse]
    if sink is not None:
        in_specs.append(pl.BlockSpec((None, 1, HEAD_DIM), lambda hk, g, i: (hk * group + g, 0, 0)))
        args.append(sink)
    in_specs += [any_spec] * 3
    args += [dq_buf, dk_buf, dv_buf]
    out_specs = [qspec, kspec, kspec]
    out_shape = [jax.ShapeDtypeStruct(dq_buf.shape, F32), jax.ShapeDtypeStruct(dk_buf.shape, F32),
                 jax.ShapeDtypeStruct(dv_buf.shape, F32)]
    if sink is not None:
        out_specs.append(pl.BlockSpec((None, 1, HEAD_DIM), lambda hk, g, i: (hk * group + g, 0, 0)))
        out_shape.append(jax.ShapeDtypeStruct((nh, 1, HEAD_DIM), F32))
    return pl.pallas_call(
        body, name=name, grid=(nkv, group, n // bq), in_specs=in_specs, out_specs=out_specs, out_shape=out_shape,
        input_output_aliases={n_in: 0, n_in + 1: 1, n_in + 2: 2},
        compiler_params=_params(("parallel", "arbitrary", "arbitrary")),
    )(*args)


def _combine_b(os_, lses, *, name):
    L, W = os_[0].shape
    tl = _tile(L, (256, 128))

    def body(o0, o1, o2, l0, l1, l2, out_ref, lt_ref):
        a, b, c = l0[...], l1[...], l2[...]
        m = jnp.maximum(jnp.maximum(a, b), c)
        ea, eb, ec = jnp.exp(a - m), jnp.exp(b - m), jnp.exp(c - m)
        tot = ea + eb + ec
        out_ref[...] = (ea * o0[...] + eb * o1[...] + ec * o2[...]) / tot
        lt_ref[...] = m + jnp.log(tot)

    blk = pl.BlockSpec((tl, W), lambda t: (t, 0))
    return pl.pallas_call(
        body, name=name, grid=(L // tl,), in_specs=[blk] * 6, out_specs=[blk, blk],
        out_shape=[jax.ShapeDtypeStruct((L, W), F32)] * 2, compiler_params=_params(("parallel",)),
    )(*os_, *lses)


C_QROWS = 4
C_KROWS = C_QROWS + C_WIN_ROWS
C_QUERIES, C_KEYS = C_QROWS * GRID_W, C_KROWS * GRID_W
_C_KIND_OFFSETS = (C_WIN_ROWS - 1, C_WIN_ROWS - 1 - C_WIN_ROWS // 2, C_WIN_ROWS - 1 - (C_KROWS - C_QROWS))


def _c_geometry(L):
    rows = L // GRID_W
    assert rows >= C_KROWS and rows % C_QROWS == 0
    return rows


def _c_bias_tiles(bias_t):
    cq = np.arange(GRID_W)[:, None]
    ck = np.arange(GRID_W)[None, :]
    start = np.clip(cq - C_WIN_COLS // 2, 0, GRID_W - C_WIN_COLS)
    masked = jnp.where(jnp.asarray((ck >= start) & (ck < start + C_WIN_COLS)), bias_t, NEG)
    blank = jnp.full((C_HEADS, GRID_W, GRID_W), NEG, F32)
    kinds = []
    for kind in range(3):
        off = _C_KIND_OFFSETS[kind]
        row_blocks = []
        for a in range(C_QROWS):
            lo = (0, a, C_KROWS - C_WIN_ROWS)[kind]
            row_blocks.append(jnp.concatenate(
                [masked[:, b - a + off] if lo <= b < lo + C_WIN_ROWS else blank for b in range(C_KROWS)], axis=-1))
        kinds.append(jnp.concatenate(row_blocks, axis=-2))
    return jnp.stack(kinds, axis=1)


def _c_block(g, rows):
    r0 = g * C_QROWS
    k0 = jnp.clip(r0 - C_WIN_ROWS // 2, 0, rows - C_KROWS)
    kind = jnp.where(g == 0, 0, jnp.where(g == rows // C_QROWS - 1, 2, 1))
    return k0, kind, k0 - r0 + (C_WIN_ROWS - 1)


def _c_fwd(qn, kn, proj, tiles, *, name):
    L = qn.shape[0]
    rows = _c_geometry(L)

    def body(q_ref, k_ref, v_ref, t_ref, o_ref, lse_ref):
        k0, _, _ = _c_block(pl.program_id(1), rows)
        krows = pl.ds(pl.multiple_of(k0 * GRID_W, GRID_W), C_KEYS)
        q = q_ref[...].astype(BF16)
        k = k_ref[krows, :].astype(BF16)
        v = v_ref[krows, :].astype(BF16)
        s = lax.dot_general(q, k, (((1,), (1,)), ((), ())), preferred_element_type=F32) * ATT_SCALE + t_ref[...]
        m = jnp.max(s, axis=-1, keepdims=True)
        p = jnp.exp(s - m)
        denom = jnp.sum(p, axis=-1, keepdims=True)
        o_ref[...] = jnp.dot((p / denom).astype(BF16), v, preferred_element_type=F32)
        lse_ref[...] = jnp.broadcast_to(m + jnp.log(denom), (C_QUERIES, HEAD_DIM))

    def tile_index(h, g):
        return (h, _c_block(g, rows)[1], 0, 0)

    ospec = pl.BlockSpec((C_QUERIES, HEAD_DIM), lambda h, g: (g, h))
    return pl.pallas_call(
        body, name=name, grid=(C_HEADS, rows // C_QROWS),
        in_specs=[pl.BlockSpec((C_QUERIES, HEAD_DIM), lambda h, g: (g, 20 + h)),
                  pl.BlockSpec((L, HEAD_DIM), lambda h, g: (0, 14 + h)),
                  pl.BlockSpec((L, HEAD_DIM), lambda h, g: (0, PC_VC + h)),
                  pl.BlockSpec((None, None, C_QUERIES, C_KEYS), tile_index)],
        out_specs=[ospec, ospec],
        out_shape=[jax.ShapeDtypeStruct((L, C_HEADS * HEAD_DIM), F32)] * 2,
        compiler_params=_params(("parallel", "arbitrary")),
    )(qn, kn, proj, tiles)


def _c_bwd(qn, kn, proj, tiles, do, o, lse, dq_buf, dk_buf, dv_buf, *, name):
    L = qn.shape[0]
    rows = _c_geometry(L)

    def body(q_ref, k_ref, v_ref, t_ref, do_ref, o_ref, lse_ref, _a, _b, _c, dq_ref, dk_ref, dv_ref, dt_ref):
        g = pl.program_id(1)
        k0, _, off = _c_block(g, rows)
        krows = pl.ds(pl.multiple_of(k0 * GRID_W, GRID_W), C_KEYS)

        @pl.when(g == 0)
        def _():
            dk_ref[...] = jnp.zeros_like(dk_ref)
            dv_ref[...] = jnp.zeros_like(dv_ref)
            dt_ref[...] = jnp.zeros_like(dt_ref)

        q = q_ref[...].astype(BF16)
        k = k_ref[krows, :].astype(BF16)
        v = v_ref[krows, :].astype(BF16)
        dov = do_ref[...]
        dob = dov.astype(BF16)
        delta = jnp.sum(dov * o_ref[...], axis=-1, keepdims=True)
        s = lax.dot_general(q, k, (((1,), (1,)), ((), ())), preferred_element_type=F32) * ATT_SCALE + t_ref[...]
        p = jnp.exp(s - lse_ref[...][:, :1])
        dp = lax.dot_general(dob, v, (((1,), (1,)), ((), ())), preferred_element_type=F32)
        ds = p * (dp - delta)
        for a in range(C_QROWS):
            for b in range(C_KROWS):
                rel = jnp.clip(b - a + off, 0, C_NREL - 1)
                dt_ref[rel] += ds[a * GRID_W:(a + 1) * GRID_W, b * GRID_W:(b + 1) * GRID_W]
        dsb = ds.astype(BF16)
        dq_ref[...] = jnp.dot(dsb, k, preferred_element_type=F32) * ATT_SCALE
        dk_ref[krows, :] += lax.dot_general(dsb, q, (((0,), (0,)), ((), ())), preferred_element_type=F32) * ATT_SCALE
        dv_ref[krows, :] += lax.dot_general(p.astype(BF16), dob, (((0,), (0,)), ((), ())),
                                            preferred_element_type=F32)

    def tile_index(h, g):
        return (h, _c_block(g, rows)[1], 0, 0)

    hspec = pl.BlockSpec((C_QUERIES, HEAD_DIM), lambda h, g: (g, h))
    qspec = pl.BlockSpec((C_QUERIES, HEAD_DIM), lambda h, g: (g, 20 + h))
    kspec = pl.BlockSpec((L, HEAD_DIM), lambda h, g: (0, 14 + h))
    any_spec = pl.BlockSpec(memory_space=pl.ANY)
    return pl.pallas_call(
        body, name=name, grid=(C_HEADS, rows // C_QROWS),
        in_specs=[qspec, kspec, pl.BlockSpec((L, HEAD_DIM), lambda h, g: (0, PC_VC + h)),
                  pl.BlockSpec((None, None, C_QUERIES, C_KEYS), tile_index),
                  hspec, hspec, hspec, any_spec, any_spec, any_spec],
        out_specs=[qspec, kspec, kspec,
                   pl.BlockSpec((None, C_NREL, GRID_W, GRID_W), lambda h, r: (h, 0, 0, 0))],
        out_shape=[jax.ShapeDtypeStruct(dq_buf.shape, F32), jax.ShapeDtypeStruct(dk_buf.shape, F32),
                   jax.ShapeDtypeStruct(dv_buf.shape, F32),
                   jax.ShapeDtypeStruct((C_HEADS, C_NREL, GRID_W, GRID_W), F32)],
        input_output_aliases={7: 0, 8: 1, 9: 2},
        compiler_params=_params(("parallel", "arbitrary")),
    )(qn, kn, proj, tiles, do, o, lse, dq_buf, dk_buf, dv_buf)


def _c_expand_matrix():
    cq = np.arange(GRID_W)[:, None]
    ck = np.arange(GRID_W)[None, :]
    d = (ck - cq + (C_WIN_COLS - 1)).reshape(-1)
    e = np.zeros((GRID_W * GRID_W, HEAD_DIM), np.float32)
    okd = (d >= 0) & (d < C_NCOL)
    e[np.arange(GRID_W * GRID_W)[okd], d[okd]] = 1.0
    return e


def _peer(p):
    return (p // 4, (p // 2) % 2, p % 2)


def _my_index():
    return 4 * lax.axis_index("x") + 2 * lax.axis_index("y") + lax.axis_index("c")


HBM_SPEC = pl.BlockSpec(memory_space=pltpu.HBM)
SEM_SPEC = pl.BlockSpec(memory_space=pltpu.SEMAPHORE)
ANY_SPEC = pl.BlockSpec(memory_space=pl.ANY)
DATAFLOW = pltpu.SideEffectType.DATAFLOW_SIDE_EFFECTING


_EXCHANGE_TRANSFERS = {"scatter": N_DEV - 1, "gather1": 4, "gather2": 3}


def _exchange_views(mode, kinds, arrays):
    nw = len(kinds)
    gather = mode != "scatter"
    if gather:
        rows = [a.shape[0] // N_DEV for a in arrays[:nw]]
    else:
        rows = [a.shape[1] // N_DEV for a in arrays[:nw]]

    def gather_slot(ref, w, who):
        return ref.at[who] if kinds[w] == "col" else ref.at[pl.ds(who * rows[w], rows[w]), :]

    x, y, c = lax.axis_index("x"), lax.axis_index("y"), lax.axis_index("c")
    me = 4 * x + 2 * y + c
    chips = [(1 - x, y), (x, 1 - y), (1 - x, 1 - y)]

    def index(px, py, pc):
        return 4 * px + 2 * py + pc

    if mode == "scatter":
        plan = [(_peer((me + off) % N_DEV), (me + off) % N_DEV, (me + N_DEV - off) % N_DEV)
                for off in range(1, N_DEV)]
    elif mode == "gather1":
        plan = [((x, y, 1 - c), me, index(x, y, 1 - c))] + [((px, py, c), me, index(px, py, c)) for px, py in chips]
    else:
        plan = [((x, y, 1 - c), index(px, py, c), index(px, py, 1 - c)) for px, py in chips]

    def src(ref, w, j):
        sent = plan[j][1]
        if gather:
            return gather_slot(ref, w, sent)
        return ref.at[sent] if kinds[w] == "col" else ref.at[0, pl.ds(sent * rows[w], rows[w]), :]

    def dst(ref, w, j):
        return gather_slot(ref, w, plan[j][1]) if gather else ref.at[me]

    def arrival(ref, w, j):
        return gather_slot(ref, w, plan[j][2]) if gather else ref.at[plan[j][2]]

    return [p[0] for p in plan], src, dst, arrival


def _place_cast(w, layer, kind, *, name):
    _, R, C = w.shape
    tr = _tile(R, (256, 128, 64, 32, 16))

    def body(w_ref, o_ref):
        o_ref[...] = w_ref[...].astype(BF16)

    if kind == "col":
        out_shape = jax.ShapeDtypeStruct((N_DEV, R, C), BF16)
        out_spec = pl.BlockSpec((None, tr, C), lambda t: (_my_index(), t, 0))
    else:
        out_shape = jax.ShapeDtypeStruct((N_DEV * R, C), BF16)
        out_spec = pl.BlockSpec((tr, C), lambda t: (_my_index() * (R // tr) + t, 0))
    return pl.pallas_call(
        body, name=name, grid=(R // tr,), in_specs=[pl.BlockSpec((None, tr, C), lambda t: (layer, t, 0))],
        out_specs=out_spec, out_shape=out_shape, compiler_params=_params(("parallel",)),
    )(w)


def _exchange_start(mode, srcs, lands, kinds, after, *, name):
    nw = len(lands)
    ns = len(srcs)
    arrays = list(srcs) + list(lands)
    na = len(arrays)
    nx = _EXCHANGE_TRANSFERS[mode]

    def body(*refs):
        l_refs = refs[ns:ns + nw]
        s_refs = refs[:ns] if ns else l_refs
        send_sems, recv_sems = refs[ns + nw + 1], refs[ns + nw + 2]
        token = refs[-1]
        peers, src, dst, _ = _exchange_views(mode, kinds, arrays)
        for j in range(nx):
            for w in range(nw):
                pltpu.make_async_remote_copy(src(s_refs[w], w, j), dst(l_refs[w], w, j),
                                             send_sems.at[w * nx + j], recv_sems.at[w * nx + j],
                                             device_id=peers[j], device_id_type=MESH).start()
        token[...] = jnp.zeros_like(token)

    outs = pl.pallas_call(
        body, name=name,
        out_shape=(pltpu.SemaphoreType.DMA((nw * nx,)), pltpu.SemaphoreType.DMA((nw * nx,)),
                   *[pltpu.HBM(a.shape, a.dtype) for a in arrays], jax.ShapeDtypeStruct((8, 128), F32)),
        in_specs=[HBM_SPEC] * na + [ANY_SPEC],
        out_specs=(SEM_SPEC, SEM_SPEC, *([HBM_SPEC] * na), pl.BlockSpec(memory_space=pltpu.VMEM)),
        input_output_aliases={k: 2 + k for k in range(na)},
        compiler_params=pltpu.CompilerParams(has_side_effects=DATAFLOW),
    )(*[pltpu.with_memory_space_constraint(a, pltpu.HBM) for a in arrays], after)
    return outs[0], outs[1], outs[2:2 + ns], outs[2 + ns:2 + na], outs[-1]


def _exchange_wait(mode, started, kinds, after, *, name):
    send_sems, recv_sems, srcs, lands, _ = started
    nw = len(lands)
    ns = len(srcs)
    arrays = list(srcs) + list(lands)
    na = len(arrays)
    nx = _EXCHANGE_TRANSFERS[mode]

    def body(*refs):
        l_refs = refs[ns:na]
        s_refs = refs[:ns] if ns else l_refs
        send_ref, recv_ref = refs[na], refs[na + 1]
        peers, src, _, arrival = _exchange_views(mode, kinds, arrays)
        for j in range(nx):
            for w in range(nw):
                cp = pltpu.make_async_remote_copy(src(s_refs[w], w, j), arrival(l_refs[w], w, j),
                                                  send_ref.at[w * nx + j], recv_ref.at[w * nx + j],
                                                  device_id=peers[j], device_id_type=MESH)
                cp.wait_send()
                cp.wait_recv()

    outs = pl.pallas_call(
        body, name=name, out_shape=[pltpu.HBM(a.shape, a.dtype) for a in arrays],
        in_specs=[HBM_SPEC] * na + [SEM_SPEC, SEM_SPEC, ANY_SPEC], out_specs=[HBM_SPEC] * na,
        input_output_aliases={k: k for k in range(na)},
        compiler_params=pltpu.CompilerParams(has_side_effects=DATAFLOW),
    )(*arrays, send_sems, recv_sems, after)
    return outs[:ns], outs[ns:]


def _all_reduce_small(x):
    R = x.shape[0]

    def body(x_ref, o_ref, gath, send_sems, recv_sems):
        me = _my_index()
        gath[me] = x_ref[...]
        sends = []
        for off in range(1, N_DEV):
            to = (me + off) % N_DEV
            cp = pltpu.make_async_remote_copy(x_ref, gath.at[me], send_sems.at[off], recv_sems.at[off],
                                              device_id=_peer(to), device_id_type=MESH)
            cp.start()
            sends.append(cp)
        for off in range(1, N_DEV):
            frm = (me + N_DEV - off) % N_DEV
            pltpu.make_async_remote_copy(x_ref, gath.at[frm], send_sems.at[off], recv_sems.at[off],
                                         device_id=_peer(frm), device_id_type=MESH).wait_recv()
        for cp in sends:
            cp.wait_send()
        acc = gath[0]
        for s in range(1, N_DEV):
            acc = acc + gath[s]
        o_ref[...] = acc

    vm = pl.BlockSpec(memory_space=pltpu.VMEM)
    return pl.pallas_call(
        body, name="all_reduce_small", in_specs=[vm], out_specs=vm, out_shape=jax.ShapeDtypeStruct((R, 128), F32),
        scratch_shapes=[pltpu.VMEM((N_DEV, R, 128), F32), pltpu.SemaphoreType.DMA((N_DEV,)),
                        pltpu.SemaphoreType.DMA((N_DEV,))],
        compiler_params=pltpu.CompilerParams(has_side_effects=True),
    )(x)


def _adamw_math(w, g, m, v):
    m = ADAM_B1 * m + (1.0 - ADAM_B1) * g
    v = ADAM_B2 * v + (1.0 - ADAM_B2) * (g * g)
    m_hat = m / (1.0 - ADAM_B1 ** ADAM_STEP)
    v_hat = v / (1.0 - ADAM_B2 ** ADAM_STEP)
    delta = -ADAM_LR * (m_hat / (jnp.sqrt(v_hat) + ADAM_EPS) + ADAM_WD * w)
    return delta, m, v


def _adamw_layer(recv, own, kind, w, m, v, outs, layer, dep, *, name):
    nl, R, C = w.shape
    tr = _tile(R, (128, 64, 32, 16))

    def body(r_ref, o_ref, w_ref, m_ref, v_ref, _0, _1, _2, _3, _dep, g_out, d_out, m_out, v_out, token):
        token[...] = jnp.zeros_like(token)
        me = _my_index()
        mine = o_ref[...].astype(F32)
        g = jnp.where(me == 0, mine, r_ref[0].astype(F32))
        for s in range(1, N_DEV):
            g = g + jnp.where(me == s, mine, r_ref[s].astype(F32))
        delta, mn, vn = _adamw_math(w_ref[...], g, m_ref[...], v_ref[...])
        g_out[...] = g
        d_out[...] = delta
        m_out[...] = mn
        v_out[...] = vn

    if kind == "col":
        own_spec = pl.BlockSpec((None, tr, C), lambda t: (_my_index(), t, 0))
    else:
        own_spec = pl.BlockSpec((None, tr, C), lambda t: (0, _my_index() * (R // tr) + t, 0))
    wspec = pl.BlockSpec((None, tr, C), lambda t: (layer, t, 0))
    res = pl.pallas_call(
        body, name=name, grid=(R // tr,),
        in_specs=[pl.BlockSpec((N_DEV, tr, C), lambda t: (0, t, 0)), own_spec] + [wspec] * 3 + [ANY_SPEC] * 5,
        out_specs=[wspec] * 4 + [pl.BlockSpec((8, 128), lambda t: (0, 0))],
        out_shape=[jax.ShapeDtypeStruct((nl, R, C), F32)] * 4 + [jax.ShapeDtypeStruct((8, 128), F32)],
        input_output_aliases={5: 0, 6: 1, 7: 2, 8: 3},
        compiler_params=_params(("arbitrary",)),
    )(recv, own, w, m, v, *outs, dep)
    return res[:4], res[4]


def _adamw_small(g, w, m, v):
    def body(g_ref, w_ref, m_ref, v_ref, d_out, m_out, v_out):
        delta, mn, vn = _adamw_math(w_ref[...], g_ref[...], m_ref[...], v_ref[...])
        d_out[...] = delta
        m_out[...] = mn
        v_out[...] = vn

    return pl.pallas_call(body, name="adamw_small", out_shape=[jax.ShapeDtypeStruct(g.shape, F32)] * 3)(g, w, m, v)


def _pack(arrays, rows):
    flat = jnp.concatenate([a.reshape(-1) for a in arrays])
    return jnp.pad(flat, (0, rows * 128 - flat.shape[0])).reshape(rows, 128)


def _unpack(packed, shapes):
    flat = packed.reshape(-1)
    out, pos = [], 0
    for s in shapes:
        size = int(np.prod(s))
        out.append(flat[pos:pos + size].reshape(s))
        pos += size
    return out


def kernel(x, norm1_g, w_in, qk_norm_g, sink_a, rpb_c, w_br_a, w_br_b, w_br_c, w_o, norm2_g, w_gate_up, w_down, loss_target, m_norm1_g, m_w_in, m_qk_norm_g, m_sink_a, m_rpb_c, m_w_br_a, m_w_br_b, m_w_br_c, m_w_o, m_norm2_g, m_w_gate_up, m_w_down, v_norm1_g, v_w_in, v_qk_norm_g, v_sink_a, v_rpb_c, v_w_br_a, v_w_br_b, v_w_br_c, v_w_o, v_norm2_g, v_w_gate_up, v_w_down):
    nl = w_in.shape[0]
    L, D = x.shape[1], x.shape[2]
    x0 = x.reshape(L, D)
    tgt = loss_target.reshape(L, D)

    big = [w_in, w_br_a, w_br_b, w_br_c, w_o, w_gate_up, w_down]
    kinds = ["col", "col", "col", "col", "row", "col", "row"]

    big_names = ["w_in", "w_br_a", "w_br_b", "w_br_c", "w_o", "w_gate_up", "w_down"]
    ALL = list(range(len(big)))
    REST = ALL[1:]

    def gather_place(i):
        return [_place_cast(w, i, k, name="gather_place_" + n) for w, k, n in zip(big, kinds, big_names)]

    def gather_start(mode, lands, sub, after, tag):
        return _exchange_start(mode, [], lands, [kinds[j] for j in sub], after, name=mode + "_start" + tag)

    def gather_wait(mode, started, sub, after, tag):
        return _exchange_wait(mode, started, [kinds[j] for j in sub], after, name=mode + "_wait" + tag)[1]

    def matmul_views(lands, sub):
        return [g.reshape((N_DEV, 1) + g.shape[1:]) if kinds[j] == "col" else g.reshape((1, 1) + g.shape)
                for g, j in zip(lands, sub)]

    half = HEAD_DIM // 2
    inv_freq = ROPE_THETA ** (-jnp.arange(half, dtype=F32) * 2.0 / HEAD_DIM)
    ang = jnp.arange(L, dtype=F32)[:, None] * inv_freq[None, :]
    cos = jnp.concatenate([jnp.cos(ang), jnp.cos(ang)], axis=-1)
    sin = jnp.concatenate([-jnp.sin(ang), jnp.sin(ang)], axis=-1)
    expand = jnp.asarray(_c_expand_matrix(), BF16)
    expand_t = jnp.asarray(_c_expand_matrix().T, BF16)

    def gain_tables(i):
        g = qk_norm_g[i]
        gq = jnp.concatenate([jnp.tile(g[0][None], (8, 1)), jnp.tile(g[2][None], (12, 1)), jnp.tile(g[4][None], (8, 1))])
        gk = jnp.concatenate([jnp.tile(g[1][None], (2, 1)), jnp.tile(g[3][None], (12, 1)), jnp.tile(g[5][None], (8, 1))])
        return (gq.reshape(NQ_CHUNKS // Q_CG, 1, Q_CG * HEAD_DIM), gk.reshape(NK_CHUNKS // K_CG, 1, K_CG * HEAD_DIM))

    def bias_table(i):
        rp = jnp.pad(rpb_c[i].reshape(C_HEADS * C_NREL, C_NCOL), ((0, 0), (0, HEAD_DIM - C_NCOL)))
        t = _exact_mm(rp, expand_t, name="c_bias_expand")
        return _c_bias_tiles(t.reshape(C_HEADS, C_NREL, GRID_W, GRID_W))

    def sink_table(i):
        return jnp.broadcast_to(sink_a[i][:, None, None], (A_Q_HEADS, 1, HEAD_DIM))

    saved = []
    gws = [None] * nl
    xi = x0
    lands0 = gather_place(0)
    lvl1 = gather_start("gather1", lands0[:1], [0], x0, "_first")
    lvl2 = gather_start("gather2", gather_wait("gather1", lvl1, [0], x0, "_first"), [0], x0, "_first")
    gws[0] = matmul_views(gather_wait("gather2", lvl2, [0], x0, "_first"), [0])
    rest1 = gather_start("gather1", lands0[1:], REST, gws[0][0], "_rest")
    dep = rest1[4]
    for i in range(nl):
        qk_dep = None
        if i >= 1 and i + 1 < nl:
            nxt1 = gather_start("gather1", gather_place(i + 1), ALL, dep, "")
            dep = nxt1[4]
        gw_in = gws[i][0]
        gq, gk = gain_tables(i)
        bias_t = bias_table(i)
        sink = sink_table(i)
        h1 = _rms_fwd(xi, norm1_g[i][None], dep, name="rms1_fwd")
        proj = _mm_nn(h1, gw_in, 0, out_dtype=F32, name="proj_fwd")
        if i == 0:
            rest2 = gather_start("gather2", gather_wait("gather1", rest1, REST, proj, "_rest"), REST, proj, "_rest")
            qk_dep = rest2[4]
            if nl > 1:
                nxt1 = gather_start("gather1", gather_place(1), ALL, rest2[4], "")
                qk_dep = nxt1[4]
        qn = _qk_fwd(proj, gq, cos, sin, Q_PIECES, NQ_CHUNKS, Q_ROPE_UPTO, Q_CG, name="qnorm_fwd", dep=qk_dep)
        kn = _qk_fwd(proj, gk, cos, sin, K_PIECES, NK_CHUNKS, K_ROPE_UPTO, K_CG, name="knorm_fwd")
        oa, lse_a = _band_fwd(qn, kn, proj, dil=1, radius=A_RADIUS, nkv=A_KV_HEADS, group=A_GROUP,
                              q0=0, k0=0, v0=PC_VA, sink=sink, name="attn_a_fwd")
        obs, lbs = [], []
        for g, (window, dil) in enumerate(B_PATTERNS):
            o_g, l_g = _band_fwd(qn, kn, proj, dil=dil, radius=window // (2 * dil), nkv=B_HG, group=1,
                                 q0=8 + g * B_HG, k0=2 + g * B_HG, v0=PC_VB + g * B_HG, name=f"attn_b{g}_fwd")
            obs.append(o_g)
            lbs.append(l_g)
        ob, lse_b = _combine_b(obs, lbs, name="attn_b_combine")
        oc, lse_c = _c_fwd(qn, kn, proj, bias_t, name="attn_c_fwd")
        if i == 0:
            gws[0] = gws[0] + matmul_views(gather_wait("gather2", rest2, REST, oc, "_rest"), REST)
        _, gw_a, gw_b, gw_c, gw_o, gw_gu, gw_d = gws[i]
        ta = _mm_nn(oa, gw_a, 0, out_dtype=F32, name="br_a_fwd")
        tb = _mm_nn(ob, gw_b, 0, out_dtype=F32, name="br_b_fwd")
        tc = _mm_nn(oc, gw_c, 0, out_dtype=F32, name="br_c_fwd")
        merged = _gate_fwd(proj, ta, tb, tc, name="gate_fwd")
        x1 = _mm_nn(merged, gw_o, 0, out_dtype=F32, name="wo_fwd", res=xi)
        dep = x1
        if i + 1 < nl:
            nxt2 = gather_start("gather2", gather_wait("gather1", nxt1, ALL, x1, ""), ALL, x1, "")
            dep = nxt2[4]
        h2 = _rms_fwd(x1, norm2_g[i][None], dep, name="rms2_fwd")
        gu = _mm_nn(h2, gw_gu, 0, out_dtype=F32, name="gate_up_fwd")
        act = _swiglu_fwd(gu, name="swiglu_fwd")
        x2 = _mm_nn(act, gw_d, 0, out_dtype=F32, name="down_fwd", res=x1)
        saved.append(dict(x=xi, h1=h1, proj=proj, qn=qn, kn=kn, oa=oa, lse_a=lse_a, ob=ob, lse_b=lse_b, oc=oc,
                          lse_c=lse_c, ta=ta, tb=tb, tc=tc, merged=merged, x1=x1, h2=h2, gu=gu, act=act,
                          gq=gq, gk=gk, bias_t=bias_t, sink=sink))
        xi = x2
        dep = x2
        if i + 1 < nl:
            gws[i + 1] = matmul_views(gather_wait("gather2", nxt2, ALL, x2, ""), ALL)

    dx, dxb, loss_row = _loss(xi, tgt, name="loss")

    def scatter_start(grads, sub, after, tag):
        lands = []
        for g, j in zip(grads, sub):
            shape = g.shape if kinds[j] == "col" else (N_DEV, g.shape[1] // N_DEV, g.shape[2])
            lands.append(lax.empty(shape, BF16))
        return _exchange_start("scatter", grads, lands, [kinds[j] for j in sub], after, name="scatter_start" + tag)

    def scatter_wait(pair, after):
        own_a, recv_a = _exchange_wait("scatter", pair[0], [kinds[0]], after, name="scatter_wait_in")
        own_b, recv_b = _exchange_wait("scatter", pair[1], [kinds[j] for j in REST], after, name="scatter_wait_rest")
        return list(recv_a) + list(recv_b), list(own_a) + list(own_b)

    small_grads = [None] * nl
    recv = [None] * nl
    own = [None] * nl
    pending = None
    for i in reversed(range(nl)):
        s = saved[i]
        gw_in, gw_a, gw_b, gw_c, gw_o, gw_gu, gw_d = gws[i]
        dact = _mm_nt(dxb, gw_d, 0, out_dtype=F32, name="down_bwd_x", dep=None if pending is None else pending[0][4])
        g_down = _mm_tn(s["act"], dxb, 1, name="down_bwd_w")
        dgu = _swiglu_bwd(s["gu"], dact, name="swiglu_bwd")
        g_gu = _mm_tn(s["h2"], dgu, N_DEV, name="gate_up_bwd_w")
        dh2 = _mm_nt(dgu, gw_gu, 0, out_dtype=F32, name="gate_up_bwd_x")
        dx1, dx1b, dg2 = _rms_bwd(s["x1"], norm2_g[i][None], dh2, dx, name="rms2_bwd")
        dmerged = _mm_nt(dx1b, gw_o, 0, out_dtype=F32, name="wo_bwd_x")
        g_o = _mm_tn(s["merged"], dx1b, 1, name="wo_bwd_w")
        dta, dtb, dtc, dproj = _gate_bwd(s["proj"], s["ta"], s["tb"], s["tc"], dmerged, name="gate_bwd")
        g_a = _mm_tn(s["oa"], dta, N_DEV, name="br_a_bwd_w")
        g_b = _mm_tn(s["ob"], dtb, N_DEV, name="br_b_bwd_w")
        g_c = _mm_tn(s["oc"], dtc, N_DEV, name="br_c_bwd_w")
        rest = scatter_start([g_a, g_b, g_c, g_o, g_gu, g_down], REST, g_c, "_rest")
        doa = _mm_nt(dta, gw_a, 0, out_dtype=F32, name="br_a_bwd_x", dep=rest[4])
        dob = _mm_nt(dtb, gw_b, 0, out_dtype=F32, name="br_b_bwd_x")
        doc = _mm_nt(dtc, gw_c, 0, out_dtype=F32, name="br_c_bwd_x")
        dq_buf = lax.empty((L, NQ_CHUNKS * HEAD_DIM), F32)
        dk_buf = lax.empty((L, NK_CHUNKS * HEAD_DIM), F32)
        dv_buf = lax.empty((L, NK_CHUNKS * HEAD_DIM), F32)
        dq_buf, dk_buf, dv_buf, dsink = _band_bwd(
            s["qn"], s["kn"], s["proj"], doa, s["oa"], s["lse_a"], dq_buf, dk_buf, dv_buf, dil=1, radius=A_RADIUS,
            nkv=A_KV_HEADS, group=A_GROUP, q0=0, k0=0, v0=PC_VA, o0=0, sink=s["sink"], name="attn_a_bwd")
        for g, (window, dil) in enumerate(B_PATTERNS):
            dq_buf, dk_buf, dv_buf = _band_bwd(
                s["qn"], s["kn"], s["proj"], dob, s["ob"], s["lse_b"], dq_buf, dk_buf, dv_buf, dil=dil,
                radius=window // (2 * dil), nkv=B_HG, group=1, q0=8 + g * B_HG, k0=2 + g * B_HG,
                v0=PC_VB + g * B_HG, o0=0, name=f"attn_b{g}_bwd")
        dq_buf, dk_buf, dv_buf, dbias_t = _c_bwd(s["qn"], s["kn"], s["proj"], s["bias_t"], doc, s["oc"], s["lse_c"],
                                                 dq_buf, dk_buf, dv_buf, name="attn_c_bwd")
        dproj, dgq = _qk_bwd(dq_buf, s["proj"], s["gq"], cos, sin, dproj, Q_PIECES, NQ_CHUNKS, Q_ROPE_UPTO, Q_CG,
                             name="qnorm_bwd")
        dproj, dgk = _qk_bwd(dk_buf, s["proj"], s["gk"], cos, sin, dproj, K_PIECES, NK_CHUNKS, K_ROPE_UPTO, K_CG,
                             name="knorm_bwd")
        dproj = _v_bwd(dv_buf, dproj, name="v_bwd")
        g_in = _mm_tn(s["h1"], dproj, N_DEV, name="proj_bwd_w")
        dh1 = _mm_nt(dproj, gw_in, 0, out_dtype=F32, name="proj_bwd_x")
        dx, dxb, dg1 = _rms_bwd(s["x"], norm1_g[i][None], dh1, dx1, name="rms1_bwd")
        if pending is not None:
            recv[i + 1], own[i + 1] = scatter_wait(pending, dx)
        pending = (scatter_start([g_in], [0], dx, "_in"), rest)

        drpb = _exact_mm(dbias_t.reshape(C_HEADS * C_NREL, GRID_W * GRID_W), expand, name="c_bias_reduce")
        dgq, dgk = dgq.reshape(NQ_CHUNKS, HEAD_DIM), dgk.reshape(NK_CHUNKS, HEAD_DIM)
        dqk_g = jnp.stack([dgq[0:8].sum(0), dgk[0:2].sum(0), dgq[8:20].sum(0), dgk[2:14].sum(0),
                           dgq[20:28].sum(0), dgk[14:22].sum(0)])
        small_grads[i] = (dg1.reshape(D), dqk_g, dsink[:, 0, 0],
                          drpb[:, :C_NCOL].reshape(C_HEADS, C_NREL, C_NCOL), dg2.reshape(D))

    small_names = [norm1_g, qk_norm_g, sink_a, rpb_c, norm2_g]
    small_m = [m_norm1_g, m_qk_norm_g, m_sink_a, m_rpb_c, m_norm2_g]
    small_v = [v_norm1_g, v_qk_norm_g, v_sink_a, v_rpb_c, v_norm2_g]
    shapes = [a.shape for a in small_names]
    total = sum(int(np.prod(sh)) for sh in shapes) + 128
    rows = -(-total // 1024) * 8
    stacked = [jnp.stack([small_grads[i][j] for i in range(nl)]) for j in range(5)]
    packed = _pack([loss_row.reshape(-1)] + stacked, rows)
    summed = _all_reduce_small(packed)
    loss = summed[0, 0]
    zero_row = jnp.zeros((128,), F32)
    d_s, m_s, v_s = _adamw_small(summed, _pack([zero_row] + small_names, rows), _pack([zero_row] + small_m, rows),
                                 _pack([zero_row] + small_v, rows))
    shapes1 = [(128,)] + shapes
    g_small = _unpack(summed, shapes1)[1:]
    d_small = _unpack(d_s, shapes1)[1:]
    m_small = _unpack(m_s, shapes1)[1:]
    v_small = _unpack(v_s, shapes1)[1:]

    big_m = [m_w_in, m_w_br_a, m_w_br_b, m_w_br_c, m_w_o, m_w_gate_up, m_w_down]
    big_v = [v_w_in, v_w_br_a, v_w_br_b, v_w_br_c, v_w_o, v_w_gate_up, v_w_down]
    big_out = [[lax.empty(w.shape, F32) for _ in range(4)] for w in big]
    token = pending[0][4]
    for i in list(range(nl - 1, 0, -1)) + [0]:
        if i == 0:
            recv[0], own[0] = scatter_wait(pending, token)
        for j in range(len(big)):
            big_out[j], token = _adamw_layer(recv[i][j], own[i][j], kinds[j], big[j], big_m[j], big_v[j],
                                             big_out[j], i, token, name="adamw_" + big_names[j])

    order = ["norm1_g", "w_in", "qk_norm_g", "sink_a", "rpb_c", "w_br_a", "w_br_b", "w_br_c", "w_o", "norm2_g",
             "w_gate_up", "w_down"]
    small_idx = {"norm1_g": 0, "qk_norm_g": 1, "sink_a": 2, "rpb_c": 3, "norm2_g": 4}
    big_idx = {n: j for j, n in enumerate(big_names)}

    def pick(kind):
        out = []
        for n in order:
            if n in small_idx:
                out.append([g_small, d_small, m_small, v_small][kind][small_idx[n]])
            else:
                out.append(big_out[big_idx[n]][kind])
        return out

    return (loss, dx.reshape(1, L, D), *pick(0), *pick(1), *pick(2), *pick(3))
```

```python
import functools
import math

import numpy as np
import jax
import jax.numpy as jnp
from jax import lax
from jax.experimental import pallas as pl
from jax.experimental.pallas import tpu as pltpu

F32 = jnp.float32
BF16 = jnp.bfloat16
MESH = pl.DeviceIdType.MESH
N_DEV = 8

HEAD_DIM = 128
NORM_EPS = 1e-6
ROPE_THETA = 10000.0
ATT_SCALE = HEAD_DIM ** -0.5
NEG = -1e30

A_Q_HEADS, A_KV_HEADS, A_RADIUS = 8, 2, 128
A_GROUP = A_Q_HEADS // A_KV_HEADS
B_PATTERNS = ((128, 1), (512, 4), (2048, 16))
B_HG = 4
B_HEADS = len(B_PATTERNS) * B_HG
C_HEADS, GRID_W, C_WIN_ROWS, C_WIN_COLS = 8, 64, 8, 16
C_NREL = 2 * C_WIN_ROWS - 1
C_NCOL = 2 * C_WIN_COLS - 1

PC_QA, PC_KA, PC_VA = 0, 8, 10
PC_QB, PC_KB, PC_VB = 12, 24, 36
PC_QC, PC_KC, PC_VC = 48, 56, 64
N_QKV_CHUNKS = 72
Q_PIECES = ((0, 8, PC_QA), (8, 12, PC_QB), (20, 8, PC_QC))
K_PIECES = ((0, 2, PC_KA), (2, 12, PC_KB), (14, 8, PC_KC))
V_PIECES = ((0, 2, PC_VA), (2, 12, PC_VB), (14, 8, PC_VC))
NQ_CHUNKS, NK_CHUNKS = 28, 22
Q_ROPE_UPTO, K_ROPE_UPTO = 20, 14
Q_CG, K_CG = 4, 2

ADAM_LR, ADAM_B1, ADAM_B2, ADAM_EPS, ADAM_WD, ADAM_STEP = 0.001, 0.9, 0.999, 1e-08, 0.01, 10

VMEM_LIMIT = 48 * 1024 * 1024


def _tile(dim, prefs):
    for p in prefs:
        if dim % p == 0:
            return p
    return dim


NN_WEIGHT_TILE_BYTES = 8 * 1024 * 1024
NT_WEIGHT_TILE_BYTES = 4 * 1024 * 1024
TN_ACC_BYTES = 6 * 1024 * 1024
MAX_COL_TILE = 2048


def _col_tile(ns):
    return ns if ns <= MAX_COL_TILE else _tile(ns, (MAX_COL_TILE, 1024, 512, 256, 128))


def _params(sem, **kw):
    return pltpu.CompilerParams(dimension_semantics=sem, vmem_limit_bytes=VMEM_LIMIT, **kw)


def _piece_map(pieces):
    def f(c):
        out = c - pieces[0][0] + pieces[0][2]
        for first, _, pfirst in pieces[1:]:
            out = jnp.where(c >= first, c - first + pfirst, out)
        return out
    return f


def _mm_nn(a, w, layer, *, out_dtype, name, res=None):
    M, K = a.shape
    nb, _, Kw, ns = w.shape
    assert Kw == K
    tn = _col_tile(ns)
    tm = _tile(M, (1024, 512, 256) if tn <= 512 else (512, 256))
    tk = _tile(K, tuple(t for t in (2048, 1408, 1024, 512, 256) if t * tn * 2 <= NN_WEIGHT_TILE_BYTES))
    nj, nk = ns // tn, K // tk

    def body(*refs):
        a_ref, w_ref = refs[:2]
        r_ref = None if res is None else refs[2]
        o_ref = refs[2 if res is None else 3]
        part = jnp.dot(a_ref[...].astype(BF16), w_ref[...], preferred_element_type=F32)
        if nk == 1:
            if r_ref is not None:
                part = part + r_ref[...]
            o_ref[...] = part.astype(out_dtype)
            return
        acc_ref = refs[-1]
        k = pl.program_id(3)

        @pl.when(k == 0)
        def _():
            acc_ref[...] = part

        @pl.when(k > 0)
        def _():
            acc_ref[...] += part

        @pl.when(k == nk - 1)
        def _():
            r = acc_ref[...]
            if r_ref is not None:
                r = r + r_ref[...]
            o_ref[...] = r.astype(out_dtype)

    in_specs = [pl.BlockSpec((tm, tk), lambda i, b, j, k: (i, k)),
                pl.BlockSpec((None, None, tk, tn), lambda i, b, j, k: (b, layer, k, j))]
    args = [a, w]
    if res is not None:
        in_specs.append(pl.BlockSpec((tm, tn), lambda i, b, j, k: (i, b * nj + j)))
        args.append(res)
    return pl.pallas_call(
        body, name=name, grid=(M // tm, nb, nj, nk), in_specs=in_specs,
        out_specs=pl.BlockSpec((tm, tn), lambda i, b, j, k: (i, b * nj + j)),
        out_shape=jax.ShapeDtypeStruct((M, nb * ns), out_dtype),
        scratch_shapes=[] if nk == 1 else [pltpu.VMEM((tm, tn), F32)],
        compiler_params=_params(("parallel", "parallel", "parallel", "arbitrary")),
    )(*args)


def _mm_nt(a, w, layer, *, out_dtype, name, dep=None):
    M, N = a.shape
    nb, _, K, ns = w.shape
    assert N == nb * ns
    tm = _tile(M, (1024, 512, 256))
    tn = _col_tile(ns)
    tk = _tile(K, tuple(t for t in (1024, 512, 256) if t * tn * 2 <= NT_WEIGHT_TILE_BYTES))
    nj = ns // tn
    nred = nb * nj

    def body(*refs):
        a_ref, w_ref = refs[:2]
        o_ref, acc_ref = refs[-2:]
        s = pl.program_id(2) * nj + pl.program_id(3)
        part = lax.dot_general(a_ref[...].astype(BF16), w_ref[...], (((1,), (1,)), ((), ())),
                               preferred_element_type=F32)

        @pl.when(s == 0)
        def _():
            acc_ref[...] = part

        @pl.when(s > 0)
        def _():
            acc_ref[...] += part

        @pl.when(s == nred - 1)
        def _():
            o_ref[...] = acc_ref[...].astype(out_dtype)

    in_specs = [pl.BlockSpec((tm, tn), lambda i, kk, b, j: (i, b * nj + j)),
                pl.BlockSpec((None, None, tk, tn), lambda i, kk, b, j: (b, layer, kk, j))]
    args = [a, w]
    if dep is not None:
        in_specs.append(ANY_SPEC)
        args.append(dep)
    return pl.pallas_call(
        body, name=name, grid=(M // tm, K // tk, nb, nj), in_specs=in_specs,
        out_specs=pl.BlockSpec((tm, tk), lambda i, kk, b, j: (i, kk)),
        out_shape=jax.ShapeDtypeStruct((M, K), out_dtype),
        scratch_shapes=[pltpu.VMEM((tm, tk), F32)],
        compiler_params=_params(("parallel", "parallel", "arbitrary", "arbitrary")),
    )(*args)


def _mm_tn(a, g, nb, *, name):
    M, Ka = a.shape
    N = g.shape[1]
    ns = N // nb
    tn = _col_tile(ns)
    tka = _tile(Ka, tuple(t for t in (1024, 512, 256) if t * tn * 4 <= TN_ACC_BYTES))
    tm = _tile(M, (2048, 1024, 512, 256))
    nj, nm = ns // tn, M // tm

    def body(a_ref, g_ref, o_ref, acc_ref):
        m = pl.program_id(3)
        part = lax.dot_general(a_ref[...].astype(BF16), g_ref[...].astype(BF16), (((0,), (0,)), ((), ())),
                               preferred_element_type=F32)

        @pl.when(m == 0)
        def _():
            acc_ref[...] = part

        @pl.when(m > 0)
        def _():
            acc_ref[...] += part

        @pl.when(m == nm - 1)
        def _():
            o_ref[...] = acc_ref[...].astype(BF16)

    return pl.pallas_call(
        body, name=name, grid=(Ka // tka, nb, nj, nm),
        in_specs=[pl.BlockSpec((tm, tka), lambda ka, b, j, m: (m, ka)),
                  pl.BlockSpec((tm, tn), lambda ka, b, j, m: (m, b * nj + j))],
        out_specs=pl.BlockSpec((None, tka, tn), lambda ka, b, j, m: (b, ka, j)),
        out_shape=jax.ShapeDtypeStruct((nb, Ka, ns), BF16),
        scratch_shapes=[pltpu.VMEM((tka, tn), F32)],
        compiler_params=_params(("parallel", "parallel", "parallel", "arbitrary")),
    )(a, g)


def _exact_mm(a, e, *, name):
    R, K = a.shape
    N = e.shape[1]

    def body(a_ref, e_ref, o_ref):
        x = a_ref[...]
        hi = x.astype(BF16)
        r1 = x - hi.astype(F32)
        mid = r1.astype(BF16)
        lo = (r1 - mid.astype(F32)).astype(BF16)
        ev = e_ref[...]
        o_ref[...] = (jnp.dot(hi, ev, preferred_element_type=F32) + jnp.dot(mid, ev, preferred_element_type=F32)
                      + jnp.dot(lo, ev, preferred_element_type=F32))

    return pl.pallas_call(body, name=name, out_shape=jax.ShapeDtypeStruct((R, N), F32),
                          compiler_params=pltpu.CompilerParams(vmem_limit_bytes=VMEM_LIMIT))(a, e)


def _rms_fwd(x, g, dep, *, name):
    L, D = x.shape
    tl = _tile(L, (256, 128))

    def body(x_ref, g_ref, _dep, h_ref):
        xv = x_ref[...]
        rstd = lax.rsqrt(jnp.mean(xv * xv, axis=-1, keepdims=True) + NORM_EPS)
        h_ref[...] = (xv * rstd * g_ref[...]).astype(BF16)

    return pl.pallas_call(
        body, name=name, grid=(L // tl,),
        in_specs=[pl.BlockSpec((tl, D), lambda t: (t, 0)), pl.BlockSpec((1, D), lambda t: (0, 0)), ANY_SPEC],
        out_specs=pl.BlockSpec((tl, D), lambda t: (t, 0)),
        out_shape=jax.ShapeDtypeStruct((L, D), BF16),
        compiler_params=_params(("parallel",)),
    )(x, g, dep)


def _rms_bwd(x, g, dy, dres, *, name):
    L, D = x.shape
    tl = _tile(L, (128,))

    def body(x_ref, g_ref, dy_ref, dres_ref, dx_ref, dxb_ref, dg_ref):
        t = pl.program_id(0)
        xv = x_ref[...]
        rstd = lax.rsqrt(jnp.mean(xv * xv, axis=-1, keepdims=True) + NORM_EPS)
        xhat = xv * rstd
        dyv = dy_ref[...]
        dxhat = dyv * g_ref[...]
        c = jnp.mean(dxhat * xhat, axis=-1, keepdims=True)
        dx = dres_ref[...] + rstd * (dxhat - xhat * c)
        dx_ref[...] = dx
        dxb_ref[...] = dx.astype(BF16)
        dgp = jnp.sum(dyv * xhat, axis=0, keepdims=True)

        @pl.when(t == 0)
        def _():
            dg_ref[...] = dgp

        @pl.when(t > 0)
        def _():
            dg_ref[...] += dgp

    row = pl.BlockSpec((tl, D), lambda t: (t, 0))
    vec = pl.BlockSpec((1, D), lambda t: (0, 0))
    return pl.pallas_call(
        body, name=name, grid=(L // tl,), in_specs=[row, vec, row, row], out_specs=[row, row, vec],
        out_shape=[jax.ShapeDtypeStruct((L, D), F32), jax.ShapeDtypeStruct((L, D), BF16),
                   jax.ShapeDtypeStruct((1, D), F32)],
        compiler_params=_params(("arbitrary",)),
    )(x, g, dy, dres)


def _gate_fwd(proj, ta, tb, tc, *, name):
    L, D = ta.shape
    tl, tcw = _tile(L, (256, 128)), _tile(D, (1024, 512, 256, 128))
    off = N_QKV_CHUNKS * HEAD_DIM // tcw
    nd = D // tcw

    def body(g0, g1, g2, a_ref, b_ref, c_ref, o_ref):
        m = (jax.nn.sigmoid(g0[...]) * a_ref[...].astype(F32) + jax.nn.sigmoid(g1[...]) * b_ref[...].astype(F32)
             + jax.nn.sigmoid(g2[...]) * c_ref[...].astype(F32))
        o_ref[...] = m.astype(BF16)

    blk = pl.BlockSpec((tl, tcw), lambda t, j: (t, j))
    gl = [pl.BlockSpec((tl, tcw), functools.partial(lambda t, j, i: (t, off + i * nd + j), i=i)) for i in range(3)]
    return pl.pallas_call(
        body, name=name, grid=(L // tl, nd), in_specs=gl + [blk, blk, blk], out_specs=blk,
        out_shape=jax.ShapeDtypeStruct((L, D), BF16),
        compiler_params=_params(("parallel", "parallel")),
    )(proj, proj, proj, ta, tb, tc)


def _gate_bwd(proj, ta, tb, tc, dmerged, *, name):
    L, D = ta.shape
    ncols = proj.shape[1]
    tl, tcw = _tile(L, (256, 128)), _tile(D, (1024, 512, 256, 128))
    off = N_QKV_CHUNKS * HEAD_DIM // tcw
    nd = D // tcw

    def body(g0, g1, g2, a_ref, b_ref, c_ref, dm_ref, da_ref, db_ref, dc_ref, dgl_ref):
        i = pl.program_id(2)
        sg = jax.nn.sigmoid(jnp.where(i == 0, g0[...], jnp.where(i == 1, g1[...], g2[...])))
        sel_t = jnp.where(i == 0, a_ref[...], jnp.where(i == 1, b_ref[...], c_ref[...])).astype(F32)
        dt = dm_ref[...] * sg
        dtb = dt.astype(BF16)

        @pl.when(i == 0)
        def _():
            da_ref[...] = dtb

        @pl.when(i == 1)
        def _():
            db_ref[...] = dtb

        @pl.when(i == 2)
        def _():
            dc_ref[...] = dtb

        dgl_ref[...] = (dt * sel_t * (1.0 - sg)).astype(BF16)

    blk = pl.BlockSpec((tl, tcw), lambda t, j, i: (t, j))
    gl = [pl.BlockSpec((tl, tcw), functools.partial(lambda t, j, i, q: (t, off + q * nd + j), q=q)) for q in range(3)]
    return pl.pallas_call(
        body, name=name, grid=(L // tl, nd, 3), in_specs=gl + [blk, blk, blk, blk],
        out_specs=[blk, blk, blk, pl.BlockSpec((tl, tcw), lambda t, j, i: (t, off + i * nd + j))],
        out_shape=[jax.ShapeDtypeStruct((L, D), BF16)] * 3 + [jax.ShapeDtypeStruct((L, ncols), BF16)],
        compiler_params=_params(("parallel", "parallel", "arbitrary")),
    )(proj, proj, proj, ta, tb, tc, dmerged)


def _swiglu_fwd(gu, *, name):
    L, F2 = gu.shape
    F = F2 // 2
    tl = _tile(L, (128, 64))

    def body(gu_ref, o_ref):
        gt = gu_ref[:, :F].astype(F32)
        o_ref[...] = (gt * jax.nn.sigmoid(gt) * gu_ref[:, F:].astype(F32)).astype(BF16)

    return pl.pallas_call(
        body, name=name, grid=(L // tl,), in_specs=[pl.BlockSpec((tl, F2), lambda t: (t, 0))],
        out_specs=pl.BlockSpec((tl, F), lambda t: (t, 0)),
        out_shape=jax.ShapeDtypeStruct((L, F), BF16),
        compiler_params=_params(("parallel",)),
    )(gu)


def _swiglu_bwd(gu, dact, *, name):
    L, F2 = gu.shape
    F = F2 // 2
    tl = _tile(L, (128, 64))

    def body(gu_ref, d_ref, o_ref):
        gt, up, d = gu_ref[:, :F].astype(F32), gu_ref[:, F:].astype(F32), d_ref[...].astype(F32)
        sg = jax.nn.sigmoid(gt)
        o_ref[:, :F] = (d * up * sg * (1.0 + gt * (1.0 - sg))).astype(BF16)
        o_ref[:, F:] = (d * gt * sg).astype(BF16)

    return pl.pallas_call(
        body, name=name, grid=(L // tl,),
        in_specs=[pl.BlockSpec((tl, F2), lambda t: (t, 0)), pl.BlockSpec((tl, F), lambda t: (t, 0))],
        out_specs=pl.BlockSpec((tl, F2), lambda t: (t, 0)),
        out_shape=jax.ShapeDtypeStruct((L, F2), BF16),
        compiler_params=_params(("parallel",)),
    )(gu, dact)


def _loss(y, tgt, *, name):
    L, D = y.shape
    tl = _tile(L, (256, 128))
    nt = L // tl

    def body(y_ref, t_ref, dy_ref, dyb_ref, loss_ref, acc_ref):
        t = pl.program_id(0)
        e = y_ref[...] - t_ref[...]
        dy = e * (1.0 / D)
        dy_ref[...] = dy
        dyb_ref[...] = dy.astype(BF16)
        part = jnp.sum(e * e, axis=0, keepdims=True)

        @pl.when(t == 0)
        def _():
            acc_ref[...] = part

        @pl.when(t > 0)
        def _():
            acc_ref[...] += part

        @pl.when(t == nt - 1)
        def _():
            loss_ref[...] = jnp.broadcast_to(jnp.sum(acc_ref[...], axis=-1, keepdims=True) * (0.5 / D), (1, 128))

    row = pl.BlockSpec((tl, D), lambda t: (t, 0))
    return pl.pallas_call(
        body, name=name, grid=(nt,), in_specs=[row, row],
        out_specs=[row, row, pl.BlockSpec((1, 128), lambda t: (0, 0))],
        out_shape=[jax.ShapeDtypeStruct((L, D), F32), jax.ShapeDtypeStruct((L, D), BF16),
                   jax.ShapeDtypeStruct((1, 128), F32)],
        scratch_shapes=[pltpu.VMEM((1, D), F32)],
        compiler_params=_params(("arbitrary",)),
    )(y, tgt)


def _rope(v, cos, sin_signed):
    return v * cos + pltpu.roll(v, HEAD_DIM // 2, 1) * sin_signed


def _head_mean(x):
    hi = x.astype(BF16)
    lo = (x - hi.astype(F32)).astype(BF16)
    ones = jnp.ones((HEAD_DIM, HEAD_DIM), BF16)
    total = jnp.dot(hi, ones, preferred_element_type=F32) + jnp.dot(lo, ones, preferred_element_type=F32)
    return total * (1.0 / HEAD_DIM)


def _qk_fwd(proj, gtab, cos, sin, pieces, nchunks, rope_upto, cg, *, name, dep=None):
    L = proj.shape[0]
    tl = _tile(L, (512, 256, 128))
    W = cg * HEAD_DIM
    pmap = _piece_map(tuple((a // cg, n // cg, p // cg) for a, n, p in pieces))

    def body(*refs):
        p_ref, g_ref, cos_ref, sin_ref = refs[:4]
        o_ref = refs[-1]
        c = pl.program_id(1)

        def norm(j):
            cols = slice(j * HEAD_DIM, (j + 1) * HEAD_DIM)
            x = p_ref[:, cols]
            rstd = lax.rsqrt(_head_mean(x * x) + NORM_EPS)
            return cols, x * rstd * g_ref[:, cols]

        @pl.when(c < rope_upto // cg)
        def _():
            for j in range(cg):
                cols, y = norm(j)
                o_ref[:, cols] = _rope(y, cos_ref[...], sin_ref[...])

        @pl.when(c >= rope_upto // cg)
        def _():
            for j in range(cg):
                cols, y = norm(j)
                o_ref[:, cols] = y

    pos = pl.BlockSpec((tl, HEAD_DIM), lambda t, c: (t, 0))
    in_specs = [pl.BlockSpec((tl, W), lambda t, c: (t, pmap(c))),
                pl.BlockSpec((None, 1, W), lambda t, c: (c, 0, 0)), pos, pos]
    args = [proj, gtab, cos, sin]
    if dep is not None:
        in_specs.append(ANY_SPEC)
        args.append(dep)
    return pl.pallas_call(
        body, name=name, grid=(L // tl, nchunks // cg), in_specs=in_specs,
        out_specs=pl.BlockSpec((tl, W), lambda t, c: (t, c)),
        out_shape=jax.ShapeDtypeStruct((L, nchunks * HEAD_DIM), F32),
        compiler_params=_params(("parallel", "parallel")),
    )(*args)


def _qk_bwd(dqk, proj, gtab, cos, sin, dproj, pieces, nchunks, rope_upto, cg, *, name):
    L = proj.shape[0]
    tl = _tile(L, (512, 256, 128))
    W = cg * HEAD_DIM
    pmap = _piece_map(tuple((a // cg, n // cg, p // cg) for a, n, p in pieces))

    def body(d_ref, p_ref, g_ref, cos_ref, sin_ref, _, o_ref, dg_ref):
        c, t = pl.program_id(0), pl.program_id(1)

        @pl.when(t == 0)
        def _():
            dg_ref[...] = jnp.zeros_like(dg_ref)

        for j in range(cg):
            cols = slice(j * HEAD_DIM, (j + 1) * HEAD_DIM)
            x = p_ref[:, cols]
            rstd = lax.rsqrt(_head_mean(x * x) + NORM_EPS)
            xhat = x * rstd
            dy = d_ref[:, cols]
            dy = jnp.where(c < rope_upto // cg, _rope(dy, cos_ref[...], -sin_ref[...]), dy)
            dxhat = dy * g_ref[:, cols]
            cm = _head_mean(dxhat * xhat)
            o_ref[:, cols] = (rstd * (dxhat - xhat * cm)).astype(BF16)
            dg_ref[:, cols] += jnp.sum(dy * xhat, axis=0, keepdims=True)

    pos = pl.BlockSpec((tl, HEAD_DIM), lambda c, t: (t, 0))
    gspec = pl.BlockSpec((None, 1, W), lambda c, t: (c, 0, 0))
    out, dg = pl.pallas_call(
        body, name=name, grid=(nchunks // cg, L // tl),
        in_specs=[pl.BlockSpec((tl, W), lambda c, t: (t, c)),
                  pl.BlockSpec((tl, W), lambda c, t: (t, pmap(c))), gspec, pos, pos,
                  pl.BlockSpec(memory_space=pl.ANY)],
        out_specs=[pl.BlockSpec((tl, W), lambda c, t: (t, pmap(c))), gspec],
        out_shape=[jax.ShapeDtypeStruct(dproj.shape, BF16), jax.ShapeDtypeStruct((nchunks // cg, 1, W), F32)],
        input_output_aliases={5: 0},
        compiler_params=_params(("parallel", "arbitrary")),
    )(dqk, proj, gtab, cos, sin, dproj)
    return out, dg


def _v_bwd(dv, dproj, *, name):
    L = dv.shape[0]
    tl = _tile(L, (512, 256, 128))
    pmap = _piece_map(tuple((a // 2, n // 2, p // 2) for a, n, p in V_PIECES))

    def body(d_ref, _, o_ref):
        o_ref[...] = d_ref[...].astype(BF16)

    return pl.pallas_call(
        body, name=name, grid=(L // tl, NK_CHUNKS // 2),
        in_specs=[pl.BlockSpec((tl, 2 * HEAD_DIM), lambda t, c: (t, c)), pl.BlockSpec(memory_space=pl.ANY)],
        out_specs=pl.BlockSpec((tl, 2 * HEAD_DIM), lambda t, c: (t, pmap(c))),
        out_shape=jax.ShapeDtypeStruct(dproj.shape, BF16),
        input_output_aliases={1: 0},
        compiler_params=_params(("parallel", "parallel")),
    )(dv, dproj)


def _band_geometry(L, dil, radius):
    n = L // dil
    bq = min(256, max(n // 2, 64), n)
    width = min(bq + 2 * radius, n)
    return n, bq, width


def _band_rows(dil, r, first, count):
    if dil == 1:
        return pl.ds(pl.multiple_of(first, 8), count)
    return pl.ds(r + first * dil, count, stride=dil)


def _band_mask(i, bq, width, radius, ws):
    qpos = i * bq + lax.broadcasted_iota(jnp.int32, (bq, width), 0)
    kpos = ws + lax.broadcasted_iota(jnp.int32, (bq, width), 1)
    return jnp.abs(kpos - qpos) <= radius


def _band_fwd(qn, kn, proj, *, dil, radius, nkv, group, q0, k0, v0, sink=None, name):
    L = qn.shape[0]
    n, bq, width = _band_geometry(L, dil, radius)
    tq = bq * dil
    nh = nkv * group

    def body(*refs):
        if sink is None:
            q_ref, k_ref, v_ref, o_ref, lse_ref = refs
        else:
            q_ref, k_ref, v_ref, s_ref, o_ref, lse_ref = refs
        i = pl.program_id(2)
        ws = jnp.clip(i * bq - radius, 0, n - width)
        valid = _band_mask(i, bq, width, radius, ws)

        def one(r, carry):
            qrows = _band_rows(dil, r, 0, bq)
            krows = _band_rows(dil, r, ws, width)
            q = q_ref[qrows, :].astype(BF16)
            k = k_ref[krows, :].astype(BF16)
            v = v_ref[krows, :].astype(BF16)
            s = lax.dot_general(q, k, (((1,), (1,)), ((), ())), preferred_element_type=F32) * ATT_SCALE
            s = jnp.where(valid, s, NEG)
            m = jnp.max(s, axis=-1, keepdims=True)
            if sink is not None:
                m = jnp.maximum(m, s_ref[...][:, :1])
            p = jnp.exp(s - m)
            denom = jnp.sum(p, axis=-1, keepdims=True)
            if sink is not None:
                denom = denom + jnp.exp(s_ref[...][:, :1] - m)
            pn = (p / denom).astype(BF16)
            o_ref[qrows, :] = jnp.dot(pn, v, preferred_element_type=F32)
            lse_ref[qrows, :] = jnp.broadcast_to(m + jnp.log(denom), (bq, HEAD_DIM))
            return carry

        if dil == 1:
            one(0, 0)
        else:
            lax.fori_loop(0, dil, one, 0)

    qspec = pl.BlockSpec((tq, HEAD_DIM), lambda hk, g, i: (i, q0 + hk * group + g))
    in_specs = [qspec,
                pl.BlockSpec((L, HEAD_DIM), lambda hk, g, i: (0, k0 + hk)),
                pl.BlockSpec((L, HEAD_DIM), lambda hk, g, i: (0, v0 + hk))]
    args = [qn, kn, proj]
    if sink is not None:
        in_specs.append(pl.BlockSpec((None, 1, HEAD_DIM), lambda hk, g, i: (hk * group + g, 0, 0)))
        args.append(sink)
    ospec = pl.BlockSpec((tq, HEAD_DIM), lambda hk, g, i: (i, hk * group + g))
    return pl.pallas_call(
        body, name=name, grid=(nkv, group, n // bq), in_specs=in_specs, out_specs=[ospec, ospec],
        out_shape=[jax.ShapeDtypeStruct((L, nh * HEAD_DIM), F32)] * 2,
        compiler_params=_params(("parallel", "parallel", "arbitrary")),
    )(*args)


def _band_bwd(qn, kn, proj, do, o, lse, dq_buf, dk_buf, dv_buf, *, dil, radius, nkv, group, q0, k0, v0, o0,
              sink=None, name):
    L = qn.shape[0]
    n, bq, width = _band_geometry(L, dil, radius)
    tq = bq * dil
    nh = nkv * group
    n_in = 6 + (1 if sink is not None else 0)

    def body(*refs):
        q_ref, k_ref, v_ref, do_ref, o_ref, lse_ref = refs[:6]
        s_ref = refs[6] if sink is not None else None
        outs = refs[n_in + 3:]
        dq_ref, dk_ref, dv_ref = outs[:3]
        ds_ref = outs[3] if sink is not None else None
        g, i = pl.program_id(1), pl.program_id(2)
        ws = jnp.clip(i * bq - radius, 0, n - width)
        valid = _band_mask(i, bq, width, radius, ws)

        @pl.when((g == 0) & (i == 0))
        def _():
            dk_ref[...] = jnp.zeros_like(dk_ref)
            dv_ref[...] = jnp.zeros_like(dv_ref)

        if sink is not None:
            @pl.when(i == 0)
            def _():
                ds_ref[...] = jnp.zeros_like(ds_ref)

        def one(r, carry):
            qrows = _band_rows(dil, r, 0, bq)
            krows = _band_rows(dil, r, ws, width)
            q = q_ref[qrows, :].astype(BF16)
            k = k_ref[krows, :].astype(BF16)
            v = v_ref[krows, :].astype(BF16)
            dov = do_ref[qrows, :]
            lse_v = lse_ref[qrows, :][:, :1]
            delta = jnp.sum(dov * o_ref[qrows, :], axis=-1, keepdims=True)
            dob = dov.astype(BF16)
            s = lax.dot_general(q, k, (((1,), (1,)), ((), ())), preferred_element_type=F32) * ATT_SCALE
            p = jnp.where(valid, jnp.exp(s - lse_v), 0.0)
            dp = lax.dot_general(dob, v, (((1,), (1,)), ((), ())), preferred_element_type=F32)
            dsb = (p * (dp - delta)).astype(BF16)
            dq_ref[qrows, :] = jnp.dot(dsb, k, preferred_element_type=F32) * ATT_SCALE
            dk_ref[krows, :] += lax.dot_general(dsb, q, (((0,), (0,)), ((), ())),
                                                preferred_element_type=F32) * ATT_SCALE
            dv_ref[krows, :] += lax.dot_general(p.astype(BF16), dob, (((0,), (0,)), ((), ())),
                                                preferred_element_type=F32)
            if sink is not None:
                ps = jnp.exp(s_ref[...][:, :1] - lse_v)
                ds_ref[...] += jnp.broadcast_to(jnp.sum(-ps * delta, axis=0, keepdims=True), (1, HEAD_DIM))
            return carry

        if dil == 1:
            one(0, 0)
        else:
            lax.fori_loop(0, dil, one, 0)

    hspec = pl.BlockSpec((tq, HEAD_DIM), lambda hk, g, i: (i, o0 + hk * group + g))
    qspec = pl.BlockSpec((tq, HEAD_DIM), lambda hk, g, i: (i, q0 + hk * group + g))
    kspec = pl.BlockSpec((L, HEAD_DIM), lambda hk, g, i: (0, k0 + hk))
    any_spec = pl.BlockSpec(memory_space=pl.ANY)
    in_specs = [qspec, kspec, pl.BlockSpec((L, HEAD_DIM), lambda hk, g, i: (0, v0 + hk)), hspec, hspec, hspec]
    args = [qn, kn, proj, do, o, lse]
    if sink is not None:
        in_specs.append(pl.BlockSpec((None, 1, HEAD_DIM), lambda hk, g, i: (hk * group + g, 0, 0)))
        args.append(sink)
    in_specs += [any_spec] * 3
    args += [dq_buf, dk_buf, dv_buf]
    out_specs = [qspec, kspec, kspec]
    out_shape = [jax.ShapeDtypeStruct(dq_buf.shape, F32), jax.ShapeDtypeStruct(dk_buf.shape, F32),
                 jax.ShapeDtypeStruct(dv_buf.shape, F32)]
    if sink is not None:
        out_specs.append(pl.BlockSpec((None, 1, HEAD_DIM), lambda hk, g, i: (hk * group + g, 0, 0)))
        out_shape.append(jax.ShapeDtypeStruct((nh, 1, HEAD_DIM), F32))
    return pl.pallas_call(
        body, name=name, grid=(nkv, group, n // bq), in_specs=in_specs, out_specs=out_specs, out_shape=out_shape,
        input_output_aliases={n_in: 0, n_in + 1: 1, n_in + 2: 2},
        compiler_params=_params(("parallel", "arbitrary", "arbitrary")),
    )(*args)


def _combine_b(os_, lses, *, name):
    L, W = os_[0].shape
    tl = _tile(L, (256, 128))

    def body(o0, o1, o2, l0, l1, l2, out_ref, lt_ref):
        a, b, c = l0[...], l1[...], l2[...]
        m = jnp.maximum(jnp.maximum(a, b), c)
        ea, eb, ec = jnp.exp(a - m), jnp.exp(b - m), jnp.exp(c - m)
        tot = ea + eb + ec
        out_ref[...] = (ea * o0[...] + eb * o1[...] + ec * o2[...]) / tot
        lt_ref[...] = m + jnp.log(tot)

    blk = pl.BlockSpec((tl, W), lambda t: (t, 0))
    return pl.pallas_call(
        body, name=name, grid=(L // tl,), in_specs=[blk] * 6, out_specs=[blk, blk],
        out_shape=[jax.ShapeDtypeStruct((L, W), F32)] * 2, compiler_params=_params(("parallel",)),
    )(*os_, *lses)


C_QROWS = 4
C_KROWS = C_QROWS + C_WIN_ROWS
C_QUERIES, C_KEYS = C_QROWS * GRID_W, C_KROWS * GRID_W
_C_KIND_OFFSETS = (C_WIN_ROWS - 1, C_WIN_ROWS - 1 - C_WIN_ROWS // 2, C_WIN_ROWS - 1 - (C_KROWS - C_QROWS))


def _c_geometry(L):
    rows = L // GRID_W
    assert rows >= C_KROWS and rows % C_QROWS == 0
    return rows


def _c_bias_tiles(bias_t):
    cq = np.arange(GRID_W)[:, None]
    ck = np.arange(GRID_W)[None, :]
    start = np.clip(cq - C_WIN_COLS // 2, 0, GRID_W - C_WIN_COLS)
    masked = jnp.where(jnp.asarray((ck >= start) & (ck < start + C_WIN_COLS)), bias_t, NEG)
    blank = jnp.full((C_HEADS, GRID_W, GRID_W), NEG, F32)
    kinds = []
    for kind in range(3):
        off = _C_KIND_OFFSETS[kind]
        row_blocks = []
        for a in range(C_QROWS):
            lo = (0, a, C_KROWS - C_WIN_ROWS)[kind]
            row_blocks.append(jnp.concatenate(
                [masked[:, b - a + off] if lo <= b < lo + C_WIN_ROWS else blank for b in range(C_KROWS)], axis=-1))
        kinds.append(jnp.concatenate(row_blocks, axis=-2))
    return jnp.stack(kinds, axis=1)


def _c_block(g, rows):
    r0 = g * C_QROWS
    k0 = jnp.clip(r0 - C_WIN_ROWS // 2, 0, rows - C_KROWS)
    kind = jnp.where(g == 0, 0, jnp.where(g == rows // C_QROWS - 1, 2, 1))
    return k0, kind, k0 - r0 + (C_WIN_ROWS - 1)


def _c_fwd(qn, kn, proj, tiles, *, name):
    L = qn.shape[0]
    rows = _c_geometry(L)

    def body(q_ref, k_ref, v_ref, t_ref, o_ref, lse_ref):
        k0, _, _ = _c_block(pl.program_id(1), rows)
        krows = pl.ds(pl.multiple_of(k0 * GRID_W, GRID_W), C_KEYS)
        q = q_ref[...].astype(BF16)
        k = k_ref[krows, :].astype(BF16)
        v = v_ref[krows, :].astype(BF16)
        s = lax.dot_general(q, k, (((1,), (1,)), ((), ())), preferred_element_type=F32) * ATT_SCALE + t_ref[...]
        m = jnp.max(s, axis=-1, keepdims=True)
        p = jnp.exp(s - m)
        denom = jnp.sum(p, axis=-1, keepdims=True)
        o_ref[...] = jnp.dot((p / denom).astype(BF16), v, preferred_element_type=F32)
        lse_ref[...] = jnp.broadcast_to(m + jnp.log(denom), (C_QUERIES, HEAD_DIM))

    def tile_index(h, g):
        return (h, _c_block(g, rows)[1], 0, 0)

    ospec = pl.BlockSpec((C_QUERIES, HEAD_DIM), lambda h, g: (g, h))
    return pl.pallas_call(
        body, name=name, grid=(C_HEADS, rows // C_QROWS),
        in_specs=[pl.BlockSpec((C_QUERIES, HEAD_DIM), lambda h, g: (g, 20 + h)),
                  pl.BlockSpec((L, HEAD_DIM), lambda h, g: (0, 14 + h)),
                  pl.BlockSpec((L, HEAD_DIM), lambda h, g: (0, PC_VC + h)),
                  pl.BlockSpec((None, None, C_QUERIES, C_KEYS), tile_index)],
        out_specs=[ospec, ospec],
        out_shape=[jax.ShapeDtypeStruct((L, C_HEADS * HEAD_DIM), F32)] * 2,
        compiler_params=_params(("parallel", "arbitrary")),
    )(qn, kn, proj, tiles)


def _c_bwd(qn, kn, proj, tiles, do, o, lse, dq_buf, dk_buf, dv_buf, *, name):
    L = qn.shape[0]
    rows = _c_geometry(L)

    def body(q_ref, k_ref, v_ref, t_ref, do_ref, o_ref, lse_ref, _a, _b, _c, dq_ref, dk_ref, dv_ref, dt_ref):
        g = pl.program_id(1)
        k0, _, off = _c_block(g, rows)
        krows = pl.ds(pl.multiple_of(k0 * GRID_W, GRID_W), C_KEYS)

        @pl.when(g == 0)
        def _():
            dk_ref[...] = jnp.zeros_like(dk_ref)
            dv_ref[...] = jnp.zeros_like(dv_ref)
            dt_ref[...] = jnp.zeros_like(dt_ref)

        q = q_ref[...].astype(BF16)
        k = k_ref[krows, :].astype(BF16)
        v = v_ref[krows, :].astype(BF16)
        dov = do_ref[...]
        dob = dov.astype(BF16)
        delta = jnp.sum(dov * o_ref[...], axis=-1, keepdims=True)
        s = lax.dot_general(q, k, (((1,), (1,)), ((), ())), preferred_element_type=F32) * ATT_SCALE + t_ref[...]
        p = jnp.exp(s - lse_ref[...][:, :1])
        dp = lax.dot_general(dob, v, (((1,), (1,)), ((), ())), preferred_element_type=F32)
        ds = p * (dp - delta)
        for a in range(C_QROWS):
            for b in range(C_KROWS):
                rel = jnp.clip(b - a + off, 0, C_NREL - 1)
                dt_ref[rel] += ds[a * GRID_W:(a + 1) * GRID_W, b * GRID_W:(b + 1) * GRID_W]
        dsb = ds.astype(BF16)
        dq_ref[...] = jnp.dot(dsb, k, preferred_element_type=F32) * ATT_SCALE
        dk_ref[krows, :] += lax.dot_general(dsb, q, (((0,), (0,)), ((), ())), preferred_element_type=F32) * ATT_SCALE
        dv_ref[krows, :] += lax.dot_general(p.astype(BF16), dob, (((0,), (0,)), ((), ())),
                                            preferred_element_type=F32)

    def tile_index(h, g):
        return (h, _c_block(g, rows)[1], 0, 0)

    hspec = pl.BlockSpec((C_QUERIES, HEAD_DIM), lambda h, g: (g, h))
    qspec = pl.BlockSpec((C_QUERIES, HEAD_DIM), lambda h, g: (g, 20 + h))
    kspec = pl.BlockSpec((L, HEAD_DIM), lambda h, g: (0, 14 + h))
    any_spec = pl.BlockSpec(memory_space=pl.ANY)
    return pl.pallas_call(
        body, name=name, grid=(C_HEADS, rows // C_QROWS),
        in_specs=[qspec, kspec, pl.BlockSpec((L, HEAD_DIM), lambda h, g: (0, PC_VC + h)),
                  pl.BlockSpec((None, None, C_QUERIES, C_KEYS), tile_index),
                  hspec, hspec, hspec, any_spec, any_spec, any_spec],
        out_specs=[qspec, kspec, kspec,
                   pl.BlockSpec((None, C_NREL, GRID_W, GRID_W), lambda h, r: (h, 0, 0, 0))],
        out_shape=[jax.ShapeDtypeStruct(dq_buf.shape, F32), jax.ShapeDtypeStruct(dk_buf.shape, F32),
                   jax.ShapeDtypeStruct(dv_buf.shape, F32),
                   jax.ShapeDtypeStruct((C_HEADS, C_NREL, GRID_W, GRID_W), F32)],
        input_output_aliases={7: 0, 8: 1, 9: 2},
        compiler_params=_params(("parallel", "arbitrary")),
    )(qn, kn, proj, tiles, do, o, lse, dq_buf, dk_buf, dv_buf)


def _c_expand_matrix():
    cq = np.arange(GRID_W)[:, None]
    ck = np.arange(GRID_W)[None, :]
    d = (ck - cq + (C_WIN_COLS - 1)).reshape(-1)
    e = np.zeros((GRID_W * GRID_W, HEAD_DIM), np.float32)
    okd = (d >= 0) & (d < C_NCOL)
    e[np.arange(GRID_W * GRID_W)[okd], d[okd]] = 1.0
    return e


def _peer(p):
    return (p // 4, (p // 2) % 2, p % 2)


def _my_index():
    return 4 * lax.axis_index("x") + 2 * lax.axis_index("y") + lax.axis_index("c")


HBM_SPEC = pl.BlockSpec(memory_space=pltpu.HBM)
SEM_SPEC = pl.BlockSpec(memory_space=pltpu.SEMAPHORE)
ANY_SPEC = pl.BlockSpec(memory_space=pl.ANY)
DATAFLOW = pltpu.SideEffectType.DATAFLOW_SIDE_EFFECTING


_EXCHANGE_TRANSFERS = {"scatter": N_DEV - 1, "gather1": 4, "gather2": 3}


def _exchange_views(mode, kinds, arrays):
    nw = len(kinds)
    gather = mode != "scatter"
    if gather:
        rows = [a.shape[0] // N_DEV for a in arrays[:nw]]
    else:
        rows = [a.shape[1] // N_DEV for a in arrays[:nw]]

    def gather_slot(ref, w, who):
        return ref.at[who] if kinds[w] == "col" else ref.at[pl.ds(who * rows[w], rows[w]), :]

    x, y, c = lax.axis_index("x"), lax.axis_index("y"), lax.axis_index("c")
    me = 4 * x + 2 * y + c
    chips = [(1 - x, y), (x, 1 - y), (1 - x, 1 - y)]

    def index(px, py, pc):
        return 4 * px + 2 * py + pc

    if mode == "scatter":
        plan = [(_peer((me + off) % N_DEV), (me + off) % N_DEV, (me + N_DEV - off) % N_DEV)
                for off in range(1, N_DEV)]
    elif mode == "gather1":
        plan = [((x, y, 1 - c), me, index(x, y, 1 - c))] + [((px, py, c), me, index(px, py, c)) for px, py in chips]
    else:
        plan = [((x, y, 1 - c), index(px, py, c), index(px, py, 1 - c)) for px, py in chips]

    def src(ref, w, j):
        sent = plan[j][1]
        if gather:
            return gather_slot(ref, w, sent)
        return ref.at[sent] if kinds[w] == "col" else ref.at[0, pl.ds(sent * rows[w], rows[w]), :]

    def dst(ref, w, j):
        return gather_slot(ref, w, plan[j][1]) if gather else ref.at[me]

    def arrival(ref, w, j):
        return gather_slot(ref, w, plan[j][2]) if gather else ref.at[plan[j][2]]

    return [p[0] for p in plan], src, dst, arrival


def _place_cast(w, layer, kind, *, name):
    _, R, C = w.shape
    tr = _tile(R, (256, 128, 64, 32, 16))

    def body(w_ref, o_ref):
        o_ref[...] = w_ref[...].astype(BF16)

    if kind == "col":
        out_shape = jax.ShapeDtypeStruct((N_DEV, R, C), BF16)
        out_spec = pl.BlockSpec((None, tr, C), lambda t: (_my_index(), t, 0))
    else:
        out_shape = jax.ShapeDtypeStruct((N_DEV * R, C), BF16)
        out_spec = pl.BlockSpec((tr, C), lambda t: (_my_index() * (R // tr) + t, 0))
    return pl.pallas_call(
        body, name=name, grid=(R // tr,), in_specs=[pl.BlockSpec((None, tr, C), lambda t: (layer, t, 0))],
        out_specs=out_spec, out_shape=out_shape, compiler_params=_params(("parallel",)),
    )(w)


def _exchange_start(mode, srcs, lands, kinds, after, *, name):
    nw = len(lands)
    ns = len(srcs)
    arrays = list(srcs) + list(lands)
    na = len(arrays)
    nx = _EXCHANGE_TRANSFERS[mode]

    def body(*refs):
        l_refs = refs[ns:ns + nw]
        s_refs = refs[:ns] if ns else l_refs
        send_sems, recv_sems = refs[ns + nw + 1], refs[ns + nw + 2]
        token = refs[-1]
        peers, src, dst, _ = _exchange_views(mode, kinds, arrays)
        for j in range(nx):
            for w in range(nw):
                pltpu.make_async_remote_copy(src(s_refs[w], w, j), dst(l_refs[w], w, j),
                                             send_sems.at[w * nx + j], recv_sems.at[w * nx + j],
                                             device_id=peers[j], device_id_type=MESH).start()
        token[...] = jnp.zeros_like(token)

    outs = pl.pallas_call(
        body, name=name,
        out_shape=(pltpu.SemaphoreType.DMA((nw * nx,)), pltpu.SemaphoreType.DMA((nw * nx,)),
                   *[pltpu.HBM(a.shape, a.dtype) for a in arrays], jax.ShapeDtypeStruct((8, 128), F32)),
        in_specs=[HBM_SPEC] * na + [ANY_SPEC],
        out_specs=(SEM_SPEC, SEM_SPEC, *([HBM_SPEC] * na), pl.BlockSpec(memory_space=pltpu.VMEM)),
        input_output_aliases={k: 2 + k for k in range(na)},
        compiler_params=pltpu.CompilerParams(has_side_effects=DATAFLOW),
    )(*[pltpu.with_memory_space_constraint(a, pltpu.HBM) for a in arrays], after)
    return outs[0], outs[1], outs[2:2 + ns], outs[2 + ns:2 + na], outs[-1]


def _exchange_wait(mode, started, kinds, after, *, name):
    send_sems, recv_sems, srcs, lands, _ = started
    nw = len(lands)
    ns = len(srcs)
    arrays = list(srcs) + list(lands)
    na = len(arrays)
    nx = _EXCHANGE_TRANSFERS[mode]

    def body(*refs):
        l_refs = refs[ns:na]
        s_refs = refs[:ns] if ns else l_refs
        send_ref, recv_ref = refs[na], refs[na + 1]
        peers, src, _, arrival = _exchange_views(mode, kinds, arrays)
        for j in range(nx):
            for w in range(nw):
                cp = pltpu.make_async_remote_copy(src(s_refs[w], w, j), arrival(l_refs[w], w, j),
                                                  send_ref.at[w * nx + j], recv_ref.at[w * nx + j],
                                                  device_id=peers[j], device_id_type=MESH)
                cp.wait_send()
                cp.wait_recv()

    outs = pl.pallas_call(
        body, name=name, out_shape=[pltpu.HBM(a.shape, a.dtype) for a in arrays],
        in_specs=[HBM_SPEC] * na + [SEM_SPEC, SEM_SPEC, ANY_SPEC], out_specs=[HBM_SPEC] * na,
        input_output_aliases={k: k for k in range(na)},
        compiler_params=pltpu.CompilerParams(has_side_effects=DATAFLOW),
    )(*arrays, send_sems, recv_sems, after)
    return outs[:ns], outs[ns:]


def _all_reduce_small(x):
    R = x.shape[0]

    def body(x_ref, o_ref, gath, send_sems, recv_sems):
        me = _my_index()
        gath[me] = x_ref[...]
        sends = []
        for off in range(1, N_DEV):
            to = (me + off) % N_DEV
            cp = pltpu.make_async_remote_copy(x_ref, gath.at[me], send_sems.at[off], recv_sems.at[off],
                                              device_id=_peer(to), device_id_type=MESH)
            cp.start()
            sends.append(cp)
        for off in range(1, N_DEV):
            frm = (me + N_DEV - off) % N_DEV
            pltpu.make_async_remote_copy(x_ref, gath.at[frm], send_sems.at[off], recv_sems.at[off],
                                         device_id=_peer(frm), device_id_type=MESH).wait_recv()
        for cp in sends:
            cp.wait_send()
        acc = gath[0]
        for s in range(1, N_DEV):
            acc = acc + gath[s]
        o_ref[...] = acc

    vm = pl.BlockSpec(memory_space=pltpu.VMEM)
    return pl.pallas_call(
        body, name="all_reduce_small", in_specs=[vm], out_specs=vm, out_shape=jax.ShapeDtypeStruct((R, 128), F32),
        scratch_shapes=[pltpu.VMEM((N_DEV, R, 128), F32), pltpu.SemaphoreType.DMA((N_DEV,)),
                        pltpu.SemaphoreType.DMA((N_DEV,))],
        compiler_params=pltpu.CompilerParams(has_side_effects=True),
    )(x)


def _adamw_math(w, g, m, v):
    m = ADAM_B1 * m + (1.0 - ADAM_B1) * g
    v = ADAM_B2 * v + (1.0 - ADAM_B2) * (g * g)
    m_hat = m / (1.0 - ADAM_B1 ** ADAM_STEP)
    v_hat = v / (1.0 - ADAM_B2 ** ADAM_STEP)
    delta = -ADAM_LR * (m_hat / (jnp.sqrt(v_hat) + ADAM_EPS) + ADAM_WD * w)
    return delta, m, v


def _adamw_layer(recv, own, kind, w, m, v, outs, layer, dep, *, name):
    nl, R, C = w.shape
    tr = _tile(R, (128, 64, 32, 16))

    def body(r_ref, o_ref, w_ref, m_ref, v_ref, _0, _1, _2, _3, _dep, g_out, d_out, m_out, v_out, token):
        token[...] = jnp.zeros_like(token)
        me = _my_index()
        mine = o_ref[...].astype(F32)
        g = jnp.where(me == 0, mine, r_ref[0].astype(F32))
        for s in range(1, N_DEV):
            g = g + jnp.where(me == s, mine, r_ref[s].astype(F32))
        delta, mn, vn = _adamw_math(w_ref[...], g, m_ref[...], v_ref[...])
        g_out[...] = g
        d_out[...] = delta
        m_out[...] = mn
        v_out[...] = vn

    if kind == "col":
        own_spec = pl.BlockSpec((None, tr, C), lambda t: (_my_index(), t, 0))
    else:
        own_spec = pl.BlockSpec((None, tr, C), lambda t: (0, _my_index() * (R // tr) + t, 0))
    wspec = pl.BlockSpec((None, tr, C), lambda t: (layer, t, 0))
    res = pl.pallas_call(
        body, name=name, grid=(R // tr,),
        in_specs=[pl.BlockSpec((N_DEV, tr, C), lambda t: (0, t, 0)), own_spec] + [wspec] * 3 + [ANY_SPEC] * 5,
        out_specs=[wspec] * 4 + [pl.BlockSpec((8, 128), lambda t: (0, 0))],
        out_shape=[jax.ShapeDtypeStruct((nl, R, C), F32)] * 4 + [jax.ShapeDtypeStruct((8, 128), F32)],
        input_output_aliases={5: 0, 6: 1, 7: 2, 8: 3},
        compiler_params=_params(("arbitrary",)),
    )(recv, own, w, m, v, *outs, dep)
    return res[:4], res[4]


def _adamw_small(g, w, m, v):
    def body(g_ref, w_ref, m_ref, v_ref, d_out, m_out, v_out):
        delta, mn, vn = _adamw_math(w_ref[...], g_ref[...], m_ref[...], v_ref[...])
        d_out[...] = delta
        m_out[...] = mn
        v_out[...] = vn

    return pl.pallas_call(body, name="adamw_small", out_shape=[jax.ShapeDtypeStruct(g.shape, F32)] * 3)(g, w, m, v)


def _pack(arrays, rows):
    flat = jnp.concatenate([a.reshape(-1) for a in arrays])
    return jnp.pad(flat, (0, rows * 128 - flat.shape[0])).reshape(rows, 128)


def _unpack(packed, shapes):
    flat = packed.reshape(-1)
    out, pos = [], 0
    for s in shapes:
        size = int(np.prod(s))
        out.append(flat[pos:pos + size].reshape(s))
        pos += size
    return out


def kernel(x, norm1_g, w_in, qk_norm_g, sink_a, rpb_c, w_br_a, w_br_b, w_br_c, w_o, norm2_g, w_gate_up, w_down, loss_target, m_norm1_g, m_w_in, m_qk_norm_g, m_sink_a, m_rpb_c, m_w_br_a, m_w_br_b, m_w_br_c, m_w_o, m_norm2_g, m_w_gate_up, m_w_down, v_norm1_g, v_w_in, v_qk_norm_g, v_sink_a, v_rpb_c, v_w_br_a, v_w_br_b, v_w_br_c, v_w_o, v_norm2_g, v_w_gate_up, v_w_down):
    nl = w_in.shape[0]
    L, D = x.shape[1], x.shape[2]
    x0 = x.reshape(L, D)
    tgt = loss_target.reshape(L, D)

    big = [w_in, w_br_a, w_br_b, w_br_c, w_o, w_gate_up, w_down]
    kinds = ["col", "col", "col", "col", "row", "col", "row"]

    big_names = ["w_in", "w_br_a", "w_br_b", "w_br_c", "w_o", "w_gate_up", "w_down"]
    ALL = list(range(len(big)))
    REST = ALL[1:]

    def gather_place(i):
        return [_place_cast(w, i, k, name="gather_place_" + n) for w, k, n in zip(big, kinds, big_names)]

    def gather_start(mode, lands, sub, after, tag):
        return _exchange_start(mode, [], lands, [kinds[j] for j in sub], after, name=mode + "_start" + tag)

    def gather_wait(mode, started, sub, after, tag):
        return _exchange_wait(mode, started, [kinds[j] for j in sub], after, name=mode + "_wait" + tag)[1]

    def matmul_views(lands, sub):
        return [g.reshape((N_DEV, 1) + g.shape[1:]) if kinds[j] == "col" else g.reshape((1, 1) + g.shape)
                for g, j in zip(lands, sub)]

    half = HEAD_DIM // 2
    inv_freq = ROPE_THETA ** (-jnp.arange(half, dtype=F32) * 2.0 / HEAD_DIM)
    ang = jnp.arange(L, dtype=F32)[:, None] * inv_freq[None, :]
    cos = jnp.concatenate([jnp.cos(ang), jnp.cos(ang)], axis=-1)
    sin = jnp.concatenate([-jnp.sin(ang), jnp.sin(ang)], axis=-1)
    expand = jnp.asarray(_c_expand_matrix(), BF16)
    expand_t = jnp.asarray(_c_expand_matrix().T, BF16)

    def gain_tables(i):
        g = qk_norm_g[i]
        gq = jnp.concatenate([jnp.tile(g[0][None], (8, 1)), jnp.tile(g[2][None], (12, 1)), jnp.tile(g[4][None], (8, 1))])
        gk = jnp.concatenate([jnp.tile(g[1][None], (2, 1)), jnp.tile(g[3][None], (12, 1)), jnp.tile(g[5][None], (8, 1))])
        return (gq.reshape(NQ_CHUNKS // Q_CG, 1, Q_CG * HEAD_DIM), gk.reshape(NK_CHUNKS // K_CG, 1, K_CG * HEAD_DIM))

    def bias_table(i):
        rp = jnp.pad(rpb_c[i].reshape(C_HEADS * C_NREL, C_NCOL), ((0, 0), (0, HEAD_DIM - C_NCOL)))
        t = _exact_mm(rp, expand_t, name="c_bias_expand")
        return _c_bias_tiles(t.reshape(C_HEADS, C_NREL, GRID_W, GRID_W))

    def sink_table(i):
        return jnp.broadcast_to(sink_a[i][:, None, None], (A_Q_HEADS, 1, HEAD_DIM))

    saved = []
    gws = [None] * nl
    xi = x0
    lands0 = gather_place(0)
    lvl1 = gather_start("gather1", lands0[:1], [0], x0, "_first")
    lvl2 = gather_start("gather2", gather_wait("gather1", lvl1, [0], x0, "_first"), [0], x0, "_first")
    gws[0] = matmul_views(gather_wait("gather2", lvl2, [0], x0, "_first"), [0])
    rest1 = gather_start("gather1", lands0[1:], REST, gws[0][0], "_rest")
    dep = rest1[4]
    for i in range(nl):
        qk_dep = None
        if i >= 1 and i + 1 < nl:
            nxt1 = gather_start("gather1", gather_place(i + 1), ALL, dep, "")
            dep = nxt1[4]
        gw_in = gws[i][0]
        gq, gk = gain_tables(i)
        bias_t = bias_table(i)
        sink = sink_table(i)
        h1 = _rms_fwd(xi, norm1_g[i][None], dep, name="rms1_fwd")
        proj = _mm_nn(h1, gw_in, 0, out_dtype=F32, name="proj_fwd")
        if i == 0:
            rest2 = gather_start("gather2", gather_wait("gather1", rest1, REST, proj, "_rest"), REST, proj, "_rest")
            qk_dep = rest2[4]
            if nl > 1:
                nxt1 = gather_start("gather1", gather_place(1), ALL, rest2[4], "")
                qk_dep = nxt1[4]
        qn = _qk_fwd(proj, gq, cos, sin, Q_PIECES, NQ_CHUNKS, Q_ROPE_UPTO, Q_CG, name="qnorm_fwd", dep=qk_dep)
        kn = _qk_fwd(proj, gk, cos, sin, K_PIECES, NK_CHUNKS, K_ROPE_UPTO, K_CG, name="knorm_fwd")
        oa, lse_a = _band_fwd(qn, kn, proj, dil=1, radius=A_RADIUS, nkv=A_KV_HEADS, group=A_GROUP,
                              q0=0, k0=0, v0=PC_VA, sink=sink, name="attn_a_fwd")
        obs, lbs = [], []
        for g, (window, dil) in enumerate(B_PATTERNS):
            o_g, l_g = _band_fwd(qn, kn, proj, dil=dil, radius=window // (2 * dil), nkv=B_HG, group=1,
                                 q0=8 + g * B_HG, k0=2 + g * B_HG, v0=PC_VB + g * B_HG, name=f"attn_b{g}_fwd")
            obs.append(o_g)
            lbs.append(l_g)
        ob, lse_b = _combine_b(obs, lbs, name="attn_b_combine")
        oc, lse_c = _c_fwd(qn, kn, proj, bias_t, name="attn_c_fwd")
        if i == 0:
            gws[0] = gws[0] + matmul_views(gather_wait("gather2", rest2, REST, oc, "_rest"), REST)
        _, gw_a, gw_b, gw_c, gw_o, gw_gu, gw_d = gws[i]
        ta = _mm_nn(oa, gw_a, 0, out_dtype=BF16, name="br_a_fwd")
        tb = _mm_nn(ob, gw_b, 0, out_dtype=BF16, name="br_b_fwd")
        tc = _mm_nn(oc, gw_c, 0, out_dtype=BF16, name="br_c_fwd")
        merged = _gate_fwd(proj, ta, tb, tc, name="gate_fwd")
        x1 = _mm_nn(merged, gw_o, 0, out_dtype=F32, name="wo_fwd", res=xi)
        dep = x1
        if i + 1 < nl:
            nxt2 = gather_start("gather2", gather_wait("gather1", nxt1, ALL, x1, ""), ALL, x1, "")
            dep = nxt2[4]
        h2 = _rms_fwd(x1, norm2_g[i][None], dep, name="rms2_fwd")
        gu = _mm_nn(h2, gw_gu, 0, out_dtype=BF16, name="gate_up_fwd")
        act = _swiglu_fwd(gu, name="swiglu_fwd")
        x2 = _mm_nn(act, gw_d, 0, out_dtype=F32, name="down_fwd", res=x1)
        saved.append(dict(x=xi, h1=h1, proj=proj, qn=qn, kn=kn, oa=oa, lse_a=lse_a, ob=ob, lse_b=lse_b, oc=oc,
                          lse_c=lse_c, ta=ta, tb=tb, tc=tc, merged=merged, x1=x1, h2=h2, gu=gu, act=act,
                          gq=gq, gk=gk, bias_t=bias_t, sink=sink))
        xi = x2
        dep = x2
        if i + 1 < nl:
            gws[i + 1] = matmul_views(gather_wait("gather2", nxt2, ALL, x2, ""), ALL)

    dx, dxb, loss_row = _loss(xi, tgt, name="loss")

    def scatter_start(grads, sub, after, tag):
        lands = []
        for g, j in zip(grads, sub):
            shape = g.shape if kinds[j] == "col" else (N_DEV, g.shape[1] // N_DEV, g.shape[2])
            lands.append(lax.empty(shape, BF16))
        return _exchange_start("scatter", grads, lands, [kinds[j] for j in sub], after, name="scatter_start" + tag)

    def scatter_wait(pair, after):
        own_a, recv_a = _exchange_wait("scatter", pair[0], [kinds[0]], after, name="scatter_wait_in")
        own_b, recv_b = _exchange_wait("scatter", pair[1], [kinds[j] for j in REST], after, name="scatter_wait_rest")
        return list(recv_a) + list(recv_b), list(own_a) + list(own_b)

    small_grads = [None] * nl
    recv = [None] * nl
    own = [None] * nl
    pending = None
    for i in reversed(range(nl)):
        s = saved[i]
        gw_in, gw_a, gw_b, gw_c, gw_o, gw_gu, gw_d = gws[i]
        dact = _mm_nt(dxb, gw_d, 0, out_dtype=BF16, name="down_bwd_x", dep=None if pending is None else pending[0][4])
        g_down = _mm_tn(s["act"], dxb, 1, name="down_bwd_w")
        dgu = _swiglu_bwd(s["gu"], dact, name="swiglu_bwd")
        g_gu = _mm_tn(s["h2"], dgu, N_DEV, name="gate_up_bwd_w")
        dh2 = _mm_nt(dgu, gw_gu, 0, out_dtype=F32, name="gate_up_bwd_x")
        dx1, dx1b, dg2 = _rms_bwd(s["x1"], norm2_g[i][None], dh2, dx, name="rms2_bwd")
        dmerged = _mm_nt(dx1b, gw_o, 0, out_dtype=F32, name="wo_bwd_x")
        g_o = _mm_tn(s["merged"], dx1b, 1, name="wo_bwd_w")
        dta, dtb, dtc, dproj = _gate_bwd(s["proj"], s["ta"], s["tb"], s["tc"], dmerged, name="gate_bwd")
        g_a = _mm_tn(s["oa"], dta, N_DEV, name="br_a_bwd_w")
        g_b = _mm_tn(s["ob"], dtb, N_DEV, name="br_b_bwd_w")
        g_c = _mm_tn(s["oc"], dtc, N_DEV, name="br_c_bwd_w")
        rest = scatter_start([g_a, g_b, g_c, g_o, g_gu, g_down], REST, g_c, "_rest")
        doa = _mm_nt(dta, gw_a, 0, out_dtype=F32, name="br_a_bwd_x", dep=rest[4])
        dob = _mm_nt(dtb, gw_b, 0, out_dtype=F32, name="br_b_bwd_x")
        doc = _mm_nt(dtc, gw_c, 0, out_dtype=F32, name="br_c_bwd_x")
        dq_buf = lax.empty((L, NQ_CHUNKS * HEAD_DIM), F32)
        dk_buf = lax.empty((L, NK_CHUNKS * HEAD_DIM), F32)
        dv_buf = lax.empty((L, NK_CHUNKS * HEAD_DIM), F32)
        dq_buf, dk_buf, dv_buf, dsink = _band_bwd(
            s["qn"], s["kn"], s["proj"], doa, s["oa"], s["lse_a"], dq_buf, dk_buf, dv_buf, dil=1, radius=A_RADIUS,
            nkv=A_KV_HEADS, group=A_GROUP, q0=0, k0=0, v0=PC_VA, o0=0, sink=s["sink"], name="attn_a_bwd")
        for g, (window, dil) in enumerate(B_PATTERNS):
            dq_buf, dk_buf, dv_buf = _band_bwd(
                s["qn"], s["kn"], s["proj"], dob, s["ob"], s["lse_b"], dq_buf, dk_buf, dv_buf, dil=dil,
                radius=window // (2 * dil), nkv=B_HG, group=1, q0=8 + g * B_HG, k0=2 + g * B_HG,
                v0=PC_VB + g * B_HG, o0=0, name=f"attn_b{g}_bwd")
        dq_buf, dk_buf, dv_buf, dbias_t = _c_bwd(s["qn"], s["kn"], s["proj"], s["bias_t"], doc, s["oc"], s["lse_c"],
                                                 dq_buf, dk_buf, dv_buf, name="attn_c_bwd")
        dproj, dgq = _qk_bwd(dq_buf, s["proj"], s["gq"], cos, sin, dproj, Q_PIECES, NQ_CHUNKS, Q_ROPE_UPTO, Q_CG,
                             name="qnorm_bwd")
        dproj, dgk = _qk_bwd(dk_buf, s["proj"], s["gk"], cos, sin, dproj, K_PIECES, NK_CHUNKS, K_ROPE_UPTO, K_CG,
                             name="knorm_bwd")
        dproj = _v_bwd(dv_buf, dproj, name="v_bwd")
        g_in = _mm_tn(s["h1"], dproj, N_DEV, name="proj_bwd_w")
        dh1 = _mm_nt(dproj, gw_in, 0, out_dtype=F32, name="proj_bwd_x")
        dx, dxb, dg1 = _rms_bwd(s["x"], norm1_g[i][None], dh1, dx1, name="rms1_bwd")
        if pending is not None:
            recv[i + 1], own[i + 1] = scatter_wait(pending, dx)
        pending = (scatter_start([g_in], [0], dx, "_in"), rest)

        drpb = _exact_mm(dbias_t.reshape(C_HEADS * C_NREL, GRID_W * GRID_W), expand, name="c_bias_reduce")
        dgq, dgk = dgq.reshape(NQ_CHUNKS, HEAD_DIM), dgk.reshape(NK_CHUNKS, HEAD_DIM)
        dqk_g = jnp.stack([dgq[0:8].sum(0), dgk[0:2].sum(0), dgq[8:20].sum(0), dgk[2:14].sum(0),
                           dgq[20:28].sum(0), dgk[14:22].sum(0)])
        small_grads[i] = (dg1.reshape(D), dqk_g, dsink[:, 0, 0],
                          drpb[:, :C_NCOL].reshape(C_HEADS, C_NREL, C_NCOL), dg2.reshape(D))

    small_names = [norm1_g, qk_norm_g, sink_a, rpb_c, norm2_g]
    small_m = [m_norm1_g, m_qk_norm_g, m_sink_a, m_rpb_c, m_norm2_g]
    small_v = [v_norm1_g, v_qk_norm_g, v_sink_a, v_rpb_c, v_norm2_g]
    shapes = [a.shape for a in small_names]
    total = sum(int(np.prod(sh)) for sh in shapes) + 128
    rows = -(-total // 1024) * 8
    stacked = [jnp.stack([small_grads[i][j] for i in range(nl)]) for j in range(5)]
    packed = _pack([loss_row.reshape(-1)] + stacked, rows)
    summed = _all_reduce_small(packed)
    loss = summed[0, 0]
    zero_row = jnp.zeros((128,), F32)
    d_s, m_s, v_s = _adamw_small(summed, _pack([zero_row] + small_names, rows), _pack([zero_row] + small_m, rows),
                                 _pack([zero_row] + small_v, rows))
    shapes1 = [(128,)] + shapes
    g_small = _unpack(summed, shapes1)[1:]
    d_small = _unpack(d_s, shapes1)[1:]
    m_small = _unpack(m_s, shapes1)[1:]
    v_small = _unpack(v_s, shapes1)[1:]

    big_m = [m_w_in, m_w_br_a, m_w_br_b, m_w_br_c, m_w_o, m_w_gate_up, m_w_down]
    big_v = [v_w_in, v_w_br_a, v_w_br_b, v_w_br_c, v_w_o, v_w_gate_up, v_w_down]
    big_out = [[lax.empty(w.shape, F32) for _ in range(4)] for w in big]
    token = pending[0][4]
    for i in list(range(nl - 1, 0, -1)) + [0]:
        if i == 0:
            recv[0], own[0] = scatter_wait(pending, token)
        for j in range(len(big)):
            big_out[j], token = _adamw_layer(recv[i][j], own[i][j], kinds[j], big[j], big_m[j], big_v[j],
                                             big_out[j], i, token, name="adamw_" + big_names[j])

    order = ["norm1_g", "w_in", "qk_norm_g", "sink_a", "rpb_c", "w_br_a", "w_br_b", "w_br_c", "w_o", "norm2_g",
             "w_gate_up", "w_down"]
    small_idx = {"norm1_g": 0, "qk_norm_g": 1, "sink_a": 2, "rpb_c": 3, "norm2_g": 4}
    big_idx = {n: j for j, n in enumerate(big_names)}

    def pick(kind):
        out = []
        for n in order:
            if n in small_idx:
                out.append([g_small, d_small, m_small, v_small][kind][small_idx[n]])
            else:
                out.append(big_out[big_idx[n]][kind])
        return out

    return (loss, dx.reshape(1, L, D), *pick(0), *pick(1), *pick(2), *pick(3))
```

```python
import functools
import math

import numpy as np
import jax
import jax.numpy as jnp
from jax import lax
from jax.experimental import pallas as pl
from jax.experimental.pallas import tpu as pltpu

F32 = jnp.float32
BF16 = jnp.bfloat16
MESH = pl.DeviceIdType.MESH
N_DEV = 8

HEAD_DIM = 128
NORM_EPS = 1e-6
ROPE_THETA = 10000.0
ATT_SCALE = HEAD_DIM ** -0.5
NEG = -1e30

A_Q_HEADS, A_KV_HEADS, A_RADIUS = 8, 2, 128
A_GROUP = A_Q_HEADS // A_KV_HEADS
B_PATTERNS = ((128, 1), (512, 4), (2048, 16))
B_HG = 4
B_HEADS = len(B_PATTERNS) * B_HG
C_HEADS, GRID_W, C_WIN_ROWS, C_WIN_COLS = 8, 64, 8, 16
C_NREL = 2 * C_WIN_ROWS - 1
C_NCOL = 2 * C_WIN_COLS - 1

PC_QA, PC_KA, PC_VA = 0, 8, 10
PC_QB, PC_KB, PC_VB = 12, 24, 36
PC_QC, PC_KC, PC_VC = 48, 56, 64
N_QKV_CHUNKS = 72
Q_PIECES = ((0, 8, PC_QA), (8, 12, PC_QB), (20, 8, PC_QC))
K_PIECES = ((0, 2, PC_KA), (2, 12, PC_KB), (14, 8, PC_KC))
V_PIECES = ((0, 2, PC_VA), (2, 12, PC_VB), (14, 8, PC_VC))
NQ_CHUNKS, NK_CHUNKS = 28, 22
Q_ROPE_UPTO, K_ROPE_UPTO = 20, 14
Q_CG, K_CG = 4, 2

ADAM_LR, ADAM_B1, ADAM_B2, ADAM_EPS, ADAM_WD, ADAM_STEP = 0.001, 0.9, 0.999, 1e-08, 0.01, 10

VMEM_LIMIT = 48 * 1024 * 1024


def _tile(dim, prefs):
    for p in prefs:
        if dim % p == 0:
            return p
    return dim


NN_WEIGHT_TILE_BYTES = 8 * 1024 * 1024
NT_WEIGHT_TILE_BYTES = 4 * 1024 * 1024
TN_ACC_BYTES = 6 * 1024 * 1024
MAX_COL_TILE = 2048


def _col_tile(ns):
    return ns if ns <= MAX_COL_TILE else _tile(ns, (MAX_COL_TILE, 1024, 512, 256, 128))


def _params(sem, **kw):
    return pltpu.CompilerParams(dimension_semantics=sem, vmem_limit_bytes=VMEM_LIMIT, **kw)


def _piece_map(pieces):
    def f(c):
        out = c - pieces[0][0] + pieces[0][2]
        for first, _, pfirst in pieces[1:]:
            out = jnp.where(c >= first, c - first + pfirst, out)
        return out
    return f


def _mm_nn(a, w, layer, *, out_dtype, name, res=None):
    M, K = a.shape
    nb, _, Kw, ns = w.shape
    assert Kw == K
    tn = _col_tile(ns)
    tm = _tile(M, (1024, 512, 256) if tn <= 512 else (512, 256))
    tk = _tile(K, tuple(t for t in (2048, 1408, 1024, 512, 256) if t * tn * 2 <= NN_WEIGHT_TILE_BYTES))
    nj, nk = ns // tn, K // tk

    def body(*refs):
        a_ref, w_ref = refs[:2]
        r_ref = None if res is None else refs[2]
        o_ref = refs[2 if res is None else 3]
        part = jnp.dot(a_ref[...].astype(BF16), w_ref[...], preferred_element_type=F32)
        if nk == 1:
            if r_ref is not None:
                part = part + r_ref[...]
            o_ref[...] = part.astype(out_dtype)
            return
        acc_ref = refs[-1]
        k = pl.program_id(3)

        @pl.when(k == 0)
        def _():
            acc_ref[...] = part

        @pl.when(k > 0)
        def _():
            acc_ref[...] += part

        @pl.when(k == nk - 1)
        def _():
            r = acc_ref[...]
            if r_ref is not None:
                r = r + r_ref[...]
            o_ref[...] = r.astype(out_dtype)

    in_specs = [pl.BlockSpec((tm, tk), lambda i, b, j, k: (i, k)),
                pl.BlockSpec((None, None, tk, tn), lambda i, b, j, k: (b, layer, k, j))]
    args = [a, w]
    if res is not None:
        in_specs.append(pl.BlockSpec((tm, tn), lambda i, b, j, k: (i, b * nj + j)))
        args.append(res)
    return pl.pallas_call(
        body, name=name, grid=(M // tm, nb, nj, nk), in_specs=in_specs,
        out_specs=pl.BlockSpec((tm, tn), lambda i, b, j, k: (i, b * nj + j)),
        out_shape=jax.ShapeDtypeStruct((M, nb * ns), out_dtype),
        scratch_shapes=[] if nk == 1 else [pltpu.VMEM((tm, tn), F32)],
        compiler_params=_params(("parallel", "parallel", "parallel", "arbitrary")),
    )(*args)


def _mm_nt(a, w, layer, *, out_dtype, name, dep=None):
    M, N = a.shape
    nb, _, K, ns = w.shape
    assert N == nb * ns
    tm = _tile(M, (1024, 512, 256))
    tn = _col_tile(ns)
    tk = _tile(K, tuple(t for t in (1024, 512, 256) if t * tn * 2 <= NT_WEIGHT_TILE_BYTES))
    nj = ns // tn
    nred = nb * nj

    def body(*refs):
        a_ref, w_ref = refs[:2]
        o_ref, acc_ref = refs[-2:]
        s = pl.program_id(2) * nj + pl.program_id(3)
        part = lax.dot_general(a_ref[...].astype(BF16), w_ref[...], (((1,), (1,)), ((), ())),
                               preferred_element_type=F32)

        @pl.when(s == 0)
        def _():
            acc_ref[...] = part

        @pl.when(s > 0)
        def _():
            acc_ref[...] += part

        @pl.when(s == nred - 1)
        def _():
            o_ref[...] = acc_ref[...].astype(out_dtype)

    in_specs = [pl.BlockSpec((tm, tn), lambda i, kk, b, j: (i, b * nj + j)),
                pl.BlockSpec((None, None, tk, tn), lambda i, kk, b, j: (b, layer, kk, j))]
    args = [a, w]
    if dep is not None:
        in_specs.append(ANY_SPEC)
        args.append(dep)
    return pl.pallas_call(
        body, name=name, grid=(M // tm, K // tk, nb, nj), in_specs=in_specs,
        out_specs=pl.BlockSpec((tm, tk), lambda i, kk, b, j: (i, kk)),
        out_shape=jax.ShapeDtypeStruct((M, K), out_dtype),
        scratch_shapes=[pltpu.VMEM((tm, tk), F32)],
        compiler_params=_params(("parallel", "parallel", "arbitrary", "arbitrary")),
    )(*args)


def _mm_tn(a, g, nb, *, name):
    M, Ka = a.shape
    N = g.shape[1]
    ns = N // nb
    tn = _col_tile(ns)
    tka = _tile(Ka, tuple(t for t in (1024, 512, 256) if t * tn * 4 <= TN_ACC_BYTES))
    tm = _tile(M, (2048, 1024, 512, 256))
    nj, nm = ns // tn, M // tm

    def body(a_ref, g_ref, o_ref, acc_ref):
        m = pl.program_id(3)
        part = lax.dot_general(a_ref[...].astype(BF16), g_ref[...].astype(BF16), (((0,), (0,)), ((), ())),
                               preferred_element_type=F32)

        @pl.when(m == 0)
        def _():
            acc_ref[...] = part

        @pl.when(m > 0)
        def _():
            acc_ref[...] += part

        @pl.when(m == nm - 1)
        def _():
            o_ref[...] = acc_ref[...].astype(BF16)

    return pl.pallas_call(
        body, name=name, grid=(Ka // tka, nb, nj, nm),
        in_specs=[pl.BlockSpec((tm, tka), lambda ka, b, j, m: (m, ka)),
                  pl.BlockSpec((tm, tn), lambda ka, b, j, m: (m, b * nj + j))],
        out_specs=pl.BlockSpec((None, tka, tn), lambda ka, b, j, m: (b, ka, j)),
        out_shape=jax.ShapeDtypeStruct((nb, Ka, ns), BF16),
        scratch_shapes=[pltpu.VMEM((tka, tn), F32)],
        compiler_params=_params(("parallel", "parallel", "parallel", "arbitrary")),
    )(a, g)


def _exact_mm(a, e, *, name):
    R, K = a.shape
    N = e.shape[1]

    def body(a_ref, e_ref, o_ref):
        x = a_ref[...]
        hi = x.astype(BF16)
        r1 = x - hi.astype(F32)
        mid = r1.astype(BF16)
        lo = (r1 - mid.astype(F32)).astype(BF16)
        ev = e_ref[...]
        o_ref[...] = (jnp.dot(hi, ev, preferred_element_type=F32) + jnp.dot(mid, ev, preferred_element_type=F32)
                      + jnp.dot(lo, ev, preferred_element_type=F32))

    return pl.pallas_call(body, name=name, out_shape=jax.ShapeDtypeStruct((R, N), F32),
                          compiler_params=pltpu.CompilerParams(vmem_limit_bytes=VMEM_LIMIT))(a, e)


def _rms_fwd(x, g, dep, *, name):
    L, D = x.shape
    tl = _tile(L, (256, 128))

    def body(x_ref, g_ref, _dep, h_ref):
        xv = x_ref[...]
        rstd = lax.rsqrt(jnp.mean(xv * xv, axis=-1, keepdims=True) + NORM_EPS)
        h_ref[...] = (xv * rstd * g_ref[...]).astype(BF16)

    return pl.pallas_call(
        body, name=name, grid=(L // tl,),
        in_specs=[pl.BlockSpec((tl, D), lambda t: (t, 0)), pl.BlockSpec((1, D), lambda t: (0, 0)), ANY_SPEC],
        out_specs=pl.BlockSpec((tl, D), lambda t: (t, 0)),
        out_shape=jax.ShapeDtypeStruct((L, D), BF16),
        compiler_params=_params(("parallel",)),
    )(x, g, dep)


def _rms_bwd(x, g, dy, dres, *, name):
    L, D = x.shape
    tl = _tile(L, (128,))

    def body(x_ref, g_ref, dy_ref, dres_ref, dx_ref, dxb_ref, dg_ref):
        t = pl.program_id(0)
        xv = x_ref[...]
        rstd = lax.rsqrt(jnp.mean(xv * xv, axis=-1, keepdims=True) + NORM_EPS)
        xhat = xv * rstd
        dyv = dy_ref[...]
        dxhat = dyv * g_ref[...]
        c = jnp.mean(dxhat * xhat, axis=-1, keepdims=True)
        dx = dres_ref[...] + rstd * (dxhat - xhat * c)
        dx_ref[...] = dx
        dxb_ref[...] = dx.astype(BF16)
        dgp = jnp.sum(dyv * xhat, axis=0, keepdims=True)

        @pl.when(t == 0)
        def _():
            dg_ref[...] = dgp

        @pl.when(t > 0)
        def _():
            dg_ref[...] += dgp

    row = pl.BlockSpec((tl, D), lambda t: (t, 0))
    vec = pl.BlockSpec((1, D), lambda t: (0, 0))
    return pl.pallas_call(
        body, name=name, grid=(L // tl,), in_specs=[row, vec, row, row], out_specs=[row, row, vec],
        out_shape=[jax.ShapeDtypeStruct((L, D), F32), jax.ShapeDtypeStruct((L, D), BF16),
                   jax.ShapeDtypeStruct((1, D), F32)],
        compiler_params=_params(("arbitrary",)),
    )(x, g, dy, dres)


def _gate_fwd(proj, ta, tb, tc, *, name):
    L, D = ta.shape
    tl, tcw = _tile(L, (256, 128)), _tile(D, (1024, 512, 256, 128))
    off = N_QKV_CHUNKS * HEAD_DIM // tcw
    nd = D // tcw

    def body(g0, g1, g2, a_ref, b_ref, c_ref, o_ref):
        m = (jax.nn.sigmoid(g0[...]) * a_ref[...].astype(F32) + jax.nn.sigmoid(g1[...]) * b_ref[...].astype(F32)
             + jax.nn.sigmoid(g2[...]) * c_ref[...].astype(F32))
        o_ref[...] = m.astype(BF16)

    blk = pl.BlockSpec((tl, tcw), lambda t, j: (t, j))
    gl = [pl.BlockSpec((tl, tcw), functools.partial(lambda t, j, i: (t, off + i * nd + j), i=i)) for i in range(3)]
    return pl.pallas_call(
        body, name=name, grid=(L // tl, nd), in_specs=gl + [blk, blk, blk], out_specs=blk,
        out_shape=jax.ShapeDtypeStruct((L, D), BF16),
        compiler_params=_params(("parallel", "parallel")),
    )(proj, proj, proj, ta, tb, tc)


def _gate_bwd(proj, ta, tb, tc, dmerged, *, name):
    L, D = ta.shape
    ncols = proj.shape[1]
    tl, tcw = _tile(L, (256, 128)), _tile(D, (1024, 512, 256, 128))
    off = N_QKV_CHUNKS * HEAD_DIM // tcw
    nd = D // tcw

    def body(g0, g1, g2, a_ref, b_ref, c_ref, dm_ref, da_ref, db_ref, dc_ref, dgl_ref):
        i = pl.program_id(2)
        sg = jax.nn.sigmoid(jnp.where(i == 0, g0[...], jnp.where(i == 1, g1[...], g2[...])))
        sel_t = jnp.where(i == 0, a_ref[...], jnp.where(i == 1, b_ref[...], c_ref[...])).astype(F32)
        dt = dm_ref[...] * sg
        dtb = dt.astype(BF16)

        @pl.when(i == 0)
        def _():
            da_ref[...] = dtb

        @pl.when(i == 1)
        def _():
            db_ref[...] = dtb

        @pl.when(i == 2)
        def _():
            dc_ref[...] = dtb

        dgl_ref[...] = (dt * sel_t * (1.0 - sg)).astype(BF16)

    blk = pl.BlockSpec((tl, tcw), lambda t, j, i: (t, j))
    gl = [pl.BlockSpec((tl, tcw), functools.partial(lambda t, j, i, q: (t, off + q * nd + j), q=q)) for q in range(3)]
    return pl.pallas_call(
        body, name=name, grid=(L // tl, nd, 3), in_specs=gl + [blk, blk, blk, blk],
        out_specs=[blk, blk, blk, pl.BlockSpec((tl, tcw), lambda t, j, i: (t, off + i * nd + j))],
        out_shape=[jax.ShapeDtypeStruct((L, D), BF16)] * 3 + [jax.ShapeDtypeStruct((L, ncols), BF16)],
        compiler_params=_params(("parallel", "parallel", "arbitrary")),
    )(proj, proj, proj, ta, tb, tc, dmerged)


def _swiglu_fwd(gu, *, name):
    L, F2 = gu.shape
    F = F2 // 2
    tl = _tile(L, (128, 64))

    def body(gu_ref, o_ref):
        gt = gu_ref[:, :F].astype(F32)
        o_ref[...] = (gt * jax.nn.sigmoid(gt) * gu_ref[:, F:].astype(F32)).astype(BF16)

    return pl.pallas_call(
        body, name=name, grid=(L // tl,), in_specs=[pl.BlockSpec((tl, F2), lambda t: (t, 0))],
        out_specs=pl.BlockSpec((tl, F), lambda t: (t, 0)),
        out_shape=jax.ShapeDtypeStruct((L, F), BF16),
        compiler_params=_params(("parallel",)),
    )(gu)


def _swiglu_bwd(gu, dact, *, name):
    L, F2 = gu.shape
    F = F2 // 2
    tl = _tile(L, (128, 64))

    def body(gu_ref, d_ref, o_ref):
        gt, up, d = gu_ref[:, :F].astype(F32), gu_ref[:, F:].astype(F32), d_ref[...].astype(F32)
        sg = jax.nn.sigmoid(gt)
        o_ref[:, :F] = (d * up * sg * (1.0 + gt * (1.0 - sg))).astype(BF16)
        o_ref[:, F:] = (d * gt * sg).astype(BF16)

    return pl.pallas_call(
        body, name=name, grid=(L // tl,),
        in_specs=[pl.BlockSpec((tl, F2), lambda t: (t, 0)), pl.BlockSpec((tl, F), lambda t: (t, 0))],
        out_specs=pl.BlockSpec((tl, F2), lambda t: (t, 0)),
        out_shape=jax.ShapeDtypeStruct((L, F2), BF16),
        compiler_params=_params(("parallel",)),
    )(gu, dact)


def _loss(y, tgt, *, name):
    L, D = y.shape
    tl = _tile(L, (256, 128))
    nt = L // tl

    def body(y_ref, t_ref, dy_ref, dyb_ref, loss_ref, acc_ref):
        t = pl.program_id(0)
        e = y_ref[...] - t_ref[...]
        dy = e * (1.0 / D)
        dy_ref[...] = dy
        dyb_ref[...] = dy.astype(BF16)
        part = jnp.sum(e * e, axis=0, keepdims=True)

        @pl.when(t == 0)
        def _():
            acc_ref[...] = part

        @pl.when(t > 0)
        def _():
            acc_ref[...] += part

        @pl.when(t == nt - 1)
        def _():
            loss_ref[...] = jnp.broadcast_to(jnp.sum(acc_ref[...], axis=-1, keepdims=True) * (0.5 / D), (1, 128))

    row = pl.BlockSpec((tl, D), lambda t: (t, 0))
    return pl.pallas_call(
        body, name=name, grid=(nt,), in_specs=[row, row],
        out_specs=[row, row, pl.BlockSpec((1, 128), lambda t: (0, 0))],
        out_shape=[jax.ShapeDtypeStruct((L, D), F32), jax.ShapeDtypeStruct((L, D), BF16),
                   jax.ShapeDtypeStruct((1, 128), F32)],
        scratch_shapes=[pltpu.VMEM((1, D), F32)],
        compiler_params=_params(("arbitrary",)),
    )(y, tgt)


def _rope(v, cos, sin_signed):
    return v * cos + pltpu.roll(v, HEAD_DIM // 2, 1) * sin_signed


def _head_mean(x):
    hi = x.astype(BF16)
    lo = (x - hi.astype(F32)).astype(BF16)
    ones = jnp.ones((HEAD_DIM, HEAD_DIM), BF16)
    total = jnp.dot(hi, ones, preferred_element_type=F32) + jnp.dot(lo, ones, preferred_element_type=F32)
    return total * (1.0 / HEAD_DIM)


def _qk_fwd(proj, gtab, cos, sin, pieces, nchunks, rope_upto, cg, *, name, dep=None):
    L = proj.shape[0]
    tl = _tile(L, (512, 256, 128))
    W = cg * HEAD_DIM
    pmap = _piece_map(tuple((a // cg, n // cg, p // cg) for a, n, p in pieces))

    def body(*refs):
        p_ref, g_ref, cos_ref, sin_ref = refs[:4]
        o_ref = refs[-1]
        c = pl.program_id(1)

        def norm(j):
            cols = slice(j * HEAD_DIM, (j + 1) * HEAD_DIM)
            x = p_ref[:, cols]
            rstd = lax.rsqrt(_head_mean(x * x) + NORM_EPS)
            return cols, x * rstd * g_ref[:, cols]

        @pl.when(c < rope_upto // cg)
        def _():
            for j in range(cg):
                cols, y = norm(j)
                o_ref[:, cols] = _rope(y, cos_ref[...], sin_ref[...])

        @pl.when(c >= rope_upto // cg)
        def _():
            for j in range(cg):
                cols, y = norm(j)
                o_ref[:, cols] = y

    pos = pl.BlockSpec((tl, HEAD_DIM), lambda t, c: (t, 0))
    in_specs = [pl.BlockSpec((tl, W), lambda t, c: (t, pmap(c))),
                pl.BlockSpec((None, 1, W), lambda t, c: (c, 0, 0)), pos, pos]
    args = [proj, gtab, cos, sin]
    if dep is not None:
        in_specs.append(ANY_SPEC)
        args.append(dep)
    return pl.pallas_call(
        body, name=name, grid=(L // tl, nchunks // cg), in_specs=in_specs,
        out_specs=pl.BlockSpec((tl, W), lambda t, c: (t, c)),
        out_shape=jax.ShapeDtypeStruct((L, nchunks * HEAD_DIM), F32),
        compiler_params=_params(("parallel", "parallel")),
    )(*args)


def _qk_bwd(dqk, proj, gtab, cos, sin, dproj, pieces, nchunks, rope_upto, cg, *, name):
    L = proj.shape[0]
    tl = _tile(L, (512, 256, 128))
    W = cg * HEAD_DIM
    pmap = _piece_map(tuple((a // cg, n // cg, p // cg) for a, n, p in pieces))

    def body(d_ref, p_ref, g_ref, cos_ref, sin_ref, _, o_ref, dg_ref):
        c, t = pl.program_id(0), pl.program_id(1)

        @pl.when(t == 0)
        def _():
            dg_ref[...] = jnp.zeros_like(dg_ref)

        for j in range(cg):
            cols = slice(j * HEAD_DIM, (j + 1) * HEAD_DIM)
            x = p_ref[:, cols]
            rstd = lax.rsqrt(_head_mean(x * x) + NORM_EPS)
            xhat = x * rstd
            dy = d_ref[:, cols]
            dy = jnp.where(c < rope_upto // cg, _rope(dy, cos_ref[...], -sin_ref[...]), dy)
            dxhat = dy * g_ref[:, cols]
            cm = _head_mean(dxhat * xhat)
            o_ref[:, cols] = (rstd * (dxhat - xhat * cm)).astype(BF16)
            dg_ref[:, cols] += jnp.sum(dy * xhat, axis=0, keepdims=True)

    pos = pl.BlockSpec((tl, HEAD_DIM), lambda c, t: (t, 0))
    gspec = pl.BlockSpec((None, 1, W), lambda c, t: (c, 0, 0))
    out, dg = pl.pallas_call(
        body, name=name, grid=(nchunks // cg, L // tl),
        in_specs=[pl.BlockSpec((tl, W), lambda c, t: (t, c)),
                  pl.BlockSpec((tl, W), lambda c, t: (t, pmap(c))), gspec, pos, pos,
                  pl.BlockSpec(memory_space=pl.ANY)],
        out_specs=[pl.BlockSpec((tl, W), lambda c, t: (t, pmap(c))), gspec],
        out_shape=[jax.ShapeDtypeStruct(dproj.shape, BF16), jax.ShapeDtypeStruct((nchunks // cg, 1, W), F32)],
        input_output_aliases={5: 0},
        compiler_params=_params(("parallel", "arbitrary")),
    )(dqk, proj, gtab, cos, sin, dproj)
    return out, dg


def _v_bwd(dv, dproj, *, name):
    L = dv.shape[0]
    tl = _tile(L, (512, 256, 128))
    pmap = _piece_map(tuple((a // 2, n // 2, p // 2) for a, n, p in V_PIECES))

    def body(d_ref, _, o_ref):
        o_ref[...] = d_ref[...].astype(BF16)

    return pl.pallas_call(
        body, name=name, grid=(L // tl, NK_CHUNKS // 2),
        in_specs=[pl.BlockSpec((tl, 2 * HEAD_DIM), lambda t, c: (t, c)), pl.BlockSpec(memory_space=pl.ANY)],
        out_specs=pl.BlockSpec((tl, 2 * HEAD_DIM), lambda t, c: (t, pmap(c))),
        out_shape=jax.ShapeDtypeStruct(dproj.shape, BF16),
        input_output_aliases={1: 0},
        compiler_params=_params(("parallel", "parallel")),
    )(dv, dproj)


def _band_geometry(L, dil, radius):
    n = L // dil
    bq = min(256, max(n // 2, 64), n)
    width = min(bq + 2 * radius, n)
    nsub = _tile(n // bq, (4, 2)) if dil == 1 else 1
    return n, bq, width, nsub


def _band_loop(dil, one):
    if dil == 1:
        one(0, 0)
    else:
        lax.fori_loop(0, dil, one, 0, unroll=min(dil, 4))


def _band_rows(dil, r, first, count):
    if dil == 1:
        return pl.ds(pl.multiple_of(first, 8), count)
    return pl.ds(r + first * dil, count, stride=dil)


def _band_mask(i, bq, width, radius, ws):
    qpos = i * bq + lax.broadcasted_iota(jnp.int32, (bq, width), 0)
    kpos = ws + lax.broadcasted_iota(jnp.int32, (bq, width), 1)
    return jnp.abs(kpos - qpos) <= radius


def _band_fwd(qn, kn, proj, *, dil, radius, nkv, group, q0, k0, v0, sink=None, name):
    L = qn.shape[0]
    n, bq, width, nsub = _band_geometry(L, dil, radius)
    tq = nsub * bq * dil
    nh = nkv * group

    def body(*refs):
        if sink is None:
            q_ref, k_ref, v_ref, o_ref, lse_ref = refs
        else:
            q_ref, k_ref, v_ref, s_ref, o_ref, lse_ref = refs
        for sb in range(nsub):
            block(sb, q_ref, k_ref, v_ref, None if sink is None else s_ref, o_ref, lse_ref)

    def block(sb, q_ref, k_ref, v_ref, s_ref, o_ref, lse_ref):
        i = pl.program_id(2) * nsub + sb
        ws = jnp.clip(i * bq - radius, 0, n - width)
        valid = _band_mask(i, bq, width, radius, ws)

        def one(r, carry):
            qrows = _band_rows(dil, r, sb * bq, bq)
            krows = _band_rows(dil, r, ws, width)
            q = q_ref[qrows, :].astype(BF16)
            k = k_ref[krows, :].astype(BF16)
            v = v_ref[krows, :].astype(BF16)
            s = lax.dot_general(q, k, (((1,), (1,)), ((), ())), preferred_element_type=F32) * ATT_SCALE
            s = jnp.where(valid, s, NEG)
            m = jnp.max(s, axis=-1, keepdims=True)
            if sink is not None:
                m = jnp.maximum(m, s_ref[...][:, :1])
            p = jnp.exp(s - m)
            denom = jnp.sum(p, axis=-1, keepdims=True)
            if sink is not None:
                denom = denom + jnp.exp(s_ref[...][:, :1] - m)
            pn = (p / denom).astype(BF16)
            o_ref[qrows, :] = jnp.dot(pn, v, preferred_element_type=F32)
            lse_ref[qrows, :] = jnp.broadcast_to(m + jnp.log(denom), (bq, HEAD_DIM))
            return carry

        _band_loop(dil, one)

    qspec = pl.BlockSpec((tq, HEAD_DIM), lambda hk, g, i: (i, q0 + hk * group + g))
    in_specs = [qspec,
                pl.BlockSpec((L, HEAD_DIM), lambda hk, g, i: (0, k0 + hk)),
                pl.BlockSpec((L, HEAD_DIM), lambda hk, g, i: (0, v0 + hk))]
    args = [qn, kn, proj]
    if sink is not None:
        in_specs.append(pl.BlockSpec((None, 1, HEAD_DIM), lambda hk, g, i: (hk * group + g, 0, 0)))
        args.append(sink)
    ospec = pl.BlockSpec((tq, HEAD_DIM), lambda hk, g, i: (i, hk * group + g))
    return pl.pallas_call(
        body, name=name, grid=(nkv, group, n // (bq * nsub)), in_specs=in_specs, out_specs=[ospec, ospec],
        out_shape=[jax.ShapeDtypeStruct((L, nh * HEAD_DIM), F32)] * 2,
        compiler_params=_params(("parallel", "parallel", "arbitrary")),
    )(*args)


def _band_bwd(qn, kn, proj, do, o, lse, dq_buf, dk_buf, dv_buf, *, dil, radius, nkv, group, q0, k0, v0, o0,
              sink=None, name):
    L = qn.shape[0]
    n, bq, width, nsub = _band_geometry(L, dil, radius)
    tq = nsub * bq * dil
    nh = nkv * group
    n_in = 6 + (1 if sink is not None else 0)

    def body(*refs):
        q_ref, k_ref, v_ref, do_ref, o_ref, lse_ref = refs[:6]
        s_ref = refs[6] if sink is not None else None
        outs = refs[n_in + 3:]
        dq_ref, dk_ref, dv_ref = outs[:3]
        ds_ref = outs[3] if sink is not None else None
        g, step = pl.program_id(1), pl.program_id(2)

        @pl.when((g == 0) & (step == 0))
        def _():
            dk_ref[...] = jnp.zeros_like(dk_ref)
            dv_ref[...] = jnp.zeros_like(dv_ref)

        if sink is not None:
            @pl.when(step == 0)
            def _():
                ds_ref[...] = jnp.zeros_like(ds_ref)

        for sb in range(nsub):
            block(sb, q_ref, k_ref, v_ref, do_ref, o_ref, lse_ref, s_ref, dq_ref, dk_ref, dv_ref, ds_ref)

    def block(sb, q_ref, k_ref, v_ref, do_ref, o_ref, lse_ref, s_ref, dq_ref, dk_ref, dv_ref, ds_ref):
        i = pl.program_id(2) * nsub + sb
        ws = jnp.clip(i * bq - radius, 0, n - width)
        valid = _band_mask(i, bq, width, radius, ws)

        def one(r, carry):
            qrows = _band_rows(dil, r, sb * bq, bq)
            krows = _band_rows(dil, r, ws, width)
            q = q_ref[qrows, :].astype(BF16)
            k = k_ref[krows, :].astype(BF16)
            v = v_ref[krows, :].astype(BF16)
            dov = do_ref[qrows, :]
            lse_v = lse_ref[qrows, :][:, :1]
            delta = jnp.sum(dov * o_ref[qrows, :], axis=-1, keepdims=True)
            dob = dov.astype(BF16)
            s = lax.dot_general(q, k, (((1,), (1,)), ((), ())), preferred_element_type=F32) * ATT_SCALE
            p = jnp.where(valid, jnp.exp(s - lse_v), 0.0)
            dp = lax.dot_general(dob, v, (((1,), (1,)), ((), ())), preferred_element_type=F32)
            dsb = (p * (dp - delta)).astype(BF16)
            dq_ref[qrows, :] = jnp.dot(dsb, k, preferred_element_type=F32) * ATT_SCALE
            dk_ref[krows, :] += lax.dot_general(dsb, q, (((0,), (0,)), ((), ())),
                                                preferred_element_type=F32) * ATT_SCALE
            dv_ref[krows, :] += lax.dot_general(p.astype(BF16), dob, (((0,), (0,)), ((), ())),
                                                preferred_element_type=F32)
            if sink is not None:
                ps = jnp.exp(s_ref[...][:, :1] - lse_v)
                ds_ref[...] += jnp.broadcast_to(jnp.sum(-ps * delta, axis=0, keepdims=True), (1, HEAD_DIM))
            return carry

        _band_loop(dil, one)

    hspec = pl.BlockSpec((tq, HEAD_DIM), lambda hk, g, i: (i, o0 + hk * group + g))
    qspec = pl.BlockSpec((tq, HEAD_DIM), lambda hk, g, i: (i, q0 + hk * group + g))
    kspec = pl.BlockSpec((L, HEAD_DIM), lambda hk, g, i: (0, k0 + hk))
    any_spec = pl.BlockSpec(memory_space=pl.ANY)
    in_specs = [qspec, kspec, pl.BlockSpec((L, HEAD_DIM), lambda hk, g, i: (0, v0 + hk)), hspec, hspec, hspec]
    args = [qn, kn, proj, do, o, lse]
    if sink is not None:
        in_specs.append(pl.BlockSpec((None, 1, HEAD_DIM), lambda hk, g, i: (hk * group + g, 0, 0)))
        args.append(sink)
    in_specs += [any_spec] * 3
    args += [dq_buf, dk_buf, dv_buf]
    out_specs = [qspec, kspec, kspec]
    out_shape = [jax.ShapeDtypeStruct(dq_buf.shape, F32), jax.ShapeDtypeStruct(dk_buf.shape, F32),
                 jax.ShapeDtypeStruct(dv_buf.shape, F32)]
    if sink is not None:
        out_specs.append(pl.BlockSpec((None, 1, HEAD_DIM), lambda hk, g, i: (hk * group + g, 0, 0)))
        out_shape.append(jax.ShapeDtypeStruct((nh, 1, HEAD_DIM), F32))
    return pl.pallas_call(
        body, name=name, grid=(nkv, group, n // (bq * nsub)), in_specs=in_specs, out_specs=out_specs,
        out_shape=out_shape,
        input_output_aliases={n_in: 0, n_in + 1: 1, n_in + 2: 2},
        compiler_params=_params(("parallel", "arbitrary", "arbitrary")),
    )(*args)


def _combine_b(os_, lses, *, name):
    L, W = os_[0].shape
    tl = _tile(L, (256, 128))

    def body(o0, o1, o2, l0, l1, l2, out_ref, lt_ref):
        a, b, c = l0[...], l1[...], l2[...]
        m = jnp.maximum(jnp.maximum(a, b), c)
        ea, eb, ec = jnp.exp(a - m), jnp.exp(b - m), jnp.exp(c - m)
        tot = ea + eb + ec
        out_ref[...] = (ea * o0[...] + eb * o1[...] + ec * o2[...]) / tot
        lt_ref[...] = m + jnp.log(tot)

    blk = pl.BlockSpec((tl, W), lambda t: (t, 0))
    return pl.pallas_call(
        body, name=name, grid=(L // tl,), in_specs=[blk] * 6, out_specs=[blk, blk],
        out_shape=[jax.ShapeDtypeStruct((L, W), F32)] * 2, compiler_params=_params(("parallel",)),
    )(*os_, *lses)


C_QROWS = 4
C_KROWS = C_QROWS + C_WIN_ROWS
C_QUERIES, C_KEYS = C_QROWS * GRID_W, C_KROWS * GRID_W
_C_KIND_OFFSETS = (C_WIN_ROWS - 1, C_WIN_ROWS - 1 - C_WIN_ROWS // 2, C_WIN_ROWS - 1 - (C_KROWS - C_QROWS))


def _c_geometry(L):
    rows = L // GRID_W
    assert rows >= C_KROWS and rows % C_QROWS == 0
    return rows


def _c_bias_tiles(bias_t):
    cq = np.arange(GRID_W)[:, None]
    ck = np.arange(GRID_W)[None, :]
    start = np.clip(cq - C_WIN_COLS // 2, 0, GRID_W - C_WIN_COLS)
    masked = jnp.where(jnp.asarray((ck >= start) & (ck < start + C_WIN_COLS)), bias_t, NEG)
    blank = jnp.full((C_HEADS, GRID_W, GRID_W), NEG, F32)
    kinds = []
    for kind in range(3):
        off = _C_KIND_OFFSETS[kind]
        row_blocks = []
        for a in range(C_QROWS):
            lo = (0, a, C_KROWS - C_WIN_ROWS)[kind]
            row_blocks.append(jnp.concatenate(
                [masked[:, b - a + off] if lo <= b < lo + C_WIN_ROWS else blank for b in range(C_KROWS)], axis=-1))
        kinds.append(jnp.concatenate(row_blocks, axis=-2))
    return jnp.stack(kinds, axis=1)


def _c_block(g, rows):
    r0 = g * C_QROWS
    k0 = jnp.clip(r0 - C_WIN_ROWS // 2, 0, rows - C_KROWS)
    kind = jnp.where(g == 0, 0, jnp.where(g == rows // C_QROWS - 1, 2, 1))
    return k0, kind, k0 - r0 + (C_WIN_ROWS - 1)


def _c_fwd(qn, kn, proj, tiles, *, name):
    L = qn.shape[0]
    rows = _c_geometry(L)

    nsub = _tile(rows // C_QROWS, (2,))

    def body(q_ref, k_ref, v_ref, t_ref, o_ref, lse_ref):
        for sb in range(nsub):
            k0, kind, _ = _c_block(pl.program_id(1) * nsub + sb, rows)
            krows = pl.ds(pl.multiple_of(k0 * GRID_W, GRID_W), C_KEYS)
            qrows = pl.ds(sb * C_QUERIES, C_QUERIES)
            q = q_ref[qrows, :].astype(BF16)
            k = k_ref[krows, :].astype(BF16)
            v = v_ref[krows, :].astype(BF16)
            s = lax.dot_general(q, k, (((1,), (1,)), ((), ())), preferred_element_type=F32) * ATT_SCALE + t_ref[kind]
            m = jnp.max(s, axis=-1, keepdims=True)
            p = jnp.exp(s - m)
            denom = jnp.sum(p, axis=-1, keepdims=True)
            o_ref[qrows, :] = jnp.dot((p / denom).astype(BF16), v, preferred_element_type=F32)
            lse_ref[qrows, :] = jnp.broadcast_to(m + jnp.log(denom), (C_QUERIES, HEAD_DIM))

    ospec = pl.BlockSpec((nsub * C_QUERIES, HEAD_DIM), lambda h, g: (g, h))
    return pl.pallas_call(
        body, name=name, grid=(C_HEADS, rows // (C_QROWS * nsub)),
        in_specs=[pl.BlockSpec((nsub * C_QUERIES, HEAD_DIM), lambda h, g: (g, 20 + h)),
                  pl.BlockSpec((L, HEAD_DIM), lambda h, g: (0, 14 + h)),
                  pl.BlockSpec((L, HEAD_DIM), lambda h, g: (0, PC_VC + h)),
                  pl.BlockSpec((None, 3, C_QUERIES, C_KEYS), lambda h, g: (h, 0, 0, 0))],
        out_specs=[ospec, ospec],
        out_shape=[jax.ShapeDtypeStruct((L, C_HEADS * HEAD_DIM), F32)] * 2,
        compiler_params=_params(("parallel", "arbitrary")),
    )(qn, kn, proj, tiles)


def _c_bwd(qn, kn, proj, tiles, do, o, lse, dq_buf, dk_buf, dv_buf, *, name):
    L = qn.shape[0]
    rows = _c_geometry(L)

    nsub = _tile(rows // C_QROWS, (2,))

    def body(q_ref, k_ref, v_ref, t_ref, do_ref, o_ref, lse_ref, _a, _b, _c, dq_ref, dk_ref, dv_ref, dt_ref):
        @pl.when(pl.program_id(1) == 0)
        def _():
            dk_ref[...] = jnp.zeros_like(dk_ref)
            dv_ref[...] = jnp.zeros_like(dv_ref)
            dt_ref[...] = jnp.zeros_like(dt_ref)

        for sb in range(nsub):
            k0, kind, off = _c_block(pl.program_id(1) * nsub + sb, rows)
            krows = pl.ds(pl.multiple_of(k0 * GRID_W, GRID_W), C_KEYS)
            qrows = pl.ds(sb * C_QUERIES, C_QUERIES)
            q = q_ref[qrows, :].astype(BF16)
            k = k_ref[krows, :].astype(BF16)
            v = v_ref[krows, :].astype(BF16)
            dov = do_ref[qrows, :]
            dob = dov.astype(BF16)
            delta = jnp.sum(dov * o_ref[qrows, :], axis=-1, keepdims=True)
            s = lax.dot_general(q, k, (((1,), (1,)), ((), ())), preferred_element_type=F32) * ATT_SCALE + t_ref[kind]
            p = jnp.exp(s - lse_ref[qrows, :][:, :1])
            dp = lax.dot_general(dob, v, (((1,), (1,)), ((), ())), preferred_element_type=F32)
            ds = p * (dp - delta)
            for a in range(C_QROWS):
                for b in range(C_KROWS):
                    rel = jnp.clip(b - a + off, 0, C_NREL - 1)
                    dt_ref[rel] += ds[a * GRID_W:(a + 1) * GRID_W, b * GRID_W:(b + 1) * GRID_W]
            dsb = ds.astype(BF16)
            dq_ref[qrows, :] = jnp.dot(dsb, k, preferred_element_type=F32) * ATT_SCALE
            dk_ref[krows, :] += lax.dot_general(dsb, q, (((0,), (0,)), ((), ())),
                                                preferred_element_type=F32) * ATT_SCALE
            dv_ref[krows, :] += lax.dot_general(p.astype(BF16), dob, (((0,), (0,)), ((), ())),
                                                preferred_element_type=F32)

    hspec = pl.BlockSpec((nsub * C_QUERIES, HEAD_DIM), lambda h, g: (g, h))
    qspec = pl.BlockSpec((nsub * C_QUERIES, HEAD_DIM), lambda h, g: (g, 20 + h))
    kspec = pl.BlockSpec((L, HEAD_DIM), lambda h, g: (0, 14 + h))
    any_spec = pl.BlockSpec(memory_space=pl.ANY)
    return pl.pallas_call(
        body, name=name, grid=(C_HEADS, rows // (C_QROWS * nsub)),
        in_specs=[qspec, kspec, pl.BlockSpec((L, HEAD_DIM), lambda h, g: (0, PC_VC + h)),
                  pl.BlockSpec((None, 3, C_QUERIES, C_KEYS), lambda h, g: (h, 0, 0, 0)),
                  hspec, hspec, hspec, any_spec, any_spec, any_spec],
        out_specs=[qspec, kspec, kspec,
                   pl.BlockSpec((None, C_NREL, GRID_W, GRID_W), lambda h, r: (h, 0, 0, 0))],
        out_shape=[jax.ShapeDtypeStruct(dq_buf.shape, F32), jax.ShapeDtypeStruct(dk_buf.shape, F32),
                   jax.ShapeDtypeStruct(dv_buf.shape, F32),
                   jax.ShapeDtypeStruct((C_HEADS, C_NREL, GRID_W, GRID_W), F32)],
        input_output_aliases={7: 0, 8: 1, 9: 2},
        compiler_params=_params(("parallel", "arbitrary")),
    )(qn, kn, proj, tiles, do, o, lse, dq_buf, dk_buf, dv_buf)


def _c_expand_matrix():
    cq = np.arange(GRID_W)[:, None]
    ck = np.arange(GRID_W)[None, :]
    d = (ck - cq + (C_WIN_COLS - 1)).reshape(-1)
    e = np.zeros((GRID_W * GRID_W, HEAD_DIM), np.float32)
    okd = (d >= 0) & (d < C_NCOL)
    e[np.arange(GRID_W * GRID_W)[okd], d[okd]] = 1.0
    return e


def _peer(p):
    return (p // 4, (p // 2) % 2, p % 2)


def _my_index():
    return 4 * lax.axis_index("x") + 2 * lax.axis_index("y") + lax.axis_index("c")


HBM_SPEC = pl.BlockSpec(memory_space=pltpu.HBM)
SEM_SPEC = pl.BlockSpec(memory_space=pltpu.SEMAPHORE)
ANY_SPEC = pl.BlockSpec(memory_space=pl.ANY)
DATAFLOW = pltpu.SideEffectType.DATAFLOW_SIDE_EFFECTING


_EXCHANGE_TRANSFERS = {"scatter": N_DEV - 1, "gather1": 4, "gather2": 3}


def _exchange_views(mode, kinds, arrays):
    nw = len(kinds)
    gather = mode != "scatter"
    if gather:
        rows = [a.shape[0] // N_DEV for a in arrays[:nw]]
    else:
        rows = [a.shape[1] // N_DEV for a in arrays[:nw]]

    def gather_slot(ref, w, who):
        return ref.at[who] if kinds[w] == "col" else ref.at[pl.ds(who * rows[w], rows[w]), :]

    x, y, c = lax.axis_index("x"), lax.axis_index("y"), lax.axis_index("c")
    me = 4 * x + 2 * y + c
    chips = [(1 - x, y), (x, 1 - y), (1 - x, 1 - y)]

    def index(px, py, pc):
        return 4 * px + 2 * py + pc

    if mode == "scatter":
        plan = [(_peer((me + off) % N_DEV), (me + off) % N_DEV, (me + N_DEV - off) % N_DEV)
                for off in range(1, N_DEV)]
    elif mode == "gather1":
        plan = [((x, y, 1 - c), me, index(x, y, 1 - c))] + [((px, py, c), me, index(px, py, c)) for px, py in chips]
    else:
        plan = [((x, y, 1 - c), index(px, py, c), index(px, py, 1 - c)) for px, py in chips]

    def src(ref, w, j):
        sent = plan[j][1]
        if gather:
            return gather_slot(ref, w, sent)
        return ref.at[sent] if kinds[w] == "col" else ref.at[0, pl.ds(sent * rows[w], rows[w]), :]

    def dst(ref, w, j):
        return gather_slot(ref, w, plan[j][1]) if gather else ref.at[me]

    def arrival(ref, w, j):
        return gather_slot(ref, w, plan[j][2]) if gather else ref.at[plan[j][2]]

    return [p[0] for p in plan], src, dst, arrival


def _place_cast(w, layer, kind, *, name):
    _, R, C = w.shape
    tr = _tile(R, (256, 128, 64, 32, 16))

    def body(w_ref, o_ref):
        o_ref[...] = w_ref[...].astype(BF16)

    if kind == "col":
        out_shape = jax.ShapeDtypeStruct((N_DEV, R, C), BF16)
        out_spec = pl.BlockSpec((None, tr, C), lambda t: (_my_index(), t, 0))
    else:
        out_shape = jax.ShapeDtypeStruct((N_DEV * R, C), BF16)
        out_spec = pl.BlockSpec((tr, C), lambda t: (_my_index() * (R // tr) + t, 0))
    return pl.pallas_call(
        body, name=name, grid=(R // tr,), in_specs=[pl.BlockSpec((None, tr, C), lambda t: (layer, t, 0))],
        out_specs=out_spec, out_shape=out_shape, compiler_params=_params(("parallel",)),
    )(w)


def _exchange_start(mode, srcs, lands, kinds, after, *, name):
    nw = len(lands)
    ns = len(srcs)
    arrays = list(srcs) + list(lands)
    na = len(arrays)
    nx = _EXCHANGE_TRANSFERS[mode]

    def body(*refs):
        l_refs = refs[ns:ns + nw]
        s_refs = refs[:ns] if ns else l_refs
        send_sems, recv_sems = refs[ns + nw + 1], refs[ns + nw + 2]
        token = refs[-1]
        peers, src, dst, _ = _exchange_views(mode, kinds, arrays)
        for j in range(nx):
            for w in range(nw):
                pltpu.make_async_remote_copy(src(s_refs[w], w, j), dst(l_refs[w], w, j),
                                             send_sems.at[w * nx + j], recv_sems.at[w * nx + j],
                                             device_id=peers[j], device_id_type=MESH).start()
        token[...] = jnp.zeros_like(token)

    outs = pl.pallas_call(
        body, name=name,
        out_shape=(pltpu.SemaphoreType.DMA((nw * nx,)), pltpu.SemaphoreType.DMA((nw * nx,)),
                   *[pltpu.HBM(a.shape, a.dtype) for a in arrays], jax.ShapeDtypeStruct((8, 128), F32)),
        in_specs=[HBM_SPEC] * na + [ANY_SPEC],
        out_specs=(SEM_SPEC, SEM_SPEC, *([HBM_SPEC] * na), pl.BlockSpec(memory_space=pltpu.VMEM)),
        input_output_aliases={k: 2 + k for k in range(na)},
        compiler_params=pltpu.CompilerParams(has_side_effects=DATAFLOW),
    )(*[pltpu.with_memory_space_constraint(a, pltpu.HBM) for a in arrays], after)
    return outs[0], outs[1], outs[2:2 + ns], outs[2 + ns:2 + na], outs[-1]


def _exchange_wait(mode, started, kinds, after, *, name):
    send_sems, recv_sems, srcs, lands, _ = started
    nw = len(lands)
    ns = len(srcs)
    arrays = list(srcs) + list(lands)
    na = len(arrays)
    nx = _EXCHANGE_TRANSFERS[mode]

    def body(*refs):
        l_refs = refs[ns:na]
        s_refs = refs[:ns] if ns else l_refs
        send_ref, recv_ref = refs[na], refs[na + 1]
        peers, src, _, arrival = _exchange_views(mode, kinds, arrays)
        for j in range(nx):
            for w in range(nw):
                cp = pltpu.make_async_remote_copy(src(s_refs[w], w, j), arrival(l_refs[w], w, j),
                                                  send_ref.at[w * nx + j], recv_ref.at[w * nx + j],
                                                  device_id=peers[j], device_id_type=MESH)
                cp.wait_send()
                cp.wait_recv()

    outs = pl.pallas_call(
        body, name=name, out_shape=[pltpu.HBM(a.shape, a.dtype) for a in arrays],
        in_specs=[HBM_SPEC] * na + [SEM_SPEC, SEM_SPEC, ANY_SPEC], out_specs=[HBM_SPEC] * na,
        input_output_aliases={k: k for k in range(na)},
        compiler_params=pltpu.CompilerParams(has_side_effects=DATAFLOW),
    )(*arrays, send_sems, recv_sems, after)
    return outs[:ns], outs[ns:]


def _all_reduce_small(x):
    R = x.shape[0]

    def body(x_ref, o_ref, gath, send_sems, recv_sems):
        me = _my_index()
        gath[me] = x_ref[...]
        sends = []
        for off in range(1, N_DEV):
            to = (me + off) % N_DEV
            cp = pltpu.make_async_remote_copy(x_ref, gath.at[me], send_sems.at[off], recv_sems.at[off],
                                              device_id=_peer(to), device_id_type=MESH)
            cp.start()
            sends.append(cp)
        for off in range(1, N_DEV):
            frm = (me + N_DEV - off) % N_DEV
            pltpu.make_async_remote_copy(x_ref, gath.at[frm], send_sems.at[off], recv_sems.at[off],
                                         device_id=_peer(frm), device_id_type=MESH).wait_recv()
        for cp in sends:
            cp.wait_send()
        acc = gath[0]
        for s in range(1, N_DEV):
            acc = acc + gath[s]
        o_ref[...] = acc

    vm = pl.BlockSpec(memory_space=pltpu.VMEM)
    return pl.pallas_call(
        body, name="all_reduce_small", in_specs=[vm], out_specs=vm, out_shape=jax.ShapeDtypeStruct((R, 128), F32),
        scratch_shapes=[pltpu.VMEM((N_DEV, R, 128), F32), pltpu.SemaphoreType.DMA((N_DEV,)),
                        pltpu.SemaphoreType.DMA((N_DEV,))],
        compiler_params=pltpu.CompilerParams(has_side_effects=True),
    )(x)


def _adamw_math(w, g, m, v):
    m = ADAM_B1 * m + (1.0 - ADAM_B1) * g
    v = ADAM_B2 * v + (1.0 - ADAM_B2) * (g * g)
    m_hat = m / (1.0 - ADAM_B1 ** ADAM_STEP)
    v_hat = v / (1.0 - ADAM_B2 ** ADAM_STEP)
    delta = -ADAM_LR * (m_hat / (jnp.sqrt(v_hat) + ADAM_EPS) + ADAM_WD * w)
    return delta, m, v


def _adamw_layer(recv, own, kind, w, m, v, outs, layer, dep, *, name):
    nl, R, C = w.shape
    tr = _tile(R, (128, 64, 32, 16))

    def body(r_ref, o_ref, w_ref, m_ref, v_ref, _0, _1, _2, _3, _dep, g_out, d_out, m_out, v_out, token):
        token[...] = jnp.zeros_like(token)
        me = _my_index()
        mine = o_ref[...].astype(F32)
        g = jnp.where(me == 0, mine, r_ref[0].astype(F32))
        for s in range(1, N_DEV):
            g = g + jnp.where(me == s, mine, r_ref[s].astype(F32))
        delta, mn, vn = _adamw_math(w_ref[...], g, m_ref[...], v_ref[...])
        g_out[...] = g
        d_out[...] = delta
        m_out[...] = mn
        v_out[...] = vn

    if kind == "col":
        own_spec = pl.BlockSpec((None, tr, C), lambda t: (_my_index(), t, 0))
    else:
        own_spec = pl.BlockSpec((None, tr, C), lambda t: (0, _my_index() * (R // tr) + t, 0))
    wspec = pl.BlockSpec((None, tr, C), lambda t: (layer, t, 0))
    res = pl.pallas_call(
        body, name=name, grid=(R // tr,),
        in_specs=[pl.BlockSpec((N_DEV, tr, C), lambda t: (0, t, 0)), own_spec] + [wspec] * 3 + [ANY_SPEC] * 5,
        out_specs=[wspec] * 4 + [pl.BlockSpec((8, 128), lambda t: (0, 0))],
        out_shape=[jax.ShapeDtypeStruct((nl, R, C), F32)] * 4 + [jax.ShapeDtypeStruct((8, 128), F32)],
        input_output_aliases={5: 0, 6: 1, 7: 2, 8: 3},
        compiler_params=_params(("arbitrary",)),
    )(recv, own, w, m, v, *outs, dep)
    return res[:4], res[4]


def _adamw_small(g, w, m, v):
    def body(g_ref, w_ref, m_ref, v_ref, d_out, m_out, v_out):
        delta, mn, vn = _adamw_math(w_ref[...], g_ref[...], m_ref[...], v_ref[...])
        d_out[...] = delta
        m_out[...] = mn
        v_out[...] = vn

    return pl.pallas_call(body, name="adamw_small", out_shape=[jax.ShapeDtypeStruct(g.shape, F32)] * 3)(g, w, m, v)


def _pack(arrays, rows):
    flat = jnp.concatenate([a.reshape(-1) for a in arrays])
    return jnp.pad(flat, (0, rows * 128 - flat.shape[0])).reshape(rows, 128)


def _unpack(packed, shapes):
    flat = packed.reshape(-1)
    out, pos = [], 0
    for s in shapes:
        size = int(np.prod(s))
        out.append(flat[pos:pos + size].reshape(s))
        pos += size
    return out


def kernel(x, norm1_g, w_in, qk_norm_g, sink_a, rpb_c, w_br_a, w_br_b, w_br_c, w_o, norm2_g, w_gate_up, w_down, loss_target, m_norm1_g, m_w_in, m_qk_norm_g, m_sink_a, m_rpb_c, m_w_br_a, m_w_br_b, m_w_br_c, m_w_o, m_norm2_g, m_w_gate_up, m_w_down, v_norm1_g, v_w_in, v_qk_norm_g, v_sink_a, v_rpb_c, v_w_br_a, v_w_br_b, v_w_br_c, v_w_o, v_norm2_g, v_w_gate_up, v_w_down):
    nl = w_in.shape[0]
    L, D = x.shape[1], x.shape[2]
    x0 = x.reshape(L, D)
    tgt = loss_target.reshape(L, D)

    big = [w_in, w_br_a, w_br_b, w_br_c, w_o, w_gate_up, w_down]
    kinds = ["col", "col", "col", "col", "row", "col", "row"]

    big_names = ["w_in", "w_br_a", "w_br_b", "w_br_c", "w_o", "w_gate_up", "w_down"]
    ALL = list(range(len(big)))
    REST = ALL[1:]

    def gather_place(i):
        return [_place_cast(w, i, k, name="gather_place_" + n) for w, k, n in zip(big, kinds, big_names)]

    def gather_start(mode, lands, sub, after, tag):
        return _exchange_start(mode, [], lands, [kinds[j] for j in sub], after, name=mode + "_start" + tag)

    def gather_wait(mode, started, sub, after, tag):
        return _exchange_wait(mode, started, [kinds[j] for j in sub], after, name=mode + "_wait" + tag)[1]

    def matmul_views(lands, sub):
        return [g.reshape((N_DEV, 1) + g.shape[1:]) if kinds[j] == "col" else g.reshape((1, 1) + g.shape)
                for g, j in zip(lands, sub)]

    half = HEAD_DIM // 2
    inv_freq = ROPE_THETA ** (-jnp.arange(half, dtype=F32) * 2.0 / HEAD_DIM)
    ang = jnp.arange(L, dtype=F32)[:, None] * inv_freq[None, :]
    cos = jnp.concatenate([jnp.cos(ang), jnp.cos(ang)], axis=-1)
    sin = jnp.concatenate([-jnp.sin(ang), jnp.sin(ang)], axis=-1)
    expand = jnp.asarray(_c_expand_matrix(), BF16)
    expand_t = jnp.asarray(_c_expand_matrix().T, BF16)

    def gain_tables(i):
        g = qk_norm_g[i]
        gq = jnp.concatenate([jnp.tile(g[0][None], (8, 1)), jnp.tile(g[2][None], (12, 1)), jnp.tile(g[4][None], (8, 1))])
        gk = jnp.concatenate([jnp.tile(g[1][None], (2, 1)), jnp.tile(g[3][None], (12, 1)), jnp.tile(g[5][None], (8, 1))])
        return (gq.reshape(NQ_CHUNKS // Q_CG, 1, Q_CG * HEAD_DIM), gk.reshape(NK_CHUNKS // K_CG, 1, K_CG * HEAD_DIM))

    def bias_table(i):
        rp = jnp.pad(rpb_c[i].reshape(C_HEADS * C_NREL, C_NCOL), ((0, 0), (0, HEAD_DIM - C_NCOL)))
        t = _exact_mm(rp, expand_t, name="c_bias_expand")
        return _c_bias_tiles(t.reshape(C_HEADS, C_NREL, GRID_W, GRID_W))

    def sink_table(i):
        return jnp.broadcast_to(sink_a[i][:, None, None], (A_Q_HEADS, 1, HEAD_DIM))

    saved = []
    gws = [None] * nl
    xi = x0
    lands0 = gather_place(0)
    lvl1 = gather_start("gather1", lands0[:1], [0], x0, "_first")
    lvl2 = gather_start("gather2", gather_wait("gather1", lvl1, [0], x0, "_first"), [0], x0, "_first")
    gws[0] = matmul_views(gather_wait("gather2", lvl2, [0], x0, "_first"), [0])
    rest1 = gather_start("gather1", lands0[1:], REST, gws[0][0], "_rest")
    dep = rest1[4]
    for i in range(nl):
        qk_dep = None
        if i >= 1 and i + 1 < nl:
            nxt1 = gather_start("gather1", gather_place(i + 1), ALL, dep, "")
            dep = nxt1[4]
        gw_in = gws[i][0]
        gq, gk = gain_tables(i)
        bias_t = bias_table(i)
        sink = sink_table(i)
        h1 = _rms_fwd(xi, norm1_g[i][None], dep, name="rms1_fwd")
        proj = _mm_nn(h1, gw_in, 0, out_dtype=F32, name="proj_fwd")
        if i == 0:
            rest2 = gather_start("gather2", gather_wait("gather1", rest1, REST, proj, "_rest"), REST, proj, "_rest")
            qk_dep = rest2[4]
            if nl > 1:
                nxt1 = gather_start("gather1", gather_place(1), ALL, rest2[4], "")
                qk_dep = nxt1[4]
        qn = _qk_fwd(proj, gq, cos, sin, Q_PIECES, NQ_CHUNKS, Q_ROPE_UPTO, Q_CG, name="qnorm_fwd", dep=qk_dep)
        kn = _qk_fwd(proj, gk, cos, sin, K_PIECES, NK_CHUNKS, K_ROPE_UPTO, K_CG, name="knorm_fwd")
        oa, lse_a = _band_fwd(qn, kn, proj, dil=1, radius=A_RADIUS, nkv=A_KV_HEADS, group=A_GROUP,
                              q0=0, k0=0, v0=PC_VA, sink=sink, name="attn_a_fwd")
        obs, lbs = [], []
        for g, (window, dil) in enumerate(B_PATTERNS):
            o_g, l_g = _band_fwd(qn, kn, proj, dil=dil, radius=window // (2 * dil), nkv=B_HG, group=1,
                                 q0=8 + g * B_HG, k0=2 + g * B_HG, v0=PC_VB + g * B_HG, name=f"attn_b{g}_fwd")
            obs.append(o_g)
            lbs.append(l_g)
        ob, lse_b = _combine_b(obs, lbs, name="attn_b_combine")
        oc, lse_c = _c_fwd(qn, kn, proj, bias_t, name="attn_c_fwd")
        if i == 0:
            gws[0] = gws[0] + matmul_views(gather_wait("gather2", rest2, REST, oc, "_rest"), REST)
        _, gw_a, gw_b, gw_c, gw_o, gw_gu, gw_d = gws[i]
        ta = _mm_nn(oa, gw_a, 0, out_dtype=BF16, name="br_a_fwd")
        tb = _mm_nn(ob, gw_b, 0, out_dtype=BF16, name="br_b_fwd")
        tc = _mm_nn(oc, gw_c, 0, out_dtype=BF16, name="br_c_fwd")
        merged = _gate_fwd(proj, ta, tb, tc, name="gate_fwd")
        x1 = _mm_nn(merged, gw_o, 0, out_dtype=F32, name="wo_fwd", res=xi)
        dep = x1
        if i + 1 < nl:
            nxt2 = gather_start("gather2", gather_wait("gather1", nxt1, ALL, x1, ""), ALL, x1, "")
            dep = nxt2[4]
        h2 = _rms_fwd(x1, norm2_g[i][None], dep, name="rms2_fwd")
        gu = _mm_nn(h2, gw_gu, 0, out_dtype=BF16, name="gate_up_fwd")
        act = _swiglu_fwd(gu, name="swiglu_fwd")
        x2 = _mm_nn(act, gw_d, 0, out_dtype=F32, name="down_fwd", res=x1)
        saved.append(dict(x=xi, h1=h1, proj=proj, qn=qn, kn=kn, oa=oa, lse_a=lse_a, ob=ob, lse_b=lse_b, oc=oc,
                          lse_c=lse_c, ta=ta, tb=tb, tc=tc, merged=merged, x1=x1, h2=h2, gu=gu, act=act,
                          gq=gq, gk=gk, bias_t=bias_t, sink=sink))
        xi = x2
        dep = x2
        if i + 1 < nl:
            gws[i + 1] = matmul_views(gather_wait("gather2", nxt2, ALL, x2, ""), ALL)

    dx, dxb, loss_row = _loss(xi, tgt, name="loss")

    def scatter_start(grads, sub, after, tag):
        lands = []
        for g, j in zip(grads, sub):
            shape = g.shape if kinds[j] == "col" else (N_DEV, g.shape[1] // N_DEV, g.shape[2])
            lands.append(lax.empty(shape, BF16))
        return _exchange_start("scatter", grads, lands, [kinds[j] for j in sub], after, name="scatter_start" + tag)

    def scatter_wait(pair, after):
        own_a, recv_a = _exchange_wait("scatter", pair[0], [kinds[0]], after, name="scatter_wait_in")
        own_b, recv_b = _exchange_wait("scatter", pair[1], [kinds[j] for j in REST], after, name="scatter_wait_rest")
        return list(recv_a) + list(recv_b), list(own_a) + list(own_b)

    small_grads = [None] * nl
    recv = [None] * nl
    own = [None] * nl
    pending = None
    for i in reversed(range(nl)):
        s = saved[i]
        gw_in, gw_a, gw_b, gw_c, gw_o, gw_gu, gw_d = gws[i]
        dact = _mm_nt(dxb, gw_d, 0, out_dtype=BF16, name="down_bwd_x", dep=None if pending is None else pending[0][4])
        g_down = _mm_tn(s["act"], dxb, 1, name="down_bwd_w")
        dgu = _swiglu_bwd(s["gu"], dact, name="swiglu_bwd")
        g_gu = _mm_tn(s["h2"], dgu, N_DEV, name="gate_up_bwd_w")
        dh2 = _mm_nt(dgu, gw_gu, 0, out_dtype=F32, name="gate_up_bwd_x")
        dx1, dx1b, dg2 = _rms_bwd(s["x1"], norm2_g[i][None], dh2, dx, name="rms2_bwd")
        dmerged = _mm_nt(dx1b, gw_o, 0, out_dtype=F32, name="wo_bwd_x")
        g_o = _mm_tn(s["merged"], dx1b, 1, name="wo_bwd_w")
        dta, dtb, dtc, dproj = _gate_bwd(s["proj"], s["ta"], s["tb"], s["tc"], dmerged, name="gate_bwd")
        g_a = _mm_tn(s["oa"], dta, N_DEV, name="br_a_bwd_w")
        g_b = _mm_tn(s["ob"], dtb, N_DEV, name="br_b_bwd_w")
        g_c = _mm_tn(s["oc"], dtc, N_DEV, name="br_c_bwd_w")
        rest = scatter_start([g_a, g_b, g_c, g_o, g_gu, g_down], REST, g_c, "_rest")
        doa = _mm_nt(dta, gw_a, 0, out_dtype=F32, name="br_a_bwd_x", dep=rest[4])
        dob = _mm_nt(dtb, gw_b, 0, out_dtype=F32, name="br_b_bwd_x")
        doc = _mm_nt(dtc, gw_c, 0, out_dtype=F32, name="br_c_bwd_x")
        dq_buf = lax.empty((L, NQ_CHUNKS * HEAD_DIM), F32)
        dk_buf = lax.empty((L, NK_CHUNKS * HEAD_DIM), F32)
        dv_buf = lax.empty((L, NK_CHUNKS * HEAD_DIM), F32)
        dq_buf, dk_buf, dv_buf, dsink = _band_bwd(
            s["qn"], s["kn"], s["proj"], doa, s["oa"], s["lse_a"], dq_buf, dk_buf, dv_buf, dil=1, radius=A_RADIUS,
            nkv=A_KV_HEADS, group=A_GROUP, q0=0, k0=0, v0=PC_VA, o0=0, sink=s["sink"], name="attn_a_bwd")
        for g, (window, dil) in enumerate(B_PATTERNS):
            dq_buf, dk_buf, dv_buf = _band_bwd(
                s["qn"], s["kn"], s["proj"], dob, s["ob"], s["lse_b"], dq_buf, dk_buf, dv_buf, dil=dil,
                radius=window // (2 * dil), nkv=B_HG, group=1, q0=8 + g * B_HG, k0=2 + g * B_HG,
                v0=PC_VB + g * B_HG, o0=0, name=f"attn_b{g}_bwd")
        dq_buf, dk_buf, dv_buf, dbias_t = _c_bwd(s["qn"], s["kn"], s["proj"], s["bias_t"], doc, s["oc"], s["lse_c"],
                                                 dq_buf, dk_buf, dv_buf, name="attn_c_bwd")
        dproj, dgq = _qk_bwd(dq_buf, s["proj"], s["gq"], cos, sin, dproj, Q_PIECES, NQ_CHUNKS, Q_ROPE_UPTO, Q_CG,
                             name="qnorm_bwd")
        dproj, dgk = _qk_bwd(dk_buf, s["proj"], s["gk"], cos, sin, dproj, K_PIECES, NK_CHUNKS, K_ROPE_UPTO, K_CG,
                             name="knorm_bwd")
        dproj = _v_bwd(dv_buf, dproj, name="v_bwd")
        g_in = _mm_tn(s["h1"], dproj, N_DEV, name="proj_bwd_w")
        dh1 = _mm_nt(dproj, gw_in, 0, out_dtype=F32, name="proj_bwd_x")
        dx, dxb, dg1 = _rms_bwd(s["x"], norm1_g[i][None], dh1, dx1, name="rms1_bwd")
        if pending is not None:
            recv[i + 1], own[i + 1] = scatter_wait(pending, dx)
        pending = (scatter_start([g_in], [0], dx, "_in"), rest)

        drpb = _exact_mm(dbias_t.reshape(C_HEADS * C_NREL, GRID_W * GRID_W), expand, name="c_bias_reduce")
        dgq, dgk = dgq.reshape(NQ_CHUNKS, HEAD_DIM), dgk.reshape(NK_CHUNKS, HEAD_DIM)
        dqk_g = jnp.stack([dgq[0:8].sum(0), dgk[0:2].sum(0), dgq[8:20].sum(0), dgk[2:14].sum(0),
                           dgq[20:28].sum(0), dgk[14:22].sum(0)])
        small_grads[i] = (dg1.reshape(D), dqk_g, dsink[:, 0, 0],
                          drpb[:, :C_NCOL].reshape(C_HEADS, C_NREL, C_NCOL), dg2.reshape(D))

    small_names = [norm1_g, qk_norm_g, sink_a, rpb_c, norm2_g]
    small_m = [m_norm1_g, m_qk_norm_g, m_sink_a, m_rpb_c, m_norm2_g]
    small_v = [v_norm1_g, v_qk_norm_g, v_sink_a, v_rpb_c, v_norm2_g]
    shapes = [a.shape for a in small_names]
    total = sum(int(np.prod(sh)) for sh in shapes) + 128
    rows = -(-total // 1024) * 8
    stacked = [jnp.stack([small_grads[i][j] for i in range(nl)]) for j in range(5)]
    packed = _pack([loss_row.reshape(-1)] + stacked, rows)
    summed = _all_reduce_small(packed)
    loss = summed[0, 0]
    zero_row = jnp.zeros((128,), F32)
    d_s, m_s, v_s = _adamw_small(summed, _pack([zero_row] + small_names, rows), _pack([zero_row] + small_m, rows),
                                 _pack([zero_row] + small_v, rows))
    shapes1 = [(128,)] + shapes
    g_small = _unpack(summed, shapes1)[1:]
    d_small = _unpack(d_s, shapes1)[1:]
    m_small = _unpack(m_s, shapes1)[1:]
    v_small = _unpack(v_s, shapes1)[1:]

    big_m = [m_w_in, m_w_br_a, m_w_br_b, m_w_br_c, m_w_o, m_w_gate_up, m_w_down]
    big_v = [v_w_in, v_w_br_a, v_w_br_b, v_w_br_c, v_w_o, v_w_gate_up, v_w_down]
    big_out = [[lax.empty(w.shape, F32) for _ in range(4)] for w in big]
    token = pending[0][4]
    for i in list(range(nl - 1, 0, -1)) + [0]:
        if i == 0:
            recv[0], own[0] = scatter_wait(pending, token)
        for j in range(len(big)):
            big_out[j], token = _adamw_layer(recv[i][j], own[i][j], kinds[j], big[j], big_m[j], big_v[j],
                                             big_out[j], i, token, name="adamw_" + big_names[j])

    order = ["norm1_g", "w_in", "qk_norm_g", "sink_a", "rpb_c", "w_br_a", "w_br_b", "w_br_c", "w_o", "norm2_g",
             "w_gate_up", "w_down"]
    small_idx = {"norm1_g": 0, "qk_norm_g": 1, "sink_a": 2, "rpb_c": 3, "norm2_g": 4}
    big_idx = {n: j for j, n in enumerate(big_names)}

    def pick(kind):
        out = []
        for n in order:
            if n in small_idx:
                out.append([g_small, d_small, m_small, v_small][kind][small_idx[n]])
            else:
                out.append(big_out[big_idx[n]][kind])
        return out

    return (loss, dx.reshape(1, L, D), *pick(0), *pick(1), *pick(2), *pick(3))
```

```python
import functools
import math

import numpy as np
import jax
import jax.numpy as jnp
from jax import lax
from jax.experimental import pallas as pl
from jax.experimental.pallas import tpu as pltpu

F32 = jnp.float32
BF16 = jnp.bfloat16
MESH = pl.DeviceIdType.MESH
N_DEV = 8

HEAD_DIM = 128
NORM_EPS = 1e-6
ROPE_THETA = 10000.0
ATT_SCALE = HEAD_DIM ** -0.5
NEG = -1e30

A_Q_HEADS, A_KV_HEADS, A_RADIUS = 8, 2, 128
A_GROUP = A_Q_HEADS // A_KV_HEADS
B_PATTERNS = ((128, 1), (512, 4), (2048, 16))
B_HG = 4
B_HEADS = len(B_PATTERNS) * B_HG
C_HEADS, GRID_W, C_WIN_ROWS, C_WIN_COLS = 8, 64, 8, 16
C_NREL = 2 * C_WIN_ROWS - 1
C_NCOL = 2 * C_WIN_COLS - 1

PC_QA, PC_KA, PC_VA = 0, 8, 10
PC_QB, PC_KB, PC_VB = 12, 24, 36
PC_QC, PC_KC, PC_VC = 48, 56, 64
N_QKV_CHUNKS = 72
Q_PIECES = ((0, 8, PC_QA), (8, 12, PC_QB), (20, 8, PC_QC))
K_PIECES = ((0, 2, PC_KA), (2, 12, PC_KB), (14, 8, PC_KC))
V_PIECES = ((0, 2, PC_VA), (2, 12, PC_VB), (14, 8, PC_VC))
NQ_CHUNKS, NK_CHUNKS = 28, 22
Q_ROPE_UPTO, K_ROPE_UPTO = 20, 14
Q_CG, K_CG = 4, 2

ADAM_LR, ADAM_B1, ADAM_B2, ADAM_EPS, ADAM_WD, ADAM_STEP = 0.001, 0.9, 0.999, 1e-08, 0.01, 10

VMEM_LIMIT = 48 * 1024 * 1024


def _tile(dim, prefs):
    for p in prefs:
        if dim % p == 0:
            return p
    return dim


NN_WEIGHT_TILE_BYTES = 8 * 1024 * 1024
NT_WEIGHT_TILE_BYTES = 4 * 1024 * 1024
TN_ACC_BYTES = 6 * 1024 * 1024
MAX_COL_TILE = 2048


def _col_tile(ns):
    return ns if ns <= MAX_COL_TILE else _tile(ns, (MAX_COL_TILE, 1024, 512, 256, 128))


def _params(sem, **kw):
    return pltpu.CompilerParams(dimension_semantics=sem, vmem_limit_bytes=VMEM_LIMIT, **kw)


def _piece_map(pieces):
    def f(c):
        out = c - pieces[0][0] + pieces[0][2]
        for first, _, pfirst in pieces[1:]:
            out = jnp.where(c >= first, c - first + pfirst, out)
        return out
    return f


def _mm_nn(a, w, layer, *, out_dtype, name, res=None):
    M, K = a.shape
    nb, _, Kw, ns = w.shape
    assert Kw == K
    tn = _col_tile(ns)
    tm = _tile(M, (1024, 512, 256) if tn <= 512 else (512, 256))
    tk = _tile(K, tuple(t for t in (2048, 1408, 1024, 512, 256) if t * tn * 2 <= NN_WEIGHT_TILE_BYTES))
    nj, nk = ns // tn, K // tk

    def body(*refs):
        a_ref, w_ref = refs[:2]
        r_ref = None if res is None else refs[2]
        o_ref = refs[2 if res is None else 3]
        part = jnp.dot(a_ref[...].astype(BF16), w_ref[...], preferred_element_type=F32)
        if nk == 1:
            if r_ref is not None:
                part = part + r_ref[...]
            o_ref[...] = part.astype(out_dtype)
            return
        acc_ref = refs[-1]
        k = pl.program_id(3)

        @pl.when(k == 0)
        def _():
            acc_ref[...] = part

        @pl.when(k > 0)
        def _():
            acc_ref[...] += part

        @pl.when(k == nk - 1)
        def _():
            r = acc_ref[...]
            if r_ref is not None:
                r = r + r_ref[...]
            o_ref[...] = r.astype(out_dtype)

    in_specs = [pl.BlockSpec((tm, tk), lambda i, b, j, k: (i, k)),
                pl.BlockSpec((None, None, tk, tn), lambda i, b, j, k: (b, layer, k, j))]
    args = [a, w]
    if res is not None:
        in_specs.append(pl.BlockSpec((tm, tn), lambda i, b, j, k: (i, b * nj + j)))
        args.append(res)
    return pl.pallas_call(
        body, name=name, grid=(M // tm, nb, nj, nk), in_specs=in_specs,
        out_specs=pl.BlockSpec((tm, tn), lambda i, b, j, k: (i, b * nj + j)),
        out_shape=jax.ShapeDtypeStruct((M, nb * ns), out_dtype),
        scratch_shapes=[] if nk == 1 else [pltpu.VMEM((tm, tn), F32)],
        compiler_params=_params(("parallel", "parallel", "parallel", "arbitrary")),
    )(*args)


def _mm_nt(a, w, layer, *, out_dtype, name, dep=None):
    M, N = a.shape
    nb, _, K, ns = w.shape
    assert N == nb * ns
    tm = _tile(M, (1024, 512, 256))
    tn = _col_tile(ns)
    tk = _tile(K, tuple(t for t in (1024, 512, 256) if t * tn * 2 <= NT_WEIGHT_TILE_BYTES))
    nj = ns // tn
    nred = nb * nj

    def body(*refs):
        a_ref, w_ref = refs[:2]
        o_ref, acc_ref = refs[-2:]
        s = pl.program_id(2) * nj + pl.program_id(3)
        part = lax.dot_general(a_ref[...].astype(BF16), w_ref[...], (((1,), (1,)), ((), ())),
                               preferred_element_type=F32)

        @pl.when(s == 0)
        def _():
            acc_ref[...] = part

        @pl.when(s > 0)
        def _():
            acc_ref[...] += part

        @pl.when(s == nred - 1)
        def _():
            o_ref[...] = acc_ref[...].astype(out_dtype)

    in_specs = [pl.BlockSpec((tm, tn), lambda i, kk, b, j: (i, b * nj + j)),
                pl.BlockSpec((None, None, tk, tn), lambda i, kk, b, j: (b, layer, kk, j))]
    args = [a, w]
    if dep is not None:
        in_specs.append(ANY_SPEC)
        args.append(dep)
    return pl.pallas_call(
        body, name=name, grid=(M // tm, K // tk, nb, nj), in_specs=in_specs,
        out_specs=pl.BlockSpec((tm, tk), lambda i, kk, b, j: (i, kk)),
        out_shape=jax.ShapeDtypeStruct((M, K), out_dtype),
        scratch_shapes=[pltpu.VMEM((tm, tk), F32)],
        compiler_params=_params(("parallel", "parallel", "arbitrary", "arbitrary")),
    )(*args)


def _mm_tn(a, g, nb, *, name):
    M, Ka = a.shape
    N = g.shape[1]
    ns = N // nb
    tn = _col_tile(ns)
    tka = _tile(Ka, tuple(t for t in (1024, 512, 256) if t * tn * 4 <= TN_ACC_BYTES))
    tm = _tile(M, (2048, 1024, 512, 256))
    nj, nm = ns // tn, M // tm

    def body(a_ref, g_ref, o_ref, acc_ref):
        m = pl.program_id(3)
        part = lax.dot_general(a_ref[...].astype(BF16), g_ref[...].astype(BF16), (((0,), (0,)), ((), ())),
                               preferred_element_type=F32)

        @pl.when(m == 0)
        def _():
            acc_ref[...] = part

        @pl.when(m > 0)
        def _():
            acc_ref[...] += part

        @pl.when(m == nm - 1)
        def _():
            o_ref[...] = acc_ref[...].astype(BF16)

    return pl.pallas_call(
        body, name=name, grid=(Ka // tka, nb, nj, nm),
        in_specs=[pl.BlockSpec((tm, tka), lambda ka, b, j, m: (m, ka)),
                  pl.BlockSpec((tm, tn), lambda ka, b, j, m: (m, b * nj + j))],
        out_specs=pl.BlockSpec((None, tka, tn), lambda ka, b, j, m: (b, ka, j)),
        out_shape=jax.ShapeDtypeStruct((nb, Ka, ns), BF16),
        scratch_shapes=[pltpu.VMEM((tka, tn), F32)],
        compiler_params=_params(("parallel", "parallel", "parallel", "arbitrary")),
    )(a, g)


def _blocks_to_wide(w, *, name):
    nb, _, K, ns = w.shape

    def body(i_ref, o_ref):
        o_ref[...] = i_ref[...]

    return pl.pallas_call(
        body, name=name, grid=(nb,), in_specs=[pl.BlockSpec((None, None, K, ns), lambda b: (b, 0, 0, 0))],
        out_specs=pl.BlockSpec((None, None, K, ns), lambda b: (0, 0, 0, b)),
        out_shape=jax.ShapeDtypeStruct((1, 1, K, nb * ns), w.dtype), compiler_params=_params(("parallel",)),
    )(w)


def _wide_to_blocks(g, nb, *, name):
    _, K, N = g.shape
    ns = N // nb

    def body(i_ref, o_ref):
        o_ref[...] = i_ref[...]

    return pl.pallas_call(
        body, name=name, grid=(nb,), in_specs=[pl.BlockSpec((None, K, ns), lambda b: (0, 0, b))],
        out_specs=pl.BlockSpec((None, K, ns), lambda b: (b, 0, 0)),
        out_shape=jax.ShapeDtypeStruct((nb, K, ns), g.dtype), compiler_params=_params(("parallel",)),
    )(g)


def _exact_mm(a, e, *, name):
    R, K = a.shape
    N = e.shape[1]

    def body(a_ref, e_ref, o_ref):
        x = a_ref[...]
        hi = x.astype(BF16)
        r1 = x - hi.astype(F32)
        mid = r1.astype(BF16)
        lo = (r1 - mid.astype(F32)).astype(BF16)
        ev = e_ref[...]
        o_ref[...] = (jnp.dot(hi, ev, preferred_element_type=F32) + jnp.dot(mid, ev, preferred_element_type=F32)
                      + jnp.dot(lo, ev, preferred_element_type=F32))

    return pl.pallas_call(body, name=name, out_shape=jax.ShapeDtypeStruct((R, N), F32),
                          compiler_params=pltpu.CompilerParams(vmem_limit_bytes=VMEM_LIMIT))(a, e)


def _rms_fwd(x, g, dep, *, name):
    L, D = x.shape
    tl = _tile(L, (256, 128))

    def body(x_ref, g_ref, _dep, h_ref):
        xv = x_ref[...]
        rstd = lax.rsqrt(jnp.mean(xv * xv, axis=-1, keepdims=True) + NORM_EPS)
        h_ref[...] = (xv * rstd * g_ref[...]).astype(BF16)

    return pl.pallas_call(
        body, name=name, grid=(L // tl,),
        in_specs=[pl.BlockSpec((tl, D), lambda t: (t, 0)), pl.BlockSpec((1, D), lambda t: (0, 0)), ANY_SPEC],
        out_specs=pl.BlockSpec((tl, D), lambda t: (t, 0)),
        out_shape=jax.ShapeDtypeStruct((L, D), BF16),
        compiler_params=_params(("parallel",)),
    )(x, g, dep)


def _rms_bwd(x, g, dy, dres, *, name):
    L, D = x.shape
    tl = _tile(L, (128,))

    def body(x_ref, g_ref, dy_ref, dres_ref, dx_ref, dxb_ref, dg_ref):
        t = pl.program_id(0)
        xv = x_ref[...]
        rstd = lax.rsqrt(jnp.mean(xv * xv, axis=-1, keepdims=True) + NORM_EPS)
        xhat = xv * rstd
        dyv = dy_ref[...]
        dxhat = dyv * g_ref[...]
        c = jnp.mean(dxhat * xhat, axis=-1, keepdims=True)
        dx = dres_ref[...] + rstd * (dxhat - xhat * c)
        dx_ref[...] = dx
        dxb_ref[...] = dx.astype(BF16)
        dgp = jnp.sum(dyv * xhat, axis=0, keepdims=True)

        @pl.when(t == 0)
        def _():
            dg_ref[...] = dgp

        @pl.when(t > 0)
        def _():
            dg_ref[...] += dgp

    row = pl.BlockSpec((tl, D), lambda t: (t, 0))
    vec = pl.BlockSpec((1, D), lambda t: (0, 0))
    return pl.pallas_call(
        body, name=name, grid=(L // tl,), in_specs=[row, vec, row, row], out_specs=[row, row, vec],
        out_shape=[jax.ShapeDtypeStruct((L, D), F32), jax.ShapeDtypeStruct((L, D), BF16),
                   jax.ShapeDtypeStruct((1, D), F32)],
        compiler_params=_params(("arbitrary",)),
    )(x, g, dy, dres)


def _gate_fwd(proj, ta, tb, tc, *, name):
    L, D = ta.shape
    tl, tcw = _tile(L, (256, 128)), _tile(D, (1024, 512, 256, 128))
    off = N_QKV_CHUNKS * HEAD_DIM // tcw
    nd = D // tcw

    def body(g0, g1, g2, a_ref, b_ref, c_ref, o_ref):
        m = (jax.nn.sigmoid(g0[...]) * a_ref[...].astype(F32) + jax.nn.sigmoid(g1[...]) * b_ref[...].astype(F32)
             + jax.nn.sigmoid(g2[...]) * c_ref[...].astype(F32))
        o_ref[...] = m.astype(BF16)

    blk = pl.BlockSpec((tl, tcw), lambda t, j: (t, j))
    gl = [pl.BlockSpec((tl, tcw), functools.partial(lambda t, j, i: (t, off + i * nd + j), i=i)) for i in range(3)]
    return pl.pallas_call(
        body, name=name, grid=(L // tl, nd), in_specs=gl + [blk, blk, blk], out_specs=blk,
        out_shape=jax.ShapeDtypeStruct((L, D), BF16),
        compiler_params=_params(("parallel", "parallel")),
    )(proj, proj, proj, ta, tb, tc)


def _gate_bwd(proj, ta, tb, tc, dmerged, *, name):
    L, D = ta.shape
    ncols = proj.shape[1]
    tl, tcw = _tile(L, (256, 128)), _tile(D, (1024, 512, 256, 128))
    off = N_QKV_CHUNKS * HEAD_DIM // tcw
    nd = D // tcw

    def body(g0, g1, g2, a_ref, b_ref, c_ref, dm_ref, da_ref, db_ref, dc_ref, dgl_ref):
        i = pl.program_id(2)
        sg = jax.nn.sigmoid(jnp.where(i == 0, g0[...], jnp.where(i == 1, g1[...], g2[...])))
        sel_t = jnp.where(i == 0, a_ref[...], jnp.where(i == 1, b_ref[...], c_ref[...])).astype(F32)
        dt = dm_ref[...] * sg
        dtb = dt.astype(BF16)

        @pl.when(i == 0)
        def _():
            da_ref[...] = dtb

        @pl.when(i == 1)
        def _():
            db_ref[...] = dtb

        @pl.when(i == 2)
        def _():
            dc_ref[...] = dtb

        dgl_ref[...] = (dt * sel_t * (1.0 - sg)).astype(BF16)

    blk = pl.BlockSpec((tl, tcw), lambda t, j, i: (t, j))
    gl = [pl.BlockSpec((tl, tcw), functools.partial(lambda t, j, i, q: (t, off + q * nd + j), q=q)) for q in range(3)]
    return pl.pallas_call(
        body, name=name, grid=(L // tl, nd, 3), in_specs=gl + [blk, blk, blk, blk],
        out_specs=[blk, blk, blk, pl.BlockSpec((tl, tcw), lambda t, j, i: (t, off + i * nd + j))],
        out_shape=[jax.ShapeDtypeStruct((L, D), BF16)] * 3 + [jax.ShapeDtypeStruct((L, ncols), BF16)],
        compiler_params=_params(("parallel", "parallel", "arbitrary")),
    )(proj, proj, proj, ta, tb, tc, dmerged)


def _swiglu_fwd(gu, *, name):
    L, F2 = gu.shape
    F = F2 // 2
    tl = _tile(L, (128, 64))

    def body(gu_ref, o_ref):
        gt = gu_ref[:, :F].astype(F32)
        o_ref[...] = (gt * jax.nn.sigmoid(gt) * gu_ref[:, F:].astype(F32)).astype(BF16)

    return pl.pallas_call(
        body, name=name, grid=(L // tl,), in_specs=[pl.BlockSpec((tl, F2), lambda t: (t, 0))],
        out_specs=pl.BlockSpec((tl, F), lambda t: (t, 0)),
        out_shape=jax.ShapeDtypeStruct((L, F), BF16),
        compiler_params=_params(("parallel",)),
    )(gu)


def _swiglu_bwd(gu, dact, *, name):
    L, F2 = gu.shape
    F = F2 // 2
    tl = _tile(L, (128, 64))

    def body(gu_ref, d_ref, o_ref):
        gt, up, d = gu_ref[:, :F].astype(F32), gu_ref[:, F:].astype(F32), d_ref[...].astype(F32)
        sg = jax.nn.sigmoid(gt)
        o_ref[:, :F] = (d * up * sg * (1.0 + gt * (1.0 - sg))).astype(BF16)
        o_ref[:, F:] = (d * gt * sg).astype(BF16)

    return pl.pallas_call(
        body, name=name, grid=(L // tl,),
        in_specs=[pl.BlockSpec((tl, F2), lambda t: (t, 0)), pl.BlockSpec((tl, F), lambda t: (t, 0))],
        out_specs=pl.BlockSpec((tl, F2), lambda t: (t, 0)),
        out_shape=jax.ShapeDtypeStruct((L, F2), BF16),
        compiler_params=_params(("parallel",)),
    )(gu, dact)


def _loss(y, tgt, *, name):
    L, D = y.shape
    tl = _tile(L, (256, 128))
    nt = L // tl

    def body(y_ref, t_ref, dy_ref, dyb_ref, loss_ref, acc_ref):
        t = pl.program_id(0)
        e = y_ref[...] - t_ref[...]
        dy = e * (1.0 / D)
        dy_ref[...] = dy
        dyb_ref[...] = dy.astype(BF16)
        part = jnp.sum(e * e, axis=0, keepdims=True)

        @pl.when(t == 0)
        def _():
            acc_ref[...] = part

        @pl.when(t > 0)
        def _():
            acc_ref[...] += part

        @pl.when(t == nt - 1)
        def _():
            loss_ref[...] = jnp.broadcast_to(jnp.sum(acc_ref[...], axis=-1, keepdims=True) * (0.5 / D), (1, 128))

    row = pl.BlockSpec((tl, D), lambda t: (t, 0))
    return pl.pallas_call(
        body, name=name, grid=(nt,), in_specs=[row, row],
        out_specs=[row, row, pl.BlockSpec((1, 128), lambda t: (0, 0))],
        out_shape=[jax.ShapeDtypeStruct((L, D), F32), jax.ShapeDtypeStruct((L, D), BF16),
                   jax.ShapeDtypeStruct((1, 128), F32)],
        scratch_shapes=[pltpu.VMEM((1, D), F32)],
        compiler_params=_params(("arbitrary",)),
    )(y, tgt)


def _rope(v, cos, sin_signed):
    return v * cos + pltpu.roll(v, HEAD_DIM // 2, 1) * sin_signed


def _head_mean(x):
    hi = x.astype(BF16)
    lo = (x - hi.astype(F32)).astype(BF16)
    ones = jnp.ones((HEAD_DIM, HEAD_DIM), BF16)
    total = jnp.dot(hi, ones, preferred_element_type=F32) + jnp.dot(lo, ones, preferred_element_type=F32)
    return total * (1.0 / HEAD_DIM)


def _qk_fwd(proj, gtab, cos, sin, pieces, nchunks, rope_upto, cg, *, name, dep=None):
    L = proj.shape[0]
    tl = _tile(L, (512, 256, 128))
    W = cg * HEAD_DIM
    pmap = _piece_map(tuple((a // cg, n // cg, p // cg) for a, n, p in pieces))

    def body(*refs):
        p_ref, g_ref, cos_ref, sin_ref = refs[:4]
        o_ref = refs[-1]
        c = pl.program_id(1)

        def norm(j):
            cols = slice(j * HEAD_DIM, (j + 1) * HEAD_DIM)
            x = p_ref[:, cols]
            rstd = lax.rsqrt(_head_mean(x * x) + NORM_EPS)
            return cols, x * rstd * g_ref[:, cols]

        @pl.when(c < rope_upto // cg)
        def _():
            for j in range(cg):
                cols, y = norm(j)
                o_ref[:, cols] = _rope(y, cos_ref[...], sin_ref[...])

        @pl.when(c >= rope_upto // cg)
        def _():
            for j in range(cg):
                cols, y = norm(j)
                o_ref[:, cols] = y

    pos = pl.BlockSpec((tl, HEAD_DIM), lambda t, c: (t, 0))
    in_specs = [pl.BlockSpec((tl, W), lambda t, c: (t, pmap(c))),
                pl.BlockSpec((None, 1, W), lambda t, c: (c, 0, 0)), pos, pos]
    args = [proj, gtab, cos, sin]
    if dep is not None:
        in_specs.append(ANY_SPEC)
        args.append(dep)
    return pl.pallas_call(
        body, name=name, grid=(L // tl, nchunks // cg), in_specs=in_specs,
        out_specs=pl.BlockSpec((tl, W), lambda t, c: (t, c)),
        out_shape=jax.ShapeDtypeStruct((L, nchunks * HEAD_DIM), F32),
        compiler_params=_params(("parallel", "parallel")),
    )(*args)


def _qk_bwd(dqk, proj, gtab, cos, sin, dproj, pieces, nchunks, rope_upto, cg, *, name):
    L = proj.shape[0]
    tl = _tile(L, (512, 256, 128))
    W = cg * HEAD_DIM
    pmap = _piece_map(tuple((a // cg, n // cg, p // cg) for a, n, p in pieces))

    def body(d_ref, p_ref, g_ref, cos_ref, sin_ref, _, o_ref, dg_ref):
        c, t = pl.program_id(0), pl.program_id(1)

        @pl.when(t == 0)
        def _():
            dg_ref[...] = jnp.zeros_like(dg_ref)

        for j in range(cg):
            cols = slice(j * HEAD_DIM, (j + 1) * HEAD_DIM)
            x = p_ref[:, cols]
            rstd = lax.rsqrt(_head_mean(x * x) + NORM_EPS)
            xhat = x * rstd
            dy = d_ref[:, cols]
            dy = jnp.where(c < rope_upto // cg, _rope(dy, cos_ref[...], -sin_ref[...]), dy)
            dxhat = dy * g_ref[:, cols]
            cm = _head_mean(dxhat * xhat)
            o_ref[:, cols] = (rstd * (dxhat - xhat * cm)).astype(BF16)
            dg_ref[:, cols] += jnp.sum(dy * xhat, axis=0, keepdims=True)

    pos = pl.BlockSpec((tl, HEAD_DIM), lambda c, t: (t, 0))
    gspec = pl.BlockSpec((None, 1, W), lambda c, t: (c, 0, 0))
    out, dg = pl.pallas_call(
        body, name=name, grid=(nchunks // cg, L // tl),
        in_specs=[pl.BlockSpec((tl, W), lambda c, t: (t, c)),
                  pl.BlockSpec((tl, W), lambda c, t: (t, pmap(c))), gspec, pos, pos,
                  pl.BlockSpec(memory_space=pl.ANY)],
        out_specs=[pl.BlockSpec((tl, W), lambda c, t: (t, pmap(c))), gspec],
        out_shape=[jax.ShapeDtypeStruct(dproj.shape, BF16), jax.ShapeDtypeStruct((nchunks // cg, 1, W), F32)],
        input_output_aliases={5: 0},
        compiler_params=_params(("parallel", "arbitrary")),
    )(dqk, proj, gtab, cos, sin, dproj)
    return out, dg


def _v_bwd(dv, dproj, *, name):
    L = dv.shape[0]
    tl = _tile(L, (512, 256, 128))
    pmap = _piece_map(tuple((a // 2, n // 2, p // 2) for a, n, p in V_PIECES))

    def body(d_ref, _, o_ref):
        o_ref[...] = d_ref[...].astype(BF16)

    return pl.pallas_call(
        body, name=name, grid=(L // tl, NK_CHUNKS // 2),
        in_specs=[pl.BlockSpec((tl, 2 * HEAD_DIM), lambda t, c: (t, c)), pl.BlockSpec(memory_space=pl.ANY)],
        out_specs=pl.BlockSpec((tl, 2 * HEAD_DIM), lambda t, c: (t, pmap(c))),
        out_shape=jax.ShapeDtypeStruct(dproj.shape, BF16),
        input_output_aliases={1: 0},
        compiler_params=_params(("parallel", "parallel")),
    )(dv, dproj)


def _band_geometry(L, dil, radius):
    n = L // dil
    bq = min(256, max(n // 2, 64), n)
    width = min(bq + 2 * radius, n)
    nsub = _tile(n // bq, (4, 2)) if dil == 1 else 1
    return n, bq, width, nsub


def _band_loop(dil, one):
    if dil == 1:
        one(0, 0)
    else:
        lax.fori_loop(0, dil, one, 0, unroll=min(dil, 4))


def _band_rows(dil, r, first, count):
    if dil == 1:
        return pl.ds(pl.multiple_of(first, 8), count)
    return pl.ds(r + first * dil, count, stride=dil)


def _band_mask(i, bq, width, radius, ws):
    qpos = i * bq + lax.broadcasted_iota(jnp.int32, (bq, width), 0)
    kpos = ws + lax.broadcasted_iota(jnp.int32, (bq, width), 1)
    return jnp.abs(kpos - qpos) <= radius


def _band_fwd(qn, kn, proj, *, dil, radius, nkv, group, q0, k0, v0, sink=None, name):
    L = qn.shape[0]
    n, bq, width, nsub = _band_geometry(L, dil, radius)
    tq = nsub * bq * dil
    nh = nkv * group

    def body(*refs):
        if sink is None:
            q_ref, k_ref, v_ref, o_ref, lse_ref = refs
        else:
            q_ref, k_ref, v_ref, s_ref, o_ref, lse_ref = refs
        for sb in range(nsub):
            block(sb, q_ref, k_ref, v_ref, None if sink is None else s_ref, o_ref, lse_ref)

    def block(sb, q_ref, k_ref, v_ref, s_ref, o_ref, lse_ref):
        i = pl.program_id(2) * nsub + sb
        ws = jnp.clip(i * bq - radius, 0, n - width)
        valid = _band_mask(i, bq, width, radius, ws)

        def one(r, carry):
            qrows = _band_rows(dil, r, sb * bq, bq)
            krows = _band_rows(dil, r, ws, width)
            q = q_ref[qrows, :].astype(BF16)
            k = k_ref[krows, :].astype(BF16)
            v = v_ref[krows, :].astype(BF16)
            s = lax.dot_general(q, k, (((1,), (1,)), ((), ())), preferred_element_type=F32) * ATT_SCALE
            s = jnp.where(valid, s, NEG)
            m = jnp.max(s, axis=-1, keepdims=True)
            if sink is not None:
                m = jnp.maximum(m, s_ref[...][:, :1])
            p = jnp.exp(s - m)
            denom = jnp.sum(p, axis=-1, keepdims=True)
            if sink is not None:
                denom = denom + jnp.exp(s_ref[...][:, :1] - m)
            pn = (p / denom).astype(BF16)
            o_ref[qrows, :] = jnp.dot(pn, v, preferred_element_type=F32)
            lse_ref[qrows, :] = jnp.broadcast_to(m + jnp.log(denom), (bq, HEAD_DIM))
            return carry

        _band_loop(dil, one)

    qspec = pl.BlockSpec((tq, HEAD_DIM), lambda hk, g, i: (i, q0 + hk * group + g))
    in_specs = [qspec,
                pl.BlockSpec((L, HEAD_DIM), lambda hk, g, i: (0, k0 + hk)),
                pl.BlockSpec((L, HEAD_DIM), lambda hk, g, i: (0, v0 + hk))]
    args = [qn, kn, proj]
    if sink is not None:
        in_specs.append(pl.BlockSpec((None, 1, HEAD_DIM), lambda hk, g, i: (hk * group + g, 0, 0)))
        args.append(sink)
    ospec = pl.BlockSpec((tq, HEAD_DIM), lambda hk, g, i: (i, hk * group + g))
    return pl.pallas_call(
        body, name=name, grid=(nkv, group, n // (bq * nsub)), in_specs=in_specs, out_specs=[ospec, ospec],
        out_shape=[jax.ShapeDtypeStruct((L, nh * HEAD_DIM), F32)] * 2,
        compiler_params=_params(("parallel", "parallel", "arbitrary")),
    )(*args)


def _band_bwd(qn, kn, proj, do, o, lse, dq_buf, dk_buf, dv_buf, *, dil, radius, nkv, group, q0, k0, v0, o0,
              sink=None, name):
    L = qn.shape[0]
    n, bq, width, nsub = _band_geometry(L, dil, radius)
    tq = nsub * bq * dil
    nh = nkv * group
    n_in = 6 + (1 if sink is not None else 0)

    def body(*refs):
        q_ref, k_ref, v_ref, do_ref, o_ref, lse_ref = refs[:6]
        s_ref = refs[6] if sink is not None else None
        outs = refs[n_in + 3:]
        dq_ref, dk_ref, dv_ref = outs[:3]
        ds_ref = outs[3] if sink is not None else None
        g, step = pl.program_id(1), pl.program_id(2)

        @pl.when((g == 0) & (step == 0))
        def _():
            dk_ref[...] = jnp.zeros_like(dk_ref)
            dv_ref[...] = jnp.zeros_like(dv_ref)

        if sink is not None:
            @pl.when(step == 0)
            def _():
                ds_ref[...] = jnp.zeros_like(ds_ref)

        for sb in range(nsub):
            block(sb, q_ref, k_ref, v_ref, do_ref, o_ref, lse_ref, s_ref, dq_ref, dk_ref, dv_ref, ds_ref)

    def block(sb, q_ref, k_ref, v_ref, do_ref, o_ref, lse_ref, s_ref, dq_ref, dk_ref, dv_ref, ds_ref):
        i = pl.program_id(2) * nsub + sb
        ws = jnp.clip(i * bq - radius, 0, n - width)
        valid = _band_mask(i, bq, width, radius, ws)

        def one(r, carry):
            qrows = _band_rows(dil, r, sb * bq, bq)
            krows = _band_rows(dil, r, ws, width)
            q = q_ref[qrows, :].astype(BF16)
            k = k_ref[krows, :].astype(BF16)
            v = v_ref[krows, :].astype(BF16)
            dov = do_ref[qrows, :]
            lse_v = lse_ref[qrows, :][:, :1]
            delta = jnp.sum(dov * o_ref[qrows, :], axis=-1, keepdims=True)
            dob = dov.astype(BF16)
            s = lax.dot_general(q, k, (((1,), (1,)), ((), ())), preferred_element_type=F32) * ATT_SCALE
            p = jnp.where(valid, jnp.exp(s - lse_v), 0.0)
            dp = lax.dot_general(dob, v, (((1,), (1,)), ((), ())), preferred_element_type=F32)
            dsb = (p * (dp - delta)).astype(BF16)
            dq_ref[qrows, :] = jnp.dot(dsb, k, preferred_element_type=F32) * ATT_SCALE
            dk_ref[krows, :] += lax.dot_general(dsb, q, (((0,), (0,)), ((), ())),
                                                preferred_element_type=F32) * ATT_SCALE
            dv_ref[krows, :] += lax.dot_general(p.astype(BF16), dob, (((0,), (0,)), ((), ())),
                                                preferred_element_type=F32)
            if sink is not None:
                ps = jnp.exp(s_ref[...][:, :1] - lse_v)
                ds_ref[...] += jnp.broadcast_to(jnp.sum(-ps * delta, axis=0, keepdims=True), (1, HEAD_DIM))
            return carry

        _band_loop(dil, one)

    hspec = pl.BlockSpec((tq, HEAD_DIM), lambda hk, g, i: (i, o0 + hk * group + g))
    qspec = pl.BlockSpec((tq, HEAD_DIM), lambda hk, g, i: (i, q0 + hk * group + g))
    kspec = pl.BlockSpec((L, HEAD_DIM), lambda hk, g, i: (0, k0 + hk))
    any_spec = pl.BlockSpec(memory_space=pl.ANY)
    in_specs = [qspec, kspec, pl.BlockSpec((L, HEAD_DIM), lambda hk, g, i: (0, v0 + hk)), hspec, hspec, hspec]
    args = [qn, kn, proj, do, o, lse]
    if sink is not None:
        in_specs.append(pl.BlockSpec((None, 1, HEAD_DIM), lambda hk, g, i: (hk * group + g, 0, 0)))
        args.append(sink)
    in_specs += [any_spec] * 3
    args += [dq_buf, dk_buf, dv_buf]
    out_specs = [qspec, kspec, kspec]
    out_shape = [jax.ShapeDtypeStruct(dq_buf.shape, F32), jax.ShapeDtypeStruct(dk_buf.shape, F32),
                 jax.ShapeDtypeStruct(dv_buf.shape, F32)]
    if sink is not None:
        out_specs.append(pl.BlockSpec((None, 1, HEAD_DIM), lambda hk, g, i: (hk * group + g, 0, 0)))
        out_shape.append(jax.ShapeDtypeStruct((nh, 1, HEAD_DIM), F32))
    return pl.pallas_call(
        body, name=name, grid=(nkv, group, n // (bq * nsub)), in_specs=in_specs, out_specs=out_specs,
        out_shape=out_shape,
        input_output_aliases={n_in: 0, n_in + 1: 1, n_in + 2: 2},
        compiler_params=_params(("parallel", "arbitrary", "arbitrary")),
    )(*args)


def _combine_b(os_, lses, *, name):
    L, W = os_[0].shape
    tl = _tile(L, (256, 128))

    def body(o0, o1, o2, l0, l1, l2, out_ref, lt_ref):
        a, b, c = l0[...], l1[...], l2[...]
        m = jnp.maximum(jnp.maximum(a, b), c)
        ea, eb, ec = jnp.exp(a - m), jnp.exp(b - m), jnp.exp(c - m)
        tot = ea + eb + ec
        out_ref[...] = (ea * o0[...] + eb * o1[...] + ec * o2[...]) / tot
        lt_ref[...] = m + jnp.log(tot)

    blk = pl.BlockSpec((tl, W), lambda t: (t, 0))
    return pl.pallas_call(
        body, name=name, grid=(L // tl,), in_specs=[blk] * 6, out_specs=[blk, blk],
        out_shape=[jax.ShapeDtypeStruct((L, W), F32)] * 2, compiler_params=_params(("parallel",)),
    )(*os_, *lses)


C_QROWS = 4
C_KROWS = C_QROWS + C_WIN_ROWS
C_QUERIES, C_KEYS = C_QROWS * GRID_W, C_KROWS * GRID_W
_C_KIND_OFFSETS = (C_WIN_ROWS - 1, C_WIN_ROWS - 1 - C_WIN_ROWS // 2, C_WIN_ROWS - 1 - (C_KROWS - C_QROWS))


def _c_geometry(L):
    rows = L // GRID_W
    assert rows >= C_KROWS and rows % C_QROWS == 0
    return rows


def _c_bias_tiles(bias_t):
    cq = np.arange(GRID_W)[:, None]
    ck = np.arange(GRID_W)[None, :]
    start = np.clip(cq - C_WIN_COLS // 2, 0, GRID_W - C_WIN_COLS)
    masked = jnp.where(jnp.asarray((ck >= start) & (ck < start + C_WIN_COLS)), bias_t, NEG)
    blank = jnp.full((C_HEADS, GRID_W, GRID_W), NEG, F32)
    kinds = []
    for kind in range(3):
        off = _C_KIND_OFFSETS[kind]
        row_blocks = []
        for a in range(C_QROWS):
            lo = (0, a, C_KROWS - C_WIN_ROWS)[kind]
            row_blocks.append(jnp.concatenate(
                [masked[:, b - a + off] if lo <= b < lo + C_WIN_ROWS else blank for b in range(C_KROWS)], axis=-1))
        kinds.append(jnp.concatenate(row_blocks, axis=-2))
    return jnp.stack(kinds, axis=1)


def _c_block(g, rows):
    r0 = g * C_QROWS
    k0 = jnp.clip(r0 - C_WIN_ROWS // 2, 0, rows - C_KROWS)
    kind = jnp.where(g == 0, 0, jnp.where(g == rows // C_QROWS - 1, 2, 1))
    return k0, kind, k0 - r0 + (C_WIN_ROWS - 1)


def _c_fwd(qn, kn, proj, tiles, *, name):
    L = qn.shape[0]
    rows = _c_geometry(L)

    nsub = _tile(rows // C_QROWS, (2,))

    def body(q_ref, k_ref, v_ref, t_ref, o_ref, lse_ref):
        for sb in range(nsub):
            k0, kind, _ = _c_block(pl.program_id(1) * nsub + sb, rows)
            krows = pl.ds(pl.multiple_of(k0 * GRID_W, GRID_W), C_KEYS)
            qrows = pl.ds(sb * C_QUERIES, C_QUERIES)
            q = q_ref[qrows, :].astype(BF16)
            k = k_ref[krows, :].astype(BF16)
            v = v_ref[krows, :].astype(BF16)
            s = lax.dot_general(q, k, (((1,), (1,)), ((), ())), preferred_element_type=F32) * ATT_SCALE + t_ref[kind]
            m = jnp.max(s, axis=-1, keepdims=True)
            p = jnp.exp(s - m)
            denom = jnp.sum(p, axis=-1, keepdims=True)
            o_ref[qrows, :] = jnp.dot((p / denom).astype(BF16), v, preferred_element_type=F32)
            lse_ref[qrows, :] = jnp.broadcast_to(m + jnp.log(denom), (C_QUERIES, HEAD_DIM))

    ospec = pl.BlockSpec((nsub * C_QUERIES, HEAD_DIM), lambda h, g: (g, h))
    return pl.pallas_call(
        body, name=name, grid=(C_HEADS, rows // (C_QROWS * nsub)),
        in_specs=[pl.BlockSpec((nsub * C_QUERIES, HEAD_DIM), lambda h, g: (g, 20 + h)),
                  pl.BlockSpec((L, HEAD_DIM), lambda h, g: (0, 14 + h)),
                  pl.BlockSpec((L, HEAD_DIM), lambda h, g: (0, PC_VC + h)),
                  pl.BlockSpec((None, 3, C_QUERIES, C_KEYS), lambda h, g: (h, 0, 0, 0))],
        out_specs=[ospec, ospec],
        out_shape=[jax.ShapeDtypeStruct((L, C_HEADS * HEAD_DIM), F32)] * 2,
        compiler_params=_params(("parallel", "arbitrary")),
    )(qn, kn, proj, tiles)


def _c_bwd(qn, kn, proj, tiles, do, o, lse, dq_buf, dk_buf, dv_buf, *, name):
    L = qn.shape[0]
    rows = _c_geometry(L)

    nsub = _tile(rows // C_QROWS, (2,))

    def body(q_ref, k_ref, v_ref, t_ref, do_ref, o_ref, lse_ref, _a, _b, _c, dq_ref, dk_ref, dv_ref, dt_ref):
        @pl.when(pl.program_id(1) == 0)
        def _():
            dk_ref[...] = jnp.zeros_like(dk_ref)
            dv_ref[...] = jnp.zeros_like(dv_ref)
            dt_ref[...] = jnp.zeros_like(dt_ref)

        for sb in range(nsub):
            k0, kind, off = _c_block(pl.program_id(1) * nsub + sb, rows)
            krows = pl.ds(pl.multiple_of(k0 * GRID_W, GRID_W), C_KEYS)
            qrows = pl.ds(sb * C_QUERIES, C_QUERIES)
            q = q_ref[qrows, :].astype(BF16)
            k = k_ref[krows, :].astype(BF16)
            v = v_ref[krows, :].astype(BF16)
            dov = do_ref[qrows, :]
            dob = dov.astype(BF16)
            delta = jnp.sum(dov * o_ref[qrows, :], axis=-1, keepdims=True)
            s = lax.dot_general(q, k, (((1,), (1,)), ((), ())), preferred_element_type=F32) * ATT_SCALE + t_ref[kind]
            p = jnp.exp(s - lse_ref[qrows, :][:, :1])
            dp = lax.dot_general(dob, v, (((1,), (1,)), ((), ())), preferred_element_type=F32)
            ds = p * (dp - delta)
            for a in range(C_QROWS):
                for b in range(C_KROWS):
                    rel = jnp.clip(b - a + off, 0, C_NREL - 1)
                    dt_ref[rel] += ds[a * GRID_W:(a + 1) * GRID_W, b * GRID_W:(b + 1) * GRID_W]
            dsb = ds.astype(BF16)
            dq_ref[qrows, :] = jnp.dot(dsb, k, preferred_element_type=F32) * ATT_SCALE
            dk_ref[krows, :] += lax.dot_general(dsb, q, (((0,), (0,)), ((), ())),
                                                preferred_element_type=F32) * ATT_SCALE
            dv_ref[krows, :] += lax.dot_general(p.astype(BF16), dob, (((0,), (0,)), ((), ())),
                                                preferred_element_type=F32)

    hspec = pl.BlockSpec((nsub * C_QUERIES, HEAD_DIM), lambda h, g: (g, h))
    qspec = pl.BlockSpec((nsub * C_QUERIES, HEAD_DIM), lambda h, g: (g, 20 + h))
    kspec = pl.BlockSpec((L, HEAD_DIM), lambda h, g: (0, 14 + h))
    any_spec = pl.BlockSpec(memory_space=pl.ANY)
    return pl.pallas_call(
        body, name=name, grid=(C_HEADS, rows // (C_QROWS * nsub)),
        in_specs=[qspec, kspec, pl.BlockSpec((L, HEAD_DIM), lambda h, g: (0, PC_VC + h)),
                  pl.BlockSpec((None, 3, C_QUERIES, C_KEYS), lambda h, g: (h, 0, 0, 0)),
                  hspec, hspec, hspec, any_spec, any_spec, any_spec],
        out_specs=[qspec, kspec, kspec,
                   pl.BlockSpec((None, C_NREL, GRID_W, GRID_W), lambda h, r: (h, 0, 0, 0))],
        out_shape=[jax.ShapeDtypeStruct(dq_buf.shape, F32), jax.ShapeDtypeStruct(dk_buf.shape, F32),
                   jax.ShapeDtypeStruct(dv_buf.shape, F32),
                   jax.ShapeDtypeStruct((C_HEADS, C_NREL, GRID_W, GRID_W), F32)],
        input_output_aliases={7: 0, 8: 1, 9: 2},
        compiler_params=_params(("parallel", "arbitrary")),
    )(qn, kn, proj, tiles, do, o, lse, dq_buf, dk_buf, dv_buf)


def _c_expand_matrix():
    cq = np.arange(GRID_W)[:, None]
    ck = np.arange(GRID_W)[None, :]
    d = (ck - cq + (C_WIN_COLS - 1)).reshape(-1)
    e = np.zeros((GRID_W * GRID_W, HEAD_DIM), np.float32)
    okd = (d >= 0) & (d < C_NCOL)
    e[np.arange(GRID_W * GRID_W)[okd], d[okd]] = 1.0
    return e


def _peer(p):
    return (p // 4, (p // 2) % 2, p % 2)


def _my_index():
    return 4 * lax.axis_index("x") + 2 * lax.axis_index("y") + lax.axis_index("c")


HBM_SPEC = pl.BlockSpec(memory_space=pltpu.HBM)
SEM_SPEC = pl.BlockSpec(memory_space=pltpu.SEMAPHORE)
ANY_SPEC = pl.BlockSpec(memory_space=pl.ANY)
DATAFLOW = pltpu.SideEffectType.DATAFLOW_SIDE_EFFECTING


_EXCHANGE_TRANSFERS = {"scatter": N_DEV - 1, "gather1": 4, "gather2": 3}


def _exchange_views(mode, kinds, arrays):
    nw = len(kinds)
    gather = mode != "scatter"
    if gather:
        rows = [a.shape[0] // N_DEV for a in arrays[:nw]]
    else:
        rows = [a.shape[1] // N_DEV for a in arrays[:nw]]

    def gather_slot(ref, w, who):
        return ref.at[who] if kinds[w] == "col" else ref.at[pl.ds(who * rows[w], rows[w]), :]

    x, y, c = lax.axis_index("x"), lax.axis_index("y"), lax.axis_index("c")
    me = 4 * x + 2 * y + c
    chips = [(1 - x, y), (x, 1 - y), (1 - x, 1 - y)]

    def index(px, py, pc):
        return 4 * px + 2 * py + pc

    if mode == "scatter":
        plan = [(_peer((me + off) % N_DEV), (me + off) % N_DEV, (me + N_DEV - off) % N_DEV)
                for off in range(1, N_DEV)]
    elif mode == "gather1":
        plan = [((x, y, 1 - c), me, index(x, y, 1 - c))] + [((px, py, c), me, index(px, py, c)) for px, py in chips]
    else:
        plan = [((x, y, 1 - c), index(px, py, c), index(px, py, 1 - c)) for px, py in chips]

    def src(ref, w, j):
        sent = plan[j][1]
        if gather:
            return gather_slot(ref, w, sent)
        return ref.at[sent] if kinds[w] == "col" else ref.at[0, pl.ds(sent * rows[w], rows[w]), :]

    def dst(ref, w, j):
        return gather_slot(ref, w, plan[j][1]) if gather else ref.at[me]

    def arrival(ref, w, j):
        return gather_slot(ref, w, plan[j][2]) if gather else ref.at[plan[j][2]]

    return [p[0] for p in plan], src, dst, arrival


def _place_cast(w, layer, kind, *, name):
    _, R, C = w.shape
    tr = _tile(R, (256, 128, 64, 32, 16))

    def body(w_ref, o_ref):
        o_ref[...] = w_ref[...].astype(BF16)

    if kind == "col":
        out_shape = jax.ShapeDtypeStruct((N_DEV, R, C), BF16)
        out_spec = pl.BlockSpec((None, tr, C), lambda t: (_my_index(), t, 0))
    else:
        out_shape = jax.ShapeDtypeStruct((N_DEV * R, C), BF16)
        out_spec = pl.BlockSpec((tr, C), lambda t: (_my_index() * (R // tr) + t, 0))
    return pl.pallas_call(
        body, name=name, grid=(R // tr,), in_specs=[pl.BlockSpec((None, tr, C), lambda t: (layer, t, 0))],
        out_specs=out_spec, out_shape=out_shape, compiler_params=_params(("parallel",)),
    )(w)


def _exchange_start(mode, srcs, lands, kinds, after, *, name):
    nw = len(lands)
    ns = len(srcs)
    arrays = list(srcs) + list(lands)
    na = len(arrays)
    nx = _EXCHANGE_TRANSFERS[mode]

    def body(*refs):
        l_refs = refs[ns:ns + nw]
        s_refs = refs[:ns] if ns else l_refs
        send_sems, recv_sems = refs[ns + nw + 1], refs[ns + nw + 2]
        token = refs[-1]
        peers, src, dst, _ = _exchange_views(mode, kinds, arrays)
        for j in range(nx):
            for w in range(nw):
                pltpu.make_async_remote_copy(src(s_refs[w], w, j), dst(l_refs[w], w, j),
                                             send_sems.at[w * nx + j], recv_sems.at[w * nx + j],
                                             device_id=peers[j], device_id_type=MESH).start()
        token[...] = jnp.zeros_like(token)

    outs = pl.pallas_call(
        body, name=name,
        out_shape=(pltpu.SemaphoreType.DMA((nw * nx,)), pltpu.SemaphoreType.DMA((nw * nx,)),
                   *[pltpu.HBM(a.shape, a.dtype) for a in arrays], jax.ShapeDtypeStruct((8, 128), F32)),
        in_specs=[HBM_SPEC] * na + [ANY_SPEC],
        out_specs=(SEM_SPEC, SEM_SPEC, *([HBM_SPEC] * na), pl.BlockSpec(memory_space=pltpu.VMEM)),
        input_output_aliases={k: 2 + k for k in range(na)},
        compiler_params=pltpu.CompilerParams(has_side_effects=DATAFLOW),
    )(*[pltpu.with_memory_space_constraint(a, pltpu.HBM) for a in arrays], after)
    return outs[0], outs[1], outs[2:2 + ns], outs[2 + ns:2 + na], outs[-1]


def _exchange_wait(mode, started, kinds, after, *, name):
    send_sems, recv_sems, srcs, lands, _ = started
    nw = len(lands)
    ns = len(srcs)
    arrays = list(srcs) + list(lands)
    na = len(arrays)
    nx = _EXCHANGE_TRANSFERS[mode]

    def body(*refs):
        l_refs = refs[ns:na]
        s_refs = refs[:ns] if ns else l_refs
        send_ref, recv_ref = refs[na], refs[na + 1]
        peers, src, _, arrival = _exchange_views(mode, kinds, arrays)
        for j in range(nx):
            for w in range(nw):
                cp = pltpu.make_async_remote_copy(src(s_refs[w], w, j), arrival(l_refs[w], w, j),
                                                  send_ref.at[w * nx + j], recv_ref.at[w * nx + j],
                                                  device_id=peers[j], device_id_type=MESH)
                cp.wait_send()
                cp.wait_recv()

    outs = pl.pallas_call(
        body, name=name, out_shape=[pltpu.HBM(a.shape, a.dtype) for a in arrays],
        in_specs=[HBM_SPEC] * na + [SEM_SPEC, SEM_SPEC, ANY_SPEC], out_specs=[HBM_SPEC] * na,
        input_output_aliases={k: k for k in range(na)},
        compiler_params=pltpu.CompilerParams(has_side_effects=DATAFLOW),
    )(*arrays, send_sems, recv_sems, after)
    return outs[:ns], outs[ns:]


def _all_reduce_small(x):
    R = x.shape[0]

    def body(x_ref, o_ref, gath, send_sems, recv_sems):
        me = _my_index()
        gath[me] = x_ref[...]
        sends = []
        for off in range(1, N_DEV):
            to = (me + off) % N_DEV
            cp = pltpu.make_async_remote_copy(x_ref, gath.at[me], send_sems.at[off], recv_sems.at[off],
                                              device_id=_peer(to), device_id_type=MESH)
            cp.start()
            sends.append(cp)
        for off in range(1, N_DEV):
            frm = (me + N_DEV - off) % N_DEV
            pltpu.make_async_remote_copy(x_ref, gath.at[frm], send_sems.at[off], recv_sems.at[off],
                                         device_id=_peer(frm), device_id_type=MESH).wait_recv()
        for cp in sends:
            cp.wait_send()
        acc = gath[0]
        for s in range(1, N_DEV):
            acc = acc + gath[s]
        o_ref[...] = acc

    vm = pl.BlockSpec(memory_space=pltpu.VMEM)
    return pl.pallas_call(
        body, name="all_reduce_small", in_specs=[vm], out_specs=vm, out_shape=jax.ShapeDtypeStruct((R, 128), F32),
        scratch_shapes=[pltpu.VMEM((N_DEV, R, 128), F32), pltpu.SemaphoreType.DMA((N_DEV,)),
                        pltpu.SemaphoreType.DMA((N_DEV,))],
        compiler_params=pltpu.CompilerParams(has_side_effects=True),
    )(x)


def _adamw_math(w, g, m, v):
    m = ADAM_B1 * m + (1.0 - ADAM_B1) * g
    v = ADAM_B2 * v + (1.0 - ADAM_B2) * (g * g)
    m_hat = m / (1.0 - ADAM_B1 ** ADAM_STEP)
    v_hat = v / (1.0 - ADAM_B2 ** ADAM_STEP)
    delta = -ADAM_LR * (m_hat / (jnp.sqrt(v_hat) + ADAM_EPS) + ADAM_WD * w)
    return delta, m, v


def _adamw_layer(recv, own, kind, w, m, v, outs, layer, dep, *, name):
    nl, R, C = w.shape
    tr = _tile(R, (128, 64, 32, 16))

    def body(r_ref, o_ref, w_ref, m_ref, v_ref, _0, _1, _2, _3, _dep, g_out, d_out, m_out, v_out, token):
        token[...] = jnp.zeros_like(token)
        me = _my_index()
        mine = o_ref[...].astype(F32)
        g = jnp.where(me == 0, mine, r_ref[0].astype(F32))
        for s in range(1, N_DEV):
            g = g + jnp.where(me == s, mine, r_ref[s].astype(F32))
        delta, mn, vn = _adamw_math(w_ref[...], g, m_ref[...], v_ref[...])
        g_out[...] = g
        d_out[...] = delta
        m_out[...] = mn
        v_out[...] = vn

    if kind == "col":
        own_spec = pl.BlockSpec((None, tr, C), lambda t: (_my_index(), t, 0))
    else:
        own_spec = pl.BlockSpec((None, tr, C), lambda t: (0, _my_index() * (R // tr) + t, 0))
    wspec = pl.BlockSpec((None, tr, C), lambda t: (layer, t, 0))
    res = pl.pallas_call(
        body, name=name, grid=(R // tr,),
        in_specs=[pl.BlockSpec((N_DEV, tr, C), lambda t: (0, t, 0)), own_spec] + [wspec] * 3 + [ANY_SPEC] * 5,
        out_specs=[wspec] * 4 + [pl.BlockSpec((8, 128), lambda t: (0, 0))],
        out_shape=[jax.ShapeDtypeStruct((nl, R, C), F32)] * 4 + [jax.ShapeDtypeStruct((8, 128), F32)],
        input_output_aliases={5: 0, 6: 1, 7: 2, 8: 3},
        compiler_params=_params(("arbitrary",)),
    )(recv, own, w, m, v, *outs, dep)
    return res[:4], res[4]


def _adamw_small(g, w, m, v):
    def body(g_ref, w_ref, m_ref, v_ref, d_out, m_out, v_out):
        delta, mn, vn = _adamw_math(w_ref[...], g_ref[...], m_ref[...], v_ref[...])
        d_out[...] = delta
        m_out[...] = mn
        v_out[...] = vn

    return pl.pallas_call(body, name="adamw_small", out_shape=[jax.ShapeDtypeStruct(g.shape, F32)] * 3)(g, w, m, v)


def _pack(arrays, rows):
    flat = jnp.concatenate([a.reshape(-1) for a in arrays])
    return jnp.pad(flat, (0, rows * 128 - flat.shape[0])).reshape(rows, 128)


def _unpack(packed, shapes):
    flat = packed.reshape(-1)
    out, pos = [], 0
    for s in shapes:
        size = int(np.prod(s))
        out.append(flat[pos:pos + size].reshape(s))
        pos += size
    return out


def kernel(x, norm1_g, w_in, qk_norm_g, sink_a, rpb_c, w_br_a, w_br_b, w_br_c, w_o, norm2_g, w_gate_up, w_down, loss_target, m_norm1_g, m_w_in, m_qk_norm_g, m_sink_a, m_rpb_c, m_w_br_a, m_w_br_b, m_w_br_c, m_w_o, m_norm2_g, m_w_gate_up, m_w_down, v_norm1_g, v_w_in, v_qk_norm_g, v_sink_a, v_rpb_c, v_w_br_a, v_w_br_b, v_w_br_c, v_w_o, v_norm2_g, v_w_gate_up, v_w_down):
    nl = w_in.shape[0]
    L, D = x.shape[1], x.shape[2]
    x0 = x.reshape(L, D)
    tgt = loss_target.reshape(L, D)

    big = [w_in, w_br_a, w_br_b, w_br_c, w_o, w_gate_up, w_down]
    kinds = ["col", "col", "col", "col", "row", "col", "row"]

    big_names = ["w_in", "w_br_a", "w_br_b", "w_br_c", "w_o", "w_gate_up", "w_down"]
    ALL = list(range(len(big)))
    REST = ALL[1:]

    def gather_place(i):
        return [_place_cast(w, i, k, name="gather_place_" + n) for w, k, n in zip(big, kinds, big_names)]

    def gather_start(mode, lands, sub, after, tag):
        return _exchange_start(mode, [], lands, [kinds[j] for j in sub], after, name=mode + "_start" + tag)

    def gather_wait(mode, started, sub, after, tag):
        return _exchange_wait(mode, started, [kinds[j] for j in sub], after, name=mode + "_wait" + tag)[1]

    def matmul_views(lands, sub):
        return [g.reshape((N_DEV, 1) + g.shape[1:]) if kinds[j] == "col" else g.reshape((1, 1) + g.shape)
                for g, j in zip(lands, sub)]

    half = HEAD_DIM // 2
    inv_freq = ROPE_THETA ** (-jnp.arange(half, dtype=F32) * 2.0 / HEAD_DIM)
    ang = jnp.arange(L, dtype=F32)[:, None] * inv_freq[None, :]
    cos = jnp.concatenate([jnp.cos(ang), jnp.cos(ang)], axis=-1)
    sin = jnp.concatenate([-jnp.sin(ang), jnp.sin(ang)], axis=-1)
    expand = jnp.asarray(_c_expand_matrix(), BF16)
    expand_t = jnp.asarray(_c_expand_matrix().T, BF16)

    def gain_tables(i):
        g = qk_norm_g[i]
        gq = jnp.concatenate([jnp.tile(g[0][None], (8, 1)), jnp.tile(g[2][None], (12, 1)), jnp.tile(g[4][None], (8, 1))])
        gk = jnp.concatenate([jnp.tile(g[1][None], (2, 1)), jnp.tile(g[3][None], (12, 1)), jnp.tile(g[5][None], (8, 1))])
        return (gq.reshape(NQ_CHUNKS // Q_CG, 1, Q_CG * HEAD_DIM), gk.reshape(NK_CHUNKS // K_CG, 1, K_CG * HEAD_DIM))

    def bias_table(i):
        rp = jnp.pad(rpb_c[i].reshape(C_HEADS * C_NREL, C_NCOL), ((0, 0), (0, HEAD_DIM - C_NCOL)))
        t = _exact_mm(rp, expand_t, name="c_bias_expand")
        return _c_bias_tiles(t.reshape(C_HEADS, C_NREL, GRID_W, GRID_W))

    def sink_table(i):
        return jnp.broadcast_to(sink_a[i][:, None, None], (A_Q_HEADS, 1, HEAD_DIM))

    saved = []
    gws = [None] * nl
    xi = x0
    lands0 = gather_place(0)
    lvl1 = gather_start("gather1", lands0[:1], [0], x0, "_first")
    lvl2 = gather_start("gather2", gather_wait("gather1", lvl1, [0], x0, "_first"), [0], x0, "_first")
    gws[0] = matmul_views(gather_wait("gather2", lvl2, [0], x0, "_first"), [0])
    rest1 = gather_start("gather1", lands0[1:], REST, gws[0][0], "_rest")
    dep = rest1[4]
    for i in range(nl):
        qk_dep = None
        if i >= 1 and i + 1 < nl:
            nxt1 = gather_start("gather1", gather_place(i + 1), ALL, dep, "")
            dep = nxt1[4]
        gw_in = gws[i][0]
        gq, gk = gain_tables(i)
        bias_t = bias_table(i)
        sink = sink_table(i)
        h1 = _rms_fwd(xi, norm1_g[i][None], dep, name="rms1_fwd")
        proj = _mm_nn(h1, gw_in, 0, out_dtype=F32, name="proj_fwd")
        if i == 0:
            rest2 = gather_start("gather2", gather_wait("gather1", rest1, REST, proj, "_rest"), REST, proj, "_rest")
            qk_dep = rest2[4]
            if nl > 1:
                nxt1 = gather_start("gather1", gather_place(1), ALL, rest2[4], "")
                qk_dep = nxt1[4]
        qn = _qk_fwd(proj, gq, cos, sin, Q_PIECES, NQ_CHUNKS, Q_ROPE_UPTO, Q_CG, name="qnorm_fwd", dep=qk_dep)
        kn = _qk_fwd(proj, gk, cos, sin, K_PIECES, NK_CHUNKS, K_ROPE_UPTO, K_CG, name="knorm_fwd")
        oa, lse_a = _band_fwd(qn, kn, proj, dil=1, radius=A_RADIUS, nkv=A_KV_HEADS, group=A_GROUP,
                              q0=0, k0=0, v0=PC_VA, sink=sink, name="attn_a_fwd")
        obs, lbs = [], []
        for g, (window, dil) in enumerate(B_PATTERNS):
            o_g, l_g = _band_fwd(qn, kn, proj, dil=dil, radius=window // (2 * dil), nkv=B_HG, group=1,
                                 q0=8 + g * B_HG, k0=2 + g * B_HG, v0=PC_VB + g * B_HG, name=f"attn_b{g}_fwd")
            obs.append(o_g)
            lbs.append(l_g)
        ob, lse_b = _combine_b(obs, lbs, name="attn_b_combine")
        oc, lse_c = _c_fwd(qn, kn, proj, bias_t, name="attn_c_fwd")
        if i == 0:
            gws[0] = gws[0] + matmul_views(gather_wait("gather2", rest2, REST, oc, "_rest"), REST)
        gws[i][1:4] = [_blocks_to_wide(g, name="br_wide_" + n) for g, n in zip(gws[i][1:4], "abc")]
        _, gw_a, gw_b, gw_c, gw_o, gw_gu, gw_d = gws[i]
        ta = _mm_nn(oa, gw_a, 0, out_dtype=BF16, name="br_a_fwd")
        tb = _mm_nn(ob, gw_b, 0, out_dtype=BF16, name="br_b_fwd")
        tc = _mm_nn(oc, gw_c, 0, out_dtype=BF16, name="br_c_fwd")
        merged = _gate_fwd(proj, ta, tb, tc, name="gate_fwd")
        x1 = _mm_nn(merged, gw_o, 0, out_dtype=F32, name="wo_fwd", res=xi)
        dep = x1
        if i + 1 < nl:
            nxt2 = gather_start("gather2", gather_wait("gather1", nxt1, ALL, x1, ""), ALL, x1, "")
            dep = nxt2[4]
        h2 = _rms_fwd(x1, norm2_g[i][None], dep, name="rms2_fwd")
        gu = _mm_nn(h2, gw_gu, 0, out_dtype=BF16, name="gate_up_fwd")
        act = _swiglu_fwd(gu, name="swiglu_fwd")
        x2 = _mm_nn(act, gw_d, 0, out_dtype=F32, name="down_fwd", res=x1)
        saved.append(dict(x=xi, h1=h1, proj=proj, qn=qn, kn=kn, oa=oa, lse_a=lse_a, ob=ob, lse_b=lse_b, oc=oc,
                          lse_c=lse_c, ta=ta, tb=tb, tc=tc, merged=merged, x1=x1, h2=h2, gu=gu, act=act,
                          gq=gq, gk=gk, bias_t=bias_t, sink=sink))
        xi = x2
        dep = x2
        if i + 1 < nl:
            gws[i + 1] = matmul_views(gather_wait("gather2", nxt2, ALL, x2, ""), ALL)

    dx, dxb, loss_row = _loss(xi, tgt, name="loss")

    def scatter_start(grads, sub, after, tag):
        lands = []
        for g, j in zip(grads, sub):
            shape = g.shape if kinds[j] == "col" else (N_DEV, g.shape[1] // N_DEV, g.shape[2])
            lands.append(lax.empty(shape, BF16))
        return _exchange_start("scatter", grads, lands, [kinds[j] for j in sub], after, name="scatter_start" + tag)

    def scatter_wait(pair, after):
        own_a, recv_a = _exchange_wait("scatter", pair[0], [kinds[0]], after, name="scatter_wait_in")
        own_b, recv_b = _exchange_wait("scatter", pair[1], [kinds[j] for j in REST], after, name="scatter_wait_rest")
        return list(recv_a) + list(recv_b), list(own_a) + list(own_b)

    small_grads = [None] * nl
    recv = [None] * nl
    own = [None] * nl
    pending = None
    for i in reversed(range(nl)):
        s = saved[i]
        gw_in, gw_a, gw_b, gw_c, gw_o, gw_gu, gw_d = gws[i]
        dact = _mm_nt(dxb, gw_d, 0, out_dtype=BF16, name="down_bwd_x", dep=None if pending is None else pending[0][4])
        g_down = _mm_tn(s["act"], dxb, 1, name="down_bwd_w")
        dgu = _swiglu_bwd(s["gu"], dact, name="swiglu_bwd")
        g_gu = _mm_tn(s["h2"], dgu, N_DEV, name="gate_up_bwd_w")
        dh2 = _mm_nt(dgu, gw_gu, 0, out_dtype=F32, name="gate_up_bwd_x")
        dx1, dx1b, dg2 = _rms_bwd(s["x1"], norm2_g[i][None], dh2, dx, name="rms2_bwd")
        dmerged = _mm_nt(dx1b, gw_o, 0, out_dtype=F32, name="wo_bwd_x")
        g_o = _mm_tn(s["merged"], dx1b, 1, name="wo_bwd_w")
        dta, dtb, dtc, dproj = _gate_bwd(s["proj"], s["ta"], s["tb"], s["tc"], dmerged, name="gate_bwd")
        g_a = _wide_to_blocks(_mm_tn(s["oa"], dta, 1, name="br_a_bwd_w"), N_DEV, name="br_blocks_a")
        g_b = _wide_to_blocks(_mm_tn(s["ob"], dtb, 1, name="br_b_bwd_w"), N_DEV, name="br_blocks_b")
        g_c = _wide_to_blocks(_mm_tn(s["oc"], dtc, 1, name="br_c_bwd_w"), N_DEV, name="br_blocks_c")
        rest = scatter_start([g_a, g_b, g_c, g_o, g_gu, g_down], REST, g_c, "_rest")
        doa = _mm_nt(dta, gw_a, 0, out_dtype=F32, name="br_a_bwd_x", dep=rest[4])
        dob = _mm_nt(dtb, gw_b, 0, out_dtype=F32, name="br_b_bwd_x")
        doc = _mm_nt(dtc, gw_c, 0, out_dtype=F32, name="br_c_bwd_x")
        dq_buf = lax.empty((L, NQ_CHUNKS * HEAD_DIM), F32)
        dk_buf = lax.empty((L, NK_CHUNKS * HEAD_DIM), F32)
        dv_buf = lax.empty((L, NK_CHUNKS * HEAD_DIM), F32)
        dq_buf, dk_buf, dv_buf, dsink = _band_bwd(
            s["qn"], s["kn"], s["proj"], doa, s["oa"], s["lse_a"], dq_buf, dk_buf, dv_buf, dil=1, radius=A_RADIUS,
            nkv=A_KV_HEADS, group=A_GROUP, q0=0, k0=0, v0=PC_VA, o0=0, sink=s["sink"], name="attn_a_bwd")
        for g, (window, dil) in enumerate(B_PATTERNS):
            dq_buf, dk_buf, dv_buf = _band_bwd(
                s["qn"], s["kn"], s["proj"], dob, s["ob"], s["lse_b"], dq_buf, dk_buf, dv_buf, dil=dil,
                radius=window // (2 * dil), nkv=B_HG, group=1, q0=8 + g * B_HG, k0=2 + g * B_HG,
                v0=PC_VB + g * B_HG, o0=0, name=f"attn_b{g}_bwd")
        dq_buf, dk_buf, dv_buf, dbias_t = _c_bwd(s["qn"], s["kn"], s["proj"], s["bias_t"], doc, s["oc"], s["lse_c"],
                                                 dq_buf, dk_buf, dv_buf, name="attn_c_bwd")
        dproj, dgq = _qk_bwd(dq_buf, s["proj"], s["gq"], cos, sin, dproj, Q_PIECES, NQ_CHUNKS, Q_ROPE_UPTO, Q_CG,
                             name="qnorm_bwd")
        dproj, dgk = _qk_bwd(dk_buf, s["proj"], s["gk"], cos, sin, dproj, K_PIECES, NK_CHUNKS, K_ROPE_UPTO, K_CG,
                             name="knorm_bwd")
        dproj = _v_bwd(dv_buf, dproj, name="v_bwd")
        g_in = _mm_tn(s["h1"], dproj, N_DEV, name="proj_bwd_w")
        dh1 = _mm_nt(dproj, gw_in, 0, out_dtype=F32, name="proj_bwd_x")
        dx, dxb, dg1 = _rms_bwd(s["x"], norm1_g[i][None], dh1, dx1, name="rms1_bwd")
        if pending is not None:
            recv[i + 1], own[i + 1] = scatter_wait(pending, dx)
        pending = (scatter_start([g_in], [0], dx, "_in"), rest)

        drpb = _exact_mm(dbias_t.reshape(C_HEADS * C_NREL, GRID_W * GRID_W), expand, name="c_bias_reduce")
        dgq, dgk = dgq.reshape(NQ_CHUNKS, HEAD_DIM), dgk.reshape(NK_CHUNKS, HEAD_DIM)
        dqk_g = jnp.stack([dgq[0:8].sum(0), dgk[0:2].sum(0), dgq[8:20].sum(0), dgk[2:14].sum(0),
                           dgq[20:28].sum(0), dgk[14:22].sum(0)])
        small_grads[i] = (dg1.reshape(D), dqk_g, dsink[:, 0, 0],
                          drpb[:, :C_NCOL].reshape(C_HEADS, C_NREL, C_NCOL), dg2.reshape(D))

    small_names = [norm1_g, qk_norm_g, sink_a, rpb_c, norm2_g]
    small_m = [m_norm1_g, m_qk_norm_g, m_sink_a, m_rpb_c, m_norm2_g]
    small_v = [v_norm1_g, v_qk_norm_g, v_sink_a, v_rpb_c, v_norm2_g]
    shapes = [a.shape for a in small_names]
    total = sum(int(np.prod(sh)) for sh in shapes) + 128
    rows = -(-total // 1024) * 8
    stacked = [jnp.stack([small_grads[i][j] for i in range(nl)]) for j in range(5)]
    packed = _pack([loss_row.reshape(-1)] + stacked, rows)
    summed = _all_reduce_small(packed)
    loss = summed[0, 0]
    zero_row = jnp.zeros((128,), F32)
    d_s, m_s, v_s = _adamw_small(summed, _pack([zero_row] + small_names, rows), _pack([zero_row] + small_m, rows),
                                 _pack([zero_row] + small_v, rows))
    shapes1 = [(128,)] + shapes
    g_small = _unpack(summed, shapes1)[1:]
    d_small = _unpack(d_s, shapes1)[1:]
    m_small = _unpack(m_s, shapes1)[1:]
    v_small = _unpack(v_s, shapes1)[1:]

    big_m = [m_w_in, m_w_br_a, m_w_br_b, m_w_br_c, m_w_o, m_w_gate_up, m_w_down]
    big_v = [v_w_in, v_w_br_a, v_w_br_b, v_w_br_c, v_w_o, v_w_gate_up, v_w_down]
    big_out = [[lax.empty(w.shape, F32) for _ in range(4)] for w in big]
    token = pending[0][4]
    for i in list(range(nl - 1, 0, -1)) + [0]:
        if i == 0:
            recv[0], own[0] = scatter_wait(pending, token)
        for j in range(len(big)):
            big_out[j], token = _adamw_layer(recv[i][j], own[i][j], kinds[j], big[j], big_m[j], big_v[j],
                                             big_out[j], i, token, name="adamw_" + big_names[j])

    order = ["norm1_g", "w_in", "qk_norm_g", "sink_a", "rpb_c", "w_br_a", "w_br_b", "w_br_c", "w_o", "norm2_g",
             "w_gate_up", "w_down"]
    small_idx = {"norm1_g": 0, "qk_norm_g": 1, "sink_a": 2, "rpb_c": 3, "norm2_g": 4}
    big_idx = {n: j for j, n in enumerate(big_names)}

    def pick(kind):
        out = []
        for n in order:
            if n in small_idx:
                out.append([g_small, d_small, m_small, v_small][kind][small_idx[n]])
            else:
                out.append(big_out[big_idx[n]][kind])
        return out

    return (loss, dx.reshape(1, L, D), *pick(0), *pick(1), *pick(2), *pick(3))
```

```python
import functools
import math

import numpy as np
import jax
import jax.numpy as jnp
from jax import lax
from jax.experimental import pallas as pl
from jax.experimental.pallas import tpu as pltpu

F32 = jnp.float32
BF16 = jnp.bfloat16
MESH = pl.DeviceIdType.MESH
N_DEV = 8

HEAD_DIM = 128
NORM_EPS = 1e-6
ROPE_THETA = 10000.0
ATT_SCALE = HEAD_DIM ** -0.5
NEG = -1e30

A_Q_HEADS, A_KV_HEADS, A_RADIUS = 8, 2, 128
A_GROUP = A_Q_HEADS // A_KV_HEADS
B_PATTERNS = ((128, 1), (512, 4), (2048, 16))
B_HG = 4
B_HEADS = len(B_PATTERNS) * B_HG
C_HEADS, GRID_W, C_WIN_ROWS, C_WIN_COLS = 8, 64, 8, 16
C_NREL = 2 * C_WIN_ROWS - 1
C_NCOL = 2 * C_WIN_COLS - 1

PC_QA, PC_KA, PC_VA = 0, 8, 10
PC_QB, PC_KB, PC_VB = 12, 24, 36
PC_QC, PC_KC, PC_VC = 48, 56, 64
N_QKV_CHUNKS = 72
Q_PIECES = ((0, 8, PC_QA), (8, 12, PC_QB), (20, 8, PC_QC))
K_PIECES = ((0, 2, PC_KA), (2, 12, PC_KB), (14, 8, PC_KC))
V_PIECES = ((0, 2, PC_VA), (2, 12, PC_VB), (14, 8, PC_VC))
NQ_CHUNKS, NK_CHUNKS = 28, 22
Q_ROPE_UPTO, K_ROPE_UPTO = 20, 14
Q_CG, K_CG = 4, 2

ADAM_LR, ADAM_B1, ADAM_B2, ADAM_EPS, ADAM_WD, ADAM_STEP = 0.001, 0.9, 0.999, 1e-08, 0.01, 10

VMEM_LIMIT = 48 * 1024 * 1024


def _tile(dim, prefs):
    for p in prefs:
        if dim % p == 0:
            return p
    return dim


NN_WEIGHT_TILE_BYTES = 8 * 1024 * 1024
NT_WEIGHT_TILE_BYTES = 4 * 1024 * 1024
TN_ACC_BYTES = 6 * 1024 * 1024
MAX_COL_TILE = 2048


def _col_tile(ns):
    return ns if ns <= MAX_COL_TILE else _tile(ns, (MAX_COL_TILE, 1024, 512, 256, 128))


def _params(sem, **kw):
    return pltpu.CompilerParams(dimension_semantics=sem, vmem_limit_bytes=VMEM_LIMIT, **kw)


def _piece_map(pieces):
    def f(c):
        out = c - pieces[0][0] + pieces[0][2]
        for first, _, pfirst in pieces[1:]:
            out = jnp.where(c >= first, c - first + pfirst, out)
        return out
    return f


def _mm_nn(a, w, layer, *, out_dtype, name, res=None):
    M, K = a.shape
    nb, _, Kw, ns = w.shape
    assert Kw == K
    tn = _col_tile(ns)
    tm = _tile(M, (1024, 512, 256) if tn <= 512 else (512, 256))
    tk = _tile(K, tuple(t for t in (2048, 1408, 1024, 512, 256) if t * tn * 2 <= NN_WEIGHT_TILE_BYTES))
    nj, nk = ns // tn, K // tk

    def body(*refs):
        a_ref, w_ref = refs[:2]
        r_ref = None if res is None else refs[2]
        o_ref = refs[2 if res is None else 3]
        part = jnp.dot(a_ref[...].astype(BF16), w_ref[...], preferred_element_type=F32)
        if nk == 1:
            if r_ref is not None:
                part = part + r_ref[...]
            o_ref[...] = part.astype(out_dtype)
            return
        acc_ref = refs[-1]
        k = pl.program_id(3)

        @pl.when(k == 0)
        def _():
            acc_ref[...] = part

        @pl.when(k > 0)
        def _():
            acc_ref[...] += part

        @pl.when(k == nk - 1)
        def _():
            r = acc_ref[...]
            if r_ref is not None:
                r = r + r_ref[...]
            o_ref[...] = r.astype(out_dtype)

    in_specs = [pl.BlockSpec((tm, tk), lambda i, b, j, k: (i, k)),
                pl.BlockSpec((None, None, tk, tn), lambda i, b, j, k: (b, layer, k, j))]
    args = [a, w]
    if res is not None:
        in_specs.append(pl.BlockSpec((tm, tn), lambda i, b, j, k: (i, b * nj + j)))
        args.append(res)
    return pl.pallas_call(
        body, name=name, grid=(M // tm, nb, nj, nk), in_specs=in_specs,
        out_specs=pl.BlockSpec((tm, tn), lambda i, b, j, k: (i, b * nj + j)),
        out_shape=jax.ShapeDtypeStruct((M, nb * ns), out_dtype),
        scratch_shapes=[] if nk == 1 else [pltpu.VMEM((tm, tn), F32)],
        compiler_params=_params(("parallel", "parallel", "parallel", "arbitrary")),
    )(*args)


def _mm_nt(a, w, layer, *, out_dtype, name, dep=None):
    M, N = a.shape
    nb, _, K, ns = w.shape
    assert N == nb * ns
    tm = _tile(M, (1024, 512, 256))
    tn = _col_tile(ns)
    tk = _tile(K, tuple(t for t in (1024, 512, 256) if t * tn * 2 <= NT_WEIGHT_TILE_BYTES))
    nj = ns // tn
    nred = nb * nj

    def body(*refs):
        a_ref, w_ref = refs[:2]
        o_ref, acc_ref = refs[-2:]
        s = pl.program_id(2) * nj + pl.program_id(3)
        part = lax.dot_general(a_ref[...].astype(BF16), w_ref[...], (((1,), (1,)), ((), ())),
                               preferred_element_type=F32)

        @pl.when(s == 0)
        def _():
            acc_ref[...] = part

        @pl.when(s > 0)
        def _():
            acc_ref[...] += part

        @pl.when(s == nred - 1)
        def _():
            o_ref[...] = acc_ref[...].astype(out_dtype)

    in_specs = [pl.BlockSpec((tm, tn), lambda i, kk, b, j: (i, b * nj + j)),
                pl.BlockSpec((None, None, tk, tn), lambda i, kk, b, j: (b, layer, kk, j))]
    args = [a, w]
    if dep is not None:
        in_specs.append(ANY_SPEC)
        args.append(dep)
    return pl.pallas_call(
        body, name=name, grid=(M // tm, K // tk, nb, nj), in_specs=in_specs,
        out_specs=pl.BlockSpec((tm, tk), lambda i, kk, b, j: (i, kk)),
        out_shape=jax.ShapeDtypeStruct((M, K), out_dtype),
        scratch_shapes=[pltpu.VMEM((tm, tk), F32)],
        compiler_params=_params(("parallel", "parallel", "arbitrary", "arbitrary")),
    )(*args)


def _mm_tn(a, g, nb, *, name):
    M, Ka = a.shape
    N = g.shape[1]
    ns = N // nb
    tn = _col_tile(ns)
    tka = _tile(Ka, tuple(t for t in (1024, 512, 256) if t * tn * 4 <= TN_ACC_BYTES))
    tm = _tile(M, (2048, 1024, 512, 256))
    nj, nm = ns // tn, M // tm

    def body(a_ref, g_ref, o_ref, acc_ref):
        m = pl.program_id(3)
        part = lax.dot_general(a_ref[...].astype(BF16), g_ref[...].astype(BF16), (((0,), (0,)), ((), ())),
                               preferred_element_type=F32)

        @pl.when(m == 0)
        def _():
            acc_ref[...] = part

        @pl.when(m > 0)
        def _():
            acc_ref[...] += part

        @pl.when(m == nm - 1)
        def _():
            o_ref[...] = acc_ref[...].astype(BF16)

    return pl.pallas_call(
        body, name=name, grid=(Ka // tka, nb, nj, nm),
        in_specs=[pl.BlockSpec((tm, tka), lambda ka, b, j, m: (m, ka)),
                  pl.BlockSpec((tm, tn), lambda ka, b, j, m: (m, b * nj + j))],
        out_specs=pl.BlockSpec((None, tka, tn), lambda ka, b, j, m: (b, ka, j)),
        out_shape=jax.ShapeDtypeStruct((nb, Ka, ns), BF16),
        scratch_shapes=[pltpu.VMEM((tka, tn), F32)],
        compiler_params=_params(("parallel", "parallel", "parallel", "arbitrary")),
    )(a, g)


def _blocks_to_wide(w, *, name):
    nb, _, K, ns = w.shape

    def body(i_ref, o_ref):
        o_ref[...] = i_ref[...]

    return pl.pallas_call(
        body, name=name, grid=(nb,), in_specs=[pl.BlockSpec((None, None, K, ns), lambda b: (b, 0, 0, 0))],
        out_specs=pl.BlockSpec((None, None, K, ns), lambda b: (0, 0, 0, b)),
        out_shape=jax.ShapeDtypeStruct((1, 1, K, nb * ns), w.dtype), compiler_params=_params(("parallel",)),
    )(w)


def _wide_to_blocks(g, nb, *, name):
    _, K, N = g.shape
    ns = N // nb

    def body(i_ref, o_ref):
        o_ref[...] = i_ref[...]

    return pl.pallas_call(
        body, name=name, grid=(nb,), in_specs=[pl.BlockSpec((None, K, ns), lambda b: (0, 0, b))],
        out_specs=pl.BlockSpec((None, K, ns), lambda b: (b, 0, 0)),
        out_shape=jax.ShapeDtypeStruct((nb, K, ns), g.dtype), compiler_params=_params(("parallel",)),
    )(g)


def _exact_mm(a, e, *, name):
    R, K = a.shape
    N = e.shape[1]

    def body(a_ref, e_ref, o_ref):
        x = a_ref[...]
        hi = x.astype(BF16)
        r1 = x - hi.astype(F32)
        mid = r1.astype(BF16)
        lo = (r1 - mid.astype(F32)).astype(BF16)
        ev = e_ref[...]
        o_ref[...] = (jnp.dot(hi, ev, preferred_element_type=F32) + jnp.dot(mid, ev, preferred_element_type=F32)
                      + jnp.dot(lo, ev, preferred_element_type=F32))

    return pl.pallas_call(body, name=name, out_shape=jax.ShapeDtypeStruct((R, N), F32),
                          compiler_params=pltpu.CompilerParams(vmem_limit_bytes=VMEM_LIMIT))(a, e)


def _rms_fwd(x, g, dep, *, name):
    L, D = x.shape
    tl = _tile(L, (512, 256, 128))

    def body(x_ref, g_ref, _dep, h_ref):
        xv = x_ref[...]
        rstd = lax.rsqrt(jnp.mean(xv * xv, axis=-1, keepdims=True) + NORM_EPS)
        h_ref[...] = (xv * rstd * g_ref[...]).astype(BF16)

    return pl.pallas_call(
        body, name=name, grid=(L // tl,),
        in_specs=[pl.BlockSpec((tl, D), lambda t: (t, 0)), pl.BlockSpec((1, D), lambda t: (0, 0)), ANY_SPEC],
        out_specs=pl.BlockSpec((tl, D), lambda t: (t, 0)),
        out_shape=jax.ShapeDtypeStruct((L, D), BF16),
        compiler_params=_params(("parallel",)),
    )(x, g, dep)


def _rms_bwd(x, g, dy, dres, *, name):
    L, D = x.shape
    tl = _tile(L, (256, 128))

    def body(x_ref, g_ref, dy_ref, dres_ref, dx_ref, dxb_ref, dg_ref):
        t = pl.program_id(0)
        xv = x_ref[...]
        rstd = lax.rsqrt(jnp.mean(xv * xv, axis=-1, keepdims=True) + NORM_EPS)
        xhat = xv * rstd
        dyv = dy_ref[...]
        dxhat = dyv * g_ref[...]
        c = jnp.mean(dxhat * xhat, axis=-1, keepdims=True)
        dx = dres_ref[...] + rstd * (dxhat - xhat * c)
        dx_ref[...] = dx
        dxb_ref[...] = dx.astype(BF16)
        dgp = jnp.sum(dyv * xhat, axis=0, keepdims=True)

        @pl.when(t == 0)
        def _():
            dg_ref[...] = dgp

        @pl.when(t > 0)
        def _():
            dg_ref[...] += dgp

    row = pl.BlockSpec((tl, D), lambda t: (t, 0))
    vec = pl.BlockSpec((1, D), lambda t: (0, 0))
    return pl.pallas_call(
        body, name=name, grid=(L // tl,), in_specs=[row, vec, row, row], out_specs=[row, row, vec],
        out_shape=[jax.ShapeDtypeStruct((L, D), F32), jax.ShapeDtypeStruct((L, D), BF16),
                   jax.ShapeDtypeStruct((1, D), F32)],
        compiler_params=_params(("arbitrary",)),
    )(x, g, dy, dres)


def _gate_fwd(proj, ta, tb, tc, *, name):
    L, D = ta.shape
    tl, tcw = _tile(L, (512, 256, 128)), _tile(D, (1024, 512, 256, 128))
    off = N_QKV_CHUNKS * HEAD_DIM // tcw
    nd = D // tcw

    def body(g0, g1, g2, a_ref, b_ref, c_ref, o_ref):
        m = (jax.nn.sigmoid(g0[...]) * a_ref[...].astype(F32) + jax.nn.sigmoid(g1[...]) * b_ref[...].astype(F32)
             + jax.nn.sigmoid(g2[...]) * c_ref[...].astype(F32))
        o_ref[...] = m.astype(BF16)

    blk = pl.BlockSpec((tl, tcw), lambda t, j: (t, j))
    gl = [pl.BlockSpec((tl, tcw), functools.partial(lambda t, j, i: (t, off + i * nd + j), i=i)) for i in range(3)]
    return pl.pallas_call(
        body, name=name, grid=(L // tl, nd), in_specs=gl + [blk, blk, blk], out_specs=blk,
        out_shape=jax.ShapeDtypeStruct((L, D), BF16),
        compiler_params=_params(("parallel", "parallel")),
    )(proj, proj, proj, ta, tb, tc)


def _gate_bwd(proj, ta, tb, tc, dmerged, *, name):
    L, D = ta.shape
    ncols = proj.shape[1]
    tl, tcw = _tile(L, (512, 256, 128)), _tile(D, (1024, 512, 256, 128))
    off = N_QKV_CHUNKS * HEAD_DIM // tcw
    nd = D // tcw

    def body(g0, g1, g2, a_ref, b_ref, c_ref, dm_ref, da_ref, db_ref, dc_ref, dgl_ref):
        i = pl.program_id(2)
        sg = jax.nn.sigmoid(jnp.where(i == 0, g0[...], jnp.where(i == 1, g1[...], g2[...])))
        sel_t = jnp.where(i == 0, a_ref[...], jnp.where(i == 1, b_ref[...], c_ref[...])).astype(F32)
        dt = dm_ref[...] * sg
        dtb = dt.astype(BF16)

        @pl.when(i == 0)
        def _():
            da_ref[...] = dtb

        @pl.when(i == 1)
        def _():
            db_ref[...] = dtb

        @pl.when(i == 2)
        def _():
            dc_ref[...] = dtb

        dgl_ref[...] = (dt * sel_t * (1.0 - sg)).astype(BF16)

    blk = pl.BlockSpec((tl, tcw), lambda t, j, i: (t, j))
    gl = [pl.BlockSpec((tl, tcw), functools.partial(lambda t, j, i, q: (t, off + q * nd + j), q=q)) for q in range(3)]
    return pl.pallas_call(
        body, name=name, grid=(L // tl, nd, 3), in_specs=gl + [blk, blk, blk, blk],
        out_specs=[blk, blk, blk, pl.BlockSpec((tl, tcw), lambda t, j, i: (t, off + i * nd + j))],
        out_shape=[jax.ShapeDtypeStruct((L, D), BF16)] * 3 + [jax.ShapeDtypeStruct((L, ncols), BF16)],
        compiler_params=_params(("parallel", "parallel", "arbitrary")),
    )(proj, proj, proj, ta, tb, tc, dmerged)


def _swiglu_fwd(gu, *, name):
    L, F2 = gu.shape
    F = F2 // 2
    tl = _tile(L, (128, 64))

    def body(gu_ref, o_ref):
        gt = gu_ref[:, :F].astype(F32)
        o_ref[...] = (gt * jax.nn.sigmoid(gt) * gu_ref[:, F:].astype(F32)).astype(BF16)

    return pl.pallas_call(
        body, name=name, grid=(L // tl,), in_specs=[pl.BlockSpec((tl, F2), lambda t: (t, 0))],
        out_specs=pl.BlockSpec((tl, F), lambda t: (t, 0)),
        out_shape=jax.ShapeDtypeStruct((L, F), BF16),
        compiler_params=_params(("parallel",)),
    )(gu)


def _swiglu_bwd(gu, dact, *, name):
    L, F2 = gu.shape
    F = F2 // 2
    tl = _tile(L, (128, 64))

    def body(gu_ref, d_ref, o_ref):
        gt, up, d = gu_ref[:, :F].astype(F32), gu_ref[:, F:].astype(F32), d_ref[...].astype(F32)
        sg = jax.nn.sigmoid(gt)
        o_ref[:, :F] = (d * up * sg * (1.0 + gt * (1.0 - sg))).astype(BF16)
        o_ref[:, F:] = (d * gt * sg).astype(BF16)

    return pl.pallas_call(
        body, name=name, grid=(L // tl,),
        in_specs=[pl.BlockSpec((tl, F2), lambda t: (t, 0)), pl.BlockSpec((tl, F), lambda t: (t, 0))],
        out_specs=pl.BlockSpec((tl, F2), lambda t: (t, 0)),
        out_shape=jax.ShapeDtypeStruct((L, F2), BF16),
        compiler_params=_params(("parallel",)),
    )(gu, dact)


def _loss(y, tgt, *, name):
    L, D = y.shape
    tl = _tile(L, (256, 128))
    nt = L // tl

    def body(y_ref, t_ref, dy_ref, dyb_ref, loss_ref, acc_ref):
        t = pl.program_id(0)
        e = y_ref[...] - t_ref[...]
        dy = e * (1.0 / D)
        dy_ref[...] = dy
        dyb_ref[...] = dy.astype(BF16)
        part = jnp.sum(e * e, axis=0, keepdims=True)

        @pl.when(t == 0)
        def _():
            acc_ref[...] = part

        @pl.when(t > 0)
        def _():
            acc_ref[...] += part

        @pl.when(t == nt - 1)
        def _():
            loss_ref[...] = jnp.broadcast_to(jnp.sum(acc_ref[...], axis=-1, keepdims=True) * (0.5 / D), (1, 128))

    row = pl.BlockSpec((tl, D), lambda t: (t, 0))
    return pl.pallas_call(
        body, name=name, grid=(nt,), in_specs=[row, row],
        out_specs=[row, row, pl.BlockSpec((1, 128), lambda t: (0, 0))],
        out_shape=[jax.ShapeDtypeStruct((L, D), F32), jax.ShapeDtypeStruct((L, D), BF16),
                   jax.ShapeDtypeStruct((1, 128), F32)],
        scratch_shapes=[pltpu.VMEM((1, D), F32)],
        compiler_params=_params(("arbitrary",)),
    )(y, tgt)


def _rope(v, cos, sin_signed):
    return v * cos + pltpu.roll(v, HEAD_DIM // 2, 1) * sin_signed


def _head_mean(x):
    hi = x.astype(BF16)
    lo = (x - hi.astype(F32)).astype(BF16)
    ones = jnp.ones((HEAD_DIM, HEAD_DIM), BF16)
    total = jnp.dot(hi, ones, preferred_element_type=F32) + jnp.dot(lo, ones, preferred_element_type=F32)
    return total * (1.0 / HEAD_DIM)


def _qk_fwd(proj, gtab, cos, sin, pieces, nchunks, rope_upto, cg, *, name, dep=None):
    L = proj.shape[0]
    tl = _tile(L, (512, 256, 128))
    W = cg * HEAD_DIM
    pmap = _piece_map(tuple((a // cg, n // cg, p // cg) for a, n, p in pieces))

    def body(*refs):
        p_ref, g_ref, cos_ref, sin_ref = refs[:4]
        o_ref = refs[-1]
        c = pl.program_id(1)

        def norm(j):
            cols = slice(j * HEAD_DIM, (j + 1) * HEAD_DIM)
            x = p_ref[:, cols]
            rstd = lax.rsqrt(_head_mean(x * x) + NORM_EPS)
            return cols, x * rstd * g_ref[:, cols]

        @pl.when(c < rope_upto // cg)
        def _():
            for j in range(cg):
                cols, y = norm(j)
                o_ref[:, cols] = _rope(y, cos_ref[...], sin_ref[...])

        @pl.when(c >= rope_upto // cg)
        def _():
            for j in range(cg):
                cols, y = norm(j)
                o_ref[:, cols] = y

    pos = pl.BlockSpec((tl, HEAD_DIM), lambda t, c: (t, 0))
    in_specs = [pl.BlockSpec((tl, W), lambda t, c: (t, pmap(c))),
                pl.BlockSpec((None, 1, W), lambda t, c: (c, 0, 0)), pos, pos]
    args = [proj, gtab, cos, sin]
    if dep is not None:
        in_specs.append(ANY_SPEC)
        args.append(dep)
    return pl.pallas_call(
        body, name=name, grid=(L // tl, nchunks // cg), in_specs=in_specs,
        out_specs=pl.BlockSpec((tl, W), lambda t, c: (t, c)),
        out_shape=jax.ShapeDtypeStruct((L, nchunks * HEAD_DIM), F32),
        compiler_params=_params(("parallel", "parallel")),
    )(*args)


def _qk_bwd(dqk, proj, gtab, cos, sin, dproj, pieces, nchunks, rope_upto, cg, *, name):
    L = proj.shape[0]
    tl = _tile(L, (512, 256, 128))
    W = cg * HEAD_DIM
    pmap = _piece_map(tuple((a // cg, n // cg, p // cg) for a, n, p in pieces))

    def body(d_ref, p_ref, g_ref, cos_ref, sin_ref, _, o_ref, dg_ref):
        c, t = pl.program_id(0), pl.program_id(1)

        @pl.when(t == 0)
        def _():
            dg_ref[...] = jnp.zeros_like(dg_ref)

        for j in range(cg):
            cols = slice(j * HEAD_DIM, (j + 1) * HEAD_DIM)
            x = p_ref[:, cols]
            rstd = lax.rsqrt(_head_mean(x * x) + NORM_EPS)
            xhat = x * rstd
            dy = d_ref[:, cols]
            dy = jnp.where(c < rope_upto // cg, _rope(dy, cos_ref[...], -sin_ref[...]), dy)
            dxhat = dy * g_ref[:, cols]
            cm = _head_mean(dxhat * xhat)
            o_ref[:, cols] = (rstd * (dxhat - xhat * cm)).astype(BF16)
            dg_ref[:, cols] += jnp.sum(dy * xhat, axis=0, keepdims=True)

    pos = pl.BlockSpec((tl, HEAD_DIM), lambda c, t: (t, 0))
    gspec = pl.BlockSpec((None, 1, W), lambda c, t: (c, 0, 0))
    out, dg = pl.pallas_call(
        body, name=name, grid=(nchunks // cg, L // tl),
        in_specs=[pl.BlockSpec((tl, W), lambda c, t: (t, c)),
                  pl.BlockSpec((tl, W), lambda c, t: (t, pmap(c))), gspec, pos, pos,
                  pl.BlockSpec(memory_space=pl.ANY)],
        out_specs=[pl.BlockSpec((tl, W), lambda c, t: (t, pmap(c))), gspec],
        out_shape=[jax.ShapeDtypeStruct(dproj.shape, BF16), jax.ShapeDtypeStruct((nchunks // cg, 1, W), F32)],
        input_output_aliases={5: 0},
        compiler_params=_params(("parallel", "arbitrary")),
    )(dqk, proj, gtab, cos, sin, dproj)
    return out, dg


def _v_bwd(dv, dproj, *, name):
    L = dv.shape[0]
    tl = _tile(L, (2048, 1024, 512, 256, 128))
    pmap = _piece_map(tuple((a // 2, n // 2, p // 2) for a, n, p in V_PIECES))

    def body(d_ref, _, o_ref):
        o_ref[...] = d_ref[...].astype(BF16)

    return pl.pallas_call(
        body, name=name, grid=(L // tl, NK_CHUNKS // 2),
        in_specs=[pl.BlockSpec((tl, 2 * HEAD_DIM), lambda t, c: (t, c)), pl.BlockSpec(memory_space=pl.ANY)],
        out_specs=pl.BlockSpec((tl, 2 * HEAD_DIM), lambda t, c: (t, pmap(c))),
        out_shape=jax.ShapeDtypeStruct(dproj.shape, BF16),
        input_output_aliases={1: 0},
        compiler_params=_params(("parallel", "parallel")),
    )(dv, dproj)


def _band_geometry(L, dil, radius):
    n = L // dil
    bq = min(256, max(n // 2, 64), n)
    width = min(bq + 2 * radius, n)
    nsub = _tile(n // bq, (4, 2)) if dil == 1 else 1
    return n, bq, width, nsub


def _band_loop(dil, one):
    if dil == 1:
        one(0, 0)
    else:
        lax.fori_loop(0, dil, one, 0, unroll=min(dil, 4))


def _band_rows(dil, r, first, count):
    if dil == 1:
        return pl.ds(pl.multiple_of(first, 8), count)
    return pl.ds(r + first * dil, count, stride=dil)


def _band_mask(i, bq, width, radius, ws):
    qpos = i * bq + lax.broadcasted_iota(jnp.int32, (bq, width), 0)
    kpos = ws + lax.broadcasted_iota(jnp.int32, (bq, width), 1)
    return jnp.abs(kpos - qpos) <= radius


def _band_fwd(qn, kn, proj, *, dil, radius, nkv, group, q0, k0, v0, sink=None, name):
    L = qn.shape[0]
    n, bq, width, nsub = _band_geometry(L, dil, radius)
    tq = nsub * bq * dil
    nh = nkv * group

    def body(*refs):
        if sink is None:
            q_ref, k_ref, v_ref, o_ref, lse_ref = refs
        else:
            q_ref, k_ref, v_ref, s_ref, o_ref, lse_ref = refs
        for sb in range(nsub):
            block(sb, q_ref, k_ref, v_ref, None if sink is None else s_ref, o_ref, lse_ref)

    def block(sb, q_ref, k_ref, v_ref, s_ref, o_ref, lse_ref):
        i = pl.program_id(2) * nsub + sb
        ws = jnp.clip(i * bq - radius, 0, n - width)
        valid = _band_mask(i, bq, width, radius, ws)

        def one(r, carry):
            qrows = _band_rows(dil, r, sb * bq, bq)
            krows = _band_rows(dil, r, ws, width)
            q = q_ref[qrows, :].astype(BF16)
            k = k_ref[krows, :].astype(BF16)
            v = v_ref[krows, :].astype(BF16)
            s = lax.dot_general(q, k, (((1,), (1,)), ((), ())), preferred_element_type=F32) * ATT_SCALE
            s = jnp.where(valid, s, NEG)
            m = jnp.max(s, axis=-1, keepdims=True)
            if sink is not None:
                m = jnp.maximum(m, s_ref[...][:, :1])
            p = jnp.exp(s - m)
            denom = jnp.sum(p, axis=-1, keepdims=True)
            if sink is not None:
                denom = denom + jnp.exp(s_ref[...][:, :1] - m)
            pn = (p / denom).astype(BF16)
            o_ref[qrows, :] = jnp.dot(pn, v, preferred_element_type=F32)
            lse_ref[qrows, :] = jnp.broadcast_to(m + jnp.log(denom), (bq, HEAD_DIM))
            return carry

        _band_loop(dil, one)

    qspec = pl.BlockSpec((tq, HEAD_DIM), lambda hk, g, i: (i, q0 + hk * group + g))
    in_specs = [qspec,
                pl.BlockSpec((L, HEAD_DIM), lambda hk, g, i: (0, k0 + hk)),
                pl.BlockSpec((L, HEAD_DIM), lambda hk, g, i: (0, v0 + hk))]
    args = [qn, kn, proj]
    if sink is not None:
        in_specs.append(pl.BlockSpec((None, 1, HEAD_DIM), lambda hk, g, i: (hk * group + g, 0, 0)))
        args.append(sink)
    ospec = pl.BlockSpec((tq, HEAD_DIM), lambda hk, g, i: (i, hk * group + g))
    return pl.pallas_call(
        body, name=name, grid=(nkv, group, n // (bq * nsub)), in_specs=in_specs, out_specs=[ospec, ospec],
        out_shape=[jax.ShapeDtypeStruct((L, nh * HEAD_DIM), F32)] * 2,
        compiler_params=_params(("parallel", "parallel", "arbitrary")),
    )(*args)


def _band_bwd(qn, kn, proj, do, o, lse, dq_buf, dk_buf, dv_buf, *, dil, radius, nkv, group, q0, k0, v0, o0,
              sink=None, name):
    L = qn.shape[0]
    n, bq, width, nsub = _band_geometry(L, dil, radius)
    tq = nsub * bq * dil
    nh = nkv * group
    n_in = 6 + (1 if sink is not None else 0)

    def body(*refs):
        q_ref, k_ref, v_ref, do_ref, o_ref, lse_ref = refs[:6]
        s_ref = refs[6] if sink is not None else None
        outs = refs[n_in + 3:]
        dq_ref, dk_ref, dv_ref = outs[:3]
        ds_ref = outs[3] if sink is not None else None
        g, step = pl.program_id(1), pl.program_id(2)

        @pl.when((g == 0) & (step == 0))
        def _():
            dk_ref[...] = jnp.zeros_like(dk_ref)
            dv_ref[...] = jnp.zeros_like(dv_ref)

        if sink is not None:
            @pl.when(step == 0)
            def _():
                ds_ref[...] = jnp.zeros_like(ds_ref)

        for sb in range(nsub):
            block(sb, q_ref, k_ref, v_ref, do_ref, o_ref, lse_ref, s_ref, dq_ref, dk_ref, dv_ref, ds_ref)

    def block(sb, q_ref, k_ref, v_ref, do_ref, o_ref, lse_ref, s_ref, dq_ref, dk_ref, dv_ref, ds_ref):
        i = pl.program_id(2) * nsub + sb
        ws = jnp.clip(i * bq - radius, 0, n - width)
        valid = _band_mask(i, bq, width, radius, ws)

        def one(r, carry):
            qrows = _band_rows(dil, r, sb * bq, bq)
            krows = _band_rows(dil, r, ws, width)
            q = q_ref[qrows, :].astype(BF16)
            k = k_ref[krows, :].astype(BF16)
            v = v_ref[krows, :].astype(BF16)
            dov = do_ref[qrows, :]
            lse_v = lse_ref[qrows, :][:, :1]
            delta = jnp.sum(dov * o_ref[qrows, :], axis=-1, keepdims=True)
            dob = dov.astype(BF16)
            s = lax.dot_general(q, k, (((1,), (1,)), ((), ())), preferred_element_type=F32) * ATT_SCALE
            p = jnp.where(valid, jnp.exp(s - lse_v), 0.0)
            dp = lax.dot_general(dob, v, (((1,), (1,)), ((), ())), preferred_element_type=F32)
            dsb = (p * (dp - delta)).astype(BF16)
            dq_ref[qrows, :] = jnp.dot(dsb, k, preferred_element_type=F32) * ATT_SCALE
            dk_ref[krows, :] += lax.dot_general(dsb, q, (((0,), (0,)), ((), ())),
                                                preferred_element_type=F32) * ATT_SCALE
            dv_ref[krows, :] += lax.dot_general(p.astype(BF16), dob, (((0,), (0,)), ((), ())),
                                                preferred_element_type=F32)
            if sink is not None:
                ps = jnp.exp(s_ref[...][:, :1] - lse_v)
                ds_ref[...] += jnp.broadcast_to(jnp.sum(-ps * delta, axis=0, keepdims=True), (1, HEAD_DIM))
            return carry

        _band_loop(dil, one)

    hspec = pl.BlockSpec((tq, HEAD_DIM), lambda hk, g, i: (i, o0 + hk * group + g))
    qspec = pl.BlockSpec((tq, HEAD_DIM), lambda hk, g, i: (i, q0 + hk * group + g))
    kspec = pl.BlockSpec((L, HEAD_DIM), lambda hk, g, i: (0, k0 + hk))
    any_spec = pl.BlockSpec(memory_space=pl.ANY)
    in_specs = [qspec, kspec, pl.BlockSpec((L, HEAD_DIM), lambda hk, g, i: (0, v0 + hk)), hspec, hspec, hspec]
    args = [qn, kn, proj, do, o, lse]
    if sink is not None:
        in_specs.append(pl.BlockSpec((None, 1, HEAD_DIM), lambda hk, g, i: (hk * group + g, 0, 0)))
        args.append(sink)
    in_specs += [any_spec] * 3
    args += [dq_buf, dk_buf, dv_buf]
    out_specs = [qspec, kspec, kspec]
    out_shape = [jax.ShapeDtypeStruct(dq_buf.shape, F32), jax.ShapeDtypeStruct(dk_buf.shape, F32),
                 jax.ShapeDtypeStruct(dv_buf.shape, F32)]
    if sink is not None:
        out_specs.append(pl.BlockSpec((None, 1, HEAD_DIM), lambda hk, g, i: (hk * group + g, 0, 0)))
        out_shape.append(jax.ShapeDtypeStruct((nh, 1, HEAD_DIM), F32))
    return pl.pallas_call(
        body, name=name, grid=(nkv, group, n // (bq * nsub)), in_specs=in_specs, out_specs=out_specs,
        out_shape=out_shape,
        input_output_aliases={n_in: 0, n_in + 1: 1, n_in + 2: 2},
        compiler_params=_params(("parallel", "arbitrary", "arbitrary")),
    )(*args)


def _combine_b(os_, lses, *, name):
    L, W = os_[0].shape
    tl = _tile(L, (256, 128))

    def body(o0, o1, o2, l0, l1, l2, out_ref, lt_ref):
        a, b, c = l0[...], l1[...], l2[...]
        m = jnp.maximum(jnp.maximum(a, b), c)
        ea, eb, ec = jnp.exp(a - m), jnp.exp(b - m), jnp.exp(c - m)
        tot = ea + eb + ec
        out_ref[...] = (ea * o0[...] + eb * o1[...] + ec * o2[...]) / tot
        lt_ref[...] = m + jnp.log(tot)

    blk = pl.BlockSpec((tl, W), lambda t: (t, 0))
    return pl.pallas_call(
        body, name=name, grid=(L // tl,), in_specs=[blk] * 6, out_specs=[blk, blk],
        out_shape=[jax.ShapeDtypeStruct((L, W), F32)] * 2, compiler_params=_params(("parallel",)),
    )(*os_, *lses)


C_QROWS = 4
C_KROWS = C_QROWS + C_WIN_ROWS
C_QUERIES, C_KEYS = C_QROWS * GRID_W, C_KROWS * GRID_W
_C_KIND_OFFSETS = (C_WIN_ROWS - 1, C_WIN_ROWS - 1 - C_WIN_ROWS // 2, C_WIN_ROWS - 1 - (C_KROWS - C_QROWS))


def _c_geometry(L):
    rows = L // GRID_W
    assert rows >= C_KROWS and rows % C_QROWS == 0
    return rows


def _c_bias_tiles(bias_t):
    cq = np.arange(GRID_W)[:, None]
    ck = np.arange(GRID_W)[None, :]
    start = np.clip(cq - C_WIN_COLS // 2, 0, GRID_W - C_WIN_COLS)
    masked = jnp.where(jnp.asarray((ck >= start) & (ck < start + C_WIN_COLS)), bias_t, NEG)
    blank = jnp.full((C_HEADS, GRID_W, GRID_W), NEG, F32)
    kinds = []
    for kind in range(3):
        off = _C_KIND_OFFSETS[kind]
        row_blocks = []
        for a in range(C_QROWS):
            lo = (0, a, C_KROWS - C_WIN_ROWS)[kind]
            row_blocks.append(jnp.concatenate(
                [masked[:, b - a + off] if lo <= b < lo + C_WIN_ROWS else blank for b in range(C_KROWS)], axis=-1))
        kinds.append(jnp.concatenate(row_blocks, axis=-2))
    return jnp.stack(kinds, axis=1)


def _c_block(g, rows):
    r0 = g * C_QROWS
    k0 = jnp.clip(r0 - C_WIN_ROWS // 2, 0, rows - C_KROWS)
    kind = jnp.where(g == 0, 0, jnp.where(g == rows // C_QROWS - 1, 2, 1))
    return k0, kind, k0 - r0 + (C_WIN_ROWS - 1)


def _c_fwd(qn, kn, proj, tiles, *, name):
    L = qn.shape[0]
    rows = _c_geometry(L)

    nsub = _tile(rows // C_QROWS, (2,))

    def body(q_ref, k_ref, v_ref, t_ref, o_ref, lse_ref):
        for sb in range(nsub):
            k0, kind, _ = _c_block(pl.program_id(1) * nsub + sb, rows)
            krows = pl.ds(pl.multiple_of(k0 * GRID_W, GRID_W), C_KEYS)
            qrows = pl.ds(sb * C_QUERIES, C_QUERIES)
            q = q_ref[qrows, :].astype(BF16)
            k = k_ref[krows, :].astype(BF16)
            v = v_ref[krows, :].astype(BF16)
            s = lax.dot_general(q, k, (((1,), (1,)), ((), ())), preferred_element_type=F32) * ATT_SCALE + t_ref[kind]
            m = jnp.max(s, axis=-1, keepdims=True)
            p = jnp.exp(s - m)
            denom = jnp.sum(p, axis=-1, keepdims=True)
            o_ref[qrows, :] = jnp.dot((p / denom).astype(BF16), v, preferred_element_type=F32)
            lse_ref[qrows, :] = jnp.broadcast_to(m + jnp.log(denom), (C_QUERIES, HEAD_DIM))

    ospec = pl.BlockSpec((nsub * C_QUERIES, HEAD_DIM), lambda h, g: (g, h))
    return pl.pallas_call(
        body, name=name, grid=(C_HEADS, rows // (C_QROWS * nsub)),
        in_specs=[pl.BlockSpec((nsub * C_QUERIES, HEAD_DIM), lambda h, g: (g, 20 + h)),
                  pl.BlockSpec((L, HEAD_DIM), lambda h, g: (0, 14 + h)),
                  pl.BlockSpec((L, HEAD_DIM), lambda h, g: (0, PC_VC + h)),
                  pl.BlockSpec((None, 3, C_QUERIES, C_KEYS), lambda h, g: (h, 0, 0, 0))],
        out_specs=[ospec, ospec],
        out_shape=[jax.ShapeDtypeStruct((L, C_HEADS * HEAD_DIM), F32)] * 2,
        compiler_params=_params(("parallel", "arbitrary")),
    )(qn, kn, proj, tiles)


def _c_bwd(qn, kn, proj, tiles, do, o, lse, dq_buf, dk_buf, dv_buf, *, name):
    L = qn.shape[0]
    rows = _c_geometry(L)

    nsub = _tile(rows // C_QROWS, (2,))

    def body(q_ref, k_ref, v_ref, t_ref, do_ref, o_ref, lse_ref, _a, _b, _c, dq_ref, dk_ref, dv_ref, dt_ref):
        @pl.when(pl.program_id(1) == 0)
        def _():
            dk_ref[...] = jnp.zeros_like(dk_ref)
            dv_ref[...] = jnp.zeros_like(dv_ref)
            dt_ref[...] = jnp.zeros_like(dt_ref)

        for sb in range(nsub):
            k0, kind, off = _c_block(pl.program_id(1) * nsub + sb, rows)
            krows = pl.ds(pl.multiple_of(k0 * GRID_W, GRID_W), C_KEYS)
            qrows = pl.ds(sb * C_QUERIES, C_QUERIES)
            q = q_ref[qrows, :].astype(BF16)
            k = k_ref[krows, :].astype(BF16)
            v = v_ref[krows, :].astype(BF16)
            dov = do_ref[qrows, :]
            dob = dov.astype(BF16)
            delta = jnp.sum(dov * o_ref[qrows, :], axis=-1, keepdims=True)
            s = lax.dot_general(q, k, (((1,), (1,)), ((), ())), preferred_element_type=F32) * ATT_SCALE + t_ref[kind]
            p = jnp.exp(s - lse_ref[qrows, :][:, :1])
            dp = lax.dot_general(dob, v, (((1,), (1,)), ((), ())), preferred_element_type=F32)
            ds = p * (dp - delta)
            for a in range(C_QROWS):
                for b in range(C_KROWS):
                    rel = jnp.clip(b - a + off, 0, C_NREL - 1)
                    dt_ref[rel] += ds[a * GRID_W:(a + 1) * GRID_W, b * GRID_W:(b + 1) * GRID_W]
            dsb = ds.astype(BF16)
            dq_ref[qrows, :] = jnp.dot(dsb, k, preferred_element_type=F32) * ATT_SCALE
            dk_ref[krows, :] += lax.dot_general(dsb, q, (((0,), (0,)), ((), ())),
                                                preferred_element_type=F32) * ATT_SCALE
            dv_ref[krows, :] += lax.dot_general(p.astype(BF16), dob, (((0,), (0,)), ((), ())),
                                                preferred_element_type=F32)

    hspec = pl.BlockSpec((nsub * C_QUERIES, HEAD_DIM), lambda h, g: (g, h))
    qspec = pl.BlockSpec((nsub * C_QUERIES, HEAD_DIM), lambda h, g: (g, 20 + h))
    kspec = pl.BlockSpec((L, HEAD_DIM), lambda h, g: (0, 14 + h))
    any_spec = pl.BlockSpec(memory_space=pl.ANY)
    return pl.pallas_call(
        body, name=name, grid=(C_HEADS, rows // (C_QROWS * nsub)),
        in_specs=[qspec, kspec, pl.BlockSpec((L, HEAD_DIM), lambda h, g: (0, PC_VC + h)),
                  pl.BlockSpec((None, 3, C_QUERIES, C_KEYS), lambda h, g: (h, 0, 0, 0)),
                  hspec, hspec, hspec, any_spec, any_spec, any_spec],
        out_specs=[qspec, kspec, kspec,
                   pl.BlockSpec((None, C_NREL, GRID_W, GRID_W), lambda h, r: (h, 0, 0, 0))],
        out_shape=[jax.ShapeDtypeStruct(dq_buf.shape, F32), jax.ShapeDtypeStruct(dk_buf.shape, F32),
                   jax.ShapeDtypeStruct(dv_buf.shape, F32),
                   jax.ShapeDtypeStruct((C_HEADS, C_NREL, GRID_W, GRID_W), F32)],
        input_output_aliases={7: 0, 8: 1, 9: 2},
        compiler_params=_params(("parallel", "arbitrary")),
    )(qn, kn, proj, tiles, do, o, lse, dq_buf, dk_buf, dv_buf)


def _c_expand_matrix():
    cq = np.arange(GRID_W)[:, None]
    ck = np.arange(GRID_W)[None, :]
    d = (ck - cq + (C_WIN_COLS - 1)).reshape(-1)
    e = np.zeros((GRID_W * GRID_W, HEAD_DIM), np.float32)
    okd = (d >= 0) & (d < C_NCOL)
    e[np.arange(GRID_W * GRID_W)[okd], d[okd]] = 1.0
    return e


def _peer(p):
    return (p // 4, (p // 2) % 2, p % 2)


def _my_index():
    return 4 * lax.axis_index("x") + 2 * lax.axis_index("y") + lax.axis_index("c")


HBM_SPEC = pl.BlockSpec(memory_space=pltpu.HBM)
SEM_SPEC = pl.BlockSpec(memory_space=pltpu.SEMAPHORE)
ANY_SPEC = pl.BlockSpec(memory_space=pl.ANY)
DATAFLOW = pltpu.SideEffectType.DATAFLOW_SIDE_EFFECTING


_EXCHANGE_TRANSFERS = {"scatter": N_DEV - 1, "gather1": 4, "gather2": 3}


def _exchange_views(mode, kinds, arrays):
    nw = len(kinds)
    gather = mode != "scatter"
    if gather:
        rows = [a.shape[0] // N_DEV for a in arrays[:nw]]
    else:
        rows = [a.shape[1] // N_DEV for a in arrays[:nw]]

    def gather_slot(ref, w, who):
        return ref.at[who] if kinds[w] == "col" else ref.at[pl.ds(who * rows[w], rows[w]), :]

    x, y, c = lax.axis_index("x"), lax.axis_index("y"), lax.axis_index("c")
    me = 4 * x + 2 * y + c
    chips = [(1 - x, y), (x, 1 - y), (1 - x, 1 - y)]

    def index(px, py, pc):
        return 4 * px + 2 * py + pc

    if mode == "scatter":
        plan = [(_peer((me + off) % N_DEV), (me + off) % N_DEV, (me + N_DEV - off) % N_DEV)
                for off in range(1, N_DEV)]
    elif mode == "gather1":
        plan = [((x, y, 1 - c), me, index(x, y, 1 - c))] + [((px, py, c), me, index(px, py, c)) for px, py in chips]
    else:
        plan = [((x, y, 1 - c), index(px, py, c), index(px, py, 1 - c)) for px, py in chips]

    def src(ref, w, j):
        sent = plan[j][1]
        if gather:
            return gather_slot(ref, w, sent)
        return ref.at[sent] if kinds[w] == "col" else ref.at[0, pl.ds(sent * rows[w], rows[w]), :]

    def dst(ref, w, j):
        return gather_slot(ref, w, plan[j][1]) if gather else ref.at[me]

    def arrival(ref, w, j):
        return gather_slot(ref, w, plan[j][2]) if gather else ref.at[plan[j][2]]

    return [p[0] for p in plan], src, dst, arrival


def _place_cast(w, layer, kind, *, name):
    _, R, C = w.shape
    tr = _tile(R, (256, 128, 64, 32, 16))

    def body(w_ref, o_ref):
        o_ref[...] = w_ref[...].astype(BF16)

    if kind == "col":
        out_shape = jax.ShapeDtypeStruct((N_DEV, R, C), BF16)
        out_spec = pl.BlockSpec((None, tr, C), lambda t: (_my_index(), t, 0))
    else:
        out_shape = jax.ShapeDtypeStruct((N_DEV * R, C), BF16)
        out_spec = pl.BlockSpec((tr, C), lambda t: (_my_index() * (R // tr) + t, 0))
    return pl.pallas_call(
        body, name=name, grid=(R // tr,), in_specs=[pl.BlockSpec((None, tr, C), lambda t: (layer, t, 0))],
        out_specs=out_spec, out_shape=out_shape, compiler_params=_params(("parallel",)),
    )(w)


def _exchange_start(mode, srcs, lands, kinds, after, *, name):
    nw = len(lands)
    ns = len(srcs)
    arrays = list(srcs) + list(lands)
    na = len(arrays)
    nx = _EXCHANGE_TRANSFERS[mode]

    def body(*refs):
        l_refs = refs[ns:ns + nw]
        s_refs = refs[:ns] if ns else l_refs
        send_sems, recv_sems = refs[ns + nw + 1], refs[ns + nw + 2]
        token = refs[-1]
        peers, src, dst, _ = _exchange_views(mode, kinds, arrays)
        for j in range(nx):
            for w in range(nw):
                pltpu.make_async_remote_copy(src(s_refs[w], w, j), dst(l_refs[w], w, j),
                                             send_sems.at[w * nx + j], recv_sems.at[w * nx + j],
                                             device_id=peers[j], device_id_type=MESH).start()
        token[...] = jnp.zeros_like(token)

    outs = pl.pallas_call(
        body, name=name,
        out_shape=(pltpu.SemaphoreType.DMA((nw * nx,)), pltpu.SemaphoreType.DMA((nw * nx,)),
                   *[pltpu.HBM(a.shape, a.dtype) for a in arrays], jax.ShapeDtypeStruct((8, 128), F32)),
        in_specs=[HBM_SPEC] * na + [ANY_SPEC],
        out_specs=(SEM_SPEC, SEM_SPEC, *([HBM_SPEC] * na), pl.BlockSpec(memory_space=pltpu.VMEM)),
        input_output_aliases={k: 2 + k for k in range(na)},
        compiler_params=pltpu.CompilerParams(has_side_effects=DATAFLOW),
    )(*[pltpu.with_memory_space_constraint(a, pltpu.HBM) for a in arrays], after)
    return outs[0], outs[1], outs[2:2 + ns], outs[2 + ns:2 + na], outs[-1]


def _exchange_wait(mode, started, kinds, after, *, name):
    send_sems, recv_sems, srcs, lands, _ = started
    nw = len(lands)
    ns = len(srcs)
    arrays = list(srcs) + list(lands)
    na = len(arrays)
    nx = _EXCHANGE_TRANSFERS[mode]

    def body(*refs):
        l_refs = refs[ns:na]
        s_refs = refs[:ns] if ns else l_refs
        send_ref, recv_ref = refs[na], refs[na + 1]
        peers, src, _, arrival = _exchange_views(mode, kinds, arrays)
        for j in range(nx):
            for w in range(nw):
                cp = pltpu.make_async_remote_copy(src(s_refs[w], w, j), arrival(l_refs[w], w, j),
                                                  send_ref.at[w * nx + j], recv_ref.at[w * nx + j],
                                                  device_id=peers[j], device_id_type=MESH)
                cp.wait_send()
                cp.wait_recv()

    outs = pl.pallas_call(
        body, name=name, out_shape=[pltpu.HBM(a.shape, a.dtype) for a in arrays],
        in_specs=[HBM_SPEC] * na + [SEM_SPEC, SEM_SPEC, ANY_SPEC], out_specs=[HBM_SPEC] * na,
        input_output_aliases={k: k for k in range(na)},
        compiler_params=pltpu.CompilerParams(has_side_effects=DATAFLOW),
    )(*arrays, send_sems, recv_sems, after)
    return outs[:ns], outs[ns:]


def _all_reduce_small(x):
    R = x.shape[0]

    def body(x_ref, o_ref, gath, send_sems, recv_sems):
        me = _my_index()
        gath[me] = x_ref[...]
        sends = []
        for off in range(1, N_DEV):
            to = (me + off) % N_DEV
            cp = pltpu.make_async_remote_copy(x_ref, gath.at[me], send_sems.at[off], recv_sems.at[off],
                                              device_id=_peer(to), device_id_type=MESH)
            cp.start()
            sends.append(cp)
        for off in range(1, N_DEV):
            frm = (me + N_DEV - off) % N_DEV
            pltpu.make_async_remote_copy(x_ref, gath.at[frm], send_sems.at[off], recv_sems.at[off],
                                         device_id=_peer(frm), device_id_type=MESH).wait_recv()
        for cp in sends:
            cp.wait_send()
        acc = gath[0]
        for s in range(1, N_DEV):
            acc = acc + gath[s]
        o_ref[...] = acc

    vm = pl.BlockSpec(memory_space=pltpu.VMEM)
    return pl.pallas_call(
        body, name="all_reduce_small", in_specs=[vm], out_specs=vm, out_shape=jax.ShapeDtypeStruct((R, 128), F32),
        scratch_shapes=[pltpu.VMEM((N_DEV, R, 128), F32), pltpu.SemaphoreType.DMA((N_DEV,)),
                        pltpu.SemaphoreType.DMA((N_DEV,))],
        compiler_params=pltpu.CompilerParams(has_side_effects=True),
    )(x)


def _adamw_math(w, g, m, v):
    m = ADAM_B1 * m + (1.0 - ADAM_B1) * g
    v = ADAM_B2 * v + (1.0 - ADAM_B2) * (g * g)
    m_hat = m / (1.0 - ADAM_B1 ** ADAM_STEP)
    v_hat = v / (1.0 - ADAM_B2 ** ADAM_STEP)
    delta = -ADAM_LR * (m_hat / (jnp.sqrt(v_hat) + ADAM_EPS) + ADAM_WD * w)
    return delta, m, v


def _adamw_layer(recv, own, kind, w, m, v, outs, layer, dep, *, name):
    nl, R, C = w.shape
    tr = _tile(R, (128, 64, 32, 16))

    def body(r_ref, o_ref, w_ref, m_ref, v_ref, _0, _1, _2, _3, _dep, g_out, d_out, m_out, v_out, token):
        token[...] = jnp.zeros_like(token)
        me = _my_index()
        mine = o_ref[...].astype(F32)
        g = jnp.where(me == 0, mine, r_ref[0].astype(F32))
        for s in range(1, N_DEV):
            g = g + jnp.where(me == s, mine, r_ref[s].astype(F32))
        delta, mn, vn = _adamw_math(w_ref[...], g, m_ref[...], v_ref[...])
        g_out[...] = g
        d_out[...] = delta
        m_out[...] = mn
        v_out[...] = vn

    if kind == "col":
        own_spec = pl.BlockSpec((None, tr, C), lambda t: (_my_index(), t, 0))
    else:
        own_spec = pl.BlockSpec((None, tr, C), lambda t: (0, _my_index() * (R // tr) + t, 0))
    wspec = pl.BlockSpec((None, tr, C), lambda t: (layer, t, 0))
    res = pl.pallas_call(
        body, name=name, grid=(R // tr,),
        in_specs=[pl.BlockSpec((N_DEV, tr, C), lambda t: (0, t, 0)), own_spec] + [wspec] * 3 + [ANY_SPEC] * 5,
        out_specs=[wspec] * 4 + [pl.BlockSpec((8, 128), lambda t: (0, 0))],
        out_shape=[jax.ShapeDtypeStruct((nl, R, C), F32)] * 4 + [jax.ShapeDtypeStruct((8, 128), F32)],
        input_output_aliases={5: 0, 6: 1, 7: 2, 8: 3},
        compiler_params=_params(("arbitrary",)),
    )(recv, own, w, m, v, *outs, dep)
    return res[:4], res[4]


def _adamw_small(g, w, m, v):
    def body(g_ref, w_ref, m_ref, v_ref, d_out, m_out, v_out):
        delta, mn, vn = _adamw_math(w_ref[...], g_ref[...], m_ref[...], v_ref[...])
        d_out[...] = delta
        m_out[...] = mn
        v_out[...] = vn

    return pl.pallas_call(body, name="adamw_small", out_shape=[jax.ShapeDtypeStruct(g.shape, F32)] * 3)(g, w, m, v)


def _pack(arrays, rows):
    flat = jnp.concatenate([a.reshape(-1) for a in arrays])
    return jnp.pad(flat, (0, rows * 128 - flat.shape[0])).reshape(rows, 128)


def _unpack(packed, shapes):
    flat = packed.reshape(-1)
    out, pos = [], 0
    for s in shapes:
        size = int(np.prod(s))
        out.append(flat[pos:pos + size].reshape(s))
        pos += size
    return out


def kernel(x, norm1_g, w_in, qk_norm_g, sink_a, rpb_c, w_br_a, w_br_b, w_br_c, w_o, norm2_g, w_gate_up, w_down, loss_target, m_norm1_g, m_w_in, m_qk_norm_g, m_sink_a, m_rpb_c, m_w_br_a, m_w_br_b, m_w_br_c, m_w_o, m_norm2_g, m_w_gate_up, m_w_down, v_norm1_g, v_w_in, v_qk_norm_g, v_sink_a, v_rpb_c, v_w_br_a, v_w_br_b, v_w_br_c, v_w_o, v_norm2_g, v_w_gate_up, v_w_down):
    nl = w_in.shape[0]
    L, D = x.shape[1], x.shape[2]
    x0 = x.reshape(L, D)
    tgt = loss_target.reshape(L, D)

    big = [w_in, w_br_a, w_br_b, w_br_c, w_o, w_gate_up, w_down]
    kinds = ["col", "col", "col", "col", "row", "col", "row"]

    big_names = ["w_in", "w_br_a", "w_br_b", "w_br_c", "w_o", "w_gate_up", "w_down"]
    ALL = list(range(len(big)))
    REST = ALL[1:]

    def gather_place(i):
        return [_place_cast(w, i, k, name="gather_place_" + n) for w, k, n in zip(big, kinds, big_names)]

    def gather_start(mode, lands, sub, after, tag):
        return _exchange_start(mode, [], lands, [kinds[j] for j in sub], after, name=mode + "_start" + tag)

    def gather_wait(mode, started, sub, after, tag):
        return _exchange_wait(mode, started, [kinds[j] for j in sub], after, name=mode + "_wait" + tag)[1]

    def matmul_views(lands, sub):
        return [g.reshape((N_DEV, 1) + g.shape[1:]) if kinds[j] == "col" else g.reshape((1, 1) + g.shape)
                for g, j in zip(lands, sub)]

    half = HEAD_DIM // 2
    inv_freq = ROPE_THETA ** (-jnp.arange(half, dtype=F32) * 2.0 / HEAD_DIM)
    ang = jnp.arange(L, dtype=F32)[:, None] * inv_freq[None, :]
    cos = jnp.concatenate([jnp.cos(ang), jnp.cos(ang)], axis=-1)
    sin = jnp.concatenate([-jnp.sin(ang), jnp.sin(ang)], axis=-1)
    expand = jnp.asarray(_c_expand_matrix(), BF16)
    expand_t = jnp.asarray(_c_expand_matrix().T, BF16)

    def gain_tables(i):
        g = qk_norm_g[i]
        gq = jnp.concatenate([jnp.tile(g[0][None], (8, 1)), jnp.tile(g[2][None], (12, 1)), jnp.tile(g[4][None], (8, 1))])
        gk = jnp.concatenate([jnp.tile(g[1][None], (2, 1)), jnp.tile(g[3][None], (12, 1)), jnp.tile(g[5][None], (8, 1))])
        return (gq.reshape(NQ_CHUNKS // Q_CG, 1, Q_CG * HEAD_DIM), gk.reshape(NK_CHUNKS // K_CG, 1, K_CG * HEAD_DIM))

    def bias_table(i):
        rp = jnp.pad(rpb_c[i].reshape(C_HEADS * C_NREL, C_NCOL), ((0, 0), (0, HEAD_DIM - C_NCOL)))
        t = _exact_mm(rp, expand_t, name="c_bias_expand")
        return _c_bias_tiles(t.reshape(C_HEADS, C_NREL, GRID_W, GRID_W))

    def sink_table(i):
        return jnp.broadcast_to(sink_a[i][:, None, None], (A_Q_HEADS, 1, HEAD_DIM))

    saved = []
    gws = [None] * nl
    xi = x0
    lands0 = gather_place(0)
    lvl1 = gather_start("gather1", lands0[:1], [0], x0, "_first")
    lvl2 = gather_start("gather2", gather_wait("gather1", lvl1, [0], x0, "_first"), [0], x0, "_first")
    gws[0] = matmul_views(gather_wait("gather2", lvl2, [0], x0, "_first"), [0])
    rest1 = gather_start("gather1", lands0[1:], REST, gws[0][0], "_rest")
    dep = rest1[4]
    for i in range(nl):
        qk_dep = None
        if i >= 1 and i + 1 < nl:
            nxt1 = gather_start("gather1", gather_place(i + 1), ALL, dep, "")
            dep = nxt1[4]
        gw_in = gws[i][0]
        gq, gk = gain_tables(i)
        bias_t = bias_table(i)
        sink = sink_table(i)
        h1 = _rms_fwd(xi, norm1_g[i][None], dep, name="rms1_fwd")
        proj = _mm_nn(h1, gw_in, 0, out_dtype=F32, name="proj_fwd")
        if i == 0:
            rest2 = gather_start("gather2", gather_wait("gather1", rest1, REST, proj, "_rest"), REST, proj, "_rest")
            qk_dep = rest2[4]
            if nl > 1:
                nxt1 = gather_start("gather1", gather_place(1), ALL, rest2[4], "")
                qk_dep = nxt1[4]
        qn = _qk_fwd(proj, gq, cos, sin, Q_PIECES, NQ_CHUNKS, Q_ROPE_UPTO, Q_CG, name="qnorm_fwd", dep=qk_dep)
        kn = _qk_fwd(proj, gk, cos, sin, K_PIECES, NK_CHUNKS, K_ROPE_UPTO, K_CG, name="knorm_fwd")
        oa, lse_a = _band_fwd(qn, kn, proj, dil=1, radius=A_RADIUS, nkv=A_KV_HEADS, group=A_GROUP,
                              q0=0, k0=0, v0=PC_VA, sink=sink, name="attn_a_fwd")
        obs, lbs = [], []
        for g, (window, dil) in enumerate(B_PATTERNS):
            o_g, l_g = _band_fwd(qn, kn, proj, dil=dil, radius=window // (2 * dil), nkv=B_HG, group=1,
                                 q0=8 + g * B_HG, k0=2 + g * B_HG, v0=PC_VB + g * B_HG, name=f"attn_b{g}_fwd")
            obs.append(o_g)
            lbs.append(l_g)
        ob, lse_b = _combine_b(obs, lbs, name="attn_b_combine")
        oc, lse_c = _c_fwd(qn, kn, proj, bias_t, name="attn_c_fwd")
        if i == 0:
            gws[0] = gws[0] + matmul_views(gather_wait("gather2", rest2, REST, oc, "_rest"), REST)
        gws[i][1:4] = [_blocks_to_wide(g, name="br_wide_" + n) for g, n in zip(gws[i][1:4], "abc")]
        _, gw_a, gw_b, gw_c, gw_o, gw_gu, gw_d = gws[i]
        ta = _mm_nn(oa, gw_a, 0, out_dtype=BF16, name="br_a_fwd")
        tb = _mm_nn(ob, gw_b, 0, out_dtype=BF16, name="br_b_fwd")
        tc = _mm_nn(oc, gw_c, 0, out_dtype=BF16, name="br_c_fwd")
        merged = _gate_fwd(proj, ta, tb, tc, name="gate_fwd")
        x1 = _mm_nn(merged, gw_o, 0, out_dtype=F32, name="wo_fwd", res=xi)
        dep = x1
        if i + 1 < nl:
            nxt2 = gather_start("gather2", gather_wait("gather1", nxt1, ALL, x1, ""), ALL, x1, "")
            dep = nxt2[4]
        h2 = _rms_fwd(x1, norm2_g[i][None], dep, name="rms2_fwd")
        gu = _mm_nn(h2, gw_gu, 0, out_dtype=BF16, name="gate_up_fwd")
        act = _swiglu_fwd(gu, name="swiglu_fwd")
        x2 = _mm_nn(act, gw_d, 0, out_dtype=F32, name="down_fwd", res=x1)
        saved.append(dict(x=xi, h1=h1, proj=proj, qn=qn, kn=kn, oa=oa, lse_a=lse_a, ob=ob, lse_b=lse_b, oc=oc,
                          lse_c=lse_c, ta=ta, tb=tb, tc=tc, merged=merged, x1=x1, h2=h2, gu=gu, act=act,
                          gq=gq, gk=gk, bias_t=bias_t, sink=sink))
        xi = x2
        dep = x2
        if i + 1 < nl:
            gws[i + 1] = matmul_views(gather_wait("gather2", nxt2, ALL, x2, ""), ALL)

    dx, dxb, loss_row = _loss(xi, tgt, name="loss")

    def scatter_start(grads, sub, after, tag):
        lands = []
        for g, j in zip(grads, sub):
            shape = g.shape if kinds[j] == "col" else (N_DEV, g.shape[1] // N_DEV, g.shape[2])
            lands.append(lax.empty(shape, BF16))
        return _exchange_start("scatter", grads, lands, [kinds[j] for j in sub], after, name="scatter_start" + tag)

    def scatter_wait(pair, after):
        own_a, recv_a = _exchange_wait("scatter", pair[0], [kinds[0]], after, name="scatter_wait_in")
        own_b, recv_b = _exchange_wait("scatter", pair[1], [kinds[j] for j in REST], after, name="scatter_wait_rest")
        return list(recv_a) + list(recv_b), list(own_a) + list(own_b)

    small_grads = [None] * nl
    recv = [None] * nl
    own = [None] * nl
    pending = None
    for i in reversed(range(nl)):
        s = saved[i]
        gw_in, gw_a, gw_b, gw_c, gw_o, gw_gu, gw_d = gws[i]
        dact = _mm_nt(dxb, gw_d, 0, out_dtype=BF16, name="down_bwd_x", dep=None if pending is None else pending[0][4])
        g_down = _mm_tn(s["act"], dxb, 1, name="down_bwd_w")
        dgu = _swiglu_bwd(s["gu"], dact, name="swiglu_bwd")
        g_gu = _mm_tn(s["h2"], dgu, N_DEV, name="gate_up_bwd_w")
        dh2 = _mm_nt(dgu, gw_gu, 0, out_dtype=F32, name="gate_up_bwd_x")
        dx1, dx1b, dg2 = _rms_bwd(s["x1"], norm2_g[i][None], dh2, dx, name="rms2_bwd")
        dmerged = _mm_nt(dx1b, gw_o, 0, out_dtype=F32, name="wo_bwd_x")
        g_o = _mm_tn(s["merged"], dx1b, 1, name="wo_bwd_w")
        dta, dtb, dtc, dproj = _gate_bwd(s["proj"], s["ta"], s["tb"], s["tc"], dmerged, name="gate_bwd")
        g_a = _wide_to_blocks(_mm_tn(s["oa"], dta, 1, name="br_a_bwd_w"), N_DEV, name="br_blocks_a")
        g_b = _wide_to_blocks(_mm_tn(s["ob"], dtb, 1, name="br_b_bwd_w"), N_DEV, name="br_blocks_b")
        g_c = _wide_to_blocks(_mm_tn(s["oc"], dtc, 1, name="br_c_bwd_w"), N_DEV, name="br_blocks_c")
        rest = scatter_start([g_a, g_b, g_c, g_o, g_gu, g_down], REST, g_c, "_rest")
        doa = _mm_nt(dta, gw_a, 0, out_dtype=F32, name="br_a_bwd_x", dep=rest[4])
        dob = _mm_nt(dtb, gw_b, 0, out_dtype=F32, name="br_b_bwd_x")
        doc = _mm_nt(dtc, gw_c, 0, out_dtype=F32, name="br_c_bwd_x")
        dq_buf = lax.empty((L, NQ_CHUNKS * HEAD_DIM), F32)
        dk_buf = lax.empty((L, NK_CHUNKS * HEAD_DIM), F32)
        dv_buf = lax.empty((L, NK_CHUNKS * HEAD_DIM), F32)
        dq_buf, dk_buf, dv_buf, dsink = _band_bwd(
            s["qn"], s["kn"], s["proj"], doa, s["oa"], s["lse_a"], dq_buf, dk_buf, dv_buf, dil=1, radius=A_RADIUS,
            nkv=A_KV_HEADS, group=A_GROUP, q0=0, k0=0, v0=PC_VA, o0=0, sink=s["sink"], name="attn_a_bwd")
        for g, (window, dil) in enumerate(B_PATTERNS):
            dq_buf, dk_buf, dv_buf = _band_bwd(
                s["qn"], s["kn"], s["proj"], dob, s["ob"], s["lse_b"], dq_buf, dk_buf, dv_buf, dil=dil,
                radius=window // (2 * dil), nkv=B_HG, group=1, q0=8 + g * B_HG, k0=2 + g * B_HG,
                v0=PC_VB + g * B_HG, o0=0, name=f"attn_b{g}_bwd")
        dq_buf, dk_buf, dv_buf, dbias_t = _c_bwd(s["qn"], s["kn"], s["proj"], s["bias_t"], doc, s["oc"], s["lse_c"],
                                                 dq_buf, dk_buf, dv_buf, name="attn_c_bwd")
        dproj, dgq = _qk_bwd(dq_buf, s["proj"], s["gq"], cos, sin, dproj, Q_PIECES, NQ_CHUNKS, Q_ROPE_UPTO, Q_CG,
                             name="qnorm_bwd")
        dproj, dgk = _qk_bwd(dk_buf, s["proj"], s["gk"], cos, sin, dproj, K_PIECES, NK_CHUNKS, K_ROPE_UPTO, K_CG,
                             name="knorm_bwd")
        dproj = _v_bwd(dv_buf, dproj, name="v_bwd")
        g_in = _mm_tn(s["h1"], dproj, N_DEV, name="proj_bwd_w")
        dh1 = _mm_nt(dproj, gw_in, 0, out_dtype=F32, name="proj_bwd_x")
        dx, dxb, dg1 = _rms_bwd(s["x"], norm1_g[i][None], dh1, dx1, name="rms1_bwd")
        if pending is not None:
            recv[i + 1], own[i + 1] = scatter_wait(pending, dx)
        pending = (scatter_start([g_in], [0], dx, "_in"), rest)

        drpb = _exact_mm(dbias_t.reshape(C_HEADS * C_NREL, GRID_W * GRID_W), expand, name="c_bias_reduce")
        dgq, dgk = dgq.reshape(NQ_CHUNKS, HEAD_DIM), dgk.reshape(NK_CHUNKS, HEAD_DIM)
        dqk_g = jnp.stack([dgq[0:8].sum(0), dgk[0:2].sum(0), dgq[8:20].sum(0), dgk[2:14].sum(0),
                           dgq[20:28].sum(0), dgk[14:22].sum(0)])
        small_grads[i] = (dg1.reshape(D), dqk_g, dsink[:, 0, 0],
                          drpb[:, :C_NCOL].reshape(C_HEADS, C_NREL, C_NCOL), dg2.reshape(D))

    small_names = [norm1_g, qk_norm_g, sink_a, rpb_c, norm2_g]
    small_m = [m_norm1_g, m_qk_norm_g, m_sink_a, m_rpb_c, m_norm2_g]
    small_v = [v_norm1_g, v_qk_norm_g, v_sink_a, v_rpb_c, v_norm2_g]
    shapes = [a.shape for a in small_names]
    total = sum(int(np.prod(sh)) for sh in shapes) + 128
    rows = -(-total // 1024) * 8
    stacked = [jnp.stack([small_grads[i][j] for i in range(nl)]) for j in range(5)]
    packed = _pack([loss_row.reshape(-1)] + stacked, rows)
    summed = _all_reduce_small(packed)
    loss = summed[0, 0]
    zero_row = jnp.zeros((128,), F32)
    d_s, m_s, v_s = _adamw_small(summed, _pack([zero_row] + small_names, rows), _pack([zero_row] + small_m, rows),
                                 _pack([zero_row] + small_v, rows))
    shapes1 = [(128,)] + shapes
    g_small = _unpack(summed, shapes1)[1:]
    d_small = _unpack(d_s, shapes1)[1:]
    m_small = _unpack(m_s, shapes1)[1:]
    v_small = _unpack(v_s, shapes1)[1:]

    big_m = [m_w_in, m_w_br_a, m_w_br_b, m_w_br_c, m_w_o, m_w_gate_up, m_w_down]
    big_v = [v_w_in, v_w_br_a, v_w_br_b, v_w_br_c, v_w_o, v_w_gate_up, v_w_down]
    big_out = [[lax.empty(w.shape, F32) for _ in range(4)] for w in big]
    token = pending[0][4]
    for i in list(range(nl - 1, 0, -1)) + [0]:
        if i == 0:
            recv[0], own[0] = scatter_wait(pending, token)
        for j in range(len(big)):
            big_out[j], token = _adamw_layer(recv[i][j], own[i][j], kinds[j], big[j], big_m[j], big_v[j],
                                             big_out[j], i, token, name="adamw_" + big_names[j])

    order = ["norm1_g", "w_in", "qk_norm_g", "sink_a", "rpb_c", "w_br_a", "w_br_b", "w_br_c", "w_o", "norm2_g",
             "w_gate_up", "w_down"]
    small_idx = {"norm1_g": 0, "qk_norm_g": 1, "sink_a": 2, "rpb_c": 3, "norm2_g": 4}
    big_idx = {n: j for j, n in enumerate(big_names)}

    def pick(kind):
        out = []
        for n in order:
            if n in small_idx:
                out.append([g_small, d_small, m_small, v_small][kind][small_idx[n]])
            else:
                out.append(big_out[big_idx[n]][kind])
        return out

    return (loss, dx.reshape(1, L, D), *pick(0), *pick(1), *pick(2), *pick(3))
```

```python
import functools
import math

import numpy as np
import jax
import jax.numpy as jnp
from jax import lax
from jax.experimental import pallas as pl
from jax.experimental.pallas import tpu as pltpu

F32 = jnp.float32
BF16 = jnp.bfloat16
MESH = pl.DeviceIdType.MESH
N_DEV = 8

HEAD_DIM = 128
NORM_EPS = 1e-6
ROPE_THETA = 10000.0
ATT_SCALE = HEAD_DIM ** -0.5
NEG = -1e30

A_Q_HEADS, A_KV_HEADS, A_RADIUS = 8, 2, 128
A_GROUP = A_Q_HEADS // A_KV_HEADS
B_PATTERNS = ((128, 1), (512, 4), (2048, 16))
B_HG = 4
B_HEADS = len(B_PATTERNS) * B_HG
C_HEADS, GRID_W, C_WIN_ROWS, C_WIN_COLS = 8, 64, 8, 16
C_NREL = 2 * C_WIN_ROWS - 1
C_NCOL = 2 * C_WIN_COLS - 1

PC_QA, PC_KA, PC_VA = 0, 8, 10
PC_QB, PC_KB, PC_VB = 12, 24, 36
PC_QC, PC_KC, PC_VC = 48, 56, 64
N_QKV_CHUNKS = 72
Q_PIECES = ((0, 8, PC_QA), (8, 12, PC_QB), (20, 8, PC_QC))
K_PIECES = ((0, 2, PC_KA), (2, 12, PC_KB), (14, 8, PC_KC))
V_PIECES = ((0, 2, PC_VA), (2, 12, PC_VB), (14, 8, PC_VC))
NQ_CHUNKS, NK_CHUNKS = 28, 22
Q_ROPE_UPTO, K_ROPE_UPTO = 20, 14
Q_CG, K_CG = 4, 2

ADAM_LR, ADAM_B1, ADAM_B2, ADAM_EPS, ADAM_WD, ADAM_STEP = 0.001, 0.9, 0.999, 1e-08, 0.01, 10

VMEM_LIMIT = 48 * 1024 * 1024


def _tile(dim, prefs):
    for p in prefs:
        if dim % p == 0:
            return p
    return dim


NN_WEIGHT_TILE_BYTES = 8 * 1024 * 1024
NT_WEIGHT_TILE_BYTES = 4 * 1024 * 1024
TN_ACC_BYTES = 6 * 1024 * 1024
MAX_COL_TILE = 2048


def _col_tile(ns):
    return ns if ns <= MAX_COL_TILE else _tile(ns, (MAX_COL_TILE, 1024, 512, 256, 128))


def _params(sem, **kw):
    return pltpu.CompilerParams(dimension_semantics=sem, vmem_limit_bytes=VMEM_LIMIT, **kw)


def _piece_map(pieces):
    def f(c):
        out = c - pieces[0][0] + pieces[0][2]
        for first, _, pfirst in pieces[1:]:
            out = jnp.where(c >= first, c - first + pfirst, out)
        return out
    return f


def _mm_nn(a, w, layer, *, out_dtype, name, res=None):
    M, K = a.shape
    nb, _, Kw, ns = w.shape
    assert Kw == K
    tn = _col_tile(ns)
    tm = _tile(M, (1024, 512, 256) if tn <= 512 else (512, 256))
    tk = _tile(K, tuple(t for t in (2048, 1408, 1024, 512, 256) if t * tn * 2 <= NN_WEIGHT_TILE_BYTES))
    nj, nk = ns // tn, K // tk

    def body(*refs):
        a_ref, w_ref = refs[:2]
        r_ref = None if res is None else refs[2]
        o_ref = refs[2 if res is None else 3]
        part = jnp.dot(a_ref[...].astype(BF16), w_ref[...], preferred_element_type=F32)
        if nk == 1:
            if r_ref is not None:
                part = part + r_ref[...]
            o_ref[...] = part.astype(out_dtype)
            return
        acc_ref = refs[-1]
        k = pl.program_id(3)

        @pl.when(k == 0)
        def _():
            acc_ref[...] = part

        @pl.when(k > 0)
        def _():
            acc_ref[...] += part

        @pl.when(k == nk - 1)
        def _():
            r = acc_ref[...]
            if r_ref is not None:
                r = r + r_ref[...]
            o_ref[...] = r.astype(out_dtype)

    in_specs = [pl.BlockSpec((tm, tk), lambda i, b, j, k: (i, k)),
                pl.BlockSpec((None, None, tk, tn), lambda i, b, j, k: (b, layer, k, j))]
    args = [a, w]
    if res is not None:
        in_specs.append(pl.BlockSpec((tm, tn), lambda i, b, j, k: (i, b * nj + j)))
        args.append(res)
    return pl.pallas_call(
        body, name=name, grid=(M // tm, nb, nj, nk), in_specs=in_specs,
        out_specs=pl.BlockSpec((tm, tn), lambda i, b, j, k: (i, b * nj + j)),
        out_shape=jax.ShapeDtypeStruct((M, nb * ns), out_dtype),
        scratch_shapes=[] if nk == 1 else [pltpu.VMEM((tm, tn), F32)],
        compiler_params=_params(("parallel", "parallel", "parallel", "arbitrary")),
    )(*args)


def _mm_nt(a, w, layer, *, out_dtype, name, dep=None):
    M, N = a.shape
    nb, _, K, ns = w.shape
    assert N == nb * ns
    tm = _tile(M, (1024, 512, 256))
    tn = _col_tile(ns)
    tk = _tile(K, tuple(t for t in (1024, 512, 256) if t * tn * 2 <= NT_WEIGHT_TILE_BYTES))
    nj = ns // tn
    nred = nb * nj

    def body(*refs):
        a_ref, w_ref = refs[:2]
        o_ref, acc_ref = refs[-2:]
        s = pl.program_id(2) * nj + pl.program_id(3)
        part = lax.dot_general(a_ref[...].astype(BF16), w_ref[...], (((1,), (1,)), ((), ())),
                               preferred_element_type=F32)

        @pl.when(s == 0)
        def _():
            acc_ref[...] = part

        @pl.when(s > 0)
        def _():
            acc_ref[...] += part

        @pl.when(s == nred - 1)
        def _():
            o_ref[...] = acc_ref[...].astype(out_dtype)

    in_specs = [pl.BlockSpec((tm, tn), lambda i, kk, b, j: (i, b * nj + j)),
                pl.BlockSpec((None, None, tk, tn), lambda i, kk, b, j: (b, layer, kk, j))]
    args = [a, w]
    if dep is not None:
        in_specs.append(ANY_SPEC)
        args.append(dep)
    return pl.pallas_call(
        body, name=name, grid=(M // tm, K // tk, nb, nj), in_specs=in_specs,
        out_specs=pl.BlockSpec((tm, tk), lambda i, kk, b, j: (i, kk)),
        out_shape=jax.ShapeDtypeStruct((M, K), out_dtype),
        scratch_shapes=[pltpu.VMEM((tm, tk), F32)],
        compiler_params=_params(("parallel", "parallel", "arbitrary", "arbitrary")),
    )(*args)


def _mm_tn(a, g, nb, *, name):
    M, Ka = a.shape
    N = g.shape[1]
    ns = N // nb
    tn = _col_tile(ns)
    tka = _tile(Ka, tuple(t for t in (1024, 512, 256) if t * tn * 4 <= TN_ACC_BYTES))
    tm = _tile(M, (2048, 1024, 512, 256))
    nj, nm = ns // tn, M // tm

    def body(a_ref, g_ref, o_ref, acc_ref):
        m = pl.program_id(3)
        part = lax.dot_general(a_ref[...].astype(BF16), g_ref[...].astype(BF16), (((0,), (0,)), ((), ())),
                               preferred_element_type=F32)

        @pl.when(m == 0)
        def _():
            acc_ref[...] = part

        @pl.when(m > 0)
        def _():
            acc_ref[...] += part

        @pl.when(m == nm - 1)
        def _():
            o_ref[...] = acc_ref[...].astype(BF16)

    return pl.pallas_call(
        body, name=name, grid=(Ka // tka, nb, nj, nm),
        in_specs=[pl.BlockSpec((tm, tka), lambda ka, b, j, m: (m, ka)),
                  pl.BlockSpec((tm, tn), lambda ka, b, j, m: (m, b * nj + j))],
        out_specs=pl.BlockSpec((None, tka, tn), lambda ka, b, j, m: (b, ka, j)),
        out_shape=jax.ShapeDtypeStruct((nb, Ka, ns), BF16),
        scratch_shapes=[pltpu.VMEM((tka, tn), F32)],
        compiler_params=_params(("parallel", "parallel", "parallel", "arbitrary")),
    )(a, g)


def _blocks_to_wide(w, *, name):
    nb, _, K, ns = w.shape

    def body(i_ref, o_ref):
        o_ref[...] = i_ref[...]

    return pl.pallas_call(
        body, name=name, grid=(nb,), in_specs=[pl.BlockSpec((None, None, K, ns), lambda b: (b, 0, 0, 0))],
        out_specs=pl.BlockSpec((None, None, K, ns), lambda b: (0, 0, 0, b)),
        out_shape=jax.ShapeDtypeStruct((1, 1, K, nb * ns), w.dtype), compiler_params=_params(("parallel",)),
    )(w)


def _wide_to_blocks(g, nb, *, name):
    _, K, N = g.shape
    ns = N // nb

    def body(i_ref, o_ref):
        o_ref[...] = i_ref[...]

    return pl.pallas_call(
        body, name=name, grid=(nb,), in_specs=[pl.BlockSpec((None, K, ns), lambda b: (0, 0, b))],
        out_specs=pl.BlockSpec((None, K, ns), lambda b: (b, 0, 0)),
        out_shape=jax.ShapeDtypeStruct((nb, K, ns), g.dtype), compiler_params=_params(("parallel",)),
    )(g)


def _exact_mm(a, e, *, name):
    R, K = a.shape
    N = e.shape[1]

    def body(a_ref, e_ref, o_ref):
        x = a_ref[...]
        hi = x.astype(BF16)
        r1 = x - hi.astype(F32)
        mid = r1.astype(BF16)
        lo = (r1 - mid.astype(F32)).astype(BF16)
        ev = e_ref[...]
        o_ref[...] = (jnp.dot(hi, ev, preferred_element_type=F32) + jnp.dot(mid, ev, preferred_element_type=F32)
                      + jnp.dot(lo, ev, preferred_element_type=F32))

    return pl.pallas_call(body, name=name, out_shape=jax.ShapeDtypeStruct((R, N), F32),
                          compiler_params=pltpu.CompilerParams(vmem_limit_bytes=VMEM_LIMIT))(a, e)


def _rms_fwd(x, g, dep, *, name):
    L, D = x.shape
    tl = _tile(L, (512, 256, 128))

    def body(x_ref, g_ref, _dep, h_ref):
        xv = x_ref[...]
        rstd = lax.rsqrt(jnp.mean(xv * xv, axis=-1, keepdims=True) + NORM_EPS)
        h_ref[...] = (xv * rstd * g_ref[...]).astype(BF16)

    return pl.pallas_call(
        body, name=name, grid=(L // tl,),
        in_specs=[pl.BlockSpec((tl, D), lambda t: (t, 0)), pl.BlockSpec((1, D), lambda t: (0, 0)), ANY_SPEC],
        out_specs=pl.BlockSpec((tl, D), lambda t: (t, 0)),
        out_shape=jax.ShapeDtypeStruct((L, D), BF16),
        compiler_params=_params(("parallel",)),
    )(x, g, dep)


def _rms_bwd(x, g, dy, dres, *, name):
    L, D = x.shape
    tl = _tile(L, (256, 128))

    def body(x_ref, g_ref, dy_ref, dres_ref, dx_ref, dxb_ref, dg_ref):
        t = pl.program_id(0)
        xv = x_ref[...]
        rstd = lax.rsqrt(jnp.mean(xv * xv, axis=-1, keepdims=True) + NORM_EPS)
        xhat = xv * rstd
        dyv = dy_ref[...]
        dxhat = dyv * g_ref[...]
        c = jnp.mean(dxhat * xhat, axis=-1, keepdims=True)
        dx = dres_ref[...] + rstd * (dxhat - xhat * c)
        dx_ref[...] = dx
        dxb_ref[...] = dx.astype(BF16)
        dgp = jnp.sum(dyv * xhat, axis=0, keepdims=True)

        @pl.when(t == 0)
        def _():
            dg_ref[...] = dgp

        @pl.when(t > 0)
        def _():
            dg_ref[...] += dgp

    row = pl.BlockSpec((tl, D), lambda t: (t, 0))
    vec = pl.BlockSpec((1, D), lambda t: (0, 0))
    return pl.pallas_call(
        body, name=name, grid=(L // tl,), in_specs=[row, vec, row, row], out_specs=[row, row, vec],
        out_shape=[jax.ShapeDtypeStruct((L, D), F32), jax.ShapeDtypeStruct((L, D), BF16),
                   jax.ShapeDtypeStruct((1, D), F32)],
        compiler_params=_params(("arbitrary",)),
    )(x, g, dy, dres)


def _gate_fwd(proj, ta, tb, tc, *, name):
    L, D = ta.shape
    tl, tcw = _tile(L, (512, 256, 128)), _tile(D, (1024, 512, 256, 128))
    off = N_QKV_CHUNKS * HEAD_DIM // tcw
    nd = D // tcw

    def body(g0, g1, g2, a_ref, b_ref, c_ref, o_ref):
        m = (jax.nn.sigmoid(g0[...]) * a_ref[...].astype(F32) + jax.nn.sigmoid(g1[...]) * b_ref[...].astype(F32)
             + jax.nn.sigmoid(g2[...]) * c_ref[...].astype(F32))
        o_ref[...] = m.astype(BF16)

    blk = pl.BlockSpec((tl, tcw), lambda t, j: (t, j))
    gl = [pl.BlockSpec((tl, tcw), functools.partial(lambda t, j, i: (t, off + i * nd + j), i=i)) for i in range(3)]
    return pl.pallas_call(
        body, name=name, grid=(L // tl, nd), in_specs=gl + [blk, blk, blk], out_specs=blk,
        out_shape=jax.ShapeDtypeStruct((L, D), BF16),
        compiler_params=_params(("parallel", "parallel")),
    )(proj, proj, proj, ta, tb, tc)


def _gate_bwd(proj, ta, tb, tc, dmerged, *, name):
    L, D = ta.shape
    ncols = proj.shape[1]
    tl, tcw = _tile(L, (512, 256, 128)), _tile(D, (1024, 512, 256, 128))
    off = N_QKV_CHUNKS * HEAD_DIM // tcw
    nd = D // tcw

    def body(g0, g1, g2, a_ref, b_ref, c_ref, dm_ref, da_ref, db_ref, dc_ref, dgl_ref):
        i = pl.program_id(2)
        sg = jax.nn.sigmoid(jnp.where(i == 0, g0[...], jnp.where(i == 1, g1[...], g2[...])))
        sel_t = jnp.where(i == 0, a_ref[...], jnp.where(i == 1, b_ref[...], c_ref[...])).astype(F32)
        dt = dm_ref[...] * sg
        dtb = dt.astype(BF16)

        @pl.when(i == 0)
        def _():
            da_ref[...] = dtb

        @pl.when(i == 1)
        def _():
            db_ref[...] = dtb

        @pl.when(i == 2)
        def _():
            dc_ref[...] = dtb

        dgl_ref[...] = (dt * sel_t * (1.0 - sg)).astype(BF16)

    blk = pl.BlockSpec((tl, tcw), lambda t, j, i: (t, j))
    gl = [pl.BlockSpec((tl, tcw), functools.partial(lambda t, j, i, q: (t, off + q * nd + j), q=q)) for q in range(3)]
    return pl.pallas_call(
        body, name=name, grid=(L // tl, nd, 3), in_specs=gl + [blk, blk, blk, blk],
        out_specs=[blk, blk, blk, pl.BlockSpec((tl, tcw), lambda t, j, i: (t, off + i * nd + j))],
        out_shape=[jax.ShapeDtypeStruct((L, D), BF16)] * 3 + [jax.ShapeDtypeStruct((L, ncols), BF16)],
        compiler_params=_params(("parallel", "parallel", "arbitrary")),
    )(proj, proj, proj, ta, tb, tc, dmerged)


def _swiglu_fwd(gu, *, name):
    L, F2 = gu.shape
    F = F2 // 2
    tl = _tile(L, (128, 64))

    def body(gu_ref, o_ref):
        gt = gu_ref[:, :F].astype(F32)
        o_ref[...] = (gt * jax.nn.sigmoid(gt) * gu_ref[:, F:].astype(F32)).astype(BF16)

    return pl.pallas_call(
        body, name=name, grid=(L // tl,), in_specs=[pl.BlockSpec((tl, F2), lambda t: (t, 0))],
        out_specs=pl.BlockSpec((tl, F), lambda t: (t, 0)),
        out_shape=jax.ShapeDtypeStruct((L, F), BF16),
        compiler_params=_params(("parallel",)),
    )(gu)


def _swiglu_bwd(gu, dact, *, name):
    L, F2 = gu.shape
    F = F2 // 2
    tl = _tile(L, (128, 64))

    def body(gu_ref, d_ref, o_ref):
        gt, up, d = gu_ref[:, :F].astype(F32), gu_ref[:, F:].astype(F32), d_ref[...].astype(F32)
        sg = jax.nn.sigmoid(gt)
        o_ref[:, :F] = (d * up * sg * (1.0 + gt * (1.0 - sg))).astype(BF16)
        o_ref[:, F:] = (d * gt * sg).astype(BF16)

    return pl.pallas_call(
        body, name=name, grid=(L // tl,),
        in_specs=[pl.BlockSpec((tl, F2), lambda t: (t, 0)), pl.BlockSpec((tl, F), lambda t: (t, 0))],
        out_specs=pl.BlockSpec((tl, F2), lambda t: (t, 0)),
        out_shape=jax.ShapeDtypeStruct((L, F2), BF16),
        compiler_params=_params(("parallel",)),
    )(gu, dact)


def _loss(y, tgt, *, name):
    L, D = y.shape
    tl = _tile(L, (256, 128))
    nt = L // tl

    def body(y_ref, t_ref, dy_ref, dyb_ref, loss_ref, acc_ref):
        t = pl.program_id(0)
        e = y_ref[...] - t_ref[...]
        dy = e * (1.0 / D)
        dy_ref[...] = dy
        dyb_ref[...] = dy.astype(BF16)
        part = jnp.sum(e * e, axis=0, keepdims=True)

        @pl.when(t == 0)
        def _():
            acc_ref[...] = part

        @pl.when(t > 0)
        def _():
            acc_ref[...] += part

        @pl.when(t == nt - 1)
        def _():
            loss_ref[...] = jnp.broadcast_to(jnp.sum(acc_ref[...], axis=-1, keepdims=True) * (0.5 / D), (1, 128))

    row = pl.BlockSpec((tl, D), lambda t: (t, 0))
    return pl.pallas_call(
        body, name=name, grid=(nt,), in_specs=[row, row],
        out_specs=[row, row, pl.BlockSpec((1, 128), lambda t: (0, 0))],
        out_shape=[jax.ShapeDtypeStruct((L, D), F32), jax.ShapeDtypeStruct((L, D), BF16),
                   jax.ShapeDtypeStruct((1, 128), F32)],
        scratch_shapes=[pltpu.VMEM((1, D), F32)],
        compiler_params=_params(("arbitrary",)),
    )(y, tgt)


def _rope(v, cos, sin_signed):
    return v * cos + pltpu.roll(v, HEAD_DIM // 2, 1) * sin_signed


def _head_mean(x, j=0):
    if j % 2:
        return jnp.mean(x, axis=-1, keepdims=True)
    hi = x.astype(BF16)
    lo = (x - hi.astype(F32)).astype(BF16)
    ones = jnp.ones((HEAD_DIM, HEAD_DIM), BF16)
    total = jnp.dot(hi, ones, preferred_element_type=F32) + jnp.dot(lo, ones, preferred_element_type=F32)
    return total * (1.0 / HEAD_DIM)


def _qk_fwd(proj, gtab, cos, sin, pieces, nchunks, rope_upto, cg, *, name, dep=None):
    L = proj.shape[0]
    tl = _tile(L, (512, 256, 128))
    W = cg * HEAD_DIM
    pmap = _piece_map(tuple((a // cg, n // cg, p // cg) for a, n, p in pieces))

    def body(*refs):
        p_ref, g_ref, cos_ref, sin_ref = refs[:4]
        o_ref = refs[-1]
        c = pl.program_id(1)

        def norm(j):
            cols = slice(j * HEAD_DIM, (j + 1) * HEAD_DIM)
            x = p_ref[:, cols]
            rstd = lax.rsqrt(_head_mean(x * x, j) + NORM_EPS)
            return cols, x * rstd * g_ref[:, cols]

        @pl.when(c < rope_upto // cg)
        def _():
            for j in range(cg):
                cols, y = norm(j)
                o_ref[:, cols] = _rope(y, cos_ref[...], sin_ref[...])

        @pl.when(c >= rope_upto // cg)
        def _():
            for j in range(cg):
                cols, y = norm(j)
                o_ref[:, cols] = y

    pos = pl.BlockSpec((tl, HEAD_DIM), lambda t, c: (t, 0))
    in_specs = [pl.BlockSpec((tl, W), lambda t, c: (t, pmap(c))),
                pl.BlockSpec((None, 1, W), lambda t, c: (c, 0, 0)), pos, pos]
    args = [proj, gtab, cos, sin]
    if dep is not None:
        in_specs.append(ANY_SPEC)
        args.append(dep)
    return pl.pallas_call(
        body, name=name, grid=(L // tl, nchunks // cg), in_specs=in_specs,
        out_specs=pl.BlockSpec((tl, W), lambda t, c: (t, c)),
        out_shape=jax.ShapeDtypeStruct((L, nchunks * HEAD_DIM), F32),
        compiler_params=_params(("parallel", "parallel")),
    )(*args)


def _qk_bwd(dqk, proj, gtab, cos, sin, dproj, pieces, nchunks, rope_upto, cg, *, name):
    L = proj.shape[0]
    tl = _tile(L, (512, 256, 128))
    W = cg * HEAD_DIM
    pmap = _piece_map(tuple((a // cg, n // cg, p // cg) for a, n, p in pieces))

    def body(d_ref, p_ref, g_ref, cos_ref, sin_ref, _, o_ref, dg_ref):
        c, t = pl.program_id(0), pl.program_id(1)

        @pl.when(t == 0)
        def _():
            dg_ref[...] = jnp.zeros_like(dg_ref)

        for j in range(cg):
            cols = slice(j * HEAD_DIM, (j + 1) * HEAD_DIM)
            x = p_ref[:, cols]
            rstd = lax.rsqrt(_head_mean(x * x, j) + NORM_EPS)
            xhat = x * rstd
            dy = d_ref[:, cols]
            dy = jnp.where(c < rope_upto // cg, _rope(dy, cos_ref[...], -sin_ref[...]), dy)
            dxhat = dy * g_ref[:, cols]
            cm = _head_mean(dxhat * xhat, j)
            o_ref[:, cols] = (rstd * (dxhat - xhat * cm)).astype(BF16)
            dg_ref[:, cols] += jnp.sum(dy * xhat, axis=0, keepdims=True)

    pos = pl.BlockSpec((tl, HEAD_DIM), lambda c, t: (t, 0))
    gspec = pl.BlockSpec((None, 1, W), lambda c, t: (c, 0, 0))
    out, dg = pl.pallas_call(
        body, name=name, grid=(nchunks // cg, L // tl),
        in_specs=[pl.BlockSpec((tl, W), lambda c, t: (t, c)),
                  pl.BlockSpec((tl, W), lambda c, t: (t, pmap(c))), gspec, pos, pos,
                  pl.BlockSpec(memory_space=pl.ANY)],
        out_specs=[pl.BlockSpec((tl, W), lambda c, t: (t, pmap(c))), gspec],
        out_shape=[jax.ShapeDtypeStruct(dproj.shape, BF16), jax.ShapeDtypeStruct((nchunks // cg, 1, W), F32)],
        input_output_aliases={5: 0},
        compiler_params=_params(("parallel", "arbitrary")),
    )(dqk, proj, gtab, cos, sin, dproj)
    return out, dg


def _v_bwd(dv, dproj, *, name):
    L = dv.shape[0]
    tl = _tile(L, (2048, 1024, 512, 256, 128))
    pmap = _piece_map(tuple((a // 2, n // 2, p // 2) for a, n, p in V_PIECES))

    def body(d_ref, _, o_ref):
        o_ref[...] = d_ref[...].astype(BF16)

    return pl.pallas_call(
        body, name=name, grid=(L // tl, NK_CHUNKS // 2),
        in_specs=[pl.BlockSpec((tl, 2 * HEAD_DIM), lambda t, c: (t, c)), pl.BlockSpec(memory_space=pl.ANY)],
        out_specs=pl.BlockSpec((tl, 2 * HEAD_DIM), lambda t, c: (t, pmap(c))),
        out_shape=jax.ShapeDtypeStruct(dproj.shape, BF16),
        input_output_aliases={1: 0},
        compiler_params=_params(("parallel", "parallel")),
    )(dv, dproj)


def _band_geometry(L, dil, radius):
    n = L // dil
    bq = min(256, max(n // 2, 64), n)
    width = min(bq + 2 * radius, n)
    nsub = _tile(n // bq, (4, 2)) if dil == 1 else 1
    return n, bq, width, nsub


def _band_loop(dil, one):
    if dil == 1:
        one(0, 0)
    else:
        lax.fori_loop(0, dil, one, 0, unroll=min(dil, 4))


def _band_rows(dil, r, first, count):
    if dil == 1:
        return pl.ds(pl.multiple_of(first, 8), count)
    return pl.ds(r + first * dil, count, stride=dil)


def _band_mask(i, bq, width, radius, ws):
    qpos = i * bq + lax.broadcasted_iota(jnp.int32, (bq, width), 0)
    kpos = ws + lax.broadcasted_iota(jnp.int32, (bq, width), 1)
    return jnp.abs(kpos - qpos) <= radius


def _band_fwd(qn, kn, proj, *, dil, radius, nkv, group, q0, k0, v0, sink=None, name):
    L = qn.shape[0]
    n, bq, width, nsub = _band_geometry(L, dil, radius)
    tq = nsub * bq * dil
    nh = nkv * group

    def body(*refs):
        if sink is None:
            q_ref, k_ref, v_ref, o_ref, lse_ref = refs
        else:
            q_ref, k_ref, v_ref, s_ref, o_ref, lse_ref = refs
        for sb in range(nsub):
            block(sb, q_ref, k_ref, v_ref, None if sink is None else s_ref, o_ref, lse_ref)

    def block(sb, q_ref, k_ref, v_ref, s_ref, o_ref, lse_ref):
        i = pl.program_id(2) * nsub + sb
        ws = jnp.clip(i * bq - radius, 0, n - width)
        valid = _band_mask(i, bq, width, radius, ws)

        def one(r, carry):
            qrows = _band_rows(dil, r, sb * bq, bq)
            krows = _band_rows(dil, r, ws, width)
            q = q_ref[qrows, :].astype(BF16)
            k = k_ref[krows, :].astype(BF16)
            v = v_ref[krows, :].astype(BF16)
            s = lax.dot_general(q, k, (((1,), (1,)), ((), ())), preferred_element_type=F32) * ATT_SCALE
            s = jnp.where(valid, s, NEG)
            m = jnp.max(s, axis=-1, keepdims=True)
            if sink is not None:
                m = jnp.maximum(m, s_ref[...][:, :1])
            p = jnp.exp(s - m)
            denom = jnp.sum(p, axis=-1, keepdims=True)
            if sink is not None:
                denom = denom + jnp.exp(s_ref[...][:, :1] - m)
            pn = (p / denom).astype(BF16)
            o_ref[qrows, :] = jnp.dot(pn, v, preferred_element_type=F32)
            lse_ref[qrows, :] = jnp.broadcast_to(m + jnp.log(denom), (bq, HEAD_DIM))
            return carry

        _band_loop(dil, one)

    qspec = pl.BlockSpec((tq, HEAD_DIM), lambda hk, g, i: (i, q0 + hk * group + g))
    in_specs = [qspec,
                pl.BlockSpec((L, HEAD_DIM), lambda hk, g, i: (0, k0 + hk)),
                pl.BlockSpec((L, HEAD_DIM), lambda hk, g, i: (0, v0 + hk))]
    args = [qn, kn, proj]
    if sink is not None:
        in_specs.append(pl.BlockSpec((None, 1, HEAD_DIM), lambda hk, g, i: (hk * group + g, 0, 0)))
        args.append(sink)
    ospec = pl.BlockSpec((tq, HEAD_DIM), lambda hk, g, i: (i, hk * group + g))
    return pl.pallas_call(
        body, name=name, grid=(nkv, group, n // (bq * nsub)), in_specs=in_specs, out_specs=[ospec, ospec],
        out_shape=[jax.ShapeDtypeStruct((L, nh * HEAD_DIM), F32)] * 2,
        compiler_params=_params(("parallel", "parallel", "arbitrary")),
    )(*args)


def _band_bwd(qn, kn, proj, do, o, lse, dq_buf, dk_buf, dv_buf, *, dil, radius, nkv, group, q0, k0, v0, o0,
              sink=None, name):
    L = qn.shape[0]
    n, bq, width, nsub = _band_geometry(L, dil, radius)
    tq = nsub * bq * dil
    nh = nkv * group
    n_in = 6 + (1 if sink is not None else 0)

    def body(*refs):
        q_ref, k_ref, v_ref, do_ref, o_ref, lse_ref = refs[:6]
        s_ref = refs[6] if sink is not None else None
        outs = refs[n_in + 3:]
        dq_ref, dk_ref, dv_ref = outs[:3]
        ds_ref = outs[3] if sink is not None else None
        g, step = pl.program_id(1), pl.program_id(2)

        @pl.when((g == 0) & (step == 0))
        def _():
            dk_ref[...] = jnp.zeros_like(dk_ref)
            dv_ref[...] = jnp.zeros_like(dv_ref)

        if sink is not None:
            @pl.when(step == 0)
            def _():
                ds_ref[...] = jnp.zeros_like(ds_ref)

        for sb in range(nsub):
            block(sb, q_ref, k_ref, v_ref, do_ref, o_ref, lse_ref, s_ref, dq_ref, dk_ref, dv_ref, ds_ref)

    def block(sb, q_ref, k_ref, v_ref, do_ref, o_ref, lse_ref, s_ref, dq_ref, dk_ref, dv_ref, ds_ref):
        i = pl.program_id(2) * nsub + sb
        ws = jnp.clip(i * bq - radius, 0, n - width)
        valid = _band_mask(i, bq, width, radius, ws)

        def one(r, carry):
            qrows = _band_rows(dil, r, sb * bq, bq)
            krows = _band_rows(dil, r, ws, width)
            q = q_ref[qrows, :].astype(BF16)
            k = k_ref[krows, :].astype(BF16)
            v = v_ref[krows, :].astype(BF16)
            dov = do_ref[qrows, :]
            lse_v = lse_ref[qrows, :][:, :1]
            delta = jnp.sum(dov * o_ref[qrows, :], axis=-1, keepdims=True)
            dob = dov.astype(BF16)
            s = lax.dot_general(q, k, (((1,), (1,)), ((), ())), preferred_element_type=F32) * ATT_SCALE
            p = jnp.where(valid, jnp.exp(s - lse_v), 0.0)
            dp = lax.dot_general(dob, v, (((1,), (1,)), ((), ())), preferred_element_type=F32)
            dsb = (p * (dp - delta)).astype(BF16)
            dq_ref[qrows, :] = jnp.dot(dsb, k, preferred_element_type=F32) * ATT_SCALE
            dk_ref[krows, :] += lax.dot_general(dsb, q, (((0,), (0,)), ((), ())),
                                                preferred_element_type=F32) * ATT_SCALE
            dv_ref[krows, :] += lax.dot_general(p.astype(BF16), dob, (((0,), (0,)), ((), ())),
                                                preferred_element_type=F32)
            if sink is not None:
                ps = jnp.exp(s_ref[...][:, :1] - lse_v)
                ds_ref[...] += jnp.broadcast_to(jnp.sum(-ps * delta, axis=0, keepdims=True), (1, HEAD_DIM))
            return carry

        _band_loop(dil, one)

    hspec = pl.BlockSpec((tq, HEAD_DIM), lambda hk, g, i: (i, o0 + hk * group + g))
    qspec = pl.BlockSpec((tq, HEAD_DIM), lambda hk, g, i: (i, q0 + hk * group + g))
    kspec = pl.BlockSpec((L, HEAD_DIM), lambda hk, g, i: (0, k0 + hk))
    any_spec = pl.BlockSpec(memory_space=pl.ANY)
    in_specs = [qspec, kspec, pl.BlockSpec((L, HEAD_DIM), lambda hk, g, i: (0, v0 + hk)), hspec, hspec, hspec]
    args = [qn, kn, proj, do, o, lse]
    if sink is not None:
        in_specs.append(pl.BlockSpec((None, 1, HEAD_DIM), lambda hk, g, i: (hk * group + g, 0, 0)))
        args.append(sink)
    in_specs += [any_spec] * 3
    args += [dq_buf, dk_buf, dv_buf]
    out_specs = [qspec, kspec, kspec]
    out_shape = [jax.ShapeDtypeStruct(dq_buf.shape, F32), jax.ShapeDtypeStruct(dk_buf.shape, F32),
                 jax.ShapeDtypeStruct(dv_buf.shape, F32)]
    if sink is not None:
        out_specs.append(pl.BlockSpec((None, 1, HEAD_DIM), lambda hk, g, i: (hk * group + g, 0, 0)))
        out_shape.append(jax.ShapeDtypeStruct((nh, 1, HEAD_DIM), F32))
    return pl.pallas_call(
        body, name=name, grid=(nkv, group, n // (bq * nsub)), in_specs=in_specs, out_specs=out_specs,
        out_shape=out_shape,
        input_output_aliases={n_in: 0, n_in + 1: 1, n_in + 2: 2},
        compiler_params=_params(("parallel", "arbitrary", "arbitrary")),
    )(*args)


def _combine_b(os_, lses, *, name):
    L, W = os_[0].shape
    tl = _tile(L, (256, 128))

    def body(o0, o1, o2, l0, l1, l2, out_ref, lt_ref):
        a, b, c = l0[...], l1[...], l2[...]
        m = jnp.maximum(jnp.maximum(a, b), c)
        ea, eb, ec = jnp.exp(a - m), jnp.exp(b - m), jnp.exp(c - m)
        tot = ea + eb + ec
        out_ref[...] = (ea * o0[...] + eb * o1[...] + ec * o2[...]) / tot
        lt_ref[...] = m + jnp.log(tot)

    blk = pl.BlockSpec((tl, W), lambda t: (t, 0))
    return pl.pallas_call(
        body, name=name, grid=(L // tl,), in_specs=[blk] * 6, out_specs=[blk, blk],
        out_shape=[jax.ShapeDtypeStruct((L, W), F32)] * 2, compiler_params=_params(("parallel",)),
    )(*os_, *lses)


C_QROWS = 4
C_KROWS = C_QROWS + C_WIN_ROWS
C_QUERIES, C_KEYS = C_QROWS * GRID_W, C_KROWS * GRID_W
_C_KIND_OFFSETS = (C_WIN_ROWS - 1, C_WIN_ROWS - 1 - C_WIN_ROWS // 2, C_WIN_ROWS - 1 - (C_KROWS - C_QROWS))


def _c_geometry(L):
    rows = L // GRID_W
    assert rows >= C_KROWS and rows % C_QROWS == 0
    return rows


def _c_bias_tiles(bias_t):
    cq = np.arange(GRID_W)[:, None]
    ck = np.arange(GRID_W)[None, :]
    start = np.clip(cq - C_WIN_COLS // 2, 0, GRID_W - C_WIN_COLS)
    masked = jnp.where(jnp.asarray((ck >= start) & (ck < start + C_WIN_COLS)), bias_t, NEG)
    blank = jnp.full((C_HEADS, GRID_W, GRID_W), NEG, F32)
    kinds = []
    for kind in range(3):
        off = _C_KIND_OFFSETS[kind]
        row_blocks = []
        for a in range(C_QROWS):
            lo = (0, a, C_KROWS - C_WIN_ROWS)[kind]
            row_blocks.append(jnp.concatenate(
                [masked[:, b - a + off] if lo <= b < lo + C_WIN_ROWS else blank for b in range(C_KROWS)], axis=-1))
        kinds.append(jnp.concatenate(row_blocks, axis=-2))
    return jnp.stack(kinds, axis=1)


def _c_block(g, rows):
    r0 = g * C_QROWS
    k0 = jnp.clip(r0 - C_WIN_ROWS // 2, 0, rows - C_KROWS)
    kind = jnp.where(g == 0, 0, jnp.where(g == rows // C_QROWS - 1, 2, 1))
    return k0, kind, k0 - r0 + (C_WIN_ROWS - 1)


def _c_fwd(qn, kn, proj, tiles, *, name):
    L = qn.shape[0]
    rows = _c_geometry(L)

    nsub = _tile(rows // C_QROWS, (2,))

    def body(q_ref, k_ref, v_ref, t_ref, o_ref, lse_ref):
        for sb in range(nsub):
            k0, kind, _ = _c_block(pl.program_id(1) * nsub + sb, rows)
            krows = pl.ds(pl.multiple_of(k0 * GRID_W, GRID_W), C_KEYS)
            qrows = pl.ds(sb * C_QUERIES, C_QUERIES)
            q = q_ref[qrows, :].astype(BF16)
            k = k_ref[krows, :].astype(BF16)
            v = v_ref[krows, :].astype(BF16)
            s = lax.dot_general(q, k, (((1,), (1,)), ((), ())), preferred_element_type=F32) * ATT_SCALE + t_ref[kind]
            m = jnp.max(s, axis=-1, keepdims=True)
            p = jnp.exp(s - m)
            denom = jnp.sum(p, axis=-1, keepdims=True)
            o_ref[qrows, :] = jnp.dot((p / denom).astype(BF16), v, preferred_element_type=F32)
            lse_ref[qrows, :] = jnp.broadcast_to(m + jnp.log(denom), (C_QUERIES, HEAD_DIM))

    ospec = pl.BlockSpec((nsub * C_QUERIES, HEAD_DIM), lambda h, g: (g, h))
    return pl.pallas_call(
        body, name=name, grid=(C_HEADS, rows // (C_QROWS * nsub)),
        in_specs=[pl.BlockSpec((nsub * C_QUERIES, HEAD_DIM), lambda h, g: (g, 20 + h)),
                  pl.BlockSpec((L, HEAD_DIM), lambda h, g: (0, 14 + h)),
                  pl.BlockSpec((L, HEAD_DIM), lambda h, g: (0, PC_VC + h)),
                  pl.BlockSpec((None, 3, C_QUERIES, C_KEYS), lambda h, g: (h, 0, 0, 0))],
        out_specs=[ospec, ospec],
        out_shape=[jax.ShapeDtypeStruct((L, C_HEADS * HEAD_DIM), F32)] * 2,
        compiler_params=_params(("parallel", "arbitrary")),
    )(qn, kn, proj, tiles)


def _c_bwd(qn, kn, proj, tiles, do, o, lse, dq_buf, dk_buf, dv_buf, *, name):
    L = qn.shape[0]
    rows = _c_geometry(L)

    nsub = _tile(rows // C_QROWS, (2,))

    def body(q_ref, k_ref, v_ref, t_ref, do_ref, o_ref, lse_ref, _a, _b, _c, dq_ref, dk_ref, dv_ref, dt_ref):
        @pl.when(pl.program_id(1) == 0)
        def _():
            dk_ref[...] = jnp.zeros_like(dk_ref)
            dv_ref[...] = jnp.zeros_like(dv_ref)
            dt_ref[...] = jnp.zeros_like(dt_ref)

        for sb in range(nsub):
            k0, kind, off = _c_block(pl.program_id(1) * nsub + sb, rows)
            krows = pl.ds(pl.multiple_of(k0 * GRID_W, GRID_W), C_KEYS)
            qrows = pl.ds(sb * C_QUERIES, C_QUERIES)
            q = q_ref[qrows, :].astype(BF16)
            k = k_ref[krows, :].astype(BF16)
            v = v_ref[krows, :].astype(BF16)
            dov = do_ref[qrows, :]
            dob = dov.astype(BF16)
            delta = jnp.sum(dov * o_ref[qrows, :], axis=-1, keepdims=True)
            s = lax.dot_general(q, k, (((1,), (1,)), ((), ())), preferred_element_type=F32) * ATT_SCALE + t_ref[kind]
            p = jnp.exp(s - lse_ref[qrows, :][:, :1])
            dp = lax.dot_general(dob, v, (((1,), (1,)), ((), ())), preferred_element_type=F32)
            ds = p * (dp - delta)
            for a in range(C_QROWS):
                for b in range(C_KROWS):
                    rel = jnp.clip(b - a + off, 0, C_NREL - 1)
                    dt_ref[rel] += ds[a * GRID_W:(a + 1) * GRID_W, b * GRID_W:(b + 1) * GRID_W]
            dsb = ds.astype(BF16)
            dq_ref[qrows, :] = jnp.dot(dsb, k, preferred_element_type=F32) * ATT_SCALE
            dk_ref[krows, :] += lax.dot_general(dsb, q, (((0,), (0,)), ((), ())),
                                                preferred_element_type=F32) * ATT_SCALE
            dv_ref[krows, :] += lax.dot_general(p.astype(BF16), dob, (((0,), (0,)), ((), ())),
                                                preferred_element_type=F32)

    hspec = pl.BlockSpec((nsub * C_QUERIES, HEAD_DIM), lambda h, g: (g, h))
    qspec = pl.BlockSpec((nsub * C_QUERIES, HEAD_DIM), lambda h, g: (g, 20 + h))
    kspec = pl.BlockSpec((L, HEAD_DIM), lambda h, g: (0, 14 + h))
    any_spec = pl.BlockSpec(memory_space=pl.ANY)
    return pl.pallas_call(
        body, name=name, grid=(C_HEADS, rows // (C_QROWS * nsub)),
        in_specs=[qspec, kspec, pl.BlockSpec((L, HEAD_DIM), lambda h, g: (0, PC_VC + h)),
                  pl.BlockSpec((None, 3, C_QUERIES, C_KEYS), lambda h, g: (h, 0, 0, 0)),
                  hspec, hspec, hspec, any_spec, any_spec, any_spec],
        out_specs=[qspec, kspec, kspec,
                   pl.BlockSpec((None, C_NREL, GRID_W, GRID_W), lambda h, r: (h, 0, 0, 0))],
        out_shape=[jax.ShapeDtypeStruct(dq_buf.shape, F32), jax.ShapeDtypeStruct(dk_buf.shape, F32),
                   jax.ShapeDtypeStruct(dv_buf.shape, F32),
                   jax.ShapeDtypeStruct((C_HEADS, C_NREL, GRID_W, GRID_W), F32)],
        input_output_aliases={7: 0, 8: 1, 9: 2},
        compiler_params=_params(("parallel", "arbitrary")),
    )(qn, kn, proj, tiles, do, o, lse, dq_buf, dk_buf, dv_buf)


def _c_expand_matrix():
    cq = np.arange(GRID_W)[:, None]
    ck = np.arange(GRID_W)[None, :]
    d = (ck - cq + (C_WIN_COLS - 1)).reshape(-1)
    e = np.zeros((GRID_W * GRID_W, HEAD_DIM), np.float32)
    okd = (d >= 0) & (d < C_NCOL)
    e[np.arange(GRID_W * GRID_W)[okd], d[okd]] = 1.0
    return e


def _peer(p):
    return (p // 4, (p // 2) % 2, p % 2)


def _my_index():
    return 4 * lax.axis_index("x") + 2 * lax.axis_index("y") + lax.axis_index("c")


HBM_SPEC = pl.BlockSpec(memory_space=pltpu.HBM)
SEM_SPEC = pl.BlockSpec(memory_space=pltpu.SEMAPHORE)
ANY_SPEC = pl.BlockSpec(memory_space=pl.ANY)
DATAFLOW = pltpu.SideEffectType.DATAFLOW_SIDE_EFFECTING


_EXCHANGE_TRANSFERS = {"scatter": N_DEV - 1, "gather1": 4, "gather2": 3}


def _exchange_views(mode, kinds, arrays):
    nw = len(kinds)
    gather = mode != "scatter"
    if gather:
        rows = [a.shape[0] // N_DEV for a in arrays[:nw]]
    else:
        rows = [a.shape[1] // N_DEV for a in arrays[:nw]]

    def gather_slot(ref, w, who):
        return ref.at[who] if kinds[w] == "col" else ref.at[pl.ds(who * rows[w], rows[w]), :]

    x, y, c = lax.axis_index("x"), lax.axis_index("y"), lax.axis_index("c")
    me = 4 * x + 2 * y + c
    chips = [(1 - x, y), (x, 1 - y), (1 - x, 1 - y)]

    def index(px, py, pc):
        return 4 * px + 2 * py + pc

    if mode == "scatter":
        plan = [(_peer((me + off) % N_DEV), (me + off) % N_DEV, (me + N_DEV - off) % N_DEV)
                for off in range(1, N_DEV)]
    elif mode == "gather1":
        plan = [((x, y, 1 - c), me, index(x, y, 1 - c))] + [((px, py, c), me, index(px, py, c)) for px, py in chips]
    else:
        plan = [((x, y, 1 - c), index(px, py, c), index(px, py, 1 - c)) for px, py in chips]

    def src(ref, w, j):
        sent = plan[j][1]
        if gather:
            return gather_slot(ref, w, sent)
        return ref.at[sent] if kinds[w] == "col" else ref.at[0, pl.ds(sent * rows[w], rows[w]), :]

    def dst(ref, w, j):
        return gather_slot(ref, w, plan[j][1]) if gather else ref.at[me]

    def arrival(ref, w, j):
        return gather_slot(ref, w, plan[j][2]) if gather else ref.at[plan[j][2]]

    return [p[0] for p in plan], src, dst, arrival


def _place_cast(w, layer, kind, *, name):
    _, R, C = w.shape
    tr = _tile(R, (256, 128, 64, 32, 16))

    def body(w_ref, o_ref):
        o_ref[...] = w_ref[...].astype(BF16)

    if kind == "col":
        out_shape = jax.ShapeDtypeStruct((N_DEV, R, C), BF16)
        out_spec = pl.BlockSpec((None, tr, C), lambda t: (_my_index(), t, 0))
    else:
        out_shape = jax.ShapeDtypeStruct((N_DEV * R, C), BF16)
        out_spec = pl.BlockSpec((tr, C), lambda t: (_my_index() * (R // tr) + t, 0))
    return pl.pallas_call(
        body, name=name, grid=(R // tr,), in_specs=[pl.BlockSpec((None, tr, C), lambda t: (layer, t, 0))],
        out_specs=out_spec, out_shape=out_shape, compiler_params=_params(("parallel",)),
    )(w)


def _exchange_start(mode, srcs, lands, kinds, after, *, name):
    nw = len(lands)
    ns = len(srcs)
    arrays = list(srcs) + list(lands)
    na = len(arrays)
    nx = _EXCHANGE_TRANSFERS[mode]

    def body(*refs):
        l_refs = refs[ns:ns + nw]
        s_refs = refs[:ns] if ns else l_refs
        send_sems, recv_sems = refs[ns + nw + 1], refs[ns + nw + 2]
        token = refs[-1]
        peers, src, dst, _ = _exchange_views(mode, kinds, arrays)
        for j in range(nx):
            for w in range(nw):
                pltpu.make_async_remote_copy(src(s_refs[w], w, j), dst(l_refs[w], w, j),
                                             send_sems.at[w * nx + j], recv_sems.at[w * nx + j],
                                             device_id=peers[j], device_id_type=MESH).start()
        token[...] = jnp.zeros_like(token)

    outs = pl.pallas_call(
        body, name=name,
        out_shape=(pltpu.SemaphoreType.DMA((nw * nx,)), pltpu.SemaphoreType.DMA((nw * nx,)),
                   *[pltpu.HBM(a.shape, a.dtype) for a in arrays], jax.ShapeDtypeStruct((8, 128), F32)),
        in_specs=[HBM_SPEC] * na + [ANY_SPEC],
        out_specs=(SEM_SPEC, SEM_SPEC, *([HBM_SPEC] * na), pl.BlockSpec(memory_space=pltpu.VMEM)),
        input_output_aliases={k: 2 + k for k in range(na)},
        compiler_params=pltpu.CompilerParams(has_side_effects=DATAFLOW),
    )(*[pltpu.with_memory_space_constraint(a, pltpu.HBM) for a in arrays], after)
    return outs[0], outs[1], outs[2:2 + ns], outs[2 + ns:2 + na], outs[-1]


def _exchange_wait(mode, started, kinds, after, *, name):
    send_sems, recv_sems, srcs, lands, _ = started
    nw = len(lands)
    ns = len(srcs)
    arrays = list(srcs) + list(lands)
    na = len(arrays)
    nx = _EXCHANGE_TRANSFERS[mode]

    def body(*refs):
        l_refs = refs[ns:na]
        s_refs = refs[:ns] if ns else l_refs
        send_ref, recv_ref = refs[na], refs[na + 1]
        peers, src, _, arrival = _exchange_views(mode, kinds, arrays)
        for j in range(nx):
            for w in range(nw):
                cp = pltpu.make_async_remote_copy(src(s_refs[w], w, j), arrival(l_refs[w], w, j),
                                                  send_ref.at[w * nx + j], recv_ref.at[w * nx + j],
                                                  device_id=peers[j], device_id_type=MESH)
                cp.wait_send()
                cp.wait_recv()

    outs = pl.pallas_call(
        body, name=name, out_shape=[pltpu.HBM(a.shape, a.dtype) for a in arrays],
        in_specs=[HBM_SPEC] * na + [SEM_SPEC, SEM_SPEC, ANY_SPEC], out_specs=[HBM_SPEC] * na,
        input_output_aliases={k: k for k in range(na)},
        compiler_params=pltpu.CompilerParams(has_side_effects=DATAFLOW),
    )(*arrays, send_sems, recv_sems, after)
    return outs[:ns], outs[ns:]


def _all_reduce_small(x):
    R = x.shape[0]

    def body(x_ref, o_ref, gath, send_sems, recv_sems):
        me = _my_index()
        gath[me] = x_ref[...]
        sends = []
        for off in range(1, N_DEV):
            to = (me + off) % N_DEV
            cp = pltpu.make_async_remote_copy(x_ref, gath.at[me], send_sems.at[off], recv_sems.at[off],
                                              device_id=_peer(to), device_id_type=MESH)
            cp.start()
            sends.append(cp)
        for off in range(1, N_DEV):
            frm = (me + N_DEV - off) % N_DEV
            pltpu.make_async_remote_copy(x_ref, gath.at[frm], send_sems.at[off], recv_sems.at[off],
                                         device_id=_peer(frm), device_id_type=MESH).wait_recv()
        for cp in sends:
            cp.wait_send()
        acc = gath[0]
        for s in range(1, N_DEV):
            acc = acc + gath[s]
        o_ref[...] = acc

    vm = pl.BlockSpec(memory_space=pltpu.VMEM)
    return pl.pallas_call(
        body, name="all_reduce_small", in_specs=[vm], out_specs=vm, out_shape=jax.ShapeDtypeStruct((R, 128), F32),
        scratch_shapes=[pltpu.VMEM((N_DEV, R, 128), F32), pltpu.SemaphoreType.DMA((N_DEV,)),
                        pltpu.SemaphoreType.DMA((N_DEV,))],
        compiler_params=pltpu.CompilerParams(has_side_effects=True),
    )(x)


def _adamw_math(w, g, m, v):
    m = ADAM_B1 * m + (1.0 - ADAM_B1) * g
    v = ADAM_B2 * v + (1.0 - ADAM_B2) * (g * g)
    m_hat = m / (1.0 - ADAM_B1 ** ADAM_STEP)
    v_hat = v / (1.0 - ADAM_B2 ** ADAM_STEP)
    delta = -ADAM_LR * (m_hat / (jnp.sqrt(v_hat) + ADAM_EPS) + ADAM_WD * w)
    return delta, m, v


def _adamw_layer(recv, own, kind, w, m, v, outs, layer, dep, *, name):
    nl, R, C = w.shape
    tr = _tile(R, (128, 64, 32, 16))

    def body(r_ref, o_ref, w_ref, m_ref, v_ref, _0, _1, _2, _3, _dep, g_out, d_out, m_out, v_out, token):
        token[...] = jnp.zeros_like(token)
        me = _my_index()
        mine = o_ref[...].astype(F32)
        g = jnp.where(me == 0, mine, r_ref[0].astype(F32))
        for s in range(1, N_DEV):
            g = g + jnp.where(me == s, mine, r_ref[s].astype(F32))
        delta, mn, vn = _adamw_math(w_ref[...], g, m_ref[...], v_ref[...])
        g_out[...] = g
        d_out[...] = delta
        m_out[...] = mn
        v_out[...] = vn

    if kind == "col":
        own_spec = pl.BlockSpec((None, tr, C), lambda t: (_my_index(), t, 0))
    else:
        own_spec = pl.BlockSpec((None, tr, C), lambda t: (0, _my_index() * (R // tr) + t, 0))
    wspec = pl.BlockSpec((None, tr, C), lambda t: (layer, t, 0))
    res = pl.pallas_call(
        body, name=name, grid=(R // tr,),
        in_specs=[pl.BlockSpec((N_DEV, tr, C), lambda t: (0, t, 0)), own_spec] + [wspec] * 3 + [ANY_SPEC] * 5,
        out_specs=[wspec] * 4 + [pl.BlockSpec((8, 128), lambda t: (0, 0))],
        out_shape=[jax.ShapeDtypeStruct((nl, R, C), F32)] * 4 + [jax.ShapeDtypeStruct((8, 128), F32)],
        input_output_aliases={5: 0, 6: 1, 7: 2, 8: 3},
        compiler_params=_params(("arbitrary",)),
    )(recv, own, w, m, v, *outs, dep)
    return res[:4], res[4]


def _adamw_small(g, w, m, v):
    def body(g_ref, w_ref, m_ref, v_ref, d_out, m_out, v_out):
        delta, mn, vn = _adamw_math(w_ref[...], g_ref[...], m_ref[...], v_ref[...])
        d_out[...] = delta
        m_out[...] = mn
        v_out[...] = vn

    return pl.pallas_call(body, name="adamw_small", out_shape=[jax.ShapeDtypeStruct(g.shape, F32)] * 3)(g, w, m, v)


def _pack(arrays, rows):
    flat = jnp.concatenate([a.reshape(-1) for a in arrays])
    return jnp.pad(flat, (0, rows * 128 - flat.shape[0])).reshape(rows, 128)


def _unpack(packed, shapes):
    flat = packed.reshape(-1)
    out, pos = [], 0
    for s in shapes:
        size = int(np.prod(s))
        out.append(flat[pos:pos + size].reshape(s))
        pos += size
    return out


def kernel(x, norm1_g, w_in, qk_norm_g, sink_a, rpb_c, w_br_a, w_br_b, w_br_c, w_o, norm2_g, w_gate_up, w_down, loss_target, m_norm1_g, m_w_in, m_qk_norm_g, m_sink_a, m_rpb_c, m_w_br_a, m_w_br_b, m_w_br_c, m_w_o, m_norm2_g, m_w_gate_up, m_w_down, v_norm1_g, v_w_in, v_qk_norm_g, v_sink_a, v_rpb_c, v_w_br_a, v_w_br_b, v_w_br_c, v_w_o, v_norm2_g, v_w_gate_up, v_w_down):
    nl = w_in.shape[0]
    L, D = x.shape[1], x.shape[2]
    x0 = x.reshape(L, D)
    tgt = loss_target.reshape(L, D)

    big = [w_in, w_br_a, w_br_b, w_br_c, w_o, w_gate_up, w_down]
    kinds = ["col", "col", "col", "col", "row", "col", "row"]

    big_names = ["w_in", "w_br_a", "w_br_b", "w_br_c", "w_o", "w_gate_up", "w_down"]
    ALL = list(range(len(big)))
    REST = ALL[1:]

    def gather_place(i):
        return [_place_cast(w, i, k, name="gather_place_" + n) for w, k, n in zip(big, kinds, big_names)]

    def gather_start(mode, lands, sub, after, tag):
        return _exchange_start(mode, [], lands, [kinds[j] for j in sub], after, name=mode + "_start" + tag)

    def gather_wait(mode, started, sub, after, tag):
        return _exchange_wait(mode, started, [kinds[j] for j in sub], after, name=mode + "_wait" + tag)[1]

    def matmul_views(lands, sub):
        return [g.reshape((N_DEV, 1) + g.shape[1:]) if kinds[j] == "col" else g.reshape((1, 1) + g.shape)
                for g, j in zip(lands, sub)]

    half = HEAD_DIM // 2
    inv_freq = ROPE_THETA ** (-jnp.arange(half, dtype=F32) * 2.0 / HEAD_DIM)
    ang = jnp.arange(L, dtype=F32)[:, None] * inv_freq[None, :]
    cos = jnp.concatenate([jnp.cos(ang), jnp.cos(ang)], axis=-1)
    sin = jnp.concatenate([-jnp.sin(ang), jnp.sin(ang)], axis=-1)
    expand = jnp.asarray(_c_expand_matrix(), BF16)
    expand_t = jnp.asarray(_c_expand_matrix().T, BF16)

    def gain_tables(i):
        g = qk_norm_g[i]
        gq = jnp.concatenate([jnp.tile(g[0][None], (8, 1)), jnp.tile(g[2][None], (12, 1)), jnp.tile(g[4][None], (8, 1))])
        gk = jnp.concatenate([jnp.tile(g[1][None], (2, 1)), jnp.tile(g[3][None], (12, 1)), jnp.tile(g[5][None], (8, 1))])
        return (gq.reshape(NQ_CHUNKS // Q_CG, 1, Q_CG * HEAD_DIM), gk.reshape(NK_CHUNKS // K_CG, 1, K_CG * HEAD_DIM))

    def bias_table(i):
        rp = jnp.pad(rpb_c[i].reshape(C_HEADS * C_NREL, C_NCOL), ((0, 0), (0, HEAD_DIM - C_NCOL)))
        t = _exact_mm(rp, expand_t, name="c_bias_expand")
        return _c_bias_tiles(t.reshape(C_HEADS, C_NREL, GRID_W, GRID_W))

    def sink_table(i):
        return jnp.broadcast_to(sink_a[i][:, None, None], (A_Q_HEADS, 1, HEAD_DIM))

    saved = []
    gws = [None] * nl
    xi = x0
    lands0 = gather_place(0)
    lvl1 = gather_start("gather1", lands0[:1], [0], x0, "_first")
    lvl2 = gather_start("gather2", gather_wait("gather1", lvl1, [0], x0, "_first"), [0], x0, "_first")
    gws[0] = matmul_views(gather_wait("gather2", lvl2, [0], x0, "_first"), [0])
    rest1 = gather_start("gather1", lands0[1:], REST, gws[0][0], "_rest")
    dep = rest1[4]
    for i in range(nl):
        qk_dep = None
        if i >= 1 and i + 1 < nl:
            nxt1 = gather_start("gather1", gather_place(i + 1), ALL, dep, "")
            dep = nxt1[4]
        gw_in = gws[i][0]
        gq, gk = gain_tables(i)
        bias_t = bias_table(i)
        sink = sink_table(i)
        h1 = _rms_fwd(xi, norm1_g[i][None], dep, name="rms1_fwd")
        proj = _mm_nn(h1, gw_in, 0, out_dtype=F32, name="proj_fwd")
        if i == 0:
            rest2 = gather_start("gather2", gather_wait("gather1", rest1, REST, proj, "_rest"), REST, proj, "_rest")
            qk_dep = rest2[4]
            if nl > 1:
                nxt1 = gather_start("gather1", gather_place(1), ALL, rest2[4], "")
                qk_dep = nxt1[4]
        qn = _qk_fwd(proj, gq, cos, sin, Q_PIECES, NQ_CHUNKS, Q_ROPE_UPTO, Q_CG, name="qnorm_fwd", dep=qk_dep)
        kn = _qk_fwd(proj, gk, cos, sin, K_PIECES, NK_CHUNKS, K_ROPE_UPTO, K_CG, name="knorm_fwd")
        oa, lse_a = _band_fwd(qn, kn, proj, dil=1, radius=A_RADIUS, nkv=A_KV_HEADS, group=A_GROUP,
                              q0=0, k0=0, v0=PC_VA, sink=sink, name="attn_a_fwd")
        obs, lbs = [], []
        for g, (window, dil) in enumerate(B_PATTERNS):
            o_g, l_g = _band_fwd(qn, kn, proj, dil=dil, radius=window // (2 * dil), nkv=B_HG, group=1,
                                 q0=8 + g * B_HG, k0=2 + g * B_HG, v0=PC_VB + g * B_HG, name=f"attn_b{g}_fwd")
            obs.append(o_g)
            lbs.append(l_g)
        ob, lse_b = _combine_b(obs, lbs, name="attn_b_combine")
        oc, lse_c = _c_fwd(qn, kn, proj, bias_t, name="attn_c_fwd")
        if i == 0:
            gws[0] = gws[0] + matmul_views(gather_wait("gather2", rest2, REST, oc, "_rest"), REST)
        gws[i][1:4] = [_blocks_to_wide(g, name="br_wide_" + n) for g, n in zip(gws[i][1:4], "abc")]
        _, gw_a, gw_b, gw_c, gw_o, gw_gu, gw_d = gws[i]
        ta = _mm_nn(oa, gw_a, 0, out_dtype=BF16, name="br_a_fwd")
        tb = _mm_nn(ob, gw_b, 0, out_dtype=BF16, name="br_b_fwd")
        tc = _mm_nn(oc, gw_c, 0, out_dtype=BF16, name="br_c_fwd")
        merged = _gate_fwd(proj, ta, tb, tc, name="gate_fwd")
        x1 = _mm_nn(merged, gw_o, 0, out_dtype=F32, name="wo_fwd", res=xi)
        dep = x1
        if i + 1 < nl:
            nxt2 = gather_start("gather2", gather_wait("gather1", nxt1, ALL, x1, ""), ALL, x1, "")
            dep = nxt2[4]
        h2 = _rms_fwd(x1, norm2_g[i][None], dep, name="rms2_fwd")
        gu = _mm_nn(h2, gw_gu, 0, out_dtype=BF16, name="gate_up_fwd")
        act = _swiglu_fwd(gu, name="swiglu_fwd")
        x2 = _mm_nn(act, gw_d, 0, out_dtype=F32, name="down_fwd", res=x1)
        saved.append(dict(x=xi, h1=h1, proj=proj, qn=qn, kn=kn, oa=oa, lse_a=lse_a, ob=ob, lse_b=lse_b, oc=oc,
                          lse_c=lse_c, ta=ta, tb=tb, tc=tc, merged=merged, x1=x1, h2=h2, gu=gu, act=act,
                          gq=gq, gk=gk, bias_t=bias_t, sink=sink))
        xi = x2
        dep = x2
        if i + 1 < nl:
            gws[i + 1] = matmul_views(gather_wait("gather2", nxt2, ALL, x2, ""), ALL)

    dx, dxb, loss_row = _loss(xi, tgt, name="loss")

    def scatter_start(grads, sub, after, tag):
        lands = []
        for g, j in zip(grads, sub):
            shape = g.shape if kinds[j] == "col" else (N_DEV, g.shape[1] // N_DEV, g.shape[2])
            lands.append(lax.empty(shape, BF16))
        return _exchange_start("scatter", grads, lands, [kinds[j] for j in sub], after, name="scatter_start" + tag)

    def scatter_wait(pair, after):
        own_a, recv_a = _exchange_wait("scatter", pair[0], [kinds[0]], after, name="scatter_wait_in")
        own_b, recv_b = _exchange_wait("scatter", pair[1], [kinds[j] for j in REST], after, name="scatter_wait_rest")
        return list(recv_a) + list(recv_b), list(own_a) + list(own_b)

    small_grads = [None] * nl
    recv = [None] * nl
    own = [None] * nl
    pending = None
    for i in reversed(range(nl)):
        s = saved[i]
        gw_in, gw_a, gw_b, gw_c, gw_o, gw_gu, gw_d = gws[i]
        dact = _mm_nt(dxb, gw_d, 0, out_dtype=BF16, name="down_bwd_x", dep=None if pending is None else pending[0][4])
        g_down = _mm_tn(s["act"], dxb, 1, name="down_bwd_w")
        dgu = _swiglu_bwd(s["gu"], dact, name="swiglu_bwd")
        g_gu = _mm_tn(s["h2"], dgu, N_DEV, name="gate_up_bwd_w")
        dh2 = _mm_nt(dgu, gw_gu, 0, out_dtype=F32, name="gate_up_bwd_x")
        dx1, dx1b, dg2 = _rms_bwd(s["x1"], norm2_g[i][None], dh2, dx, name="rms2_bwd")
        dmerged = _mm_nt(dx1b, gw_o, 0, out_dtype=F32, name="wo_bwd_x")
        g_o = _mm_tn(s["merged"], dx1b, 1, name="wo_bwd_w")
        dta, dtb, dtc, dproj = _gate_bwd(s["proj"], s["ta"], s["tb"], s["tc"], dmerged, name="gate_bwd")
        g_a = _wide_to_blocks(_mm_tn(s["oa"], dta, 1, name="br_a_bwd_w"), N_DEV, name="br_blocks_a")
        g_b = _wide_to_blocks(_mm_tn(s["ob"], dtb, 1, name="br_b_bwd_w"), N_DEV, name="br_blocks_b")
        g_c = _wide_to_blocks(_mm_tn(s["oc"], dtc, 1, name="br_c_bwd_w"), N_DEV, name="br_blocks_c")
        rest = scatter_start([g_a, g_b, g_c, g_o, g_gu, g_down], REST, g_c, "_rest")
        doa = _mm_nt(dta, gw_a, 0, out_dtype=F32, name="br_a_bwd_x", dep=rest[4])
        dob = _mm_nt(dtb, gw_b, 0, out_dtype=F32, name="br_b_bwd_x")
        doc = _mm_nt(dtc, gw_c, 0, out_dtype=F32, name="br_c_bwd_x")
        dq_buf = lax.empty((L, NQ_CHUNKS * HEAD_DIM), F32)
        dk_buf = lax.empty((L, NK_CHUNKS * HEAD_DIM), F32)
        dv_buf = lax.empty((L, NK_CHUNKS * HEAD_DIM), F32)
        dq_buf, dk_buf, dv_buf, dsink = _band_bwd(
            s["qn"], s["kn"], s["proj"], doa, s["oa"], s["lse_a"], dq_buf, dk_buf, dv_buf, dil=1, radius=A_RADIUS,
            nkv=A_KV_HEADS, group=A_GROUP, q0=0, k0=0, v0=PC_VA, o0=0, sink=s["sink"], name="attn_a_bwd")
        for g, (window, dil) in enumerate(B_PATTERNS):
            dq_buf, dk_buf, dv_buf = _band_bwd(
                s["qn"], s["kn"], s["proj"], dob, s["ob"], s["lse_b"], dq_buf, dk_buf, dv_buf, dil=dil,
                radius=window // (2 * dil), nkv=B_HG, group=1, q0=8 + g * B_HG, k0=2 + g * B_HG,
                v0=PC_VB + g * B_HG, o0=0, name=f"attn_b{g}_bwd")
        dq_buf, dk_buf, dv_buf, dbias_t = _c_bwd(s["qn"], s["kn"], s["proj"], s["bias_t"], doc, s["oc"], s["lse_c"],
                                                 dq_buf, dk_buf, dv_buf, name="attn_c_bwd")
        dproj, dgq = _qk_bwd(dq_buf, s["proj"], s["gq"], cos, sin, dproj, Q_PIECES, NQ_CHUNKS, Q_ROPE_UPTO, Q_CG,
                             name="qnorm_bwd")
        dproj, dgk = _qk_bwd(dk_buf, s["proj"], s["gk"], cos, sin, dproj, K_PIECES, NK_CHUNKS, K_ROPE_UPTO, K_CG,
                             name="knorm_bwd")
        dproj = _v_bwd(dv_buf, dproj, name="v_bwd")
        g_in = _mm_tn(s["h1"], dproj, N_DEV, name="proj_bwd_w")
        dh1 = _mm_nt(dproj, gw_in, 0, out_dtype=F32, name="proj_bwd_x")
        dx, dxb, dg1 = _rms_bwd(s["x"], norm1_g[i][None], dh1, dx1, name="rms1_bwd")
        if pending is not None:
            recv[i + 1], own[i + 1] = scatter_wait(pending, dx)
        pending = (scatter_start([g_in], [0], dx, "_in"), rest)

        drpb = _exact_mm(dbias_t.reshape(C_HEADS * C_NREL, GRID_W * GRID_W), expand, name="c_bias_reduce")
        dgq, dgk = dgq.reshape(NQ_CHUNKS, HEAD_DIM), dgk.reshape(NK_CHUNKS, HEAD_DIM)
        dqk_g = jnp.stack([dgq[0:8].sum(0), dgk[0:2].sum(0), dgq[8:20].sum(0), dgk[2:14].sum(0),
                           dgq[20:28].sum(0), dgk[14:22].sum(0)])
        small_grads[i] = (dg1.reshape(D), dqk_g, dsink[:, 0, 0],
                          drpb[:, :C_NCOL].reshape(C_HEADS, C_NREL, C_NCOL), dg2.reshape(D))

    small_names = [norm1_g, qk_norm_g, sink_a, rpb_c, norm2_g]
    small_m = [m_norm1_g, m_qk_norm_g, m_sink_a, m_rpb_c, m_norm2_g]
    small_v = [v_norm1_g, v_qk_norm_g, v_sink_a, v_rpb_c, v_norm2_g]
    shapes = [a.shape for a in small_names]
    total = sum(int(np.prod(sh)) for sh in shapes) + 128
    rows = -(-total // 1024) * 8
    stacked = [jnp.stack([small_grads[i][j] for i in range(nl)]) for j in range(5)]
    packed = _pack([loss_row.reshape(-1)] + stacked, rows)
    summed = _all_reduce_small(packed)
    loss = summed[0, 0]
    zero_row = jnp.zeros((128,), F32)
    d_s, m_s, v_s = _adamw_small(summed, _pack([zero_row] + small_names, rows), _pack([zero_row] + small_m, rows),
                                 _pack([zero_row] + small_v, rows))
    shapes1 = [(128,)] + shapes
    g_small = _unpack(summed, shapes1)[1:]
    d_small = _unpack(d_s, shapes1)[1:]
    m_small = _unpack(m_s, shapes1)[1:]
    v_small = _unpack(v_s, shapes1)[1:]

    big_m = [m_w_in, m_w_br_a, m_w_br_b, m_w_br_c, m_w_o, m_w_gate_up, m_w_down]
    big_v = [v_w_in, v_w_br_a, v_w_br_b, v_w_br_c, v_w_o, v_w_gate_up, v_w_down]
    big_out = [[lax.empty(w.shape, F32) for _ in range(4)] for w in big]
    token = pending[0][4]
    for i in list(range(nl - 1, 0, -1)) + [0]:
        if i == 0:
            recv[0], own[0] = scatter_wait(pending, token)
        for j in range(len(big)):
            big_out[j], token = _adamw_layer(recv[i][j], own[i][j], kinds[j], big[j], big_m[j], big_v[j],
                                             big_out[j], i, token, name="adamw_" + big_names[j])

    order = ["norm1_g", "w_in", "qk_norm_g", "sink_a", "rpb_c", "w_br_a", "w_br_b", "w_br_c", "w_o", "norm2_g",
             "w_gate_up", "w_down"]
    small_idx = {"norm1_g": 0, "qk_norm_g": 1, "sink_a": 2, "rpb_c": 3, "norm2_g": 4}
    big_idx = {n: j for j, n in enumerate(big_names)}

    def pick(kind):
        out = []
        for n in order:
            if n in small_idx:
                out.append([g_small, d_small, m_small, v_small][kind][small_idx[n]])
            else:
                out.append(big_out[big_idx[n]][kind])
        return out

    return (loss, dx.reshape(1, L, D), *pick(0), *pick(1), *pick(2), *pick(3))
```

```python
import functools
import math

import numpy as np
import jax
import jax.numpy as jnp
from jax import lax
from jax.experimental import pallas as pl
from jax.experimental.pallas import tpu as pltpu

F32 = jnp.float32
BF16 = jnp.bfloat16
MESH = pl.DeviceIdType.MESH
N_DEV = 8

HEAD_DIM = 128
NORM_EPS = 1e-6
ROPE_THETA = 10000.0
ATT_SCALE = HEAD_DIM ** -0.5
NEG = -1e30

A_Q_HEADS, A_KV_HEADS, A_RADIUS = 8, 2, 128
A_GROUP = A_Q_HEADS // A_KV_HEADS
B_PATTERNS = ((128, 1), (512, 4), (2048, 16))
B_HG = 4
B_HEADS = len(B_PATTERNS) * B_HG
C_HEADS, GRID_W, C_WIN_ROWS, C_WIN_COLS = 8, 64, 8, 16
C_NREL = 2 * C_WIN_ROWS - 1
C_NCOL = 2 * C_WIN_COLS - 1

PC_QA, PC_KA, PC_VA = 0, 8, 10
PC_QB, PC_KB, PC_VB = 12, 24, 36
PC_QC, PC_KC, PC_VC = 48, 56, 64
N_QKV_CHUNKS = 72
Q_PIECES = ((0, 8, PC_QA), (8, 12, PC_QB), (20, 8, PC_QC))
K_PIECES = ((0, 2, PC_KA), (2, 12, PC_KB), (14, 8, PC_KC))
V_PIECES = ((0, 2, PC_VA), (2, 12, PC_VB), (14, 8, PC_VC))
NQ_CHUNKS, NK_CHUNKS = 28, 22
Q_ROPE_UPTO, K_ROPE_UPTO = 20, 14
Q_CG, K_CG = 4, 2

ADAM_LR, ADAM_B1, ADAM_B2, ADAM_EPS, ADAM_WD, ADAM_STEP = 0.001, 0.9, 0.999, 1e-08, 0.01, 10

VMEM_LIMIT = 48 * 1024 * 1024


def _tile(dim, prefs):
    for p in prefs:
        if dim % p == 0:
            return p
    return dim


NN_WEIGHT_TILE_BYTES = 8 * 1024 * 1024
NT_WEIGHT_TILE_BYTES = 4 * 1024 * 1024
TN_ACC_BYTES = 6 * 1024 * 1024
MAX_COL_TILE = 2048


def _col_tile(ns):
    return ns if ns <= MAX_COL_TILE else _tile(ns, (MAX_COL_TILE, 1024, 512, 256, 128))


def _params(sem, **kw):
    return pltpu.CompilerParams(dimension_semantics=sem, vmem_limit_bytes=VMEM_LIMIT, **kw)


def _piece_map(pieces):
    def f(c):
        out = c - pieces[0][0] + pieces[0][2]
        for first, _, pfirst in pieces[1:]:
            out = jnp.where(c >= first, c - first + pfirst, out)
        return out
    return f


def _mm_nn(a, w, layer, *, out_dtype, name, res=None):
    M, K = a.shape
    nb, _, Kw, ns = w.shape
    assert Kw == K
    tn = _col_tile(ns)
    tm = _tile(M, (1024, 512, 256) if tn <= 512 else (512, 256))
    tk = _tile(K, tuple(t for t in (2048, 1408, 1024, 512, 256) if t * tn * 2 <= NN_WEIGHT_TILE_BYTES))
    nj, nk = ns // tn, K // tk

    def body(*refs):
        a_ref, w_ref = refs[:2]
        r_ref = None if res is None else refs[2]
        o_ref = refs[2 if res is None else 3]
        part = jnp.dot(a_ref[...].astype(BF16), w_ref[...], preferred_element_type=F32)
        if nk == 1:
            if r_ref is not None:
                part = part + r_ref[...]
            o_ref[...] = part.astype(out_dtype)
            return
        acc_ref = refs[-1]
        k = pl.program_id(3)

        @pl.when(k == 0)
        def _():
            acc_ref[...] = part

        @pl.when(k > 0)
        def _():
            acc_ref[...] += part

        @pl.when(k == nk - 1)
        def _():
            r = acc_ref[...]
            if r_ref is not None:
                r = r + r_ref[...]
            o_ref[...] = r.astype(out_dtype)

    in_specs = [pl.BlockSpec((tm, tk), lambda i, b, j, k: (i, k)),
                pl.BlockSpec((None, None, tk, tn), lambda i, b, j, k: (b, layer, k, j))]
    args = [a, w]
    if res is not None:
        in_specs.append(pl.BlockSpec((tm, tn), lambda i, b, j, k: (i, b * nj + j)))
        args.append(res)
    return pl.pallas_call(
        body, name=name, grid=(M // tm, nb, nj, nk), in_specs=in_specs,
        out_specs=pl.BlockSpec((tm, tn), lambda i, b, j, k: (i, b * nj + j)),
        out_shape=jax.ShapeDtypeStruct((M, nb * ns), out_dtype),
        scratch_shapes=[] if nk == 1 else [pltpu.VMEM((tm, tn), F32)],
        compiler_params=_params(("parallel", "parallel", "parallel", "arbitrary")),
    )(*args)


def _mm_nt(a, w, layer, *, out_dtype, name, dep=None):
    M, N = a.shape
    nb, _, K, ns = w.shape
    assert N == nb * ns
    tm = _tile(M, (1024, 512, 256))
    tn = _col_tile(ns)
    tk = _tile(K, tuple(t for t in (1024, 512, 256) if t * tn * 2 <= NT_WEIGHT_TILE_BYTES))
    nj = ns // tn
    nred = nb * nj

    def body(*refs):
        a_ref, w_ref = refs[:2]
        o_ref, acc_ref = refs[-2:]
        s = pl.program_id(2) * nj + pl.program_id(3)
        part = lax.dot_general(a_ref[...].astype(BF16), w_ref[...], (((1,), (1,)), ((), ())),
                               preferred_element_type=F32)

        @pl.when(s == 0)
        def _():
            acc_ref[...] = part

        @pl.when(s > 0)
        def _():
            acc_ref[...] += part

        @pl.when(s == nred - 1)
        def _():
            o_ref[...] = acc_ref[...].astype(out_dtype)

    in_specs = [pl.BlockSpec((tm, tn), lambda i, kk, b, j: (i, b * nj + j)),
                pl.BlockSpec((None, None, tk, tn), lambda i, kk, b, j: (b, layer, kk, j))]
    args = [a, w]
    if dep is not None:
        in_specs.append(ANY_SPEC)
        args.append(dep)
    return pl.pallas_call(
        body, name=name, grid=(M // tm, K // tk, nb, nj), in_specs=in_specs,
        out_specs=pl.BlockSpec((tm, tk), lambda i, kk, b, j: (i, kk)),
        out_shape=jax.ShapeDtypeStruct((M, K), out_dtype),
        scratch_shapes=[pltpu.VMEM((tm, tk), F32)],
        compiler_params=_params(("parallel", "parallel", "arbitrary", "arbitrary")),
    )(*args)


def _mm_tn(a, g, nb, *, name):
    M, Ka = a.shape
    N = g.shape[1]
    ns = N // nb
    tn = _col_tile(ns)
    tka = _tile(Ka, tuple(t for t in (1024, 512, 256) if t * tn * 4 <= TN_ACC_BYTES))
    tm = _tile(M, (2048, 1024, 512, 256))
    nj, nm = ns // tn, M // tm

    def body(a_ref, g_ref, o_ref, acc_ref):
        m = pl.program_id(3)
        part = lax.dot_general(a_ref[...].astype(BF16), g_ref[...].astype(BF16), (((0,), (0,)), ((), ())),
                               preferred_element_type=F32)

        @pl.when(m == 0)
        def _():
            acc_ref[...] = part

        @pl.when(m > 0)
        def _():
            acc_ref[...] += part

        @pl.when(m == nm - 1)
        def _():
            o_ref[...] = acc_ref[...].astype(BF16)

    return pl.pallas_call(
        body, name=name, grid=(Ka // tka, nb, nj, nm),
        in_specs=[pl.BlockSpec((tm, tka), lambda ka, b, j, m: (m, ka)),
                  pl.BlockSpec((tm, tn), lambda ka, b, j, m: (m, b * nj + j))],
        out_specs=pl.BlockSpec((None, tka, tn), lambda ka, b, j, m: (b, ka, j)),
        out_shape=jax.ShapeDtypeStruct((nb, Ka, ns), BF16),
        scratch_shapes=[pltpu.VMEM((tka, tn), F32)],
        compiler_params=_params(("parallel", "parallel", "parallel", "arbitrary")),
    )(a, g)


def _blocks_to_wide(w, *, name):
    nb, _, K, ns = w.shape

    def body(i_ref, o_ref):
        o_ref[...] = i_ref[...]

    return pl.pallas_call(
        body, name=name, grid=(nb,), in_specs=[pl.BlockSpec((None, None, K, ns), lambda b: (b, 0, 0, 0))],
        out_specs=pl.BlockSpec((None, None, K, ns), lambda b: (0, 0, 0, b)),
        out_shape=jax.ShapeDtypeStruct((1, 1, K, nb * ns), w.dtype), compiler_params=_params(("parallel",)),
    )(w)


def _wide_to_blocks(g, nb, *, name):
    _, K, N = g.shape
    ns = N // nb

    def body(i_ref, o_ref):
        o_ref[...] = i_ref[...]

    return pl.pallas_call(
        body, name=name, grid=(nb,), in_specs=[pl.BlockSpec((None, K, ns), lambda b: (0, 0, b))],
        out_specs=pl.BlockSpec((None, K, ns), lambda b: (b, 0, 0)),
        out_shape=jax.ShapeDtypeStruct((nb, K, ns), g.dtype), compiler_params=_params(("parallel",)),
    )(g)


def _exact_mm(a, e, *, name):
    R, K = a.shape
    N = e.shape[1]

    def body(a_ref, e_ref, o_ref):
        x = a_ref[...]
        hi = x.astype(BF16)
        r1 = x - hi.astype(F32)
        mid = r1.astype(BF16)
        lo = (r1 - mid.astype(F32)).astype(BF16)
        ev = e_ref[...]
        o_ref[...] = (jnp.dot(hi, ev, preferred_element_type=F32) + jnp.dot(mid, ev, preferred_element_type=F32)
                      + jnp.dot(lo, ev, preferred_element_type=F32))

    return pl.pallas_call(body, name=name, out_shape=jax.ShapeDtypeStruct((R, N), F32),
                          compiler_params=pltpu.CompilerParams(vmem_limit_bytes=VMEM_LIMIT))(a, e)


def _rms_fwd(x, g, dep, *, name):
    L, D = x.shape
    tl = _tile(L, (512, 256, 128))

    def body(x_ref, g_ref, _dep, h_ref):
        xv = x_ref[...]
        rstd = lax.rsqrt(jnp.mean(xv * xv, axis=-1, keepdims=True) + NORM_EPS)
        h_ref[...] = (xv * rstd * g_ref[...]).astype(BF16)

    return pl.pallas_call(
        body, name=name, grid=(L // tl,),
        in_specs=[pl.BlockSpec((tl, D), lambda t: (t, 0)), pl.BlockSpec((1, D), lambda t: (0, 0)), ANY_SPEC],
        out_specs=pl.BlockSpec((tl, D), lambda t: (t, 0)),
        out_shape=jax.ShapeDtypeStruct((L, D), BF16),
        compiler_params=_params(("parallel",)),
    )(x, g, dep)


def _rms_bwd(x, g, dy, dres, *, name):
    L, D = x.shape
    tl = _tile(L, (256, 128))

    def body(x_ref, g_ref, dy_ref, dres_ref, dx_ref, dxb_ref, dg_ref):
        t = pl.program_id(0)
        xv = x_ref[...]
        rstd = lax.rsqrt(jnp.mean(xv * xv, axis=-1, keepdims=True) + NORM_EPS)
        xhat = xv * rstd
        dyv = dy_ref[...]
        dxhat = dyv * g_ref[...]
        c = jnp.mean(dxhat * xhat, axis=-1, keepdims=True)
        dx = dres_ref[...] + rstd * (dxhat - xhat * c)
        dx_ref[...] = dx
        dxb_ref[...] = dx.astype(BF16)
        dgp = jnp.sum(dyv * xhat, axis=0, keepdims=True)

        @pl.when(t == 0)
        def _():
            dg_ref[...] = dgp

        @pl.when(t > 0)
        def _():
            dg_ref[...] += dgp

    row = pl.BlockSpec((tl, D), lambda t: (t, 0))
    vec = pl.BlockSpec((1, D), lambda t: (0, 0))
    return pl.pallas_call(
        body, name=name, grid=(L // tl,), in_specs=[row, vec, row, row], out_specs=[row, row, vec],
        out_shape=[jax.ShapeDtypeStruct((L, D), F32), jax.ShapeDtypeStruct((L, D), BF16),
                   jax.ShapeDtypeStruct((1, D), F32)],
        compiler_params=_params(("arbitrary",)),
    )(x, g, dy, dres)


def _gate_fwd(proj, ta, tb, tc, *, name):
    L, D = ta.shape
    tl, tcw = _tile(L, (512, 256, 128)), _tile(D, (1024, 512, 256, 128))
    off = N_QKV_CHUNKS * HEAD_DIM // tcw
    nd = D // tcw

    def body(g0, g1, g2, a_ref, b_ref, c_ref, o_ref):
        m = (jax.nn.sigmoid(g0[...]) * a_ref[...].astype(F32) + jax.nn.sigmoid(g1[...]) * b_ref[...].astype(F32)
             + jax.nn.sigmoid(g2[...]) * c_ref[...].astype(F32))
        o_ref[...] = m.astype(BF16)

    blk = pl.BlockSpec((tl, tcw), lambda t, j: (t, j))
    gl = [pl.BlockSpec((tl, tcw), functools.partial(lambda t, j, i: (t, off + i * nd + j), i=i)) for i in range(3)]
    return pl.pallas_call(
        body, name=name, grid=(L // tl, nd), in_specs=gl + [blk, blk, blk], out_specs=blk,
        out_shape=jax.ShapeDtypeStruct((L, D), BF16),
        compiler_params=_params(("parallel", "parallel")),
    )(proj, proj, proj, ta, tb, tc)


def _gate_bwd(proj, ta, tb, tc, dmerged, *, name):
    L, D = ta.shape
    ncols = proj.shape[1]
    tl, tcw = _tile(L, (512, 256, 128)), _tile(D, (1024, 512, 256, 128))
    off = N_QKV_CHUNKS * HEAD_DIM // tcw
    nd = D // tcw

    def body(g0, g1, g2, a_ref, b_ref, c_ref, dm_ref, da_ref, db_ref, dc_ref, dgl_ref):
        i = pl.program_id(2)
        sg = jax.nn.sigmoid(jnp.where(i == 0, g0[...], jnp.where(i == 1, g1[...], g2[...])))
        sel_t = jnp.where(i == 0, a_ref[...], jnp.where(i == 1, b_ref[...], c_ref[...])).astype(F32)
        dt = dm_ref[...] * sg
        dtb = dt.astype(BF16)

        @pl.when(i == 0)
        def _():
            da_ref[...] = dtb

        @pl.when(i == 1)
        def _():
            db_ref[...] = dtb

        @pl.when(i == 2)
        def _():
            dc_ref[...] = dtb

        dgl_ref[...] = (dt * sel_t * (1.0 - sg)).astype(BF16)

    blk = pl.BlockSpec((tl, tcw), lambda t, j, i: (t, j))
    gl = [pl.BlockSpec((tl, tcw), functools.partial(lambda t, j, i, q: (t, off + q * nd + j), q=q)) for q in range(3)]
    return pl.pallas_call(
        body, name=name, grid=(L // tl, nd, 3), in_specs=gl + [blk, blk, blk, blk],
        out_specs=[blk, blk, blk, pl.BlockSpec((tl, tcw), lambda t, j, i: (t, off + i * nd + j))],
        out_shape=[jax.ShapeDtypeStruct((L, D), BF16)] * 3 + [jax.ShapeDtypeStruct((L, ncols), BF16)],
        compiler_params=_params(("parallel", "parallel", "arbitrary")),
    )(proj, proj, proj, ta, tb, tc, dmerged)


def _swiglu_fwd(gu, *, name):
    L, F2 = gu.shape
    F = F2 // 2
    tl = _tile(L, (128, 64))

    def body(gu_ref, o_ref):
        gt = gu_ref[:, :F].astype(F32)
        o_ref[...] = (gt * jax.nn.sigmoid(gt) * gu_ref[:, F:].astype(F32)).astype(BF16)

    return pl.pallas_call(
        body, name=name, grid=(L // tl,), in_specs=[pl.BlockSpec((tl, F2), lambda t: (t, 0))],
        out_specs=pl.BlockSpec((tl, F), lambda t: (t, 0)),
        out_shape=jax.ShapeDtypeStruct((L, F), BF16),
        compiler_params=_params(("parallel",)),
    )(gu)


def _swiglu_bwd(gu, dact, *, name):
    L, F2 = gu.shape
    F = F2 // 2
    tl = _tile(L, (128, 64))

    def body(gu_ref, d_ref, o_ref):
        gt, up, d = gu_ref[:, :F].astype(F32), gu_ref[:, F:].astype(F32), d_ref[...].astype(F32)
        sg = jax.nn.sigmoid(gt)
        o_ref[:, :F] = (d * up * sg * (1.0 + gt * (1.0 - sg))).astype(BF16)
        o_ref[:, F:] = (d * gt * sg).astype(BF16)

    return pl.pallas_call(
        body, name=name, grid=(L // tl,),
        in_specs=[pl.BlockSpec((tl, F2), lambda t: (t, 0)), pl.BlockSpec((tl, F), lambda t: (t, 0))],
        out_specs=pl.BlockSpec((tl, F2), lambda t: (t, 0)),
        out_shape=jax.ShapeDtypeStruct((L, F2), BF16),
        compiler_params=_params(("parallel",)),
    )(gu, dact)


def _loss(y, tgt, *, name):
    L, D = y.shape
    tl = _tile(L, (256, 128))
    nt = L // tl

    def body(y_ref, t_ref, dy_ref, dyb_ref, loss_ref, acc_ref):
        t = pl.program_id(0)
        e = y_ref[...] - t_ref[...]
        dy = e * (1.0 / D)
        dy_ref[...] = dy
        dyb_ref[...] = dy.astype(BF16)
        part = jnp.sum(e * e, axis=0, keepdims=True)

        @pl.when(t == 0)
        def _():
            acc_ref[...] = part

        @pl.when(t > 0)
        def _():
            acc_ref[...] += part

        @pl.when(t == nt - 1)
        def _():
            loss_ref[...] = jnp.broadcast_to(jnp.sum(acc_ref[...], axis=-1, keepdims=True) * (0.5 / D), (1, 128))

    row = pl.BlockSpec((tl, D), lambda t: (t, 0))
    return pl.pallas_call(
        body, name=name, grid=(nt,), in_specs=[row, row],
        out_specs=[row, row, pl.BlockSpec((1, 128), lambda t: (0, 0))],
        out_shape=[jax.ShapeDtypeStruct((L, D), F32), jax.ShapeDtypeStruct((L, D), BF16),
                   jax.ShapeDtypeStruct((1, 128), F32)],
        scratch_shapes=[pltpu.VMEM((1, D), F32)],
        compiler_params=_params(("arbitrary",)),
    )(y, tgt)


def _rope(v, cos, sin_signed):
    return v * cos + pltpu.roll(v, HEAD_DIM // 2, 1) * sin_signed


def _head_mean(x, j=0):
    if j % 2:
        return jnp.mean(x, axis=-1, keepdims=True)
    hi = x.astype(BF16)
    lo = (x - hi.astype(F32)).astype(BF16)
    ones = jnp.ones((HEAD_DIM, HEAD_DIM), BF16)
    total = jnp.dot(hi, ones, preferred_element_type=F32) + jnp.dot(lo, ones, preferred_element_type=F32)
    return total * (1.0 / HEAD_DIM)


def _qk_fwd(proj, gtab, cos, sin, pieces, nchunks, rope_upto, cg, *, name, dep=None):
    L = proj.shape[0]
    tl = _tile(L, (512, 256, 128))
    W = cg * HEAD_DIM
    pmap = _piece_map(tuple((a // cg, n // cg, p // cg) for a, n, p in pieces))

    def body(*refs):
        p_ref, g_ref, cos_ref, sin_ref = refs[:4]
        o_ref = refs[-1]
        c = pl.program_id(1)

        def norm(j):
            cols = slice(j * HEAD_DIM, (j + 1) * HEAD_DIM)
            x = p_ref[:, cols]
            rstd = lax.rsqrt(_head_mean(x * x, j) + NORM_EPS)
            return cols, x * rstd * g_ref[:, cols]

        @pl.when(c < rope_upto // cg)
        def _():
            for j in range(cg):
                cols, y = norm(j)
                o_ref[:, cols] = _rope(y, cos_ref[...], sin_ref[...])

        @pl.when(c >= rope_upto // cg)
        def _():
            for j in range(cg):
                cols, y = norm(j)
                o_ref[:, cols] = y

    pos = pl.BlockSpec((tl, HEAD_DIM), lambda t, c: (t, 0))
    in_specs = [pl.BlockSpec((tl, W), lambda t, c: (t, pmap(c))),
                pl.BlockSpec((None, 1, W), lambda t, c: (c, 0, 0)), pos, pos]
    args = [proj, gtab, cos, sin]
    if dep is not None:
        in_specs.append(ANY_SPEC)
        args.append(dep)
    return pl.pallas_call(
        body, name=name, grid=(L // tl, nchunks // cg), in_specs=in_specs,
        out_specs=pl.BlockSpec((tl, W), lambda t, c: (t, c)),
        out_shape=jax.ShapeDtypeStruct((L, nchunks * HEAD_DIM), F32),
        compiler_params=_params(("parallel", "parallel")),
    )(*args)


def _qk_bwd(dqk, proj, gtab, cos, sin, dproj, pieces, nchunks, rope_upto, cg, *, name):
    L = proj.shape[0]
    tl = _tile(L, (512, 256, 128))
    W = cg * HEAD_DIM
    pmap = _piece_map(tuple((a // cg, n // cg, p // cg) for a, n, p in pieces))

    def body(d_ref, p_ref, g_ref, cos_ref, sin_ref, _, o_ref, dg_ref):
        c, t = pl.program_id(0), pl.program_id(1)

        @pl.when(t == 0)
        def _():
            dg_ref[...] = jnp.zeros_like(dg_ref)

        for j in range(cg):
            cols = slice(j * HEAD_DIM, (j + 1) * HEAD_DIM)
            x = p_ref[:, cols]
            rstd = lax.rsqrt(_head_mean(x * x, j) + NORM_EPS)
            xhat = x * rstd
            dy = d_ref[:, cols]
            dy = jnp.where(c < rope_upto // cg, _rope(dy, cos_ref[...], -sin_ref[...]), dy)
            dxhat = dy * g_ref[:, cols]
            cm = _head_mean(dxhat * xhat, j)
            o_ref[:, cols] = (rstd * (dxhat - xhat * cm)).astype(BF16)
            dg_ref[:, cols] += jnp.sum(dy * xhat, axis=0, keepdims=True)

    pos = pl.BlockSpec((tl, HEAD_DIM), lambda c, t: (t, 0))
    gspec = pl.BlockSpec((None, 1, W), lambda c, t: (c, 0, 0))
    out, dg = pl.pallas_call(
        body, name=name, grid=(nchunks // cg, L // tl),
        in_specs=[pl.BlockSpec((tl, W), lambda c, t: (t, c)),
                  pl.BlockSpec((tl, W), lambda c, t: (t, pmap(c))), gspec, pos, pos,
                  pl.BlockSpec(memory_space=pl.ANY)],
        out_specs=[pl.BlockSpec((tl, W), lambda c, t: (t, pmap(c))), gspec],
        out_shape=[jax.ShapeDtypeStruct(dproj.shape, BF16), jax.ShapeDtypeStruct((nchunks // cg, 1, W), F32)],
        input_output_aliases={5: 0},
        compiler_params=_params(("parallel", "arbitrary")),
    )(dqk, proj, gtab, cos, sin, dproj)
    return out, dg


def _v_bwd(dv, dproj, *, name):
    L = dv.shape[0]
    tl = _tile(L, (2048, 1024, 512, 256, 128))
    pmap = _piece_map(tuple((a // 2, n // 2, p // 2) for a, n, p in V_PIECES))

    def body(d_ref, _, o_ref):
        o_ref[...] = d_ref[...].astype(BF16)

    return pl.pallas_call(
        body, name=name, grid=(L // tl, NK_CHUNKS // 2),
        in_specs=[pl.BlockSpec((tl, 2 * HEAD_DIM), lambda t, c: (t, c)), pl.BlockSpec(memory_space=pl.ANY)],
        out_specs=pl.BlockSpec((tl, 2 * HEAD_DIM), lambda t, c: (t, pmap(c))),
        out_shape=jax.ShapeDtypeStruct(dproj.shape, BF16),
        input_output_aliases={1: 0},
        compiler_params=_params(("parallel", "parallel")),
    )(dv, dproj)


def _band_geometry(L, dil, radius):
    n = L // dil
    bq = min(256, max(n // 2, 64), n)
    width = min(bq + 2 * radius, n)
    nsub = _tile(n // bq, (8, 4, 2)) if dil == 1 else 1
    return n, bq, width, nsub


def _band_loop(dil, one):
    if dil == 1:
        one(0, 0)
    else:
        lax.fori_loop(0, dil, one, 0, unroll=min(dil, 4))


def _band_rows(dil, r, first, count):
    if dil == 1:
        return pl.ds(pl.multiple_of(first, 8), count)
    return pl.ds(r + first * dil, count, stride=dil)


def _band_mask(i, bq, width, radius, ws):
    qpos = i * bq + lax.broadcasted_iota(jnp.int32, (bq, width), 0)
    kpos = ws + lax.broadcasted_iota(jnp.int32, (bq, width), 1)
    return jnp.abs(kpos - qpos) <= radius


def _band_fwd(qn, kn, proj, *, dil, radius, nkv, group, q0, k0, v0, sink=None, name):
    L = qn.shape[0]
    n, bq, width, nsub = _band_geometry(L, dil, radius)
    tq = nsub * bq * dil
    nh = nkv * group

    def body(*refs):
        if sink is None:
            q_ref, k_ref, v_ref, o_ref, lse_ref = refs
        else:
            q_ref, k_ref, v_ref, s_ref, o_ref, lse_ref = refs
        for sb in range(nsub):
            block(sb, q_ref, k_ref, v_ref, None if sink is None else s_ref, o_ref, lse_ref)

    def block(sb, q_ref, k_ref, v_ref, s_ref, o_ref, lse_ref):
        i = pl.program_id(2) * nsub + sb
        ws = jnp.clip(i * bq - radius, 0, n - width)
        valid = _band_mask(i, bq, width, radius, ws)

        def one(r, carry):
            qrows = _band_rows(dil, r, sb * bq, bq)
            krows = _band_rows(dil, r, ws, width)
            q = q_ref[qrows, :].astype(BF16)
            k = k_ref[krows, :].astype(BF16)
            v = v_ref[krows, :].astype(BF16)
            s = lax.dot_general(q, k, (((1,), (1,)), ((), ())), preferred_element_type=F32) * ATT_SCALE
            s = jnp.where(valid, s, NEG)
            m = jnp.max(s, axis=-1, keepdims=True)
            if sink is not None:
                m = jnp.maximum(m, s_ref[...][:, :1])
            p = jnp.exp(s - m)
            denom = jnp.sum(p, axis=-1, keepdims=True)
            if sink is not None:
                denom = denom + jnp.exp(s_ref[...][:, :1] - m)
            pn = (p / denom).astype(BF16)
            o_ref[qrows, :] = jnp.dot(pn, v, preferred_element_type=F32)
            lse_ref[qrows, :] = jnp.broadcast_to(m + jnp.log(denom), (bq, HEAD_DIM))
            return carry

        _band_loop(dil, one)

    qspec = pl.BlockSpec((tq, HEAD_DIM), lambda hk, g, i: (i, q0 + hk * group + g))
    in_specs = [qspec,
                pl.BlockSpec((L, HEAD_DIM), lambda hk, g, i: (0, k0 + hk)),
                pl.BlockSpec((L, HEAD_DIM), lambda hk, g, i: (0, v0 + hk))]
    args = [qn, kn, proj]
    if sink is not None:
        in_specs.append(pl.BlockSpec((None, 1, HEAD_DIM), lambda hk, g, i: (hk * group + g, 0, 0)))
        args.append(sink)
    ospec = pl.BlockSpec((tq, HEAD_DIM), lambda hk, g, i: (i, hk * group + g))
    return pl.pallas_call(
        body, name=name, grid=(nkv, group, n // (bq * nsub)), in_specs=in_specs, out_specs=[ospec, ospec],
        out_shape=[jax.ShapeDtypeStruct((L, nh * HEAD_DIM), F32)] * 2,
        compiler_params=_params(("parallel", "parallel", "arbitrary")),
    )(*args)


def _band_bwd(qn, kn, proj, do, o, lse, dq_buf, dk_buf, dv_buf, *, dil, radius, nkv, group, q0, k0, v0, o0,
              sink=None, name):
    L = qn.shape[0]
    n, bq, width, nsub = _band_geometry(L, dil, radius)
    tq = nsub * bq * dil
    nh = nkv * group
    n_in = 6 + (1 if sink is not None else 0)

    def body(*refs):
        q_ref, k_ref, v_ref, do_ref, o_ref, lse_ref = refs[:6]
        s_ref = refs[6] if sink is not None else None
        outs = refs[n_in + 3:]
        dq_ref, dk_ref, dv_ref = outs[:3]
        ds_ref = outs[3] if sink is not None else None
        g, step = pl.program_id(1), pl.program_id(2)

        @pl.when((g == 0) & (step == 0))
        def _():
            dk_ref[...] = jnp.zeros_like(dk_ref)
            dv_ref[...] = jnp.zeros_like(dv_ref)

        if sink is not None:
            @pl.when(step == 0)
            def _():
                ds_ref[...] = jnp.zeros_like(ds_ref)

        for sb in range(nsub):
            block(sb, q_ref, k_ref, v_ref, do_ref, o_ref, lse_ref, s_ref, dq_ref, dk_ref, dv_ref, ds_ref)

    def block(sb, q_ref, k_ref, v_ref, do_ref, o_ref, lse_ref, s_ref, dq_ref, dk_ref, dv_ref, ds_ref):
        i = pl.program_id(2) * nsub + sb
        ws = jnp.clip(i * bq - radius, 0, n - width)
        valid = _band_mask(i, bq, width, radius, ws)

        def one(r, carry):
            qrows = _band_rows(dil, r, sb * bq, bq)
            krows = _band_rows(dil, r, ws, width)
            q = q_ref[qrows, :].astype(BF16)
            k = k_ref[krows, :].astype(BF16)
            v = v_ref[krows, :].astype(BF16)
            dov = do_ref[qrows, :]
            lse_v = lse_ref[qrows, :][:, :1]
            delta = jnp.sum(dov * o_ref[qrows, :], axis=-1, keepdims=True)
            dob = dov.astype(BF16)
            s = lax.dot_general(q, k, (((1,), (1,)), ((), ())), preferred_element_type=F32) * ATT_SCALE
            p = jnp.where(valid, jnp.exp(s - lse_v), 0.0)
            dp = lax.dot_general(dob, v, (((1,), (1,)), ((), ())), preferred_element_type=F32)
            dsb = (p * (dp - delta)).astype(BF16)
            dq_ref[qrows, :] = jnp.dot(dsb, k, preferred_element_type=F32) * ATT_SCALE
            dk_ref[krows, :] += lax.dot_general(dsb, q, (((0,), (0,)), ((), ())),
                                                preferred_element_type=F32) * ATT_SCALE
            dv_ref[krows, :] += lax.dot_general(p.astype(BF16), dob, (((0,), (0,)), ((), ())),
                                                preferred_element_type=F32)
            if sink is not None:
                ps = jnp.exp(s_ref[...][:, :1] - lse_v)
                ds_ref[...] += jnp.broadcast_to(jnp.sum(-ps * delta, axis=0, keepdims=True), (1, HEAD_DIM))
            return carry

        _band_loop(dil, one)

    hspec = pl.BlockSpec((tq, HEAD_DIM), lambda hk, g, i: (i, o0 + hk * group + g))
    qspec = pl.BlockSpec((tq, HEAD_DIM), lambda hk, g, i: (i, q0 + hk * group + g))
    kspec = pl.BlockSpec((L, HEAD_DIM), lambda hk, g, i: (0, k0 + hk))
    any_spec = pl.BlockSpec(memory_space=pl.ANY)
    in_specs = [qspec, kspec, pl.BlockSpec((L, HEAD_DIM), lambda hk, g, i: (0, v0 + hk)), hspec, hspec, hspec]
    args = [qn, kn, proj, do, o, lse]
    if sink is not None:
        in_specs.append(pl.BlockSpec((None, 1, HEAD_DIM), lambda hk, g, i: (hk * group + g, 0, 0)))
        args.append(sink)
    in_specs += [any_spec] * 3
    args += [dq_buf, dk_buf, dv_buf]
    out_specs = [qspec, kspec, kspec]
    out_shape = [jax.ShapeDtypeStruct(dq_buf.shape, F32), jax.ShapeDtypeStruct(dk_buf.shape, F32),
                 jax.ShapeDtypeStruct(dv_buf.shape, F32)]
    if sink is not None:
        out_specs.append(pl.BlockSpec((None, 1, HEAD_DIM), lambda hk, g, i: (hk * group + g, 0, 0)))
        out_shape.append(jax.ShapeDtypeStruct((nh, 1, HEAD_DIM), F32))
    return pl.pallas_call(
        body, name=name, grid=(nkv, group, n // (bq * nsub)), in_specs=in_specs, out_specs=out_specs,
        out_shape=out_shape,
        input_output_aliases={n_in: 0, n_in + 1: 1, n_in + 2: 2},
        compiler_params=_params(("parallel", "arbitrary", "arbitrary")),
    )(*args)


def _combine_b(os_, lses, *, name):
    L, W = os_[0].shape
    tl = _tile(L, (256, 128))

    def body(o0, o1, o2, l0, l1, l2, out_ref, lt_ref):
        a, b, c = l0[...], l1[...], l2[...]
        m = jnp.maximum(jnp.maximum(a, b), c)
        ea, eb, ec = jnp.exp(a - m), jnp.exp(b - m), jnp.exp(c - m)
        tot = ea + eb + ec
        out_ref[...] = (ea * o0[...] + eb * o1[...] + ec * o2[...]) / tot
        lt_ref[...] = m + jnp.log(tot)

    blk = pl.BlockSpec((tl, W), lambda t: (t, 0))
    return pl.pallas_call(
        body, name=name, grid=(L // tl,), in_specs=[blk] * 6, out_specs=[blk, blk],
        out_shape=[jax.ShapeDtypeStruct((L, W), F32)] * 2, compiler_params=_params(("parallel",)),
    )(*os_, *lses)


C_QROWS = 4
C_KROWS = C_QROWS + C_WIN_ROWS
C_QUERIES, C_KEYS = C_QROWS * GRID_W, C_KROWS * GRID_W
_C_KIND_OFFSETS = (C_WIN_ROWS - 1, C_WIN_ROWS - 1 - C_WIN_ROWS // 2, C_WIN_ROWS - 1 - (C_KROWS - C_QROWS))


def _c_geometry(L):
    rows = L // GRID_W
    assert rows >= C_KROWS and rows % C_QROWS == 0
    return rows


def _c_bias_tiles(bias_t):
    cq = np.arange(GRID_W)[:, None]
    ck = np.arange(GRID_W)[None, :]
    start = np.clip(cq - C_WIN_COLS // 2, 0, GRID_W - C_WIN_COLS)
    masked = jnp.where(jnp.asarray((ck >= start) & (ck < start + C_WIN_COLS)), bias_t, NEG)
    blank = jnp.full((C_HEADS, GRID_W, GRID_W), NEG, F32)
    kinds = []
    for kind in range(3):
        off = _C_KIND_OFFSETS[kind]
        row_blocks = []
        for a in range(C_QROWS):
            lo = (0, a, C_KROWS - C_WIN_ROWS)[kind]
            row_blocks.append(jnp.concatenate(
                [masked[:, b - a + off] if lo <= b < lo + C_WIN_ROWS else blank for b in range(C_KROWS)], axis=-1))
        kinds.append(jnp.concatenate(row_blocks, axis=-2))
    return jnp.stack(kinds, axis=1)


def _c_block(g, rows):
    r0 = g * C_QROWS
    k0 = jnp.clip(r0 - C_WIN_ROWS // 2, 0, rows - C_KROWS)
    kind = jnp.where(g == 0, 0, jnp.where(g == rows // C_QROWS - 1, 2, 1))
    return k0, kind, k0 - r0 + (C_WIN_ROWS - 1)


def _c_fwd(qn, kn, proj, tiles, *, name):
    L = qn.shape[0]
    rows = _c_geometry(L)

    nsub = _tile(rows // C_QROWS, (4, 2))

    def body(q_ref, k_ref, v_ref, t_ref, o_ref, lse_ref):
        for sb in range(nsub):
            k0, kind, _ = _c_block(pl.program_id(1) * nsub + sb, rows)
            krows = pl.ds(pl.multiple_of(k0 * GRID_W, GRID_W), C_KEYS)
            qrows = pl.ds(sb * C_QUERIES, C_QUERIES)
            q = q_ref[qrows, :].astype(BF16)
            k = k_ref[krows, :].astype(BF16)
            v = v_ref[krows, :].astype(BF16)
            s = lax.dot_general(q, k, (((1,), (1,)), ((), ())), preferred_element_type=F32) * ATT_SCALE + t_ref[kind]
            m = jnp.max(s, axis=-1, keepdims=True)
            p = jnp.exp(s - m)
            denom = jnp.sum(p, axis=-1, keepdims=True)
            o_ref[qrows, :] = jnp.dot((p / denom).astype(BF16), v, preferred_element_type=F32)
            lse_ref[qrows, :] = jnp.broadcast_to(m + jnp.log(denom), (C_QUERIES, HEAD_DIM))

    ospec = pl.BlockSpec((nsub * C_QUERIES, HEAD_DIM), lambda h, g: (g, h))
    return pl.pallas_call(
        body, name=name, grid=(C_HEADS, rows // (C_QROWS * nsub)),
        in_specs=[pl.BlockSpec((nsub * C_QUERIES, HEAD_DIM), lambda h, g: (g, 20 + h)),
                  pl.BlockSpec((L, HEAD_DIM), lambda h, g: (0, 14 + h)),
                  pl.BlockSpec((L, HEAD_DIM), lambda h, g: (0, PC_VC + h)),
                  pl.BlockSpec((None, 3, C_QUERIES, C_KEYS), lambda h, g: (h, 0, 0, 0))],
        out_specs=[ospec, ospec],
        out_shape=[jax.ShapeDtypeStruct((L, C_HEADS * HEAD_DIM), F32)] * 2,
        compiler_params=_params(("parallel", "arbitrary")),
    )(qn, kn, proj, tiles)


def _c_bwd(qn, kn, proj, tiles, do, o, lse, dq_buf, dk_buf, dv_buf, *, name):
    L = qn.shape[0]
    rows = _c_geometry(L)

    nsub = _tile(rows // C_QROWS, (4, 2))

    def body(q_ref, k_ref, v_ref, t_ref, do_ref, o_ref, lse_ref, _a, _b, _c, dq_ref, dk_ref, dv_ref, dt_ref):
        @pl.when(pl.program_id(1) == 0)
        def _():
            dk_ref[...] = jnp.zeros_like(dk_ref)
            dv_ref[...] = jnp.zeros_like(dv_ref)
            dt_ref[...] = jnp.zeros_like(dt_ref)

        for sb in range(nsub):
            k0, kind, off = _c_block(pl.program_id(1) * nsub + sb, rows)
            krows = pl.ds(pl.multiple_of(k0 * GRID_W, GRID_W), C_KEYS)
            qrows = pl.ds(sb * C_QUERIES, C_QUERIES)
            q = q_ref[qrows, :].astype(BF16)
            k = k_ref[krows, :].astype(BF16)
            v = v_ref[krows, :].astype(BF16)
            dov = do_ref[qrows, :]
            dob = dov.astype(BF16)
            delta = jnp.sum(dov * o_ref[qrows, :], axis=-1, keepdims=True)
            s = lax.dot_general(q, k, (((1,), (1,)), ((), ())), preferred_element_type=F32) * ATT_SCALE + t_ref[kind]
            p = jnp.exp(s - lse_ref[qrows, :][:, :1])
            dp = lax.dot_general(dob, v, (((1,), (1,)), ((), ())), preferred_element_type=F32)
            ds = p * (dp - delta)
            for a in range(C_QROWS):
                for b in range(C_KROWS):
                    rel = jnp.clip(b - a + off, 0, C_NREL - 1)
                    dt_ref[rel] += ds[a * GRID_W:(a + 1) * GRID_W, b * GRID_W:(b + 1) * GRID_W]
            dsb = ds.astype(BF16)
            dq_ref[qrows, :] = jnp.dot(dsb, k, preferred_element_type=F32) * ATT_SCALE
            dk_ref[krows, :] += lax.dot_general(dsb, q, (((0,), (0,)), ((), ())),
                                                preferred_element_type=F32) * ATT_SCALE
            dv_ref[krows, :] += lax.dot_general(p.astype(BF16), dob, (((0,), (0,)), ((), ())),
                                                preferred_element_type=F32)

    hspec = pl.BlockSpec((nsub * C_QUERIES, HEAD_DIM), lambda h, g: (g, h))
    qspec = pl.BlockSpec((nsub * C_QUERIES, HEAD_DIM), lambda h, g: (g, 20 + h))
    kspec = pl.BlockSpec((L, HEAD_DIM), lambda h, g: (0, 14 + h))
    any_spec = pl.BlockSpec(memory_space=pl.ANY)
    return pl.pallas_call(
        body, name=name, grid=(C_HEADS, rows // (C_QROWS * nsub)),
        in_specs=[qspec, kspec, pl.BlockSpec((L, HEAD_DIM), lambda h, g: (0, PC_VC + h)),
                  pl.BlockSpec((None, 3, C_QUERIES, C_KEYS), lambda h, g: (h, 0, 0, 0)),
                  hspec, hspec, hspec, any_spec, any_spec, any_spec],
        out_specs=[qspec, kspec, kspec,
                   pl.BlockSpec((None, C_NREL, GRID_W, GRID_W), lambda h, r: (h, 0, 0, 0))],
        out_shape=[jax.ShapeDtypeStruct(dq_buf.shape, F32), jax.ShapeDtypeStruct(dk_buf.shape, F32),
                   jax.ShapeDtypeStruct(dv_buf.shape, F32),
                   jax.ShapeDtypeStruct((C_HEADS, C_NREL, GRID_W, GRID_W), F32)],
        input_output_aliases={7: 0, 8: 1, 9: 2},
        compiler_params=_params(("parallel", "arbitrary")),
    )(qn, kn, proj, tiles, do, o, lse, dq_buf, dk_buf, dv_buf)


def _c_expand_matrix():
    cq = np.arange(GRID_W)[:, None]
    ck = np.arange(GRID_W)[None, :]
    d = (ck - cq + (C_WIN_COLS - 1)).reshape(-1)
    e = np.zeros((GRID_W * GRID_W, HEAD_DIM), np.float32)
    okd = (d >= 0) & (d < C_NCOL)
    e[np.arange(GRID_W * GRID_W)[okd], d[okd]] = 1.0
    return e


def _peer(p):
    return (p // 4, (p // 2) % 2, p % 2)


def _my_index():
    return 4 * lax.axis_index("x") + 2 * lax.axis_index("y") + lax.axis_index("c")


HBM_SPEC = pl.BlockSpec(memory_space=pltpu.HBM)
SEM_SPEC = pl.BlockSpec(memory_space=pltpu.SEMAPHORE)
ANY_SPEC = pl.BlockSpec(memory_space=pl.ANY)
DATAFLOW = pltpu.SideEffectType.DATAFLOW_SIDE_EFFECTING


_EXCHANGE_TRANSFERS = {"scatter": N_DEV - 1, "gather1": 4, "gather2": 3}


def _exchange_views(mode, kinds, arrays):
    nw = len(kinds)
    gather = mode != "scatter"
    if gather:
        rows = [a.shape[0] // N_DEV for a in arrays[:nw]]
    else:
        rows = [a.shape[1] // N_DEV for a in arrays[:nw]]

    def gather_slot(ref, w, who):
        return ref.at[who] if kinds[w] == "col" else ref.at[pl.ds(who * rows[w], rows[w]), :]

    x, y, c = lax.axis_index("x"), lax.axis_index("y"), lax.axis_index("c")
    me = 4 * x + 2 * y + c
    chips = [(1 - x, y), (x, 1 - y), (1 - x, 1 - y)]

    def index(px, py, pc):
        return 4 * px + 2 * py + pc

    if mode == "scatter":
        plan = [(_peer((me + off) % N_DEV), (me + off) % N_DEV, (me + N_DEV - off) % N_DEV)
                for off in range(1, N_DEV)]
    elif mode == "gather1":
        plan = [((x, y, 1 - c), me, index(x, y, 1 - c))] + [((px, py, c), me, index(px, py, c)) for px, py in chips]
    else:
        plan = [((x, y, 1 - c), index(px, py, c), index(px, py, 1 - c)) for px, py in chips]

    def src(ref, w, j):
        sent = plan[j][1]
        if gather:
            return gather_slot(ref, w, sent)
        return ref.at[sent] if kinds[w] == "col" else ref.at[0, pl.ds(sent * rows[w], rows[w]), :]

    def dst(ref, w, j):
        return gather_slot(ref, w, plan[j][1]) if gather else ref.at[me]

    def arrival(ref, w, j):
        return gather_slot(ref, w, plan[j][2]) if gather else ref.at[plan[j][2]]

    return [p[0] for p in plan], src, dst, arrival


def _place_cast(w, layer, kind, *, name):
    _, R, C = w.shape
    tr = _tile(R, (256, 128, 64, 32, 16))

    def body(w_ref, o_ref):
        o_ref[...] = w_ref[...].astype(BF16)

    if kind == "col":
        out_shape = jax.ShapeDtypeStruct((N_DEV, R, C), BF16)
        out_spec = pl.BlockSpec((None, tr, C), lambda t: (_my_index(), t, 0))
    else:
        out_shape = jax.ShapeDtypeStruct((N_DEV * R, C), BF16)
        out_spec = pl.BlockSpec((tr, C), lambda t: (_my_index() * (R // tr) + t, 0))
    return pl.pallas_call(
        body, name=name, grid=(R // tr,), in_specs=[pl.BlockSpec((None, tr, C), lambda t: (layer, t, 0))],
        out_specs=out_spec, out_shape=out_shape, compiler_params=_params(("parallel",)),
    )(w)


def _exchange_start(mode, srcs, lands, kinds, after, *, name):
    nw = len(lands)
    ns = len(srcs)
    arrays = list(srcs) + list(lands)
    na = len(arrays)
    nx = _EXCHANGE_TRANSFERS[mode]

    def body(*refs):
        l_refs = refs[ns:ns + nw]
        s_refs = refs[:ns] if ns else l_refs
        send_sems, recv_sems = refs[ns + nw + 1], refs[ns + nw + 2]
        token = refs[-1]
        peers, src, dst, _ = _exchange_views(mode, kinds, arrays)
        for j in range(nx):
            for w in range(nw):
                pltpu.make_async_remote_copy(src(s_refs[w], w, j), dst(l_refs[w], w, j),
                                             send_sems.at[w * nx + j], recv_sems.at[w * nx + j],
                                             device_id=peers[j], device_id_type=MESH).start()
        token[...] = jnp.zeros_like(token)

    outs = pl.pallas_call(
        body, name=name,
        out_shape=(pltpu.SemaphoreType.DMA((nw * nx,)), pltpu.SemaphoreType.DMA((nw * nx,)),
                   *[pltpu.HBM(a.shape, a.dtype) for a in arrays], jax.ShapeDtypeStruct((8, 128), F32)),
        in_specs=[HBM_SPEC] * na + [ANY_SPEC],
        out_specs=(SEM_SPEC, SEM_SPEC, *([HBM_SPEC] * na), pl.BlockSpec(memory_space=pltpu.VMEM)),
        input_output_aliases={k: 2 + k for k in range(na)},
        compiler_params=pltpu.CompilerParams(has_side_effects=DATAFLOW),
    )(*[pltpu.with_memory_space_constraint(a, pltpu.HBM) for a in arrays], after)
    return outs[0], outs[1], outs[2:2 + ns], outs[2 + ns:2 + na], outs[-1]


def _exchange_wait(mode, started, kinds, after, *, name):
    send_sems, recv_sems, srcs, lands, _ = started
    nw = len(lands)
    ns = len(srcs)
    arrays = list(srcs) + list(lands)
    na = len(arrays)
    nx = _EXCHANGE_TRANSFERS[mode]

    def body(*refs):
        l_refs = refs[ns:na]
        s_refs = refs[:ns] if ns else l_refs
        send_ref, recv_ref = refs[na], refs[na + 1]
        peers, src, _, arrival = _exchange_views(mode, kinds, arrays)
        for j in range(nx):
            for w in range(nw):
                cp = pltpu.make_async_remote_copy(src(s_refs[w], w, j), arrival(l_refs[w], w, j),
                                                  send_ref.at[w * nx + j], recv_ref.at[w * nx + j],
                                                  device_id=peers[j], device_id_type=MESH)
                cp.wait_send()
                cp.wait_recv()

    outs = pl.pallas_call(
        body, name=name, out_shape=[pltpu.HBM(a.shape, a.dtype) for a in arrays],
        in_specs=[HBM_SPEC] * na + [SEM_SPEC, SEM_SPEC, ANY_SPEC], out_specs=[HBM_SPEC] * na,
        input_output_aliases={k: k for k in range(na)},
        compiler_params=pltpu.CompilerParams(has_side_effects=DATAFLOW),
    )(*arrays, send_sems, recv_sems, after)
    return outs[:ns], outs[ns:]


def _all_reduce_small(x):
    R = x.shape[0]

    def body(x_ref, o_ref, gath, send_sems, recv_sems):
        me = _my_index()
        gath[me] = x_ref[...]
        sends = []
        for off in range(1, N_DEV):
            to = (me + off) % N_DEV
            cp = pltpu.make_async_remote_copy(x_ref, gath.at[me], send_sems.at[off], recv_sems.at[off],
                                              device_id=_peer(to), device_id_type=MESH)
            cp.start()
            sends.append(cp)
        for off in range(1, N_DEV):
            frm = (me + N_DEV - off) % N_DEV
            pltpu.make_async_remote_copy(x_ref, gath.at[frm], send_sems.at[off], recv_sems.at[off],
                                         device_id=_peer(frm), device_id_type=MESH).wait_recv()
        for cp in sends:
            cp.wait_send()
        acc = gath[0]
        for s in range(1, N_DEV):
            acc = acc + gath[s]
        o_ref[...] = acc

    vm = pl.BlockSpec(memory_space=pltpu.VMEM)
    return pl.pallas_call(
        body, name="all_reduce_small", in_specs=[vm], out_specs=vm, out_shape=jax.ShapeDtypeStruct((R, 128), F32),
        scratch_shapes=[pltpu.VMEM((N_DEV, R, 128), F32), pltpu.SemaphoreType.DMA((N_DEV,)),
                        pltpu.SemaphoreType.DMA((N_DEV,))],
        compiler_params=pltpu.CompilerParams(has_side_effects=True),
    )(x)


def _adamw_math(w, g, m, v):
    m = ADAM_B1 * m + (1.0 - ADAM_B1) * g
    v = ADAM_B2 * v + (1.0 - ADAM_B2) * (g * g)
    m_hat = m / (1.0 - ADAM_B1 ** ADAM_STEP)
    v_hat = v / (1.0 - ADAM_B2 ** ADAM_STEP)
    delta = -ADAM_LR * (m_hat / (jnp.sqrt(v_hat) + ADAM_EPS) + ADAM_WD * w)
    return delta, m, v


def _adamw_layer(recv, own, kind, w, m, v, outs, layer, dep, *, name):
    nl, R, C = w.shape
    tr = _tile(R, (128, 64, 32, 16))

    def body(r_ref, o_ref, w_ref, m_ref, v_ref, _0, _1, _2, _3, _dep, g_out, d_out, m_out, v_out, token):
        token[...] = jnp.zeros_like(token)
        me = _my_index()
        mine = o_ref[...].astype(F32)
        g = jnp.where(me == 0, mine, r_ref[0].astype(F32))
        for s in range(1, N_DEV):
            g = g + jnp.where(me == s, mine, r_ref[s].astype(F32))
        delta, mn, vn = _adamw_math(w_ref[...], g, m_ref[...], v_ref[...])
        g_out[...] = g
        d_out[...] = delta
        m_out[...] = mn
        v_out[...] = vn

    if kind == "col":
        own_spec = pl.BlockSpec((None, tr, C), lambda t: (_my_index(), t, 0))
    else:
        own_spec = pl.BlockSpec((None, tr, C), lambda t: (0, _my_index() * (R // tr) + t, 0))
    wspec = pl.BlockSpec((None, tr, C), lambda t: (layer, t, 0))
    res = pl.pallas_call(
        body, name=name, grid=(R // tr,),
        in_specs=[pl.BlockSpec((N_DEV, tr, C), lambda t: (0, t, 0)), own_spec] + [wspec] * 3 + [ANY_SPEC] * 5,
        out_specs=[wspec] * 4 + [pl.BlockSpec((8, 128), lambda t: (0, 0))],
        out_shape=[jax.ShapeDtypeStruct((nl, R, C), F32)] * 4 + [jax.ShapeDtypeStruct((8, 128), F32)],
        input_output_aliases={5: 0, 6: 1, 7: 2, 8: 3},
        compiler_params=_params(("arbitrary",)),
    )(recv, own, w, m, v, *outs, dep)
    return res[:4], res[4]


def _adamw_small(g, w, m, v):
    def body(g_ref, w_ref, m_ref, v_ref, d_out, m_out, v_out):
        delta, mn, vn = _adamw_math(w_ref[...], g_ref[...], m_ref[...], v_ref[...])
        d_out[...] = delta
        m_out[...] = mn
        v_out[...] = vn

    return pl.pallas_call(body, name="adamw_small", out_shape=[jax.ShapeDtypeStruct(g.shape, F32)] * 3)(g, w, m, v)


def _pack(arrays, rows):
    flat = jnp.concatenate([a.reshape(-1) for a in arrays])
    return jnp.pad(flat, (0, rows * 128 - flat.shape[0])).reshape(rows, 128)


def _unpack(packed, shapes):
    flat = packed.reshape(-1)
    out, pos = [], 0
    for s in shapes:
        size = int(np.prod(s))
        out.append(flat[pos:pos + size].reshape(s))
        pos += size
    return out


def kernel(x, norm1_g, w_in, qk_norm_g, sink_a, rpb_c, w_br_a, w_br_b, w_br_c, w_o, norm2_g, w_gate_up, w_down, loss_target, m_norm1_g, m_w_in, m_qk_norm_g, m_sink_a, m_rpb_c, m_w_br_a, m_w_br_b, m_w_br_c, m_w_o, m_norm2_g, m_w_gate_up, m_w_down, v_norm1_g, v_w_in, v_qk_norm_g, v_sink_a, v_rpb_c, v_w_br_a, v_w_br_b, v_w_br_c, v_w_o, v_norm2_g, v_w_gate_up, v_w_down):
    nl = w_in.shape[0]
    L, D = x.shape[1], x.shape[2]
    x0 = x.reshape(L, D)
    tgt = loss_target.reshape(L, D)

    big = [w_in, w_br_a, w_br_b, w_br_c, w_o, w_gate_up, w_down]
    kinds = ["col", "col", "col", "col", "row", "col", "row"]

    big_names = ["w_in", "w_br_a", "w_br_b", "w_br_c", "w_o", "w_gate_up", "w_down"]
    ALL = list(range(len(big)))
    REST = ALL[1:]

    def gather_place(i):
        return [_place_cast(w, i, k, name="gather_place_" + n) for w, k, n in zip(big, kinds, big_names)]

    def gather_start(mode, lands, sub, after, tag):
        return _exchange_start(mode, [], lands, [kinds[j] for j in sub], after, name=mode + "_start" + tag)

    def gather_wait(mode, started, sub, after, tag):
        return _exchange_wait(mode, started, [kinds[j] for j in sub], after, name=mode + "_wait" + tag)[1]

    def matmul_views(lands, sub):
        return [g.reshape((N_DEV, 1) + g.shape[1:]) if kinds[j] == "col" else g.reshape((1, 1) + g.shape)
                for g, j in zip(lands, sub)]

    half = HEAD_DIM // 2
    inv_freq = ROPE_THETA ** (-jnp.arange(half, dtype=F32) * 2.0 / HEAD_DIM)
    ang = jnp.arange(L, dtype=F32)[:, None] * inv_freq[None, :]
    cos = jnp.concatenate([jnp.cos(ang), jnp.cos(ang)], axis=-1)
    sin = jnp.concatenate([-jnp.sin(ang), jnp.sin(ang)], axis=-1)
    expand = jnp.asarray(_c_expand_matrix(), BF16)
    expand_t = jnp.asarray(_c_expand_matrix().T, BF16)

    def gain_tables(i):
        g = qk_norm_g[i]
        gq = jnp.concatenate([jnp.tile(g[0][None], (8, 1)), jnp.tile(g[2][None], (12, 1)), jnp.tile(g[4][None], (8, 1))])
        gk = jnp.concatenate([jnp.tile(g[1][None], (2, 1)), jnp.tile(g[3][None], (12, 1)), jnp.tile(g[5][None], (8, 1))])
        return (gq.reshape(NQ_CHUNKS // Q_CG, 1, Q_CG * HEAD_DIM), gk.reshape(NK_CHUNKS // K_CG, 1, K_CG * HEAD_DIM))

    def bias_table(i):
        rp = jnp.pad(rpb_c[i].reshape(C_HEADS * C_NREL, C_NCOL), ((0, 0), (0, HEAD_DIM - C_NCOL)))
        t = _exact_mm(rp, expand_t, name="c_bias_expand")
        return _c_bias_tiles(t.reshape(C_HEADS, C_NREL, GRID_W, GRID_W))

    def sink_table(i):
        return jnp.broadcast_to(sink_a[i][:, None, None], (A_Q_HEADS, 1, HEAD_DIM))

    saved = []
    gws = [None] * nl
    xi = x0
    lands0 = gather_place(0)
    lvl1 = gather_start("gather1", lands0[:1], [0], x0, "_first")
    lvl2 = gather_start("gather2", gather_wait("gather1", lvl1, [0], x0, "_first"), [0], x0, "_first")
    gws[0] = matmul_views(gather_wait("gather2", lvl2, [0], x0, "_first"), [0])
    rest1 = gather_start("gather1", lands0[1:], REST, gws[0][0], "_rest")
    dep = rest1[4]
    for i in range(nl):
        qk_dep = None
        if i >= 1 and i + 1 < nl:
            nxt1 = gather_start("gather1", gather_place(i + 1), ALL, dep, "")
            dep = nxt1[4]
        gw_in = gws[i][0]
        gq, gk = gain_tables(i)
        bias_t = bias_table(i)
        sink = sink_table(i)
        h1 = _rms_fwd(xi, norm1_g[i][None], dep, name="rms1_fwd")
        proj = _mm_nn(h1, gw_in, 0, out_dtype=F32, name="proj_fwd")
        if i == 0:
            rest2 = gather_start("gather2", gather_wait("gather1", rest1, REST, proj, "_rest"), REST, proj, "_rest")
            qk_dep = rest2[4]
            if nl > 1:
                nxt1 = gather_start("gather1", gather_place(1), ALL, rest2[4], "")
                qk_dep = nxt1[4]
        qn = _qk_fwd(proj, gq, cos, sin, Q_PIECES, NQ_CHUNKS, Q_ROPE_UPTO, Q_CG, name="qnorm_fwd", dep=qk_dep)
        kn = _qk_fwd(proj, gk, cos, sin, K_PIECES, NK_CHUNKS, K_ROPE_UPTO, K_CG, name="knorm_fwd")
        oa, lse_a = _band_fwd(qn, kn, proj, dil=1, radius=A_RADIUS, nkv=A_KV_HEADS, group=A_GROUP,
                              q0=0, k0=0, v0=PC_VA, sink=sink, name="attn_a_fwd")
        obs, lbs = [], []
        for g, (window, dil) in enumerate(B_PATTERNS):
            o_g, l_g = _band_fwd(qn, kn, proj, dil=dil, radius=window // (2 * dil), nkv=B_HG, group=1,
                                 q0=8 + g * B_HG, k0=2 + g * B_HG, v0=PC_VB + g * B_HG, name=f"attn_b{g}_fwd")
            obs.append(o_g)
            lbs.append(l_g)
        ob, lse_b = _combine_b(obs, lbs, name="attn_b_combine")
        oc, lse_c = _c_fwd(qn, kn, proj, bias_t, name="attn_c_fwd")
        if i == 0:
            gws[0] = gws[0] + matmul_views(gather_wait("gather2", rest2, REST, oc, "_rest"), REST)
        gws[i][1:4] = [_blocks_to_wide(g, name="br_wide_" + n) for g, n in zip(gws[i][1:4], "abc")]
        _, gw_a, gw_b, gw_c, gw_o, gw_gu, gw_d = gws[i]
        ta = _mm_nn(oa, gw_a, 0, out_dtype=BF16, name="br_a_fwd")
        tb = _mm_nn(ob, gw_b, 0, out_dtype=BF16, name="br_b_fwd")
        tc = _mm_nn(oc, gw_c, 0, out_dtype=BF16, name="br_c_fwd")
        merged = _gate_fwd(proj, ta, tb, tc, name="gate_fwd")
        x1 = _mm_nn(merged, gw_o, 0, out_dtype=F32, name="wo_fwd", res=xi)
        dep = x1
        if i + 1 < nl:
            nxt2 = gather_start("gather2", gather_wait("gather1", nxt1, ALL, x1, ""), ALL, x1, "")
            dep = nxt2[4]
        h2 = _rms_fwd(x1, norm2_g[i][None], dep, name="rms2_fwd")
        gu = _mm_nn(h2, gw_gu, 0, out_dtype=BF16, name="gate_up_fwd")
        act = _swiglu_fwd(gu, name="swiglu_fwd")
        x2 = _mm_nn(act, gw_d, 0, out_dtype=F32, name="down_fwd", res=x1)
        saved.append(dict(x=xi, h1=h1, proj=proj, qn=qn, kn=kn, oa=oa, lse_a=lse_a, ob=ob, lse_b=lse_b, oc=oc,
                          lse_c=lse_c, ta=ta, tb=tb, tc=tc, merged=merged, x1=x1, h2=h2, gu=gu, act=act,
                          gq=gq, gk=gk, bias_t=bias_t, sink=sink))
        xi = x2
        dep = x2
        if i + 1 < nl:
            gws[i + 1] = matmul_views(gather_wait("gather2", nxt2, ALL, x2, ""), ALL)

    dx, dxb, loss_row = _loss(xi, tgt, name="loss")

    def scatter_start(grads, sub, after, tag):
        lands = []
        for g, j in zip(grads, sub):
            shape = g.shape if kinds[j] == "col" else (N_DEV, g.shape[1] // N_DEV, g.shape[2])
            lands.append(lax.empty(shape, BF16))
        return _exchange_start("scatter", grads, lands, [kinds[j] for j in sub], after, name="scatter_start" + tag)

    def scatter_wait(pair, after):
        own_a, recv_a = _exchange_wait("scatter", pair[0], [kinds[0]], after, name="scatter_wait_in")
        own_b, recv_b = _exchange_wait("scatter", pair[1], [kinds[j] for j in REST], after, name="scatter_wait_rest")
        return list(recv_a) + list(recv_b), list(own_a) + list(own_b)

    small_grads = [None] * nl
    recv = [None] * nl
    own = [None] * nl
    pending = None
    for i in reversed(range(nl)):
        s = saved[i]
        gw_in, gw_a, gw_b, gw_c, gw_o, gw_gu, gw_d = gws[i]
        dact = _mm_nt(dxb, gw_d, 0, out_dtype=BF16, name="down_bwd_x", dep=None if pending is None else pending[0][4])
        g_down = _mm_tn(s["act"], dxb, 1, name="down_bwd_w")
        dgu = _swiglu_bwd(s["gu"], dact, name="swiglu_bwd")
        g_gu = _mm_tn(s["h2"], dgu, N_DEV, name="gate_up_bwd_w")
        dh2 = _mm_nt(dgu, gw_gu, 0, out_dtype=F32, name="gate_up_bwd_x")
        dx1, dx1b, dg2 = _rms_bwd(s["x1"], norm2_g[i][None], dh2, dx, name="rms2_bwd")
        dmerged = _mm_nt(dx1b, gw_o, 0, out_dtype=F32, name="wo_bwd_x")
        g_o = _mm_tn(s["merged"], dx1b, 1, name="wo_bwd_w")
        dta, dtb, dtc, dproj = _gate_bwd(s["proj"], s["ta"], s["tb"], s["tc"], dmerged, name="gate_bwd")
        g_a = _wide_to_blocks(_mm_tn(s["oa"], dta, 1, name="br_a_bwd_w"), N_DEV, name="br_blocks_a")
        g_b = _wide_to_blocks(_mm_tn(s["ob"], dtb, 1, name="br_b_bwd_w"), N_DEV, name="br_blocks_b")
        g_c = _wide_to_blocks(_mm_tn(s["oc"], dtc, 1, name="br_c_bwd_w"), N_DEV, name="br_blocks_c")
        rest = scatter_start([g_a, g_b, g_c, g_o, g_gu, g_down], REST, g_c, "_rest")
        doa = _mm_nt(dta, gw_a, 0, out_dtype=F32, name="br_a_bwd_x", dep=rest[4])
        dob = _mm_nt(dtb, gw_b, 0, out_dtype=F32, name="br_b_bwd_x")
        doc = _mm_nt(dtc, gw_c, 0, out_dtype=F32, name="br_c_bwd_x")
        dq_buf = lax.empty((L, NQ_CHUNKS * HEAD_DIM), F32)
        dk_buf = lax.empty((L, NK_CHUNKS * HEAD_DIM), F32)
        dv_buf = lax.empty((L, NK_CHUNKS * HEAD_DIM), F32)
        dq_buf, dk_buf, dv_buf, dsink = _band_bwd(
            s["qn"], s["kn"], s["proj"], doa, s["oa"], s["lse_a"], dq_buf, dk_buf, dv_buf, dil=1, radius=A_RADIUS,
            nkv=A_KV_HEADS, group=A_GROUP, q0=0, k0=0, v0=PC_VA, o0=0, sink=s["sink"], name="attn_a_bwd")
        for g, (window, dil) in enumerate(B_PATTERNS):
            dq_buf, dk_buf, dv_buf = _band_bwd(
                s["qn"], s["kn"], s["proj"], dob, s["ob"], s["lse_b"], dq_buf, dk_buf, dv_buf, dil=dil,
                radius=window // (2 * dil), nkv=B_HG, group=1, q0=8 + g * B_HG, k0=2 + g * B_HG,
                v0=PC_VB + g * B_HG, o0=0, name=f"attn_b{g}_bwd")
        dq_buf, dk_buf, dv_buf, dbias_t = _c_bwd(s["qn"], s["kn"], s["proj"], s["bias_t"], doc, s["oc"], s["lse_c"],
                                                 dq_buf, dk_buf, dv_buf, name="attn_c_bwd")
        dproj, dgq = _qk_bwd(dq_buf, s["proj"], s["gq"], cos, sin, dproj, Q_PIECES, NQ_CHUNKS, Q_ROPE_UPTO, Q_CG,
                             name="qnorm_bwd")
        dproj, dgk = _qk_bwd(dk_buf, s["proj"], s["gk"], cos, sin, dproj, K_PIECES, NK_CHUNKS, K_ROPE_UPTO, K_CG,
                             name="knorm_bwd")
        dproj = _v_bwd(dv_buf, dproj, name="v_bwd")
        g_in = _mm_tn(s["h1"], dproj, N_DEV, name="proj_bwd_w")
        dh1 = _mm_nt(dproj, gw_in, 0, out_dtype=F32, name="proj_bwd_x")
        dx, dxb, dg1 = _rms_bwd(s["x"], norm1_g[i][None], dh1, dx1, name="rms1_bwd")
        if pending is not None:
            recv[i + 1], own[i + 1] = scatter_wait(pending, dx)
        pending = (scatter_start([g_in], [0], dx, "_in"), rest)

        drpb = _exact_mm(dbias_t.reshape(C_HEADS * C_NREL, GRID_W * GRID_W), expand, name="c_bias_reduce")
        dgq, dgk = dgq.reshape(NQ_CHUNKS, HEAD_DIM), dgk.reshape(NK_CHUNKS, HEAD_DIM)
        dqk_g = jnp.stack([dgq[0:8].sum(0), dgk[0:2].sum(0), dgq[8:20].sum(0), dgk[2:14].sum(0),
                           dgq[20:28].sum(0), dgk[14:22].sum(0)])
        small_grads[i] = (dg1.reshape(D), dqk_g, dsink[:, 0, 0],
                          drpb[:, :C_NCOL].reshape(C_HEADS, C_NREL, C_NCOL), dg2.reshape(D))

    small_names = [norm1_g, qk_norm_g, sink_a, rpb_c, norm2_g]
    small_m = [m_norm1_g, m_qk_norm_g, m_sink_a, m_rpb_c, m_norm2_g]
    small_v = [v_norm1_g, v_qk_norm_g, v_sink_a, v_rpb_c, v_norm2_g]
    shapes = [a.shape for a in small_names]
    total = sum(int(np.prod(sh)) for sh in shapes) + 128
    rows = -(-total // 1024) * 8
    stacked = [jnp.stack([small_grads[i][j] for i in range(nl)]) for j in range(5)]
    packed = _pack([loss_row.reshape(-1)] + stacked, rows)
    summed = _all_reduce_small(packed)
    loss = summed[0, 0]
    zero_row = jnp.zeros((128,), F32)
    d_s, m_s, v_s = _adamw_small(summed, _pack([zero_row] + small_names, rows), _pack([zero_row] + small_m, rows),
                                 _pack([zero_row] + small_v, rows))
    shapes1 = [(128,)] + shapes
    g_small = _unpack(summed, shapes1)[1:]
    d_small = _unpack(d_s, shapes1)[1:]
    m_small = _unpack(m_s, shapes1)[1:]
    v_small = _unpack(v_s, shapes1)[1:]

    big_m = [m_w_in, m_w_br_a, m_w_br_b, m_w_br_c, m_w_o, m_w_gate_up, m_w_down]
    big_v = [v_w_in, v_w_br_a, v_w_br_b, v_w_br_c, v_w_o, v_w_gate_up, v_w_down]
    big_out = [[lax.empty(w.shape, F32) for _ in range(4)] for w in big]
    token = pending[0][4]
    for i in list(range(nl - 1, 0, -1)) + [0]:
        if i == 0:
            recv[0], own[0] = scatter_wait(pending, token)
        for j in range(len(big)):
            big_out[j], token = _adamw_layer(recv[i][j], own[i][j], kinds[j], big[j], big_m[j], big_v[j],
                                             big_out[j], i, token, name="adamw_" + big_names[j])

    order = ["norm1_g", "w_in", "qk_norm_g", "sink_a", "rpb_c", "w_br_a", "w_br_b", "w_br_c", "w_o", "norm2_g",
             "w_gate_up", "w_down"]
    small_idx = {"norm1_g": 0, "qk_norm_g": 1, "sink_a": 2, "rpb_c": 3, "norm2_g": 4}
    big_idx = {n: j for j, n in enumerate(big_names)}

    def pick(kind):
        out = []
        for n in order:
            if n in small_idx:
                out.append([g_small, d_small, m_small, v_small][kind][small_idx[n]])
            else:
                out.append(big_out[big_idx[n]][kind])
        return out

    return (loss, dx.reshape(1, L, D), *pick(0), *pick(1), *pick(2), *pick(3))
```

```python
import functools
import math

import numpy as np
import jax
import jax.numpy as jnp
from jax import lax
from jax.experimental import pallas as pl
from jax.experimental.pallas import tpu as pltpu

F32 = jnp.float32
BF16 = jnp.bfloat16
MESH = pl.DeviceIdType.MESH
N_DEV = 8

HEAD_DIM = 128
NORM_EPS = 1e-6
ROPE_THETA = 10000.0
ATT_SCALE = HEAD_DIM ** -0.5
NEG = -1e30

A_Q_HEADS, A_KV_HEADS, A_RADIUS = 8, 2, 128
A_GROUP = A_Q_HEADS // A_KV_HEADS
B_PATTERNS = ((128, 1), (512, 4), (2048, 16))
B_HG = 4
B_HEADS = len(B_PATTERNS) * B_HG
C_HEADS, GRID_W, C_WIN_ROWS, C_WIN_COLS = 8, 64, 8, 16
C_NREL = 2 * C_WIN_ROWS - 1
C_NCOL = 2 * C_WIN_COLS - 1

PC_QA, PC_KA, PC_VA = 0, 8, 10
PC_QB, PC_KB, PC_VB = 12, 24, 36
PC_QC, PC_KC, PC_VC = 48, 56, 64
N_QKV_CHUNKS = 72
Q_PIECES = ((0, 8, PC_QA), (8, 12, PC_QB), (20, 8, PC_QC))
K_PIECES = ((0, 2, PC_KA), (2, 12, PC_KB), (14, 8, PC_KC))
V_PIECES = ((0, 2, PC_VA), (2, 12, PC_VB), (14, 8, PC_VC))
NQ_CHUNKS, NK_CHUNKS = 28, 22
Q_ROPE_UPTO, K_ROPE_UPTO = 20, 14
Q_CG, K_CG = 4, 2

ADAM_LR, ADAM_B1, ADAM_B2, ADAM_EPS, ADAM_WD, ADAM_STEP = 0.001, 0.9, 0.999, 1e-08, 0.01, 10

VMEM_LIMIT = 48 * 1024 * 1024


def _tile(dim, prefs):
    for p in prefs:
        if dim % p == 0:
            return p
    return dim


NN_WEIGHT_TILE_BYTES = 8 * 1024 * 1024
NT_WEIGHT_TILE_BYTES = 4 * 1024 * 1024
TN_ACC_BYTES = 6 * 1024 * 1024
MAX_COL_TILE = 2048


def _col_tile(ns):
    return ns if ns <= MAX_COL_TILE else _tile(ns, (MAX_COL_TILE, 1024, 512, 256, 128))


def _params(sem, **kw):
    return pltpu.CompilerParams(dimension_semantics=sem, vmem_limit_bytes=VMEM_LIMIT, **kw)


def _piece_map(pieces):
    def f(c):
        out = c - pieces[0][0] + pieces[0][2]
        for first, _, pfirst in pieces[1:]:
            out = jnp.where(c >= first, c - first + pfirst, out)
        return out
    return f


def _mm_nn(a, w, layer, *, out_dtype, name, res=None):
    M, K = a.shape
    nb, _, Kw, ns = w.shape
    assert Kw == K
    tn = _col_tile(ns)
    tm = _tile(M, (1024, 512, 256) if tn <= 512 else (512, 256))
    tk = _tile(K, tuple(t for t in (2048, 1408, 1024, 512, 256) if t * tn * 2 <= NN_WEIGHT_TILE_BYTES))
    nj, nk = ns // tn, K // tk

    def body(*refs):
        a_ref, w_ref = refs[:2]
        r_ref = None if res is None else refs[2]
        o_ref = refs[2 if res is None else 3]
        part = jnp.dot(a_ref[...].astype(BF16), w_ref[...], preferred_element_type=F32)
        if nk == 1:
            if r_ref is not None:
                part = part + r_ref[...]
            o_ref[...] = part.astype(out_dtype)
            return
        acc_ref = refs[-1]
        k = pl.program_id(3)

        @pl.when(k == 0)
        def _():
            acc_ref[...] = part

        @pl.when(k > 0)
        def _():
            acc_ref[...] += part

        @pl.when(k == nk - 1)
        def _():
            r = acc_ref[...]
            if r_ref is not None:
                r = r + r_ref[...]
            o_ref[...] = r.astype(out_dtype)

    in_specs = [pl.BlockSpec((tm, tk), lambda i, b, j, k: (i, k)),
                pl.BlockSpec((None, None, tk, tn), lambda i, b, j, k: (b, layer, k, j))]
    args = [a, w]
    if res is not None:
        in_specs.append(pl.BlockSpec((tm, tn), lambda i, b, j, k: (i, b * nj + j)))
        args.append(res)
    return pl.pallas_call(
        body, name=name, grid=(M // tm, nb, nj, nk), in_specs=in_specs,
        out_specs=pl.BlockSpec((tm, tn), lambda i, b, j, k: (i, b * nj + j)),
        out_shape=jax.ShapeDtypeStruct((M, nb * ns), out_dtype),
        scratch_shapes=[] if nk == 1 else [pltpu.VMEM((tm, tn), F32)],
        compiler_params=_params(("parallel", "parallel", "parallel", "arbitrary")),
    )(*args)


def _norm_mm(x, g, w, dep, *, out_dtype, name):
    M, K = x.shape
    nb, _, Kw, ns = w.shape
    assert Kw == K
    tn = _col_tile(ns)
    tm = _tile(M, (512, 256))
    nj = ns // tn

    def body(x_ref, g_ref, w_ref, _dep, o_ref, h_ref):
        xv = x_ref[...]
        rstd = lax.rsqrt(jnp.mean(xv * xv, axis=-1, keepdims=True) + NORM_EPS)
        h = (xv * rstd * g_ref[...]).astype(BF16)
        h_ref[...] = h
        o_ref[...] = jnp.dot(h, w_ref[...], preferred_element_type=F32).astype(out_dtype)

    return pl.pallas_call(
        body, name=name, grid=(M // tm, nb, nj),
        in_specs=[pl.BlockSpec((tm, K), lambda i, b, j: (i, 0)), pl.BlockSpec((1, K), lambda i, b, j: (0, 0)),
                  pl.BlockSpec((None, None, K, tn), lambda i, b, j: (b, 0, 0, j)), ANY_SPEC],
        out_specs=[pl.BlockSpec((tm, tn), lambda i, b, j: (i, b * nj + j)),
                   pl.BlockSpec((tm, K), lambda i, b, j: (i, 0))],
        out_shape=[jax.ShapeDtypeStruct((M, nb * ns), out_dtype), jax.ShapeDtypeStruct((M, K), BF16)],
        compiler_params=_params(("parallel", "arbitrary", "arbitrary")),
    )(x, g, w, dep)


def _mm_nt(a, w, layer, *, out_dtype, name, dep=None):
    M, N = a.shape
    nb, _, K, ns = w.shape
    assert N == nb * ns
    tm = _tile(M, (1024, 512, 256))
    tn = _col_tile(ns)
    tk = _tile(K, tuple(t for t in (1024, 512, 256) if t * tn * 2 <= NT_WEIGHT_TILE_BYTES))
    nj = ns // tn
    nred = nb * nj

    def body(*refs):
        a_ref, w_ref = refs[:2]
        o_ref, acc_ref = refs[-2:]
        s = pl.program_id(2) * nj + pl.program_id(3)
        part = lax.dot_general(a_ref[...].astype(BF16), w_ref[...], (((1,), (1,)), ((), ())),
                               preferred_element_type=F32)

        @pl.when(s == 0)
        def _():
            acc_ref[...] = part

        @pl.when(s > 0)
        def _():
            acc_ref[...] += part

        @pl.when(s == nred - 1)
        def _():
            o_ref[...] = acc_ref[...].astype(out_dtype)

    in_specs = [pl.BlockSpec((tm, tn), lambda i, kk, b, j: (i, b * nj + j)),
                pl.BlockSpec((None, None, tk, tn), lambda i, kk, b, j: (b, layer, kk, j))]
    args = [a, w]
    if dep is not None:
        in_specs.append(ANY_SPEC)
        args.append(dep)
    return pl.pallas_call(
        body, name=name, grid=(M // tm, K // tk, nb, nj), in_specs=in_specs,
        out_specs=pl.BlockSpec((tm, tk), lambda i, kk, b, j: (i, kk)),
        out_shape=jax.ShapeDtypeStruct((M, K), out_dtype),
        scratch_shapes=[pltpu.VMEM((tm, tk), F32)],
        compiler_params=_params(("parallel", "parallel", "arbitrary", "arbitrary")),
    )(*args)


def _mm_tn(a, g, nb, *, name):
    M, Ka = a.shape
    N = g.shape[1]
    ns = N // nb
    tn = _col_tile(ns)
    tka = _tile(Ka, tuple(t for t in (1024, 512, 256) if t * tn * 4 <= TN_ACC_BYTES))
    tm = _tile(M, (2048, 1024, 512, 256))
    nj, nm = ns // tn, M // tm

    def body(a_ref, g_ref, o_ref, acc_ref):
        m = pl.program_id(3)
        part = lax.dot_general(a_ref[...].astype(BF16), g_ref[...].astype(BF16), (((0,), (0,)), ((), ())),
                               preferred_element_type=F32)

        @pl.when(m == 0)
        def _():
            acc_ref[...] = part

        @pl.when(m > 0)
        def _():
            acc_ref[...] += part

        @pl.when(m == nm - 1)
        def _():
            o_ref[...] = acc_ref[...].astype(BF16)

    return pl.pallas_call(
        body, name=name, grid=(Ka // tka, nb, nj, nm),
        in_specs=[pl.BlockSpec((tm, tka), lambda ka, b, j, m: (m, ka)),
                  pl.BlockSpec((tm, tn), lambda ka, b, j, m: (m, b * nj + j))],
        out_specs=pl.BlockSpec((None, tka, tn), lambda ka, b, j, m: (b, ka, j)),
        out_shape=jax.ShapeDtypeStruct((nb, Ka, ns), BF16),
        scratch_shapes=[pltpu.VMEM((tka, tn), F32)],
        compiler_params=_params(("parallel", "parallel", "parallel", "arbitrary")),
    )(a, g)


def _blocks_to_wide(w, *, name):
    nb, _, K, ns = w.shape

    def body(i_ref, o_ref):
        o_ref[...] = i_ref[...]

    return pl.pallas_call(
        body, name=name, grid=(nb,), in_specs=[pl.BlockSpec((None, None, K, ns), lambda b: (b, 0, 0, 0))],
        out_specs=pl.BlockSpec((None, None, K, ns), lambda b: (0, 0, 0, b)),
        out_shape=jax.ShapeDtypeStruct((1, 1, K, nb * ns), w.dtype), compiler_params=_params(("parallel",)),
    )(w)


def _wide_to_blocks(g, nb, *, name):
    _, K, N = g.shape
    ns = N // nb

    def body(i_ref, o_ref):
        o_ref[...] = i_ref[...]

    return pl.pallas_call(
        body, name=name, grid=(nb,), in_specs=[pl.BlockSpec((None, K, ns), lambda b: (0, 0, b))],
        out_specs=pl.BlockSpec((None, K, ns), lambda b: (b, 0, 0)),
        out_shape=jax.ShapeDtypeStruct((nb, K, ns), g.dtype), compiler_params=_params(("parallel",)),
    )(g)


def _exact_mm(a, e, *, name):
    R, K = a.shape
    N = e.shape[1]

    def body(a_ref, e_ref, o_ref):
        x = a_ref[...]
        hi = x.astype(BF16)
        r1 = x - hi.astype(F32)
        mid = r1.astype(BF16)
        lo = (r1 - mid.astype(F32)).astype(BF16)
        ev = e_ref[...]
        o_ref[...] = (jnp.dot(hi, ev, preferred_element_type=F32) + jnp.dot(mid, ev, preferred_element_type=F32)
                      + jnp.dot(lo, ev, preferred_element_type=F32))

    return pl.pallas_call(body, name=name, out_shape=jax.ShapeDtypeStruct((R, N), F32),
                          compiler_params=pltpu.CompilerParams(vmem_limit_bytes=VMEM_LIMIT))(a, e)


def _rms_fwd(x, g, dep, *, name):
    L, D = x.shape
    tl = _tile(L, (512, 256, 128))

    def body(x_ref, g_ref, _dep, h_ref):
        xv = x_ref[...]
        rstd = lax.rsqrt(jnp.mean(xv * xv, axis=-1, keepdims=True) + NORM_EPS)
        h_ref[...] = (xv * rstd * g_ref[...]).astype(BF16)

    return pl.pallas_call(
        body, name=name, grid=(L // tl,),
        in_specs=[pl.BlockSpec((tl, D), lambda t: (t, 0)), pl.BlockSpec((1, D), lambda t: (0, 0)), ANY_SPEC],
        out_specs=pl.BlockSpec((tl, D), lambda t: (t, 0)),
        out_shape=jax.ShapeDtypeStruct((L, D), BF16),
        compiler_params=_params(("parallel",)),
    )(x, g, dep)


def _rms_bwd(x, g, dy, dres, *, name):
    L, D = x.shape
    tl = _tile(L, (256, 128))

    def body(x_ref, g_ref, dy_ref, dres_ref, dx_ref, dxb_ref, dg_ref):
        t = pl.program_id(0)
        xv = x_ref[...]
        rstd = lax.rsqrt(jnp.mean(xv * xv, axis=-1, keepdims=True) + NORM_EPS)
        xhat = xv * rstd
        dyv = dy_ref[...]
        dxhat = dyv * g_ref[...]
        c = jnp.mean(dxhat * xhat, axis=-1, keepdims=True)
        dx = dres_ref[...] + rstd * (dxhat - xhat * c)
        dx_ref[...] = dx
        dxb_ref[...] = dx.astype(BF16)
        dgp = jnp.sum(dyv * xhat, axis=0, keepdims=True)

        @pl.when(t == 0)
        def _():
            dg_ref[...] = dgp

        @pl.when(t > 0)
        def _():
            dg_ref[...] += dgp

    row = pl.BlockSpec((tl, D), lambda t: (t, 0))
    vec = pl.BlockSpec((1, D), lambda t: (0, 0))
    return pl.pallas_call(
        body, name=name, grid=(L // tl,), in_specs=[row, vec, row, row], out_specs=[row, row, vec],
        out_shape=[jax.ShapeDtypeStruct((L, D), F32), jax.ShapeDtypeStruct((L, D), BF16),
                   jax.ShapeDtypeStruct((1, D), F32)],
        compiler_params=_params(("arbitrary",)),
    )(x, g, dy, dres)


def _gate_fwd(proj, ta, tb, tc, *, name):
    L, D = ta.shape
    tl, tcw = _tile(L, (512, 256, 128)), _tile(D, (1024, 512, 256, 128))
    off = N_QKV_CHUNKS * HEAD_DIM // tcw
    nd = D // tcw

    def body(g0, g1, g2, a_ref, b_ref, c_ref, o_ref):
        m = (jax.nn.sigmoid(g0[...]) * a_ref[...].astype(F32) + jax.nn.sigmoid(g1[...]) * b_ref[...].astype(F32)
             + jax.nn.sigmoid(g2[...]) * c_ref[...].astype(F32))
        o_ref[...] = m.astype(BF16)

    blk = pl.BlockSpec((tl, tcw), lambda t, j: (t, j))
    gl = [pl.BlockSpec((tl, tcw), functools.partial(lambda t, j, i: (t, off + i * nd + j), i=i)) for i in range(3)]
    return pl.pallas_call(
        body, name=name, grid=(L // tl, nd), in_specs=gl + [blk, blk, blk], out_specs=blk,
        out_shape=jax.ShapeDtypeStruct((L, D), BF16),
        compiler_params=_params(("parallel", "parallel")),
    )(proj, proj, proj, ta, tb, tc)


def _gate_bwd(proj, ta, tb, tc, dmerged, *, name):
    L, D = ta.shape
    ncols = proj.shape[1]
    tl, tcw = _tile(L, (512, 256, 128)), _tile(D, (1024, 512, 256, 128))
    off = N_QKV_CHUNKS * HEAD_DIM // tcw
    nd = D // tcw

    def body(g0, g1, g2, a_ref, b_ref, c_ref, dm_ref, da_ref, db_ref, dc_ref, dgl_ref):
        i = pl.program_id(2)
        sg = jax.nn.sigmoid(jnp.where(i == 0, g0[...], jnp.where(i == 1, g1[...], g2[...])))
        sel_t = jnp.where(i == 0, a_ref[...], jnp.where(i == 1, b_ref[...], c_ref[...])).astype(F32)
        dt = dm_ref[...] * sg
        dtb = dt.astype(BF16)

        @pl.when(i == 0)
        def _():
            da_ref[...] = dtb

        @pl.when(i == 1)
        def _():
            db_ref[...] = dtb

        @pl.when(i == 2)
        def _():
            dc_ref[...] = dtb

        dgl_ref[...] = (dt * sel_t * (1.0 - sg)).astype(BF16)

    blk = pl.BlockSpec((tl, tcw), lambda t, j, i: (t, j))
    gl = [pl.BlockSpec((tl, tcw), functools.partial(lambda t, j, i, q: (t, off + q * nd + j), q=q)) for q in range(3)]
    return pl.pallas_call(
        body, name=name, grid=(L // tl, nd, 3), in_specs=gl + [blk, blk, blk, blk],
        out_specs=[blk, blk, blk, pl.BlockSpec((tl, tcw), lambda t, j, i: (t, off + i * nd + j))],
        out_shape=[jax.ShapeDtypeStruct((L, D), BF16)] * 3 + [jax.ShapeDtypeStruct((L, ncols), BF16)],
        compiler_params=_params(("parallel", "parallel", "arbitrary")),
    )(proj, proj, proj, ta, tb, tc, dmerged)


def _swiglu_fwd(gu, *, name):
    L, F2 = gu.shape
    F = F2 // 2
    tl = _tile(L, (128, 64))

    def body(gu_ref, o_ref):
        gt = gu_ref[:, :F].astype(F32)
        o_ref[...] = (gt * jax.nn.sigmoid(gt) * gu_ref[:, F:].astype(F32)).astype(BF16)

    return pl.pallas_call(
        body, name=name, grid=(L // tl,), in_specs=[pl.BlockSpec((tl, F2), lambda t: (t, 0))],
        out_specs=pl.BlockSpec((tl, F), lambda t: (t, 0)),
        out_shape=jax.ShapeDtypeStruct((L, F), BF16),
        compiler_params=_params(("parallel",)),
    )(gu)


def _swiglu_bwd(gu, dact, *, name):
    L, F2 = gu.shape
    F = F2 // 2
    tl = _tile(L, (128, 64))

    def body(gu_ref, d_ref, o_ref):
        gt, up, d = gu_ref[:, :F].astype(F32), gu_ref[:, F:].astype(F32), d_ref[...].astype(F32)
        sg = jax.nn.sigmoid(gt)
        o_ref[:, :F] = (d * up * sg * (1.0 + gt * (1.0 - sg))).astype(BF16)
        o_ref[:, F:] = (d * gt * sg).astype(BF16)

    return pl.pallas_call(
        body, name=name, grid=(L // tl,),
        in_specs=[pl.BlockSpec((tl, F2), lambda t: (t, 0)), pl.BlockSpec((tl, F), lambda t: (t, 0))],
        out_specs=pl.BlockSpec((tl, F2), lambda t: (t, 0)),
        out_shape=jax.ShapeDtypeStruct((L, F2), BF16),
        compiler_params=_params(("parallel",)),
    )(gu, dact)


def _loss(y, tgt, *, name):
    L, D = y.shape
    tl = _tile(L, (256, 128))
    nt = L // tl

    def body(y_ref, t_ref, dy_ref, dyb_ref, loss_ref, acc_ref):
        t = pl.program_id(0)
        e = y_ref[...] - t_ref[...]
        dy = e * (1.0 / D)
        dy_ref[...] = dy
        dyb_ref[...] = dy.astype(BF16)
        part = jnp.sum(e * e, axis=0, keepdims=True)

        @pl.when(t == 0)
        def _():
            acc_ref[...] = part

        @pl.when(t > 0)
        def _():
            acc_ref[...] += part

        @pl.when(t == nt - 1)
        def _():
            loss_ref[...] = jnp.broadcast_to(jnp.sum(acc_ref[...], axis=-1, keepdims=True) * (0.5 / D), (1, 128))

    row = pl.BlockSpec((tl, D), lambda t: (t, 0))
    return pl.pallas_call(
        body, name=name, grid=(nt,), in_specs=[row, row],
        out_specs=[row, row, pl.BlockSpec((1, 128), lambda t: (0, 0))],
        out_shape=[jax.ShapeDtypeStruct((L, D), F32), jax.ShapeDtypeStruct((L, D), BF16),
                   jax.ShapeDtypeStruct((1, 128), F32)],
        scratch_shapes=[pltpu.VMEM((1, D), F32)],
        compiler_params=_params(("arbitrary",)),
    )(y, tgt)


def _rope(v, cos, sin_signed):
    return v * cos + pltpu.roll(v, HEAD_DIM // 2, 1) * sin_signed


def _head_mean(x, j=0):
    if j % 2:
        return jnp.mean(x, axis=-1, keepdims=True)
    hi = x.astype(BF16)
    lo = (x - hi.astype(F32)).astype(BF16)
    ones = jnp.ones((HEAD_DIM, HEAD_DIM), BF16)
    total = jnp.dot(hi, ones, preferred_element_type=F32) + jnp.dot(lo, ones, preferred_element_type=F32)
    return total * (1.0 / HEAD_DIM)


def _qk_fwd(proj, gtab, cos, sin, pieces, nchunks, rope_upto, cg, *, name, dep=None):
    L = proj.shape[0]
    tl = _tile(L, (512, 256, 128))
    W = cg * HEAD_DIM
    pmap = _piece_map(tuple((a // cg, n // cg, p // cg) for a, n, p in pieces))

    def body(*refs):
        p_ref, g_ref, cos_ref, sin_ref = refs[:4]
        o_ref = refs[-1]
        c = pl.program_id(1)

        def norm(j):
            cols = slice(j * HEAD_DIM, (j + 1) * HEAD_DIM)
            x = p_ref[:, cols]
            rstd = lax.rsqrt(_head_mean(x * x, j) + NORM_EPS)
            return cols, x * rstd * g_ref[:, cols]

        @pl.when(c < rope_upto // cg)
        def _():
            for j in range(cg):
                cols, y = norm(j)
                o_ref[:, cols] = _rope(y, cos_ref[...], sin_ref[...])

        @pl.when(c >= rope_upto // cg)
        def _():
            for j in range(cg):
                cols, y = norm(j)
                o_ref[:, cols] = y

    pos = pl.BlockSpec((tl, HEAD_DIM), lambda t, c: (t, 0))
    in_specs = [pl.BlockSpec((tl, W), lambda t, c: (t, pmap(c))),
                pl.BlockSpec((None, 1, W), lambda t, c: (c, 0, 0)), pos, pos]
    args = [proj, gtab, cos, sin]
    if dep is not None:
        in_specs.append(ANY_SPEC)
        args.append(dep)
    return pl.pallas_call(
        body, name=name, grid=(L // tl, nchunks // cg), in_specs=in_specs,
        out_specs=pl.BlockSpec((tl, W), lambda t, c: (t, c)),
        out_shape=jax.ShapeDtypeStruct((L, nchunks * HEAD_DIM), F32),
        compiler_params=_params(("parallel", "parallel")),
    )(*args)


def _qk_bwd(dqk, proj, gtab, cos, sin, dproj, pieces, nchunks, rope_upto, cg, *, name):
    L = proj.shape[0]
    tl = _tile(L, (512, 256, 128))
    W = cg * HEAD_DIM
    pmap = _piece_map(tuple((a // cg, n // cg, p // cg) for a, n, p in pieces))

    def body(d_ref, p_ref, g_ref, cos_ref, sin_ref, _, o_ref, dg_ref):
        c, t = pl.program_id(0), pl.program_id(1)

        @pl.when(t == 0)
        def _():
            dg_ref[...] = jnp.zeros_like(dg_ref)

        for j in range(cg):
            cols = slice(j * HEAD_DIM, (j + 1) * HEAD_DIM)
            x = p_ref[:, cols]
            rstd = lax.rsqrt(_head_mean(x * x, j) + NORM_EPS)
            xhat = x * rstd
            dy = d_ref[:, cols]
            dy = jnp.where(c < rope_upto // cg, _rope(dy, cos_ref[...], -sin_ref[...]), dy)
            dxhat = dy * g_ref[:, cols]
            cm = _head_mean(dxhat * xhat, j)
            o_ref[:, cols] = (rstd * (dxhat - xhat * cm)).astype(BF16)
            dg_ref[:, cols] += jnp.sum(dy * xhat, axis=0, keepdims=True)

    pos = pl.BlockSpec((tl, HEAD_DIM), lambda c, t: (t, 0))
    gspec = pl.BlockSpec((None, 1, W), lambda c, t: (c, 0, 0))
    out, dg = pl.pallas_call(
        body, name=name, grid=(nchunks // cg, L // tl),
        in_specs=[pl.BlockSpec((tl, W), lambda c, t: (t, c)),
                  pl.BlockSpec((tl, W), lambda c, t: (t, pmap(c))), gspec, pos, pos,
                  pl.BlockSpec(memory_space=pl.ANY)],
        out_specs=[pl.BlockSpec((tl, W), lambda c, t: (t, pmap(c))), gspec],
        out_shape=[jax.ShapeDtypeStruct(dproj.shape, BF16), jax.ShapeDtypeStruct((nchunks // cg, 1, W), F32)],
        input_output_aliases={5: 0},
        compiler_params=_params(("parallel", "arbitrary")),
    )(dqk, proj, gtab, cos, sin, dproj)
    return out, dg


def _v_bwd(dv, dproj, *, name):
    L = dv.shape[0]
    tl = _tile(L, (2048, 1024, 512, 256, 128))
    pmap = _piece_map(tuple((a // 2, n // 2, p // 2) for a, n, p in V_PIECES))

    def body(d_ref, _, o_ref):
        o_ref[...] = d_ref[...].astype(BF16)

    return pl.pallas_call(
        body, name=name, grid=(L // tl, NK_CHUNKS // 2),
        in_specs=[pl.BlockSpec((tl, 2 * HEAD_DIM), lambda t, c: (t, c)), pl.BlockSpec(memory_space=pl.ANY)],
        out_specs=pl.BlockSpec((tl, 2 * HEAD_DIM), lambda t, c: (t, pmap(c))),
        out_shape=jax.ShapeDtypeStruct(dproj.shape, BF16),
        input_output_aliases={1: 0},
        compiler_params=_params(("parallel", "parallel")),
    )(dv, dproj)


def _band_geometry(L, dil, radius):
    n = L // dil
    bq = min(256, max(n // 2, 64), n)
    width = min(bq + 2 * radius, n)
    nsub = _tile(n // bq, (8, 4, 2)) if dil == 1 else 1
    return n, bq, width, nsub


def _band_loop(dil, one):
    if dil == 1:
        one(0, 0)
    else:
        lax.fori_loop(0, dil, one, 0, unroll=min(dil, 4))


def _band_rows(dil, r, first, count):
    if dil == 1:
        return pl.ds(pl.multiple_of(first, 8), count)
    return pl.ds(r + first * dil, count, stride=dil)


def _band_mask(i, bq, width, radius, ws):
    qpos = i * bq + lax.broadcasted_iota(jnp.int32, (bq, width), 0)
    kpos = ws + lax.broadcasted_iota(jnp.int32, (bq, width), 1)
    return jnp.abs(kpos - qpos) <= radius


def _band_fwd(qn, kn, proj, *, dil, radius, nkv, group, q0, k0, v0, sink=None, name):
    L = qn.shape[0]
    n, bq, width, nsub = _band_geometry(L, dil, radius)
    tq = nsub * bq * dil
    nh = nkv * group

    def body(*refs):
        if sink is None:
            q_ref, k_ref, v_ref, o_ref, lse_ref = refs
        else:
            q_ref, k_ref, v_ref, s_ref, o_ref, lse_ref = refs
        for sb in range(nsub):
            block(sb, q_ref, k_ref, v_ref, None if sink is None else s_ref, o_ref, lse_ref)

    def block(sb, q_ref, k_ref, v_ref, s_ref, o_ref, lse_ref):
        i = pl.program_id(2) * nsub + sb
        ws = jnp.clip(i * bq - radius, 0, n - width)
        valid = _band_mask(i, bq, width, radius, ws)

        def one(r, carry):
            qrows = _band_rows(dil, r, sb * bq, bq)
            krows = _band_rows(dil, r, ws, width)
            q = q_ref[qrows, :].astype(BF16)
            k = k_ref[krows, :].astype(BF16)
            v = v_ref[krows, :].astype(BF16)
            s = lax.dot_general(q, k, (((1,), (1,)), ((), ())), preferred_element_type=F32) * ATT_SCALE
            s = jnp.where(valid, s, NEG)
            m = jnp.max(s, axis=-1, keepdims=True)
            if sink is not None:
                m = jnp.maximum(m, s_ref[...][:, :1])
            p = jnp.exp(s - m)
            denom = jnp.sum(p, axis=-1, keepdims=True)
            if sink is not None:
                denom = denom + jnp.exp(s_ref[...][:, :1] - m)
            pn = (p / denom).astype(BF16)
            o_ref[qrows, :] = jnp.dot(pn, v, preferred_element_type=F32)
            lse_ref[qrows, :] = jnp.broadcast_to(m + jnp.log(denom), (bq, HEAD_DIM))
            return carry

        _band_loop(dil, one)

    qspec = pl.BlockSpec((tq, HEAD_DIM), lambda hk, g, i: (i, q0 + hk * group + g))
    in_specs = [qspec,
                pl.BlockSpec((L, HEAD_DIM), lambda hk, g, i: (0, k0 + hk)),
                pl.BlockSpec((L, HEAD_DIM), lambda hk, g, i: (0, v0 + hk))]
    args = [qn, kn, proj]
    if sink is not None:
        in_specs.append(pl.BlockSpec((None, 1, HEAD_DIM), lambda hk, g, i: (hk * group + g, 0, 0)))
        args.append(sink)
    ospec = pl.BlockSpec((tq, HEAD_DIM), lambda hk, g, i: (i, hk * group + g))
    return pl.pallas_call(
        body, name=name, grid=(nkv, group, n // (bq * nsub)), in_specs=in_specs, out_specs=[ospec, ospec],
        out_shape=[jax.ShapeDtypeStruct((L, nh * HEAD_DIM), F32)] * 2,
        compiler_params=_params(("parallel", "parallel", "arbitrary")),
    )(*args)


def _band_bwd(qn, kn, proj, do, o, lse, dq_buf, dk_buf, dv_buf, *, dil, radius, nkv, group, q0, k0, v0, o0,
              sink=None, name):
    L = qn.shape[0]
    n, bq, width, nsub = _band_geometry(L, dil, radius)
    tq = nsub * bq * dil
    nh = nkv * group
    n_in = 6 + (1 if sink is not None else 0)

    def body(*refs):
        q_ref, k_ref, v_ref, do_ref, o_ref, lse_ref = refs[:6]
        s_ref = refs[6] if sink is not None else None
        outs = refs[n_in + 3:]
        dq_ref, dk_ref, dv_ref = outs[:3]
        ds_ref = outs[3] if sink is not None else None
        g, step = pl.program_id(1), pl.program_id(2)

        @pl.when((g == 0) & (step == 0))
        def _():
            dk_ref[...] = jnp.zeros_like(dk_ref)
            dv_ref[...] = jnp.zeros_like(dv_ref)

        if sink is not None:
            @pl.when(step == 0)
            def _():
                ds_ref[...] = jnp.zeros_like(ds_ref)

        for sb in range(nsub):
            block(sb, q_ref, k_ref, v_ref, do_ref, o_ref, lse_ref, s_ref, dq_ref, dk_ref, dv_ref, ds_ref)

    def block(sb, q_ref, k_ref, v_ref, do_ref, o_ref, lse_ref, s_ref, dq_ref, dk_ref, dv_ref, ds_ref):
        i = pl.program_id(2) * nsub + sb
        ws = jnp.clip(i * bq - radius, 0, n - width)
        valid = _band_mask(i, bq, width, radius, ws)

        def one(r, carry):
            qrows = _band_rows(dil, r, sb * bq, bq)
            krows = _band_rows(dil, r, ws, width)
            q = q_ref[qrows, :].astype(BF16)
            k = k_ref[krows, :].astype(BF16)
            v = v_ref[krows, :].astype(BF16)
            dov = do_ref[qrows, :]
            lse_v = lse_ref[qrows, :][:, :1]
            delta = jnp.sum(dov * o_ref[qrows, :], axis=-1, keepdims=True)
            dob = dov.astype(BF16)
            s = lax.dot_general(q, k, (((1,), (1,)), ((), ())), preferred_element_type=F32) * ATT_SCALE
            p = jnp.where(valid, jnp.exp(s - lse_v), 0.0)
            dp = lax.dot_general(dob, v, (((1,), (1,)), ((), ())), preferred_element_type=F32)
            dsb = (p * (dp - delta)).astype(BF16)
            dq_ref[qrows, :] = jnp.dot(dsb, k, preferred_element_type=F32) * ATT_SCALE
            dk_ref[krows, :] += lax.dot_general(dsb, q, (((0,), (0,)), ((), ())),
                                                preferred_element_type=F32) * ATT_SCALE
            dv_ref[krows, :] += lax.dot_general(p.astype(BF16), dob, (((0,), (0,)), ((), ())),
                                                preferred_element_type=F32)
            if sink is not None:
                ps = jnp.exp(s_ref[...][:, :1] - lse_v)
                ds_ref[...] += jnp.broadcast_to(jnp.sum(-ps * delta, axis=0, keepdims=True), (1, HEAD_DIM))
            return carry

        _band_loop(dil, one)

    hspec = pl.BlockSpec((tq, HEAD_DIM), lambda hk, g, i: (i, o0 + hk * group + g))
    qspec = pl.BlockSpec((tq, HEAD_DIM), lambda hk, g, i: (i, q0 + hk * group + g))
    kspec = pl.BlockSpec((L, HEAD_DIM), lambda hk, g, i: (0, k0 + hk))
    any_spec = pl.BlockSpec(memory_space=pl.ANY)
    in_specs = [qspec, kspec, pl.BlockSpec((L, HEAD_DIM), lambda hk, g, i: (0, v0 + hk)), hspec, hspec, hspec]
    args = [qn, kn, proj, do, o, lse]
    if sink is not None:
        in_specs.append(pl.BlockSpec((None, 1, HEAD_DIM), lambda hk, g, i: (hk * group + g, 0, 0)))
        args.append(sink)
    in_specs += [any_spec] * 3
    args += [dq_buf, dk_buf, dv_buf]
    out_specs = [qspec, kspec, kspec]
    out_shape = [jax.ShapeDtypeStruct(dq_buf.shape, F32), jax.ShapeDtypeStruct(dk_buf.shape, F32),
                 jax.ShapeDtypeStruct(dv_buf.shape, F32)]
    if sink is not None:
        out_specs.append(pl.BlockSpec((None, 1, HEAD_DIM), lambda hk, g, i: (hk * group + g, 0, 0)))
        out_shape.append(jax.ShapeDtypeStruct((nh, 1, HEAD_DIM), F32))
    return pl.pallas_call(
        body, name=name, grid=(nkv, group, n // (bq * nsub)), in_specs=in_specs, out_specs=out_specs,
        out_shape=out_shape,
        input_output_aliases={n_in: 0, n_in + 1: 1, n_in + 2: 2},
        compiler_params=_params(("parallel", "arbitrary", "arbitrary")),
    )(*args)


def _combine_b(os_, lses, *, name):
    L, W = os_[0].shape
    tl = _tile(L, (256, 128))

    def body(o0, o1, o2, l0, l1, l2, out_ref, lt_ref):
        a, b, c = l0[...], l1[...], l2[...]
        m = jnp.maximum(jnp.maximum(a, b), c)
        ea, eb, ec = jnp.exp(a - m), jnp.exp(b - m), jnp.exp(c - m)
        tot = ea + eb + ec
        out_ref[...] = (ea * o0[...] + eb * o1[...] + ec * o2[...]) / tot
        lt_ref[...] = m + jnp.log(tot)

    blk = pl.BlockSpec((tl, W), lambda t: (t, 0))
    return pl.pallas_call(
        body, name=name, grid=(L // tl,), in_specs=[blk] * 6, out_specs=[blk, blk],
        out_shape=[jax.ShapeDtypeStruct((L, W), F32)] * 2, compiler_params=_params(("parallel",)),
    )(*os_, *lses)


C_QROWS = 4
C_KROWS = C_QROWS + C_WIN_ROWS
C_QUERIES, C_KEYS = C_QROWS * GRID_W, C_KROWS * GRID_W
_C_KIND_OFFSETS = (C_WIN_ROWS - 1, C_WIN_ROWS - 1 - C_WIN_ROWS // 2, C_WIN_ROWS - 1 - (C_KROWS - C_QROWS))


def _c_geometry(L):
    rows = L // GRID_W
    assert rows >= C_KROWS and rows % C_QROWS == 0
    return rows


def _c_bias_tiles(bias_t):
    cq = np.arange(GRID_W)[:, None]
    ck = np.arange(GRID_W)[None, :]
    start = np.clip(cq - C_WIN_COLS // 2, 0, GRID_W - C_WIN_COLS)
    masked = jnp.where(jnp.asarray((ck >= start) & (ck < start + C_WIN_COLS)), bias_t, NEG)
    blank = jnp.full((C_HEADS, GRID_W, GRID_W), NEG, F32)
    kinds = []
    for kind in range(3):
        off = _C_KIND_OFFSETS[kind]
        row_blocks = []
        for a in range(C_QROWS):
            lo = (0, a, C_KROWS - C_WIN_ROWS)[kind]
            row_blocks.append(jnp.concatenate(
                [masked[:, b - a + off] if lo <= b < lo + C_WIN_ROWS else blank for b in range(C_KROWS)], axis=-1))
        kinds.append(jnp.concatenate(row_blocks, axis=-2))
    return jnp.stack(kinds, axis=1)


def _c_block(g, rows):
    r0 = g * C_QROWS
    k0 = jnp.clip(r0 - C_WIN_ROWS // 2, 0, rows - C_KROWS)
    kind = jnp.where(g == 0, 0, jnp.where(g == rows // C_QROWS - 1, 2, 1))
    return k0, kind, k0 - r0 + (C_WIN_ROWS - 1)


def _c_fwd(qn, kn, proj, tiles, *, name):
    L = qn.shape[0]
    rows = _c_geometry(L)

    nsub = _tile(rows // C_QROWS, (4, 2))

    def body(q_ref, k_ref, v_ref, t_ref, o_ref, lse_ref):
        for sb in range(nsub):
            k0, kind, _ = _c_block(pl.program_id(1) * nsub + sb, rows)
            krows = pl.ds(pl.multiple_of(k0 * GRID_W, GRID_W), C_KEYS)
            qrows = pl.ds(sb * C_QUERIES, C_QUERIES)
            q = q_ref[qrows, :].astype(BF16)
            k = k_ref[krows, :].astype(BF16)
            v = v_ref[krows, :].astype(BF16)
            s = lax.dot_general(q, k, (((1,), (1,)), ((), ())), preferred_element_type=F32) * ATT_SCALE + t_ref[kind]
            m = jnp.max(s, axis=-1, keepdims=True)
            p = jnp.exp(s - m)
            denom = jnp.sum(p, axis=-1, keepdims=True)
            o_ref[qrows, :] = jnp.dot((p / denom).astype(BF16), v, preferred_element_type=F32)
            lse_ref[qrows, :] = jnp.broadcast_to(m + jnp.log(denom), (C_QUERIES, HEAD_DIM))

    ospec = pl.BlockSpec((nsub * C_QUERIES, HEAD_DIM), lambda h, g: (g, h))
    return pl.pallas_call(
        body, name=name, grid=(C_HEADS, rows // (C_QROWS * nsub)),
        in_specs=[pl.BlockSpec((nsub * C_QUERIES, HEAD_DIM), lambda h, g: (g, 20 + h)),
                  pl.BlockSpec((L, HEAD_DIM), lambda h, g: (0, 14 + h)),
                  pl.BlockSpec((L, HEAD_DIM), lambda h, g: (0, PC_VC + h)),
                  pl.BlockSpec((None, 3, C_QUERIES, C_KEYS), lambda h, g: (h, 0, 0, 0))],
        out_specs=[ospec, ospec],
        out_shape=[jax.ShapeDtypeStruct((L, C_HEADS * HEAD_DIM), F32)] * 2,
        compiler_params=_params(("parallel", "arbitrary")),
    )(qn, kn, proj, tiles)


def _c_bwd(qn, kn, proj, tiles, do, o, lse, dq_buf, dk_buf, dv_buf, *, name):
    L = qn.shape[0]
    rows = _c_geometry(L)

    nsub = _tile(rows // C_QROWS, (4, 2))

    def body(q_ref, k_ref, v_ref, t_ref, do_ref, o_ref, lse_ref, _a, _b, _c, dq_ref, dk_ref, dv_ref, dt_ref):
        @pl.when(pl.program_id(1) == 0)
        def _():
            dk_ref[...] = jnp.zeros_like(dk_ref)
            dv_ref[...] = jnp.zeros_like(dv_ref)
            dt_ref[...] = jnp.zeros_like(dt_ref)

        for sb in range(nsub):
            k0, kind, off = _c_block(pl.program_id(1) * nsub + sb, rows)
            krows = pl.ds(pl.multiple_of(k0 * GRID_W, GRID_W), C_KEYS)
            qrows = pl.ds(sb * C_QUERIES, C_QUERIES)
            q = q_ref[qrows, :].astype(BF16)
            k = k_ref[krows, :].astype(BF16)
            v = v_ref[krows, :].astype(BF16)
            dov = do_ref[qrows, :]
            dob = dov.astype(BF16)
            delta = jnp.sum(dov * o_ref[qrows, :], axis=-1, keepdims=True)
            s = lax.dot_general(q, k, (((1,), (1,)), ((), ())), preferred_element_type=F32) * ATT_SCALE + t_ref[kind]
            p = jnp.exp(s - lse_ref[qrows, :][:, :1])
            dp = lax.dot_general(dob, v, (((1,), (1,)), ((), ())), preferred_element_type=F32)
            ds = p * (dp - delta)
            for a in range(C_QROWS):
                for b in range(C_KROWS):
                    rel = jnp.clip(b - a + off, 0, C_NREL - 1)
                    dt_ref[rel] += ds[a * GRID_W:(a + 1) * GRID_W, b * GRID_W:(b + 1) * GRID_W]
            dsb = ds.astype(BF16)
            dq_ref[qrows, :] = jnp.dot(dsb, k, preferred_element_type=F32) * ATT_SCALE
            dk_ref[krows, :] += lax.dot_general(dsb, q, (((0,), (0,)), ((), ())),
                                                preferred_element_type=F32) * ATT_SCALE
            dv_ref[krows, :] += lax.dot_general(p.astype(BF16), dob, (((0,), (0,)), ((), ())),
                                                preferred_element_type=F32)

    hspec = pl.BlockSpec((nsub * C_QUERIES, HEAD_DIM), lambda h, g: (g, h))
    qspec = pl.BlockSpec((nsub * C_QUERIES, HEAD_DIM), lambda h, g: (g, 20 + h))
    kspec = pl.BlockSpec((L, HEAD_DIM), lambda h, g: (0, 14 + h))
    any_spec = pl.BlockSpec(memory_space=pl.ANY)
    return pl.pallas_call(
        body, name=name, grid=(C_HEADS, rows // (C_QROWS * nsub)),
        in_specs=[qspec, kspec, pl.BlockSpec((L, HEAD_DIM), lambda h, g: (0, PC_VC + h)),
                  pl.BlockSpec((None, 3, C_QUERIES, C_KEYS), lambda h, g: (h, 0, 0, 0)),
                  hspec, hspec, hspec, any_spec, any_spec, any_spec],
        out_specs=[qspec, kspec, kspec,
                   pl.BlockSpec((None, C_NREL, GRID_W, GRID_W), lambda h, r: (h, 0, 0, 0))],
        out_shape=[jax.ShapeDtypeStruct(dq_buf.shape, F32), jax.ShapeDtypeStruct(dk_buf.shape, F32),
                   jax.ShapeDtypeStruct(dv_buf.shape, F32),
                   jax.ShapeDtypeStruct((C_HEADS, C_NREL, GRID_W, GRID_W), F32)],
        input_output_aliases={7: 0, 8: 1, 9: 2},
        compiler_params=_params(("parallel", "arbitrary")),
    )(qn, kn, proj, tiles, do, o, lse, dq_buf, dk_buf, dv_buf)


def _c_expand_matrix():
    cq = np.arange(GRID_W)[:, None]
    ck = np.arange(GRID_W)[None, :]
    d = (ck - cq + (C_WIN_COLS - 1)).reshape(-1)
    e = np.zeros((GRID_W * GRID_W, HEAD_DIM), np.float32)
    okd = (d >= 0) & (d < C_NCOL)
    e[np.arange(GRID_W * GRID_W)[okd], d[okd]] = 1.0
    return e


def _peer(p):
    return (p // 4, (p // 2) % 2, p % 2)


def _my_index():
    return 4 * lax.axis_index("x") + 2 * lax.axis_index("y") + lax.axis_index("c")


HBM_SPEC = pl.BlockSpec(memory_space=pltpu.HBM)
SEM_SPEC = pl.BlockSpec(memory_space=pltpu.SEMAPHORE)
ANY_SPEC = pl.BlockSpec(memory_space=pl.ANY)
DATAFLOW = pltpu.SideEffectType.DATAFLOW_SIDE_EFFECTING


_EXCHANGE_TRANSFERS = {"scatter": N_DEV - 1, "gather1": 4, "gather2": 3}


def _exchange_views(mode, kinds, arrays):
    nw = len(kinds)
    gather = mode != "scatter"
    if gather:
        rows = [a.shape[0] // N_DEV for a in arrays[:nw]]
    else:
        rows = [a.shape[1] // N_DEV for a in arrays[:nw]]

    def gather_slot(ref, w, who):
        return ref.at[who] if kinds[w] == "col" else ref.at[pl.ds(who * rows[w], rows[w]), :]

    x, y, c = lax.axis_index("x"), lax.axis_index("y"), lax.axis_index("c")
    me = 4 * x + 2 * y + c
    chips = [(1 - x, y), (x, 1 - y), (1 - x, 1 - y)]

    def index(px, py, pc):
        return 4 * px + 2 * py + pc

    if mode == "scatter":
        plan = [(_peer((me + off) % N_DEV), (me + off) % N_DEV, (me + N_DEV - off) % N_DEV)
                for off in range(1, N_DEV)]
    elif mode == "gather1":
        plan = [((x, y, 1 - c), me, index(x, y, 1 - c))] + [((px, py, c), me, index(px, py, c)) for px, py in chips]
    else:
        plan = [((x, y, 1 - c), index(px, py, c), index(px, py, 1 - c)) for px, py in chips]

    def src(ref, w, j):
        sent = plan[j][1]
        if gather:
            return gather_slot(ref, w, sent)
        return ref.at[sent] if kinds[w] == "col" else ref.at[0, pl.ds(sent * rows[w], rows[w]), :]

    def dst(ref, w, j):
        return gather_slot(ref, w, plan[j][1]) if gather else ref.at[me]

    def arrival(ref, w, j):
        return gather_slot(ref, w, plan[j][2]) if gather else ref.at[plan[j][2]]

    return [p[0] for p in plan], src, dst, arrival


def _place_cast(w, layer, kind, *, name):
    _, R, C = w.shape
    tr = _tile(R, (256, 128, 64, 32, 16))

    def body(w_ref, o_ref):
        o_ref[...] = w_ref[...].astype(BF16)

    if kind == "col":
        out_shape = jax.ShapeDtypeStruct((N_DEV, R, C), BF16)
        out_spec = pl.BlockSpec((None, tr, C), lambda t: (_my_index(), t, 0))
    else:
        out_shape = jax.ShapeDtypeStruct((N_DEV * R, C), BF16)
        out_spec = pl.BlockSpec((tr, C), lambda t: (_my_index() * (R // tr) + t, 0))
    return pl.pallas_call(
        body, name=name, grid=(R // tr,), in_specs=[pl.BlockSpec((None, tr, C), lambda t: (layer, t, 0))],
        out_specs=out_spec, out_shape=out_shape, compiler_params=_params(("parallel",)),
    )(w)


def _exchange_start(mode, srcs, lands, kinds, after, *, name):
    nw = len(lands)
    ns = len(srcs)
    arrays = list(srcs) + list(lands)
    na = len(arrays)
    nx = _EXCHANGE_TRANSFERS[mode]

    def body(*refs):
        l_refs = refs[ns:ns + nw]
        s_refs = refs[:ns] if ns else l_refs
        send_sems, recv_sems = refs[ns + nw + 1], refs[ns + nw + 2]
        token = refs[-1]
        peers, src, dst, _ = _exchange_views(mode, kinds, arrays)
        for j in range(nx):
            for w in range(nw):
                pltpu.make_async_remote_copy(src(s_refs[w], w, j), dst(l_refs[w], w, j),
                                             send_sems.at[w * nx + j], recv_sems.at[w * nx + j],
                                             device_id=peers[j], device_id_type=MESH).start()
        token[...] = jnp.zeros_like(token)

    outs = pl.pallas_call(
        body, name=name,
        out_shape=(pltpu.SemaphoreType.DMA((nw * nx,)), pltpu.SemaphoreType.DMA((nw * nx,)),
                   *[pltpu.HBM(a.shape, a.dtype) for a in arrays], jax.ShapeDtypeStruct((8, 128), F32)),
        in_specs=[HBM_SPEC] * na + [ANY_SPEC],
        out_specs=(SEM_SPEC, SEM_SPEC, *([HBM_SPEC] * na), pl.BlockSpec(memory_space=pltpu.VMEM)),
        input_output_aliases={k: 2 + k for k in range(na)},
        compiler_params=pltpu.CompilerParams(has_side_effects=DATAFLOW),
    )(*[pltpu.with_memory_space_constraint(a, pltpu.HBM) for a in arrays], after)
    return outs[0], outs[1], outs[2:2 + ns], outs[2 + ns:2 + na], outs[-1]


def _exchange_wait(mode, started, kinds, after, *, name):
    send_sems, recv_sems, srcs, lands, _ = started
    nw = len(lands)
    ns = len(srcs)
    arrays = list(srcs) + list(lands)
    na = len(arrays)
    nx = _EXCHANGE_TRANSFERS[mode]

    def body(*refs):
        l_refs = refs[ns:na]
        s_refs = refs[:ns] if ns else l_refs
        send_ref, recv_ref = refs[na], refs[na + 1]
        peers, src, _, arrival = _exchange_views(mode, kinds, arrays)
        for j in range(nx):
            for w in range(nw):
                cp = pltpu.make_async_remote_copy(src(s_refs[w], w, j), arrival(l_refs[w], w, j),
                                                  send_ref.at[w * nx + j], recv_ref.at[w * nx + j],
                                                  device_id=peers[j], device_id_type=MESH)
                cp.wait_send()
                cp.wait_recv()

    outs = pl.pallas_call(
        body, name=name, out_shape=[pltpu.HBM(a.shape, a.dtype) for a in arrays],
        in_specs=[HBM_SPEC] * na + [SEM_SPEC, SEM_SPEC, ANY_SPEC], out_specs=[HBM_SPEC] * na,
        input_output_aliases={k: k for k in range(na)},
        compiler_params=pltpu.CompilerParams(has_side_effects=DATAFLOW),
    )(*arrays, send_sems, recv_sems, after)
    return outs[:ns], outs[ns:]


def _all_reduce_small(x):
    R = x.shape[0]

    def body(x_ref, o_ref, gath, send_sems, recv_sems):
        me = _my_index()
        gath[me] = x_ref[...]
        sends = []
        for off in range(1, N_DEV):
            to = (me + off) % N_DEV
            cp = pltpu.make_async_remote_copy(x_ref, gath.at[me], send_sems.at[off], recv_sems.at[off],
                                              device_id=_peer(to), device_id_type=MESH)
            cp.start()
            sends.append(cp)
        for off in range(1, N_DEV):
            frm = (me + N_DEV - off) % N_DEV
            pltpu.make_async_remote_copy(x_ref, gath.at[frm], send_sems.at[off], recv_sems.at[off],
                                         device_id=_peer(frm), device_id_type=MESH).wait_recv()
        for cp in sends:
            cp.wait_send()
        acc = gath[0]
        for s in range(1, N_DEV):
            acc = acc + gath[s]
        o_ref[...] = acc

    vm = pl.BlockSpec(memory_space=pltpu.VMEM)
    return pl.pallas_call(
        body, name="all_reduce_small", in_specs=[vm], out_specs=vm, out_shape=jax.ShapeDtypeStruct((R, 128), F32),
        scratch_shapes=[pltpu.VMEM((N_DEV, R, 128), F32), pltpu.SemaphoreType.DMA((N_DEV,)),
                        pltpu.SemaphoreType.DMA((N_DEV,))],
        compiler_params=pltpu.CompilerParams(has_side_effects=True),
    )(x)


def _adamw_math(w, g, m, v):
    m = ADAM_B1 * m + (1.0 - ADAM_B1) * g
    v = ADAM_B2 * v + (1.0 - ADAM_B2) * (g * g)
    m_hat = m / (1.0 - ADAM_B1 ** ADAM_STEP)
    v_hat = v / (1.0 - ADAM_B2 ** ADAM_STEP)
    delta = -ADAM_LR * (m_hat / (jnp.sqrt(v_hat) + ADAM_EPS) + ADAM_WD * w)
    return delta, m, v


def _adamw_layer(recv, own, kind, w, m, v, outs, layer, dep, *, name):
    nl, R, C = w.shape
    tr = _tile(R, (128, 64, 32, 16))

    def body(r_ref, o_ref, w_ref, m_ref, v_ref, _0, _1, _2, _3, _dep, g_out, d_out, m_out, v_out, token):
        token[...] = jnp.zeros_like(token)
        me = _my_index()
        mine = o_ref[...].astype(F32)
        g = jnp.where(me == 0, mine, r_ref[0].astype(F32))
        for s in range(1, N_DEV):
            g = g + jnp.where(me == s, mine, r_ref[s].astype(F32))
        delta, mn, vn = _adamw_math(w_ref[...], g, m_ref[...], v_ref[...])
        g_out[...] = g
        d_out[...] = delta
        m_out[...] = mn
        v_out[...] = vn

    if kind == "col":
        own_spec = pl.BlockSpec((None, tr, C), lambda t: (_my_index(), t, 0))
    else:
        own_spec = pl.BlockSpec((None, tr, C), lambda t: (0, _my_index() * (R // tr) + t, 0))
    wspec = pl.BlockSpec((None, tr, C), lambda t: (layer, t, 0))
    res = pl.pallas_call(
        body, name=name, grid=(R // tr,),
        in_specs=[pl.BlockSpec((N_DEV, tr, C), lambda t: (0, t, 0)), own_spec] + [wspec] * 3 + [ANY_SPEC] * 5,
        out_specs=[wspec] * 4 + [pl.BlockSpec((8, 128), lambda t: (0, 0))],
        out_shape=[jax.ShapeDtypeStruct((nl, R, C), F32)] * 4 + [jax.ShapeDtypeStruct((8, 128), F32)],
        input_output_aliases={5: 0, 6: 1, 7: 2, 8: 3},
        compiler_params=_params(("arbitrary",)),
    )(recv, own, w, m, v, *outs, dep)
    return res[:4], res[4]


def _adamw_small(g, w, m, v):
    def body(g_ref, w_ref, m_ref, v_ref, d_out, m_out, v_out):
        delta, mn, vn = _adamw_math(w_ref[...], g_ref[...], m_ref[...], v_ref[...])
        d_out[...] = delta
        m_out[...] = mn
        v_out[...] = vn

    return pl.pallas_call(body, name="adamw_small", out_shape=[jax.ShapeDtypeStruct(g.shape, F32)] * 3)(g, w, m, v)


def _pack(arrays, rows):
    flat = jnp.concatenate([a.reshape(-1) for a in arrays])
    return jnp.pad(flat, (0, rows * 128 - flat.shape[0])).reshape(rows, 128)


def _unpack(packed, shapes):
    flat = packed.reshape(-1)
    out, pos = [], 0
    for s in shapes:
        size = int(np.prod(s))
        out.append(flat[pos:pos + size].reshape(s))
        pos += size
    return out


def kernel(x, norm1_g, w_in, qk_norm_g, sink_a, rpb_c, w_br_a, w_br_b, w_br_c, w_o, norm2_g, w_gate_up, w_down, loss_target, m_norm1_g, m_w_in, m_qk_norm_g, m_sink_a, m_rpb_c, m_w_br_a, m_w_br_b, m_w_br_c, m_w_o, m_norm2_g, m_w_gate_up, m_w_down, v_norm1_g, v_w_in, v_qk_norm_g, v_sink_a, v_rpb_c, v_w_br_a, v_w_br_b, v_w_br_c, v_w_o, v_norm2_g, v_w_gate_up, v_w_down):
    nl = w_in.shape[0]
    L, D = x.shape[1], x.shape[2]
    x0 = x.reshape(L, D)
    tgt = loss_target.reshape(L, D)

    big = [w_in, w_br_a, w_br_b, w_br_c, w_o, w_gate_up, w_down]
    kinds = ["col", "col", "col", "col", "row", "col", "row"]

    big_names = ["w_in", "w_br_a", "w_br_b", "w_br_c", "w_o", "w_gate_up", "w_down"]
    ALL = list(range(len(big)))
    REST = ALL[1:]

    def gather_place(i):
        return [_place_cast(w, i, k, name="gather_place_" + n) for w, k, n in zip(big, kinds, big_names)]

    def gather_start(mode, lands, sub, after, tag):
        return _exchange_start(mode, [], lands, [kinds[j] for j in sub], after, name=mode + "_start" + tag)

    def gather_wait(mode, started, sub, after, tag):
        return _exchange_wait(mode, started, [kinds[j] for j in sub], after, name=mode + "_wait" + tag)[1]

    def matmul_views(lands, sub):
        return [g.reshape((N_DEV, 1) + g.shape[1:]) if kinds[j] == "col" else g.reshape((1, 1) + g.shape)
                for g, j in zip(lands, sub)]

    half = HEAD_DIM // 2
    inv_freq = ROPE_THETA ** (-jnp.arange(half, dtype=F32) * 2.0 / HEAD_DIM)
    ang = jnp.arange(L, dtype=F32)[:, None] * inv_freq[None, :]
    cos = jnp.concatenate([jnp.cos(ang), jnp.cos(ang)], axis=-1)
    sin = jnp.concatenate([-jnp.sin(ang), jnp.sin(ang)], axis=-1)
    expand = jnp.asarray(_c_expand_matrix(), BF16)
    expand_t = jnp.asarray(_c_expand_matrix().T, BF16)

    def gain_tables(i):
        g = qk_norm_g[i]
        gq = jnp.concatenate([jnp.tile(g[0][None], (8, 1)), jnp.tile(g[2][None], (12, 1)), jnp.tile(g[4][None], (8, 1))])
        gk = jnp.concatenate([jnp.tile(g[1][None], (2, 1)), jnp.tile(g[3][None], (12, 1)), jnp.tile(g[5][None], (8, 1))])
        return (gq.reshape(NQ_CHUNKS // Q_CG, 1, Q_CG * HEAD_DIM), gk.reshape(NK_CHUNKS // K_CG, 1, K_CG * HEAD_DIM))

    def bias_table(i):
        rp = jnp.pad(rpb_c[i].reshape(C_HEADS * C_NREL, C_NCOL), ((0, 0), (0, HEAD_DIM - C_NCOL)))
        t = _exact_mm(rp, expand_t, name="c_bias_expand")
        return _c_bias_tiles(t.reshape(C_HEADS, C_NREL, GRID_W, GRID_W))

    def sink_table(i):
        return jnp.broadcast_to(sink_a[i][:, None, None], (A_Q_HEADS, 1, HEAD_DIM))

    saved = []
    gws = [None] * nl
    xi = x0
    lands0 = gather_place(0)
    lvl1 = gather_start("gather1", lands0[:1], [0], x0, "_first")
    lvl2 = gather_start("gather2", gather_wait("gather1", lvl1, [0], x0, "_first"), [0], x0, "_first")
    gws[0] = matmul_views(gather_wait("gather2", lvl2, [0], x0, "_first"), [0])
    rest1 = gather_start("gather1", lands0[1:], REST, gws[0][0], "_rest")
    dep = rest1[4]
    for i in range(nl):
        qk_dep = None
        if i >= 1 and i + 1 < nl:
            nxt1 = gather_start("gather1", gather_place(i + 1), ALL, dep, "")
            dep = nxt1[4]
        gw_in = gws[i][0]
        gq, gk = gain_tables(i)
        bias_t = bias_table(i)
        sink = sink_table(i)
        proj, h1 = _norm_mm(xi, norm1_g[i][None], gw_in, dep, out_dtype=F32, name="proj_fwd")
        if i == 0:
            rest2 = gather_start("gather2", gather_wait("gather1", rest1, REST, proj, "_rest"), REST, proj, "_rest")
            qk_dep = rest2[4]
            if nl > 1:
                nxt1 = gather_start("gather1", gather_place(1), ALL, rest2[4], "")
                qk_dep = nxt1[4]
        qn = _qk_fwd(proj, gq, cos, sin, Q_PIECES, NQ_CHUNKS, Q_ROPE_UPTO, Q_CG, name="qnorm_fwd", dep=qk_dep)
        kn = _qk_fwd(proj, gk, cos, sin, K_PIECES, NK_CHUNKS, K_ROPE_UPTO, K_CG, name="knorm_fwd")
        oa, lse_a = _band_fwd(qn, kn, proj, dil=1, radius=A_RADIUS, nkv=A_KV_HEADS, group=A_GROUP,
                              q0=0, k0=0, v0=PC_VA, sink=sink, name="attn_a_fwd")
        obs, lbs = [], []
        for g, (window, dil) in enumerate(B_PATTERNS):
            o_g, l_g = _band_fwd(qn, kn, proj, dil=dil, radius=window // (2 * dil), nkv=B_HG, group=1,
                                 q0=8 + g * B_HG, k0=2 + g * B_HG, v0=PC_VB + g * B_HG, name=f"attn_b{g}_fwd")
            obs.append(o_g)
            lbs.append(l_g)
        ob, lse_b = _combine_b(obs, lbs, name="attn_b_combine")
        oc, lse_c = _c_fwd(qn, kn, proj, bias_t, name="attn_c_fwd")
        if i == 0:
            gws[0] = gws[0] + matmul_views(gather_wait("gather2", rest2, REST, oc, "_rest"), REST)
        gws[i][1:4] = [_blocks_to_wide(g, name="br_wide_" + n) for g, n in zip(gws[i][1:4], "abc")]
        _, gw_a, gw_b, gw_c, gw_o, gw_gu, gw_d = gws[i]
        ta = _mm_nn(oa, gw_a, 0, out_dtype=BF16, name="br_a_fwd")
        tb = _mm_nn(ob, gw_b, 0, out_dtype=BF16, name="br_b_fwd")
        tc = _mm_nn(oc, gw_c, 0, out_dtype=BF16, name="br_c_fwd")
        merged = _gate_fwd(proj, ta, tb, tc, name="gate_fwd")
        x1 = _mm_nn(merged, gw_o, 0, out_dtype=F32, name="wo_fwd", res=xi)
        dep = x1
        if i + 1 < nl:
            nxt2 = gather_start("gather2", gather_wait("gather1", nxt1, ALL, x1, ""), ALL, x1, "")
            dep = nxt2[4]
        gu, h2 = _norm_mm(x1, norm2_g[i][None], gw_gu, dep, out_dtype=BF16, name="gate_up_fwd")
        act = _swiglu_fwd(gu, name="swiglu_fwd")
        x2 = _mm_nn(act, gw_d, 0, out_dtype=F32, name="down_fwd", res=x1)
        saved.append(dict(x=xi, h1=h1, proj=proj, qn=qn, kn=kn, oa=oa, lse_a=lse_a, ob=ob, lse_b=lse_b, oc=oc,
                          lse_c=lse_c, ta=ta, tb=tb, tc=tc, merged=merged, x1=x1, h2=h2, gu=gu, act=act,
                          gq=gq, gk=gk, bias_t=bias_t, sink=sink))
        xi = x2
        dep = x2
        if i + 1 < nl:
            gws[i + 1] = matmul_views(gather_wait("gather2", nxt2, ALL, x2, ""), ALL)

    dx, dxb, loss_row = _loss(xi, tgt, name="loss")

    def scatter_start(grads, sub, after, tag):
        lands = []
        for g, j in zip(grads, sub):
            shape = g.shape if kinds[j] == "col" else (N_DEV, g.shape[1] // N_DEV, g.shape[2])
            lands.append(lax.empty(shape, BF16))
        return _exchange_start("scatter", grads, lands, [kinds[j] for j in sub], after, name="scatter_start" + tag)

    def scatter_wait(pair, after):
        own_a, recv_a = _exchange_wait("scatter", pair[0], [kinds[0]], after, name="scatter_wait_in")
        own_b, recv_b = _exchange_wait("scatter", pair[1], [kinds[j] for j in REST], after, name="scatter_wait_rest")
        return list(recv_a) + list(recv_b), list(own_a) + list(own_b)

    small_grads = [None] * nl
    recv = [None] * nl
    own = [None] * nl
    pending = None
    for i in reversed(range(nl)):
        s = saved[i]
        gw_in, gw_a, gw_b, gw_c, gw_o, gw_gu, gw_d = gws[i]
        dact = _mm_nt(dxb, gw_d, 0, out_dtype=BF16, name="down_bwd_x", dep=None if pending is None else pending[0][4])
        g_down = _mm_tn(s["act"], dxb, 1, name="down_bwd_w")
        dgu = _swiglu_bwd(s["gu"], dact, name="swiglu_bwd")
        g_gu = _mm_tn(s["h2"], dgu, N_DEV, name="gate_up_bwd_w")
        dh2 = _mm_nt(dgu, gw_gu, 0, out_dtype=F32, name="gate_up_bwd_x")
        dx1, dx1b, dg2 = _rms_bwd(s["x1"], norm2_g[i][None], dh2, dx, name="rms2_bwd")
        dmerged = _mm_nt(dx1b, gw_o, 0, out_dtype=F32, name="wo_bwd_x")
        g_o = _mm_tn(s["merged"], dx1b, 1, name="wo_bwd_w")
        dta, dtb, dtc, dproj = _gate_bwd(s["proj"], s["ta"], s["tb"], s["tc"], dmerged, name="gate_bwd")
        g_a = _wide_to_blocks(_mm_tn(s["oa"], dta, 1, name="br_a_bwd_w"), N_DEV, name="br_blocks_a")
        g_b = _wide_to_blocks(_mm_tn(s["ob"], dtb, 1, name="br_b_bwd_w"), N_DEV, name="br_blocks_b")
        g_c = _wide_to_blocks(_mm_tn(s["oc"], dtc, 1, name="br_c_bwd_w"), N_DEV, name="br_blocks_c")
        rest = scatter_start([g_a, g_b, g_c, g_o, g_gu, g_down], REST, g_c, "_rest")
        doa = _mm_nt(dta, gw_a, 0, out_dtype=F32, name="br_a_bwd_x", dep=rest[4])
        dob = _mm_nt(dtb, gw_b, 0, out_dtype=F32, name="br_b_bwd_x")
        doc = _mm_nt(dtc, gw_c, 0, out_dtype=F32, name="br_c_bwd_x")
        dq_buf = lax.empty((L, NQ_CHUNKS * HEAD_DIM), F32)
        dk_buf = lax.empty((L, NK_CHUNKS * HEAD_DIM), F32)
        dv_buf = lax.empty((L, NK_CHUNKS * HEAD_DIM), F32)
        dq_buf, dk_buf, dv_buf, dsink = _band_bwd(
            s["qn"], s["kn"], s["proj"], doa, s["oa"], s["lse_a"], dq_buf, dk_buf, dv_buf, dil=1, radius=A_RADIUS,
            nkv=A_KV_HEADS, group=A_GROUP, q0=0, k0=0, v0=PC_VA, o0=0, sink=s["sink"], name="attn_a_bwd")
        for g, (window, dil) in enumerate(B_PATTERNS):
            dq_buf, dk_buf, dv_buf = _band_bwd(
                s["qn"], s["kn"], s["proj"], dob, s["ob"], s["lse_b"], dq_buf, dk_buf, dv_buf, dil=dil,
                radius=window // (2 * dil), nkv=B_HG, group=1, q0=8 + g * B_HG, k0=2 + g * B_HG,
                v0=PC_VB + g * B_HG, o0=0, name=f"attn_b{g}_bwd")
        dq_buf, dk_buf, dv_buf, dbias_t = _c_bwd(s["qn"], s["kn"], s["proj"], s["bias_t"], doc, s["oc"], s["lse_c"],
                                                 dq_buf, dk_buf, dv_buf, name="attn_c_bwd")
        dproj, dgq = _qk_bwd(dq_buf, s["proj"], s["gq"], cos, sin, dproj, Q_PIECES, NQ_CHUNKS, Q_ROPE_UPTO, Q_CG,
                             name="qnorm_bwd")
        dproj, dgk = _qk_bwd(dk_buf, s["proj"], s["gk"], cos, sin, dproj, K_PIECES, NK_CHUNKS, K_ROPE_UPTO, K_CG,
                             name="knorm_bwd")
        dproj = _v_bwd(dv_buf, dproj, name="v_bwd")
        g_in = _mm_tn(s["h1"], dproj, N_DEV, name="proj_bwd_w")
        dh1 = _mm_nt(dproj, gw_in, 0, out_dtype=F32, name="proj_bwd_x")
        dx, dxb, dg1 = _rms_bwd(s["x"], norm1_g[i][None], dh1, dx1, name="rms1_bwd")
        if pending is not None:
            recv[i + 1], own[i + 1] = scatter_wait(pending, dx)
        pending = (scatter_start([g_in], [0], dx, "_in"), rest)

        drpb = _exact_mm(dbias_t.reshape(C_HEADS * C_NREL, GRID_W * GRID_W), expand, name="c_bias_reduce")
        dgq, dgk = dgq.reshape(NQ_CHUNKS, HEAD_DIM), dgk.reshape(NK_CHUNKS, HEAD_DIM)
        dqk_g = jnp.stack([dgq[0:8].sum(0), dgk[0:2].sum(0), dgq[8:20].sum(0), dgk[2:14].sum(0),
                           dgq[20:28].sum(0), dgk[14:22].sum(0)])
        small_grads[i] = (dg1.reshape(D), dqk_g, dsink[:, 0, 0],
                          drpb[:, :C_NCOL].reshape(C_HEADS, C_NREL, C_NCOL), dg2.reshape(D))

    small_names = [norm1_g, qk_norm_g, sink_a, rpb_c, norm2_g]
    small_m = [m_norm1_g, m_qk_norm_g, m_sink_a, m_rpb_c, m_norm2_g]
    small_v = [v_norm1_g, v_qk_norm_g, v_sink_a, v_rpb_c, v_norm2_g]
    shapes = [a.shape for a in small_names]
    total = sum(int(np.prod(sh)) for sh in shapes) + 128
    rows = -(-total // 1024) * 8
    stacked = [jnp.stack([small_grads[i][j] for i in range(nl)]) for j in range(5)]
    packed = _pack([loss_row.reshape(-1)] + stacked, rows)
    summed = _all_reduce_small(packed)
    loss = summed[0, 0]
    zero_row = jnp.zeros((128,), F32)
    d_s, m_s, v_s = _adamw_small(summed, _pack([zero_row] + small_names, rows), _pack([zero_row] + small_m, rows),
                                 _pack([zero_row] + small_v, rows))
    shapes1 = [(128,)] + shapes
    g_small = _unpack(summed, shapes1)[1:]
    d_small = _unpack(d_s, shapes1)[1:]
    m_small = _unpack(m_s, shapes1)[1:]
    v_small = _unpack(v_s, shapes1)[1:]

    big_m = [m_w_in, m_w_br_a, m_w_br_b, m_w_br_c, m_w_o, m_w_gate_up, m_w_down]
    big_v = [v_w_in, v_w_br_a, v_w_br_b, v_w_br_c, v_w_o, v_w_gate_up, v_w_down]
    big_out = [[lax.empty(w.shape, F32) for _ in range(4)] for w in big]
    token = pending[0][4]
    for i in list(range(nl - 1, 0, -1)) + [0]:
        if i == 0:
            recv[0], own[0] = scatter_wait(pending, token)
        for j in range(len(big)):
            big_out[j], token = _adamw_layer(recv[i][j], own[i][j], kinds[j], big[j], big_m[j], big_v[j],
                                             big_out[j], i, token, name="adamw_" + big_names[j])

    order = ["norm1_g", "w_in", "qk_norm_g", "sink_a", "rpb_c", "w_br_a", "w_br_b", "w_br_c", "w_o", "norm2_g",
             "w_gate_up", "w_down"]
    small_idx = {"norm1_g": 0, "qk_norm_g": 1, "sink_a": 2, "rpb_c": 3, "norm2_g": 4}
    big_idx = {n: j for j, n in enumerate(big_names)}

    def pick(kind):
        out = []
        for n in order:
            if n in small_idx:
                out.append([g_small, d_small, m_small, v_small][kind][small_idx[n]])
            else:
                out.append(big_out[big_idx[n]][kind])
        return out

    return (loss, dx.reshape(1, L, D), *pick(0), *pick(1), *pick(2), *pick(3))
```
